```python
import math
import jax, jax.numpy as jnp
from jax import lax
import numpy as np

D_MODEL = 2048
BATCH = 8
SEQ = 4096
DEPTH = 1

HEAD_DIM = 128
N_Q_HEADS = 16
N_KV_HEADS = 4
GROUP = N_Q_HEADS // N_KV_HEADS
WINDOW = 128
BLK = 128
ROT_DIM = HEAD_DIM // 4
ROPE_THETA = 500000.0
D_RNN = ((4 * D_MODEL // 3 + 255) // 256) * 256
N_RNN_BLOCKS = 16
RNN_BW = D_RNN // N_RNN_BLOCKS
CONV_W = 4
LRU_C = 8.0
D_FF = ((8 * D_MODEL // 3 + 255) // 256) * 256
EPS = 1e-6
NEG = -1e30

Q_W = N_Q_HEADS * HEAD_DIM
KV_W = N_KV_HEADS * HEAD_DIM
SPLITS = np.cumsum([Q_W, KV_W, KV_W, D_RNN, D_RNN, D_MODEL]).tolist()
IN_W = SPLITS[-1] + D_MODEL

kernel_name = "hybrid_swa_sink_rglru_swiglu"


def rms_norm(x, g):
    xf = x.astype(jnp.float32)
    y = xf * lax.rsqrt(jnp.mean(xf * xf, axis=-1, keepdims=True) + EPS)
    return (y * g.astype(jnp.float32)).astype(x.dtype)


def partial_rope(x, cos, sin):
    xf = x.astype(jnp.float32)
    half = ROT_DIM // 2
    x1, x2, rest = xf[..., :half], xf[..., half:ROT_DIM], xf[..., ROT_DIM:]
    out = jnp.concatenate([x1 * cos - x2 * sin, x2 * cos + x1 * sin, rest], axis=-1)
    return out.astype(x.dtype)


def window_attention(q, k, v, sinks):
    B, S = q.shape[0], q.shape[1]
    nb = S // BLK
    qb = q.reshape(B, nb, BLK, N_KV_HEADS, GROUP, HEAD_DIM)

    def with_prev(t):
        tb = t.reshape(B, nb, BLK, N_KV_HEADS, HEAD_DIM)
        prev = jnp.pad(tb, ((0, 0), (1, 0), (0, 0), (0, 0), (0, 0)))[:, :-1]
        return jnp.concatenate([prev, tb], axis=2)

    kw, vw = with_prev(k), with_prev(v)
    scale = 1.0 / math.sqrt(HEAD_DIM)
    s = jnp.einsum('bnqkgd,bnjkd->bnkgqj', qb, kw).astype(jnp.float32) * scale
    qi = jnp.arange(BLK)[:, None]
    kj = jnp.arange(2 * BLK)[None, :]
    rel = qi + BLK - kj
    band = (rel >= 0) & (rel < WINDOW)
    real = (jnp.arange(nb)[:, None, None] > 0) | (kj >= BLK)[None]
    mask = (band[None] & real)[None, :, None, None]
    s = jnp.where(mask, s, NEG)
    sink = jnp.broadcast_to(
        sinks.astype(jnp.float32).reshape(N_KV_HEADS, GROUP)[None, None, :, :, None, None],
        s.shape[:-1] + (1,))
    p = jax.nn.softmax(jnp.concatenate([s, sink], axis=-1), axis=-1)[..., :-1]
    o = jnp.einsum('bnkgqj,bnjkd->bnqkgd', p.astype(v.dtype), vw)
    return o.reshape(B, S, Q_W)


def causal_depthwise_conv(u, w, b):
    S = u.shape[1]
    up = jnp.pad(u, ((0, 0), (CONV_W - 1, 0), (0, 0)))
    y = b
    for tap in range(CONV_W):
        y = y + w[tap] * up[:, tap:tap + S]
    return y


def rg_lru(u, w_r, b_r, w_i, b_i, lam):
    B, S = u.shape[0], u.shape[1]
    uf = u.astype(jnp.float32)
    ub = uf.reshape(B, S, N_RNN_BLOCKS, RNN_BW)
    r = jax.nn.sigmoid(jnp.einsum('bsnc,ncd->bsnd', ub, w_r.astype(jnp.float32)).reshape(B, S, D_RNN)
                       + b_r.astype(jnp.float32))
    i = jax.nn.sigmoid(jnp.einsum('bsnc,ncd->bsnd', ub, w_i.astype(jnp.float32)).reshape(B, S, D_RNN)
                       + b_i.astype(jnp.float32))
    log_a = -LRU_C * r * jax.nn.softplus(-lam.astype(jnp.float32))
    a = jnp.exp(log_a)
    bterm = jnp.sqrt(jnp.maximum(-jnp.expm1(2.0 * log_a), 0.0)) * (i * uf)

    def combine(left, right):
        a1, b1 = left
        a2, b2 = right
        return a1 * a2, a2 * b1 + b2

    _, h = lax.associative_scan(combine, (a, bterm), axis=1)
    return h.astype(u.dtype)


def _fwd_setup_inputs(seed: int = 0) -> dict:
    key = jax.random.key(seed)
    ks = jax.random.split(key, 24)
    f32 = jnp.float32
    nrm = lambda k, shape, fan: jax.random.normal(k, shape, f32) * (fan ** -0.5)
    L = DEPTH
    x = jax.random.normal(ks[0], (BATCH, SEQ, D_MODEL), f32)
    offs = jax.random.randint(ks[1], (BATCH, 1), 0, 1024, dtype=jnp.int32)
    positions = offs + jnp.arange(SEQ, dtype=jnp.int32)[None, :]
    a0 = jax.random.uniform(ks[2], (L, D_RNN), f32, 0.9, 0.999)
    return {
        "x": x,
        "positions": positions,
        "norm1_g": 1.0 + 0.02 * jax.random.normal(ks[3], (L, D_MODEL), f32),
        "w_in": nrm(ks[4], (L, D_MODEL, IN_W), D_MODEL),
        "b_gates": 0.02 * jax.random.normal(ks[5], (L, 2 * D_MODEL), f32),
        "q_norm_g": 1.0 + 0.02 * jax.random.normal(ks[6], (L, HEAD_DIM), f32),
        "k_norm_g": 1.0 + 0.02 * jax.random.normal(ks[7], (L, HEAD_DIM), f32),
        "sinks": 0.5 * jax.random.normal(ks[8], (L, N_Q_HEADS), f32),
        "conv_w": nrm(ks[9], (L, CONV_W, D_RNN), CONV_W),
        "conv_b": 0.02 * jax.random.normal(ks[10], (L, D_RNN), f32),
        "w_rgate": nrm(ks[11], (L, N_RNN_BLOCKS, RNN_BW, RNN_BW), RNN_BW),
        "b_rgate": 0.02 * jax.random.normal(ks[12], (L, D_RNN), f32),
        "w_igate": nrm(ks[13], (L, N_RNN_BLOCKS, RNN_BW, RNN_BW), RNN_BW),
        "b_igate": 0.02 * jax.random.normal(ks[14], (L, D_RNN), f32),
        "lru_lambda": jnp.log(a0) - jnp.log1p(-a0),
        "w_attn_proj": nrm(ks[15], (L, Q_W, D_MODEL), Q_W),
        "w_lru_proj": nrm(ks[16], (L, D_RNN, D_MODEL), D_RNN),
        "w_out": nrm(ks[17], (L, D_MODEL, D_MODEL), D_MODEL),
        "norm2_g": 1.0 + 0.02 * jax.random.normal(ks[18], (L, D_MODEL), f32),
        "w_ffn_gate": nrm(ks[19], (L, D_MODEL, D_FF), D_MODEL),
        "w_ffn_up": nrm(ks[20], (L, D_MODEL, D_FF), D_MODEL),
        "w_ffn_down": nrm(ks[21], (L, D_FF, D_MODEL), D_FF),
    }


def _fwd_reference(x, positions, norm1_g, w_in, b_gates, q_norm_g, k_norm_g, sinks,
              conv_w, conv_b, w_rgate, b_rgate, w_igate, b_igate, lru_lambda,
              w_attn_proj, w_lru_proj, w_out, norm2_g, w_ffn_gate, w_ffn_up, w_ffn_down):
    B, S = x.shape[0], x.shape[1]
    inv_freq = ROPE_THETA ** (-jnp.arange(0, ROT_DIM, 2, dtype=jnp.float32) / ROT_DIM)
    ang = positions.astype(jnp.float32)[..., None] * inv_freq
    cos, sin = jnp.cos(ang)[:, :, None, :], jnp.sin(ang)[:, :, None, :]

    h = x
    for l in range(DEPTH):
        xn = rms_norm(h, norm1_g[l])
        z = xn @ w_in[l]
        q, k, v, u, gr, ga, gl = jnp.split(z, SPLITS, axis=-1)
        g_attn = jax.nn.sigmoid(ga + b_gates[l][:D_MODEL])
        g_lru = jax.nn.sigmoid(gl + b_gates[l][D_MODEL:])

        q = rms_norm(q.reshape(B, S, N_Q_HEADS, HEAD_DIM), q_norm_g[l])
        k = rms_norm(k.reshape(B, S, N_KV_HEADS, HEAD_DIM), k_norm_g[l])
        q = partial_rope(q, cos, sin)
        k = partial_rope(k, cos, sin)
        v = v.reshape(B, S, N_KV_HEADS, HEAD_DIM)
        attn = window_attention(q, k, v, sinks[l])

        uc = causal_depthwise_conv(u, conv_w[l], conv_b[l])
        rec = rg_lru(uc, w_rgate[l], b_rgate[l], w_igate[l], b_igate[l], lru_lambda[l])
        rec = rec * jax.nn.gelu(gr)

        merged = g_attn * (attn @ w_attn_proj[l]) + g_lru * (rec @ w_lru_proj[l])
        h = h + merged @ w_out[l]

        hn = rms_norm(h, norm2_g[l])
        ff = (jax.nn.silu(hn @ w_ffn_gate[l]) * (hn @ w_ffn_up[l])) @ w_ffn_down[l]
        h = h + ff
    return h


import jax as _jax
import jax.numpy as _jnp

TWIN_FORMAT = 'train_step'
FWD_PARAMS = ['x', 'positions', 'norm1_g', 'w_in', 'b_gates', 'q_norm_g', 'k_norm_g', 'sinks', 'conv_w', 'conv_b', 'w_rgate', 'b_rgate', 'w_igate', 'b_igate', 'lru_lambda', 'w_attn_proj', 'w_lru_proj', 'w_out', 'norm2_g', 'w_ffn_gate', 'w_ffn_up', 'w_ffn_down']
TWIN_WEIGHTS = ['norm1_g', 'w_in', 'b_gates', 'q_norm_g', 'k_norm_g', 'sinks', 'conv_w', 'conv_b', 'w_rgate', 'b_rgate', 'w_igate', 'b_igate', 'lru_lambda', 'w_attn_proj', 'w_lru_proj', 'w_out', 'norm2_g', 'w_ffn_gate', 'w_ffn_up', 'w_ffn_down']
TWIN_DIFF_INPUT = 'x'
TWIN_INPUTS = ['x', 'positions', 'norm1_g', 'w_in', 'b_gates', 'q_norm_g', 'k_norm_g', 'sinks', 'conv_w', 'conv_b', 'w_rgate', 'b_rgate', 'w_igate', 'b_igate', 'lru_lambda', 'w_attn_proj', 'w_lru_proj', 'w_out', 'norm2_g', 'w_ffn_gate', 'w_ffn_up', 'w_ffn_down', 'loss_target', 'm_norm1_g', 'm_w_in', 'm_b_gates', 'm_q_norm_g', 'm_k_norm_g', 'm_sinks', 'm_conv_w', 'm_conv_b', 'm_w_rgate', 'm_b_rgate', 'm_w_igate', 'm_b_igate', 'm_lru_lambda', 'm_w_attn_proj', 'm_w_lru_proj', 'm_w_out', 'm_norm2_g', 'm_w_ffn_gate', 'm_w_ffn_up', 'm_w_ffn_down', 'v_norm1_g', 'v_w_in', 'v_b_gates', 'v_q_norm_g', 'v_k_norm_g', 'v_sinks', 'v_conv_w', 'v_conv_b', 'v_w_rgate', 'v_b_rgate', 'v_w_igate', 'v_b_igate', 'v_lru_lambda', 'v_w_attn_proj', 'v_w_lru_proj', 'v_w_out', 'v_norm2_g', 'v_w_ffn_gate', 'v_w_ffn_up', 'v_w_ffn_down']
TWIN_OUTPUTS = ['loss', 'grad_x', 'grad_norm1_g', 'grad_w_in', 'grad_b_gates', 'grad_q_norm_g', 'grad_k_norm_g', 'grad_sinks', 'grad_conv_w', 'grad_conv_b', 'grad_w_rgate', 'grad_b_rgate', 'grad_w_igate', 'grad_b_igate', 'grad_lru_lambda', 'grad_w_attn_proj', 'grad_w_lru_proj', 'grad_w_out', 'grad_norm2_g', 'grad_w_ffn_gate', 'grad_w_ffn_up', 'grad_w_ffn_down', 'delta_norm1_g', 'delta_w_in', 'delta_b_gates', 'delta_q_norm_g', 'delta_k_norm_g', 'delta_sinks', 'delta_conv_w', 'delta_conv_b', 'delta_w_rgate', 'delta_b_rgate', 'delta_w_igate', 'delta_b_igate', 'delta_lru_lambda', 'delta_w_attn_proj', 'delta_w_lru_proj', 'delta_w_out', 'delta_norm2_g', 'delta_w_ffn_gate', 'delta_w_ffn_up', 'delta_w_ffn_down', 'new_m_norm1_g', 'new_m_w_in', 'new_m_b_gates', 'new_m_q_norm_g', 'new_m_k_norm_g', 'new_m_sinks', 'new_m_conv_w', 'new_m_conv_b', 'new_m_w_rgate', 'new_m_b_rgate', 'new_m_w_igate', 'new_m_b_igate', 'new_m_lru_lambda', 'new_m_w_attn_proj', 'new_m_w_lru_proj', 'new_m_w_out', 'new_m_norm2_g', 'new_m_w_ffn_gate', 'new_m_w_ffn_up', 'new_m_w_ffn_down', 'new_v_norm1_g', 'new_v_w_in', 'new_v_b_gates', 'new_v_q_norm_g', 'new_v_k_norm_g', 'new_v_sinks', 'new_v_conv_w', 'new_v_conv_b', 'new_v_w_rgate', 'new_v_b_rgate', 'new_v_w_igate', 'new_v_b_igate', 'new_v_lru_lambda', 'new_v_w_attn_proj', 'new_v_w_lru_proj', 'new_v_w_out', 'new_v_norm2_g', 'new_v_w_ffn_gate', 'new_v_w_ffn_up', 'new_v_w_ffn_down']
TWIN_LEAF_KINDS = {'loss': 'loss', 'grad_x': 'grad_x', 'grad_norm1_g': 'grad_w', 'grad_w_in': 'grad_w', 'grad_b_gates': 'grad_w', 'grad_q_norm_g': 'grad_w', 'grad_k_norm_g': 'grad_w', 'grad_sinks': 'grad_w', 'grad_conv_w': 'grad_w', 'grad_conv_b': 'grad_w', 'grad_w_rgate': 'grad_w', 'grad_b_rgate': 'grad_w', 'grad_w_igate': 'grad_w', 'grad_b_igate': 'grad_w', 'grad_lru_lambda': 'grad_w', 'grad_w_attn_proj': 'grad_w', 'grad_w_lru_proj': 'grad_w', 'grad_w_out': 'grad_w', 'grad_norm2_g': 'grad_w', 'grad_w_ffn_gate': 'grad_w', 'grad_w_ffn_up': 'grad_w', 'grad_w_ffn_down': 'grad_w', 'delta_norm1_g': 'delta_w', 'delta_w_in': 'delta_w', 'delta_b_gates': 'delta_w', 'delta_q_norm_g': 'delta_w', 'delta_k_norm_g': 'delta_w', 'delta_sinks': 'delta_w', 'delta_conv_w': 'delta_w', 'delta_conv_b': 'delta_w', 'delta_w_rgate': 'delta_w', 'delta_b_rgate': 'delta_w', 'delta_w_igate': 'delta_w', 'delta_b_igate': 'delta_w', 'delta_lru_lambda': 'delta_w', 'delta_w_attn_proj': 'delta_w', 'delta_w_lru_proj': 'delta_w', 'delta_w_out': 'delta_w', 'delta_norm2_g': 'delta_w', 'delta_w_ffn_gate': 'delta_w', 'delta_w_ffn_up': 'delta_w', 'delta_w_ffn_down': 'delta_w', 'new_m_norm1_g': 'new_m', 'new_m_w_in': 'new_m', 'new_m_b_gates': 'new_m', 'new_m_q_norm_g': 'new_m', 'new_m_k_norm_g': 'new_m', 'new_m_sinks': 'new_m', 'new_m_conv_w': 'new_m', 'new_m_conv_b': 'new_m', 'new_m_w_rgate': 'new_m', 'new_m_b_rgate': 'new_m', 'new_m_w_igate': 'new_m', 'new_m_b_igate': 'new_m', 'new_m_lru_lambda': 'new_m', 'new_m_w_attn_proj': 'new_m', 'new_m_w_lru_proj': 'new_m', 'new_m_w_out': 'new_m', 'new_m_norm2_g': 'new_m', 'new_m_w_ffn_gate': 'new_m', 'new_m_w_ffn_up': 'new_m', 'new_m_w_ffn_down': 'new_m', 'new_v_norm1_g': 'new_v', 'new_v_w_in': 'new_v', 'new_v_b_gates': 'new_v', 'new_v_q_norm_g': 'new_v', 'new_v_k_norm_g': 'new_v', 'new_v_sinks': 'new_v', 'new_v_conv_w': 'new_v', 'new_v_conv_b': 'new_v', 'new_v_w_rgate': 'new_v', 'new_v_b_rgate': 'new_v', 'new_v_w_igate': 'new_v', 'new_v_b_igate': 'new_v', 'new_v_lru_lambda': 'new_v', 'new_v_w_attn_proj': 'new_v', 'new_v_w_lru_proj': 'new_v', 'new_v_w_out': 'new_v', 'new_v_norm2_g': 'new_v', 'new_v_w_ffn_gate': 'new_v', 'new_v_w_ffn_up': 'new_v', 'new_v_w_ffn_down': 'new_v'}


def _forward(args):
    return _fwd_reference(*[args[k] for k in FWD_PARAMS])


def _output_shape():
    def fwd():
        inp = _fwd_setup_inputs(0)
        return _fwd_reference(*[inp[k] for k in FWD_PARAMS])
    out = _jax.eval_shape(fwd)
    return out.shape, out.dtype

N_MICROBATCH = 1
ADAM_LR = 0.001
ADAM_B1 = 0.9
ADAM_B2 = 0.999
ADAM_EPS = 1e-08
ADAM_WD = 0.01
ADAM_STEP = 10
PER_EXAMPLE_BATCH_AXIS = {'x': 0, 'positions': 0, 'loss_target': 0}
SHARED_INPUTS = []
_WEIGHT_DTYPES = {'norm1_g': _jnp.float32, 'w_in': _jnp.float32, 'b_gates': _jnp.float32, 'q_norm_g': _jnp.float32, 'k_norm_g': _jnp.float32, 'sinks': _jnp.float32, 'conv_w': _jnp.float32, 'conv_b': _jnp.float32, 'w_rgate': _jnp.float32, 'b_rgate': _jnp.float32, 'w_igate': _jnp.float32, 'b_igate': _jnp.float32, 'lru_lambda': _jnp.float32, 'w_attn_proj': _jnp.float32, 'w_lru_proj': _jnp.float32, 'w_out': _jnp.float32, 'norm2_g': _jnp.float32, 'w_ffn_gate': _jnp.float32, 'w_ffn_up': _jnp.float32, 'w_ffn_down': _jnp.float32}
MOMENT_SCALE = {'norm1_g': 1.499979e+00, 'w_in': 2.958546e-02, 'b_gates': 1.489052e-01, 'q_norm_g': 1.024305e+00, 'k_norm_g': 1.028087e+00, 'sinks': 2.944923e-01, 'conv_w': 3.357555e-01, 'conv_b': 9.763959e-01, 'w_rgate': 4.019445e-02, 'b_rgate': 3.617814e-02, 'w_igate': 7.293883e-02, 'b_igate': 2.210836e-01, 'lru_lambda': 6.668327e-02, 'w_attn_proj': 2.430201e-02, 'w_lru_proj': 5.289329e-02, 'w_out': 5.220968e-02, 'norm2_g': 1.235721e+01, 'w_ffn_gate': 7.355108e-02, 'w_ffn_up': 7.362759e-02, 'w_ffn_down': 1.114542e-01}


def _to_microbatches(a, axis):
    t = _jnp.moveaxis(a, axis, 0)
    t = t.reshape((N_MICROBATCH, t.shape[0] // N_MICROBATCH) + t.shape[1:])
    return _jnp.moveaxis(t, 1, axis + 1)


def setup_inputs(seed: int = 0) -> dict:
    inp = _fwd_setup_inputs(seed)
    key = _jax.random.fold_in(_jax.random.key(seed), 7919)
    shape, _ = _output_shape()
    out = dict(inp)
    out["loss_target"] = _jax.random.normal(_jax.random.fold_in(key, 0), shape, _jnp.float32)
    for i, name in enumerate(TWIN_WEIGHTS):
        w = inp[name].astype(_jnp.float32)
        if MOMENT_SCALE is None:
            s = _jnp.sqrt(_jnp.mean(_jnp.square(w)) + 1e-30)
        else:
            s = MOMENT_SCALE[name]
        km, kv = _jax.random.split(_jax.random.fold_in(key, i + 1))
        out[name] = w
        out["m_" + name] = s * _jax.random.normal(km, w.shape, _jnp.float32)
        out["v_" + name] = (s * s) * _jax.random.uniform(kv, w.shape, _jnp.float32, 0.5, 1.5)
    if N_MICROBATCH > 1:
        for name, axis in PER_EXAMPLE_BATCH_AXIS.items():
            out[name] = _to_microbatches(out[name], axis)
    return {'x': out['x'], 'positions': out['positions'], 'norm1_g': out['norm1_g'], 'w_in': out['w_in'], 'b_gates': out['b_gates'], 'q_norm_g': out['q_norm_g'], 'k_norm_g': out['k_norm_g'], 'sinks': out['sinks'], 'conv_w': out['conv_w'], 'conv_b': out['conv_b'], 'w_rgate': out['w_rgate'], 'b_rgate': out['b_rgate'], 'w_igate': out['w_igate'], 'b_igate': out['b_igate'], 'lru_lambda': out['lru_lambda'], 'w_attn_proj': out['w_attn_proj'], 'w_lru_proj': out['w_lru_proj'], 'w_out': out['w_out'], 'norm2_g': out['norm2_g'], 'w_ffn_gate': out['w_ffn_gate'], 'w_ffn_up': out['w_ffn_up'], 'w_ffn_down': out['w_ffn_down'], 'loss_target': out['loss_target'], 'm_norm1_g': out['m_norm1_g'], 'm_w_in': out['m_w_in'], 'm_b_gates': out['m_b_gates'], 'm_q_norm_g': out['m_q_norm_g'], 'm_k_norm_g': out['m_k_norm_g'], 'm_sinks': out['m_sinks'], 'm_conv_w': out['m_conv_w'], 'm_conv_b': out['m_conv_b'], 'm_w_rgate': out['m_w_rgate'], 'm_b_rgate': out['m_b_rgate'], 'm_w_igate': out['m_w_igate'], 'm_b_igate': out['m_b_igate'], 'm_lru_lambda': out['m_lru_lambda'], 'm_w_attn_proj': out['m_w_attn_proj'], 'm_w_lru_proj': out['m_w_lru_proj'], 'm_w_out': out['m_w_out'], 'm_norm2_g': out['m_norm2_g'], 'm_w_ffn_gate': out['m_w_ffn_gate'], 'm_w_ffn_up': out['m_w_ffn_up'], 'm_w_ffn_down': out['m_w_ffn_down'], 'v_norm1_g': out['v_norm1_g'], 'v_w_in': out['v_w_in'], 'v_b_gates': out['v_b_gates'], 'v_q_norm_g': out['v_q_norm_g'], 'v_k_norm_g': out['v_k_norm_g'], 'v_sinks': out['v_sinks'], 'v_conv_w': out['v_conv_w'], 'v_conv_b': out['v_conv_b'], 'v_w_rgate': out['v_w_rgate'], 'v_b_rgate': out['v_b_rgate'], 'v_w_igate': out['v_w_igate'], 'v_b_igate': out['v_b_igate'], 'v_lru_lambda': out['v_lru_lambda'], 'v_w_attn_proj': out['v_w_attn_proj'], 'v_w_lru_proj': out['v_w_lru_proj'], 'v_w_out': out['v_w_out'], 'v_norm2_g': out['v_norm2_g'], 'v_w_ffn_gate': out['v_w_ffn_gate'], 'v_w_ffn_up': out['v_w_ffn_up'], 'v_w_ffn_down': out['v_w_ffn_down']}


def _loss(weights, diff, rest, loss_target):
    with _jax.named_scope("forward"):
        args = {**rest, TWIN_DIFF_INPUT: diff, **{k: w.astype(_WEIGHT_DTYPES[k]) for k, w in weights.items()}}
        y = _forward(args)
    with _jax.named_scope("loss_head"):
        err = _jnp.square(y.astype(_jnp.float32) - loss_target)
        return 0.5 * _jnp.sum(_jnp.mean(err, axis=-1)) if err.ndim else 0.5 * err


def _adamw(w, g, m, v):
    m = ADAM_B1 * m + (1.0 - ADAM_B1) * g
    v = ADAM_B2 * v + (1.0 - ADAM_B2) * _jnp.square(g)
    m_hat = m / (1.0 - ADAM_B1 ** ADAM_STEP)
    v_hat = v / (1.0 - ADAM_B2 ** ADAM_STEP)
    delta = -ADAM_LR * (m_hat / (_jnp.sqrt(v_hat) + ADAM_EPS) + ADAM_WD * w)
    return delta, m, v


def reference(x, positions, norm1_g, w_in, b_gates, q_norm_g, k_norm_g, sinks, conv_w, conv_b, w_rgate, b_rgate, w_igate, b_igate, lru_lambda, w_attn_proj, w_lru_proj, w_out, norm2_g, w_ffn_gate, w_ffn_up, w_ffn_down, loss_target, m_norm1_g, m_w_in, m_b_gates, m_q_norm_g, m_k_norm_g, m_sinks, m_conv_w, m_conv_b, m_w_rgate, m_b_rgate, m_w_igate, m_b_igate, m_lru_lambda, m_w_attn_proj, m_w_lru_proj, m_w_out, m_norm2_g, m_w_ffn_gate, m_w_ffn_up, m_w_ffn_down, v_norm1_g, v_w_in, v_b_gates, v_q_norm_g, v_k_norm_g, v_sinks, v_conv_w, v_conv_b, v_w_rgate, v_b_rgate, v_w_igate, v_b_igate, v_lru_lambda, v_w_attn_proj, v_w_lru_proj, v_w_out, v_norm2_g, v_w_ffn_gate, v_w_ffn_up, v_w_ffn_down):
    given = dict(x=x, positions=positions, norm1_g=norm1_g, w_in=w_in, b_gates=b_gates, q_norm_g=q_norm_g, k_norm_g=k_norm_g, sinks=sinks, conv_w=conv_w, conv_b=conv_b, w_rgate=w_rgate, b_rgate=b_rgate, w_igate=w_igate, b_igate=b_igate, lru_lambda=lru_lambda, w_attn_proj=w_attn_proj, w_lru_proj=w_lru_proj, w_out=w_out, norm2_g=norm2_g, w_ffn_gate=w_ffn_gate, w_ffn_up=w_ffn_up, w_ffn_down=w_ffn_down, loss_target=loss_target, m_norm1_g=m_norm1_g, m_w_in=m_w_in, m_b_gates=m_b_gates, m_q_norm_g=m_q_norm_g, m_k_norm_g=m_k_norm_g, m_sinks=m_sinks, m_conv_w=m_conv_w, m_conv_b=m_conv_b, m_w_rgate=m_w_rgate, m_b_rgate=m_b_rgate, m_w_igate=m_w_igate, m_b_igate=m_b_igate, m_lru_lambda=m_lru_lambda, m_w_attn_proj=m_w_attn_proj, m_w_lru_proj=m_w_lru_proj, m_w_out=m_w_out, m_norm2_g=m_norm2_g, m_w_ffn_gate=m_w_ffn_gate, m_w_ffn_up=m_w_ffn_up, m_w_ffn_down=m_w_ffn_down, v_norm1_g=v_norm1_g, v_w_in=v_w_in, v_b_gates=v_b_gates, v_q_norm_g=v_q_norm_g, v_k_norm_g=v_k_norm_g, v_sinks=v_sinks, v_conv_w=v_conv_w, v_conv_b=v_conv_b, v_w_rgate=v_w_rgate, v_b_rgate=v_b_rgate, v_w_igate=v_w_igate, v_b_igate=v_b_igate, v_lru_lambda=v_lru_lambda, v_w_attn_proj=v_w_attn_proj, v_w_lru_proj=v_w_lru_proj, v_w_out=v_w_out, v_norm2_g=v_norm2_g, v_w_ffn_gate=v_w_ffn_gate, v_w_ffn_up=v_w_ffn_up, v_w_ffn_down=v_w_ffn_down)
    weights = {n: given[n] for n in TWIN_WEIGHTS}
    shared = {n: given[n] for n in SHARED_INPUTS}
    per_example = {n: given[n] for n in ['x', 'positions']}
    grad_fn = _jax.value_and_grad(_loss, argnums=(0, 1))

    def one_microbatch(ex, loss_target):
        ex = dict(ex)
        diff = ex.pop(TWIN_DIFF_INPUT)
        return grad_fn(weights, diff, {**shared, **ex}, loss_target)

    if N_MICROBATCH == 1:
        loss, (grad_w, grad_x) = one_microbatch(per_example, given["loss_target"])
    else:
        def body(carry, xs):
            loss_sum, grad_sum = carry
            l_k, (gw_k, gx_k) = one_microbatch(xs[0], xs[1])
            with _jax.named_scope("update"):
                return (loss_sum + l_k, _jax.tree.map(_jnp.add, grad_sum, gw_k)), gx_k

        init = (_jnp.zeros((), _jnp.float32), _jax.tree.map(_jnp.zeros_like, weights))
        (loss, grad_w), grad_x = _jax.lax.scan(body, init, (per_example, given["loss_target"]))
    with _jax.named_scope("update"):
        delta_w, new_m, new_v = {}, {}, {}
        for n in TWIN_WEIGHTS:
            delta_w[n], new_m[n], new_v[n] = _adamw(weights[n], grad_w[n], given["m_" + n], given["v_" + n])
    return (loss, grad_x, *[grad_w[n] for n in TWIN_WEIGHTS], *[delta_w[n] for n in TWIN_WEIGHTS],
            *[new_m[n] for n in TWIN_WEIGHTS], *[new_v[n] for n in TWIN_WEIGHTS])
```

```python
import functools
import math

import jax
import jax.numpy as jnp
from jax import lax
from jax.experimental import pallas as pl
from jax.experimental.pallas import tpu as pltpu

F32 = jnp.float32
BF16 = jnp.bfloat16
MESH = pl.DeviceIdType.MESH

WINDOW = 128
BLK = 128
ROPE_THETA = 500000.0
LRU_C = 8.0
EPS = 1e-6
NEG = -1e30
ADAM_LR = 0.001
ADAM_B1 = 0.9
ADAM_B2 = 0.999
ADAM_EPS = 1e-08
ADAM_WD = 0.01
ADAM_STEP = 10

VMEM_LIMIT_BYTES = 52 * 1024 * 1024
LANE = 128
SUBLANE = 8
N_CHIPS = 4
SMALL_PACK_COLS = 512


def _params(**kw):
    return pltpu.CompilerParams(vmem_limit_bytes=VMEM_LIMIT_BYTES, **kw)


def _pick(dim, cands):
    for c in cands:
        if dim % c == 0:
            return c
    return dim


def _sigmoid(x):
    return 1.0 / (1.0 + jnp.exp(-x))


_MN_TILES = (1024, 1408, 1280, 512, 256, 128)
_K_TILES = (1408, 1280, 1024, 512, 256, 128)


def _matmul(a, b, mode, name, add=None, out_dtype=F32):
    if mode == "nn":
        (m, k), (k2, n) = a.shape, b.shape
    elif mode == "nt":
        (m, k), (n, k2) = a.shape, b.shape
    else:
        (k, m), (k2, n) = a.shape, b.shape
    assert k == k2, (a.shape, b.shape, mode)
    tm, tn, tk = _pick(m, _MN_TILES), _pick(n, _MN_TILES), _pick(k, _K_TILES)
    nk = k // tk
    if mode == "nn":
        a_spec = pl.BlockSpec((tm, tk), lambda i, j, kk: (i, kk))
        b_spec = pl.BlockSpec((tk, tn), lambda i, j, kk: (kk, j))
        dims = (((1,), (0,)), ((), ()))
    elif mode == "nt":
        a_spec = pl.BlockSpec((tm, tk), lambda i, j, kk: (i, kk))
        b_spec = pl.BlockSpec((tn, tk), lambda i, j, kk: (j, kk))
        dims = (((1,), (1,)), ((), ()))
    else:
        a_spec = pl.BlockSpec((tk, tm), lambda i, j, kk: (kk, i))
        b_spec = pl.BlockSpec((tk, tn), lambda i, j, kk: (kk, j))
        dims = (((0,), (0,)), ((), ()))
    o_spec = pl.BlockSpec((tm, tn), lambda i, j, kk: (i, j))
    has_add = add is not None

    def body(*refs):
        if has_add:
            a_ref, b_ref, add_ref, o_ref, acc = refs
        else:
            a_ref, b_ref, o_ref, acc = refs
        kk = pl.program_id(2)

        @pl.when(kk == 0)
        def _():
            acc[...] = jnp.zeros_like(acc)

        acc[...] += lax.dot_general(a_ref[...].astype(BF16), b_ref[...].astype(BF16), dims,
                                    preferred_element_type=F32)

        @pl.when(kk == nk - 1)
        def _():
            r = acc[...]
            if has_add:
                r = r + add_ref[...]
            o_ref[...] = r.astype(out_dtype)

    in_specs = [a_spec, b_spec] + ([o_spec] if has_add else [])
    args = (a, b) + ((add,) if has_add else ())
    return pl.pallas_call(
        body, name=name, out_shape=jax.ShapeDtypeStruct((m, n), out_dtype),
        grid=(m // tm, n // tn, nk), in_specs=in_specs, out_specs=o_spec,
        scratch_shapes=[pltpu.VMEM((tm, tn), F32)], compiler_params=_params(),
    )(*args)


def _row_tile(rows, cols, budget_elems=512 * 1024):
    cands = [c for c in (1024, 704, 512, 352, 256, 128, 64, 32, 16) if c * cols <= budget_elems]
    return _pick(rows, cands or (16,))


def _cast_bf16(w, name):
    r, c = w.shape
    tr = _row_tile(r, c)

    def body(w_ref, o_ref):
        o_ref[...] = w_ref[...].astype(BF16)

    spec = pl.BlockSpec((tr, c), lambda i: (i, 0))
    return pl.pallas_call(body, name=name, out_shape=jax.ShapeDtypeStruct((r, c), BF16), grid=(r // tr,),
                          in_specs=[spec], out_specs=spec, compiler_params=_params())(w)


def _rms_fwd(x, g, name):
    t, d = x.shape
    tr = _row_tile(t, d)

    def body(x_ref, g_ref, o_ref):
        xv = x_ref[...]
        rstd = lax.rsqrt(jnp.mean(xv * xv, axis=-1, keepdims=True) + EPS)
        o_ref[...] = (xv * rstd * g_ref[...]).astype(BF16)

    spec = pl.BlockSpec((tr, d), lambda i: (i, 0))
    return pl.pallas_call(body, name=name, out_shape=jax.ShapeDtypeStruct((t, d), BF16), grid=(t // tr,),
                          in_specs=[spec, pl.BlockSpec((1, d), lambda i: (0, 0))], out_specs=spec,
                          compiler_params=_params())(x, g)


def _rms_bwd(dxn, x, g, resid, name):
    t, d = x.shape
    tr = _row_tile(t, d, 256 * 1024)

    def body(dxn_ref, x_ref, g_ref, r_ref, dx_ref, dg_ref):
        @pl.when(pl.program_id(0) == 0)
        def _():
            dg_ref[...] = jnp.zeros_like(dg_ref)

        xv = x_ref[...]
        rstd = lax.rsqrt(jnp.mean(xv * xv, axis=-1, keepdims=True) + EPS)
        xhat = xv * rstd
        dy = dxn_ref[...]
        dg_ref[...] += jnp.sum(dy * xhat, axis=0, keepdims=True)
        dxhat = dy * g_ref[...]
        dx_ref[...] = r_ref[...] + rstd * (dxhat - xhat * jnp.mean(dxhat * xhat, axis=-1, keepdims=True))

    spec = pl.BlockSpec((tr, d), lambda i: (i, 0))
    vec = pl.BlockSpec((1, d), lambda i: (0, 0))
    return pl.pallas_call(
        body, name=name, out_shape=(jax.ShapeDtypeStruct((t, d), F32), jax.ShapeDtypeStruct((1, d), F32)),
        grid=(t // tr,), in_specs=[spec, spec, vec, spec], out_specs=(spec, vec), compiler_params=_params(),
    )(dxn, x, g, resid)


def _swiglu_fwd(gate, up, name):
    t, f = gate.shape
    tr = _row_tile(t, f, 256 * 1024)

    def body(g_ref, u_ref, o_ref):
        gv = g_ref[...]
        o_ref[...] = (gv * _sigmoid(gv) * u_ref[...]).astype(BF16)

    spec = pl.BlockSpec((tr, f), lambda i: (i, 0))
    return pl.pallas_call(body, name=name, out_shape=jax.ShapeDtypeStruct((t, f), BF16), grid=(t // tr,),
                          in_specs=[spec, spec], out_specs=spec, compiler_params=_params())(gate, up)


def _swiglu_bwd(dact, gate, up, name):
    t, f = gate.shape
    tr = _row_tile(t, f, 128 * 1024)

    def body(d_ref, g_ref, u_ref, dg_ref, du_ref):
        gv, dv = g_ref[...], d_ref[...]
        sg = _sigmoid(gv)
        dg_ref[...] = (dv * u_ref[...] * (sg * (1.0 + gv * (1.0 - sg)))).astype(BF16)
        du_ref[...] = (dv * (gv * sg)).astype(BF16)

    spec = pl.BlockSpec((tr, f), lambda i: (i, 0))
    shp = jax.ShapeDtypeStruct((t, f), BF16)
    return pl.pallas_call(body, name=name, out_shape=(shp, shp), grid=(t // tr,), in_specs=[spec, spec, spec],
                          out_specs=(spec, spec), compiler_params=_params())(dact, gate, up)


def _merge_fwd(z, b_gates, pa, plru, ga_off, name):
    t, d = pa.shape
    cw = _pick(math.gcd(ga_off, d), (512, 256, 128))
    tr = _row_tile(t, cw, 256 * 1024)
    oa, ol, nd = ga_off // cw, (ga_off + d) // cw, d // cw

    def body(ga_ref, gl_ref, ba_ref, bl_ref, pa_ref, pl_ref, o_ref):
        sa = _sigmoid(ga_ref[...] + ba_ref[...])
        sl = _sigmoid(gl_ref[...] + bl_ref[...])
        o_ref[...] = (sa * pa_ref[...] + sl * pl_ref[...]).astype(BF16)

    blk = pl.BlockSpec((tr, cw), lambda i, j: (i, j))
    return pl.pallas_call(
        body, name=name, out_shape=jax.ShapeDtypeStruct((t, d), BF16), grid=(t // tr, nd),
        in_specs=[pl.BlockSpec((tr, cw), lambda i, j: (i, oa + j)), pl.BlockSpec((tr, cw), lambda i, j: (i, ol + j)),
                  pl.BlockSpec((1, cw), lambda i, j: (0, j)), pl.BlockSpec((1, cw), lambda i, j: (0, nd + j)),
                  blk, blk],
        out_specs=blk, compiler_params=_params(),
    )(z, z, b_gates, b_gates, pa, plru)


def _merge_bwd(dmerged, z, b_gates, pa, plru, ga_off, name):
    t, d = pa.shape
    cw = _pick(math.gcd(ga_off, d), (512, 256, 128))
    tr = _row_tile(t, cw, 256 * 1024)
    oa, ol, nd = ga_off // cw, (ga_off + d) // cw, d // cw

    def body(dm_ref, ga_ref, gl_ref, ba_ref, bl_ref, pa_ref, pl_ref, dpa_ref, dpl_ref, dga_ref, dgl_ref, sa_ref, sl_ref):
        @pl.when(pl.program_id(1) == 0)
        def _():
            sa_ref[...] = jnp.zeros_like(sa_ref)
            sl_ref[...] = jnp.zeros_like(sl_ref)

        dm = dm_ref[...]
        sa = _sigmoid(ga_ref[...] + ba_ref[...])
        sl = _sigmoid(gl_ref[...] + bl_ref[...])
        dpa_ref[...] = (dm * sa).astype(BF16)
        dpl_ref[...] = (dm * sl).astype(BF16)
        dga = dm * pa_ref[...] * (sa * (1.0 - sa))
        dgl = dm * pl_ref[...] * (sl * (1.0 - sl))
        dga_ref[...] = dga
        dgl_ref[...] = dgl
        sa_ref[...] += jnp.sum(dga, axis=0, keepdims=True)
        sl_ref[...] += jnp.sum(dgl, axis=0, keepdims=True)

    blk = pl.BlockSpec((tr, cw), lambda j, i: (i, j))
    vec = pl.BlockSpec((1, cw), lambda j, i: (0, j))
    big16, big32, v32 = (jax.ShapeDtypeStruct((t, d), BF16), jax.ShapeDtypeStruct((t, d), F32),
                         jax.ShapeDtypeStruct((1, d), F32))
    return pl.pallas_call(
        body, name=name, out_shape=(big16, big16, big32, big32, v32, v32), grid=(nd, t // tr),
        in_specs=[blk, pl.BlockSpec((tr, cw), lambda j, i: (i, oa + j)), pl.BlockSpec((tr, cw), lambda j, i: (i, ol + j)),
                  vec, pl.BlockSpec((1, cw), lambda j, i: (0, nd + j)), blk, blk],
        out_specs=(blk, blk, blk, blk, vec, vec), compiler_params=_params(),
    )(dmerged, z, z, b_gates, b_gates, pa, plru)


def _loss_head(y, target, name):
    t, d = y.shape
    tr = _row_tile(t, d, 256 * 1024)
    nt = t // tr

    def body(y_ref, t_ref, dy_ref, loss_ref, acc):
        i = pl.program_id(0)

        @pl.when(i == 0)
        def _():
            acc[...] = jnp.zeros_like(acc)

        e = y_ref[...] - t_ref[...]
        dy_ref[...] = e * (1.0 / d)
        acc[...] += jnp.sum(e * e, axis=0, keepdims=True)

        @pl.when(i == nt - 1)
        def _():
            loss_ref[...] = (0.5 / d) * jnp.sum(acc[...], axis=-1, keepdims=True)

    spec = pl.BlockSpec((tr, d), lambda i: (i, 0))
    return pl.pallas_call(
        body, name=name, out_shape=(jax.ShapeDtypeStruct((t, d), F32), jax.ShapeDtypeStruct((1, 1), F32)),
        grid=(nt,), in_specs=[spec, spec], out_specs=(spec, pl.BlockSpec((1, 1), lambda i: (0, 0))),
        scratch_shapes=[pltpu.VMEM((1, d), F32)], compiler_params=_params(),
    )(y, target)


def _adamw(w, g, m, v, name):
    r, c = w.shape
    tr = _row_tile(r, c, 128 * 1024)
    c1 = 1.0 - ADAM_B1 ** ADAM_STEP
    c2 = 1.0 - ADAM_B2 ** ADAM_STEP

    def body(w_ref, g_ref, m_ref, v_ref, d_ref, nm_ref, nv_ref):
        gv = g_ref[...]
        mn = ADAM_B1 * m_ref[...] + (1.0 - ADAM_B1) * gv
        vn = ADAM_B2 * v_ref[...] + (1.0 - ADAM_B2) * (gv * gv)
        d_ref[...] = -ADAM_LR * ((mn / c1) / (jnp.sqrt(vn / c2) + ADAM_EPS) + ADAM_WD * w_ref[...])
        nm_ref[...] = mn
        nv_ref[...] = vn

    spec = pl.BlockSpec((tr, c), lambda i: (i, 0))
    shp = jax.ShapeDtypeStruct((r, c), F32)
    return pl.pallas_call(body, name=name, out_shape=(shp, shp, shp), grid=(r // tr,), in_specs=[spec] * 4,
                          out_specs=(spec, spec, spec), compiler_params=_params())(w, g, m, v)


def _swap_halves(v, lane, half):
    n = v.shape[-1]
    return jnp.where(lane < half, pltpu.roll(v, n - half, 1),
                     jnp.where(lane < 2 * half, pltpu.roll(v, half, 1), 0.0))


def _norm_fwd(xraw, g):
    rstd = lax.rsqrt(jnp.mean(xraw * xraw, axis=-1, keepdims=True) + EPS)
    xhat = xraw * rstd
    return xhat, rstd, xhat * g


def _norm_bwd(dy, xhat, rstd, g):
    dxhat = dy * g
    dx = rstd * (dxhat - xhat * jnp.mean(dxhat * xhat, axis=-1, keepdims=True))
    return dx, jnp.sum(dy * xhat, axis=0, keepdims=True)


def _attn_specs(nb, grp, hd, kv, clamp):
    qo, ko, vo = 0, (kv * grp), (kv * grp + kv)
    cur = (lambda i: jnp.minimum(i, nb - 1)) if clamp else (lambda i: i)
    prev = lambda i: jnp.maximum(cur(i) - 1, 0)
    zq = pl.BlockSpec((BLK, grp * hd), lambda h, i: (cur(i), h))
    kc = pl.BlockSpec((BLK, hd), lambda h, i: (cur(i), ko + h))
    kp = pl.BlockSpec((BLK, hd), lambda h, i: (prev(i), ko + h))
    vc = pl.BlockSpec((BLK, hd), lambda h, i: (cur(i), vo + h))
    vp = pl.BlockSpec((BLK, hd), lambda h, i: (prev(i), vo + h))
    tc = pl.BlockSpec((BLK, hd), lambda h, i: (cur(i), 0))
    tp = pl.BlockSpec((BLK, hd), lambda h, i: (prev(i), 0))
    gs = pl.BlockSpec((1, hd), lambda h, i: (0, 0))
    return zq, kc, kp, vc, vp, tc, tp, gs


def _attn_mask(i):
    qi = lax.broadcasted_iota(jnp.int32, (BLK, 2 * BLK), 0)
    kj = lax.broadcasted_iota(jnp.int32, (BLK, 2 * BLK), 1)
    rel = qi + BLK - kj
    return (rel >= 0) & (rel < WINDOW) & ((kj >= BLK) | (i > 0))


def _attn_fwd(z, cos_t, sin_t, qg, kg, sinks, kv, grp, hd, name):
    t = z.shape[0]
    nb = t // BLK
    half = hd // 8
    scale = 1.0 / math.sqrt(hd)
    zq, kc, kp, vc, vp, tc, tp, gs = _attn_specs(nb, grp, hd, kv, False)

    def body(sink_ref, zq_ref, kc_ref, kp_ref, vc_ref, vp_ref, cc_ref, sc_ref, cp_ref, sp_ref, qg_ref, kg_ref, o_ref):
        h, i = pl.program_id(0), pl.program_id(1)
        lane = lax.broadcasted_iota(jnp.int32, (BLK, hd), 1)

        def normrope(xraw, g, c, s):
            y = _norm_fwd(xraw, g)[2]
            return y * c + _swap_halves(y, lane, half) * s

        cc, sc = cc_ref[...], sc_ref[...]
        kcur = normrope(kc_ref[...], kg_ref[...], cc, sc)
        kprev = normrope(kp_ref[...], kg_ref[...], cp_ref[...], sp_ref[...])
        kk = jnp.concatenate([kprev, kcur], axis=0).astype(BF16)
        vv = jnp.concatenate([vp_ref[...], vc_ref[...]], axis=0).astype(BF16)
        mask = _attn_mask(i)
        for g in range(grp):
            q = normrope(zq_ref[:, g * hd:(g + 1) * hd], qg_ref[...], cc, sc).astype(BF16)
            s = lax.dot_general(q, kk, (((1,), (1,)), ((), ())), preferred_element_type=F32) * scale
            s = jnp.where(mask, s, NEG)
            sk = sink_ref[h * grp + g]
            mx = jnp.maximum(jnp.max(s, axis=-1, keepdims=True), sk)
            p = jnp.exp(s - mx)
            den = jnp.sum(p, axis=-1, keepdims=True) + jnp.exp(sk - mx)
            p = p / den
            o_ref[:, g * hd:(g + 1) * hd] = jnp.dot(p.astype(BF16), vv, preferred_element_type=F32).astype(BF16)

    return pl.pallas_call(
        body, name=name, out_shape=jax.ShapeDtypeStruct((t, kv * grp * hd), BF16), grid=(kv, nb),
        in_specs=[pl.BlockSpec(memory_space=pltpu.SMEM), zq, kc, kp, vc, vp, tc, tc, tp, tp, gs, gs],
        out_specs=pl.BlockSpec((BLK, grp * hd), lambda h, i: (i, h)), compiler_params=_params(),
    )(sinks, z, z, z, z, z, cos_t, sin_t, cos_t, sin_t, qg, kg)


def _attn_bwd(dattn, z, cos_t, sin_t, qg, kg, sinks, kv, grp, hd, name):
    t = z.shape[0]
    nb = t // BLK
    half = hd // 8
    scale = 1.0 / math.sqrt(hd)
    zq, kc, kp, vc, vp, tc, tp, gs = _attn_specs(nb, grp, hd, kv, True)

    def body(sink_ref, zq_ref, kc_ref, kp_ref, vc_ref, vp_ref, cc_ref, sc_ref, cp_ref, sp_ref, qg_ref, kg_ref, do_ref,
             dq_ref, dk_ref, dv_ref, dqg_ref, dkg_ref, dsk_ref, dk_carry, dv_carry):
        h, i = pl.program_id(0), pl.program_id(1)
        lane = lax.broadcasted_iota(jnp.int32, (BLK, hd), 1)
        lane1 = lax.broadcasted_iota(jnp.int32, (1, LANE), 1)

        @pl.when((h == 0) & (i == 0))
        def _():
            dqg_ref[...] = jnp.zeros_like(dqg_ref)
            dkg_ref[...] = jnp.zeros_like(dkg_ref)
            dsk_ref[...] = jnp.zeros_like(dsk_ref)

        @pl.when(i == 0)
        def _():
            dk_carry[...] = jnp.zeros_like(dk_carry)
            dv_carry[...] = jnp.zeros_like(dv_carry)

        def rope(y, c, s):
            return y * c + _swap_halves(y, lane, half) * s

        def rope_bwd(dout, c, s):
            return dout * c + _swap_halves(dout * s, lane, half)

        @pl.when(i < nb)
        def _():
            cc, sc, cp, sp = cc_ref[...], sc_ref[...], cp_ref[...], sp_ref[...]
            qgv, kgv = qg_ref[...], kg_ref[...]
            xh_kc, rs_kc, y_kc = _norm_fwd(kc_ref[...], kgv)
            xh_kp, rs_kp, y_kp = _norm_fwd(kp_ref[...], kgv)
            kk = jnp.concatenate([rope(y_kp, cp, sp), rope(y_kc, cc, sc)], axis=0).astype(BF16)
            vv = jnp.concatenate([vp_ref[...], vc_ref[...]], axis=0).astype(BF16)
            mask = _attn_mask(i)
            dkk = jnp.zeros((2 * BLK, hd), F32)
            dvv = jnp.zeros((2 * BLK, hd), F32)
            dqg = jnp.zeros((1, hd), F32)
            dsk = jnp.zeros((1, LANE), F32)
            for g in range(grp):
                xh_q, rs_q, y_q = _norm_fwd(zq_ref[:, g * hd:(g + 1) * hd], qgv)
                q = rope(y_q, cc, sc).astype(BF16)
                s = lax.dot_general(q, kk, (((1,), (1,)), ((), ())), preferred_element_type=F32) * scale
                s = jnp.where(mask, s, NEG)
                sk = sink_ref[h * grp + g]
                mx = jnp.maximum(jnp.max(s, axis=-1, keepdims=True), sk)
                p = jnp.exp(s - mx)
                den = jnp.sum(p, axis=-1, keepdims=True) + jnp.exp(sk - mx)
                p = p / den
                psink = jnp.exp(sk - mx) / den
                dog = do_ref[:, g * hd:(g + 1) * hd].astype(BF16)
                dp = lax.dot_general(dog, vv, (((1,), (1,)), ((), ())), preferred_element_type=F32)
                rsum = jnp.sum(p * dp, axis=-1, keepdims=True)
                ds = (p * (dp - rsum) * scale).astype(BF16)
                dsk = dsk + jnp.where(lane1 == h * grp + g, jnp.sum(-psink * rsum, axis=0, keepdims=True), 0.0)
                dqn = jnp.dot(ds, kk, preferred_element_type=F32)
                dkk = dkk + lax.dot_general(ds, q, (((0,), (0,)), ((), ())), preferred_element_type=F32)
                dvv = dvv + lax.dot_general(p.astype(BF16), dog, (((0,), (0,)), ((), ())), preferred_element_type=F32)
                dxq, dg_q = _norm_bwd(rope_bwd(dqn, cc, sc), xh_q, rs_q, qgv)
                dq_ref[:, g * hd:(g + 1) * hd] = dxq
                dqg = dqg + dg_q
            dkp_raw, dg_kp = _norm_bwd(rope_bwd(dkk[:BLK], cp, sp), xh_kp, rs_kp, kgv)
            dkc_raw, dg_kc = _norm_bwd(rope_bwd(dkk[BLK:], cc, sc), xh_kc, rs_kc, kgv)
            dk_ref[...] = dk_carry[...] + dkp_raw
            dv_ref[...] = dv_carry[...] + dvv[:BLK]
            dk_carry[...] = dkc_raw
            dv_carry[...] = dvv[BLK:]
            dqg_ref[...] += dqg
            dkg_ref[...] += dg_kp + dg_kc
            dsk_ref[...] += dsk

        @pl.when(i == nb)
        def _():
            dk_ref[...] = dk_carry[...]
            dv_ref[...] = dv_carry[...]

    kvw = kv * hd
    vec = pl.BlockSpec((1, hd), lambda h, i: (0, 0))
    shifted = pl.BlockSpec((BLK, hd), lambda h, i: (jnp.maximum(i - 1, 0), h))
    return pl.pallas_call(
        body, name=name,
        out_shape=(jax.ShapeDtypeStruct((t, kv * grp * hd), F32), jax.ShapeDtypeStruct((t, kvw), F32),
                   jax.ShapeDtypeStruct((t, kvw), F32), jax.ShapeDtypeStruct((1, hd), F32),
                   jax.ShapeDtypeStruct((1, hd), F32), jax.ShapeDtypeStruct((1, LANE), F32)),
        grid=(kv, nb + 1),
        in_specs=[pl.BlockSpec(memory_space=pltpu.SMEM), zq, kc, kp, vc, vp, tc, tc, tp, tp, gs, gs,
                  pl.BlockSpec((BLK, grp * hd), lambda h, i: (jnp.minimum(i, nb - 1), h))],
        out_specs=(pl.BlockSpec((BLK, grp * hd), lambda h, i: (jnp.minimum(i, nb - 1), h)), shifted, shifted, vec, vec,
                   pl.BlockSpec((1, LANE), lambda h, i: (0, 0))),
        scratch_shapes=[pltpu.VMEM((BLK, hd), F32), pltpu.VMEM((BLK, hd), F32)], compiler_params=_params(),
    )(sinks, z, z, z, z, z, cos_t, sin_t, cos_t, sin_t, qg, kg, dattn)


def _conv_fwd(u, w, b, name):
    t, c = u.shape
    taps = w.shape[0]
    cb = _pick(c, (1408, 1024, 512, 256, 128))
    tr = _row_tile(t, cb, 256 * 1024)
    hb = tr // SUBLANE

    def body(u_ref, halo_ref, w_ref, b_ref, o_ref):
        i = pl.program_id(0)
        x = u_ref[...]
        acc = b_ref[...] + w_ref[taps - 1:taps, :] * x
        for k in range(taps - 1):
            acc = acc + w_ref[k:k + 1, :] * pltpu.roll(x, taps - 1 - k, 0)
        o_ref[...] = acc
        row = lax.broadcasted_iota(jnp.int32, (SUBLANE, cb), 0)
        hp = jnp.where(i > 0, halo_ref[...], 0.0)
        x8 = u_ref[0:SUBLANE, :]
        acc8 = b_ref[...] + w_ref[taps - 1:taps, :] * x8
        for k in range(taps - 1):
            s = taps - 1 - k
            acc8 = acc8 + w_ref[k:k + 1, :] * jnp.where(row < s, pltpu.roll(hp, s, 0), pltpu.roll(x8, s, 0))
        o_ref[0:SUBLANE, :] = acc8

    blk = pl.BlockSpec((tr, cb), lambda i, j: (i, j))
    return pl.pallas_call(
        body, name=name, out_shape=jax.ShapeDtypeStruct((t, c), F32), grid=(t // tr, c // cb),
        in_specs=[blk, pl.BlockSpec((SUBLANE, cb), lambda i, j: (jnp.maximum(i * hb - 1, 0), j)),
                  pl.BlockSpec((taps, cb), lambda i, j: (0, j)), pl.BlockSpec((1, cb), lambda i, j: (0, j))],
        out_specs=blk, compiler_params=_params(),
    )(u, u, w, b)


def _conv_bwd(duc, u, w, name):
    t, c = u.shape
    taps = w.shape[0]
    cb = _pick(c, (1408, 1024, 512, 256, 128))
    tr = _row_tile(t, cb, 256 * 1024)
    hb, nt = tr // SUBLANE, t // tr

    def body(g_ref, gnext_ref, u_ref, uprev_ref, w_ref, du_ref, dw_ref, db_ref):
        i = pl.program_id(1)

        @pl.when(i == 0)
        def _():
            dw_ref[...] = jnp.zeros_like(dw_ref)
            db_ref[...] = jnp.zeros_like(db_ref)

        row = lax.broadcasted_iota(jnp.int32, (SUBLANE, cb), 0)
        g, x = g_ref[...], u_ref[...]
        du = w_ref[taps - 1:taps, :] * g
        for k in range(taps - 1):
            du = du + w_ref[k:k + 1, :] * pltpu.roll(g, tr - (taps - 1 - k), 0)
        du_ref[...] = du
        hn = jnp.where(i < nt - 1, gnext_ref[...], 0.0)
        g8 = g_ref[tr - SUBLANE:tr, :]
        du8 = w_ref[taps - 1:taps, :] * g8
        for k in range(taps - 1):
            s = taps - 1 - k
            du8 = du8 + w_ref[k:k + 1, :] * jnp.where(row >= SUBLANE - s, pltpu.roll(hn, SUBLANE - s, 0),
                                                     pltpu.roll(g8, SUBLANE - s, 0))
        du_ref[tr - SUBLANE:tr, :] = du8

        hp = jnp.where(i > 0, uprev_ref[...], 0.0)
        xl8, gf8 = u_ref[tr - SUBLANE:tr, :], g_ref[0:SUBLANE, :]
        db_ref[...] += jnp.sum(g, axis=0, keepdims=True)
        dw_ref[taps - 1:taps, :] += jnp.sum(g * x, axis=0, keepdims=True)
        for k in range(taps - 1):
            s = taps - 1 - k
            fix = jnp.where(row < s, pltpu.roll(hp, s, 0) - pltpu.roll(xl8, s, 0), 0.0)
            dw_ref[k:k + 1, :] += (jnp.sum(g * pltpu.roll(x, s, 0), axis=0, keepdims=True)
                                   + jnp.sum(gf8 * fix, axis=0, keepdims=True))

    blk = pl.BlockSpec((tr, cb), lambda j, i: (i, j))
    nh = t // SUBLANE
    return pl.pallas_call(
        body, name=name,
        out_shape=(jax.ShapeDtypeStruct((t, c), F32), jax.ShapeDtypeStruct((taps, c), F32),
                   jax.ShapeDtypeStruct((1, c), F32)),
        grid=(c // cb, nt),
        in_specs=[blk, pl.BlockSpec((SUBLANE, cb), lambda j, i: (jnp.minimum((i + 1) * hb, nh - 1), j)),
                  blk, pl.BlockSpec((SUBLANE, cb), lambda j, i: (jnp.maximum(i * hb - 1, 0), j)),
                  pl.BlockSpec((taps, cb), lambda j, i: (0, j))],
        out_specs=(blk, pl.BlockSpec((taps, cb), lambda j, i: (0, j)), pl.BlockSpec((1, cb), lambda j, i: (0, j))),
        compiler_params=_params(),
    )(duc, duc, u, u, w)


def _dense_groups(w, gw):
    n, bw, _ = w.shape
    per = gw // bw
    out = jnp.zeros((n // per, gw, gw), BF16)
    for b in range(n):
        o = (b % per) * bw
        out = lax.dynamic_update_slice(out, w[b].astype(BF16)[None], (b // per, o, o))
    return out


def _diag_blocks(dense, n, bw):
    gw = dense.shape[-1]
    per = gw // bw
    return jnp.stack([dense[b // per, (b % per) * bw:(b % per + 1) * bw, (b % per) * bw:(b % per + 1) * bw]
                      for b in range(n)])


def _gates_fwd(uc, wr, wi, name):
    t, c = uc.shape
    ng, gw, _ = wr.shape
    tr = _pick(t, (512, 256, 128))

    def body(u_ref, wr_ref, wi_ref, r_ref, i_ref):
        a = u_ref[...].astype(BF16)
        r_ref[...] = jnp.dot(a, wr_ref[...], preferred_element_type=F32)
        i_ref[...] = jnp.dot(a, wi_ref[...], preferred_element_type=F32)

    blk = pl.BlockSpec((tr, gw), lambda h, i: (i, h))
    wsp = pl.BlockSpec((None, gw, gw), lambda h, i: (h, 0, 0))
    shp = jax.ShapeDtypeStruct((t, c), F32)
    return pl.pallas_call(body, name=name, out_shape=(shp, shp), grid=(ng, t // tr), in_specs=[blk, wsp, wsp],
                          out_specs=(blk, blk), compiler_params=_params())(uc, wr, wi)


def _gates_bwd_x(duc, drp, dip, wr, wi, name):
    t, c = duc.shape
    ng, gw, _ = wr.shape
    tr = _pick(t, (512, 256, 128))
    dims = (((1,), (1,)), ((), ()))

    def body(d_ref, r_ref, i_ref, wr_ref, wi_ref, o_ref):
        o_ref[...] = (d_ref[...]
                      + lax.dot_general(r_ref[...].astype(BF16), wr_ref[...], dims, preferred_element_type=F32)
                      + lax.dot_general(i_ref[...].astype(BF16), wi_ref[...], dims, preferred_element_type=F32))

    blk = pl.BlockSpec((tr, gw), lambda h, i: (i, h))
    wsp = pl.BlockSpec((None, gw, gw), lambda h, i: (h, 0, 0))
    return pl.pallas_call(body, name=name, out_shape=jax.ShapeDtypeStruct((t, c), F32), grid=(ng, t // tr),
                          in_specs=[blk, blk, blk, wsp, wsp], out_specs=blk, compiler_params=_params())(duc, drp, dip, wr, wi)


def _gates_bwd_w(uc, dpre, ng, gw, name):
    t, c = uc.shape
    tk = _pick(t, (512, 256, 128))
    dims = (((0,), (0,)), ((), ()))

    def body(u_ref, d_ref, o_ref):
        @pl.when(pl.program_id(1) == 0)
        def _():
            o_ref[...] = jnp.zeros_like(o_ref)

        o_ref[...] += lax.dot_general(u_ref[...].astype(BF16), d_ref[...].astype(BF16), dims, preferred_element_type=F32)

    blk = pl.BlockSpec((tk, gw), lambda h, i: (i, h))
    return pl.pallas_call(body, name=name, out_shape=jax.ShapeDtypeStruct((ng, gw, gw), F32), grid=(ng, t // tk),
                          in_specs=[blk, blk], out_specs=pl.BlockSpec((None, gw, gw), lambda h, i: (h, 0, 0)),
                          compiler_params=_params())(uc, dpre)


def _softplus(x):
    return jnp.maximum(x, 0.0) + jnp.log(1.0 + jnp.exp(-jnp.abs(x)))


def _neg_expm1(x):
    series = x * (1.0 + x * (0.5 + x * (1.0 / 6.0 + x * (1.0 / 24.0 + x * (1.0 / 120.0)))))
    return -jnp.where(x > -0.05, series, jnp.exp(x) - 1.0)


_GELU_C = math.sqrt(2.0 / math.pi)


def _gelu_parts(x):
    inner = _GELU_C * (x + 0.044715 * (x * x * x))
    th = jnp.tanh(inner)
    gelu = 0.5 * x * (1.0 + th)
    dgelu = 0.5 * (1.0 + th) + 0.5 * x * (1.0 - th * th) * (_GELU_C * (1.0 + 3.0 * 0.044715 * (x * x)))
    return gelu, dgelu


def _lru_gate_values(uc, rpre, ipre, br, bi, sp):
    r = _sigmoid(rpre + br)
    ig = _sigmoid(ipre + bi)
    a = jnp.exp(-LRU_C * r * sp)
    mult = jnp.sqrt(jnp.maximum(_neg_expm1(2.0 * (-LRU_C * r * sp)), 0.0))
    return r, ig, a, mult


def _lru_fwd(uc, rpre, ipre, gr, br, bi, lam, name):
    t, c = uc.shape
    cb = _pick(c, (1408, 1024, 512, 256, 128))
    tb = _pick(t, (512, 256, 128))
    ntile = tb // SUBLANE

    def body(uc_ref, r_ref, i_ref, gr_ref, br_ref, bi_ref, lam_ref, h_ref, rec_ref, carry):
        @pl.when(pl.program_id(1) == 0)
        def _():
            carry[...] = jnp.zeros_like(carry)

        sp = _softplus(-lam_ref[...])
        br, bi = br_ref[...], bi_ref[...]
        row = lax.broadcasted_iota(jnp.int32, (SUBLANE, cb), 0)

        def tile(k, c_in):
            sl = pl.ds(pl.multiple_of(k * SUBLANE, SUBLANE), SUBLANE)
            ucv = uc_ref[sl, :]
            _, ig, a, mult = _lru_gate_values(ucv, r_ref[sl, :], i_ref[sl, :], br, bi, sp)
            b = mult * (ig * ucv)
            for d in (1, 2, 4):
                a_s = jnp.where(row >= d, pltpu.roll(a, d, 0), 1.0)
                b_s = jnp.where(row >= d, pltpu.roll(b, d, 0), 0.0)
                b = a * b_s + b
                a = a * a_s
            hv = b + a * c_in
            h_ref[sl, :] = hv
            rec_ref[sl, :] = hv * _gelu_parts(gr_ref[sl, :])[0]
            return hv[SUBLANE - 1:SUBLANE, :]

        c_out = lax.fori_loop(0, ntile, tile, carry[0:1, :])
        carry[...] = jnp.broadcast_to(c_out, (SUBLANE, cb))

    blk = pl.BlockSpec((tb, cb), lambda j, i: (i, j))
    vec = pl.BlockSpec((1, cb), lambda j, i: (0, j))
    shp = jax.ShapeDtypeStruct((t, c), F32)
    return pl.pallas_call(body, name=name, out_shape=(shp, shp), grid=(c // cb, t // tb),
                          in_specs=[blk, blk, blk, blk, vec, vec, vec], out_specs=(blk, blk),
                          scratch_shapes=[pltpu.VMEM((SUBLANE, cb), F32)], compiler_params=_params(),
                          )(uc, rpre, ipre, gr, br, bi, lam)


def _lru_bwd(drec, hst, uc, rpre, ipre, gr, br, bi, lam, name):
    t, c = uc.shape
    cb = _pick(c, (1408, 1024, 512, 256, 128))
    tb = _pick(t, (256, 128))
    ntile, nt, hb = tb // SUBLANE, t // tb, tb // SUBLANE

    def body(drec_ref, h_ref, hprev_ref, uc_ref, r_ref, i_ref, gr_ref, br_ref, bi_ref, lam_ref,
             dgr_ref, drp_ref, dip_ref, duc_ref, dlam_ref, dbr_ref, dbi_ref, carry):
        step = pl.program_id(1)
        first_block = step == nt - 1

        @pl.when(step == 0)
        def _():
            carry[...] = jnp.zeros_like(carry)
            dlam_ref[...] = jnp.zeros_like(dlam_ref)
            dbr_ref[...] = jnp.zeros_like(dbr_ref)
            dbi_ref[...] = jnp.zeros_like(dbi_ref)

        lam = lam_ref[...]
        sp = _softplus(-lam)
        br, bi = br_ref[...], bi_ref[...]
        row = lax.broadcasted_iota(jnp.int32, (SUBLANE, cb), 0)
        halo = jnp.where(first_block, 0.0, hprev_ref[...])

        def tile(kk, state):
            c_p, acc_sp, acc_br, acc_bi = state
            k = ntile - 1 - kk
            sl = pl.ds(pl.multiple_of(k * SUBLANE, SUBLANE), SUBLANE)
            slp = pl.ds(pl.multiple_of(jnp.maximum(k - 1, 0) * SUBLANE, SUBLANE), SUBLANE)
            ucv = uc_ref[sl, :]
            r, ig, a, mult = _lru_gate_values(ucv, r_ref[sl, :], i_ref[sl, :], br, bi, sp)
            hv = h_ref[sl, :]
            below = jnp.where(k > 0, h_ref[slp, :], halo)
            hprev = jnp.where(row == 0, pltpu.roll(below, 1, 0), pltpu.roll(hv, 1, 0))
            gelu, dgelu = _gelu_parts(gr_ref[sl, :])
            drec = drec_ref[sl, :]
            dh = drec * gelu
            dgr_ref[sl, :] = drec * hv * dgelu
            pa, pb = a, a * dh
            for d in (1, 2, 4):
                a_s = jnp.where(row < SUBLANE - d, pltpu.roll(pa, SUBLANE - d, 0), 1.0)
                b_s = jnp.where(row < SUBLANE - d, pltpu.roll(pb, SUBLANE - d, 0), 0.0)
                pb = pa * b_s + pb
                pa = pa * a_s
            pv = pb + pa * c_p
            gt = dh + jnp.where(row == SUBLANE - 1, c_p, pltpu.roll(pv, SUBLANE - 1, 0))
            da = gt * hprev
            duc_ref[sl, :] = gt * mult * ig
            dmult = gt * ig * ucv
            dig = gt * mult * ucv
            dla = da * a - jnp.where(mult > 0.0, dmult * (a * a) / mult, 0.0)
            drp = dla * (-LRU_C * sp) * (r * (1.0 - r))
            dip = dig * (ig * (1.0 - ig))
            drp_ref[sl, :] = drp
            dip_ref[sl, :] = dip
            return pv[0:1, :], acc_sp + dla * (-LRU_C * r), acc_br + drp, acc_bi + dip

        zero = jnp.zeros((SUBLANE, cb), F32)
        c_out, acc_sp, acc_br, acc_bi = lax.fori_loop(0, ntile, tile, (carry[0:1, :], zero, zero, zero))
        carry[...] = jnp.broadcast_to(c_out, (SUBLANE, cb))
        dlam_ref[...] += jnp.sum(acc_sp, axis=0, keepdims=True) * (-_sigmoid(-lam))
        dbr_ref[...] += jnp.sum(acc_br, axis=0, keepdims=True)
        dbi_ref[...] += jnp.sum(acc_bi, axis=0, keepdims=True)

    blk = pl.BlockSpec((tb, cb), lambda j, i: (nt - 1 - i, j))
    vec = pl.BlockSpec((1, cb), lambda j, i: (0, j))
    halo_spec = pl.BlockSpec((SUBLANE, cb), lambda j, i: (jnp.maximum((nt - 1 - i) * hb - 1, 0), j))
    big, small = jax.ShapeDtypeStruct((t, c), F32), jax.ShapeDtypeStruct((1, c), F32)
    return pl.pallas_call(
        body, name=name, out_shape=(big, big, big, big, small, small, small), grid=(c // cb, nt),
        in_specs=[blk, blk, halo_spec, blk, blk, blk, blk, vec, vec, vec],
        out_specs=(blk, blk, blk, blk, vec, vec, vec),
        scratch_shapes=[pltpu.VMEM((SUBLANE, cb), F32)], compiler_params=_params(),
    )(drec, hst, hst, uc, rpre, ipre, gr, br, bi, lam)


ANY = pl.BlockSpec(memory_space=pl.ANY)


def _place():
    x, y, c = lax.axis_index("x"), lax.axis_index("y"), lax.axis_index("c")
    other_chips = [(1 - x, y), (x, 1 - y), (1 - x, 1 - y)]
    return x, y, c, 2 * x + y, other_chips


def _drain(copies):
    for cp in copies:
        if cp.is_remote:
            cp.wait_send()
        else:
            cp.wait()


def _shard_region(ref, kind, chip, half, rh, width):
    if kind == "col":
        return ref.at[pl.ds(half * rh, rh), pl.ds(chip * width, width)]
    return ref.at[pl.ds(chip * (2 * rh) + half * rh, rh), :]


def _allgather_weights(shards, kinds, name):
    n = len(shards)
    fulls = []
    for s, kind in zip(shards, kinds):
        r, w = s.shape
        fulls.append(jax.ShapeDtypeStruct((r, N_CHIPS * w) if kind == "col" else (N_CHIPS * r, w), s.dtype))

    def body(*refs):
        s_refs, f_refs = refs[:n], refs[n:2 * n]
        send_sems, recv_sems, loc_sems = refs[2 * n:]
        x, y, c, me, chips = _place()
        sibling = (x, y, 1 - c)
        geo = [(s.shape[0] // 2, s.shape[1]) for s in shards]

        def copy(a, k, src_chip, half, to):
            rh, w = geo[a]
            region = _shard_region(f_refs[a], kinds[a], src_chip, half, rh, w)
            src = s_refs[a].at[pl.ds(half * rh, rh), :] if k < 3 else region
            return pltpu.make_async_remote_copy(src_ref=src, dst_ref=region, send_sem=send_sems.at[a * 6 + k],
                                                recv_sem=recv_sems.at[a * 6 + k], device_id=to, device_id_type=MESH)

        started = []
        for a in range(n):
            rh, w = geo[a]
            if kinds[a] == "col":
                own = f_refs[a].at[:, pl.ds(me * w, w)]
            else:
                own = f_refs[a].at[pl.ds(me * 2 * rh, 2 * rh), :]
            loc = pltpu.make_async_copy(s_refs[a], own, loc_sems.at[a])
            loc.start()
            started.append(loc)
            for k, (cx, cy) in enumerate(chips):
                cp = copy(a, k, me, c, (cx, cy, c))
                cp.start()
                started.append(cp)
        for a in range(n):
            for k, (cx, cy) in enumerate(chips):
                copy(a, k, 2 * cx + cy, c, sibling).wait_recv()
                fwd = copy(a, 3 + k, 2 * cx + cy, c, sibling)
                fwd.start()
                started.append(fwd)
        for a in range(n):
            for k, (cx, cy) in enumerate(chips):
                copy(a, 3 + k, 2 * cx + cy, 1 - c, sibling).wait_recv()
        _drain(started)

    return pl.pallas_call(
        body, name=name, out_shape=tuple(fulls), in_specs=[ANY] * n, out_specs=tuple([ANY] * n),
        scratch_shapes=[pltpu.SemaphoreType.DMA((6 * n,)), pltpu.SemaphoreType.DMA((6 * n,)),
                        pltpu.SemaphoreType.DMA((n,))],
    )(*shards)


def _sibling_exchange(grads, name):
    n = len(grads)

    def body(*refs):
        g_refs, o_refs = refs[:n], refs[n:2 * n]
        send_sems, recv_sems = refs[2 * n:]
        x, y, c, _, _ = _place()
        copies = []
        for a in range(n):
            cp = pltpu.make_async_remote_copy(src_ref=g_refs[a].at[:, 1 - c], dst_ref=o_refs[a],
                                              send_sem=send_sems.at[a], recv_sem=recv_sems.at[a],
                                              device_id=(x, y, 1 - c), device_id_type=MESH)
            cp.start()
            copies.append(cp)
        for cp in copies:
            cp.wait()

    outs = tuple(jax.ShapeDtypeStruct((g.shape[0],) + g.shape[2:], g.dtype) for g in grads)
    return pl.pallas_call(body, name=name, out_shape=outs, in_specs=[ANY] * n, out_specs=tuple([ANY] * n),
                          scratch_shapes=[pltpu.SemaphoreType.DMA((n,)), pltpu.SemaphoreType.DMA((n,))])(*grads)


def _add_own_half(g4, recv, c_idx, name):
    p, _, rh, n = g4.shape
    tc = _pick(n, (1280, 1408, 1024, 512, 256, 128))
    tr = _row_tile(rh, tc, 256 * 1024)

    def body(c_ref, g_ref, r_ref, o_ref):
        o_ref[...] = g_ref[...] + r_ref[...]

    return pl.pallas_call(
        body, name=name, out_shape=jax.ShapeDtypeStruct((p, rh, n), F32),
        grid_spec=pltpu.PrefetchScalarGridSpec(
            num_scalar_prefetch=1, grid=(p, rh // tr, n // tc),
            in_specs=[pl.BlockSpec((None, None, tr, tc), lambda q, i, j, c_ref: (q, c_ref[0], i, j)),
                      pl.BlockSpec((None, tr, tc), lambda q, i, j, c_ref: (q, i, j))],
            out_specs=pl.BlockSpec((None, tr, tc), lambda q, i, j, c_ref: (q, i, j))),
        compiler_params=_params(),
    )(c_idx, g4, recv)


def _piece(ref, kind, chip, width):
    if kind == "col":
        return ref.at[0, :, pl.ds(chip * width, width)]
    return ref.at[chip]


def _chip_exchange(sums, kinds, name):
    n = len(sums)
    widths = [s.shape[2] // N_CHIPS if k == "col" else s.shape[2] for s, k in zip(sums, kinds)]

    def body(*refs):
        s_refs, o_refs = refs[:n], refs[n:2 * n]
        send_sems, recv_sems, loc_sems = refs[2 * n:]
        x, y, c, me, chips = _place()
        copies = []
        for a in range(n):
            loc = pltpu.make_async_copy(_piece(s_refs[a], kinds[a], me, widths[a]), o_refs[a].at[me], loc_sems.at[a])
            loc.start()
            copies.append(loc)
            for k, (cx, cy) in enumerate(chips):
                cp = pltpu.make_async_remote_copy(
                    src_ref=_piece(s_refs[a], kinds[a], 2 * cx + cy, widths[a]), dst_ref=o_refs[a].at[me],
                    send_sem=send_sems.at[a * 3 + k], recv_sem=recv_sems.at[a * 3 + k],
                    device_id=(cx, cy, c), device_id_type=MESH)
                cp.start()
                copies.append(cp)
        for a in range(n):
            for k, (cx, cy) in enumerate(chips):
                pltpu.make_async_remote_copy(
                    src_ref=_piece(s_refs[a], kinds[a], me, widths[a]), dst_ref=o_refs[a].at[2 * cx + cy],
                    send_sem=send_sems.at[a * 3 + k], recv_sem=recv_sems.at[a * 3 + k],
                    device_id=(cx, cy, c), device_id_type=MESH).wait_recv()
        _drain(copies)

    outs = tuple(jax.ShapeDtypeStruct((N_CHIPS, s.shape[1], w), s.dtype) for s, w in zip(sums, widths))
    return pl.pallas_call(body, name=name, out_shape=outs, in_specs=[ANY] * n, out_specs=tuple([ANY] * n),
                          scratch_shapes=[pltpu.SemaphoreType.DMA((3 * n,)), pltpu.SemaphoreType.DMA((3 * n,)),
                                          pltpu.SemaphoreType.DMA((n,))])(*sums)


def _sum_chips(parts, name):
    _, rh, w = parts.shape
    tc = _pick(w, (1280, 1408, 1024, 512, 256, 128))
    tr = _row_tile(rh, tc, 128 * 1024)

    def body(p0, p1, p2, p3, o_ref):
        o_ref[...] = ((p0[...] + p1[...]) + p2[...]) + p3[...]

    def spec(s):
        return pl.BlockSpec((None, tr, tc), lambda i, j: (s, i, j))

    return pl.pallas_call(body, name=name, out_shape=jax.ShapeDtypeStruct((rh, w), F32), grid=(rh // tr, w // tc),
                          in_specs=[spec(s) for s in range(N_CHIPS)], out_specs=pl.BlockSpec((tr, tc), lambda i, j: (i, j)),
                          compiler_params=_params())(parts, parts, parts, parts)


def _finish_exchange(halves, name):
    n = len(halves)

    def body(*refs):
        h_refs, o_refs = refs[:n], refs[n:2 * n]
        send_sems, recv_sems, loc_sems, bsend_sems, brecv_sems = refs[2 * n:]
        x, y, c, me, chips = _place()
        copies = []
        for a in range(n - 1):
            rh = halves[a].shape[0]
            loc = pltpu.make_async_copy(h_refs[a], o_refs[a].at[pl.ds(c * rh, rh), :], loc_sems.at[a])
            cp = pltpu.make_async_remote_copy(src_ref=h_refs[a], dst_ref=o_refs[a].at[pl.ds(c * rh, rh), :],
                                              send_sem=send_sems.at[a], recv_sem=recv_sems.at[a],
                                              device_id=(x, y, 1 - c), device_id_type=MESH)
            loc.start()
            cp.start()
            copies += [loc, cp]
        rh = halves[n - 1].shape[0]
        h_ref, o_ref = h_refs[n - 1], o_refs[n - 1]

        def rows(chip, half):
            return o_ref.at[pl.ds((chip * 2 + half) * rh, rh), :]

        loc = pltpu.make_async_copy(h_ref, rows(me, c), loc_sems.at[n - 1])
        loc.start()
        copies.append(loc)
        rel = [(fx, fy, fc) for fx in (0, 1) for fy in (0, 1) for fc in (0, 1)][1:]
        for r, (fx, fy, fc) in enumerate(rel):
            cp = pltpu.make_async_remote_copy(
                src_ref=h_ref, dst_ref=rows(me, c), send_sem=bsend_sems.at[r], recv_sem=brecv_sems.at[r],
                device_id=(1 - x if fx else x, 1 - y if fy else y, 1 - c if fc else c), device_id_type=MESH)
            cp.start()
            copies.append(cp)
        for a in range(n - 1):
            rh_a = halves[a].shape[0]
            pltpu.make_async_remote_copy(src_ref=h_refs[a], dst_ref=o_refs[a].at[pl.ds((1 - c) * rh_a, rh_a), :],
                                         send_sem=send_sems.at[a], recv_sem=recv_sems.at[a],
                                         device_id=(x, y, 1 - c), device_id_type=MESH).wait_recv()
        for r, (fx, fy, fc) in enumerate(rel):
            px, py, pc = (1 - x if fx else x), (1 - y if fy else y), (1 - c if fc else c)
            pltpu.make_async_remote_copy(src_ref=h_ref, dst_ref=rows(2 * px + py, pc), send_sem=bsend_sems.at[r],
                                         recv_sem=brecv_sems.at[r], device_id=(px, py, pc),
                                         device_id_type=MESH).wait_recv()
        _drain(copies)

    outs = [jax.ShapeDtypeStruct((2 * h.shape[0], h.shape[1]), h.dtype) for h in halves[:-1]]
    outs.append(jax.ShapeDtypeStruct((2 * N_CHIPS * halves[-1].shape[0], halves[-1].shape[1]), halves[-1].dtype))
    return pl.pallas_call(body, name=name, out_shape=tuple(outs), in_specs=[ANY] * n, out_specs=tuple([ANY] * n),
                          scratch_shapes=[pltpu.SemaphoreType.DMA((n - 1,)), pltpu.SemaphoreType.DMA((n - 1,)),
                                          pltpu.SemaphoreType.DMA((n,)), pltpu.SemaphoreType.DMA((7,)),
                                          pltpu.SemaphoreType.DMA((7,))])(*halves)


def _pack(arrays, rows):
    flat = jnp.concatenate([a.reshape(-1) for a in arrays])
    return jnp.pad(flat, (0, rows * SMALL_PACK_COLS - flat.shape[0])).reshape(rows, SMALL_PACK_COLS)


def _unpack(packed, shapes):
    flat = packed.reshape(-1)
    out, o = [], 0
    for shp in shapes:
        size = math.prod(shp)
        out.append(flat[o:o + size].reshape(shp))
        o += size
    return out


def _pack_rows(shapes):
    total = sum(math.prod(s) for s in shapes)
    unit = SMALL_PACK_COLS * N_CHIPS * 2 * SUBLANE
    return -(-total // unit) * (N_CHIPS * 2 * SUBLANE)


BIG = ("w_in", "w_attn_proj", "w_lru_proj", "w_out", "w_ffn_gate", "w_ffn_up", "w_ffn_down")
BIG_KIND = {"w_in": "col", "w_attn_proj": "row", "w_lru_proj": "row", "w_out": "row", "w_ffn_gate": "col",
            "w_ffn_up": "col", "w_ffn_down": "row"}
SMALL = ("norm1_g", "b_gates", "q_norm_g", "k_norm_g", "sinks", "conv_w", "conv_b", "w_rgate", "b_rgate",
         "w_igate", "b_igate", "lru_lambda", "norm2_g")
WEIGHTS = ("norm1_g", "w_in", "b_gates", "q_norm_g", "k_norm_g", "sinks", "conv_w", "conv_b", "w_rgate", "b_rgate",
           "w_igate", "b_igate", "lru_lambda", "w_attn_proj", "w_lru_proj", "w_out", "norm2_g", "w_ffn_gate",
           "w_ffn_up", "w_ffn_down")


def kernel(x, positions, norm1_g, w_in, b_gates, q_norm_g, k_norm_g, sinks, conv_w, conv_b, w_rgate, b_rgate, w_igate, b_igate, lru_lambda, w_attn_proj, w_lru_proj, w_out, norm2_g, w_ffn_gate, w_ffn_up, w_ffn_down, loss_target, m_norm1_g, m_w_in, m_b_gates, m_q_norm_g, m_k_norm_g, m_sinks, m_conv_w, m_conv_b, m_w_rgate, m_b_rgate, m_w_igate, m_b_igate, m_lru_lambda, m_w_attn_proj, m_w_lru_proj, m_w_out, m_norm2_g, m_w_ffn_gate, m_w_ffn_up, m_w_ffn_down, v_norm1_g, v_w_in, v_b_gates, v_q_norm_g, v_k_norm_g, v_sinks, v_conv_w, v_conv_b, v_w_rgate, v_b_rgate, v_w_igate, v_b_igate, v_lru_lambda, v_w_attn_proj, v_w_lru_proj, v_w_out, v_norm2_g, v_w_ffn_gate, v_w_ffn_up, v_w_ffn_down):
    args = dict(locals())
    w = {n: args[n] for n in WEIGHTS}
    mom = {n: args["m_" + n] for n in WEIGHTS}
    var = {n: args["v_" + n] for n in WEIGHTS}

    t, d = x.shape[1], x.shape[2]
    hd = q_norm_g.shape[-1]
    nq = sinks.shape[-1]
    q_w = nq * hd
    d_rnn = conv_b.shape[-1]
    taps = conv_w.shape[1]
    n_blocks, bw = w_rgate.shape[1], w_rgate.shape[2]
    in_w = w_in.shape[-1] * N_CHIPS
    kv_w = (in_w - q_w - 2 * d_rnn - 2 * d) // 2
    kv = kv_w // hd
    grp = nq // kv
    u_off = q_w + 2 * kv_w
    gr_off = u_off + d_rnn
    ga_off = gr_off + d_rnn
    gw = bw * LANE // math.gcd(bw, LANE)
    ng = d_rnn // gw
    c_idx = lax.axis_index("c").astype(jnp.int32).reshape(1)
    chip = 2 * lax.axis_index("x") + lax.axis_index("y")

    x2, tgt = x[0], loss_target[0]

    shards = [_cast_bf16(w[n][0], "cast_" + n) for n in BIG]
    kinds = [BIG_KIND[n] for n in BIG]
    win_f, wap_f, wlp_f, wout_f, wg_f, wu_f, wd_f = _allgather_weights(shards, kinds, "allgather_weights")
    conv_w_full = _gather_small(conv_w[0], "allgather_conv_w")
    conv_w_full = jnp.transpose(conv_w_full, (1, 0, 2)).reshape(taps, d_rnn)
    wr_dense = _dense_groups(w_rgate[0], gw)
    wi_dense = _dense_groups(w_igate[0], gw)

    inv_freq = ROPE_THETA ** (-jnp.arange(0, hd // 4, 2, dtype=F32) / (hd // 4))
    ang = positions[0].astype(F32)[:, None] * inv_freq
    cos, sin = jnp.cos(ang), jnp.sin(ang)
    rest = hd - 2 * cos.shape[1]
    cos_t = jnp.concatenate([cos, cos, jnp.ones((t, rest), F32)], axis=1)
    sin_t = jnp.concatenate([-sin, sin, jnp.zeros((t, rest), F32)], axis=1)
    sinks1 = sinks[0]

    xn = _rms_fwd(x2, norm1_g, "rms1_fwd")
    z = _matmul(xn, win_f, "nn", "in_proj")
    zu, zgr = z[:, u_off:u_off + d_rnn], z[:, gr_off:gr_off + d_rnn]
    attn = _attn_fwd(z, cos_t, sin_t, q_norm_g, k_norm_g, sinks1, kv, grp, hd, "attn_fwd")
    uc = _conv_fwd(zu, conv_w_full, conv_b, "conv_fwd")
    rpre, ipre = _gates_fwd(uc, wr_dense, wi_dense, "gates_fwd")
    hst, rec = _lru_fwd(uc, rpre, ipre, zgr, b_rgate, b_igate, lru_lambda, "lru_fwd")
    pa = _matmul(attn, wap_f, "nn", "attn_proj")
    plru = _matmul(rec, wlp_f, "nn", "lru_proj")
    merged = _merge_fwd(z, b_gates, pa, plru, ga_off, "merge_fwd")
    h1 = _matmul(merged, wout_f, "nn", "out_proj", add=x2)
    hn = _rms_fwd(h1, norm2_g, "rms2_fwd")
    gate = _matmul(hn, wg_f, "nn", "ffn_gate")
    up = _matmul(hn, wu_f, "nn", "ffn_up")
    act = _swiglu_fwd(gate, up, "swiglu_fwd")
    yout = _matmul(act, wd_f, "nn", "ffn_down", add=h1)
    dy, loss_part = _loss_head(yout, tgt, "loss_head")
    loss = lax.psum(loss_part[0, 0], ("x", "y", "c"))

    g_wd = _matmul(act, dy, "tn", "d_w_ffn_down")
    dact = _matmul(dy, wd_f, "nt", "d_act")
    dgate, dup = _swiglu_bwd(dact, gate, up, "swiglu_bwd")
    g_wg = _matmul(hn, dgate, "tn", "d_w_ffn_gate")
    g_wu = _matmul(hn, dup, "tn", "d_w_ffn_up")
    dhn = _matmul(dgate, wg_f, "nt", "d_hn_gate")
    dhn = _matmul(dup, wu_f, "nt", "d_hn_up", add=dhn)
    dh1, g_norm2 = _rms_bwd(dhn, h1, norm2_g, dy, "rms2_bwd")
    g_wout = _matmul(merged, dh1, "tn", "d_w_out")
    dmerged = _matmul(dh1, wout_f, "nt", "d_merged")
    dpa, dpl, dga, dgl, g_ba, g_bl = _merge_bwd(dmerged, z, b_gates, pa, plru, ga_off, "merge_bwd")
    g_wap = _matmul(attn, dpa, "tn", "d_w_attn_proj")
    dattn = _matmul(dpa, wap_f, "nt", "d_attn")
    g_wlp = _matmul(rec, dpl, "tn", "d_w_lru_proj")
    drec = _matmul(dpl, wlp_f, "nt", "d_rec")
    dgr, drp, dip, duc_direct, g_lam, g_br, g_bi = _lru_bwd(drec, hst, uc, rpre, ipre, zgr, b_rgate, b_igate,
                                                             lru_lambda, "lru_bwd")
    duc = _gates_bwd_x(duc_direct, drp, dip, wr_dense, wi_dense, "gates_bwd_x")
    g_wr = _diag_blocks(_gates_bwd_w(uc, drp, ng, gw, "gates_bwd_wr"), n_blocks, bw)
    g_wi = _diag_blocks(_gates_bwd_w(uc, dip, ng, gw, "gates_bwd_wi"), n_blocks, bw)
    du, g_convw, g_convb = _conv_bwd(duc, zu, conv_w_full, "conv_bwd")
    dq, dk, dv, g_qg, g_kg, g_sinks = _attn_bwd(dattn, z, cos_t, sin_t, q_norm_g, k_norm_g, sinks1, kv, grp, hd,
                                                 "attn_bwd")
    dz = jnp.concatenate([dq, dk, dv, du, dgr, dga, dgl], axis=1).astype(BF16)
    g_win = _matmul(xn, dz, "tn", "d_w_in")
    dxn = _matmul(dz, win_f, "nt", "d_xn")
    dx, g_norm1 = _rms_bwd(dxn, x2, norm1_g, dh1, "rms1_bwd")

    small_grads = {"norm1_g": g_norm1, "b_gates": jnp.concatenate([g_ba, g_bl], axis=1), "q_norm_g": g_qg,
                   "k_norm_g": g_kg, "sinks": g_sinks[:, :nq], "conv_w": g_convw, "conv_b": g_convb,
                   "w_rgate": g_wr, "b_rgate": g_br, "w_igate": g_wi, "b_igate": g_bi, "lru_lambda": g_lam,
                   "norm2_g": g_norm2}
    gshapes = [small_grads[n].shape for n in SMALL]
    grows = _pack_rows(gshapes)
    packed_g = _pack([small_grads[n] for n in SMALL], grows)
    partial = {"w_in": g_win, "w_attn_proj": g_wap, "w_lru_proj": g_wlp, "w_out": g_wout, "w_ffn_gate": g_wg,
               "w_ffn_up": g_wu, "w_ffn_down": g_wd}
    names = list(BIG) + ["small"]
    kinds8 = kinds + ["row"]
    views = []
    for n, kind in zip(names, kinds8):
        g = packed_g if n == "small" else partial[n]
        r, c = g.shape
        views.append(g.reshape(1, 2, r // 2, c) if kind == "col" else g.reshape(N_CHIPS, 2, r // (2 * N_CHIPS), c))
    recv = _sibling_exchange(views, "grad_sibling_exchange")
    sums = [_add_own_half(v, rcv, c_idx, "grad_chip_sum_" + n) for v, rcv, n in zip(views, recv, names)]
    parts = _chip_exchange(sums, kinds8, "grad_chip_exchange")
    halves = [_sum_chips(p, "grad_total_" + n) for p, n in zip(parts, names)]
    reduced = _finish_exchange(halves, "grad_finish_exchange")
    grads = dict(zip(BIG, reduced[:-1]))
    small_full = dict(zip(SMALL, _unpack(reduced[-1], gshapes)))
    per = d_rnn // N_CHIPS
    small_full["conv_w"] = lax.dynamic_slice(small_full["conv_w"], (0, chip * per), (taps, per))
    for n in SMALL:
        grads[n] = small_full[n]

    delta, new_m, new_v = {}, {}, {}
    for n in BIG:
        delta[n], new_m[n], new_v[n] = _adamw(w[n][0], grads[n], mom[n][0], var[n][0], "adamw_" + n)
    pshapes = [w[n].shape for n in SMALL]
    prows = _pack_rows(pshapes)
    pk = [_pack([src[n] for n in SMALL], prows) for src in (w, grads, mom, var)]
    for res, packed in zip((delta, new_m, new_v), _adamw(pk[0], pk[1], pk[2], pk[3], "adamw_small")):
        res.update(dict(zip(SMALL, _unpack(packed, pshapes))))

    outs = [loss, dx.reshape(x.shape)]
    for res in (grads, delta, new_m, new_v):
        outs += [res[n].reshape(w[n].shape) for n in WEIGHTS]
    return tuple(outs)


def _gather_small(shard, name):
    def body(s_ref, o_ref, send_sems, recv_sems):
        x, y, c, me, chips = _place()
        o_ref[me] = s_ref[...]
        copies = []
        for k, (cx, cy) in enumerate(chips):
            cp = pltpu.make_async_remote_copy(src_ref=s_ref, dst_ref=o_ref.at[me], send_sem=send_sems.at[k],
                                              recv_sem=recv_sems.at[k], device_id=(cx, cy, c), device_id_type=MESH)
            cp.start()
            copies.append(cp)
        for k, (cx, cy) in enumerate(chips):
            pltpu.make_async_remote_copy(src_ref=s_ref, dst_ref=o_ref.at[2 * cx + cy], send_sem=send_sems.at[k],
                                         recv_sem=recv_sems.at[k], device_id=(cx, cy, c),
                                         device_id_type=MESH).wait_recv()
        for cp in copies:
            cp.wait_send()

    vm = pl.BlockSpec(memory_space=pltpu.VMEM)
    return pl.pallas_call(body, name=name, out_shape=jax.ShapeDtypeStruct((N_CHIPS,) + shard.shape, shard.dtype),
                          in_specs=[vm], out_specs=vm,
                          scratch_shapes=[pltpu.SemaphoreType.DMA((3,)), pltpu.SemaphoreType.DMA((3,))])(shard)
```

```python
import functools
import math

import jax
import jax.numpy as jnp
from jax import lax
from jax.experimental import pallas as pl
from jax.experimental.pallas import tpu as pltpu

F32 = jnp.float32
BF16 = jnp.bfloat16
MESH = pl.DeviceIdType.MESH

WINDOW = 128
BLK = 128
ROPE_THETA = 500000.0
LRU_C = 8.0
EPS = 1e-6
NEG = -1e30
ADAM_LR = 0.001
ADAM_B1 = 0.9
ADAM_B2 = 0.999
ADAM_EPS = 1e-08
ADAM_WD = 0.01
ADAM_STEP = 10

VMEM_LIMIT_BYTES = 52 * 1024 * 1024
LANE = 128
SUBLANE = 8
N_CHIPS = 4
SMALL_PACK_COLS = 512


def _params(**kw):
    return pltpu.CompilerParams(vmem_limit_bytes=VMEM_LIMIT_BYTES, **kw)


def _pick(dim, cands):
    for c in cands:
        if dim % c == 0:
            return c
    return dim


def _sigmoid(x):
    return 1.0 / (1.0 + jnp.exp(-x))


_MN_TILES = (1024, 1408, 1280, 512, 256, 128)
_K_TILES = (1408, 1280, 1024, 512, 256, 128)


def _matmul(a, b, mode, name, add=None, out_dtype=F32):
    if mode == "nn":
        (m, k), (k2, n) = a.shape, b.shape
    elif mode == "nt":
        (m, k), (n, k2) = a.shape, b.shape
    else:
        (k, m), (k2, n) = a.shape, b.shape
    assert k == k2, (a.shape, b.shape, mode)
    tm, tn, tk = _pick(m, _MN_TILES), _pick(n, _MN_TILES), _pick(k, _K_TILES)
    nk = k // tk
    if mode == "nn":
        a_spec = pl.BlockSpec((tm, tk), lambda i, j, kk: (i, kk))
        b_spec = pl.BlockSpec((tk, tn), lambda i, j, kk: (kk, j))
        dims = (((1,), (0,)), ((), ()))
    elif mode == "nt":
        a_spec = pl.BlockSpec((tm, tk), lambda i, j, kk: (i, kk))
        b_spec = pl.BlockSpec((tn, tk), lambda i, j, kk: (j, kk))
        dims = (((1,), (1,)), ((), ()))
    else:
        a_spec = pl.BlockSpec((tk, tm), lambda i, j, kk: (kk, i))
        b_spec = pl.BlockSpec((tk, tn), lambda i, j, kk: (kk, j))
        dims = (((0,), (0,)), ((), ()))
    o_spec = pl.BlockSpec((tm, tn), lambda i, j, kk: (i, j))
    has_add = add is not None

    def body(*refs):
        if has_add:
            a_ref, b_ref, add_ref, o_ref, acc = refs
        else:
            a_ref, b_ref, o_ref, acc = refs
        kk = pl.program_id(2)

        @pl.when(kk == 0)
        def _():
            acc[...] = jnp.zeros_like(acc)

        acc[...] += lax.dot_general(a_ref[...].astype(BF16), b_ref[...].astype(BF16), dims,
                                    preferred_element_type=F32)

        @pl.when(kk == nk - 1)
        def _():
            r = acc[...]
            if has_add:
                r = r + add_ref[...]
            o_ref[...] = r.astype(out_dtype)

    in_specs = [a_spec, b_spec] + ([o_spec] if has_add else [])
    args = (a, b) + ((add,) if has_add else ())
    return pl.pallas_call(
        body, name=name, out_shape=jax.ShapeDtypeStruct((m, n), out_dtype),
        grid=(m // tm, n // tn, nk), in_specs=in_specs, out_specs=o_spec,
        scratch_shapes=[pltpu.VMEM((tm, tn), F32)], compiler_params=_params(),
    )(*args)


def _row_tile(rows, cols, budget_elems=512 * 1024):
    cands = [c for c in (1024, 704, 512, 352, 256, 128, 64, 32, 16) if c * cols <= budget_elems]
    return _pick(rows, cands or (16,))


def _rms_fwd(x, g, name):
    t, d = x.shape
    tr = _row_tile(t, d)

    def body(x_ref, g_ref, o_ref):
        xv = x_ref[...]
        rstd = lax.rsqrt(jnp.mean(xv * xv, axis=-1, keepdims=True) + EPS)
        o_ref[...] = (xv * rstd * g_ref[...]).astype(BF16)

    spec = pl.BlockSpec((tr, d), lambda i: (i, 0))
    return pl.pallas_call(body, name=name, out_shape=jax.ShapeDtypeStruct((t, d), BF16), grid=(t // tr,),
                          in_specs=[spec, pl.BlockSpec((1, d), lambda i: (0, 0))], out_specs=spec,
                          compiler_params=_params())(x, g)


def _rms_bwd(dxn, x, g, resid, name):
    t, d = x.shape
    tr = _row_tile(t, d, 256 * 1024)

    def body(dxn_ref, x_ref, g_ref, r_ref, dx_ref, dg_ref):
        @pl.when(pl.program_id(0) == 0)
        def _():
            dg_ref[...] = jnp.zeros_like(dg_ref)

        xv = x_ref[...]
        rstd = lax.rsqrt(jnp.mean(xv * xv, axis=-1, keepdims=True) + EPS)
        xhat = xv * rstd
        dy = dxn_ref[...]
        dg_ref[...] += jnp.sum(dy * xhat, axis=0, keepdims=True)
        dxhat = dy * g_ref[...]
        dx_ref[...] = r_ref[...] + rstd * (dxhat - xhat * jnp.mean(dxhat * xhat, axis=-1, keepdims=True))

    spec = pl.BlockSpec((tr, d), lambda i: (i, 0))
    vec = pl.BlockSpec((1, d), lambda i: (0, 0))
    return pl.pallas_call(
        body, name=name, out_shape=(jax.ShapeDtypeStruct((t, d), F32), jax.ShapeDtypeStruct((1, d), F32)),
        grid=(t // tr,), in_specs=[spec, spec, vec, spec], out_specs=(spec, vec), compiler_params=_params(),
    )(dxn, x, g, resid)


def _swiglu_fwd(gate, up, name):
    t, f = gate.shape
    tr = _row_tile(t, f, 256 * 1024)

    def body(g_ref, u_ref, o_ref):
        gv = g_ref[...]
        o_ref[...] = (gv * _sigmoid(gv) * u_ref[...]).astype(BF16)

    spec = pl.BlockSpec((tr, f), lambda i: (i, 0))
    return pl.pallas_call(body, name=name, out_shape=jax.ShapeDtypeStruct((t, f), BF16), grid=(t // tr,),
                          in_specs=[spec, spec], out_specs=spec, compiler_params=_params())(gate, up)


def _swiglu_bwd(dact, gate, up, name):
    t, f = gate.shape
    tr = _row_tile(t, f, 128 * 1024)

    def body(d_ref, g_ref, u_ref, dg_ref, du_ref):
        gv, dv = g_ref[...], d_ref[...]
        sg = _sigmoid(gv)
        dg_ref[...] = (dv * u_ref[...] * (sg * (1.0 + gv * (1.0 - sg)))).astype(BF16)
        du_ref[...] = (dv * (gv * sg)).astype(BF16)

    spec = pl.BlockSpec((tr, f), lambda i: (i, 0))
    shp = jax.ShapeDtypeStruct((t, f), BF16)
    return pl.pallas_call(body, name=name, out_shape=(shp, shp), grid=(t // tr,), in_specs=[spec, spec, spec],
                          out_specs=(spec, spec), compiler_params=_params())(dact, gate, up)


def _merge_fwd(z, b_gates, pa, plru, ga_off, name):
    t, d = pa.shape
    cw = _pick(math.gcd(ga_off, d), (512, 256, 128))
    tr = _row_tile(t, cw, 256 * 1024)
    oa, ol, nd = ga_off // cw, (ga_off + d) // cw, d // cw

    def body(ga_ref, gl_ref, ba_ref, bl_ref, pa_ref, pl_ref, o_ref):
        sa = _sigmoid(ga_ref[...] + ba_ref[...])
        sl = _sigmoid(gl_ref[...] + bl_ref[...])
        o_ref[...] = (sa * pa_ref[...] + sl * pl_ref[...]).astype(BF16)

    blk = pl.BlockSpec((tr, cw), lambda i, j: (i, j))
    return pl.pallas_call(
        body, name=name, out_shape=jax.ShapeDtypeStruct((t, d), BF16), grid=(t // tr, nd),
        in_specs=[pl.BlockSpec((tr, cw), lambda i, j: (i, oa + j)), pl.BlockSpec((tr, cw), lambda i, j: (i, ol + j)),
                  pl.BlockSpec((1, cw), lambda i, j: (0, j)), pl.BlockSpec((1, cw), lambda i, j: (0, nd + j)),
                  blk, blk],
        out_specs=blk, compiler_params=_params(),
    )(z, z, b_gates, b_gates, pa, plru)


def _merge_bwd(dmerged, z, b_gates, pa, plru, ga_off, name):
    t, d = pa.shape
    cw = _pick(math.gcd(ga_off, d), (512, 256, 128))
    tr = _row_tile(t, cw, 256 * 1024)
    oa, ol, nd = ga_off // cw, (ga_off + d) // cw, d // cw

    def body(dm_ref, ga_ref, gl_ref, ba_ref, bl_ref, pa_ref, pl_ref, dpa_ref, dpl_ref, dga_ref, dgl_ref, sa_ref, sl_ref):
        @pl.when(pl.program_id(1) == 0)
        def _():
            sa_ref[...] = jnp.zeros_like(sa_ref)
            sl_ref[...] = jnp.zeros_like(sl_ref)

        dm = dm_ref[...]
        sa = _sigmoid(ga_ref[...] + ba_ref[...])
        sl = _sigmoid(gl_ref[...] + bl_ref[...])
        dpa_ref[...] = (dm * sa).astype(BF16)
        dpl_ref[...] = (dm * sl).astype(BF16)
        dga = dm * pa_ref[...] * (sa * (1.0 - sa))
        dgl = dm * pl_ref[...] * (sl * (1.0 - sl))
        dga_ref[...] = dga
        dgl_ref[...] = dgl
        sa_ref[...] += jnp.sum(dga, axis=0, keepdims=True)
        sl_ref[...] += jnp.sum(dgl, axis=0, keepdims=True)

    blk = pl.BlockSpec((tr, cw), lambda j, i: (i, j))
    vec = pl.BlockSpec((1, cw), lambda j, i: (0, j))
    big16, big32, v32 = (jax.ShapeDtypeStruct((t, d), BF16), jax.ShapeDtypeStruct((t, d), F32),
                         jax.ShapeDtypeStruct((1, d), F32))
    return pl.pallas_call(
        body, name=name, out_shape=(big16, big16, big32, big32, v32, v32), grid=(nd, t // tr),
        in_specs=[blk, pl.BlockSpec((tr, cw), lambda j, i: (i, oa + j)), pl.BlockSpec((tr, cw), lambda j, i: (i, ol + j)),
                  vec, pl.BlockSpec((1, cw), lambda j, i: (0, nd + j)), blk, blk],
        out_specs=(blk, blk, blk, blk, vec, vec), compiler_params=_params(),
    )(dmerged, z, z, b_gates, b_gates, pa, plru)


def _loss_head(y, target, name):
    t, d = y.shape
    tr = _row_tile(t, d, 256 * 1024)
    nt = t // tr

    def body(y_ref, t_ref, dy_ref, loss_ref, acc):
        i = pl.program_id(0)

        @pl.when(i == 0)
        def _():
            acc[...] = jnp.zeros_like(acc)

        e = y_ref[...] - t_ref[...]
        dy_ref[...] = e * (1.0 / d)
        acc[...] += jnp.sum(e * e, axis=0, keepdims=True)

        @pl.when(i == nt - 1)
        def _():
            loss_ref[...] = (0.5 / d) * jnp.sum(acc[...], axis=-1, keepdims=True)

    spec = pl.BlockSpec((tr, d), lambda i: (i, 0))
    return pl.pallas_call(
        body, name=name, out_shape=(jax.ShapeDtypeStruct((t, d), F32), jax.ShapeDtypeStruct((1, 1), F32)),
        grid=(nt,), in_specs=[spec, spec], out_specs=(spec, pl.BlockSpec((1, 1), lambda i: (0, 0))),
        scratch_shapes=[pltpu.VMEM((1, d), F32)], compiler_params=_params(),
    )(y, target)


def _adamw(w, g, m, v, name):
    r, c = w.shape
    tr = _row_tile(r, c, 128 * 1024)
    c1 = 1.0 - ADAM_B1 ** ADAM_STEP
    c2 = 1.0 - ADAM_B2 ** ADAM_STEP

    def body(w_ref, g_ref, m_ref, v_ref, d_ref, nm_ref, nv_ref):
        gv = g_ref[...]
        mn = ADAM_B1 * m_ref[...] + (1.0 - ADAM_B1) * gv
        vn = ADAM_B2 * v_ref[...] + (1.0 - ADAM_B2) * (gv * gv)
        d_ref[...] = -ADAM_LR * ((mn / c1) / (jnp.sqrt(vn / c2) + ADAM_EPS) + ADAM_WD * w_ref[...])
        nm_ref[...] = mn
        nv_ref[...] = vn

    spec = pl.BlockSpec((tr, c), lambda i: (i, 0))
    shp = jax.ShapeDtypeStruct((r, c), F32)
    return pl.pallas_call(body, name=name, out_shape=(shp, shp, shp), grid=(r // tr,), in_specs=[spec] * 4,
                          out_specs=(spec, spec, spec), compiler_params=_params())(w, g, m, v)


def _swap_halves(v, lane, half):
    n = v.shape[-1]
    return jnp.where(lane < half, pltpu.roll(v, n - half, 1),
                     jnp.where(lane < 2 * half, pltpu.roll(v, half, 1), 0.0))


def _norm_fwd(xraw, g):
    rstd = lax.rsqrt(jnp.mean(xraw * xraw, axis=-1, keepdims=True) + EPS)
    xhat = xraw * rstd
    return xhat, rstd, xhat * g


def _norm_bwd(dy, xhat, rstd, g):
    dxhat = dy * g
    dx = rstd * (dxhat - xhat * jnp.mean(dxhat * xhat, axis=-1, keepdims=True))
    return dx, jnp.sum(dy * xhat, axis=0, keepdims=True)


def _attn_specs(nb, grp, hd, kv, clamp):
    qo, ko, vo = 0, (kv * grp), (kv * grp + kv)
    cur = (lambda i: jnp.minimum(i, nb - 1)) if clamp else (lambda i: i)
    prev = lambda i: jnp.maximum(cur(i) - 1, 0)
    zq = pl.BlockSpec((BLK, grp * hd), lambda h, i: (cur(i), h))
    kc = pl.BlockSpec((BLK, hd), lambda h, i: (cur(i), ko + h))
    kp = pl.BlockSpec((BLK, hd), lambda h, i: (prev(i), ko + h))
    vc = pl.BlockSpec((BLK, hd), lambda h, i: (cur(i), vo + h))
    vp = pl.BlockSpec((BLK, hd), lambda h, i: (prev(i), vo + h))
    tc = pl.BlockSpec((BLK, hd), lambda h, i: (cur(i), 0))
    tp = pl.BlockSpec((BLK, hd), lambda h, i: (prev(i), 0))
    gs = pl.BlockSpec((1, hd), lambda h, i: (0, 0))
    return zq, kc, kp, vc, vp, tc, tp, gs


def _attn_mask(i):
    qi = lax.broadcasted_iota(jnp.int32, (BLK, 2 * BLK), 0)
    kj = lax.broadcasted_iota(jnp.int32, (BLK, 2 * BLK), 1)
    rel = qi + BLK - kj
    return (rel >= 0) & (rel < WINDOW) & ((kj >= BLK) | (i > 0))


def _attn_fwd(z, cos_t, sin_t, qg, kg, sinks, kv, grp, hd, name):
    t = z.shape[0]
    nb = t // BLK
    half = hd // 8
    scale = 1.0 / math.sqrt(hd)
    zq, kc, kp, vc, vp, tc, tp, gs = _attn_specs(nb, grp, hd, kv, False)

    def body(sink_ref, zq_ref, kc_ref, kp_ref, vc_ref, vp_ref, cc_ref, sc_ref, cp_ref, sp_ref, qg_ref, kg_ref, o_ref):
        h, i = pl.program_id(0), pl.program_id(1)
        lane = lax.broadcasted_iota(jnp.int32, (BLK, hd), 1)

        def normrope(xraw, g, c, s):
            y = _norm_fwd(xraw, g)[2]
            return y * c + _swap_halves(y, lane, half) * s

        cc, sc = cc_ref[...], sc_ref[...]
        kcur = normrope(kc_ref[...], kg_ref[...], cc, sc)
        kprev = normrope(kp_ref[...], kg_ref[...], cp_ref[...], sp_ref[...])
        kk = jnp.concatenate([kprev, kcur], axis=0).astype(BF16)
        vv = jnp.concatenate([vp_ref[...], vc_ref[...]], axis=0).astype(BF16)
        mask = _attn_mask(i)
        for g in range(grp):
            q = normrope(zq_ref[:, g * hd:(g + 1) * hd], qg_ref[...], cc, sc).astype(BF16)
            s = lax.dot_general(q, kk, (((1,), (1,)), ((), ())), preferred_element_type=F32) * scale
            s = jnp.where(mask, s, NEG)
            sk = sink_ref[h * grp + g]
            mx = jnp.maximum(jnp.max(s, axis=-1, keepdims=True), sk)
            p = jnp.exp(s - mx)
            den = jnp.sum(p, axis=-1, keepdims=True) + jnp.exp(sk - mx)
            p = p / den
            o_ref[:, g * hd:(g + 1) * hd] = jnp.dot(p.astype(BF16), vv, preferred_element_type=F32).astype(BF16)

    return pl.pallas_call(
        body, name=name, out_shape=jax.ShapeDtypeStruct((t, kv * grp * hd), BF16), grid=(kv, nb),
        in_specs=[pl.BlockSpec(memory_space=pltpu.SMEM), zq, kc, kp, vc, vp, tc, tc, tp, tp, gs, gs],
        out_specs=pl.BlockSpec((BLK, grp * hd), lambda h, i: (i, h)), compiler_params=_params(),
    )(sinks, z, z, z, z, z, cos_t, sin_t, cos_t, sin_t, qg, kg)


def _attn_bwd(dattn, z, cos_t, sin_t, qg, kg, sinks, kv, grp, hd, name):
    t = z.shape[0]
    nb = t // BLK
    half = hd // 8
    scale = 1.0 / math.sqrt(hd)
    zq, kc, kp, vc, vp, tc, tp, gs = _attn_specs(nb, grp, hd, kv, True)

    def body(sink_ref, zq_ref, kc_ref, kp_ref, vc_ref, vp_ref, cc_ref, sc_ref, cp_ref, sp_ref, qg_ref, kg_ref, do_ref,
             dq_ref, dk_ref, dv_ref, dqg_ref, dkg_ref, dsk_ref, dk_carry, dv_carry):
        h, i = pl.program_id(0), pl.program_id(1)
        lane = lax.broadcasted_iota(jnp.int32, (BLK, hd), 1)
        lane1 = lax.broadcasted_iota(jnp.int32, (1, LANE), 1)

        @pl.when((h == 0) & (i == 0))
        def _():
            dqg_ref[...] = jnp.zeros_like(dqg_ref)
            dkg_ref[...] = jnp.zeros_like(dkg_ref)
            dsk_ref[...] = jnp.zeros_like(dsk_ref)

        @pl.when(i == 0)
        def _():
            dk_carry[...] = jnp.zeros_like(dk_carry)
            dv_carry[...] = jnp.zeros_like(dv_carry)

        def rope(y, c, s):
            return y * c + _swap_halves(y, lane, half) * s

        def rope_bwd(dout, c, s):
            return dout * c + _swap_halves(dout * s, lane, half)

        @pl.when(i < nb)
        def _():
            cc, sc, cp, sp = cc_ref[...], sc_ref[...], cp_ref[...], sp_ref[...]
            qgv, kgv = qg_ref[...], kg_ref[...]
            xh_kc, rs_kc, y_kc = _norm_fwd(kc_ref[...], kgv)
            xh_kp, rs_kp, y_kp = _norm_fwd(kp_ref[...], kgv)
            kk = jnp.concatenate([rope(y_kp, cp, sp), rope(y_kc, cc, sc)], axis=0).astype(BF16)
            vv = jnp.concatenate([vp_ref[...], vc_ref[...]], axis=0).astype(BF16)
            mask = _attn_mask(i)
            dkk = jnp.zeros((2 * BLK, hd), F32)
            dvv = jnp.zeros((2 * BLK, hd), F32)
            dqg = jnp.zeros((1, hd), F32)
            dsk = jnp.zeros((1, LANE), F32)
            for g in range(grp):
                xh_q, rs_q, y_q = _norm_fwd(zq_ref[:, g * hd:(g + 1) * hd], qgv)
                q = rope(y_q, cc, sc).astype(BF16)
                s = lax.dot_general(q, kk, (((1,), (1,)), ((), ())), preferred_element_type=F32) * scale
                s = jnp.where(mask, s, NEG)
                sk = sink_ref[h * grp + g]
                mx = jnp.maximum(jnp.max(s, axis=-1, keepdims=True), sk)
                p = jnp.exp(s - mx)
                den = jnp.sum(p, axis=-1, keepdims=True) + jnp.exp(sk - mx)
                p = p / den
                psink = jnp.exp(sk - mx) / den
                dog = do_ref[:, g * hd:(g + 1) * hd].astype(BF16)
                dp = lax.dot_general(dog, vv, (((1,), (1,)), ((), ())), preferred_element_type=F32)
                rsum = jnp.sum(p * dp, axis=-1, keepdims=True)
                ds = (p * (dp - rsum) * scale).astype(BF16)
                dsk = dsk + jnp.where(lane1 == h * grp + g, jnp.sum(-psink * rsum, axis=0, keepdims=True), 0.0)
                dqn = jnp.dot(ds, kk, preferred_element_type=F32)
                dkk = dkk + lax.dot_general(ds, q, (((0,), (0,)), ((), ())), preferred_element_type=F32)
                dvv = dvv + lax.dot_general(p.astype(BF16), dog, (((0,), (0,)), ((), ())), preferred_element_type=F32)
                dxq, dg_q = _norm_bwd(rope_bwd(dqn, cc, sc), xh_q, rs_q, qgv)
                dq_ref[:, g * hd:(g + 1) * hd] = dxq
                dqg = dqg + dg_q
            dkp_raw, dg_kp = _norm_bwd(rope_bwd(dkk[:BLK], cp, sp), xh_kp, rs_kp, kgv)
            dkc_raw, dg_kc = _norm_bwd(rope_bwd(dkk[BLK:], cc, sc), xh_kc, rs_kc, kgv)
            dk_ref[...] = dk_carry[...] + dkp_raw
            dv_ref[...] = dv_carry[...] + dvv[:BLK]
            dk_carry[...] = dkc_raw
            dv_carry[...] = dvv[BLK:]
            dqg_ref[...] += dqg
            dkg_ref[...] += dg_kp + dg_kc
            dsk_ref[...] += dsk

        @pl.when(i == nb)
        def _():
            dk_ref[...] = dk_carry[...]
            dv_ref[...] = dv_carry[...]

    kvw = kv * hd
    vec = pl.BlockSpec((1, hd), lambda h, i: (0, 0))
    shifted = pl.BlockSpec((BLK, hd), lambda h, i: (jnp.maximum(i - 1, 0), h))
    return pl.pallas_call(
        body, name=name,
        out_shape=(jax.ShapeDtypeStruct((t, kv * grp * hd), F32), jax.ShapeDtypeStruct((t, kvw), F32),
                   jax.ShapeDtypeStruct((t, kvw), F32), jax.ShapeDtypeStruct((1, hd), F32),
                   jax.ShapeDtypeStruct((1, hd), F32), jax.ShapeDtypeStruct((1, LANE), F32)),
        grid=(kv, nb + 1),
        in_specs=[pl.BlockSpec(memory_space=pltpu.SMEM), zq, kc, kp, vc, vp, tc, tc, tp, tp, gs, gs,
                  pl.BlockSpec((BLK, grp * hd), lambda h, i: (jnp.minimum(i, nb - 1), h))],
        out_specs=(pl.BlockSpec((BLK, grp * hd), lambda h, i: (jnp.minimum(i, nb - 1), h)), shifted, shifted, vec, vec,
                   pl.BlockSpec((1, LANE), lambda h, i: (0, 0))),
        scratch_shapes=[pltpu.VMEM((BLK, hd), F32), pltpu.VMEM((BLK, hd), F32)], compiler_params=_params(),
    )(sinks, z, z, z, z, z, cos_t, sin_t, cos_t, sin_t, qg, kg, dattn)


def _conv_fwd(u, w, b, name):
    t, c = u.shape
    taps = w.shape[0]
    cb = _pick(c, (1408, 1024, 512, 256, 128))
    tr = _row_tile(t, cb, 256 * 1024)
    hb = tr // SUBLANE

    def body(u_ref, halo_ref, w_ref, b_ref, o_ref):
        i = pl.program_id(0)
        x = u_ref[...]
        acc = b_ref[...] + w_ref[taps - 1:taps, :] * x
        for k in range(taps - 1):
            acc = acc + w_ref[k:k + 1, :] * pltpu.roll(x, taps - 1 - k, 0)
        o_ref[...] = acc
        row = lax.broadcasted_iota(jnp.int32, (SUBLANE, cb), 0)
        hp = jnp.where(i > 0, halo_ref[...], 0.0)
        x8 = u_ref[0:SUBLANE, :]
        acc8 = b_ref[...] + w_ref[taps - 1:taps, :] * x8
        for k in range(taps - 1):
            s = taps - 1 - k
            acc8 = acc8 + w_ref[k:k + 1, :] * jnp.where(row < s, pltpu.roll(hp, s, 0), pltpu.roll(x8, s, 0))
        o_ref[0:SUBLANE, :] = acc8

    blk = pl.BlockSpec((tr, cb), lambda i, j: (i, j))
    return pl.pallas_call(
        body, name=name, out_shape=jax.ShapeDtypeStruct((t, c), F32), grid=(t // tr, c // cb),
        in_specs=[blk, pl.BlockSpec((SUBLANE, cb), lambda i, j: (jnp.maximum(i * hb - 1, 0), j)),
                  pl.BlockSpec((taps, cb), lambda i, j: (0, j)), pl.BlockSpec((1, cb), lambda i, j: (0, j))],
        out_specs=blk, compiler_params=_params(),
    )(u, u, w, b)


def _conv_bwd(duc, u, w, name):
    t, c = u.shape
    taps = w.shape[0]
    cb = _pick(c, (1408, 1024, 512, 256, 128))
    tr = _row_tile(t, cb, 256 * 1024)
    hb, nt = tr // SUBLANE, t // tr

    def body(g_ref, gnext_ref, u_ref, uprev_ref, w_ref, du_ref, dw_ref, db_ref):
        i = pl.program_id(1)

        @pl.when(i == 0)
        def _():
            dw_ref[...] = jnp.zeros_like(dw_ref)
            db_ref[...] = jnp.zeros_like(db_ref)

        row = lax.broadcasted_iota(jnp.int32, (SUBLANE, cb), 0)
        g, x = g_ref[...], u_ref[...]
        du = w_ref[taps - 1:taps, :] * g
        for k in range(taps - 1):
            du = du + w_ref[k:k + 1, :] * pltpu.roll(g, tr - (taps - 1 - k), 0)
        du_ref[...] = du
        hn = jnp.where(i < nt - 1, gnext_ref[...], 0.0)
        g8 = g_ref[tr - SUBLANE:tr, :]
        du8 = w_ref[taps - 1:taps, :] * g8
        for k in range(taps - 1):
            s = taps - 1 - k
            du8 = du8 + w_ref[k:k + 1, :] * jnp.where(row >= SUBLANE - s, pltpu.roll(hn, SUBLANE - s, 0),
                                                     pltpu.roll(g8, SUBLANE - s, 0))
        du_ref[tr - SUBLANE:tr, :] = du8

        hp = jnp.where(i > 0, uprev_ref[...], 0.0)
        xl8, gf8 = u_ref[tr - SUBLANE:tr, :], g_ref[0:SUBLANE, :]
        db_ref[...] += jnp.sum(g, axis=0, keepdims=True)
        dw_ref[taps - 1:taps, :] += jnp.sum(g * x, axis=0, keepdims=True)
        for k in range(taps - 1):
            s = taps - 1 - k
            fix = jnp.where(row < s, pltpu.roll(hp, s, 0) - pltpu.roll(xl8, s, 0), 0.0)
            dw_ref[k:k + 1, :] += (jnp.sum(g * pltpu.roll(x, s, 0), axis=0, keepdims=True)
                                   + jnp.sum(gf8 * fix, axis=0, keepdims=True))

    blk = pl.BlockSpec((tr, cb), lambda j, i: (i, j))
    nh = t // SUBLANE
    return pl.pallas_call(
        body, name=name,
        out_shape=(jax.ShapeDtypeStruct((t, c), F32), jax.ShapeDtypeStruct((taps, c), F32),
                   jax.ShapeDtypeStruct((1, c), F32)),
        grid=(c // cb, nt),
        in_specs=[blk, pl.BlockSpec((SUBLANE, cb), lambda j, i: (jnp.minimum((i + 1) * hb, nh - 1), j)),
                  blk, pl.BlockSpec((SUBLANE, cb), lambda j, i: (jnp.maximum(i * hb - 1, 0), j)),
                  pl.BlockSpec((taps, cb), lambda j, i: (0, j))],
        out_specs=(blk, pl.BlockSpec((taps, cb), lambda j, i: (0, j)), pl.BlockSpec((1, cb), lambda j, i: (0, j))),
        compiler_params=_params(),
    )(duc, duc, u, u, w)


def _dense_groups(w, gw):
    n, bw, _ = w.shape
    per = gw // bw
    w4 = w.reshape(n // per, per, bw, bw).astype(BF16)
    eye = jnp.eye(per, dtype=BF16)
    return (w4[:, :, :, None, :] * eye[None, :, None, :, None]).reshape(n // per, gw, gw)


def _diag_blocks(dense, n, bw):
    ng, gw, _ = dense.shape
    per = gw // bw
    diag = jnp.diagonal(dense.reshape(ng, per, bw, per, bw), axis1=1, axis2=3)
    return jnp.moveaxis(diag, -1, 1).reshape(n, bw, bw)


def _gates_fwd(uc, wr, wi, name):
    t, c = uc.shape
    ng, gw, _ = wr.shape
    tr = _pick(t, (512, 256, 128))

    def body(u_ref, wr_ref, wi_ref, r_ref, i_ref):
        a = u_ref[...].astype(BF16)
        r_ref[...] = jnp.dot(a, wr_ref[...], preferred_element_type=F32)
        i_ref[...] = jnp.dot(a, wi_ref[...], preferred_element_type=F32)

    blk = pl.BlockSpec((tr, gw), lambda h, i: (i, h))
    wsp = pl.BlockSpec((None, gw, gw), lambda h, i: (h, 0, 0))
    shp = jax.ShapeDtypeStruct((t, c), F32)
    return pl.pallas_call(body, name=name, out_shape=(shp, shp), grid=(ng, t // tr), in_specs=[blk, wsp, wsp],
                          out_specs=(blk, blk), compiler_params=_params())(uc, wr, wi)


def _gates_bwd_x(duc, drp, dip, wr, wi, name):
    t, c = duc.shape
    ng, gw, _ = wr.shape
    tr = _pick(t, (512, 256, 128))
    dims = (((1,), (1,)), ((), ()))

    def body(d_ref, r_ref, i_ref, wr_ref, wi_ref, o_ref):
        o_ref[...] = (d_ref[...]
                      + lax.dot_general(r_ref[...].astype(BF16), wr_ref[...], dims, preferred_element_type=F32)
                      + lax.dot_general(i_ref[...].astype(BF16), wi_ref[...], dims, preferred_element_type=F32))

    blk = pl.BlockSpec((tr, gw), lambda h, i: (i, h))
    wsp = pl.BlockSpec((None, gw, gw), lambda h, i: (h, 0, 0))
    return pl.pallas_call(body, name=name, out_shape=jax.ShapeDtypeStruct((t, c), F32), grid=(ng, t // tr),
                          in_specs=[blk, blk, blk, wsp, wsp], out_specs=blk, compiler_params=_params())(duc, drp, dip, wr, wi)


def _gates_bwd_w(uc, dpre, ng, gw, name):
    t, c = uc.shape
    tk = _pick(t, (512, 256, 128))
    dims = (((0,), (0,)), ((), ()))

    def body(u_ref, d_ref, o_ref):
        @pl.when(pl.program_id(1) == 0)
        def _():
            o_ref[...] = jnp.zeros_like(o_ref)

        o_ref[...] += lax.dot_general(u_ref[...].astype(BF16), d_ref[...].astype(BF16), dims, preferred_element_type=F32)

    blk = pl.BlockSpec((tk, gw), lambda h, i: (i, h))
    return pl.pallas_call(body, name=name, out_shape=jax.ShapeDtypeStruct((ng, gw, gw), F32), grid=(ng, t // tk),
                          in_specs=[blk, blk], out_specs=pl.BlockSpec((None, gw, gw), lambda h, i: (h, 0, 0)),
                          compiler_params=_params())(uc, dpre)


def _softplus(x):
    return jnp.maximum(x, 0.0) + jnp.log(1.0 + jnp.exp(-jnp.abs(x)))


def _neg_expm1(x):
    series = x * (1.0 + x * (0.5 + x * (1.0 / 6.0 + x * (1.0 / 24.0 + x * (1.0 / 120.0)))))
    return -jnp.where(x > -0.05, series, jnp.exp(x) - 1.0)


_GELU_C = math.sqrt(2.0 / math.pi)


def _gelu_parts(x):
    inner = _GELU_C * (x + 0.044715 * (x * x * x))
    th = jnp.tanh(inner)
    gelu = 0.5 * x * (1.0 + th)
    dgelu = 0.5 * (1.0 + th) + 0.5 * x * (1.0 - th * th) * (_GELU_C * (1.0 + 3.0 * 0.044715 * (x * x)))
    return gelu, dgelu


def _lru_gate_values(uc, rpre, ipre, br, bi, sp):
    r = _sigmoid(rpre + br)
    ig = _sigmoid(ipre + bi)
    a = jnp.exp(-LRU_C * r * sp)
    mult = jnp.sqrt(jnp.maximum(_neg_expm1(2.0 * (-LRU_C * r * sp)), 0.0))
    return r, ig, a, mult


def _lru_fwd(uc, rpre, ipre, gr, br, bi, lam, name):
    t, c = uc.shape
    cb = _pick(c, (1408, 1024, 512, 256, 128))
    tb = _pick(t, (512, 256, 128))
    ntile = tb // SUBLANE

    def body(uc_ref, r_ref, i_ref, gr_ref, br_ref, bi_ref, lam_ref, h_ref, rec_ref, carry):
        @pl.when(pl.program_id(1) == 0)
        def _():
            carry[...] = jnp.zeros_like(carry)

        sp = _softplus(-lam_ref[...])
        br, bi = br_ref[...], bi_ref[...]
        row = lax.broadcasted_iota(jnp.int32, (SUBLANE, cb), 0)

        def tile(k, c_in):
            sl = pl.ds(pl.multiple_of(k * SUBLANE, SUBLANE), SUBLANE)
            ucv = uc_ref[sl, :]
            _, ig, a, mult = _lru_gate_values(ucv, r_ref[sl, :], i_ref[sl, :], br, bi, sp)
            b = mult * (ig * ucv)
            for d in (1, 2, 4):
                a_s = jnp.where(row >= d, pltpu.roll(a, d, 0), 1.0)
                b_s = jnp.where(row >= d, pltpu.roll(b, d, 0), 0.0)
                b = a * b_s + b
                a = a * a_s
            hv = b + a * c_in
            h_ref[sl, :] = hv
            rec_ref[sl, :] = hv * _gelu_parts(gr_ref[sl, :])[0]
            return hv[SUBLANE - 1:SUBLANE, :]

        c_out = lax.fori_loop(0, ntile, tile, carry[0:1, :])
        carry[...] = jnp.broadcast_to(c_out, (SUBLANE, cb))

    blk = pl.BlockSpec((tb, cb), lambda j, i: (i, j))
    vec = pl.BlockSpec((1, cb), lambda j, i: (0, j))
    shp = jax.ShapeDtypeStruct((t, c), F32)
    return pl.pallas_call(body, name=name, out_shape=(shp, shp), grid=(c // cb, t // tb),
                          in_specs=[blk, blk, blk, blk, vec, vec, vec], out_specs=(blk, blk),
                          scratch_shapes=[pltpu.VMEM((SUBLANE, cb), F32)], compiler_params=_params(),
                          )(uc, rpre, ipre, gr, br, bi, lam)


def _lru_bwd(drec, hst, uc, rpre, ipre, gr, br, bi, lam, name):
    t, c = uc.shape
    cb = _pick(c, (1408, 1024, 512, 256, 128))
    tb = _pick(t, (256, 128))
    ntile, nt, hb = tb // SUBLANE, t // tb, tb // SUBLANE

    def body(drec_ref, h_ref, hprev_ref, uc_ref, r_ref, i_ref, gr_ref, br_ref, bi_ref, lam_ref,
             dgr_ref, drp_ref, dip_ref, duc_ref, dlam_ref, dbr_ref, dbi_ref, carry):
        step = pl.program_id(1)
        first_block = step == nt - 1

        @pl.when(step == 0)
        def _():
            carry[...] = jnp.zeros_like(carry)
            dlam_ref[...] = jnp.zeros_like(dlam_ref)
            dbr_ref[...] = jnp.zeros_like(dbr_ref)
            dbi_ref[...] = jnp.zeros_like(dbi_ref)

        lam = lam_ref[...]
        sp = _softplus(-lam)
        br, bi = br_ref[...], bi_ref[...]
        row = lax.broadcasted_iota(jnp.int32, (SUBLANE, cb), 0)
        halo = jnp.where(first_block, 0.0, hprev_ref[...])

        def tile(kk, state):
            c_p, acc_sp, acc_br, acc_bi = state
            k = ntile - 1 - kk
            sl = pl.ds(pl.multiple_of(k * SUBLANE, SUBLANE), SUBLANE)
            slp = pl.ds(pl.multiple_of(jnp.maximum(k - 1, 0) * SUBLANE, SUBLANE), SUBLANE)
            ucv = uc_ref[sl, :]
            r, ig, a, mult = _lru_gate_values(ucv, r_ref[sl, :], i_ref[sl, :], br, bi, sp)
            hv = h_ref[sl, :]
            below = jnp.where(k > 0, h_ref[slp, :], halo)
            hprev = jnp.where(row == 0, pltpu.roll(below, 1, 0), pltpu.roll(hv, 1, 0))
            gelu, dgelu = _gelu_parts(gr_ref[sl, :])
            drec = drec_ref[sl, :]
            dh = drec * gelu
            dgr_ref[sl, :] = drec * hv * dgelu
            pa, pb = a, a * dh
            for d in (1, 2, 4):
                a_s = jnp.where(row < SUBLANE - d, pltpu.roll(pa, SUBLANE - d, 0), 1.0)
                b_s = jnp.where(row < SUBLANE - d, pltpu.roll(pb, SUBLANE - d, 0), 0.0)
                pb = pa * b_s + pb
                pa = pa * a_s
            pv = pb + pa * c_p
            gt = dh + jnp.where(row == SUBLANE - 1, c_p, pltpu.roll(pv, SUBLANE - 1, 0))
            da = gt * hprev
            duc_ref[sl, :] = gt * mult * ig
            dmult = gt * ig * ucv
            dig = gt * mult * ucv
            dla = da * a - jnp.where(mult > 0.0, dmult * (a * a) / mult, 0.0)
            drp = dla * (-LRU_C * sp) * (r * (1.0 - r))
            dip = dig * (ig * (1.0 - ig))
            drp_ref[sl, :] = drp
            dip_ref[sl, :] = dip
            return pv[0:1, :], acc_sp + dla * (-LRU_C * r), acc_br + drp, acc_bi + dip

        zero = jnp.zeros((SUBLANE, cb), F32)
        c_out, acc_sp, acc_br, acc_bi = lax.fori_loop(0, ntile, tile, (carry[0:1, :], zero, zero, zero))
        carry[...] = jnp.broadcast_to(c_out, (SUBLANE, cb))
        dlam_ref[...] += jnp.sum(acc_sp, axis=0, keepdims=True) * (-_sigmoid(-lam))
        dbr_ref[...] += jnp.sum(acc_br, axis=0, keepdims=True)
        dbi_ref[...] += jnp.sum(acc_bi, axis=0, keepdims=True)

    blk = pl.BlockSpec((tb, cb), lambda j, i: (nt - 1 - i, j))
    vec = pl.BlockSpec((1, cb), lambda j, i: (0, j))
    halo_spec = pl.BlockSpec((SUBLANE, cb), lambda j, i: (jnp.maximum((nt - 1 - i) * hb - 1, 0), j))
    big, small = jax.ShapeDtypeStruct((t, c), F32), jax.ShapeDtypeStruct((1, c), F32)
    return pl.pallas_call(
        body, name=name, out_shape=(big, big, big, big, small, small, small), grid=(c // cb, nt),
        in_specs=[blk, blk, halo_spec, blk, blk, blk, blk, vec, vec, vec],
        out_specs=(blk, blk, blk, blk, vec, vec, vec),
        scratch_shapes=[pltpu.VMEM((SUBLANE, cb), F32)], compiler_params=_params(),
    )(drec, hst, hst, uc, rpre, ipre, gr, br, bi, lam)


ANY = pl.BlockSpec(memory_space=pl.ANY)


class _Env:
    def __init__(self, ins, outs, send, recv):
        self.ins, self.outs, self.send, self.recv = ins, outs, send, recv
        self.x, self.y, self.c = lax.axis_index("x"), lax.axis_index("y"), lax.axis_index("c")
        self.me = 2 * self.x + self.y
        self.chips = [(1 - self.x, self.y), (self.x, 1 - self.y), (1 - self.x, 1 - self.y)]
        self.sibling = (self.x, self.y, 1 - self.c)

    def copy(self, src, dst, sem, to):
        return pltpu.make_async_remote_copy(src_ref=src, dst_ref=dst, send_sem=self.send.at[sem],
                                            recv_sem=self.recv.at[sem], device_id=to, device_id_type=MESH)


class _Exchange:
    inputs, out_shapes, aliases, n_sems = (), (), {}, 0

    def start(self, e):
        raise NotImplementedError

    def finish(self, e):
        raise NotImplementedError


def _run_exchange(job, name):
    n_in, n_out = len(job.inputs), len(job.out_shapes)

    def body(*refs):
        e = _Env(refs[:n_in], refs[n_in:n_in + n_out], refs[n_in + n_out], refs[n_in + n_out + 1])
        job.start(e)
        job.finish(e)

    return pl.pallas_call(
        body, name=name, out_shape=tuple(job.out_shapes), in_specs=[ANY] * n_in, out_specs=tuple([ANY] * n_out),
        input_output_aliases=dict(job.aliases),
        scratch_shapes=[pltpu.SemaphoreType.DMA((job.n_sems,)), pltpu.SemaphoreType.DMA((job.n_sems,))],
    )(*job.inputs)


def _shard_region(ref, kind, chip, half, rh, width):
    if kind == "col":
        return ref.at[pl.ds(half * rh, rh), pl.ds(chip * width, width)]
    return ref.at[pl.ds(chip * (2 * rh) + half * rh, rh), :]


class _AllGather(_Exchange):
    def __init__(self, fulls, kinds):
        self.inputs, self.kinds = list(fulls), kinds
        self.out_shapes = [jax.ShapeDtypeStruct(f.shape, f.dtype) for f in fulls]
        self.aliases = {a: a for a in range(len(fulls))}
        self.n_sems = 6 * len(fulls)
        self.geo = [(f.shape[0] // 2, f.shape[1] // N_CHIPS) if k == "col" else (f.shape[0] // (2 * N_CHIPS), f.shape[1])
                    for f, k in zip(fulls, kinds)]

    def _region(self, ref, a, chip, half):
        return _shard_region(ref, self.kinds[a], chip, half, *self.geo[a])

    def _ici(self, e, a, k, chip):
        cx, cy = e.chips[k]
        return e.copy(self._region(e.ins[a], a, chip, e.c), self._region(e.outs[a], a, chip, e.c), a * 6 + k,
                      (cx, cy, e.c))

    def _d2d(self, e, a, k, half):
        cx, cy = e.chips[k]
        region = self._region(e.outs[a], a, 2 * cx + cy, half)
        return e.copy(region, region, a * 6 + 3 + k, e.sibling)

    def start(self, e):
        for a in range(len(self.inputs)):
            for k in range(3):
                self._ici(e, a, k, e.me).start()

    def finish(self, e):
        n = len(self.inputs)
        for a in range(n):
            for k, (cx, cy) in enumerate(e.chips):
                self._ici(e, a, k, 2 * cx + cy).wait_recv()
                self._d2d(e, a, k, e.c).start()
        for a in range(n):
            for k in range(3):
                self._d2d(e, a, k, 1 - e.c).wait_recv()
        for a in range(n):
            for k in range(3):
                self._ici(e, a, k, e.me).wait_send()
                self._d2d(e, a, k, e.c).wait_send()


class _SiblingExchange(_Exchange):
    def __init__(self, grads):
        self.inputs = list(grads)
        self.out_shapes = [jax.ShapeDtypeStruct((g.shape[0],) + g.shape[2:], g.dtype) for g in grads]
        self.n_sems = len(grads)

    def _copy(self, e, a):
        return e.copy(e.ins[a].at[:, 1 - e.c], e.outs[a], a, e.sibling)

    def start(self, e):
        for a in range(len(self.inputs)):
            self._copy(e, a).start()

    def finish(self, e):
        for a in range(len(self.inputs)):
            self._copy(e, a).wait()


def _piece(ref, kind, chip, width):
    if kind == "col":
        return ref.at[0, :, pl.ds(chip * width, width)]
    return ref.at[chip]


class _ChipExchange(_Exchange):
    def __init__(self, sums, kinds):
        self.inputs, self.kinds = list(sums), kinds
        self.widths = [s.shape[2] // N_CHIPS if k == "col" else s.shape[2] for s, k in zip(sums, kinds)]
        self.out_shapes = [jax.ShapeDtypeStruct((3, s.shape[1], w), s.dtype) for s, w in zip(sums, self.widths)]
        self.n_sems = 3 * len(sums)

    def _copy(self, e, a, k, chip):
        cx, cy = e.chips[k]
        return e.copy(_piece(e.ins[a], self.kinds[a], chip, self.widths[a]), e.outs[a].at[k], a * 3 + k, (cx, cy, e.c))

    def start(self, e):
        for a in range(len(self.inputs)):
            for k, (cx, cy) in enumerate(e.chips):
                self._copy(e, a, k, 2 * cx + cy).start()

    def finish(self, e):
        for a in range(len(self.inputs)):
            for k, (cx, cy) in enumerate(e.chips):
                self._copy(e, a, k, 2 * cx + cy).wait()


class _FinishExchange(_Exchange):
    def __init__(self, finals):
        self.inputs = list(finals)
        self.out_shapes = [jax.ShapeDtypeStruct(f.shape, f.dtype) for f in finals]
        self.aliases = {a: a for a in range(len(finals))}
        self.n_big = len(finals) - 1
        self.n_sems = self.n_big + 7
        self.rel = [(fx, fy, fc) for fx in (0, 1) for fy in (0, 1) for fc in (0, 1)][1:]

    def _big(self, e, a, mine):
        rh = self.inputs[a].shape[0] // 2
        rows = pl.ds((e.c if mine else 1 - e.c) * rh, rh)
        return e.copy((e.ins[a] if mine else e.outs[a]).at[rows, :], e.outs[a].at[rows, :], a, e.sibling)

    def _small(self, e, r, mine):
        fx, fy, fc = self.rel[r]
        px, py, pc = (1 - e.x if fx else e.x), (1 - e.y if fy else e.y), (1 - e.c if fc else e.c)
        rh = self.inputs[-1].shape[0] // (2 * N_CHIPS)
        slot = (2 * e.me + e.c) if mine else (2 * (2 * px + py) + pc)
        rows = pl.ds(slot * rh, rh)
        return e.copy((e.ins[-1] if mine else e.outs[-1]).at[rows, :], e.outs[-1].at[rows, :], self.n_big + r,
                      (px, py, pc))

    def start(self, e):
        for a in range(self.n_big):
            self._big(e, a, True).start()
        for r in range(7):
            self._small(e, r, True).start()

    def finish(self, e):
        for a in range(self.n_big):
            self._big(e, a, False).wait_recv()
        for r in range(7):
            self._small(e, r, False).wait_recv()
        for a in range(self.n_big):
            self._big(e, a, True).wait_send()
        for r in range(7):
            self._small(e, r, True).wait_send()


def _cast_into_full(w, kind, idx, name):
    r, c = w.shape
    tr = _row_tile(r, c)
    nrb = r // tr

    def body(idx_ref, w_ref, o_ref):
        o_ref[...] = w_ref[...].astype(BF16)

    if kind == "col":
        full, out_map = (r, N_CHIPS * c), (lambda i, idx_ref: (i, idx_ref[1]))
    else:
        full, out_map = (N_CHIPS * r, c), (lambda i, idx_ref: (idx_ref[1] * nrb + i, 0))
    return pl.pallas_call(
        body, name=name, out_shape=jax.ShapeDtypeStruct(full, BF16),
        grid_spec=pltpu.PrefetchScalarGridSpec(
            num_scalar_prefetch=1, grid=(nrb,), in_specs=[pl.BlockSpec((tr, c), lambda i, idx_ref: (i, 0))],
            out_specs=pl.BlockSpec((tr, c), out_map)),
        compiler_params=_params(),
    )(idx, w)


_EW_COLS = (1280, 1408, 1024, 640, 512, 256, 128)


def _add_own_half(g4, recv, idx, out_dtype, name):
    p, _, rh, n = g4.shape
    tc = _pick(n, _EW_COLS)
    tr = _row_tile(rh, tc, 256 * 1024)

    def body(idx_ref, g_ref, r_ref, o_ref):
        o_ref[...] = (g_ref[...] + r_ref[...]).astype(out_dtype)

    return pl.pallas_call(
        body, name=name, out_shape=jax.ShapeDtypeStruct((p, rh, n), out_dtype),
        grid_spec=pltpu.PrefetchScalarGridSpec(
            num_scalar_prefetch=1, grid=(p, rh // tr, n // tc),
            in_specs=[pl.BlockSpec((None, None, tr, tc), lambda q, i, j, idx_ref: (q, idx_ref[0], i, j)),
                      pl.BlockSpec((None, tr, tc), lambda q, i, j, idx_ref: (q, i, j))],
            out_specs=pl.BlockSpec((None, tr, tc), lambda q, i, j, idx_ref: (q, i, j))),
        compiler_params=_params(),
    )(idx, g4, recv)


def _sum_chips(own, kind, parts, idx, slots, to_all, name):
    _, rh, w = parts.shape
    tc = _pick(w, _EW_COLS)
    tr = _row_tile(rh, tc, 128 * 1024)
    nrb, ncb = rh // tr, w // tc

    def body(idx_ref, own_ref, p0, p1, p2, o_ref):
        o_ref[...] = ((own_ref[...].astype(F32) + p0[...].astype(F32)) + p1[...].astype(F32)) + p2[...].astype(F32)

    if kind == "col":
        own_spec = pl.BlockSpec((None, tr, tc), lambda i, j, idx_ref: (0, i, idx_ref[1] * ncb + j))
    else:
        own_spec = pl.BlockSpec((None, tr, tc), lambda i, j, idx_ref: (idx_ref[1], i, j))
    if to_all:
        out_map = lambda i, j, idx_ref: ((2 * idx_ref[1] + idx_ref[0]) * nrb + i, j)
    else:
        out_map = lambda i, j, idx_ref: (idx_ref[0] * nrb + i, j)

    def part(k):
        return pl.BlockSpec((None, tr, tc), lambda i, j, idx_ref: (k, i, j))

    return pl.pallas_call(
        body, name=name, out_shape=jax.ShapeDtypeStruct((slots * rh, w), F32),
        grid_spec=pltpu.PrefetchScalarGridSpec(
            num_scalar_prefetch=1, grid=(nrb, ncb), in_specs=[own_spec, part(0), part(1), part(2)],
            out_specs=pl.BlockSpec((tr, tc), out_map)),
        compiler_params=_params(),
    )(idx, own, parts, parts, parts)


def _pack(arrays, rows):
    flat = jnp.concatenate([a.reshape(-1) for a in arrays])
    return jnp.pad(flat, (0, rows * SMALL_PACK_COLS - flat.shape[0])).reshape(rows, SMALL_PACK_COLS)


def _unpack(packed, shapes):
    flat = packed.reshape(-1)
    out, o = [], 0
    for shp in shapes:
        size = math.prod(shp)
        out.append(flat[o:o + size].reshape(shp))
        o += size
    return out


def _pack_rows(shapes):
    total = sum(math.prod(s) for s in shapes)
    unit = SMALL_PACK_COLS * N_CHIPS * 2 * SUBLANE
    return -(-total // unit) * (N_CHIPS * 2 * SUBLANE)


BIG = ("w_in", "w_attn_proj", "w_lru_proj", "w_out", "w_ffn_gate", "w_ffn_up", "w_ffn_down")
BIG_KIND = {"w_in": "col", "w_attn_proj": "row", "w_lru_proj": "row", "w_out": "row", "w_ffn_gate": "col",
            "w_ffn_up": "col", "w_ffn_down": "row"}
SMALL = ("norm1_g", "b_gates", "q_norm_g", "k_norm_g", "sinks", "conv_w", "conv_b", "w_rgate", "b_rgate",
         "w_igate", "b_igate", "lru_lambda", "norm2_g")
WEIGHTS = ("norm1_g", "w_in", "b_gates", "q_norm_g", "k_norm_g", "sinks", "conv_w", "conv_b", "w_rgate", "b_rgate",
           "w_igate", "b_igate", "lru_lambda", "w_attn_proj", "w_lru_proj", "w_out", "norm2_g", "w_ffn_gate",
           "w_ffn_up", "w_ffn_down")


def kernel(x, positions, norm1_g, w_in, b_gates, q_norm_g, k_norm_g, sinks, conv_w, conv_b, w_rgate, b_rgate, w_igate, b_igate, lru_lambda, w_attn_proj, w_lru_proj, w_out, norm2_g, w_ffn_gate, w_ffn_up, w_ffn_down, loss_target, m_norm1_g, m_w_in, m_b_gates, m_q_norm_g, m_k_norm_g, m_sinks, m_conv_w, m_conv_b, m_w_rgate, m_b_rgate, m_w_igate, m_b_igate, m_lru_lambda, m_w_attn_proj, m_w_lru_proj, m_w_out, m_norm2_g, m_w_ffn_gate, m_w_ffn_up, m_w_ffn_down, v_norm1_g, v_w_in, v_b_gates, v_q_norm_g, v_k_norm_g, v_sinks, v_conv_w, v_conv_b, v_w_rgate, v_b_rgate, v_w_igate, v_b_igate, v_lru_lambda, v_w_attn_proj, v_w_lru_proj, v_w_out, v_norm2_g, v_w_ffn_gate, v_w_ffn_up, v_w_ffn_down):
    args = dict(locals())
    w = {n: args[n] for n in WEIGHTS}
    mom = {n: args["m_" + n] for n in WEIGHTS}
    var = {n: args["v_" + n] for n in WEIGHTS}

    t, d = x.shape[1], x.shape[2]
    hd = q_norm_g.shape[-1]
    nq = sinks.shape[-1]
    q_w = nq * hd
    d_rnn = conv_b.shape[-1]
    taps = conv_w.shape[1]
    n_blocks, bw = w_rgate.shape[1], w_rgate.shape[2]
    in_w = w_in.shape[-1] * N_CHIPS
    kv_w = (in_w - q_w - 2 * d_rnn - 2 * d) // 2
    kv = kv_w // hd
    grp = nq // kv
    u_off = q_w + 2 * kv_w
    gr_off = u_off + d_rnn
    ga_off = gr_off + d_rnn
    gw = bw * LANE // math.gcd(bw, LANE)
    ng = d_rnn // gw
    chip = 2 * lax.axis_index("x") + lax.axis_index("y")
    idx = jnp.stack([lax.axis_index("c"), chip]).astype(jnp.int32)

    x2, tgt = x[0], loss_target[0]

    kinds = [BIG_KIND[n] for n in BIG]
    placed = [_cast_into_full(w[n][0], BIG_KIND[n], idx, "cast_" + n) for n in BIG]
    win_f, wap_f, wlp_f, wout_f, wg_f, wu_f, wd_f = _run_exchange(_AllGather(placed, kinds), "allgather_weights")
    conv_w_full = _gather_small(conv_w[0], "allgather_conv_w")
    conv_w_full = jnp.transpose(conv_w_full, (1, 0, 2)).reshape(taps, d_rnn)
    wr_dense = _dense_groups(w_rgate[0], gw)
    wi_dense = _dense_groups(w_igate[0], gw)

    inv_freq = ROPE_THETA ** (-jnp.arange(0, hd // 4, 2, dtype=F32) / (hd // 4))
    ang = positions[0].astype(F32)[:, None] * inv_freq
    cos, sin = jnp.cos(ang), jnp.sin(ang)
    rest = hd - 2 * cos.shape[1]
    cos_t = jnp.concatenate([cos, cos, jnp.ones((t, rest), F32)], axis=1)
    sin_t = jnp.concatenate([-sin, sin, jnp.zeros((t, rest), F32)], axis=1)
    sinks1 = sinks[0]

    xn = _rms_fwd(x2, norm1_g, "rms1_fwd")
    z = _matmul(xn, win_f, "nn", "in_proj")
    zu, zgr = z[:, u_off:u_off + d_rnn], z[:, gr_off:gr_off + d_rnn]
    attn = _attn_fwd(z, cos_t, sin_t, q_norm_g, k_norm_g, sinks1, kv, grp, hd, "attn_fwd")
    uc = _conv_fwd(zu, conv_w_full, conv_b, "conv_fwd")
    rpre, ipre = _gates_fwd(uc, wr_dense, wi_dense, "gates_fwd")
    hst, rec = _lru_fwd(uc, rpre, ipre, zgr, b_rgate, b_igate, lru_lambda, "lru_fwd")
    pa = _matmul(attn, wap_f, "nn", "attn_proj")
    plru = _matmul(rec, wlp_f, "nn", "lru_proj")
    merged = _merge_fwd(z, b_gates, pa, plru, ga_off, "merge_fwd")
    h1 = _matmul(merged, wout_f, "nn", "out_proj", add=x2)
    hn = _rms_fwd(h1, norm2_g, "rms2_fwd")
    gate = _matmul(hn, wg_f, "nn", "ffn_gate")
    up = _matmul(hn, wu_f, "nn", "ffn_up")
    act = _swiglu_fwd(gate, up, "swiglu_fwd")
    yout = _matmul(act, wd_f, "nn", "ffn_down", add=h1)
    dy, loss_part = _loss_head(yout, tgt, "loss_head")
    loss = lax.psum(loss_part[0, 0], ("x", "y", "c"))

    g_wd = _matmul(act, dy, "tn", "d_w_ffn_down")
    dact = _matmul(dy, wd_f, "nt", "d_act")
    dgate, dup = _swiglu_bwd(dact, gate, up, "swiglu_bwd")
    g_wg = _matmul(hn, dgate, "tn", "d_w_ffn_gate")
    g_wu = _matmul(hn, dup, "tn", "d_w_ffn_up")
    dhn = _matmul(dgate, wg_f, "nt", "d_hn_gate")
    dhn = _matmul(dup, wu_f, "nt", "d_hn_up", add=dhn)
    dh1, g_norm2 = _rms_bwd(dhn, h1, norm2_g, dy, "rms2_bwd")
    g_wout = _matmul(merged, dh1, "tn", "d_w_out")
    dmerged = _matmul(dh1, wout_f, "nt", "d_merged")
    dpa, dpl, dga, dgl, g_ba, g_bl = _merge_bwd(dmerged, z, b_gates, pa, plru, ga_off, "merge_bwd")
    g_wap = _matmul(attn, dpa, "tn", "d_w_attn_proj")
    dattn = _matmul(dpa, wap_f, "nt", "d_attn")
    g_wlp = _matmul(rec, dpl, "tn", "d_w_lru_proj")
    drec = _matmul(dpl, wlp_f, "nt", "d_rec")
    dgr, drp, dip, duc_direct, g_lam, g_br, g_bi = _lru_bwd(drec, hst, uc, rpre, ipre, zgr, b_rgate, b_igate,
                                                             lru_lambda, "lru_bwd")
    duc = _gates_bwd_x(duc_direct, drp, dip, wr_dense, wi_dense, "gates_bwd_x")
    g_wr = _diag_blocks(_gates_bwd_w(uc, drp, ng, gw, "gates_bwd_wr"), n_blocks, bw)
    g_wi = _diag_blocks(_gates_bwd_w(uc, dip, ng, gw, "gates_bwd_wi"), n_blocks, bw)
    du, g_convw, g_convb = _conv_bwd(duc, zu, conv_w_full, "conv_bwd")
    dq, dk, dv, g_qg, g_kg, g_sinks = _attn_bwd(dattn, z, cos_t, sin_t, q_norm_g, k_norm_g, sinks1, kv, grp, hd,
                                                 "attn_bwd")
    dz = jnp.concatenate([dq, dk, dv, du, dgr, dga, dgl], axis=1).astype(BF16)
    g_win = _matmul(xn, dz, "tn", "d_w_in")
    dxn = _matmul(dz, win_f, "nt", "d_xn")
    dx, g_norm1 = _rms_bwd(dxn, x2, norm1_g, dh1, "rms1_bwd")

    small_grads = {"norm1_g": g_norm1, "b_gates": jnp.concatenate([g_ba, g_bl], axis=1), "q_norm_g": g_qg,
                   "k_norm_g": g_kg, "sinks": g_sinks[:, :nq], "conv_w": g_convw, "conv_b": g_convb,
                   "w_rgate": g_wr, "b_rgate": g_br, "w_igate": g_wi, "b_igate": g_bi, "lru_lambda": g_lam,
                   "norm2_g": g_norm2}
    gshapes = [small_grads[n].shape for n in SMALL]
    grows = _pack_rows(gshapes)
    packed_g = _pack([small_grads[n] for n in SMALL], grows)
    partial = {"w_in": g_win, "w_attn_proj": g_wap, "w_lru_proj": g_wlp, "w_out": g_wout, "w_ffn_gate": g_wg,
               "w_ffn_up": g_wu, "w_ffn_down": g_wd}
    names = list(BIG) + ["small"]
    kinds8 = kinds + ["row"]
    views = []
    for n, kind in zip(names, kinds8):
        g = packed_g if n == "small" else partial[n]
        r, c = g.shape
        views.append(g.reshape(1, 2, r // 2, c) if kind == "col" else g.reshape(N_CHIPS, 2, r // (2 * N_CHIPS), c))
    recv = _run_exchange(_SiblingExchange(views), "grad_sibling_exchange")
    sums = [_add_own_half(v, rcv, idx, F32 if n == "small" else BF16, "grad_chip_sum_" + n)
            for v, rcv, n in zip(views, recv, names)]
    parts = _run_exchange(_ChipExchange(sums, kinds8), "grad_chip_exchange")
    finals = [_sum_chips(s, kind, p, idx, 2 * N_CHIPS if n == "small" else 2, n == "small", "grad_total_" + n)
              for s, kind, p, n in zip(sums, kinds8, parts, names)]
    reduced = _run_exchange(_FinishExchange(finals), "grad_finish_exchange")
    grads = dict(zip(BIG, reduced[:-1]))
    small_full = dict(zip(SMALL, _unpack(reduced[-1], gshapes)))
    per = d_rnn // N_CHIPS
    small_full["conv_w"] = lax.dynamic_slice(small_full["conv_w"], (0, chip * per), (taps, per))
    for n in SMALL:
        grads[n] = small_full[n]

    delta, new_m, new_v = {}, {}, {}
    for n in BIG:
        delta[n], new_m[n], new_v[n] = _adamw(w[n][0], grads[n], mom[n][0], var[n][0], "adamw_" + n)
    pshapes = [w[n].shape for n in SMALL]
    prows = _pack_rows(pshapes)
    pk = [_pack([src[n] for n in SMALL], prows) for src in (w, grads, mom, var)]
    for res, packed in zip((delta, new_m, new_v), _adamw(pk[0], pk[1], pk[2], pk[3], "adamw_small")):
        res.update(dict(zip(SMALL, _unpack(packed, pshapes))))

    outs = [loss, dx.reshape(x.shape)]
    for res in (grads, delta, new_m, new_v):
        outs += [res[n].reshape(w[n].shape) for n in WEIGHTS]
    return tuple(outs)


def _gather_small(shard, name):
    def body(s_ref, o_ref, send_sems, recv_sems):
        e = _Env((s_ref,), (o_ref,), send_sems, recv_sems)
        o_ref[e.me] = s_ref[...]
        for k, (cx, cy) in enumerate(e.chips):
            e.copy(s_ref, o_ref.at[e.me], k, (cx, cy, e.c)).start()
        for k, (cx, cy) in enumerate(e.chips):
            e.copy(s_ref, o_ref.at[2 * cx + cy], k, (cx, cy, e.c)).wait_recv()
        for k, (cx, cy) in enumerate(e.chips):
            e.copy(s_ref, o_ref.at[e.me], k, (cx, cy, e.c)).wait_send()

    vm = pl.BlockSpec(memory_space=pltpu.VMEM)
    return pl.pallas_call(body, name=name, out_shape=jax.ShapeDtypeStruct((N_CHIPS,) + shard.shape, shard.dtype),
                          in_specs=[vm], out_specs=vm,
                          scratch_shapes=[pltpu.SemaphoreType.DMA((3,)), pltpu.SemaphoreType.DMA((3,))])(shard)
```

```python
import functools
import math

import jax
import jax.numpy as jnp
from jax import lax
from jax.experimental import pallas as pl
from jax.experimental.pallas import tpu as pltpu

F32 = jnp.float32
BF16 = jnp.bfloat16
MESH = pl.DeviceIdType.MESH

WINDOW = 128
BLK = 128
ROPE_THETA = 500000.0
LRU_C = 8.0
EPS = 1e-6
NEG = -1e30
ADAM_LR = 0.001
ADAM_B1 = 0.9
ADAM_B2 = 0.999
ADAM_EPS = 1e-08
ADAM_WD = 0.01
ADAM_STEP = 10

VMEM_LIMIT_BYTES = 52 * 1024 * 1024
LANE = 128
SUBLANE = 8
N_CHIPS = 4
SMALL_PACK_COLS = 512


def _params(**kw):
    return pltpu.CompilerParams(vmem_limit_bytes=VMEM_LIMIT_BYTES, **kw)


def _pick(dim, cands):
    for c in cands:
        if dim % c == 0:
            return c
    return dim


def _sigmoid(x):
    return 1.0 / (1.0 + jnp.exp(-x))


ANY = pl.BlockSpec(memory_space=pl.ANY)


class _Env:
    def __init__(self, ins, outs, send, recv, sem0=0, place=None):
        self.ins, self.outs, self.send, self.recv, self.sem0 = ins, outs, send, recv, sem0
        self.x, self.y, self.c = place or (lax.axis_index("x"), lax.axis_index("y"), lax.axis_index("c"))
        self.me = 2 * self.x + self.y
        self.chips = [(1 - self.x, self.y), (self.x, 1 - self.y), (1 - self.x, 1 - self.y)]
        self.sibling = (self.x, self.y, 1 - self.c)

    def sub(self, i0, n_in, o0, n_out, sem0):
        return _Env(self.ins[i0:i0 + n_in], self.outs[o0:o0 + n_out], self.send, self.recv, self.sem0 + sem0,
                    (self.x, self.y, self.c))

    def copy(self, src, dst, sem, to):
        return pltpu.make_async_remote_copy(src_ref=src, dst_ref=dst, send_sem=self.send.at[self.sem0 + sem],
                                            recv_sem=self.recv.at[self.sem0 + sem], device_id=to, device_id_type=MESH)


class _Exchange:
    inputs, out_shapes, aliases, n_sems = (), (), {}, 0

    def start(self, e):
        raise NotImplementedError

    def finish(self, e):
        raise NotImplementedError


class _Jobs(_Exchange):
    def __init__(self, *jobs):
        self.jobs, self.inputs, self.out_shapes, self.aliases, self.n_sems, self.at = jobs, [], [], {}, 0, []
        for job in jobs:
            self.at.append((len(self.inputs), len(self.out_shapes), self.n_sems))
            self.aliases.update({len(self.inputs) + i: len(self.out_shapes) + o for i, o in job.aliases.items()})
            self.inputs += list(job.inputs)
            self.out_shapes += list(job.out_shapes)
            self.n_sems += job.n_sems

    def _each(self, e):
        for job, (i0, o0, s0) in zip(self.jobs, self.at):
            yield job, e.sub(i0, len(job.inputs), o0, len(job.out_shapes), s0)

    def split(self, outs):
        return [tuple(outs[o0:o0 + len(job.out_shapes)]) for job, (_, o0, _) in zip(self.jobs, self.at)]

    def start(self, e):
        for job, se in self._each(e):
            job.start(se)

    def finish(self, e):
        for job, se in self._each(e):
            job.finish(se)


def _call(body, name, out_shape, grid, in_specs, out_specs, args, scratch_shapes=(), job=None):
    if job is None:
        return pl.pallas_call(body, name=name, out_shape=out_shape, grid=grid, in_specs=list(in_specs),
                              out_specs=out_specs, scratch_shapes=list(scratch_shapes),
                              compiler_params=_params())(*args), ()
    single = not isinstance(out_shape, (tuple, list))
    shapes = [out_shape] if single else list(out_shape)
    ospecs = [out_specs] if single else list(out_specs)
    n_in, n_out, n_scr = len(args), len(shapes), len(scratch_shapes)
    j_in, j_out = len(job.inputs), len(job.out_shapes)

    def hosted(*refs):
        ins, jins = refs[:n_in], refs[n_in:n_in + j_in]
        outs = refs[n_in + j_in:n_in + j_in + n_out]
        jouts = refs[n_in + j_in + n_out:n_in + j_in + n_out + j_out]
        rest = refs[n_in + j_in + n_out + j_out:]
        e = _Env(jins, jouts, rest[n_scr], rest[n_scr + 1])
        first = functools.reduce(jnp.logical_and, [pl.program_id(d) == 0 for d in range(len(grid))])
        last = functools.reduce(jnp.logical_and, [pl.program_id(d) == g - 1 for d, g in enumerate(grid)])

        @pl.when(first)
        def _():
            job.start(e)

        body(*ins, *outs, *rest[:n_scr])

        @pl.when(last)
        def _():
            job.finish(e)

    res = pl.pallas_call(
        hosted, name=name, out_shape=tuple(shapes + list(job.out_shapes)), grid=grid,
        in_specs=list(in_specs) + [ANY] * j_in, out_specs=tuple(ospecs + [ANY] * j_out),
        scratch_shapes=list(scratch_shapes) + [pltpu.SemaphoreType.DMA((job.n_sems,)),
                                               pltpu.SemaphoreType.DMA((job.n_sems,))],
        input_output_aliases={n_in + i: n_out + o for i, o in job.aliases.items()},
        compiler_params=_params())(*args, *job.inputs)
    return (res[0] if single else tuple(res[:n_out])), tuple(res[n_out:])


def _run_exchange(job, name):
    n_in, n_out = len(job.inputs), len(job.out_shapes)

    def body(*refs):
        e = _Env(refs[:n_in], refs[n_in:n_in + n_out], refs[n_in + n_out], refs[n_in + n_out + 1])
        job.start(e)
        job.finish(e)

    return pl.pallas_call(
        body, name=name, out_shape=tuple(job.out_shapes), in_specs=[ANY] * n_in, out_specs=tuple([ANY] * n_out),
        input_output_aliases=dict(job.aliases),
        scratch_shapes=[pltpu.SemaphoreType.DMA((job.n_sems,)), pltpu.SemaphoreType.DMA((job.n_sems,))],
    )(*job.inputs)


_MN_TILES = (1024, 1408, 1280, 512, 256, 128)
_K_TILES = (1408, 1280, 1024, 512, 256, 128)


def _matmul(a, b, mode, name, add=None, out_dtype=F32, job=None):
    if mode == "nn":
        (m, k), (k2, n) = a.shape, b.shape
    elif mode == "nt":
        (m, k), (n, k2) = a.shape, b.shape
    else:
        (k, m), (k2, n) = a.shape, b.shape
    assert k == k2, (a.shape, b.shape, mode)
    tm, tn, tk = _pick(m, _MN_TILES), _pick(n, _MN_TILES), _pick(k, _K_TILES)
    nk = k // tk
    if mode == "nn":
        a_spec = pl.BlockSpec((tm, tk), lambda i, j, kk: (i, kk))
        b_spec = pl.BlockSpec((tk, tn), lambda i, j, kk: (kk, j))
        dims = (((1,), (0,)), ((), ()))
    elif mode == "nt":
        a_spec = pl.BlockSpec((tm, tk), lambda i, j, kk: (i, kk))
        b_spec = pl.BlockSpec((tn, tk), lambda i, j, kk: (j, kk))
        dims = (((1,), (1,)), ((), ()))
    else:
        a_spec = pl.BlockSpec((tk, tm), lambda i, j, kk: (kk, i))
        b_spec = pl.BlockSpec((tk, tn), lambda i, j, kk: (kk, j))
        dims = (((0,), (0,)), ((), ()))
    o_spec = pl.BlockSpec((tm, tn), lambda i, j, kk: (i, j))
    has_add = add is not None

    def body(*refs):
        if has_add:
            a_ref, b_ref, add_ref, o_ref, acc = refs
        else:
            a_ref, b_ref, o_ref, acc = refs
        kk = pl.program_id(2)

        @pl.when(kk == 0)
        def _():
            acc[...] = jnp.zeros_like(acc)

        acc[...] += lax.dot_general(a_ref[...].astype(BF16), b_ref[...].astype(BF16), dims,
                                    preferred_element_type=F32)

        @pl.when(kk == nk - 1)
        def _():
            r = acc[...]
            if has_add:
                r = r + add_ref[...]
            o_ref[...] = r.astype(out_dtype)

    in_specs = [a_spec, b_spec] + ([o_spec] if has_add else [])
    args = (a, b) + ((add,) if has_add else ())
    res, extra = _call(body, name, jax.ShapeDtypeStruct((m, n), out_dtype), (m // tm, n // tn, nk), in_specs, o_spec,
                       args, [pltpu.VMEM((tm, tn), F32)], job)
    return res if job is None else (res, extra)


def _row_tile(rows, cols, budget_elems=512 * 1024):
    cands = [c for c in (1024, 704, 512, 352, 256, 128, 64, 32, 16) if c * cols <= budget_elems]
    return _pick(rows, cands or (16,))


def _rms_fwd(x, g, name):
    t, d = x.shape
    tr = _row_tile(t, d)

    def body(x_ref, g_ref, o_ref):
        xv = x_ref[...]
        rstd = lax.rsqrt(jnp.mean(xv * xv, axis=-1, keepdims=True) + EPS)
        o_ref[...] = (xv * rstd * g_ref[...]).astype(BF16)

    spec = pl.BlockSpec((tr, d), lambda i: (i, 0))
    return pl.pallas_call(body, name=name, out_shape=jax.ShapeDtypeStruct((t, d), BF16), grid=(t // tr,),
                          in_specs=[spec, pl.BlockSpec((1, d), lambda i: (0, 0))], out_specs=spec,
                          compiler_params=_params())(x, g)


def _rms_bwd(dxn, x, g, resid, name, job=None):
    t, d = x.shape
    tr = _row_tile(t, d, 256 * 1024)

    def body(dxn_ref, x_ref, g_ref, r_ref, dx_ref, dg_ref):
        @pl.when(pl.program_id(0) == 0)
        def _():
            dg_ref[...] = jnp.zeros_like(dg_ref)

        xv = x_ref[...]
        rstd = lax.rsqrt(jnp.mean(xv * xv, axis=-1, keepdims=True) + EPS)
        xhat = xv * rstd
        dy = dxn_ref[...]
        dg_ref[...] += jnp.sum(dy * xhat, axis=0, keepdims=True)
        dxhat = dy * g_ref[...]
        dx_ref[...] = r_ref[...] + rstd * (dxhat - xhat * jnp.mean(dxhat * xhat, axis=-1, keepdims=True))

    spec = pl.BlockSpec((tr, d), lambda i: (i, 0))
    vec = pl.BlockSpec((1, d), lambda i: (0, 0))
    res, extra = _call(body, name, (jax.ShapeDtypeStruct((t, d), F32), jax.ShapeDtypeStruct((1, d), F32)), (t // tr,),
                       [spec, spec, vec, spec], (spec, vec), (dxn, x, g, resid), (), job)
    return res if job is None else (res, extra)


def _swiglu_fwd(gate, up, name):
    t, f = gate.shape
    tr = _row_tile(t, f, 256 * 1024)

    def body(g_ref, u_ref, o_ref):
        gv = g_ref[...]
        o_ref[...] = (gv * _sigmoid(gv) * u_ref[...]).astype(BF16)

    spec = pl.BlockSpec((tr, f), lambda i: (i, 0))
    return pl.pallas_call(body, name=name, out_shape=jax.ShapeDtypeStruct((t, f), BF16), grid=(t // tr,),
                          in_specs=[spec, spec], out_specs=spec, compiler_params=_params())(gate, up)


def _swiglu_bwd(dact, gate, up, name, job=None):
    t, f = gate.shape
    tr = _row_tile(t, f, 128 * 1024)

    def body(d_ref, g_ref, u_ref, dg_ref, du_ref):
        gv, dv = g_ref[...], d_ref[...]
        sg = _sigmoid(gv)
        dg_ref[...] = (dv * u_ref[...] * (sg * (1.0 + gv * (1.0 - sg)))).astype(BF16)
        du_ref[...] = (dv * (gv * sg)).astype(BF16)

    spec = pl.BlockSpec((tr, f), lambda i: (i, 0))
    shp = jax.ShapeDtypeStruct((t, f), BF16)
    res, extra = _call(body, name, (shp, shp), (t // tr,), [spec, spec, spec], (spec, spec), (dact, gate, up), (), job)
    return res if job is None else (res, extra)


def _merge_fwd(z, b_gates, pa, plru, ga_off, name):
    t, d = pa.shape
    cw = _pick(math.gcd(ga_off, d), (512, 256, 128))
    tr = _row_tile(t, cw, 256 * 1024)
    oa, ol, nd = ga_off // cw, (ga_off + d) // cw, d // cw

    def body(ga_ref, gl_ref, ba_ref, bl_ref, pa_ref, pl_ref, o_ref):
        sa = _sigmoid(ga_ref[...] + ba_ref[...])
        sl = _sigmoid(gl_ref[...] + bl_ref[...])
        o_ref[...] = (sa * pa_ref[...] + sl * pl_ref[...]).astype(BF16)

    blk = pl.BlockSpec((tr, cw), lambda i, j: (i, j))
    return pl.pallas_call(
        body, name=name, out_shape=jax.ShapeDtypeStruct((t, d), BF16), grid=(t // tr, nd),
        in_specs=[pl.BlockSpec((tr, cw), lambda i, j: (i, oa + j)), pl.BlockSpec((tr, cw), lambda i, j: (i, ol + j)),
                  pl.BlockSpec((1, cw), lambda i, j: (0, j)), pl.BlockSpec((1, cw), lambda i, j: (0, nd + j)),
                  blk, blk],
        out_specs=blk, compiler_params=_params(),
    )(z, z, b_gates, b_gates, pa, plru)


def _merge_bwd(dmerged, z, b_gates, pa, plru, ga_off, name, job=None):
    t, d = pa.shape
    cw = _pick(math.gcd(ga_off, d), (512, 256, 128))
    tr = _row_tile(t, cw, 256 * 1024)
    oa, ol, nd = ga_off // cw, (ga_off + d) // cw, d // cw

    def body(dm_ref, ga_ref, gl_ref, ba_ref, bl_ref, pa_ref, pl_ref, dpa_ref, dpl_ref, dga_ref, dgl_ref, sa_ref, sl_ref):
        @pl.when(pl.program_id(1) == 0)
        def _():
            sa_ref[...] = jnp.zeros_like(sa_ref)
            sl_ref[...] = jnp.zeros_like(sl_ref)

        dm = dm_ref[...]
        sa = _sigmoid(ga_ref[...] + ba_ref[...])
        sl = _sigmoid(gl_ref[...] + bl_ref[...])
        dpa_ref[...] = (dm * sa).astype(BF16)
        dpl_ref[...] = (dm * sl).astype(BF16)
        dga = dm * pa_ref[...] * (sa * (1.0 - sa))
        dgl = dm * pl_ref[...] * (sl * (1.0 - sl))
        dga_ref[...] = dga
        dgl_ref[...] = dgl
        sa_ref[...] += jnp.sum(dga, axis=0, keepdims=True)
        sl_ref[...] += jnp.sum(dgl, axis=0, keepdims=True)

    blk = pl.BlockSpec((tr, cw), lambda j, i: (i, j))
    vec = pl.BlockSpec((1, cw), lambda j, i: (0, j))
    big16, big32, v32 = (jax.ShapeDtypeStruct((t, d), BF16), jax.ShapeDtypeStruct((t, d), F32),
                         jax.ShapeDtypeStruct((1, d), F32))
    res, extra = _call(
        body, name, (big16, big16, big32, big32, v32, v32), (nd, t // tr),
        [blk, pl.BlockSpec((tr, cw), lambda j, i: (i, oa + j)), pl.BlockSpec((tr, cw), lambda j, i: (i, ol + j)),
         vec, pl.BlockSpec((1, cw), lambda j, i: (0, nd + j)), blk, blk],
        (blk, blk, blk, blk, vec, vec), (dmerged, z, z, b_gates, b_gates, pa, plru), (), job)
    return res if job is None else (res, extra)


def _loss_head(y, target, name):
    t, d = y.shape
    tr = _row_tile(t, d, 256 * 1024)
    nt = t // tr

    def body(y_ref, t_ref, dy_ref, loss_ref, acc):
        i = pl.program_id(0)

        @pl.when(i == 0)
        def _():
            acc[...] = jnp.zeros_like(acc)

        e = y_ref[...] - t_ref[...]
        dy_ref[...] = e * (1.0 / d)
        acc[...] += jnp.sum(e * e, axis=0, keepdims=True)

        @pl.when(i == nt - 1)
        def _():
            loss_ref[...] = (0.5 / d) * jnp.sum(acc[...], axis=-1, keepdims=True)

    spec = pl.BlockSpec((tr, d), lambda i: (i, 0))
    return pl.pallas_call(
        body, name=name, out_shape=(jax.ShapeDtypeStruct((t, d), F32), jax.ShapeDtypeStruct((1, 1), F32)),
        grid=(nt,), in_specs=[spec, spec], out_specs=(spec, pl.BlockSpec((1, 1), lambda i: (0, 0))),
        scratch_shapes=[pltpu.VMEM((1, d), F32)], compiler_params=_params(),
    )(y, target)


def _adamw(w, g, m, v, name):
    r, c = w.shape
    tr = _row_tile(r, c, 128 * 1024)
    c1 = 1.0 - ADAM_B1 ** ADAM_STEP
    c2 = 1.0 - ADAM_B2 ** ADAM_STEP

    def body(w_ref, g_ref, m_ref, v_ref, d_ref, nm_ref, nv_ref):
        gv = g_ref[...]
        mn = ADAM_B1 * m_ref[...] + (1.0 - ADAM_B1) * gv
        vn = ADAM_B2 * v_ref[...] + (1.0 - ADAM_B2) * (gv * gv)
        d_ref[...] = -ADAM_LR * ((mn / c1) / (jnp.sqrt(vn / c2) + ADAM_EPS) + ADAM_WD * w_ref[...])
        nm_ref[...] = mn
        nv_ref[...] = vn

    spec = pl.BlockSpec((tr, c), lambda i: (i, 0))
    shp = jax.ShapeDtypeStruct((r, c), F32)
    return pl.pallas_call(body, name=name, out_shape=(shp, shp, shp), grid=(r // tr,), in_specs=[spec] * 4,
                          out_specs=(spec, spec, spec), compiler_params=_params())(w, g, m, v)


def _swap_halves(v, lane, half):
    n = v.shape[-1]
    return jnp.where(lane < half, pltpu.roll(v, n - half, 1),
                     jnp.where(lane < 2 * half, pltpu.roll(v, half, 1), 0.0))


def _norm_fwd(xraw, g):
    rstd = lax.rsqrt(jnp.mean(xraw * xraw, axis=-1, keepdims=True) + EPS)
    xhat = xraw * rstd
    return xhat, rstd, xhat * g


def _norm_bwd(dy, xhat, rstd, g):
    dxhat = dy * g
    dx = rstd * (dxhat - xhat * jnp.mean(dxhat * xhat, axis=-1, keepdims=True))
    return dx, jnp.sum(dy * xhat, axis=0, keepdims=True)


def _attn_specs(nb, grp, hd, kv, clamp):
    qo, ko, vo = 0, (kv * grp), (kv * grp + kv)
    cur = (lambda i: jnp.minimum(i, nb - 1)) if clamp else (lambda i: i)
    prev = lambda i: jnp.maximum(cur(i) - 1, 0)
    zq = pl.BlockSpec((BLK, grp * hd), lambda h, i: (cur(i), h))
    kc = pl.BlockSpec((BLK, hd), lambda h, i: (cur(i), ko + h))
    kp = pl.BlockSpec((BLK, hd), lambda h, i: (prev(i), ko + h))
    vc = pl.BlockSpec((BLK, hd), lambda h, i: (cur(i), vo + h))
    vp = pl.BlockSpec((BLK, hd), lambda h, i: (prev(i), vo + h))
    tc = pl.BlockSpec((BLK, hd), lambda h, i: (cur(i), 0))
    tp = pl.BlockSpec((BLK, hd), lambda h, i: (prev(i), 0))
    gs = pl.BlockSpec((1, hd), lambda h, i: (0, 0))
    return zq, kc, kp, vc, vp, tc, tp, gs


def _attn_mask(i):
    qi = lax.broadcasted_iota(jnp.int32, (BLK, 2 * BLK), 0)
    kj = lax.broadcasted_iota(jnp.int32, (BLK, 2 * BLK), 1)
    rel = qi + BLK - kj
    return (rel >= 0) & (rel < WINDOW) & ((kj >= BLK) | (i > 0))


def _attn_fwd(z, cos_t, sin_t, qg, kg, sinks, kv, grp, hd, name, job=None):
    t = z.shape[0]
    nb = t // BLK
    half = hd // 8
    scale = 1.0 / math.sqrt(hd)
    zq, kc, kp, vc, vp, tc, tp, gs = _attn_specs(nb, grp, hd, kv, False)

    def body(sink_ref, zq_ref, kc_ref, kp_ref, vc_ref, vp_ref, cc_ref, sc_ref, cp_ref, sp_ref, qg_ref, kg_ref, o_ref):
        h, i = pl.program_id(0), pl.program_id(1)
        lane = lax.broadcasted_iota(jnp.int32, (BLK, hd), 1)

        def normrope(xraw, g, c, s):
            y = _norm_fwd(xraw, g)[2]
            return y * c + _swap_halves(y, lane, half) * s

        cc, sc = cc_ref[...], sc_ref[...]
        kcur = normrope(kc_ref[...], kg_ref[...], cc, sc)
        kprev = normrope(kp_ref[...], kg_ref[...], cp_ref[...], sp_ref[...])
        kk = jnp.concatenate([kprev, kcur], axis=0).astype(BF16)
        vv = jnp.concatenate([vp_ref[...], vc_ref[...]], axis=0).astype(BF16)
        mask = _attn_mask(i)
        for g in range(grp):
            q = normrope(zq_ref[:, g * hd:(g + 1) * hd], qg_ref[...], cc, sc).astype(BF16)
            s = lax.dot_general(q, kk, (((1,), (1,)), ((), ())), preferred_element_type=F32) * scale
            s = jnp.where(mask, s, NEG)
            sk = sink_ref[h * grp + g]
            mx = jnp.maximum(jnp.max(s, axis=-1, keepdims=True), sk)
            p = jnp.exp(s - mx)
            den = jnp.sum(p, axis=-1, keepdims=True) + jnp.exp(sk - mx)
            p = p / den
            o_ref[:, g * hd:(g + 1) * hd] = jnp.dot(p.astype(BF16), vv, preferred_element_type=F32).astype(BF16)

    res, extra = _call(
        body, name, jax.ShapeDtypeStruct((t, kv * grp * hd), BF16), (kv, nb),
        [pl.BlockSpec(memory_space=pltpu.SMEM), zq, kc, kp, vc, vp, tc, tc, tp, tp, gs, gs],
        pl.BlockSpec((BLK, grp * hd), lambda h, i: (i, h)),
        (sinks, z, z, z, z, z, cos_t, sin_t, cos_t, sin_t, qg, kg), (), job)
    return res if job is None else (res, extra)


def _attn_bwd(dattn, z, cos_t, sin_t, qg, kg, sinks, kv, grp, hd, name):
    t = z.shape[0]
    nb = t // BLK
    half = hd // 8
    scale = 1.0 / math.sqrt(hd)
    zq, kc, kp, vc, vp, tc, tp, gs = _attn_specs(nb, grp, hd, kv, True)

    def body(sink_ref, zq_ref, kc_ref, kp_ref, vc_ref, vp_ref, cc_ref, sc_ref, cp_ref, sp_ref, qg_ref, kg_ref, do_ref,
             dq_ref, dk_ref, dv_ref, dqg_ref, dkg_ref, dsk_ref, dk_carry, dv_carry):
        h, i = pl.program_id(0), pl.program_id(1)
        lane = lax.broadcasted_iota(jnp.int32, (BLK, hd), 1)
        lane1 = lax.broadcasted_iota(jnp.int32, (1, LANE), 1)

        @pl.when((h == 0) & (i == 0))
        def _():
            dqg_ref[...] = jnp.zeros_like(dqg_ref)
            dkg_ref[...] = jnp.zeros_like(dkg_ref)
            dsk_ref[...] = jnp.zeros_like(dsk_ref)

        @pl.when(i == 0)
        def _():
            dk_carry[...] = jnp.zeros_like(dk_carry)
            dv_carry[...] = jnp.zeros_like(dv_carry)

        def rope(y, c, s):
            return y * c + _swap_halves(y, lane, half) * s

        def rope_bwd(dout, c, s):
            return dout * c + _swap_halves(dout * s, lane, half)

        @pl.when(i < nb)
        def _():
            cc, sc, cp, sp = cc_ref[...], sc_ref[...], cp_ref[...], sp_ref[...]
            qgv, kgv = qg_ref[...], kg_ref[...]
            xh_kc, rs_kc, y_kc = _norm_fwd(kc_ref[...], kgv)
            xh_kp, rs_kp, y_kp = _norm_fwd(kp_ref[...], kgv)
            kk = jnp.concatenate([rope(y_kp, cp, sp), rope(y_kc, cc, sc)], axis=0).astype(BF16)
            vv = jnp.concatenate([vp_ref[...], vc_ref[...]], axis=0).astype(BF16)
            mask = _attn_mask(i)
            dkk = jnp.zeros((2 * BLK, hd), F32)
            dvv = jnp.zeros((2 * BLK, hd), F32)
            dqg = jnp.zeros((1, hd), F32)
            dsk = jnp.zeros((1, LANE), F32)
            for g in range(grp):
                xh_q, rs_q, y_q = _norm_fwd(zq_ref[:, g * hd:(g + 1) * hd], qgv)
                q = rope(y_q, cc, sc).astype(BF16)
                s = lax.dot_general(q, kk, (((1,), (1,)), ((), ())), preferred_element_type=F32) * scale
                s = jnp.where(mask, s, NEG)
                sk = sink_ref[h * grp + g]
                mx = jnp.maximum(jnp.max(s, axis=-1, keepdims=True), sk)
                p = jnp.exp(s - mx)
                den = jnp.sum(p, axis=-1, keepdims=True) + jnp.exp(sk - mx)
                p = p / den
                psink = jnp.exp(sk - mx) / den
                dog = do_ref[:, g * hd:(g + 1) * hd].astype(BF16)
                dp = lax.dot_general(dog, vv, (((1,), (1,)), ((), ())), preferred_element_type=F32)
                rsum = jnp.sum(p * dp, axis=-1, keepdims=True)
                ds = (p * (dp - rsum) * scale).astype(BF16)
                dsk = dsk + jnp.where(lane1 == h * grp + g, jnp.sum(-psink * rsum, axis=0, keepdims=True), 0.0)
                dqn = jnp.dot(ds, kk, preferred_element_type=F32)
                dkk = dkk + lax.dot_general(ds, q, (((0,), (0,)), ((), ())), preferred_element_type=F32)
                dvv = dvv + lax.dot_general(p.astype(BF16), dog, (((0,), (0,)), ((), ())), preferred_element_type=F32)
                dxq, dg_q = _norm_bwd(rope_bwd(dqn, cc, sc), xh_q, rs_q, qgv)
                dq_ref[:, g * hd:(g + 1) * hd] = dxq
                dqg = dqg + dg_q
            dkp_raw, dg_kp = _norm_bwd(rope_bwd(dkk[:BLK], cp, sp), xh_kp, rs_kp, kgv)
            dkc_raw, dg_kc = _norm_bwd(rope_bwd(dkk[BLK:], cc, sc), xh_kc, rs_kc, kgv)
            dk_ref[...] = dk_carry[...] + dkp_raw
            dv_ref[...] = dv_carry[...] + dvv[:BLK]
            dk_carry[...] = dkc_raw
            dv_carry[...] = dvv[BLK:]
            dqg_ref[...] += dqg
            dkg_ref[...] += dg_kp + dg_kc
            dsk_ref[...] += dsk

        @pl.when(i == nb)
        def _():
            dk_ref[...] = dk_carry[...]
            dv_ref[...] = dv_carry[...]

    kvw = kv * hd
    vec = pl.BlockSpec((1, hd), lambda h, i: (0, 0))
    shifted = pl.BlockSpec((BLK, hd), lambda h, i: (jnp.maximum(i - 1, 0), h))
    return pl.pallas_call(
        body, name=name,
        out_shape=(jax.ShapeDtypeStruct((t, kv * grp * hd), F32), jax.ShapeDtypeStruct((t, kvw), F32),
                   jax.ShapeDtypeStruct((t, kvw), F32), jax.ShapeDtypeStruct((1, hd), F32),
                   jax.ShapeDtypeStruct((1, hd), F32), jax.ShapeDtypeStruct((1, LANE), F32)),
        grid=(kv, nb + 1),
        in_specs=[pl.BlockSpec(memory_space=pltpu.SMEM), zq, kc, kp, vc, vp, tc, tc, tp, tp, gs, gs,
                  pl.BlockSpec((BLK, grp * hd), lambda h, i: (jnp.minimum(i, nb - 1), h))],
        out_specs=(pl.BlockSpec((BLK, grp * hd), lambda h, i: (jnp.minimum(i, nb - 1), h)), shifted, shifted, vec, vec,
                   pl.BlockSpec((1, LANE), lambda h, i: (0, 0))),
        scratch_shapes=[pltpu.VMEM((BLK, hd), F32), pltpu.VMEM((BLK, hd), F32)], compiler_params=_params(),
    )(sinks, z, z, z, z, z, cos_t, sin_t, cos_t, sin_t, qg, kg, dattn)


def _conv_fwd(u, w, b, name):
    t, c = u.shape
    taps = w.shape[0]
    cb = _pick(c, (1408, 1024, 512, 256, 128))
    tr = _row_tile(t, cb, 256 * 1024)
    hb = tr // SUBLANE

    def body(u_ref, halo_ref, w_ref, b_ref, o_ref):
        i = pl.program_id(0)
        x = u_ref[...]
        acc = b_ref[...] + w_ref[taps - 1:taps, :] * x
        for k in range(taps - 1):
            acc = acc + w_ref[k:k + 1, :] * pltpu.roll(x, taps - 1 - k, 0)
        o_ref[...] = acc
        row = lax.broadcasted_iota(jnp.int32, (SUBLANE, cb), 0)
        hp = jnp.where(i > 0, halo_ref[...], 0.0)
        x8 = u_ref[0:SUBLANE, :]
        acc8 = b_ref[...] + w_ref[taps - 1:taps, :] * x8
        for k in range(taps - 1):
            s = taps - 1 - k
            acc8 = acc8 + w_ref[k:k + 1, :] * jnp.where(row < s, pltpu.roll(hp, s, 0), pltpu.roll(x8, s, 0))
        o_ref[0:SUBLANE, :] = acc8

    blk = pl.BlockSpec((tr, cb), lambda i, j: (i, j))
    return pl.pallas_call(
        body, name=name, out_shape=jax.ShapeDtypeStruct((t, c), F32), grid=(t // tr, c // cb),
        in_specs=[blk, pl.BlockSpec((SUBLANE, cb), lambda i, j: (jnp.maximum(i * hb - 1, 0), j)),
                  pl.BlockSpec((taps, cb), lambda i, j: (0, j)), pl.BlockSpec((1, cb), lambda i, j: (0, j))],
        out_specs=blk, compiler_params=_params(),
    )(u, u, w, b)


def _conv_bwd(duc, u, w, name):
    t, c = u.shape
    taps = w.shape[0]
    cb = _pick(c, (1408, 1024, 512, 256, 128))
    tr = _row_tile(t, cb, 256 * 1024)
    hb, nt = tr // SUBLANE, t // tr

    def body(g_ref, gnext_ref, u_ref, uprev_ref, w_ref, du_ref, dw_ref, db_ref):
        i = pl.program_id(1)

        @pl.when(i == 0)
        def _():
            dw_ref[...] = jnp.zeros_like(dw_ref)
            db_ref[...] = jnp.zeros_like(db_ref)

        row = lax.broadcasted_iota(jnp.int32, (SUBLANE, cb), 0)
        g, x = g_ref[...], u_ref[...]
        du = w_ref[taps - 1:taps, :] * g
        for k in range(taps - 1):
            du = du + w_ref[k:k + 1, :] * pltpu.roll(g, tr - (taps - 1 - k), 0)
        du_ref[...] = du
        hn = jnp.where(i < nt - 1, gnext_ref[...], 0.0)
        g8 = g_ref[tr - SUBLANE:tr, :]
        du8 = w_ref[taps - 1:taps, :] * g8
        for k in range(taps - 1):
            s = taps - 1 - k
            du8 = du8 + w_ref[k:k + 1, :] * jnp.where(row >= SUBLANE - s, pltpu.roll(hn, SUBLANE - s, 0),
                                                     pltpu.roll(g8, SUBLANE - s, 0))
        du_ref[tr - SUBLANE:tr, :] = du8

        hp = jnp.where(i > 0, uprev_ref[...], 0.0)
        xl8, gf8 = u_ref[tr - SUBLANE:tr, :], g_ref[0:SUBLANE, :]
        db_ref[...] += jnp.sum(g, axis=0, keepdims=True)
        dw_ref[taps - 1:taps, :] += jnp.sum(g * x, axis=0, keepdims=True)
        for k in range(taps - 1):
            s = taps - 1 - k
            fix = jnp.where(row < s, pltpu.roll(hp, s, 0) - pltpu.roll(xl8, s, 0), 0.0)
            dw_ref[k:k + 1, :] += (jnp.sum(g * pltpu.roll(x, s, 0), axis=0, keepdims=True)
                                   + jnp.sum(gf8 * fix, axis=0, keepdims=True))

    blk = pl.BlockSpec((tr, cb), lambda j, i: (i, j))
    nh = t // SUBLANE
    return pl.pallas_call(
        body, name=name,
        out_shape=(jax.ShapeDtypeStruct((t, c), F32), jax.ShapeDtypeStruct((taps, c), F32),
                   jax.ShapeDtypeStruct((1, c), F32)),
        grid=(c // cb, nt),
        in_specs=[blk, pl.BlockSpec((SUBLANE, cb), lambda j, i: (jnp.minimum((i + 1) * hb, nh - 1), j)),
                  blk, pl.BlockSpec((SUBLANE, cb), lambda j, i: (jnp.maximum(i * hb - 1, 0), j)),
                  pl.BlockSpec((taps, cb), lambda j, i: (0, j))],
        out_specs=(blk, pl.BlockSpec((taps, cb), lambda j, i: (0, j)), pl.BlockSpec((1, cb), lambda j, i: (0, j))),
        compiler_params=_params(),
    )(duc, duc, u, u, w)


def _dense_groups(w, gw):
    n, bw, _ = w.shape
    per = gw // bw
    w4 = w.reshape(n // per, per, bw, bw).astype(BF16)
    eye = jnp.eye(per, dtype=BF16)
    return (w4[:, :, :, None, :] * eye[None, :, None, :, None]).reshape(n // per, gw, gw)


def _diag_blocks(dense, n, bw):
    ng, gw, _ = dense.shape
    per = gw // bw
    diag = jnp.diagonal(dense.reshape(ng, per, bw, per, bw), axis1=1, axis2=3)
    return jnp.moveaxis(diag, -1, 1).reshape(n, bw, bw)


def _gates_fwd(uc, wr, wi, name):
    t, c = uc.shape
    ng, gw, _ = wr.shape
    tr = _pick(t, (512, 256, 128))

    def body(u_ref, wr_ref, wi_ref, r_ref, i_ref):
        a = u_ref[...].astype(BF16)
        r_ref[...] = jnp.dot(a, wr_ref[...], preferred_element_type=F32)
        i_ref[...] = jnp.dot(a, wi_ref[...], preferred_element_type=F32)

    blk = pl.BlockSpec((tr, gw), lambda h, i: (i, h))
    wsp = pl.BlockSpec((None, gw, gw), lambda h, i: (h, 0, 0))
    shp = jax.ShapeDtypeStruct((t, c), F32)
    return pl.pallas_call(body, name=name, out_shape=(shp, shp), grid=(ng, t // tr), in_specs=[blk, wsp, wsp],
                          out_specs=(blk, blk), compiler_params=_params())(uc, wr, wi)


def _gates_bwd_x(duc, drp, dip, wr, wi, name):
    t, c = duc.shape
    ng, gw, _ = wr.shape
    tr = _pick(t, (512, 256, 128))
    dims = (((1,), (1,)), ((), ()))

    def body(d_ref, r_ref, i_ref, wr_ref, wi_ref, o_ref):
        o_ref[...] = (d_ref[...]
                      + lax.dot_general(r_ref[...].astype(BF16), wr_ref[...], dims, preferred_element_type=F32)
                      + lax.dot_general(i_ref[...].astype(BF16), wi_ref[...], dims, preferred_element_type=F32))

    blk = pl.BlockSpec((tr, gw), lambda h, i: (i, h))
    wsp = pl.BlockSpec((None, gw, gw), lambda h, i: (h, 0, 0))
    return pl.pallas_call(body, name=name, out_shape=jax.ShapeDtypeStruct((t, c), F32), grid=(ng, t // tr),
                          in_specs=[blk, blk, blk, wsp, wsp], out_specs=blk, compiler_params=_params())(duc, drp, dip, wr, wi)


def _gates_bwd_w(uc, dpre, ng, gw, name):
    t, c = uc.shape
    tk = _pick(t, (512, 256, 128))
    dims = (((0,), (0,)), ((), ()))

    def body(u_ref, d_ref, o_ref):
        @pl.when(pl.program_id(1) == 0)
        def _():
            o_ref[...] = jnp.zeros_like(o_ref)

        o_ref[...] += lax.dot_general(u_ref[...].astype(BF16), d_ref[...].astype(BF16), dims, preferred_element_type=F32)

    blk = pl.BlockSpec((tk, gw), lambda h, i: (i, h))
    return pl.pallas_call(body, name=name, out_shape=jax.ShapeDtypeStruct((ng, gw, gw), F32), grid=(ng, t // tk),
                          in_specs=[blk, blk], out_specs=pl.BlockSpec((None, gw, gw), lambda h, i: (h, 0, 0)),
                          compiler_params=_params())(uc, dpre)


def _softplus(x):
    return jnp.maximum(x, 0.0) + jnp.log(1.0 + jnp.exp(-jnp.abs(x)))


def _neg_expm1(x):
    series = x * (1.0 + x * (0.5 + x * (1.0 / 6.0 + x * (1.0 / 24.0 + x * (1.0 / 120.0)))))
    return -jnp.where(x > -0.05, series, jnp.exp(x) - 1.0)


_GELU_C = math.sqrt(2.0 / math.pi)


def _gelu_parts(x):
    inner = _GELU_C * (x + 0.044715 * (x * x * x))
    th = jnp.tanh(inner)
    gelu = 0.5 * x * (1.0 + th)
    dgelu = 0.5 * (1.0 + th) + 0.5 * x * (1.0 - th * th) * (_GELU_C * (1.0 + 3.0 * 0.044715 * (x * x)))
    return gelu, dgelu


def _lru_gate_values(uc, rpre, ipre, br, bi, sp):
    r = _sigmoid(rpre + br)
    ig = _sigmoid(ipre + bi)
    a = jnp.exp(-LRU_C * r * sp)
    mult = jnp.sqrt(jnp.maximum(_neg_expm1(2.0 * (-LRU_C * r * sp)), 0.0))
    return r, ig, a, mult


def _lru_fwd(uc, rpre, ipre, gr, br, bi, lam, name, job=None):
    t, c = uc.shape
    cb = _pick(c, (1408, 1024, 512, 256, 128))
    tb = _pick(t, (512, 256, 128))
    ntile = tb // SUBLANE

    def body(uc_ref, r_ref, i_ref, gr_ref, br_ref, bi_ref, lam_ref, h_ref, rec_ref, carry):
        @pl.when(pl.program_id(1) == 0)
        def _():
            carry[...] = jnp.zeros_like(carry)

        sp = _softplus(-lam_ref[...])
        br, bi = br_ref[...], bi_ref[...]
        row = lax.broadcasted_iota(jnp.int32, (SUBLANE, cb), 0)

        def tile(k, c_in):
            sl = pl.ds(pl.multiple_of(k * SUBLANE, SUBLANE), SUBLANE)
            ucv = uc_ref[sl, :]
            _, ig, a, mult = _lru_gate_values(ucv, r_ref[sl, :], i_ref[sl, :], br, bi, sp)
            b = mult * (ig * ucv)
            for d in (1, 2, 4):
                a_s = jnp.where(row >= d, pltpu.roll(a, d, 0), 1.0)
                b_s = jnp.where(row >= d, pltpu.roll(b, d, 0), 0.0)
                b = a * b_s + b
                a = a * a_s
            hv = b + a * c_in
            h_ref[sl, :] = hv
            rec_ref[sl, :] = hv * _gelu_parts(gr_ref[sl, :])[0]
            return hv[SUBLANE - 1:SUBLANE, :]

        c_out = lax.fori_loop(0, ntile, tile, carry[0:1, :])
        carry[...] = jnp.broadcast_to(c_out, (SUBLANE, cb))

    blk = pl.BlockSpec((tb, cb), lambda j, i: (i, j))
    vec = pl.BlockSpec((1, cb), lambda j, i: (0, j))
    shp = jax.ShapeDtypeStruct((t, c), F32)
    res, extra = _call(body, name, (shp, shp), (c // cb, t // tb), [blk, blk, blk, blk, vec, vec, vec], (blk, blk),
                       (uc, rpre, ipre, gr, br, bi, lam), [pltpu.VMEM((SUBLANE, cb), F32)], job)
    return res if job is None else (res, extra)


def _lru_bwd(drec, hst, uc, rpre, ipre, gr, br, bi, lam, name, job=None):
    t, c = uc.shape
    cb = _pick(c, (1408, 1024, 512, 256, 128))
    tb = _pick(t, (256, 128))
    ntile, nt, hb = tb // SUBLANE, t // tb, tb // SUBLANE

    def body(drec_ref, h_ref, hprev_ref, uc_ref, r_ref, i_ref, gr_ref, br_ref, bi_ref, lam_ref,
             dgr_ref, drp_ref, dip_ref, duc_ref, dlam_ref, dbr_ref, dbi_ref, carry):
        step = pl.program_id(1)
        first_block = step == nt - 1

        @pl.when(step == 0)
        def _():
            carry[...] = jnp.zeros_like(carry)
            dlam_ref[...] = jnp.zeros_like(dlam_ref)
            dbr_ref[...] = jnp.zeros_like(dbr_ref)
            dbi_ref[...] = jnp.zeros_like(dbi_ref)

        lam = lam_ref[...]
        sp = _softplus(-lam)
        br, bi = br_ref[...], bi_ref[...]
        row = lax.broadcasted_iota(jnp.int32, (SUBLANE, cb), 0)
        halo = jnp.where(first_block, 0.0, hprev_ref[...])

        def tile(kk, state):
            c_p, acc_sp, acc_br, acc_bi = state
            k = ntile - 1 - kk
            sl = pl.ds(pl.multiple_of(k * SUBLANE, SUBLANE), SUBLANE)
            slp = pl.ds(pl.multiple_of(jnp.maximum(k - 1, 0) * SUBLANE, SUBLANE), SUBLANE)
            ucv = uc_ref[sl, :]
            r, ig, a, mult = _lru_gate_values(ucv, r_ref[sl, :], i_ref[sl, :], br, bi, sp)
            hv = h_ref[sl, :]
            below = jnp.where(k > 0, h_ref[slp, :], halo)
            hprev = jnp.where(row == 0, pltpu.roll(below, 1, 0), pltpu.roll(hv, 1, 0))
            gelu, dgelu = _gelu_parts(gr_ref[sl, :])
            drec = drec_ref[sl, :]
            dh = drec * gelu
            dgr_ref[sl, :] = drec * hv * dgelu
            pa, pb = a, a * dh
            for d in (1, 2, 4):
                a_s = jnp.where(row < SUBLANE - d, pltpu.roll(pa, SUBLANE - d, 0), 1.0)
                b_s = jnp.where(row < SUBLANE - d, pltpu.roll(pb, SUBLANE - d, 0), 0.0)
                pb = pa * b_s + pb
                pa = pa * a_s
            pv = pb + pa * c_p
            gt = dh + jnp.where(row == SUBLANE - 1, c_p, pltpu.roll(pv, SUBLANE - 1, 0))
            da = gt * hprev
            duc_ref[sl, :] = gt * mult * ig
            dmult = gt * ig * ucv
            dig = gt * mult * ucv
            dla = da * a - jnp.where(mult > 0.0, dmult * (a * a) / mult, 0.0)
            drp = dla * (-LRU_C * sp) * (r * (1.0 - r))
            dip = dig * (ig * (1.0 - ig))
            drp_ref[sl, :] = drp
            dip_ref[sl, :] = dip
            return pv[0:1, :], acc_sp + dla * (-LRU_C * r), acc_br + drp, acc_bi + dip

        zero = jnp.zeros((SUBLANE, cb), F32)
        c_out, acc_sp, acc_br, acc_bi = lax.fori_loop(0, ntile, tile, (carry[0:1, :], zero, zero, zero))
        carry[...] = jnp.broadcast_to(c_out, (SUBLANE, cb))
        dlam_ref[...] += jnp.sum(acc_sp, axis=0, keepdims=True) * (-_sigmoid(-lam))
        dbr_ref[...] += jnp.sum(acc_br, axis=0, keepdims=True)
        dbi_ref[...] += jnp.sum(acc_bi, axis=0, keepdims=True)

    blk = pl.BlockSpec((tb, cb), lambda j, i: (nt - 1 - i, j))
    vec = pl.BlockSpec((1, cb), lambda j, i: (0, j))
    halo_spec = pl.BlockSpec((SUBLANE, cb), lambda j, i: (jnp.maximum((nt - 1 - i) * hb - 1, 0), j))
    big, small = jax.ShapeDtypeStruct((t, c), F32), jax.ShapeDtypeStruct((1, c), F32)
    res, extra = _call(
        body, name, (big, big, big, big, small, small, small), (c // cb, nt),
        [blk, blk, halo_spec, blk, blk, blk, blk, vec, vec, vec], (blk, blk, blk, blk, vec, vec, vec),
        (drec, hst, hst, uc, rpre, ipre, gr, br, bi, lam), [pltpu.VMEM((SUBLANE, cb), F32)], job)
    return res if job is None else (res, extra)


def _shard_region(ref, kind, chip, half, rh, width):
    if kind == "col":
        return ref.at[pl.ds(half * rh, rh), pl.ds(chip * width, width)]
    return ref.at[pl.ds(chip * (2 * rh) + half * rh, rh), :]


class _AllGather(_Exchange):
    def __init__(self, fulls, kinds):
        self.inputs, self.kinds = list(fulls), kinds
        self.out_shapes = [jax.ShapeDtypeStruct(f.shape, f.dtype) for f in fulls]
        self.aliases = {a: a for a in range(len(fulls))}
        self.n_sems = 6 * len(fulls)
        self.geo = [(f.shape[0] // 2, f.shape[1] // N_CHIPS) if k == "col" else (f.shape[0] // (2 * N_CHIPS), f.shape[1])
                    for f, k in zip(fulls, kinds)]

    def _region(self, ref, a, chip, half):
        return _shard_region(ref, self.kinds[a], chip, half, *self.geo[a])

    def _ici(self, e, a, k, chip):
        cx, cy = e.chips[k]
        return e.copy(self._region(e.ins[a], a, chip, e.c), self._region(e.outs[a], a, chip, e.c), a * 6 + k,
                      (cx, cy, e.c))

    def _d2d(self, e, a, k, half):
        cx, cy = e.chips[k]
        region = self._region(e.outs[a], a, 2 * cx + cy, half)
        return e.copy(region, region, a * 6 + 3 + k, e.sibling)

    def start(self, e):
        for a in range(len(self.inputs)):
            for k in range(3):
                self._ici(e, a, k, e.me).start()

    def finish(self, e):
        n = len(self.inputs)
        for a in range(n):
            for k, (cx, cy) in enumerate(e.chips):
                self._ici(e, a, k, 2 * cx + cy).wait_recv()
                self._d2d(e, a, k, e.c).start()
        for a in range(n):
            for k in range(3):
                self._d2d(e, a, k, 1 - e.c).wait_recv()
        for a in range(n):
            for k in range(3):
                self._ici(e, a, k, e.me).wait_send()
                self._d2d(e, a, k, e.c).wait_send()


class _SiblingExchange(_Exchange):
    def __init__(self, grads):
        self.inputs = list(grads)
        self.out_shapes = [jax.ShapeDtypeStruct((g.shape[0],) + g.shape[2:], g.dtype) for g in grads]
        self.n_sems = len(grads)

    def _copy(self, e, a):
        return e.copy(e.ins[a].at[:, 1 - e.c], e.outs[a], a, e.sibling)

    def start(self, e):
        for a in range(len(self.inputs)):
            self._copy(e, a).start()

    def finish(self, e):
        for a in range(len(self.inputs)):
            self._copy(e, a).wait()


def _piece(ref, kind, chip, width):
    if kind == "col":
        return ref.at[0, :, pl.ds(chip * width, width)]
    return ref.at[chip]


class _ChipExchange(_Exchange):
    def __init__(self, sums, kinds):
        self.inputs, self.kinds = list(sums), kinds
        self.widths = [s.shape[2] // N_CHIPS if k == "col" else s.shape[2] for s, k in zip(sums, kinds)]
        self.out_shapes = [jax.ShapeDtypeStruct((3, s.shape[1], w), s.dtype) for s, w in zip(sums, self.widths)]
        self.n_sems = 3 * len(sums)

    def _copy(self, e, a, k, chip):
        cx, cy = e.chips[k]
        return e.copy(_piece(e.ins[a], self.kinds[a], chip, self.widths[a]), e.outs[a].at[k], a * 3 + k, (cx, cy, e.c))

    def start(self, e):
        for a in range(len(self.inputs)):
            for k, (cx, cy) in enumerate(e.chips):
                self._copy(e, a, k, 2 * cx + cy).start()

    def finish(self, e):
        for a in range(len(self.inputs)):
            for k, (cx, cy) in enumerate(e.chips):
                self._copy(e, a, k, 2 * cx + cy).wait()


class _FinishExchange(_Exchange):
    def __init__(self, finals):
        self.inputs = list(finals)
        self.out_shapes = [jax.ShapeDtypeStruct(f.shape, f.dtype) for f in finals]
        self.aliases = {a: a for a in range(len(finals))}
        self.n_big = len(finals) - 1
        self.n_sems = self.n_big + 7
        self.rel = [(fx, fy, fc) for fx in (0, 1) for fy in (0, 1) for fc in (0, 1)][1:]

    def _big(self, e, a, mine):
        rh = self.inputs[a].shape[0] // 2
        rows = pl.ds((e.c if mine else 1 - e.c) * rh, rh)
        return e.copy((e.ins[a] if mine else e.outs[a]).at[rows, :], e.outs[a].at[rows, :], a, e.sibling)

    def _small(self, e, r, mine):
        fx, fy, fc = self.rel[r]
        px, py, pc = (1 - e.x if fx else e.x), (1 - e.y if fy else e.y), (1 - e.c if fc else e.c)
        rh = self.inputs[-1].shape[0] // (2 * N_CHIPS)
        slot = (2 * e.me + e.c) if mine else (2 * (2 * px + py) + pc)
        rows = pl.ds(slot * rh, rh)
        return e.copy((e.ins[-1] if mine else e.outs[-1]).at[rows, :], e.outs[-1].at[rows, :], self.n_big + r,
                      (px, py, pc))

    def start(self, e):
        for a in range(self.n_big):
            self._big(e, a, True).start()
        for r in range(7):
            self._small(e, r, True).start()

    def finish(self, e):
        for a in range(self.n_big):
            self._big(e, a, False).wait_recv()
        for r in range(7):
            self._small(e, r, False).wait_recv()
        for a in range(self.n_big):
            self._big(e, a, True).wait_send()
        for r in range(7):
            self._small(e, r, True).wait_send()


def _cast_into_full(w, kind, idx, name):
    r, c = w.shape
    tr = _row_tile(r, c)
    nrb = r // tr

    def body(idx_ref, w_ref, o_ref):
        o_ref[...] = w_ref[...].astype(BF16)

    if kind == "col":
        full, out_map = (r, N_CHIPS * c), (lambda i, idx_ref: (i, idx_ref[1]))
    else:
        full, out_map = (N_CHIPS * r, c), (lambda i, idx_ref: (idx_ref[1] * nrb + i, 0))
    return pl.pallas_call(
        body, name=name, out_shape=jax.ShapeDtypeStruct(full, BF16),
        grid_spec=pltpu.PrefetchScalarGridSpec(
            num_scalar_prefetch=1, grid=(nrb,), in_specs=[pl.BlockSpec((tr, c), lambda i, idx_ref: (i, 0))],
            out_specs=pl.BlockSpec((tr, c), out_map)),
        compiler_params=_params(),
    )(idx, w)


_EW_COLS = (1280, 1408, 1024, 640, 512, 256, 128)


def _add_own_half(g4, recv, idx, out_dtype, name):
    p, _, rh, n = g4.shape
    tc = _pick(n, _EW_COLS)
    tr = _row_tile(rh, tc, 256 * 1024)

    def body(idx_ref, g_ref, r_ref, o_ref):
        o_ref[...] = (g_ref[...] + r_ref[...]).astype(out_dtype)

    return pl.pallas_call(
        body, name=name, out_shape=jax.ShapeDtypeStruct((p, rh, n), out_dtype),
        grid_spec=pltpu.PrefetchScalarGridSpec(
            num_scalar_prefetch=1, grid=(p, rh // tr, n // tc),
            in_specs=[pl.BlockSpec((None, None, tr, tc), lambda q, i, j, idx_ref: (q, idx_ref[0], i, j)),
                      pl.BlockSpec((None, tr, tc), lambda q, i, j, idx_ref: (q, i, j))],
            out_specs=pl.BlockSpec((None, tr, tc), lambda q, i, j, idx_ref: (q, i, j))),
        compiler_params=_params(),
    )(idx, g4, recv)


def _sum_chips(own, kind, parts, idx, slots, to_all, name):
    _, rh, w = parts.shape
    tc = _pick(w, _EW_COLS)
    tr = _row_tile(rh, tc, 128 * 1024)
    nrb, ncb = rh // tr, w // tc

    def body(idx_ref, own_ref, p0, p1, p2, o_ref):
        o_ref[...] = ((own_ref[...].astype(F32) + p0[...].astype(F32)) + p1[...].astype(F32)) + p2[...].astype(F32)

    if kind == "col":
        own_spec = pl.BlockSpec((None, tr, tc), lambda i, j, idx_ref: (0, i, idx_ref[1] * ncb + j))
    else:
        own_spec = pl.BlockSpec((None, tr, tc), lambda i, j, idx_ref: (idx_ref[1], i, j))
    if to_all:
        out_map = lambda i, j, idx_ref: ((2 * idx_ref[1] + idx_ref[0]) * nrb + i, j)
    else:
        out_map = lambda i, j, idx_ref: (idx_ref[0] * nrb + i, j)

    def part(k):
        return pl.BlockSpec((None, tr, tc), lambda i, j, idx_ref: (k, i, j))

    return pl.pallas_call(
        body, name=name, out_shape=jax.ShapeDtypeStruct((slots * rh, w), F32),
        grid_spec=pltpu.PrefetchScalarGridSpec(
            num_scalar_prefetch=1, grid=(nrb, ncb), in_specs=[own_spec, part(0), part(1), part(2)],
            out_specs=pl.BlockSpec((tr, tc), out_map)),
        compiler_params=_params(),
    )(idx, own, parts, parts, parts)


class _Reduce:
    def __init__(self, name, g, kind, idx, wire, to_all):
        r, c = g.shape
        self.name, self.kind, self.idx, self.wire, self.to_all = name, kind, idx, wire, to_all
        self.view = g.reshape(1, 2, r // 2, c) if kind == "col" else g.reshape(N_CHIPS, 2, r // (2 * N_CHIPS), c)

    def sibling(self):
        return _SiblingExchange([self.view])

    def got_sibling(self, outs):
        self.sum = _add_own_half(self.view, outs[0], self.idx, self.wire, "grad_chip_sum_" + self.name)

    def chips(self):
        return _ChipExchange([self.sum], [self.kind])

    def got_chips(self, outs):
        self.total = _sum_chips(self.sum, self.kind, outs[0], self.idx, 2 * N_CHIPS if self.to_all else 2,
                                self.to_all, "grad_total_" + self.name)


def _pack(arrays, rows):
    flat = jnp.concatenate([a.reshape(-1) for a in arrays])
    return jnp.pad(flat, (0, rows * SMALL_PACK_COLS - flat.shape[0])).reshape(rows, SMALL_PACK_COLS)


def _unpack(packed, shapes):
    flat = packed.reshape(-1)
    out, o = [], 0
    for shp in shapes:
        size = math.prod(shp)
        out.append(flat[o:o + size].reshape(shp))
        o += size
    return out


def _pack_rows(shapes):
    total = sum(math.prod(s) for s in shapes)
    unit = SMALL_PACK_COLS * N_CHIPS * 2 * SUBLANE
    return -(-total // unit) * (N_CHIPS * 2 * SUBLANE)


BIG = ("w_in", "w_attn_proj", "w_lru_proj", "w_out", "w_ffn_gate", "w_ffn_up", "w_ffn_down")
BIG_KIND = {"w_in": "col", "w_attn_proj": "row", "w_lru_proj": "row", "w_out": "row", "w_ffn_gate": "col",
            "w_ffn_up": "col", "w_ffn_down": "row"}
SMALL = ("norm1_g", "b_gates", "q_norm_g", "k_norm_g", "sinks", "conv_w", "conv_b", "w_rgate", "b_rgate",
         "w_igate", "b_igate", "lru_lambda", "norm2_g")
WEIGHTS = ("norm1_g", "w_in", "b_gates", "q_norm_g", "k_norm_g", "sinks", "conv_w", "conv_b", "w_rgate", "b_rgate",
           "w_igate", "b_igate", "lru_lambda", "w_attn_proj", "w_lru_proj", "w_out", "norm2_g", "w_ffn_gate",
           "w_ffn_up", "w_ffn_down")


def kernel(x, positions, norm1_g, w_in, b_gates, q_norm_g, k_norm_g, sinks, conv_w, conv_b, w_rgate, b_rgate, w_igate, b_igate, lru_lambda, w_attn_proj, w_lru_proj, w_out, norm2_g, w_ffn_gate, w_ffn_up, w_ffn_down, loss_target, m_norm1_g, m_w_in, m_b_gates, m_q_norm_g, m_k_norm_g, m_sinks, m_conv_w, m_conv_b, m_w_rgate, m_b_rgate, m_w_igate, m_b_igate, m_lru_lambda, m_w_attn_proj, m_w_lru_proj, m_w_out, m_norm2_g, m_w_ffn_gate, m_w_ffn_up, m_w_ffn_down, v_norm1_g, v_w_in, v_b_gates, v_q_norm_g, v_k_norm_g, v_sinks, v_conv_w, v_conv_b, v_w_rgate, v_b_rgate, v_w_igate, v_b_igate, v_lru_lambda, v_w_attn_proj, v_w_lru_proj, v_w_out, v_norm2_g, v_w_ffn_gate, v_w_ffn_up, v_w_ffn_down):
    args = dict(locals())
    w = {n: args[n] for n in WEIGHTS}
    mom = {n: args["m_" + n] for n in WEIGHTS}
    var = {n: args["v_" + n] for n in WEIGHTS}

    t, d = x.shape[1], x.shape[2]
    hd = q_norm_g.shape[-1]
    nq = sinks.shape[-1]
    q_w = nq * hd
    d_rnn = conv_b.shape[-1]
    taps = conv_w.shape[1]
    n_blocks, bw = w_rgate.shape[1], w_rgate.shape[2]
    in_w = w_in.shape[-1] * N_CHIPS
    kv_w = (in_w - q_w - 2 * d_rnn - 2 * d) // 2
    kv = kv_w // hd
    grp = nq // kv
    u_off = q_w + 2 * kv_w
    gr_off = u_off + d_rnn
    ga_off = gr_off + d_rnn
    gw = bw * LANE // math.gcd(bw, LANE)
    ng = d_rnn // gw
    chip = 2 * lax.axis_index("x") + lax.axis_index("y")
    idx = jnp.stack([lax.axis_index("c"), chip]).astype(jnp.int32)

    x2, tgt = x[0], loss_target[0]

    placed = {n: _cast_into_full(w[n][0], BIG_KIND[n], idx, "cast_" + n) for n in BIG}

    def gather(*names):
        return _AllGather([placed[n] for n in names], [BIG_KIND[n] for n in names])

    (win_f,) = _run_exchange(gather("w_in"), "allgather_w_in")
    conv_w_full = _gather_small(conv_w[0], "allgather_conv_w")
    conv_w_full = jnp.transpose(conv_w_full, (1, 0, 2)).reshape(taps, d_rnn)
    wr_dense = _dense_groups(w_rgate[0], gw)
    wi_dense = _dense_groups(w_igate[0], gw)

    inv_freq = ROPE_THETA ** (-jnp.arange(0, hd // 4, 2, dtype=F32) / (hd // 4))
    ang = positions[0].astype(F32)[:, None] * inv_freq
    cos, sin = jnp.cos(ang), jnp.sin(ang)
    rest = hd - 2 * cos.shape[1]
    cos_t = jnp.concatenate([cos, cos, jnp.ones((t, rest), F32)], axis=1)
    sin_t = jnp.concatenate([-sin, sin, jnp.zeros((t, rest), F32)], axis=1)
    sinks1 = sinks[0]

    xn = _rms_fwd(x2, norm1_g, "rms1_fwd")
    z, (wap_f, wlp_f, wout_f, wg_f) = _matmul(xn, win_f, "nn", "in_proj",
                                             job=gather("w_attn_proj", "w_lru_proj", "w_out", "w_ffn_gate"))
    zu, zgr = z[:, u_off:u_off + d_rnn], z[:, gr_off:gr_off + d_rnn]
    attn, (wu_f,) = _attn_fwd(z, cos_t, sin_t, q_norm_g, k_norm_g, sinks1, kv, grp, hd, "attn_fwd",
                              job=gather("w_ffn_up"))
    uc = _conv_fwd(zu, conv_w_full, conv_b, "conv_fwd")
    rpre, ipre = _gates_fwd(uc, wr_dense, wi_dense, "gates_fwd")
    (hst, rec), (wd_f,) = _lru_fwd(uc, rpre, ipre, zgr, b_rgate, b_igate, lru_lambda, "lru_fwd",
                                   job=gather("w_ffn_down"))
    pa = _matmul(attn, wap_f, "nn", "attn_proj")
    plru = _matmul(rec, wlp_f, "nn", "lru_proj")
    merged = _merge_fwd(z, b_gates, pa, plru, ga_off, "merge_fwd")
    h1 = _matmul(merged, wout_f, "nn", "out_proj", add=x2)
    hn = _rms_fwd(h1, norm2_g, "rms2_fwd")
    gate = _matmul(hn, wg_f, "nn", "ffn_gate")
    up = _matmul(hn, wu_f, "nn", "ffn_up")
    act = _swiglu_fwd(gate, up, "swiglu_fwd")
    yout = _matmul(act, wd_f, "nn", "ffn_down", add=h1)
    dy, loss_part = _loss_head(yout, tgt, "loss_head")
    loss = lax.psum(loss_part[0, 0], ("x", "y", "c"))

    def reduction(n, g):
        return _Reduce(n, g, BIG_KIND[n], idx, BF16, False)

    r_wd = reduction("w_ffn_down", _matmul(act, dy, "tn", "d_w_ffn_down"))
    dact, got = _matmul(dy, wd_f, "nt", "d_act", job=r_wd.sibling())
    r_wd.got_sibling(got)
    (dgate, dup), got = _swiglu_bwd(dact, gate, up, "swiglu_bwd", job=r_wd.chips())
    r_wd.got_chips(got)
    r_wg = reduction("w_ffn_gate", _matmul(hn, dgate, "tn", "d_w_ffn_gate"))
    g_wu, got = _matmul(hn, dup, "tn", "d_w_ffn_up", job=r_wg.sibling())
    r_wg.got_sibling(got)
    r_wu = reduction("w_ffn_up", g_wu)
    both = _Jobs(r_wu.sibling(), r_wg.chips())
    dhn, got = _matmul(dgate, wg_f, "nt", "d_hn_gate", job=both)
    got_wu, got_wg = both.split(got)
    r_wu.got_sibling(got_wu)
    r_wg.got_chips(got_wg)
    dhn, got = _matmul(dup, wu_f, "nt", "d_hn_up", add=dhn, job=r_wu.chips())
    r_wu.got_chips(got)
    dh1, g_norm2 = _rms_bwd(dhn, h1, norm2_g, dy, "rms2_bwd")
    r_wout = reduction("w_out", _matmul(merged, dh1, "tn", "d_w_out"))
    dmerged, got = _matmul(dh1, wout_f, "nt", "d_merged", job=r_wout.sibling())
    r_wout.got_sibling(got)
    (dpa, dpl, dga, dgl, g_ba, g_bl), got = _merge_bwd(dmerged, z, b_gates, pa, plru, ga_off, "merge_bwd",
                                                       job=r_wout.chips())
    r_wout.got_chips(got)
    r_wap = reduction("w_attn_proj", _matmul(attn, dpa, "tn", "d_w_attn_proj"))
    dattn, got = _matmul(dpa, wap_f, "nt", "d_attn", job=r_wap.sibling())
    r_wap.got_sibling(got)
    g_wlp, got = _matmul(rec, dpl, "tn", "d_w_lru_proj", job=r_wap.chips())
    r_wap.got_chips(got)
    r_wlp = reduction("w_lru_proj", g_wlp)
    drec, got = _matmul(dpl, wlp_f, "nt", "d_rec", job=r_wlp.sibling())
    r_wlp.got_sibling(got)
    (dgr, drp, dip, duc_direct, g_lam, g_br, g_bi), got = _lru_bwd(
        drec, hst, uc, rpre, ipre, zgr, b_rgate, b_igate, lru_lambda, "lru_bwd", job=r_wlp.chips())
    r_wlp.got_chips(got)
    duc = _gates_bwd_x(duc_direct, drp, dip, wr_dense, wi_dense, "gates_bwd_x")
    g_wr = _diag_blocks(_gates_bwd_w(uc, drp, ng, gw, "gates_bwd_wr"), n_blocks, bw)
    g_wi = _diag_blocks(_gates_bwd_w(uc, dip, ng, gw, "gates_bwd_wi"), n_blocks, bw)
    du, g_convw, g_convb = _conv_bwd(duc, zu, conv_w_full, "conv_bwd")
    dq, dk, dv, g_qg, g_kg, g_sinks = _attn_bwd(dattn, z, cos_t, sin_t, q_norm_g, k_norm_g, sinks1, kv, grp, hd,
                                                 "attn_bwd")
    dz = jnp.concatenate([dq, dk, dv, du, dgr, dga, dgl], axis=1).astype(BF16)
    r_win = reduction("w_in", _matmul(xn, dz, "tn", "d_w_in"))
    dxn, got = _matmul(dz, win_f, "nt", "d_xn", job=r_win.sibling())
    r_win.got_sibling(got)
    (dx, g_norm1), got = _rms_bwd(dxn, x2, norm1_g, dh1, "rms1_bwd", job=r_win.chips())
    r_win.got_chips(got)

    small_grads = {"norm1_g": g_norm1, "b_gates": jnp.concatenate([g_ba, g_bl], axis=1), "q_norm_g": g_qg,
                   "k_norm_g": g_kg, "sinks": g_sinks[:, :nq], "conv_w": g_convw, "conv_b": g_convb,
                   "w_rgate": g_wr, "b_rgate": g_br, "w_igate": g_wi, "b_igate": g_bi, "lru_lambda": g_lam,
                   "norm2_g": g_norm2}
    gshapes = [small_grads[n].shape for n in SMALL]
    r_small = _Reduce("small", _pack([small_grads[n] for n in SMALL], _pack_rows(gshapes)), "row", idx, F32, True)
    r_small.got_sibling(_run_exchange(r_small.sibling(), "grad_sibling_exchange_small"))
    r_small.got_chips(_run_exchange(r_small.chips(), "grad_chip_exchange_small"))
    by_name = {"w_in": r_win, "w_attn_proj": r_wap, "w_lru_proj": r_wlp, "w_out": r_wout, "w_ffn_gate": r_wg,
               "w_ffn_up": r_wu, "w_ffn_down": r_wd}
    reduced = _run_exchange(_FinishExchange([by_name[n].total for n in BIG] + [r_small.total]),
                            "grad_finish_exchange")
    grads = dict(zip(BIG, reduced[:-1]))
    small_full = dict(zip(SMALL, _unpack(reduced[-1], gshapes)))
    per = d_rnn // N_CHIPS
    small_full["conv_w"] = lax.dynamic_slice(small_full["conv_w"], (0, chip * per), (taps, per))
    for n in SMALL:
        grads[n] = small_full[n]

    delta, new_m, new_v = {}, {}, {}
    for n in BIG:
        delta[n], new_m[n], new_v[n] = _adamw(w[n][0], grads[n], mom[n][0], var[n][0], "adamw_" + n)
    pshapes = [w[n].shape for n in SMALL]
    prows = _pack_rows(pshapes)
    pk = [_pack([src[n] for n in SMALL], prows) for src in (w, grads, mom, var)]
    for res, packed in zip((delta, new_m, new_v), _adamw(pk[0], pk[1], pk[2], pk[3], "adamw_small")):
        res.update(dict(zip(SMALL, _unpack(packed, pshapes))))

    outs = [loss, dx.reshape(x.shape)]
    for res in (grads, delta, new_m, new_v):
        outs += [res[n].reshape(w[n].shape) for n in WEIGHTS]
    return tuple(outs)


def _gather_small(shard, name):
    def body(s_ref, o_ref, send_sems, recv_sems):
        e = _Env((s_ref,), (o_ref,), send_sems, recv_sems)
        o_ref[e.me] = s_ref[...]
        for k, (cx, cy) in enumerate(e.chips):
            e.copy(s_ref, o_ref.at[e.me], k, (cx, cy, e.c)).start()
        for k, (cx, cy) in enumerate(e.chips):
            e.copy(s_ref, o_ref.at[2 * cx + cy], k, (cx, cy, e.c)).wait_recv()
        for k, (cx, cy) in enumerate(e.chips):
            e.copy(s_ref, o_ref.at[e.me], k, (cx, cy, e.c)).wait_send()

    vm = pl.BlockSpec(memory_space=pltpu.VMEM)
    return pl.pallas_call(body, name=name, out_shape=jax.ShapeDtypeStruct((N_CHIPS,) + shard.shape, shard.dtype),
                          in_specs=[vm], out_specs=vm,
                          scratch_shapes=[pltpu.SemaphoreType.DMA((3,)), pltpu.SemaphoreType.DMA((3,))])(shard)
```

```python
import functools
import math

import jax
import jax.numpy as jnp
from jax import lax
from jax.experimental import pallas as pl
from jax.experimental.pallas import tpu as pltpu

F32 = jnp.float32
BF16 = jnp.bfloat16
MESH = pl.DeviceIdType.MESH

WINDOW = 128
BLK = 128
ROPE_THETA = 500000.0
LRU_C = 8.0
EPS = 1e-6
NEG = -1e30
ADAM_LR = 0.001
ADAM_B1 = 0.9
ADAM_B2 = 0.999
ADAM_EPS = 1e-08
ADAM_WD = 0.01
ADAM_STEP = 10

VMEM_LIMIT_BYTES = 52 * 1024 * 1024
LANE = 128
SUBLANE = 8
N_CHIPS = 4
SMALL_PACK_COLS = 512


def _params(**kw):
    return pltpu.CompilerParams(vmem_limit_bytes=VMEM_LIMIT_BYTES, **kw)


def _pick(dim, cands):
    for c in cands:
        if dim % c == 0:
            return c
    return dim


def _sigmoid(x):
    return 1.0 / (1.0 + jnp.exp(-x))


ANY = pl.BlockSpec(memory_space=pl.ANY)


class _Env:
    def __init__(self, ins, outs, send, recv, sem0=0, place=None):
        self.ins, self.outs, self.send, self.recv, self.sem0 = ins, outs, send, recv, sem0
        self.x, self.y, self.c = place or (lax.axis_index("x"), lax.axis_index("y"), lax.axis_index("c"))
        self.me = 2 * self.x + self.y
        self.chips = [(1 - self.x, self.y), (self.x, 1 - self.y), (1 - self.x, 1 - self.y)]
        self.sibling = (self.x, self.y, 1 - self.c)

    def sub(self, i0, n_in, o0, n_out, sem0):
        return _Env(self.ins[i0:i0 + n_in], self.outs[o0:o0 + n_out], self.send, self.recv, self.sem0 + sem0,
                    (self.x, self.y, self.c))

    def copy(self, src, dst, sem, to):
        return pltpu.make_async_remote_copy(src_ref=src, dst_ref=dst, send_sem=self.send.at[self.sem0 + sem],
                                            recv_sem=self.recv.at[self.sem0 + sem], device_id=to, device_id_type=MESH)


class _Exchange:
    inputs, out_shapes, aliases, n_sems = (), (), {}, 0

    def start(self, e):
        raise NotImplementedError

    def finish(self, e):
        raise NotImplementedError


class _Jobs(_Exchange):
    def __init__(self, *jobs):
        self.jobs, self.inputs, self.out_shapes, self.aliases, self.n_sems, self.at = jobs, [], [], {}, 0, []
        for job in jobs:
            self.at.append((len(self.inputs), len(self.out_shapes), self.n_sems))
            self.aliases.update({len(self.inputs) + i: len(self.out_shapes) + o for i, o in job.aliases.items()})
            self.inputs += list(job.inputs)
            self.out_shapes += list(job.out_shapes)
            self.n_sems += job.n_sems

    def _each(self, e):
        for job, (i0, o0, s0) in zip(self.jobs, self.at):
            yield job, e.sub(i0, len(job.inputs), o0, len(job.out_shapes), s0)

    def split(self, outs):
        return [tuple(outs[o0:o0 + len(job.out_shapes)]) for job, (_, o0, _) in zip(self.jobs, self.at)]

    def start(self, e):
        for job, se in self._each(e):
            job.start(se)

    def finish(self, e):
        for job, se in self._each(e):
            job.finish(se)


def _call(body, name, out_shape, grid, in_specs, out_specs, args, scratch_shapes=(), job=None, aliases=None):
    aliases = dict(aliases or {})
    if job is None:
        return pl.pallas_call(body, name=name, out_shape=out_shape, grid=grid, in_specs=list(in_specs),
                              out_specs=out_specs, scratch_shapes=list(scratch_shapes), input_output_aliases=aliases,
                              compiler_params=_params())(*args), ()
    single = not isinstance(out_shape, (tuple, list))
    shapes = [out_shape] if single else list(out_shape)
    ospecs = [out_specs] if single else list(out_specs)
    n_in, n_out, n_scr = len(args), len(shapes), len(scratch_shapes)
    j_in, j_out = len(job.inputs), len(job.out_shapes)

    def hosted(*refs):
        ins, jins = refs[:n_in], refs[n_in:n_in + j_in]
        outs = refs[n_in + j_in:n_in + j_in + n_out]
        jouts = refs[n_in + j_in + n_out:n_in + j_in + n_out + j_out]
        rest = refs[n_in + j_in + n_out + j_out:]
        e = _Env(jins, jouts, rest[n_scr], rest[n_scr + 1])
        first = functools.reduce(jnp.logical_and, [pl.program_id(d) == 0 for d in range(len(grid))])
        last = functools.reduce(jnp.logical_and, [pl.program_id(d) == g - 1 for d, g in enumerate(grid)])

        @pl.when(first)
        def _():
            job.start(e)

        body(*ins, *outs, *rest[:n_scr])

        @pl.when(last)
        def _():
            job.finish(e)

    res = pl.pallas_call(
        hosted, name=name, out_shape=tuple(shapes + list(job.out_shapes)), grid=grid,
        in_specs=list(in_specs) + [ANY] * j_in, out_specs=tuple(ospecs + [ANY] * j_out),
        scratch_shapes=list(scratch_shapes) + [pltpu.SemaphoreType.DMA((job.n_sems,)),
                                               pltpu.SemaphoreType.DMA((job.n_sems,))],
        input_output_aliases={**aliases, **{n_in + i: n_out + o for i, o in job.aliases.items()}},
        compiler_params=_params())(*args, *job.inputs)
    return (res[0] if single else tuple(res[:n_out])), tuple(res[n_out:])


def _run_exchange(job, name):
    n_in, n_out = len(job.inputs), len(job.out_shapes)

    def body(*refs):
        e = _Env(refs[:n_in], refs[n_in:n_in + n_out], refs[n_in + n_out], refs[n_in + n_out + 1])
        job.start(e)
        job.finish(e)

    return pl.pallas_call(
        body, name=name, out_shape=tuple(job.out_shapes), in_specs=[ANY] * n_in, out_specs=tuple([ANY] * n_out),
        input_output_aliases=dict(job.aliases),
        scratch_shapes=[pltpu.SemaphoreType.DMA((job.n_sems,)), pltpu.SemaphoreType.DMA((job.n_sems,))],
    )(*job.inputs)


_MN_TILES = (1024, 1408, 1280, 512, 256, 128)
_K_TILES = (1408, 1280, 1024, 512, 256, 128)


def _matmul(a, b, mode, name, add=None, out_dtype=F32, job=None, m_window=None, into=None):
    if mode == "nn":
        (m, k), (k2, n) = a.shape, b.shape
    elif mode == "nt":
        (m, k), (n, k2) = a.shape, b.shape
    else:
        (k, m), (k2, n) = a.shape, b.shape
    assert k == k2, (a.shape, b.shape, mode)
    m0, m = m_window or (0, m)
    tm, tn, tk = _pick(math.gcd(m, m0) if m0 else m, _MN_TILES), _pick(n, _MN_TILES), _pick(k, _K_TILES)
    nk, mb0 = k // tk, m0 // tm
    if mode == "nn":
        a_spec = pl.BlockSpec((tm, tk), lambda i, j, kk: (mb0 + i, kk))
        b_spec = pl.BlockSpec((tk, tn), lambda i, j, kk: (kk, j))
        dims = (((1,), (0,)), ((), ()))
    elif mode == "nt":
        a_spec = pl.BlockSpec((tm, tk), lambda i, j, kk: (mb0 + i, kk))
        b_spec = pl.BlockSpec((tn, tk), lambda i, j, kk: (j, kk))
        dims = (((1,), (1,)), ((), ()))
    else:
        a_spec = pl.BlockSpec((tk, tm), lambda i, j, kk: (kk, mb0 + i))
        b_spec = pl.BlockSpec((tk, tn), lambda i, j, kk: (kk, j))
        dims = (((0,), (0,)), ((), ()))
    out_rows, ob0 = (into[1], mb0) if into is not None else (m, 0)
    o_spec = pl.BlockSpec((tm, tn), lambda i, j, kk: (ob0 + i, j))
    has_add = add is not None
    begun = into is not None and into[0] is not None

    def body(*refs):
        a_ref, b_ref = refs[:2]
        add_ref = refs[2] if has_add else None
        o_ref, acc = refs[-2:]
        kk = pl.program_id(2)

        @pl.when(kk == 0)
        def _():
            acc[...] = jnp.zeros_like(acc)

        acc[...] += lax.dot_general(a_ref[...].astype(BF16), b_ref[...].astype(BF16), dims,
                                    preferred_element_type=F32)

        @pl.when(kk == nk - 1)
        def _():
            r = acc[...]
            if has_add:
                r = r + add_ref[...]
            o_ref[...] = r.astype(out_dtype)

    in_specs = [a_spec, b_spec] + ([pl.BlockSpec((tm, tn), lambda i, j, kk: (mb0 + i, j))] if has_add else [])
    args = (a, b) + ((add,) if has_add else ())
    aliases = None
    if begun:
        aliases = {len(args): 0}
        in_specs, args = in_specs + [ANY], args + (into[0],)
    res, extra = _call(body, name, jax.ShapeDtypeStruct((out_rows, n), out_dtype), (m // tm, n // tn, nk), in_specs,
                       o_spec, args, [pltpu.VMEM((tm, tn), F32)], job, aliases)
    return res if job is None else (res, extra)


def _row_tile(rows, cols, budget_elems=512 * 1024):
    cands = [c for c in (1024, 704, 512, 352, 256, 128, 64, 32, 16) if c * cols <= budget_elems]
    return _pick(rows, cands or (16,))


_EW_COLS = (1280, 1408, 1024, 640, 512, 256, 128)


def _tile2d(rows, cols, max_elems):
    tc = _pick(cols, _EW_COLS)
    return _row_tile(rows, tc, max_elems), tc


def _rms_fwd(x, g, name):
    t, d = x.shape
    tr = _row_tile(t, d)

    def body(x_ref, g_ref, o_ref):
        xv = x_ref[...]
        rstd = lax.rsqrt(jnp.mean(xv * xv, axis=-1, keepdims=True) + EPS)
        o_ref[...] = (xv * rstd * g_ref[...]).astype(BF16)

    spec = pl.BlockSpec((tr, d), lambda i: (i, 0))
    return pl.pallas_call(body, name=name, out_shape=jax.ShapeDtypeStruct((t, d), BF16), grid=(t // tr,),
                          in_specs=[spec, pl.BlockSpec((1, d), lambda i: (0, 0))], out_specs=spec,
                          compiler_params=_params())(x, g)


def _rms_bwd(dxn, x, g, resid, name, job=None):
    t, d = x.shape
    tr = _row_tile(t, d, 256 * 1024)

    def body(dxn_ref, x_ref, g_ref, r_ref, dx_ref, dg_ref):
        @pl.when(pl.program_id(0) == 0)
        def _():
            dg_ref[...] = jnp.zeros_like(dg_ref)

        xv = x_ref[...]
        rstd = lax.rsqrt(jnp.mean(xv * xv, axis=-1, keepdims=True) + EPS)
        xhat = xv * rstd
        dy = dxn_ref[...]
        dg_ref[...] += jnp.sum(dy * xhat, axis=0, keepdims=True)
        dxhat = dy * g_ref[...]
        dx_ref[...] = r_ref[...] + rstd * (dxhat - xhat * jnp.mean(dxhat * xhat, axis=-1, keepdims=True))

    spec = pl.BlockSpec((tr, d), lambda i: (i, 0))
    vec = pl.BlockSpec((1, d), lambda i: (0, 0))
    res, extra = _call(body, name, (jax.ShapeDtypeStruct((t, d), F32), jax.ShapeDtypeStruct((1, d), F32)), (t // tr,),
                       [spec, spec, vec, spec], (spec, vec), (dxn, x, g, resid), (), job)
    return res if job is None else (res, extra)


def _swiglu_fwd(gate, up, name):
    t, f = gate.shape
    tr, tc = _tile2d(t, f, 1024 * 1024)

    def body(g_ref, u_ref, o_ref):
        gv = g_ref[...]
        o_ref[...] = (gv * _sigmoid(gv) * u_ref[...]).astype(BF16)

    spec = pl.BlockSpec((tr, tc), lambda i, j: (i, j))
    return pl.pallas_call(body, name=name, out_shape=jax.ShapeDtypeStruct((t, f), BF16), grid=(t // tr, f // tc),
                          in_specs=[spec, spec], out_specs=spec, compiler_params=_params())(gate, up)


def _swiglu_bwd(dact, gate, up, name, job=None):
    t, f = gate.shape
    tr, tc = _tile2d(t, f, 768 * 1024)

    def body(d_ref, g_ref, u_ref, dg_ref, du_ref):
        gv, dv = g_ref[...], d_ref[...]
        sg = _sigmoid(gv)
        dg_ref[...] = (dv * u_ref[...] * (sg * (1.0 + gv * (1.0 - sg)))).astype(BF16)
        du_ref[...] = (dv * (gv * sg)).astype(BF16)

    spec = pl.BlockSpec((tr, tc), lambda i, j: (i, j))
    shp = jax.ShapeDtypeStruct((t, f), BF16)
    res, extra = _call(body, name, (shp, shp), (t // tr, f // tc), [spec, spec, spec], (spec, spec), (dact, gate, up),
                       (), job)
    return res if job is None else (res, extra)


def _merge_fwd(z, b_gates, pa, plru, ga_off, name):
    t, d = pa.shape
    cw = _pick(math.gcd(ga_off, d), (512, 256, 128))
    tr = _row_tile(t, cw, 256 * 1024)
    oa, ol, nd = ga_off // cw, (ga_off + d) // cw, d // cw

    def body(ga_ref, gl_ref, ba_ref, bl_ref, pa_ref, pl_ref, o_ref):
        sa = _sigmoid(ga_ref[...] + ba_ref[...])
        sl = _sigmoid(gl_ref[...] + bl_ref[...])
        o_ref[...] = (sa * pa_ref[...] + sl * pl_ref[...]).astype(BF16)

    blk = pl.BlockSpec((tr, cw), lambda i, j: (i, j))
    return pl.pallas_call(
        body, name=name, out_shape=jax.ShapeDtypeStruct((t, d), BF16), grid=(t // tr, nd),
        in_specs=[pl.BlockSpec((tr, cw), lambda i, j: (i, oa + j)), pl.BlockSpec((tr, cw), lambda i, j: (i, ol + j)),
                  pl.BlockSpec((1, cw), lambda i, j: (0, j)), pl.BlockSpec((1, cw), lambda i, j: (0, nd + j)),
                  blk, blk],
        out_specs=blk, compiler_params=_params(),
    )(z, z, b_gates, b_gates, pa, plru)


def _merge_bwd(dmerged, z, b_gates, pa, plru, ga_off, name, job=None):
    t, d = pa.shape
    cw = _pick(math.gcd(ga_off, d), (512, 256, 128))
    tr = _row_tile(t, cw, 256 * 1024)
    oa, ol, nd = ga_off // cw, (ga_off + d) // cw, d // cw

    def body(dm_ref, ga_ref, gl_ref, ba_ref, bl_ref, pa_ref, pl_ref, dpa_ref, dpl_ref, dga_ref, dgl_ref, sa_ref, sl_ref):
        @pl.when(pl.program_id(1) == 0)
        def _():
            sa_ref[...] = jnp.zeros_like(sa_ref)
            sl_ref[...] = jnp.zeros_like(sl_ref)

        dm = dm_ref[...]
        sa = _sigmoid(ga_ref[...] + ba_ref[...])
        sl = _sigmoid(gl_ref[...] + bl_ref[...])
        dpa_ref[...] = (dm * sa).astype(BF16)
        dpl_ref[...] = (dm * sl).astype(BF16)
        dga = dm * pa_ref[...] * (sa * (1.0 - sa))
        dgl = dm * pl_ref[...] * (sl * (1.0 - sl))
        dga_ref[...] = dga
        dgl_ref[...] = dgl
        sa_ref[...] += jnp.sum(dga, axis=0, keepdims=True)
        sl_ref[...] += jnp.sum(dgl, axis=0, keepdims=True)

    blk = pl.BlockSpec((tr, cw), lambda j, i: (i, j))
    vec = pl.BlockSpec((1, cw), lambda j, i: (0, j))
    big16, big32, v32 = (jax.ShapeDtypeStruct((t, d), BF16), jax.ShapeDtypeStruct((t, d), F32),
                         jax.ShapeDtypeStruct((1, d), F32))
    res, extra = _call(
        body, name, (big16, big16, big32, big32, v32, v32), (nd, t // tr),
        [blk, pl.BlockSpec((tr, cw), lambda j, i: (i, oa + j)), pl.BlockSpec((tr, cw), lambda j, i: (i, ol + j)),
         vec, pl.BlockSpec((1, cw), lambda j, i: (0, nd + j)), blk, blk],
        (blk, blk, blk, blk, vec, vec), (dmerged, z, z, b_gates, b_gates, pa, plru), (), job)
    return res if job is None else (res, extra)


def _loss_head(y, target, name):
    t, d = y.shape
    tr = _row_tile(t, d, 256 * 1024)
    nt = t // tr

    def body(y_ref, t_ref, dy_ref, loss_ref, acc):
        i = pl.program_id(0)

        @pl.when(i == 0)
        def _():
            acc[...] = jnp.zeros_like(acc)

        e = y_ref[...] - t_ref[...]
        dy_ref[...] = e * (1.0 / d)
        acc[...] += jnp.sum(e * e, axis=0, keepdims=True)

        @pl.when(i == nt - 1)
        def _():
            loss_ref[...] = (0.5 / d) * jnp.sum(acc[...], axis=-1, keepdims=True)

    spec = pl.BlockSpec((tr, d), lambda i: (i, 0))
    return pl.pallas_call(
        body, name=name, out_shape=(jax.ShapeDtypeStruct((t, d), F32), jax.ShapeDtypeStruct((1, 1), F32)),
        grid=(nt,), in_specs=[spec, spec], out_specs=(spec, pl.BlockSpec((1, 1), lambda i: (0, 0))),
        scratch_shapes=[pltpu.VMEM((1, d), F32)], compiler_params=_params(),
    )(y, target)


def _adamw(w, g, m, v, name):
    r, c = w.shape
    tr, tc = _tile2d(r, c, 512 * 1024)
    c1 = 1.0 - ADAM_B1 ** ADAM_STEP
    c2 = 1.0 - ADAM_B2 ** ADAM_STEP

    def body(w_ref, g_ref, m_ref, v_ref, d_ref, nm_ref, nv_ref):
        gv = g_ref[...]
        mn = ADAM_B1 * m_ref[...] + (1.0 - ADAM_B1) * gv
        vn = ADAM_B2 * v_ref[...] + (1.0 - ADAM_B2) * (gv * gv)
        d_ref[...] = -ADAM_LR * ((mn / c1) / (jnp.sqrt(vn / c2) + ADAM_EPS) + ADAM_WD * w_ref[...])
        nm_ref[...] = mn
        nv_ref[...] = vn

    spec = pl.BlockSpec((tr, tc), lambda i, j: (i, j))
    shp = jax.ShapeDtypeStruct((r, c), F32)
    return pl.pallas_call(body, name=name, out_shape=(shp, shp, shp), grid=(r // tr, c // tc), in_specs=[spec] * 4,
                          out_specs=(spec, spec, spec), compiler_params=_params())(w, g, m, v)


def _swap_halves(v, lane, half):
    n = v.shape[-1]
    return jnp.where(lane < half, pltpu.roll(v, n - half, 1),
                     jnp.where(lane < 2 * half, pltpu.roll(v, half, 1), 0.0))


def _norm_fwd(xraw, g):
    rstd = lax.rsqrt(jnp.mean(xraw * xraw, axis=-1, keepdims=True) + EPS)
    xhat = xraw * rstd
    return xhat, rstd, xhat * g


def _norm_bwd(dy, xhat, rstd, g):
    dxhat = dy * g
    dx = rstd * (dxhat - xhat * jnp.mean(dxhat * xhat, axis=-1, keepdims=True))
    return dx, jnp.sum(dy * xhat, axis=0, keepdims=True)


def _attn_specs(nb, grp, hd, kv, clamp):
    qo, ko, vo = 0, (kv * grp), (kv * grp + kv)
    cur = (lambda i: jnp.minimum(i, nb - 1)) if clamp else (lambda i: i)
    prev = lambda i: jnp.maximum(cur(i) - 1, 0)
    zq = pl.BlockSpec((BLK, grp * hd), lambda h, i: (cur(i), h))
    kc = pl.BlockSpec((BLK, hd), lambda h, i: (cur(i), ko + h))
    kp = pl.BlockSpec((BLK, hd), lambda h, i: (prev(i), ko + h))
    vc = pl.BlockSpec((BLK, hd), lambda h, i: (cur(i), vo + h))
    vp = pl.BlockSpec((BLK, hd), lambda h, i: (prev(i), vo + h))
    tc = pl.BlockSpec((BLK, hd), lambda h, i: (cur(i), 0))
    tp = pl.BlockSpec((BLK, hd), lambda h, i: (prev(i), 0))
    gs = pl.BlockSpec((1, hd), lambda h, i: (0, 0))
    return zq, kc, kp, vc, vp, tc, tp, gs


def _attn_mask(i):
    qi = lax.broadcasted_iota(jnp.int32, (BLK, 2 * BLK), 0)
    kj = lax.broadcasted_iota(jnp.int32, (BLK, 2 * BLK), 1)
    rel = qi + BLK - kj
    return (rel >= 0) & (rel < WINDOW) & ((kj >= BLK) | (i > 0))


def _attn_fwd(z, cos_t, sin_t, qg, kg, sinks, kv, grp, hd, name, job=None):
    t = z.shape[0]
    nb = t // BLK
    half = hd // 8
    scale = 1.0 / math.sqrt(hd)
    zq, kc, kp, vc, vp, tc, tp, gs = _attn_specs(nb, grp, hd, kv, False)

    def body(sink_ref, zq_ref, kc_ref, kp_ref, vc_ref, vp_ref, cc_ref, sc_ref, cp_ref, sp_ref, qg_ref, kg_ref, o_ref):
        h, i = pl.program_id(0), pl.program_id(1)
        lane = lax.broadcasted_iota(jnp.int32, (BLK, hd), 1)

        def normrope(xraw, g, c, s):
            y = _norm_fwd(xraw, g)[2]
            return y * c + _swap_halves(y, lane, half) * s

        cc, sc = cc_ref[...], sc_ref[...]
        kcur = normrope(kc_ref[...], kg_ref[...], cc, sc)
        kprev = normrope(kp_ref[...], kg_ref[...], cp_ref[...], sp_ref[...])
        kk = jnp.concatenate([kprev, kcur], axis=0).astype(BF16)
        vv = jnp.concatenate([vp_ref[...], vc_ref[...]], axis=0).astype(BF16)
        mask = _attn_mask(i)
        for g in range(grp):
            q = normrope(zq_ref[:, g * hd:(g + 1) * hd], qg_ref[...], cc, sc).astype(BF16)
            s = lax.dot_general(q, kk, (((1,), (1,)), ((), ())), preferred_element_type=F32) * scale
            s = jnp.where(mask, s, NEG)
            sk = sink_ref[h * grp + g]
            mx = jnp.maximum(jnp.max(s, axis=-1, keepdims=True), sk)
            p = jnp.exp(s - mx)
            den = jnp.sum(p, axis=-1, keepdims=True) + jnp.exp(sk - mx)
            p = p / den
            o_ref[:, g * hd:(g + 1) * hd] = jnp.dot(p.astype(BF16), vv, preferred_element_type=F32).astype(BF16)

    res, extra = _call(
        body, name, jax.ShapeDtypeStruct((t, kv * grp * hd), BF16), (kv, nb),
        [pl.BlockSpec(memory_space=pltpu.SMEM), zq, kc, kp, vc, vp, tc, tc, tp, tp, gs, gs],
        pl.BlockSpec((BLK, grp * hd), lambda h, i: (i, h)),
        (sinks, z, z, z, z, z, cos_t, sin_t, cos_t, sin_t, qg, kg), (), job)
    return res if job is None else (res, extra)


def _attn_bwd(dattn, z, cos_t, sin_t, qg, kg, sinks, kv, grp, hd, name):
    t = z.shape[0]
    nb = t // BLK
    half = hd // 8
    scale = 1.0 / math.sqrt(hd)
    zq, kc, kp, vc, vp, tc, tp, gs = _attn_specs(nb, grp, hd, kv, True)

    def body(sink_ref, zq_ref, kc_ref, kp_ref, vc_ref, vp_ref, cc_ref, sc_ref, cp_ref, sp_ref, qg_ref, kg_ref, do_ref,
             dq_ref, dk_ref, dv_ref, dqg_ref, dkg_ref, dsk_ref, dk_carry, dv_carry):
        h, i = pl.program_id(0), pl.program_id(1)
        lane = lax.broadcasted_iota(jnp.int32, (BLK, hd), 1)
        lane1 = lax.broadcasted_iota(jnp.int32, (1, LANE), 1)

        @pl.when((h == 0) & (i == 0))
        def _():
            dqg_ref[...] = jnp.zeros_like(dqg_ref)
            dkg_ref[...] = jnp.zeros_like(dkg_ref)
            dsk_ref[...] = jnp.zeros_like(dsk_ref)

        @pl.when(i == 0)
        def _():
            dk_carry[...] = jnp.zeros_like(dk_carry)
            dv_carry[...] = jnp.zeros_like(dv_carry)

        def rope(y, c, s):
            return y * c + _swap_halves(y, lane, half) * s

        def rope_bwd(dout, c, s):
            return dout * c + _swap_halves(dout * s, lane, half)

        @pl.when(i < nb)
        def _():
            cc, sc, cp, sp = cc_ref[...], sc_ref[...], cp_ref[...], sp_ref[...]
            qgv, kgv = qg_ref[...], kg_ref[...]
            xh_kc, rs_kc, y_kc = _norm_fwd(kc_ref[...], kgv)
            xh_kp, rs_kp, y_kp = _norm_fwd(kp_ref[...], kgv)
            kk = jnp.concatenate([rope(y_kp, cp, sp), rope(y_kc, cc, sc)], axis=0).astype(BF16)
            vv = jnp.concatenate([vp_ref[...], vc_ref[...]], axis=0).astype(BF16)
            mask = _attn_mask(i)
            dkk = jnp.zeros((2 * BLK, hd), F32)
            dvv = jnp.zeros((2 * BLK, hd), F32)
            dqg = jnp.zeros((1, hd), F32)
            dsk = jnp.zeros((1, LANE), F32)
            for g in range(grp):
                xh_q, rs_q, y_q = _norm_fwd(zq_ref[:, g * hd:(g + 1) * hd], qgv)
                q = rope(y_q, cc, sc).astype(BF16)
                s = lax.dot_general(q, kk, (((1,), (1,)), ((), ())), preferred_element_type=F32) * scale
                s = jnp.where(mask, s, NEG)
                sk = sink_ref[h * grp + g]
                mx = jnp.maximum(jnp.max(s, axis=-1, keepdims=True), sk)
                p = jnp.exp(s - mx)
                den = jnp.sum(p, axis=-1, keepdims=True) + jnp.exp(sk - mx)
                p = p / den
                psink = jnp.exp(sk - mx) / den
                dog = do_ref[:, g * hd:(g + 1) * hd].astype(BF16)
                dp = lax.dot_general(dog, vv, (((1,), (1,)), ((), ())), preferred_element_type=F32)
                rsum = jnp.sum(p * dp, axis=-1, keepdims=True)
                ds = (p * (dp - rsum) * scale).astype(BF16)
                dsk = dsk + jnp.where(lane1 == h * grp + g, jnp.sum(-psink * rsum, axis=0, keepdims=True), 0.0)
                dqn = jnp.dot(ds, kk, preferred_element_type=F32)
                dkk = dkk + lax.dot_general(ds, q, (((0,), (0,)), ((), ())), preferred_element_type=F32)
                dvv = dvv + lax.dot_general(p.astype(BF16), dog, (((0,), (0,)), ((), ())), preferred_element_type=F32)
                dxq, dg_q = _norm_bwd(rope_bwd(dqn, cc, sc), xh_q, rs_q, qgv)
                dq_ref[:, g * hd:(g + 1) * hd] = dxq
                dqg = dqg + dg_q
            dkp_raw, dg_kp = _norm_bwd(rope_bwd(dkk[:BLK], cp, sp), xh_kp, rs_kp, kgv)
            dkc_raw, dg_kc = _norm_bwd(rope_bwd(dkk[BLK:], cc, sc), xh_kc, rs_kc, kgv)
            dk_ref[...] = dk_carry[...] + dkp_raw
            dv_ref[...] = dv_carry[...] + dvv[:BLK]
            dk_carry[...] = dkc_raw
            dv_carry[...] = dvv[BLK:]
            dqg_ref[...] += dqg
            dkg_ref[...] += dg_kp + dg_kc
            dsk_ref[...] += dsk

        @pl.when(i == nb)
        def _():
            dk_ref[...] = dk_carry[...]
            dv_ref[...] = dv_carry[...]

    kvw = kv * hd
    vec = pl.BlockSpec((1, hd), lambda h, i: (0, 0))
    shifted = pl.BlockSpec((BLK, hd), lambda h, i: (jnp.maximum(i - 1, 0), h))
    return pl.pallas_call(
        body, name=name,
        out_shape=(jax.ShapeDtypeStruct((t, kv * grp * hd), F32), jax.ShapeDtypeStruct((t, kvw), F32),
                   jax.ShapeDtypeStruct((t, kvw), F32), jax.ShapeDtypeStruct((1, hd), F32),
                   jax.ShapeDtypeStruct((1, hd), F32), jax.ShapeDtypeStruct((1, LANE), F32)),
        grid=(kv, nb + 1),
        in_specs=[pl.BlockSpec(memory_space=pltpu.SMEM), zq, kc, kp, vc, vp, tc, tc, tp, tp, gs, gs,
                  pl.BlockSpec((BLK, grp * hd), lambda h, i: (jnp.minimum(i, nb - 1), h))],
        out_specs=(pl.BlockSpec((BLK, grp * hd), lambda h, i: (jnp.minimum(i, nb - 1), h)), shifted, shifted, vec, vec,
                   pl.BlockSpec((1, LANE), lambda h, i: (0, 0))),
        scratch_shapes=[pltpu.VMEM((BLK, hd), F32), pltpu.VMEM((BLK, hd), F32)], compiler_params=_params(),
    )(sinks, z, z, z, z, z, cos_t, sin_t, cos_t, sin_t, qg, kg, dattn)


def _conv_fwd(u, w, b, name):
    t, c = u.shape
    taps = w.shape[0]
    cb = _pick(c, (1408, 1024, 512, 256, 128))
    tr = _row_tile(t, cb, 256 * 1024)
    hb = tr // SUBLANE

    def body(u_ref, halo_ref, w_ref, b_ref, o_ref):
        i = pl.program_id(0)
        x = u_ref[...]
        acc = b_ref[...] + w_ref[taps - 1:taps, :] * x
        for k in range(taps - 1):
            acc = acc + w_ref[k:k + 1, :] * pltpu.roll(x, taps - 1 - k, 0)
        o_ref[...] = acc
        row = lax.broadcasted_iota(jnp.int32, (SUBLANE, cb), 0)
        hp = jnp.where(i > 0, halo_ref[...], 0.0)
        x8 = u_ref[0:SUBLANE, :]
        acc8 = b_ref[...] + w_ref[taps - 1:taps, :] * x8
        for k in range(taps - 1):
            s = taps - 1 - k
            acc8 = acc8 + w_ref[k:k + 1, :] * jnp.where(row < s, pltpu.roll(hp, s, 0), pltpu.roll(x8, s, 0))
        o_ref[0:SUBLANE, :] = acc8

    blk = pl.BlockSpec((tr, cb), lambda i, j: (i, j))
    return pl.pallas_call(
        body, name=name, out_shape=jax.ShapeDtypeStruct((t, c), F32), grid=(t // tr, c // cb),
        in_specs=[blk, pl.BlockSpec((SUBLANE, cb), lambda i, j: (jnp.maximum(i * hb - 1, 0), j)),
                  pl.BlockSpec((taps, cb), lambda i, j: (0, j)), pl.BlockSpec((1, cb), lambda i, j: (0, j))],
        out_specs=blk, compiler_params=_params(),
    )(u, u, w, b)


def _conv_bwd(duc, u, w, name):
    t, c = u.shape
    taps = w.shape[0]
    cb = _pick(c, (1408, 1024, 512, 256, 128))
    tr = _row_tile(t, cb, 256 * 1024)
    hb, nt = tr // SUBLANE, t // tr

    def body(g_ref, gnext_ref, u_ref, uprev_ref, w_ref, du_ref, dw_ref, db_ref):
        i = pl.program_id(1)

        @pl.when(i == 0)
        def _():
            dw_ref[...] = jnp.zeros_like(dw_ref)
            db_ref[...] = jnp.zeros_like(db_ref)

        row = lax.broadcasted_iota(jnp.int32, (SUBLANE, cb), 0)
        g, x = g_ref[...], u_ref[...]
        du = w_ref[taps - 1:taps, :] * g
        for k in range(taps - 1):
            du = du + w_ref[k:k + 1, :] * pltpu.roll(g, tr - (taps - 1 - k), 0)
        du_ref[...] = du
        hn = jnp.where(i < nt - 1, gnext_ref[...], 0.0)
        g8 = g_ref[tr - SUBLANE:tr, :]
        du8 = w_ref[taps - 1:taps, :] * g8
        for k in range(taps - 1):
            s = taps - 1 - k
            du8 = du8 + w_ref[k:k + 1, :] * jnp.where(row >= SUBLANE - s, pltpu.roll(hn, SUBLANE - s, 0),
                                                     pltpu.roll(g8, SUBLANE - s, 0))
        du_ref[tr - SUBLANE:tr, :] = du8

        hp = jnp.where(i > 0, uprev_ref[...], 0.0)
        xl8, gf8 = u_ref[tr - SUBLANE:tr, :], g_ref[0:SUBLANE, :]
        db_ref[...] += jnp.sum(g, axis=0, keepdims=True)
        dw_ref[taps - 1:taps, :] += jnp.sum(g * x, axis=0, keepdims=True)
        for k in range(taps - 1):
            s = taps - 1 - k
            fix = jnp.where(row < s, pltpu.roll(hp, s, 0) - pltpu.roll(xl8, s, 0), 0.0)
            dw_ref[k:k + 1, :] += (jnp.sum(g * pltpu.roll(x, s, 0), axis=0, keepdims=True)
                                   + jnp.sum(gf8 * fix, axis=0, keepdims=True))

    blk = pl.BlockSpec((tr, cb), lambda j, i: (i, j))
    nh = t // SUBLANE
    return pl.pallas_call(
        body, name=name,
        out_shape=(jax.ShapeDtypeStruct((t, c), F32), jax.ShapeDtypeStruct((taps, c), F32),
                   jax.ShapeDtypeStruct((1, c), F32)),
        grid=(c // cb, nt),
        in_specs=[blk, pl.BlockSpec((SUBLANE, cb), lambda j, i: (jnp.minimum((i + 1) * hb, nh - 1), j)),
                  blk, pl.BlockSpec((SUBLANE, cb), lambda j, i: (jnp.maximum(i * hb - 1, 0), j)),
                  pl.BlockSpec((taps, cb), lambda j, i: (0, j))],
        out_specs=(blk, pl.BlockSpec((taps, cb), lambda j, i: (0, j)), pl.BlockSpec((1, cb), lambda j, i: (0, j))),
        compiler_params=_params(),
    )(duc, duc, u, u, w)


def _dense_groups(w, gw):
    n, bw, _ = w.shape
    per = gw // bw
    w4 = w.reshape(n // per, per, bw, bw).astype(BF16)
    eye = jnp.eye(per, dtype=BF16)
    return (w4[:, :, :, None, :] * eye[None, :, None, :, None]).reshape(n // per, gw, gw)


def _diag_blocks(dense, n, bw):
    ng, gw, _ = dense.shape
    per = gw // bw
    diag = jnp.diagonal(dense.reshape(ng, per, bw, per, bw), axis1=1, axis2=3)
    return jnp.moveaxis(diag, -1, 1).reshape(n, bw, bw)


def _gates_fwd(uc, wr, wi, name):
    t, c = uc.shape
    ng, gw, _ = wr.shape
    tr = _pick(t, (512, 256, 128))

    def body(u_ref, wr_ref, wi_ref, r_ref, i_ref):
        a = u_ref[...].astype(BF16)
        r_ref[...] = jnp.dot(a, wr_ref[...], preferred_element_type=F32)
        i_ref[...] = jnp.dot(a, wi_ref[...], preferred_element_type=F32)

    blk = pl.BlockSpec((tr, gw), lambda h, i: (i, h))
    wsp = pl.BlockSpec((None, gw, gw), lambda h, i: (h, 0, 0))
    shp = jax.ShapeDtypeStruct((t, c), F32)
    return pl.pallas_call(body, name=name, out_shape=(shp, shp), grid=(ng, t // tr), in_specs=[blk, wsp, wsp],
                          out_specs=(blk, blk), compiler_params=_params())(uc, wr, wi)


def _gates_bwd_x(duc, drp, dip, wr, wi, name):
    t, c = duc.shape
    ng, gw, _ = wr.shape
    tr = _pick(t, (512, 256, 128))
    dims = (((1,), (1,)), ((), ()))

    def body(d_ref, r_ref, i_ref, wr_ref, wi_ref, o_ref):
        o_ref[...] = (d_ref[...]
                      + lax.dot_general(r_ref[...].astype(BF16), wr_ref[...], dims, preferred_element_type=F32)
                      + lax.dot_general(i_ref[...].astype(BF16), wi_ref[...], dims, preferred_element_type=F32))

    blk = pl.BlockSpec((tr, gw), lambda h, i: (i, h))
    wsp = pl.BlockSpec((None, gw, gw), lambda h, i: (h, 0, 0))
    return pl.pallas_call(body, name=name, out_shape=jax.ShapeDtypeStruct((t, c), F32), grid=(ng, t // tr),
                          in_specs=[blk, blk, blk, wsp, wsp], out_specs=blk, compiler_params=_params())(duc, drp, dip, wr, wi)


def _gates_bwd_w(uc, dpre, ng, gw, name):
    t, c = uc.shape
    tk = _pick(t, (512, 256, 128))
    dims = (((0,), (0,)), ((), ()))

    def body(u_ref, d_ref, o_ref):
        @pl.when(pl.program_id(1) == 0)
        def _():
            o_ref[...] = jnp.zeros_like(o_ref)

        o_ref[...] += lax.dot_general(u_ref[...].astype(BF16), d_ref[...].astype(BF16), dims, preferred_element_type=F32)

    blk = pl.BlockSpec((tk, gw), lambda h, i: (i, h))
    return pl.pallas_call(body, name=name, out_shape=jax.ShapeDtypeStruct((ng, gw, gw), F32), grid=(ng, t // tk),
                          in_specs=[blk, blk], out_specs=pl.BlockSpec((None, gw, gw), lambda h, i: (h, 0, 0)),
                          compiler_params=_params())(uc, dpre)


def _softplus(x):
    return jnp.maximum(x, 0.0) + jnp.log(1.0 + jnp.exp(-jnp.abs(x)))


def _neg_expm1(x):
    series = x * (1.0 + x * (0.5 + x * (1.0 / 6.0 + x * (1.0 / 24.0 + x * (1.0 / 120.0)))))
    return -jnp.where(x > -0.05, series, jnp.exp(x) - 1.0)


_GELU_C = math.sqrt(2.0 / math.pi)


def _gelu_parts(x):
    inner = _GELU_C * (x + 0.044715 * (x * x * x))
    th = jnp.tanh(inner)
    gelu = 0.5 * x * (1.0 + th)
    dgelu = 0.5 * (1.0 + th) + 0.5 * x * (1.0 - th * th) * (_GELU_C * (1.0 + 3.0 * 0.044715 * (x * x)))
    return gelu, dgelu


def _lru_gate_values(uc, rpre, ipre, br, bi, sp):
    r = _sigmoid(rpre + br)
    ig = _sigmoid(ipre + bi)
    a = jnp.exp(-LRU_C * r * sp)
    mult = jnp.sqrt(jnp.maximum(_neg_expm1(2.0 * (-LRU_C * r * sp)), 0.0))
    return r, ig, a, mult


def _lru_fwd(uc, rpre, ipre, gr, br, bi, lam, name, job=None):
    t, c = uc.shape
    cb = _pick(c, (1408, 1024, 512, 256, 128))
    tb = _pick(t, (512, 256, 128))
    ntile = tb // SUBLANE

    def body(uc_ref, r_ref, i_ref, gr_ref, br_ref, bi_ref, lam_ref, h_ref, rec_ref, carry):
        @pl.when(pl.program_id(1) == 0)
        def _():
            carry[...] = jnp.zeros_like(carry)

        sp = _softplus(-lam_ref[...])
        br, bi = br_ref[...], bi_ref[...]
        row = lax.broadcasted_iota(jnp.int32, (SUBLANE, cb), 0)

        def tile(k, c_in):
            sl = pl.ds(pl.multiple_of(k * SUBLANE, SUBLANE), SUBLANE)
            ucv = uc_ref[sl, :]
            _, ig, a, mult = _lru_gate_values(ucv, r_ref[sl, :], i_ref[sl, :], br, bi, sp)
            b = mult * (ig * ucv)
            for d in (1, 2, 4):
                a_s = jnp.where(row >= d, pltpu.roll(a, d, 0), 1.0)
                b_s = jnp.where(row >= d, pltpu.roll(b, d, 0), 0.0)
                b = a * b_s + b
                a = a * a_s
            hv = b + a * c_in
            h_ref[sl, :] = hv
            rec_ref[sl, :] = hv * _gelu_parts(gr_ref[sl, :])[0]
            return hv[SUBLANE - 1:SUBLANE, :]

        c_out = lax.fori_loop(0, ntile, tile, carry[0:1, :])
        carry[...] = jnp.broadcast_to(c_out, (SUBLANE, cb))

    blk = pl.BlockSpec((tb, cb), lambda j, i: (i, j))
    vec = pl.BlockSpec((1, cb), lambda j, i: (0, j))
    shp = jax.ShapeDtypeStruct((t, c), F32)
    res, extra = _call(body, name, (shp, shp), (c // cb, t // tb), [blk, blk, blk, blk, vec, vec, vec], (blk, blk),
                       (uc, rpre, ipre, gr, br, bi, lam), [pltpu.VMEM((SUBLANE, cb), F32)], job)
    return res if job is None else (res, extra)


def _lru_bwd(drec, hst, uc, rpre, ipre, gr, br, bi, lam, name, job=None):
    t, c = uc.shape
    cb = _pick(c, (1408, 1024, 512, 256, 128))
    tb = _pick(t, (256, 128))
    ntile, nt, hb = tb // SUBLANE, t // tb, tb // SUBLANE

    def body(drec_ref, h_ref, hprev_ref, uc_ref, r_ref, i_ref, gr_ref, br_ref, bi_ref, lam_ref,
             dgr_ref, drp_ref, dip_ref, duc_ref, dlam_ref, dbr_ref, dbi_ref, carry):
        step = pl.program_id(1)
        first_block = step == nt - 1

        @pl.when(step == 0)
        def _():
            carry[...] = jnp.zeros_like(carry)
            dlam_ref[...] = jnp.zeros_like(dlam_ref)
            dbr_ref[...] = jnp.zeros_like(dbr_ref)
            dbi_ref[...] = jnp.zeros_like(dbi_ref)

        lam = lam_ref[...]
        sp = _softplus(-lam)
        br, bi = br_ref[...], bi_ref[...]
        row = lax.broadcasted_iota(jnp.int32, (SUBLANE, cb), 0)
        halo = jnp.where(first_block, 0.0, hprev_ref[...])

        def tile(kk, state):
            c_p, acc_sp, acc_br, acc_bi = state
            k = ntile - 1 - kk
            sl = pl.ds(pl.multiple_of(k * SUBLANE, SUBLANE), SUBLANE)
            slp = pl.ds(pl.multiple_of(jnp.maximum(k - 1, 0) * SUBLANE, SUBLANE), SUBLANE)
            ucv = uc_ref[sl, :]
            r, ig, a, mult = _lru_gate_values(ucv, r_ref[sl, :], i_ref[sl, :], br, bi, sp)
            hv = h_ref[sl, :]
            below = jnp.where(k > 0, h_ref[slp, :], halo)
            hprev = jnp.where(row == 0, pltpu.roll(below, 1, 0), pltpu.roll(hv, 1, 0))
            gelu, dgelu = _gelu_parts(gr_ref[sl, :])
            drec = drec_ref[sl, :]
            dh = drec * gelu
            dgr_ref[sl, :] = drec * hv * dgelu
            pa, pb = a, a * dh
            for d in (1, 2, 4):
                a_s = jnp.where(row < SUBLANE - d, pltpu.roll(pa, SUBLANE - d, 0), 1.0)
                b_s = jnp.where(row < SUBLANE - d, pltpu.roll(pb, SUBLANE - d, 0), 0.0)
                pb = pa * b_s + pb
                pa = pa * a_s
            pv = pb + pa * c_p
            gt = dh + jnp.where(row == SUBLANE - 1, c_p, pltpu.roll(pv, SUBLANE - 1, 0))
            da = gt * hprev
            duc_ref[sl, :] = gt * mult * ig
            dmult = gt * ig * ucv
            dig = gt * mult * ucv
            dla = da * a - jnp.where(mult > 0.0, dmult * (a * a) / mult, 0.0)
            drp = dla * (-LRU_C * sp) * (r * (1.0 - r))
            dip = dig * (ig * (1.0 - ig))
            drp_ref[sl, :] = drp
            dip_ref[sl, :] = dip
            return pv[0:1, :], acc_sp + dla * (-LRU_C * r), acc_br + drp, acc_bi + dip

        zero = jnp.zeros((SUBLANE, cb), F32)
        c_out, acc_sp, acc_br, acc_bi = lax.fori_loop(0, ntile, tile, (carry[0:1, :], zero, zero, zero))
        carry[...] = jnp.broadcast_to(c_out, (SUBLANE, cb))
        dlam_ref[...] += jnp.sum(acc_sp, axis=0, keepdims=True) * (-_sigmoid(-lam))
        dbr_ref[...] += jnp.sum(acc_br, axis=0, keepdims=True)
        dbi_ref[...] += jnp.sum(acc_bi, axis=0, keepdims=True)

    blk = pl.BlockSpec((tb, cb), lambda j, i: (nt - 1 - i, j))
    vec = pl.BlockSpec((1, cb), lambda j, i: (0, j))
    halo_spec = pl.BlockSpec((SUBLANE, cb), lambda j, i: (jnp.maximum((nt - 1 - i) * hb - 1, 0), j))
    big, small = jax.ShapeDtypeStruct((t, c), F32), jax.ShapeDtypeStruct((1, c), F32)
    res, extra = _call(
        body, name, (big, big, big, big, small, small, small), (c // cb, nt),
        [blk, blk, halo_spec, blk, blk, blk, blk, vec, vec, vec], (blk, blk, blk, blk, vec, vec, vec),
        (drec, hst, hst, uc, rpre, ipre, gr, br, bi, lam), [pltpu.VMEM((SUBLANE, cb), F32)], job)
    return res if job is None else (res, extra)


def _shard_region(ref, kind, chip, half, rh, width):
    if kind == "col":
        return ref.at[pl.ds(half * rh, rh), pl.ds(chip * width, width)]
    return ref.at[pl.ds(chip * (2 * rh) + half * rh, rh), :]


class _AllGather(_Exchange):
    def __init__(self, fulls, kinds):
        self.inputs, self.kinds = list(fulls), kinds
        self.out_shapes = [jax.ShapeDtypeStruct(f.shape, f.dtype) for f in fulls]
        self.aliases = {a: a for a in range(len(fulls))}
        self.n_sems = 6 * len(fulls)
        self.geo = [(f.shape[0] // 2, f.shape[1] // N_CHIPS) if k == "col" else (f.shape[0] // (2 * N_CHIPS), f.shape[1])
                    for f, k in zip(fulls, kinds)]

    def _region(self, ref, a, chip, half):
        return _shard_region(ref, self.kinds[a], chip, half, *self.geo[a])

    def _ici(self, e, a, k, chip):
        cx, cy = e.chips[k]
        return e.copy(self._region(e.ins[a], a, chip, e.c), self._region(e.outs[a], a, chip, e.c), a * 6 + k,
                      (cx, cy, e.c))

    def _d2d(self, e, a, k, half):
        cx, cy = e.chips[k]
        region = self._region(e.outs[a], a, 2 * cx + cy, half)
        return e.copy(region, region, a * 6 + 3 + k, e.sibling)

    def start(self, e):
        for a in range(len(self.inputs)):
            for k in range(3):
                self._ici(e, a, k, e.me).start()

    def finish(self, e):
        n = len(self.inputs)
        for a in range(n):
            for k, (cx, cy) in enumerate(e.chips):
                self._ici(e, a, k, 2 * cx + cy).wait_recv()
                self._d2d(e, a, k, e.c).start()
        for a in range(n):
            for k in range(3):
                self._d2d(e, a, k, 1 - e.c).wait_recv()
        for a in range(n):
            for k in range(3):
                self._ici(e, a, k, e.me).wait_send()
                self._d2d(e, a, k, e.c).wait_send()


class _SiblingExchange(_Exchange):
    def __init__(self, grads):
        self.inputs = list(grads)
        self.out_shapes = [jax.ShapeDtypeStruct((g.shape[0],) + g.shape[2:], g.dtype) for g in grads]
        self.n_sems = len(grads)

    def _copy(self, e, a):
        return e.copy(e.ins[a].at[:, 1 - e.c], e.outs[a], a, e.sibling)

    def start(self, e):
        for a in range(len(self.inputs)):
            self._copy(e, a).start()

    def finish(self, e):
        for a in range(len(self.inputs)):
            self._copy(e, a).wait()


def _piece(ref, kind, chip, width):
    if kind == "col":
        return ref.at[0, :, pl.ds(chip * width, width)]
    return ref.at[chip]


class _ChipExchange(_Exchange):
    def __init__(self, sums, kinds):
        self.inputs, self.kinds = list(sums), kinds
        self.widths = [s.shape[2] // N_CHIPS if k == "col" else s.shape[2] for s, k in zip(sums, kinds)]
        self.out_shapes = [jax.ShapeDtypeStruct((3, s.shape[1], w), s.dtype) for s, w in zip(sums, self.widths)]
        self.n_sems = 3 * len(sums)

    def _copy(self, e, a, k, chip):
        cx, cy = e.chips[k]
        return e.copy(_piece(e.ins[a], self.kinds[a], chip, self.widths[a]), e.outs[a].at[k], a * 3 + k, (cx, cy, e.c))

    def start(self, e):
        for a in range(len(self.inputs)):
            for k, (cx, cy) in enumerate(e.chips):
                self._copy(e, a, k, 2 * cx + cy).start()

    def finish(self, e):
        for a in range(len(self.inputs)):
            for k, (cx, cy) in enumerate(e.chips):
                self._copy(e, a, k, 2 * cx + cy).wait()


class _FinishExchange(_Exchange):
    def __init__(self, finals, to_all):
        self.inputs, self.to_all = list(finals), list(to_all)
        self.out_shapes = [jax.ShapeDtypeStruct(f.shape, f.dtype) for f in finals]
        self.aliases = {a: a for a in range(len(finals))}
        self.first_sem, self.n_sems = [], 0
        for all8 in self.to_all:
            self.first_sem.append(self.n_sems)
            self.n_sems += 7 if all8 else 1
        self.rel = [(fx, fy, fc) for fx in (0, 1) for fy in (0, 1) for fc in (0, 1)][1:]

    def _copies(self, e, mine):
        for a, all8 in enumerate(self.to_all):
            src = e.ins[a] if mine else e.outs[a]
            if not all8:
                rh = self.inputs[a].shape[0] // 2
                rows = pl.ds((e.c if mine else 1 - e.c) * rh, rh)
                yield e.copy(src.at[rows, :], e.outs[a].at[rows, :], self.first_sem[a], e.sibling)
                continue
            rh = self.inputs[a].shape[0] // (2 * N_CHIPS)
            for r, (fx, fy, fc) in enumerate(self.rel):
                px, py, pc = (1 - e.x if fx else e.x), (1 - e.y if fy else e.y), (1 - e.c if fc else e.c)
                rows = pl.ds(((2 * e.me + e.c) if mine else (2 * (2 * px + py) + pc)) * rh, rh)
                yield e.copy(src.at[rows, :], e.outs[a].at[rows, :], self.first_sem[a] + r, (px, py, pc))

    def start(self, e):
        for cp in self._copies(e, True):
            cp.start()

    def finish(self, e):
        for cp in self._copies(e, False):
            cp.wait_recv()
        for cp in self._copies(e, True):
            cp.wait_send()


def _cast_into_full(w, kind, idx, name):
    r, c = w.shape
    tr = _row_tile(r, c)
    nrb = r // tr

    def body(idx_ref, w_ref, o_ref):
        o_ref[...] = w_ref[...].astype(BF16)

    if kind == "col":
        full, out_map = (r, N_CHIPS * c), (lambda i, idx_ref: (i, idx_ref[1]))
    else:
        full, out_map = (N_CHIPS * r, c), (lambda i, idx_ref: (idx_ref[1] * nrb + i, 0))
    return pl.pallas_call(
        body, name=name, out_shape=jax.ShapeDtypeStruct(full, BF16),
        grid_spec=pltpu.PrefetchScalarGridSpec(
            num_scalar_prefetch=1, grid=(nrb,), in_specs=[pl.BlockSpec((tr, c), lambda i, idx_ref: (i, 0))],
            out_specs=pl.BlockSpec((tr, c), out_map)),
        compiler_params=_params(),
    )(idx, w)


def _add_own_half(g4, recv, idx, out_dtype, name):
    p, _, rh, n = g4.shape
    tr, tc = _tile2d(rh, n, 1024 * 1024)

    def body(idx_ref, g_ref, r_ref, o_ref):
        o_ref[...] = (g_ref[...] + r_ref[...]).astype(out_dtype)

    return pl.pallas_call(
        body, name=name, out_shape=jax.ShapeDtypeStruct((p, rh, n), out_dtype),
        grid_spec=pltpu.PrefetchScalarGridSpec(
            num_scalar_prefetch=1, grid=(p, rh // tr, n // tc),
            in_specs=[pl.BlockSpec((None, None, tr, tc), lambda q, i, j, idx_ref: (q, idx_ref[0], i, j)),
                      pl.BlockSpec((None, tr, tc), lambda q, i, j, idx_ref: (q, i, j))],
            out_specs=pl.BlockSpec((None, tr, tc), lambda q, i, j, idx_ref: (q, i, j))),
        compiler_params=_params(),
    )(idx, g4, recv)


def _sum_chips(own, kind, parts, idx, slots, to_all, name):
    _, rh, w = parts.shape
    tr, tc = _tile2d(rh, w, 512 * 1024)
    nrb, ncb = rh // tr, w // tc

    def body(idx_ref, own_ref, p0, p1, p2, o_ref):
        o_ref[...] = ((own_ref[...].astype(F32) + p0[...].astype(F32)) + p1[...].astype(F32)) + p2[...].astype(F32)

    if kind == "col":
        own_spec = pl.BlockSpec((None, tr, tc), lambda i, j, idx_ref: (0, i, idx_ref[1] * ncb + j))
    else:
        own_spec = pl.BlockSpec((None, tr, tc), lambda i, j, idx_ref: (idx_ref[1], i, j))
    if to_all:
        out_map = lambda i, j, idx_ref: ((2 * idx_ref[1] + idx_ref[0]) * nrb + i, j)
    else:
        out_map = lambda i, j, idx_ref: (idx_ref[0] * nrb + i, j)

    def part(k):
        return pl.BlockSpec((None, tr, tc), lambda i, j, idx_ref: (k, i, j))

    return pl.pallas_call(
        body, name=name, out_shape=jax.ShapeDtypeStruct((slots * rh, w), F32),
        grid_spec=pltpu.PrefetchScalarGridSpec(
            num_scalar_prefetch=1, grid=(nrb, ncb), in_specs=[own_spec, part(0), part(1), part(2)],
            out_specs=pl.BlockSpec((tr, tc), out_map)),
        compiler_params=_params(),
    )(idx, own, parts, parts, parts)


class _Reduce:
    def __init__(self, name, g, kind, idx, wire, to_all):
        r, c = g.shape
        self.name, self.kind, self.idx, self.wire, self.to_all = name, kind, idx, wire, to_all
        self.view = g.reshape(1, 2, r // 2, c) if kind == "col" else g.reshape(N_CHIPS, 2, r // (2 * N_CHIPS), c)

    def sibling(self):
        return _SiblingExchange([self.view])

    def got_sibling(self, outs):
        self.sum = _add_own_half(self.view, outs[0], self.idx, self.wire, "grad_chip_sum_" + self.name)

    def chips(self):
        return _ChipExchange([self.sum], [self.kind])

    def got_chips(self, outs):
        self.total = _sum_chips(self.sum, self.kind, outs[0], self.idx, 2 * N_CHIPS if self.to_all else 2,
                                self.to_all, "grad_total_" + self.name)


def _pack(arrays, rows):
    flat = jnp.concatenate([a.reshape(-1) for a in arrays])
    return jnp.pad(flat, (0, rows * SMALL_PACK_COLS - flat.shape[0])).reshape(rows, SMALL_PACK_COLS)


def _unpack(packed, shapes):
    flat = packed.reshape(-1)
    out, o = [], 0
    for shp in shapes:
        size = math.prod(shp)
        out.append(flat[o:o + size].reshape(shp))
        o += size
    return out


def _pack_rows(shapes):
    total = sum(math.prod(s) for s in shapes)
    unit = SMALL_PACK_COLS * N_CHIPS * 2 * SUBLANE
    return -(-total // unit) * (N_CHIPS * 2 * SUBLANE)


BIG = ("w_in", "w_attn_proj", "w_lru_proj", "w_out", "w_ffn_gate", "w_ffn_up", "w_ffn_down")
BIG_KIND = {"w_in": "col", "w_attn_proj": "row", "w_lru_proj": "row", "w_out": "row", "w_ffn_gate": "col",
            "w_ffn_up": "col", "w_ffn_down": "row"}
SMALL = ("norm1_g", "b_gates", "q_norm_g", "k_norm_g", "sinks", "conv_w", "conv_b", "w_rgate", "b_rgate",
         "w_igate", "b_igate", "lru_lambda", "norm2_g")
PACKED = tuple(n for n in SMALL if n not in ("w_rgate", "w_igate"))
WEIGHTS = ("norm1_g", "w_in", "b_gates", "q_norm_g", "k_norm_g", "sinks", "conv_w", "conv_b", "w_rgate", "b_rgate",
           "w_igate", "b_igate", "lru_lambda", "w_attn_proj", "w_lru_proj", "w_out", "norm2_g", "w_ffn_gate",
           "w_ffn_up", "w_ffn_down")


def kernel(x, positions, norm1_g, w_in, b_gates, q_norm_g, k_norm_g, sinks, conv_w, conv_b, w_rgate, b_rgate, w_igate, b_igate, lru_lambda, w_attn_proj, w_lru_proj, w_out, norm2_g, w_ffn_gate, w_ffn_up, w_ffn_down, loss_target, m_norm1_g, m_w_in, m_b_gates, m_q_norm_g, m_k_norm_g, m_sinks, m_conv_w, m_conv_b, m_w_rgate, m_b_rgate, m_w_igate, m_b_igate, m_lru_lambda, m_w_attn_proj, m_w_lru_proj, m_w_out, m_norm2_g, m_w_ffn_gate, m_w_ffn_up, m_w_ffn_down, v_norm1_g, v_w_in, v_b_gates, v_q_norm_g, v_k_norm_g, v_sinks, v_conv_w, v_conv_b, v_w_rgate, v_b_rgate, v_w_igate, v_b_igate, v_lru_lambda, v_w_attn_proj, v_w_lru_proj, v_w_out, v_norm2_g, v_w_ffn_gate, v_w_ffn_up, v_w_ffn_down):
    args = dict(locals())
    w = {n: args[n] for n in WEIGHTS}
    mom = {n: args["m_" + n] for n in WEIGHTS}
    var = {n: args["v_" + n] for n in WEIGHTS}

    t, d = x.shape[1], x.shape[2]
    hd = q_norm_g.shape[-1]
    nq = sinks.shape[-1]
    q_w = nq * hd
    d_rnn = conv_b.shape[-1]
    taps = conv_w.shape[1]
    n_blocks, bw = w_rgate.shape[1], w_rgate.shape[2]
    in_w = w_in.shape[-1] * N_CHIPS
    kv_w = (in_w - q_w - 2 * d_rnn - 2 * d) // 2
    kv = kv_w // hd
    grp = nq // kv
    u_off = q_w + 2 * kv_w
    gr_off = u_off + d_rnn
    ga_off = gr_off + d_rnn
    gw = bw * LANE // math.gcd(bw, LANE)
    ng = d_rnn // gw
    chip = 2 * lax.axis_index("x") + lax.axis_index("y")
    idx = jnp.stack([lax.axis_index("c"), chip]).astype(jnp.int32)

    x2, tgt = x[0], loss_target[0]

    placed = {n: _cast_into_full(w[n][0], BIG_KIND[n], idx, "cast_" + n) for n in BIG}

    def gather(*names):
        return _AllGather([placed[n] for n in names], [BIG_KIND[n] for n in names])

    (win_f,) = _run_exchange(gather("w_in"), "allgather_w_in")
    conv_w_full = _gather_small(conv_w[0], "allgather_conv_w")
    conv_w_full = jnp.transpose(conv_w_full, (1, 0, 2)).reshape(taps, d_rnn)
    wr_dense = _dense_groups(w_rgate[0], gw)
    wi_dense = _dense_groups(w_igate[0], gw)

    inv_freq = ROPE_THETA ** (-jnp.arange(0, hd // 4, 2, dtype=F32) / (hd // 4))
    ang = positions[0].astype(F32)[:, None] * inv_freq
    cos, sin = jnp.cos(ang), jnp.sin(ang)
    rest = hd - 2 * cos.shape[1]
    cos_t = jnp.concatenate([cos, cos, jnp.ones((t, rest), F32)], axis=1)
    sin_t = jnp.concatenate([-sin, sin, jnp.zeros((t, rest), F32)], axis=1)
    sinks1 = sinks[0]

    xn = _rms_fwd(x2, norm1_g, "rms1_fwd")
    z, (wap_f, wlp_f, wout_f, wg_f) = _matmul(xn, win_f, "nn", "in_proj",
                                             job=gather("w_attn_proj", "w_lru_proj", "w_out", "w_ffn_gate"))
    zu, zgr = z[:, u_off:u_off + d_rnn], z[:, gr_off:gr_off + d_rnn]
    attn, (wu_f,) = _attn_fwd(z, cos_t, sin_t, q_norm_g, k_norm_g, sinks1, kv, grp, hd, "attn_fwd",
                              job=gather("w_ffn_up"))
    uc = _conv_fwd(zu, conv_w_full, conv_b, "conv_fwd")
    rpre, ipre = _gates_fwd(uc, wr_dense, wi_dense, "gates_fwd")
    (hst, rec), (wd_f,) = _lru_fwd(uc, rpre, ipre, zgr, b_rgate, b_igate, lru_lambda, "lru_fwd",
                                   job=gather("w_ffn_down"))
    pa = _matmul(attn, wap_f, "nn", "attn_proj")
    plru = _matmul(rec, wlp_f, "nn", "lru_proj")
    merged = _merge_fwd(z, b_gates, pa, plru, ga_off, "merge_fwd")
    h1 = _matmul(merged, wout_f, "nn", "out_proj", add=x2)
    hn = _rms_fwd(h1, norm2_g, "rms2_fwd")
    gate = _matmul(hn, wg_f, "nn", "ffn_gate")
    up = _matmul(hn, wu_f, "nn", "ffn_up")
    act = _swiglu_fwd(gate, up, "swiglu_fwd")
    yout = _matmul(act, wd_f, "nn", "ffn_down", add=h1)
    dy, loss_part = _loss_head(yout, tgt, "loss_head")
    loss = lax.psum(loss_part[0, 0], ("x", "y", "c"))

    def reduction(n, g):
        return _Reduce(n, g, BIG_KIND[n], idx, BF16, False)

    r_wd = reduction("w_ffn_down", _matmul(act, dy, "tn", "d_w_ffn_down"))
    dact, got = _matmul(dy, wd_f, "nt", "d_act", job=r_wd.sibling())
    r_wd.got_sibling(got)
    (dgate, dup), got = _swiglu_bwd(dact, gate, up, "swiglu_bwd", job=r_wd.chips())
    r_wd.got_chips(got)
    r_wg = reduction("w_ffn_gate", _matmul(hn, dgate, "tn", "d_w_ffn_gate"))
    g_wu, got = _matmul(hn, dup, "tn", "d_w_ffn_up", job=r_wg.sibling())
    r_wg.got_sibling(got)
    r_wu = reduction("w_ffn_up", g_wu)
    both = _Jobs(r_wu.sibling(), r_wg.chips())
    dhn, got = _matmul(dgate, wg_f, "nt", "d_hn_gate", job=both)
    got_wu, got_wg = both.split(got)
    r_wu.got_sibling(got_wu)
    r_wg.got_chips(got_wg)
    dhn, got = _matmul(dup, wu_f, "nt", "d_hn_up", add=dhn, job=r_wu.chips())
    r_wu.got_chips(got)
    dh1, g_norm2 = _rms_bwd(dhn, h1, norm2_g, dy, "rms2_bwd")
    r_wout = reduction("w_out", _matmul(merged, dh1, "tn", "d_w_out"))
    dmerged, got = _matmul(dh1, wout_f, "nt", "d_merged", job=r_wout.sibling())
    r_wout.got_sibling(got)
    (dpa, dpl, dga, dgl, g_ba, g_bl), got = _merge_bwd(dmerged, z, b_gates, pa, plru, ga_off, "merge_bwd",
                                                       job=r_wout.chips())
    r_wout.got_chips(got)
    r_wap = reduction("w_attn_proj", _matmul(attn, dpa, "tn", "d_w_attn_proj"))
    dattn, got = _matmul(dpa, wap_f, "nt", "d_attn", job=r_wap.sibling())
    r_wap.got_sibling(got)
    g_wlp, got = _matmul(rec, dpl, "tn", "d_w_lru_proj", job=r_wap.chips())
    r_wap.got_chips(got)
    r_wlp = reduction("w_lru_proj", g_wlp)
    drec, got = _matmul(dpl, wlp_f, "nt", "d_rec", job=r_wlp.sibling())
    r_wlp.got_sibling(got)
    (dgr, drp, dip, duc_direct, g_lam, g_br, g_bi), got = _lru_bwd(
        drec, hst, uc, rpre, ipre, zgr, b_rgate, b_igate, lru_lambda, "lru_bwd", job=r_wlp.chips())
    r_wlp.got_chips(got)
    duc = _gates_bwd_x(duc_direct, drp, dip, wr_dense, wi_dense, "gates_bwd_x")
    g_wr = _diag_blocks(_gates_bwd_w(uc, drp, ng, gw, "gates_bwd_wr"), n_blocks, bw)
    g_wi = _diag_blocks(_gates_bwd_w(uc, dip, ng, gw, "gates_bwd_wi"), n_blocks, bw)
    du, g_convw, g_convb = _conv_bwd(duc, zu, conv_w_full, "conv_bwd")
    dq, dk, dv, g_qg, g_kg, g_sinks = _attn_bwd(dattn, z, cos_t, sin_t, q_norm_g, k_norm_g, sinks1, kv, grp, hd,
                                                 "attn_bwd")
    dz = jnp.concatenate([dq, dk, dv, du, dgr, dga, dgl], axis=1).astype(BF16)
    r_wr = _Reduce("w_rgate", g_wr.reshape(n_blocks * bw, bw), "row", idx, F32, True)
    r_wi = _Reduce("w_igate", g_wi.reshape(n_blocks * bw, bw), "row", idx, F32, True)
    both = _Jobs(r_wr.sibling(), r_wi.sibling())
    g_top, got = _matmul(xn, dz, "tn", "d_w_in_top", m_window=(0, d // 2), job=both)
    got_wr, got_wi = both.split(got)
    r_wr.got_sibling(got_wr)
    r_wi.got_sibling(got_wi)
    r_top = _Reduce("w_in_top", g_top, "col", idx, BF16, False)
    three = _Jobs(r_top.sibling(), r_wr.chips(), r_wi.chips())
    g_bot, got = _matmul(xn, dz, "tn", "d_w_in_bot", m_window=(d // 2, d // 2), job=three)
    got_top, got_wr, got_wi = three.split(got)
    r_top.got_sibling(got_top)
    r_wr.got_chips(got_wr)
    r_wi.got_chips(got_wi)
    r_bot = _Reduce("w_in_bot", g_bot, "col", idx, BF16, False)
    both = _Jobs(r_top.chips(), r_bot.sibling())
    dxn, got = _matmul(dz, win_f, "nt", "d_xn_a", m_window=(0, t // 2), into=(None, t), job=both)
    got_top, got_bot = both.split(got)
    r_top.got_chips(got_top)
    r_bot.got_sibling(got_bot)
    dxn, got = _matmul(dz, win_f, "nt", "d_xn_b", m_window=(t // 2, t // 2), into=(dxn, t), job=r_bot.chips())
    r_bot.got_chips(got)
    dx, g_norm1 = _rms_bwd(dxn, x2, norm1_g, dh1, "rms1_bwd")

    small_grads = {"norm1_g": g_norm1, "b_gates": jnp.concatenate([g_ba, g_bl], axis=1), "q_norm_g": g_qg,
                   "k_norm_g": g_kg, "sinks": g_sinks[:, :nq], "conv_w": g_convw, "conv_b": g_convb,
                   "b_rgate": g_br, "b_igate": g_bi, "lru_lambda": g_lam, "norm2_g": g_norm2}
    gshapes = [small_grads[n].shape for n in PACKED]
    r_small = _Reduce("small", _pack([small_grads[n] for n in PACKED], _pack_rows(gshapes)), "row", idx, F32, True)
    r_small.got_sibling(_run_exchange(r_small.sibling(), "grad_sibling_exchange_small"))
    r_small.got_chips(_run_exchange(r_small.chips(), "grad_chip_exchange_small"))
    sharded = [r_top, r_bot, r_wap, r_wlp, r_wout, r_wg, r_wu, r_wd]
    everywhere = [r_wr, r_wi, r_small]
    reduced = _run_exchange(_FinishExchange([r.total for r in sharded + everywhere],
                                            [False] * len(sharded) + [True] * len(everywhere)), "grad_finish_exchange")
    grads = dict(zip(BIG[1:], reduced[2:len(sharded)]))
    grads["w_in"] = jnp.concatenate(reduced[:2], axis=0)
    grads["w_rgate"], grads["w_igate"] = reduced[len(sharded)], reduced[len(sharded) + 1]
    small_full = dict(zip(PACKED, _unpack(reduced[-1], gshapes)))
    per = d_rnn // N_CHIPS
    small_full["conv_w"] = lax.dynamic_slice(small_full["conv_w"], (0, chip * per), (taps, per))
    grads.update(small_full)

    delta, new_m, new_v = {}, {}, {}
    for n in BIG + ("w_rgate", "w_igate"):
        as2d = (lambda a: a[0]) if n in BIG else (lambda a: a.reshape(n_blocks * bw, bw))
        delta[n], new_m[n], new_v[n] = _adamw(as2d(w[n]), grads[n], as2d(mom[n]), as2d(var[n]), "adamw_" + n)
    pshapes = [w[n].shape for n in PACKED]
    prows = _pack_rows(pshapes)
    pk = [_pack([src[n] for n in PACKED], prows) for src in (w, grads, mom, var)]
    for res, packed in zip((delta, new_m, new_v), _adamw(pk[0], pk[1], pk[2], pk[3], "adamw_small")):
        res.update(dict(zip(PACKED, _unpack(packed, pshapes))))

    outs = [loss, dx.reshape(x.shape)]
    for res in (grads, delta, new_m, new_v):
        outs += [res[n].reshape(w[n].shape) for n in WEIGHTS]
    return tuple(outs)


def _gather_small(shard, name):
    def body(s_ref, o_ref, send_sems, recv_sems):
        e = _Env((s_ref,), (o_ref,), send_sems, recv_sems)
        o_ref[e.me] = s_ref[...]
        for k, (cx, cy) in enumerate(e.chips):
            e.copy(s_ref, o_ref.at[e.me], k, (cx, cy, e.c)).start()
        for k, (cx, cy) in enumerate(e.chips):
            e.copy(s_ref, o_ref.at[2 * cx + cy], k, (cx, cy, e.c)).wait_recv()
        for k, (cx, cy) in enumerate(e.chips):
            e.copy(s_ref, o_ref.at[e.me], k, (cx, cy, e.c)).wait_send()

    vm = pl.BlockSpec(memory_space=pltpu.VMEM)
    return pl.pallas_call(body, name=name, out_shape=jax.ShapeDtypeStruct((N_CHIPS,) + shard.shape, shard.dtype),
                          in_specs=[vm], out_specs=vm,
                          scratch_shapes=[pltpu.SemaphoreType.DMA((3,)), pltpu.SemaphoreType.DMA((3,))])(shard)
```

```python
import functools
import math

import jax
import jax.numpy as jnp
from jax import lax
from jax.experimental import pallas as pl
from jax.experimental.pallas import tpu as pltpu

F32 = jnp.float32
BF16 = jnp.bfloat16
MESH = pl.DeviceIdType.MESH

WINDOW = 128
BLK = 128
ROPE_THETA = 500000.0
LRU_C = 8.0
EPS = 1e-6
NEG = -1e30
ADAM_LR = 0.001
ADAM_B1 = 0.9
ADAM_B2 = 0.999
ADAM_EPS = 1e-08
ADAM_WD = 0.01
ADAM_STEP = 10

VMEM_LIMIT_BYTES = 52 * 1024 * 1024
LANE = 128
SUBLANE = 8
N_CHIPS = 4
SMALL_PACK_COLS = 512


def _params(**kw):
    return pltpu.CompilerParams(vmem_limit_bytes=VMEM_LIMIT_BYTES, **kw)


def _pick(dim, cands):
    for c in cands:
        if dim % c == 0:
            return c
    return dim


def _sigmoid(x):
    return 1.0 / (1.0 + jnp.exp(-x))


ANY = pl.BlockSpec(memory_space=pl.ANY)


class _Env:
    def __init__(self, ins, outs, send, recv, sem0=0, place=None):
        self.ins, self.outs, self.send, self.recv, self.sem0 = ins, outs, send, recv, sem0
        self.x, self.y, self.c = place or (lax.axis_index("x"), lax.axis_index("y"), lax.axis_index("c"))
        self.me = 2 * self.x + self.y
        self.chips = [(1 - self.x, self.y), (self.x, 1 - self.y), (1 - self.x, 1 - self.y)]
        self.sibling = (self.x, self.y, 1 - self.c)

    def sub(self, i0, n_in, o0, n_out, sem0):
        return _Env(self.ins[i0:i0 + n_in], self.outs[o0:o0 + n_out], self.send, self.recv, self.sem0 + sem0,
                    (self.x, self.y, self.c))

    def copy(self, src, dst, sem, to):
        return pltpu.make_async_remote_copy(src_ref=src, dst_ref=dst, send_sem=self.send.at[self.sem0 + sem],
                                            recv_sem=self.recv.at[self.sem0 + sem], device_id=to, device_id_type=MESH)


class _Exchange:
    inputs, out_shapes, aliases, n_sems = (), (), {}, 0

    def start(self, e):
        raise NotImplementedError

    def finish(self, e):
        raise NotImplementedError


class _Jobs(_Exchange):
    def __init__(self, *jobs):
        self.jobs, self.inputs, self.out_shapes, self.aliases, self.n_sems, self.at = jobs, [], [], {}, 0, []
        for job in jobs:
            self.at.append((len(self.inputs), len(self.out_shapes), self.n_sems))
            self.aliases.update({len(self.inputs) + i: len(self.out_shapes) + o for i, o in job.aliases.items()})
            self.inputs += list(job.inputs)
            self.out_shapes += list(job.out_shapes)
            self.n_sems += job.n_sems

    def _each(self, e):
        for job, (i0, o0, s0) in zip(self.jobs, self.at):
            yield job, e.sub(i0, len(job.inputs), o0, len(job.out_shapes), s0)

    def split(self, outs):
        return [tuple(outs[o0:o0 + len(job.out_shapes)]) for job, (_, o0, _) in zip(self.jobs, self.at)]

    def start(self, e):
        for job, se in self._each(e):
            job.start(se)

    def finish(self, e):
        for job, se in self._each(e):
            job.finish(se)


def _call(body, name, out_shape, grid, in_specs, out_specs, args, scratch_shapes=(), job=None, aliases=None):
    aliases = dict(aliases or {})
    if job is None:
        return pl.pallas_call(body, name=name, out_shape=out_shape, grid=grid, in_specs=list(in_specs),
                              out_specs=out_specs, scratch_shapes=list(scratch_shapes), input_output_aliases=aliases,
                              compiler_params=_params())(*args), ()
    single = not isinstance(out_shape, (tuple, list))
    shapes = [out_shape] if single else list(out_shape)
    ospecs = [out_specs] if single else list(out_specs)
    n_in, n_out, n_scr = len(args), len(shapes), len(scratch_shapes)
    j_in, j_out = len(job.inputs), len(job.out_shapes)

    def hosted(*refs):
        ins, jins = refs[:n_in], refs[n_in:n_in + j_in]
        outs = refs[n_in + j_in:n_in + j_in + n_out]
        jouts = refs[n_in + j_in + n_out:n_in + j_in + n_out + j_out]
        rest = refs[n_in + j_in + n_out + j_out:]
        e = _Env(jins, jouts, rest[n_scr], rest[n_scr + 1])
        first = functools.reduce(jnp.logical_and, [pl.program_id(d) == 0 for d in range(len(grid))])
        last = functools.reduce(jnp.logical_and, [pl.program_id(d) == g - 1 for d, g in enumerate(grid)])

        @pl.when(first)
        def _():
            job.start(e)

        body(*ins, *outs, *rest[:n_scr])

        @pl.when(last)
        def _():
            job.finish(e)

    res = pl.pallas_call(
        hosted, name=name, out_shape=tuple(shapes + list(job.out_shapes)), grid=grid,
        in_specs=list(in_specs) + [ANY] * j_in, out_specs=tuple(ospecs + [ANY] * j_out),
        scratch_shapes=list(scratch_shapes) + [pltpu.SemaphoreType.DMA((job.n_sems,)),
                                               pltpu.SemaphoreType.DMA((job.n_sems,))],
        input_output_aliases={**aliases, **{n_in + i: n_out + o for i, o in job.aliases.items()}},
        compiler_params=_params())(*args, *job.inputs)
    return (res[0] if single else tuple(res[:n_out])), tuple(res[n_out:])


def _run_exchange(job, name):
    n_in, n_out = len(job.inputs), len(job.out_shapes)

    def body(*refs):
        e = _Env(refs[:n_in], refs[n_in:n_in + n_out], refs[n_in + n_out], refs[n_in + n_out + 1])
        job.start(e)
        job.finish(e)

    return pl.pallas_call(
        body, name=name, out_shape=tuple(job.out_shapes), in_specs=[ANY] * n_in, out_specs=tuple([ANY] * n_out),
        input_output_aliases=dict(job.aliases),
        scratch_shapes=[pltpu.SemaphoreType.DMA((job.n_sems,)), pltpu.SemaphoreType.DMA((job.n_sems,))],
    )(*job.inputs)


_M_TILES = (1024, 1408, 1280, 512, 256, 128)
_N_TILES = (1408, 1280, 1024, 640, 512, 256, 128)
MXU_FULL_ROWS = 1024
MATMUL_VMEM_BUDGET = 42 * 1024 * 1024


def _matmul_tiles(m, n, k, sa, sb, so, has_add):
    best = None
    for tm in [c for c in _M_TILES if m % c == 0] or [m]:
        for tn in [c for c in _N_TILES if n % c == 0] or [n]:
            for nk in range(1, 17):
                tk = k // nk
                if k % nk or tk % LANE:
                    continue
                need = 2 * (tm * tk * sa + tk * tn * sb) + 2 * tm * tn * (so + (4 if has_add else 0))
                need += tm * tn * 4 if nk > 1 else 0
                if need > MATMUL_VMEM_BUDGET:
                    continue
                key = (min(tm, MXU_FULL_ROWS), -nk, tn, tm)
                if best is None or key > best[0]:
                    best = (key, (tm, tn, tk))
    assert best is not None, (m, n, k)
    return best[1]


def _matmul(a, b, mode, name, add=None, out_dtype=F32, job=None, m_window=None, into=None):
    if mode == "nn":
        (m, k), (k2, n) = a.shape, b.shape
    elif mode == "nt":
        (m, k), (n, k2) = a.shape, b.shape
    else:
        (k, m), (k2, n) = a.shape, b.shape
    assert k == k2, (a.shape, b.shape, mode)
    m0, m = m_window or (0, m)
    tm, tn, tk = _matmul_tiles(math.gcd(m, m0) if m0 else m, n, k, a.dtype.itemsize, b.dtype.itemsize,
                               jnp.dtype(out_dtype).itemsize, add is not None)
    nk, mb0 = k // tk, m0 // tm
    if mode == "nn":
        a_spec = pl.BlockSpec((tm, tk), lambda i, j, kk: (mb0 + i, kk))
        b_spec = pl.BlockSpec((tk, tn), lambda i, j, kk: (kk, j))
        dims = (((1,), (0,)), ((), ()))
    elif mode == "nt":
        a_spec = pl.BlockSpec((tm, tk), lambda i, j, kk: (mb0 + i, kk))
        b_spec = pl.BlockSpec((tn, tk), lambda i, j, kk: (j, kk))
        dims = (((1,), (1,)), ((), ()))
    else:
        a_spec = pl.BlockSpec((tk, tm), lambda i, j, kk: (kk, mb0 + i))
        b_spec = pl.BlockSpec((tk, tn), lambda i, j, kk: (kk, j))
        dims = (((0,), (0,)), ((), ()))
    out_rows, ob0 = (into[1], mb0) if into is not None else (m, 0)
    o_spec = pl.BlockSpec((tm, tn), lambda i, j, kk: (ob0 + i, j))
    has_add = add is not None
    begun = into is not None and into[0] is not None

    def body(*refs):
        a_ref, b_ref = refs[:2]
        add_ref = refs[2] if has_add else None
        part = lax.dot_general(a_ref[...].astype(BF16), b_ref[...].astype(BF16), dims, preferred_element_type=F32)
        if nk == 1:
            o_ref = refs[-1]
            o_ref[...] = (part + add_ref[...] if has_add else part).astype(out_dtype)
            return
        o_ref, acc = refs[-2:]
        kk = pl.program_id(2)

        @pl.when(kk == 0)
        def _():
            acc[...] = part

        @pl.when(kk > 0)
        def _():
            acc[...] += part

        @pl.when(kk == nk - 1)
        def _():
            r = acc[...]
            if has_add:
                r = r + add_ref[...]
            o_ref[...] = r.astype(out_dtype)

    in_specs = [a_spec, b_spec] + ([pl.BlockSpec((tm, tn), lambda i, j, kk: (mb0 + i, j))] if has_add else [])
    args = (a, b) + ((add,) if has_add else ())
    aliases = None
    if begun:
        aliases = {len(args): 0}
        in_specs, args = in_specs + [ANY], args + (into[0],)
    res, extra = _call(body, name, jax.ShapeDtypeStruct((out_rows, n), out_dtype), (m // tm, n // tn, nk), in_specs,
                       o_spec, args, [pltpu.VMEM((tm, tn), F32)] if nk > 1 else [], job, aliases)
    return res if job is None else (res, extra)


def _row_tile(rows, cols, budget_elems=512 * 1024):
    cands = [c for c in (1024, 704, 512, 352, 256, 128, 64, 32, 16) if c * cols <= budget_elems]
    return _pick(rows, cands or (16,))


_EW_COLS = (1280, 1408, 1024, 640, 512, 256, 128)


def _tile2d(rows, cols, max_elems):
    tc = _pick(cols, _EW_COLS)
    return _row_tile(rows, tc, max_elems), tc


def _rms_fwd(x, g, name):
    t, d = x.shape
    tr = _row_tile(t, d)

    def body(x_ref, g_ref, o_ref):
        xv = x_ref[...]
        rstd = lax.rsqrt(jnp.mean(xv * xv, axis=-1, keepdims=True) + EPS)
        o_ref[...] = (xv * rstd * g_ref[...]).astype(BF16)

    spec = pl.BlockSpec((tr, d), lambda i: (i, 0))
    return pl.pallas_call(body, name=name, out_shape=jax.ShapeDtypeStruct((t, d), BF16), grid=(t // tr,),
                          in_specs=[spec, pl.BlockSpec((1, d), lambda i: (0, 0))], out_specs=spec,
                          compiler_params=_params())(x, g)


def _rms_bwd(dxn, x, g, resid, name, job=None, mxu_copy=False):
    t, d = x.shape
    tr = _row_tile(t, d, 256 * 1024)

    def body(dxn_ref, x_ref, g_ref, r_ref, dx_ref, dg_ref, *dx16_ref):
        @pl.when(pl.program_id(0) == 0)
        def _():
            dg_ref[...] = jnp.zeros_like(dg_ref)

        xv = x_ref[...]
        rstd = lax.rsqrt(jnp.mean(xv * xv, axis=-1, keepdims=True) + EPS)
        xhat = xv * rstd
        dy = dxn_ref[...]
        dg_ref[...] += jnp.sum(dy * xhat, axis=0, keepdims=True)
        dxhat = dy * g_ref[...]
        dx = r_ref[...] + rstd * (dxhat - xhat * jnp.mean(dxhat * xhat, axis=-1, keepdims=True))
        dx_ref[...] = dx
        if mxu_copy:
            dx16_ref[0][...] = dx.astype(BF16)

    spec = pl.BlockSpec((tr, d), lambda i: (i, 0))
    vec = pl.BlockSpec((1, d), lambda i: (0, 0))
    shapes = (jax.ShapeDtypeStruct((t, d), F32), jax.ShapeDtypeStruct((1, d), F32))
    shapes += (jax.ShapeDtypeStruct((t, d), BF16),) if mxu_copy else ()
    res, extra = _call(body, name, shapes, (t // tr,), [spec, spec, vec, spec],
                       (spec, vec) + ((spec,) if mxu_copy else ()), (dxn, x, g, resid), (), job)
    return res if job is None else (res, extra)


def _swiglu_fwd(gate, up, name):
    t, f = gate.shape
    tr, tc = _tile2d(t, f, 1024 * 1024)

    def body(g_ref, u_ref, o_ref):
        gv = g_ref[...]
        o_ref[...] = (gv * _sigmoid(gv) * u_ref[...]).astype(BF16)

    spec = pl.BlockSpec((tr, tc), lambda i, j: (i, j))
    return pl.pallas_call(body, name=name, out_shape=jax.ShapeDtypeStruct((t, f), BF16), grid=(t // tr, f // tc),
                          in_specs=[spec, spec], out_specs=spec, compiler_params=_params())(gate, up)


def _swiglu_bwd(dact, gate, up, name, job=None):
    t, f = gate.shape
    tr, tc = _tile2d(t, f, 768 * 1024)

    def body(d_ref, g_ref, u_ref, dg_ref, du_ref):
        gv, dv = g_ref[...], d_ref[...]
        sg = _sigmoid(gv)
        dg_ref[...] = (dv * u_ref[...] * (sg * (1.0 + gv * (1.0 - sg)))).astype(BF16)
        du_ref[...] = (dv * (gv * sg)).astype(BF16)

    spec = pl.BlockSpec((tr, tc), lambda i, j: (i, j))
    shp = jax.ShapeDtypeStruct((t, f), BF16)
    res, extra = _call(body, name, (shp, shp), (t // tr, f // tc), [spec, spec, spec], (spec, spec), (dact, gate, up),
                       (), job)
    return res if job is None else (res, extra)


def _merge_fwd(z, b_gates, pa, plru, ga_off, name):
    t, d = pa.shape
    cw = _pick(math.gcd(ga_off, d), (512, 256, 128))
    tr = _row_tile(t, cw, 256 * 1024)
    oa, ol, nd = ga_off // cw, (ga_off + d) // cw, d // cw

    def body(ga_ref, gl_ref, ba_ref, bl_ref, pa_ref, pl_ref, o_ref):
        sa = _sigmoid(ga_ref[...] + ba_ref[...])
        sl = _sigmoid(gl_ref[...] + bl_ref[...])
        o_ref[...] = (sa * pa_ref[...] + sl * pl_ref[...]).astype(BF16)

    blk = pl.BlockSpec((tr, cw), lambda i, j: (i, j))
    return pl.pallas_call(
        body, name=name, out_shape=jax.ShapeDtypeStruct((t, d), BF16), grid=(t // tr, nd),
        in_specs=[pl.BlockSpec((tr, cw), lambda i, j: (i, oa + j)), pl.BlockSpec((tr, cw), lambda i, j: (i, ol + j)),
                  pl.BlockSpec((1, cw), lambda i, j: (0, j)), pl.BlockSpec((1, cw), lambda i, j: (0, nd + j)),
                  blk, blk],
        out_specs=blk, compiler_params=_params(),
    )(z, z, b_gates, b_gates, pa, plru)


def _merge_bwd(dmerged, z, b_gates, pa, plru, ga_off, name, job=None):
    t, d = pa.shape
    cw = _pick(math.gcd(ga_off, d), (512, 256, 128))
    tr = _row_tile(t, cw, 256 * 1024)
    oa, ol, nd = ga_off // cw, (ga_off + d) // cw, d // cw

    def body(dm_ref, ga_ref, gl_ref, ba_ref, bl_ref, pa_ref, pl_ref, dpa_ref, dpl_ref, dga_ref, dgl_ref, sa_ref, sl_ref):
        @pl.when(pl.program_id(1) == 0)
        def _():
            sa_ref[...] = jnp.zeros_like(sa_ref)
            sl_ref[...] = jnp.zeros_like(sl_ref)

        dm = dm_ref[...]
        sa = _sigmoid(ga_ref[...] + ba_ref[...])
        sl = _sigmoid(gl_ref[...] + bl_ref[...])
        dpa_ref[...] = (dm * sa).astype(BF16)
        dpl_ref[...] = (dm * sl).astype(BF16)
        dga = dm * pa_ref[...] * (sa * (1.0 - sa))
        dgl = dm * pl_ref[...] * (sl * (1.0 - sl))
        dga_ref[...] = dga
        dgl_ref[...] = dgl
        sa_ref[...] += jnp.sum(dga, axis=0, keepdims=True)
        sl_ref[...] += jnp.sum(dgl, axis=0, keepdims=True)

    blk = pl.BlockSpec((tr, cw), lambda j, i: (i, j))
    vec = pl.BlockSpec((1, cw), lambda j, i: (0, j))
    big16, big32, v32 = (jax.ShapeDtypeStruct((t, d), BF16), jax.ShapeDtypeStruct((t, d), F32),
                         jax.ShapeDtypeStruct((1, d), F32))
    res, extra = _call(
        body, name, (big16, big16, big32, big32, v32, v32), (nd, t // tr),
        [blk, pl.BlockSpec((tr, cw), lambda j, i: (i, oa + j)), pl.BlockSpec((tr, cw), lambda j, i: (i, ol + j)),
         vec, pl.BlockSpec((1, cw), lambda j, i: (0, nd + j)), blk, blk],
        (blk, blk, blk, blk, vec, vec), (dmerged, z, z, b_gates, b_gates, pa, plru), (), job)
    return res if job is None else (res, extra)


def _loss_head(y, target, name):
    t, d = y.shape
    tr = _row_tile(t, d, 256 * 1024)
    nt = t // tr

    def body(y_ref, t_ref, dy_ref, dy16_ref, loss_ref, acc):
        i = pl.program_id(0)

        @pl.when(i == 0)
        def _():
            acc[...] = jnp.zeros_like(acc)

        e = y_ref[...] - t_ref[...]
        dy = e * (1.0 / d)
        dy_ref[...] = dy
        dy16_ref[...] = dy.astype(BF16)
        acc[...] += jnp.sum(e * e, axis=0, keepdims=True)

        @pl.when(i == nt - 1)
        def _():
            loss_ref[...] = (0.5 / d) * jnp.sum(acc[...], axis=-1, keepdims=True)

    spec = pl.BlockSpec((tr, d), lambda i: (i, 0))
    return pl.pallas_call(
        body, name=name, out_shape=(jax.ShapeDtypeStruct((t, d), F32), jax.ShapeDtypeStruct((t, d), BF16),
                                    jax.ShapeDtypeStruct((1, 1), F32)),
        grid=(nt,), in_specs=[spec, spec], out_specs=(spec, spec, pl.BlockSpec((1, 1), lambda i: (0, 0))),
        scratch_shapes=[pltpu.VMEM((1, d), F32)], compiler_params=_params(),
    )(y, target)


def _adamw(w, g, m, v, name):
    r, c = w.shape
    tr, tc = _tile2d(r, c, 512 * 1024)
    c1 = 1.0 - ADAM_B1 ** ADAM_STEP
    c2 = 1.0 - ADAM_B2 ** ADAM_STEP

    def body(w_ref, g_ref, m_ref, v_ref, d_ref, nm_ref, nv_ref):
        gv = g_ref[...]
        mn = ADAM_B1 * m_ref[...] + (1.0 - ADAM_B1) * gv
        vn = ADAM_B2 * v_ref[...] + (1.0 - ADAM_B2) * (gv * gv)
        d_ref[...] = -ADAM_LR * ((mn / c1) / (jnp.sqrt(vn / c2) + ADAM_EPS) + ADAM_WD * w_ref[...])
        nm_ref[...] = mn
        nv_ref[...] = vn

    spec = pl.BlockSpec((tr, tc), lambda i, j: (i, j))
    shp = jax.ShapeDtypeStruct((r, c), F32)
    return pl.pallas_call(body, name=name, out_shape=(shp, shp, shp), grid=(r // tr, c // tc), in_specs=[spec] * 4,
                          out_specs=(spec, spec, spec), compiler_params=_params())(w, g, m, v)


def _swap_halves(v, lane, half):
    n = v.shape[-1]
    return jnp.where(lane < half, pltpu.roll(v, n - half, 1),
                     jnp.where(lane < 2 * half, pltpu.roll(v, half, 1), 0.0))


def _norm_fwd(xraw, g):
    rstd = lax.rsqrt(jnp.mean(xraw * xraw, axis=-1, keepdims=True) + EPS)
    xhat = xraw * rstd
    return xhat, rstd, xhat * g


def _norm_bwd(dy, xhat, rstd, g):
    dxhat = dy * g
    dx = rstd * (dxhat - xhat * jnp.mean(dxhat * xhat, axis=-1, keepdims=True))
    return dx, jnp.sum(dy * xhat, axis=0, keepdims=True)


def _attn_specs(nb, grp, hd, kv, clamp):
    qo, ko, vo = 0, (kv * grp), (kv * grp + kv)
    cur = (lambda i: jnp.minimum(i, nb - 1)) if clamp else (lambda i: i)
    prev = lambda i: jnp.maximum(cur(i) - 1, 0)
    zq = pl.BlockSpec((BLK, grp * hd), lambda h, i: (cur(i), h))
    kc = pl.BlockSpec((BLK, hd), lambda h, i: (cur(i), ko + h))
    kp = pl.BlockSpec((BLK, hd), lambda h, i: (prev(i), ko + h))
    vc = pl.BlockSpec((BLK, hd), lambda h, i: (cur(i), vo + h))
    vp = pl.BlockSpec((BLK, hd), lambda h, i: (prev(i), vo + h))
    tc = pl.BlockSpec((BLK, hd), lambda h, i: (cur(i), 0))
    tp = pl.BlockSpec((BLK, hd), lambda h, i: (prev(i), 0))
    gs = pl.BlockSpec((1, hd), lambda h, i: (0, 0))
    return zq, kc, kp, vc, vp, tc, tp, gs


def _attn_mask(i):
    qi = lax.broadcasted_iota(jnp.int32, (BLK, 2 * BLK), 0)
    kj = lax.broadcasted_iota(jnp.int32, (BLK, 2 * BLK), 1)
    rel = qi + BLK - kj
    return (rel >= 0) & (rel < WINDOW) & ((kj >= BLK) | (i > 0))


def _attn_fwd(z, cos_t, sin_t, qg, kg, sinks, kv, grp, hd, name, job=None):
    t = z.shape[0]
    nb = t // BLK
    half = hd // 8
    scale = 1.0 / math.sqrt(hd)
    zq, kc, kp, vc, vp, tc, tp, gs = _attn_specs(nb, grp, hd, kv, False)

    def body(sink_ref, zq_ref, kc_ref, kp_ref, vc_ref, vp_ref, cc_ref, sc_ref, cp_ref, sp_ref, qg_ref, kg_ref, o_ref):
        h, i = pl.program_id(0), pl.program_id(1)
        lane = lax.broadcasted_iota(jnp.int32, (BLK, hd), 1)

        def normrope(xraw, g, c, s):
            y = _norm_fwd(xraw, g)[2]
            return y * c + _swap_halves(y, lane, half) * s

        cc, sc = cc_ref[...], sc_ref[...]
        kcur = normrope(kc_ref[...], kg_ref[...], cc, sc)
        kprev = normrope(kp_ref[...], kg_ref[...], cp_ref[...], sp_ref[...])
        kk = jnp.concatenate([kprev, kcur], axis=0).astype(BF16)
        vv = jnp.concatenate([vp_ref[...], vc_ref[...]], axis=0).astype(BF16)
        mask = _attn_mask(i)
        for g in range(grp):
            q = normrope(zq_ref[:, g * hd:(g + 1) * hd], qg_ref[...], cc, sc).astype(BF16)
            s = lax.dot_general(q, kk, (((1,), (1,)), ((), ())), preferred_element_type=F32) * scale
            s = jnp.where(mask, s, NEG)
            sk = sink_ref[h * grp + g]
            mx = jnp.maximum(jnp.max(s, axis=-1, keepdims=True), sk)
            p = jnp.exp(s - mx)
            den = jnp.sum(p, axis=-1, keepdims=True) + jnp.exp(sk - mx)
            p = p / den
            o_ref[:, g * hd:(g + 1) * hd] = jnp.dot(p.astype(BF16), vv, preferred_element_type=F32).astype(BF16)

    res, extra = _call(
        body, name, jax.ShapeDtypeStruct((t, kv * grp * hd), BF16), (kv, nb),
        [pl.BlockSpec(memory_space=pltpu.SMEM), zq, kc, kp, vc, vp, tc, tc, tp, tp, gs, gs],
        pl.BlockSpec((BLK, grp * hd), lambda h, i: (i, h)),
        (sinks, z, z, z, z, z, cos_t, sin_t, cos_t, sin_t, qg, kg), (), job)
    return res if job is None else (res, extra)


def _attn_bwd(dattn, z, cos_t, sin_t, qg, kg, sinks, kv, grp, hd, name):
    t = z.shape[0]
    nb = t // BLK
    half = hd // 8
    scale = 1.0 / math.sqrt(hd)
    zq, kc, kp, vc, vp, tc, tp, gs = _attn_specs(nb, grp, hd, kv, True)

    def body(sink_ref, zq_ref, kc_ref, kp_ref, vc_ref, vp_ref, cc_ref, sc_ref, cp_ref, sp_ref, qg_ref, kg_ref, do_ref,
             dq_ref, dk_ref, dv_ref, dqg_ref, dkg_ref, dsk_ref, dk_carry, dv_carry):
        h, i = pl.program_id(0), pl.program_id(1)
        lane = lax.broadcasted_iota(jnp.int32, (BLK, hd), 1)
        lane1 = lax.broadcasted_iota(jnp.int32, (1, LANE), 1)

        @pl.when((h == 0) & (i == 0))
        def _():
            dqg_ref[...] = jnp.zeros_like(dqg_ref)
            dkg_ref[...] = jnp.zeros_like(dkg_ref)
            dsk_ref[...] = jnp.zeros_like(dsk_ref)

        @pl.when(i == 0)
        def _():
            dk_carry[...] = jnp.zeros_like(dk_carry)
            dv_carry[...] = jnp.zeros_like(dv_carry)

        def rope(y, c, s):
            return y * c + _swap_halves(y, lane, half) * s

        def rope_bwd(dout, c, s):
            return dout * c + _swap_halves(dout * s, lane, half)

        @pl.when(i < nb)
        def _():
            cc, sc, cp, sp = cc_ref[...], sc_ref[...], cp_ref[...], sp_ref[...]
            qgv, kgv = qg_ref[...], kg_ref[...]
            xh_kc, rs_kc, y_kc = _norm_fwd(kc_ref[...], kgv)
            xh_kp, rs_kp, y_kp = _norm_fwd(kp_ref[...], kgv)
            kk = jnp.concatenate([rope(y_kp, cp, sp), rope(y_kc, cc, sc)], axis=0).astype(BF16)
            vv = jnp.concatenate([vp_ref[...], vc_ref[...]], axis=0).astype(BF16)
            mask = _attn_mask(i)
            dkk = jnp.zeros((2 * BLK, hd), F32)
            dvv = jnp.zeros((2 * BLK, hd), F32)
            dqg = jnp.zeros((1, hd), F32)
            dsk = jnp.zeros((1, LANE), F32)
            for g in range(grp):
                xh_q, rs_q, y_q = _norm_fwd(zq_ref[:, g * hd:(g + 1) * hd], qgv)
                q = rope(y_q, cc, sc).astype(BF16)
                s = lax.dot_general(q, kk, (((1,), (1,)), ((), ())), preferred_element_type=F32) * scale
                s = jnp.where(mask, s, NEG)
                sk = sink_ref[h * grp + g]
                mx = jnp.maximum(jnp.max(s, axis=-1, keepdims=True), sk)
                p = jnp.exp(s - mx)
                den = jnp.sum(p, axis=-1, keepdims=True) + jnp.exp(sk - mx)
                p = p / den
                psink = jnp.exp(sk - mx) / den
                dog = do_ref[:, g * hd:(g + 1) * hd].astype(BF16)
                dp = lax.dot_general(dog, vv, (((1,), (1,)), ((), ())), preferred_element_type=F32)
                rsum = jnp.sum(p * dp, axis=-1, keepdims=True)
                ds = (p * (dp - rsum) * scale).astype(BF16)
                dsk = dsk + jnp.where(lane1 == h * grp + g, jnp.sum(-psink * rsum, axis=0, keepdims=True), 0.0)
                dqn = jnp.dot(ds, kk, preferred_element_type=F32)
                dkk = dkk + lax.dot_general(ds, q, (((0,), (0,)), ((), ())), preferred_element_type=F32)
                dvv = dvv + lax.dot_general(p.astype(BF16), dog, (((0,), (0,)), ((), ())), preferred_element_type=F32)
                dxq, dg_q = _norm_bwd(rope_bwd(dqn, cc, sc), xh_q, rs_q, qgv)
                dq_ref[:, g * hd:(g + 1) * hd] = dxq
                dqg = dqg + dg_q
            dkp_raw, dg_kp = _norm_bwd(rope_bwd(dkk[:BLK], cp, sp), xh_kp, rs_kp, kgv)
            dkc_raw, dg_kc = _norm_bwd(rope_bwd(dkk[BLK:], cc, sc), xh_kc, rs_kc, kgv)
            dk_ref[...] = dk_carry[...] + dkp_raw
            dv_ref[...] = dv_carry[...] + dvv[:BLK]
            dk_carry[...] = dkc_raw
            dv_carry[...] = dvv[BLK:]
            dqg_ref[...] += dqg
            dkg_ref[...] += dg_kp + dg_kc
            dsk_ref[...] += dsk

        @pl.when(i == nb)
        def _():
            dk_ref[...] = dk_carry[...]
            dv_ref[...] = dv_carry[...]

    kvw = kv * hd
    vec = pl.BlockSpec((1, hd), lambda h, i: (0, 0))
    shifted = pl.BlockSpec((BLK, hd), lambda h, i: (jnp.maximum(i - 1, 0), h))
    return pl.pallas_call(
        body, name=name,
        out_shape=(jax.ShapeDtypeStruct((t, kv * grp * hd), F32), jax.ShapeDtypeStruct((t, kvw), F32),
                   jax.ShapeDtypeStruct((t, kvw), F32), jax.ShapeDtypeStruct((1, hd), F32),
                   jax.ShapeDtypeStruct((1, hd), F32), jax.ShapeDtypeStruct((1, LANE), F32)),
        grid=(kv, nb + 1),
        in_specs=[pl.BlockSpec(memory_space=pltpu.SMEM), zq, kc, kp, vc, vp, tc, tc, tp, tp, gs, gs,
                  pl.BlockSpec((BLK, grp * hd), lambda h, i: (jnp.minimum(i, nb - 1), h))],
        out_specs=(pl.BlockSpec((BLK, grp * hd), lambda h, i: (jnp.minimum(i, nb - 1), h)), shifted, shifted, vec, vec,
                   pl.BlockSpec((1, LANE), lambda h, i: (0, 0))),
        scratch_shapes=[pltpu.VMEM((BLK, hd), F32), pltpu.VMEM((BLK, hd), F32)], compiler_params=_params(),
    )(sinks, z, z, z, z, z, cos_t, sin_t, cos_t, sin_t, qg, kg, dattn)


def _conv_fwd(u, w, b, name):
    t, c = u.shape
    taps = w.shape[0]
    cb = _pick(c, (1408, 1024, 512, 256, 128))
    tr = _row_tile(t, cb, 256 * 1024)
    hb = tr // SUBLANE

    def body(u_ref, halo_ref, w_ref, b_ref, o_ref):
        i = pl.program_id(0)
        x = u_ref[...]
        acc = b_ref[...] + w_ref[taps - 1:taps, :] * x
        for k in range(taps - 1):
            acc = acc + w_ref[k:k + 1, :] * pltpu.roll(x, taps - 1 - k, 0)
        o_ref[...] = acc
        row = lax.broadcasted_iota(jnp.int32, (SUBLANE, cb), 0)
        hp = jnp.where(i > 0, halo_ref[...], 0.0)
        x8 = u_ref[0:SUBLANE, :]
        acc8 = b_ref[...] + w_ref[taps - 1:taps, :] * x8
        for k in range(taps - 1):
            s = taps - 1 - k
            acc8 = acc8 + w_ref[k:k + 1, :] * jnp.where(row < s, pltpu.roll(hp, s, 0), pltpu.roll(x8, s, 0))
        o_ref[0:SUBLANE, :] = acc8

    blk = pl.BlockSpec((tr, cb), lambda i, j: (i, j))
    return pl.pallas_call(
        body, name=name, out_shape=jax.ShapeDtypeStruct((t, c), F32), grid=(t // tr, c // cb),
        in_specs=[blk, pl.BlockSpec((SUBLANE, cb), lambda i, j: (jnp.maximum(i * hb - 1, 0), j)),
                  pl.BlockSpec((taps, cb), lambda i, j: (0, j)), pl.BlockSpec((1, cb), lambda i, j: (0, j))],
        out_specs=blk, compiler_params=_params(),
    )(u, u, w, b)


def _conv_bwd(duc, u, w, name):
    t, c = u.shape
    taps = w.shape[0]
    cb = _pick(c, (1408, 1024, 512, 256, 128))
    tr = _row_tile(t, cb, 256 * 1024)
    hb, nt = tr // SUBLANE, t // tr

    def body(g_ref, gnext_ref, u_ref, uprev_ref, w_ref, du_ref, dw_ref, db_ref):
        i = pl.program_id(1)

        @pl.when(i == 0)
        def _():
            dw_ref[...] = jnp.zeros_like(dw_ref)
            db_ref[...] = jnp.zeros_like(db_ref)

        row = lax.broadcasted_iota(jnp.int32, (SUBLANE, cb), 0)
        g, x = g_ref[...], u_ref[...]
        du = w_ref[taps - 1:taps, :] * g
        for k in range(taps - 1):
            du = du + w_ref[k:k + 1, :] * pltpu.roll(g, tr - (taps - 1 - k), 0)
        du_ref[...] = du
        hn = jnp.where(i < nt - 1, gnext_ref[...], 0.0)
        g8 = g_ref[tr - SUBLANE:tr, :]
        du8 = w_ref[taps - 1:taps, :] * g8
        for k in range(taps - 1):
            s = taps - 1 - k
            du8 = du8 + w_ref[k:k + 1, :] * jnp.where(row >= SUBLANE - s, pltpu.roll(hn, SUBLANE - s, 0),
                                                     pltpu.roll(g8, SUBLANE - s, 0))
        du_ref[tr - SUBLANE:tr, :] = du8

        hp = jnp.where(i > 0, uprev_ref[...], 0.0)
        xl8, gf8 = u_ref[tr - SUBLANE:tr, :], g_ref[0:SUBLANE, :]
        db_ref[...] += jnp.sum(g, axis=0, keepdims=True)
        dw_ref[taps - 1:taps, :] += jnp.sum(g * x, axis=0, keepdims=True)
        for k in range(taps - 1):
            s = taps - 1 - k
            fix = jnp.where(row < s, pltpu.roll(hp, s, 0) - pltpu.roll(xl8, s, 0), 0.0)
            dw_ref[k:k + 1, :] += (jnp.sum(g * pltpu.roll(x, s, 0), axis=0, keepdims=True)
                                   + jnp.sum(gf8 * fix, axis=0, keepdims=True))

    blk = pl.BlockSpec((tr, cb), lambda j, i: (i, j))
    nh = t // SUBLANE
    return pl.pallas_call(
        body, name=name,
        out_shape=(jax.ShapeDtypeStruct((t, c), F32), jax.ShapeDtypeStruct((taps, c), F32),
                   jax.ShapeDtypeStruct((1, c), F32)),
        grid=(c // cb, nt),
        in_specs=[blk, pl.BlockSpec((SUBLANE, cb), lambda j, i: (jnp.minimum((i + 1) * hb, nh - 1), j)),
                  blk, pl.BlockSpec((SUBLANE, cb), lambda j, i: (jnp.maximum(i * hb - 1, 0), j)),
                  pl.BlockSpec((taps, cb), lambda j, i: (0, j))],
        out_specs=(blk, pl.BlockSpec((taps, cb), lambda j, i: (0, j)), pl.BlockSpec((1, cb), lambda j, i: (0, j))),
        compiler_params=_params(),
    )(duc, duc, u, u, w)


def _dense_groups(w, gw):
    n, bw, _ = w.shape
    per = gw // bw
    w4 = w.reshape(n // per, per, bw, bw).astype(BF16)
    eye = jnp.eye(per, dtype=BF16)
    return (w4[:, :, :, None, :] * eye[None, :, None, :, None]).reshape(n // per, gw, gw)


def _diag_blocks(dense, n, bw):
    ng, gw, _ = dense.shape
    per = gw // bw
    diag = jnp.diagonal(dense.reshape(ng, per, bw, per, bw), axis1=1, axis2=3)
    return jnp.moveaxis(diag, -1, 1).reshape(n, bw, bw)


def _gates_fwd(uc, wr, wi, name):
    t, c = uc.shape
    ng, gw, _ = wr.shape
    tr = _pick(t, (512, 256, 128))

    def body(u_ref, wr_ref, wi_ref, r_ref, i_ref):
        a = u_ref[...].astype(BF16)
        r_ref[...] = jnp.dot(a, wr_ref[...], preferred_element_type=F32)
        i_ref[...] = jnp.dot(a, wi_ref[...], preferred_element_type=F32)

    blk = pl.BlockSpec((tr, gw), lambda h, i: (i, h))
    wsp = pl.BlockSpec((None, gw, gw), lambda h, i: (h, 0, 0))
    shp = jax.ShapeDtypeStruct((t, c), F32)
    return pl.pallas_call(body, name=name, out_shape=(shp, shp), grid=(ng, t // tr), in_specs=[blk, wsp, wsp],
                          out_specs=(blk, blk), compiler_params=_params())(uc, wr, wi)


def _gates_bwd_x(duc, drp, dip, wr, wi, name):
    t, c = duc.shape
    ng, gw, _ = wr.shape
    tr = _pick(t, (512, 256, 128))
    dims = (((1,), (1,)), ((), ()))

    def body(d_ref, r_ref, i_ref, wr_ref, wi_ref, o_ref):
        o_ref[...] = (d_ref[...]
                      + lax.dot_general(r_ref[...].astype(BF16), wr_ref[...], dims, preferred_element_type=F32)
                      + lax.dot_general(i_ref[...].astype(BF16), wi_ref[...], dims, preferred_element_type=F32))

    blk = pl.BlockSpec((tr, gw), lambda h, i: (i, h))
    wsp = pl.BlockSpec((None, gw, gw), lambda h, i: (h, 0, 0))
    return pl.pallas_call(body, name=name, out_shape=jax.ShapeDtypeStruct((t, c), F32), grid=(ng, t // tr),
                          in_specs=[blk, blk, blk, wsp, wsp], out_specs=blk, compiler_params=_params())(duc, drp, dip, wr, wi)


def _gates_bwd_w(uc, dpre, ng, gw, name):
    t, c = uc.shape
    tk = _pick(t, (512, 256, 128))
    dims = (((0,), (0,)), ((), ()))

    def body(u_ref, d_ref, o_ref):
        @pl.when(pl.program_id(1) == 0)
        def _():
            o_ref[...] = jnp.zeros_like(o_ref)

        o_ref[...] += lax.dot_general(u_ref[...].astype(BF16), d_ref[...].astype(BF16), dims, preferred_element_type=F32)

    blk = pl.BlockSpec((tk, gw), lambda h, i: (i, h))
    return pl.pallas_call(body, name=name, out_shape=jax.ShapeDtypeStruct((ng, gw, gw), F32), grid=(ng, t // tk),
                          in_specs=[blk, blk], out_specs=pl.BlockSpec((None, gw, gw), lambda h, i: (h, 0, 0)),
                          compiler_params=_params())(uc, dpre)


def _softplus(x):
    return jnp.maximum(x, 0.0) + jnp.log(1.0 + jnp.exp(-jnp.abs(x)))


def _neg_expm1(x):
    series = x * (1.0 + x * (0.5 + x * (1.0 / 6.0 + x * (1.0 / 24.0 + x * (1.0 / 120.0)))))
    return -jnp.where(x > -0.05, series, jnp.exp(x) - 1.0)


_GELU_C = math.sqrt(2.0 / math.pi)


def _gelu_parts(x):
    inner = _GELU_C * (x + 0.044715 * (x * x * x))
    th = jnp.tanh(inner)
    gelu = 0.5 * x * (1.0 + th)
    dgelu = 0.5 * (1.0 + th) + 0.5 * x * (1.0 - th * th) * (_GELU_C * (1.0 + 3.0 * 0.044715 * (x * x)))
    return gelu, dgelu


def _lru_gate_values(uc, rpre, ipre, br, bi, sp):
    r = _sigmoid(rpre + br)
    ig = _sigmoid(ipre + bi)
    a = jnp.exp(-LRU_C * r * sp)
    mult = jnp.sqrt(jnp.maximum(_neg_expm1(2.0 * (-LRU_C * r * sp)), 0.0))
    return r, ig, a, mult


def _lru_fwd(uc, rpre, ipre, gr, br, bi, lam, name, job=None):
    t, c = uc.shape
    cb = _pick(c, (1408, 1024, 512, 256, 128))
    tb = _pick(t, (512, 256, 128))
    ntile = tb // SUBLANE

    def body(uc_ref, r_ref, i_ref, gr_ref, br_ref, bi_ref, lam_ref, h_ref, rec16_ref, carry, rec_ref):
        @pl.when(pl.program_id(1) == 0)
        def _():
            carry[...] = jnp.zeros_like(carry)

        sp = _softplus(-lam_ref[...])
        br, bi = br_ref[...], bi_ref[...]
        row = lax.broadcasted_iota(jnp.int32, (SUBLANE, cb), 0)

        def tile(k, c_in):
            sl = pl.ds(pl.multiple_of(k * SUBLANE, SUBLANE), SUBLANE)
            ucv = uc_ref[sl, :]
            _, ig, a, mult = _lru_gate_values(ucv, r_ref[sl, :], i_ref[sl, :], br, bi, sp)
            b = mult * (ig * ucv)
            for d in (1, 2, 4):
                a_s = jnp.where(row >= d, pltpu.roll(a, d, 0), 1.0)
                b_s = jnp.where(row >= d, pltpu.roll(b, d, 0), 0.0)
                b = a * b_s + b
                a = a * a_s
            hv = b + a * c_in
            h_ref[sl, :] = hv
            rec_ref[sl, :] = hv * _gelu_parts(gr_ref[sl, :])[0]
            return hv[SUBLANE - 1:SUBLANE, :]

        c_out = lax.fori_loop(0, ntile, tile, carry[0:1, :])
        carry[...] = jnp.broadcast_to(c_out, (SUBLANE, cb))
        rec16_ref[...] = rec_ref[...].astype(BF16)

    blk = pl.BlockSpec((tb, cb), lambda j, i: (i, j))
    vec = pl.BlockSpec((1, cb), lambda j, i: (0, j))
    res, extra = _call(body, name, (jax.ShapeDtypeStruct((t, c), F32), jax.ShapeDtypeStruct((t, c), BF16)),
                       (c // cb, t // tb), [blk, blk, blk, blk, vec, vec, vec], (blk, blk),
                       (uc, rpre, ipre, gr, br, bi, lam), [pltpu.VMEM((SUBLANE, cb), F32), pltpu.VMEM((tb, cb), F32)], job)
    return res if job is None else (res, extra)


def _lru_bwd(drec, hst, uc, rpre, ipre, gr, br, bi, lam, name, job=None):
    t, c = uc.shape
    cb = _pick(c, (1408, 1024, 512, 256, 128))
    tb = _pick(t, (256, 128))
    ntile, nt, hb = tb // SUBLANE, t // tb, tb // SUBLANE

    def body(drec_ref, h_ref, hprev_ref, uc_ref, r_ref, i_ref, gr_ref, br_ref, bi_ref, lam_ref,
             dgr_ref, drp_ref, dip_ref, duc_ref, dlam_ref, dbr_ref, dbi_ref, carry):
        step = pl.program_id(1)
        first_block = step == nt - 1

        @pl.when(step == 0)
        def _():
            carry[...] = jnp.zeros_like(carry)
            dlam_ref[...] = jnp.zeros_like(dlam_ref)
            dbr_ref[...] = jnp.zeros_like(dbr_ref)
            dbi_ref[...] = jnp.zeros_like(dbi_ref)

        lam = lam_ref[...]
        sp = _softplus(-lam)
        br, bi = br_ref[...], bi_ref[...]
        row = lax.broadcasted_iota(jnp.int32, (SUBLANE, cb), 0)
        halo = jnp.where(first_block, 0.0, hprev_ref[...])

        def tile(kk, state):
            c_p, acc_sp, acc_br, acc_bi = state
            k = ntile - 1 - kk
            sl = pl.ds(pl.multiple_of(k * SUBLANE, SUBLANE), SUBLANE)
            slp = pl.ds(pl.multiple_of(jnp.maximum(k - 1, 0) * SUBLANE, SUBLANE), SUBLANE)
            ucv = uc_ref[sl, :]
            r, ig, a, mult = _lru_gate_values(ucv, r_ref[sl, :], i_ref[sl, :], br, bi, sp)
            hv = h_ref[sl, :]
            below = jnp.where(k > 0, h_ref[slp, :], halo)
            hprev = jnp.where(row == 0, pltpu.roll(below, 1, 0), pltpu.roll(hv, 1, 0))
            gelu, dgelu = _gelu_parts(gr_ref[sl, :])
            drec = drec_ref[sl, :]
            dh = drec * gelu
            dgr_ref[sl, :] = drec * hv * dgelu
            pa, pb = a, a * dh
            for d in (1, 2, 4):
                a_s = jnp.where(row < SUBLANE - d, pltpu.roll(pa, SUBLANE - d, 0), 1.0)
                b_s = jnp.where(row < SUBLANE - d, pltpu.roll(pb, SUBLANE - d, 0), 0.0)
                pb = pa * b_s + pb
                pa = pa * a_s
            pv = pb + pa * c_p
            gt = dh + jnp.where(row == SUBLANE - 1, c_p, pltpu.roll(pv, SUBLANE - 1, 0))
            da = gt * hprev
            duc_ref[sl, :] = gt * mult * ig
            dmult = gt * ig * ucv
            dig = gt * mult * ucv
            dla = da * a - jnp.where(mult > 0.0, dmult * (a * a) / mult, 0.0)
            drp = dla * (-LRU_C * sp) * (r * (1.0 - r))
            dip = dig * (ig * (1.0 - ig))
            drp_ref[sl, :] = drp
            dip_ref[sl, :] = dip
            return pv[0:1, :], acc_sp + dla * (-LRU_C * r), acc_br + drp, acc_bi + dip

        zero = jnp.zeros((SUBLANE, cb), F32)
        c_out, acc_sp, acc_br, acc_bi = lax.fori_loop(0, ntile, tile, (carry[0:1, :], zero, zero, zero))
        carry[...] = jnp.broadcast_to(c_out, (SUBLANE, cb))
        dlam_ref[...] += jnp.sum(acc_sp, axis=0, keepdims=True) * (-_sigmoid(-lam))
        dbr_ref[...] += jnp.sum(acc_br, axis=0, keepdims=True)
        dbi_ref[...] += jnp.sum(acc_bi, axis=0, keepdims=True)

    blk = pl.BlockSpec((tb, cb), lambda j, i: (nt - 1 - i, j))
    vec = pl.BlockSpec((1, cb), lambda j, i: (0, j))
    halo_spec = pl.BlockSpec((SUBLANE, cb), lambda j, i: (jnp.maximum((nt - 1 - i) * hb - 1, 0), j))
    big, small = jax.ShapeDtypeStruct((t, c), F32), jax.ShapeDtypeStruct((1, c), F32)
    res, extra = _call(
        body, name, (big, big, big, big, small, small, small), (c // cb, nt),
        [blk, blk, halo_spec, blk, blk, blk, blk, vec, vec, vec], (blk, blk, blk, blk, vec, vec, vec),
        (drec, hst, hst, uc, rpre, ipre, gr, br, bi, lam), [pltpu.VMEM((SUBLANE, cb), F32)], job)
    return res if job is None else (res, extra)


def _shard_region(ref, kind, chip, half, rh, width):
    if kind == "col":
        return ref.at[pl.ds(half * rh, rh), pl.ds(chip * width, width)]
    return ref.at[pl.ds(chip * (2 * rh) + half * rh, rh), :]


class _AllGather(_Exchange):
    def __init__(self, fulls, kinds):
        self.inputs, self.kinds = list(fulls), kinds
        self.out_shapes = [jax.ShapeDtypeStruct(f.shape, f.dtype) for f in fulls]
        self.aliases = {a: a for a in range(len(fulls))}
        self.n_sems = 6 * len(fulls)
        self.geo = [(f.shape[0] // 2, f.shape[1] // N_CHIPS) if k == "col" else (f.shape[0] // (2 * N_CHIPS), f.shape[1])
                    for f, k in zip(fulls, kinds)]

    def _region(self, ref, a, chip, half):
        return _shard_region(ref, self.kinds[a], chip, half, *self.geo[a])

    def _ici(self, e, a, k, chip):
        cx, cy = e.chips[k]
        return e.copy(self._region(e.ins[a], a, chip, e.c), self._region(e.outs[a], a, chip, e.c), a * 6 + k,
                      (cx, cy, e.c))

    def _d2d(self, e, a, k, half):
        cx, cy = e.chips[k]
        region = self._region(e.outs[a], a, 2 * cx + cy, half)
        return e.copy(region, region, a * 6 + 3 + k, e.sibling)

    def start(self, e):
        for a in range(len(self.inputs)):
            for k in range(3):
                self._ici(e, a, k, e.me).start()

    def finish(self, e):
        n = len(self.inputs)
        for a in range(n):
            for k, (cx, cy) in enumerate(e.chips):
                self._ici(e, a, k, 2 * cx + cy).wait_recv()
                self._d2d(e, a, k, e.c).start()
        for a in range(n):
            for k in range(3):
                self._d2d(e, a, k, 1 - e.c).wait_recv()
        for a in range(n):
            for k in range(3):
                self._ici(e, a, k, e.me).wait_send()
                self._d2d(e, a, k, e.c).wait_send()


class _SiblingExchange(_Exchange):
    def __init__(self, grads):
        self.inputs = list(grads)
        self.out_shapes = [jax.ShapeDtypeStruct((g.shape[0],) + g.shape[2:], g.dtype) for g in grads]
        self.n_sems = len(grads)

    def _copy(self, e, a):
        return e.copy(e.ins[a].at[:, 1 - e.c], e.outs[a], a, e.sibling)

    def start(self, e):
        for a in range(len(self.inputs)):
            self._copy(e, a).start()

    def finish(self, e):
        for a in range(len(self.inputs)):
            self._copy(e, a).wait()


def _piece(ref, kind, chip, width):
    if kind == "col":
        return ref.at[0, :, pl.ds(chip * width, width)]
    return ref.at[chip]


class _ChipExchange(_Exchange):
    def __init__(self, sums, kinds):
        self.inputs, self.kinds = list(sums), kinds
        self.widths = [s.shape[2] // N_CHIPS if k == "col" else s.shape[2] for s, k in zip(sums, kinds)]
        self.out_shapes = [jax.ShapeDtypeStruct((3, s.shape[1], w), s.dtype) for s, w in zip(sums, self.widths)]
        self.n_sems = 3 * len(sums)

    def _copy(self, e, a, k, chip):
        cx, cy = e.chips[k]
        return e.copy(_piece(e.ins[a], self.kinds[a], chip, self.widths[a]), e.outs[a].at[k], a * 3 + k, (cx, cy, e.c))

    def start(self, e):
        for a in range(len(self.inputs)):
            for k, (cx, cy) in enumerate(e.chips):
                self._copy(e, a, k, 2 * cx + cy).start()

    def finish(self, e):
        for a in range(len(self.inputs)):
            for k, (cx, cy) in enumerate(e.chips):
                self._copy(e, a, k, 2 * cx + cy).wait()


class _FinishExchange(_Exchange):
    def __init__(self, finals, to_all):
        self.inputs, self.to_all = list(finals), list(to_all)
        self.out_shapes = [jax.ShapeDtypeStruct(f.shape, f.dtype) for f in finals]
        self.aliases = {a: a for a in range(len(finals))}
        self.first_sem, self.n_sems = [], 0
        for all8 in self.to_all:
            self.first_sem.append(self.n_sems)
            self.n_sems += 7 if all8 else 1
        self.rel = [(fx, fy, fc) for fx in (0, 1) for fy in (0, 1) for fc in (0, 1)][1:]

    def _copies(self, e, mine):
        for a, all8 in enumerate(self.to_all):
            src = e.ins[a] if mine else e.outs[a]
            if not all8:
                rh = self.inputs[a].shape[0] // 2
                rows = pl.ds((e.c if mine else 1 - e.c) * rh, rh)
                yield e.copy(src.at[rows, :], e.outs[a].at[rows, :], self.first_sem[a], e.sibling)
                continue
            rh = self.inputs[a].shape[0] // (2 * N_CHIPS)
            for r, (fx, fy, fc) in enumerate(self.rel):
                px, py, pc = (1 - e.x if fx else e.x), (1 - e.y if fy else e.y), (1 - e.c if fc else e.c)
                rows = pl.ds(((2 * e.me + e.c) if mine else (2 * (2 * px + py) + pc)) * rh, rh)
                yield e.copy(src.at[rows, :], e.outs[a].at[rows, :], self.first_sem[a] + r, (px, py, pc))

    def start(self, e):
        for cp in self._copies(e, True):
            cp.start()

    def finish(self, e):
        for cp in self._copies(e, False):
            cp.wait_recv()
        for cp in self._copies(e, True):
            cp.wait_send()


def _cast_into_full(w, kind, idx, name):
    r, c = w.shape
    tr = _row_tile(r, c)
    nrb = r // tr

    def body(idx_ref, w_ref, o_ref):
        o_ref[...] = w_ref[...].astype(BF16)

    if kind == "col":
        full, out_map = (r, N_CHIPS * c), (lambda i, idx_ref: (i, idx_ref[1]))
    else:
        full, out_map = (N_CHIPS * r, c), (lambda i, idx_ref: (idx_ref[1] * nrb + i, 0))
    return pl.pallas_call(
        body, name=name, out_shape=jax.ShapeDtypeStruct(full, BF16),
        grid_spec=pltpu.PrefetchScalarGridSpec(
            num_scalar_prefetch=1, grid=(nrb,), in_specs=[pl.BlockSpec((tr, c), lambda i, idx_ref: (i, 0))],
            out_specs=pl.BlockSpec((tr, c), out_map)),
        compiler_params=_params(),
    )(idx, w)


def _add_own_half(g4, recv, idx, out_dtype, name):
    p, _, rh, n = g4.shape
    tr, tc = _tile2d(rh, n, 1024 * 1024)

    def body(idx_ref, g_ref, r_ref, o_ref):
        o_ref[...] = (g_ref[...] + r_ref[...]).astype(out_dtype)

    return pl.pallas_call(
        body, name=name, out_shape=jax.ShapeDtypeStruct((p, rh, n), out_dtype),
        grid_spec=pltpu.PrefetchScalarGridSpec(
            num_scalar_prefetch=1, grid=(p, rh // tr, n // tc),
            in_specs=[pl.BlockSpec((None, None, tr, tc), lambda q, i, j, idx_ref: (q, idx_ref[0], i, j)),
                      pl.BlockSpec((None, tr, tc), lambda q, i, j, idx_ref: (q, i, j))],
            out_specs=pl.BlockSpec((None, tr, tc), lambda q, i, j, idx_ref: (q, i, j))),
        compiler_params=_params(),
    )(idx, g4, recv)


def _sum_chips(own, kind, parts, idx, slots, to_all, name):
    _, rh, w = parts.shape
    tr, tc = _tile2d(rh, w, 512 * 1024)
    nrb, ncb = rh // tr, w // tc

    def body(idx_ref, own_ref, p0, p1, p2, o_ref):
        o_ref[...] = ((own_ref[...].astype(F32) + p0[...].astype(F32)) + p1[...].astype(F32)) + p2[...].astype(F32)

    if kind == "col":
        own_spec = pl.BlockSpec((None, tr, tc), lambda i, j, idx_ref: (0, i, idx_ref[1] * ncb + j))
    else:
        own_spec = pl.BlockSpec((None, tr, tc), lambda i, j, idx_ref: (idx_ref[1], i, j))
    if to_all:
        out_map = lambda i, j, idx_ref: ((2 * idx_ref[1] + idx_ref[0]) * nrb + i, j)
    else:
        out_map = lambda i, j, idx_ref: (idx_ref[0] * nrb + i, j)

    def part(k):
        return pl.BlockSpec((None, tr, tc), lambda i, j, idx_ref: (k, i, j))

    return pl.pallas_call(
        body, name=name, out_shape=jax.ShapeDtypeStruct((slots * rh, w), F32),
        grid_spec=pltpu.PrefetchScalarGridSpec(
            num_scalar_prefetch=1, grid=(nrb, ncb), in_specs=[own_spec, part(0), part(1), part(2)],
            out_specs=pl.BlockSpec((tr, tc), out_map)),
        compiler_params=_params(),
    )(idx, own, parts, parts, parts)


class _Reduce:
    def __init__(self, name, g, kind, idx, wire, to_all):
        r, c = g.shape
        self.name, self.kind, self.idx, self.wire, self.to_all = name, kind, idx, wire, to_all
        self.view = g.reshape(1, 2, r // 2, c) if kind == "col" else g.reshape(N_CHIPS, 2, r // (2 * N_CHIPS), c)

    def sibling(self):
        return _SiblingExchange([self.view])

    def got_sibling(self, outs):
        self.sum = _add_own_half(self.view, outs[0], self.idx, self.wire, "grad_chip_sum_" + self.name)

    def chips(self):
        return _ChipExchange([self.sum], [self.kind])

    def got_chips(self, outs):
        self.total = _sum_chips(self.sum, self.kind, outs[0], self.idx, 2 * N_CHIPS if self.to_all else 2,
                                self.to_all, "grad_total_" + self.name)


def _pack(arrays, rows):
    flat = jnp.concatenate([a.reshape(-1) for a in arrays])
    return jnp.pad(flat, (0, rows * SMALL_PACK_COLS - flat.shape[0])).reshape(rows, SMALL_PACK_COLS)


def _unpack(packed, shapes):
    flat = packed.reshape(-1)
    out, o = [], 0
    for shp in shapes:
        size = math.prod(shp)
        out.append(flat[o:o + size].reshape(shp))
        o += size
    return out


def _pack_rows(shapes):
    total = sum(math.prod(s) for s in shapes)
    unit = SMALL_PACK_COLS * N_CHIPS * 2 * SUBLANE
    return -(-total // unit) * (N_CHIPS * 2 * SUBLANE)


BIG = ("w_in", "w_attn_proj", "w_lru_proj", "w_out", "w_ffn_gate", "w_ffn_up", "w_ffn_down")
BIG_KIND = {"w_in": "col", "w_attn_proj": "row", "w_lru_proj": "row", "w_out": "row", "w_ffn_gate": "col",
            "w_ffn_up": "col", "w_ffn_down": "row"}
SMALL = ("norm1_g", "b_gates", "q_norm_g", "k_norm_g", "sinks", "conv_w", "conv_b", "w_rgate", "b_rgate",
         "w_igate", "b_igate", "lru_lambda", "norm2_g")
PACKED = tuple(n for n in SMALL if n not in ("w_rgate", "w_igate"))
WEIGHTS = ("norm1_g", "w_in", "b_gates", "q_norm_g", "k_norm_g", "sinks", "conv_w", "conv_b", "w_rgate", "b_rgate",
           "w_igate", "b_igate", "lru_lambda", "w_attn_proj", "w_lru_proj", "w_out", "norm2_g", "w_ffn_gate",
           "w_ffn_up", "w_ffn_down")


def kernel(x, positions, norm1_g, w_in, b_gates, q_norm_g, k_norm_g, sinks, conv_w, conv_b, w_rgate, b_rgate, w_igate, b_igate, lru_lambda, w_attn_proj, w_lru_proj, w_out, norm2_g, w_ffn_gate, w_ffn_up, w_ffn_down, loss_target, m_norm1_g, m_w_in, m_b_gates, m_q_norm_g, m_k_norm_g, m_sinks, m_conv_w, m_conv_b, m_w_rgate, m_b_rgate, m_w_igate, m_b_igate, m_lru_lambda, m_w_attn_proj, m_w_lru_proj, m_w_out, m_norm2_g, m_w_ffn_gate, m_w_ffn_up, m_w_ffn_down, v_norm1_g, v_w_in, v_b_gates, v_q_norm_g, v_k_norm_g, v_sinks, v_conv_w, v_conv_b, v_w_rgate, v_b_rgate, v_w_igate, v_b_igate, v_lru_lambda, v_w_attn_proj, v_w_lru_proj, v_w_out, v_norm2_g, v_w_ffn_gate, v_w_ffn_up, v_w_ffn_down):
    args = dict(locals())
    w = {n: args[n] for n in WEIGHTS}
    mom = {n: args["m_" + n] for n in WEIGHTS}
    var = {n: args["v_" + n] for n in WEIGHTS}

    t, d = x.shape[1], x.shape[2]
    hd = q_norm_g.shape[-1]
    nq = sinks.shape[-1]
    q_w = nq * hd
    d_rnn = conv_b.shape[-1]
    taps = conv_w.shape[1]
    n_blocks, bw = w_rgate.shape[1], w_rgate.shape[2]
    in_w = w_in.shape[-1] * N_CHIPS
    kv_w = (in_w - q_w - 2 * d_rnn - 2 * d) // 2
    kv = kv_w // hd
    grp = nq // kv
    u_off = q_w + 2 * kv_w
    gr_off = u_off + d_rnn
    ga_off = gr_off + d_rnn
    gw = bw * LANE // math.gcd(bw, LANE)
    ng = d_rnn // gw
    chip = 2 * lax.axis_index("x") + lax.axis_index("y")
    idx = jnp.stack([lax.axis_index("c"), chip]).astype(jnp.int32)

    x2, tgt = x[0], loss_target[0]

    placed = {n: _cast_into_full(w[n][0], BIG_KIND[n], idx, "cast_" + n) for n in BIG}

    def gather(*names):
        return _AllGather([placed[n] for n in names], [BIG_KIND[n] for n in names])

    (win_f,) = _run_exchange(gather("w_in"), "allgather_w_in")
    conv_w_full = _gather_small(conv_w[0], "allgather_conv_w")
    conv_w_full = jnp.transpose(conv_w_full, (1, 0, 2)).reshape(taps, d_rnn)
    wr_dense = _dense_groups(w_rgate[0], gw)
    wi_dense = _dense_groups(w_igate[0], gw)

    inv_freq = ROPE_THETA ** (-jnp.arange(0, hd // 4, 2, dtype=F32) / (hd // 4))
    ang = positions[0].astype(F32)[:, None] * inv_freq
    cos, sin = jnp.cos(ang), jnp.sin(ang)
    rest = hd - 2 * cos.shape[1]
    cos_t = jnp.concatenate([cos, cos, jnp.ones((t, rest), F32)], axis=1)
    sin_t = jnp.concatenate([-sin, sin, jnp.zeros((t, rest), F32)], axis=1)
    sinks1 = sinks[0]

    xn = _rms_fwd(x2, norm1_g, "rms1_fwd")
    z, (wap_f, wlp_f, wout_f, wg_f) = _matmul(xn, win_f, "nn", "in_proj",
                                             job=gather("w_attn_proj", "w_lru_proj", "w_out", "w_ffn_gate"))
    zu, zgr = z[:, u_off:u_off + d_rnn], z[:, gr_off:gr_off + d_rnn]
    attn, (wu_f,) = _attn_fwd(z, cos_t, sin_t, q_norm_g, k_norm_g, sinks1, kv, grp, hd, "attn_fwd",
                              job=gather("w_ffn_up"))
    uc = _conv_fwd(zu, conv_w_full, conv_b, "conv_fwd")
    rpre, ipre = _gates_fwd(uc, wr_dense, wi_dense, "gates_fwd")
    (hst, rec), (wd_f,) = _lru_fwd(uc, rpre, ipre, zgr, b_rgate, b_igate, lru_lambda, "lru_fwd",
                                   job=gather("w_ffn_down"))
    pa = _matmul(attn, wap_f, "nn", "attn_proj")
    plru = _matmul(rec, wlp_f, "nn", "lru_proj")
    merged = _merge_fwd(z, b_gates, pa, plru, ga_off, "merge_fwd")
    h1 = _matmul(merged, wout_f, "nn", "out_proj", add=x2)
    hn = _rms_fwd(h1, norm2_g, "rms2_fwd")
    gate = _matmul(hn, wg_f, "nn", "ffn_gate")
    up = _matmul(hn, wu_f, "nn", "ffn_up")
    act = _swiglu_fwd(gate, up, "swiglu_fwd")
    yout = _matmul(act, wd_f, "nn", "ffn_down", add=h1)
    dy, dy16, loss_part = _loss_head(yout, tgt, "loss_head")
    loss = lax.psum(loss_part[0, 0], ("x", "y", "c"))

    def reduction(n, g):
        return _Reduce(n, g, BIG_KIND[n], idx, BF16, False)

    r_wd = reduction("w_ffn_down", _matmul(act, dy16, "tn", "d_w_ffn_down"))
    dact, got = _matmul(dy16, wd_f, "nt", "d_act", job=r_wd.sibling())
    r_wd.got_sibling(got)
    (dgate, dup), got = _swiglu_bwd(dact, gate, up, "swiglu_bwd", job=r_wd.chips())
    r_wd.got_chips(got)
    r_wg = reduction("w_ffn_gate", _matmul(hn, dgate, "tn", "d_w_ffn_gate"))
    g_wu, got = _matmul(hn, dup, "tn", "d_w_ffn_up", job=r_wg.sibling())
    r_wg.got_sibling(got)
    r_wu = reduction("w_ffn_up", g_wu)
    both = _Jobs(r_wu.sibling(), r_wg.chips())
    dhn, got = _matmul(dgate, wg_f, "nt", "d_hn_gate", job=both)
    got_wu, got_wg = both.split(got)
    r_wu.got_sibling(got_wu)
    r_wg.got_chips(got_wg)
    dhn, got = _matmul(dup, wu_f, "nt", "d_hn_up", add=dhn, job=r_wu.chips())
    r_wu.got_chips(got)
    dh1, g_norm2, dh1_16 = _rms_bwd(dhn, h1, norm2_g, dy, "rms2_bwd", mxu_copy=True)
    r_wout = reduction("w_out", _matmul(merged, dh1_16, "tn", "d_w_out"))
    dmerged, got = _matmul(dh1_16, wout_f, "nt", "d_merged", job=r_wout.sibling())
    r_wout.got_sibling(got)
    (dpa, dpl, dga, dgl, g_ba, g_bl), got = _merge_bwd(dmerged, z, b_gates, pa, plru, ga_off, "merge_bwd",
                                                       job=r_wout.chips())
    r_wout.got_chips(got)
    r_wap = reduction("w_attn_proj", _matmul(attn, dpa, "tn", "d_w_attn_proj"))
    dattn, got = _matmul(dpa, wap_f, "nt", "d_attn", job=r_wap.sibling())
    r_wap.got_sibling(got)
    g_wlp, got = _matmul(rec, dpl, "tn", "d_w_lru_proj", job=r_wap.chips())
    r_wap.got_chips(got)
    r_wlp = reduction("w_lru_proj", g_wlp)
    drec, got = _matmul(dpl, wlp_f, "nt", "d_rec", job=r_wlp.sibling())
    r_wlp.got_sibling(got)
    (dgr, drp, dip, duc_direct, g_lam, g_br, g_bi), got = _lru_bwd(
        drec, hst, uc, rpre, ipre, zgr, b_rgate, b_igate, lru_lambda, "lru_bwd", job=r_wlp.chips())
    r_wlp.got_chips(got)
    duc = _gates_bwd_x(duc_direct, drp, dip, wr_dense, wi_dense, "gates_bwd_x")
    g_wr = _diag_blocks(_gates_bwd_w(uc, drp, ng, gw, "gates_bwd_wr"), n_blocks, bw)
    g_wi = _diag_blocks(_gates_bwd_w(uc, dip, ng, gw, "gates_bwd_wi"), n_blocks, bw)
    du, g_convw, g_convb = _conv_bwd(duc, zu, conv_w_full, "conv_bwd")
    dq, dk, dv, g_qg, g_kg, g_sinks = _attn_bwd(dattn, z, cos_t, sin_t, q_norm_g, k_norm_g, sinks1, kv, grp, hd,
                                                 "attn_bwd")
    dz = jnp.concatenate([dq, dk, dv, du, dgr, dga, dgl], axis=1).astype(BF16)
    r_wr = _Reduce("w_rgate", g_wr.reshape(n_blocks * bw, bw), "row", idx, F32, True)
    r_wi = _Reduce("w_igate", g_wi.reshape(n_blocks * bw, bw), "row", idx, F32, True)
    both = _Jobs(r_wr.sibling(), r_wi.sibling())
    g_top, got = _matmul(xn, dz, "tn", "d_w_in_top", m_window=(0, d // 2), job=both)
    got_wr, got_wi = both.split(got)
    r_wr.got_sibling(got_wr)
    r_wi.got_sibling(got_wi)
    r_top = _Reduce("w_in_top", g_top, "col", idx, BF16, False)
    three = _Jobs(r_top.sibling(), r_wr.chips(), r_wi.chips())
    g_bot, got = _matmul(xn, dz, "tn", "d_w_in_bot", m_window=(d // 2, d // 2), job=three)
    got_top, got_wr, got_wi = three.split(got)
    r_top.got_sibling(got_top)
    r_wr.got_chips(got_wr)
    r_wi.got_chips(got_wi)
    r_bot = _Reduce("w_in_bot", g_bot, "col", idx, BF16, False)
    both = _Jobs(r_top.chips(), r_bot.sibling())
    dxn, got = _matmul(dz, win_f, "nt", "d_xn_a", m_window=(0, t // 2), into=(None, t), job=both)
    got_top, got_bot = both.split(got)
    r_top.got_chips(got_top)
    r_bot.got_sibling(got_bot)
    dxn, got = _matmul(dz, win_f, "nt", "d_xn_b", m_window=(t // 2, t // 2), into=(dxn, t), job=r_bot.chips())
    r_bot.got_chips(got)
    dx, g_norm1 = _rms_bwd(dxn, x2, norm1_g, dh1, "rms1_bwd")

    small_grads = {"norm1_g": g_norm1, "b_gates": jnp.concatenate([g_ba, g_bl], axis=1), "q_norm_g": g_qg,
                   "k_norm_g": g_kg, "sinks": g_sinks[:, :nq], "conv_w": g_convw, "conv_b": g_convb,
                   "b_rgate": g_br, "b_igate": g_bi, "lru_lambda": g_lam, "norm2_g": g_norm2}
    gshapes = [small_grads[n].shape for n in PACKED]
    r_small = _Reduce("small", _pack([small_grads[n] for n in PACKED], _pack_rows(gshapes)), "row", idx, F32, True)
    r_small.got_sibling(_run_exchange(r_small.sibling(), "grad_sibling_exchange_small"))
    r_small.got_chips(_run_exchange(r_small.chips(), "grad_chip_exchange_small"))
    sharded = [r_top, r_bot, r_wap, r_wlp, r_wout, r_wg, r_wu, r_wd]
    everywhere = [r_wr, r_wi, r_small]
    reduced = _run_exchange(_FinishExchange([r.total for r in sharded + everywhere],
                                            [False] * len(sharded) + [True] * len(everywhere)), "grad_finish_exchange")
    grads = dict(zip(BIG[1:], reduced[2:len(sharded)]))
    grads["w_in"] = jnp.concatenate(reduced[:2], axis=0)
    grads["w_rgate"], grads["w_igate"] = reduced[len(sharded)], reduced[len(sharded) + 1]
    small_full = dict(zip(PACKED, _unpack(reduced[-1], gshapes)))
    per = d_rnn // N_CHIPS
    small_full["conv_w"] = lax.dynamic_slice(small_full["conv_w"], (0, chip * per), (taps, per))
    grads.update(small_full)

    delta, new_m, new_v = {}, {}, {}
    for n in BIG + ("w_rgate", "w_igate"):
        as2d = (lambda a: a[0]) if n in BIG else (lambda a: a.reshape(n_blocks * bw, bw))
        delta[n], new_m[n], new_v[n] = _adamw(as2d(w[n]), grads[n], as2d(mom[n]), as2d(var[n]), "adamw_" + n)
    pshapes = [w[n].shape for n in PACKED]
    prows = _pack_rows(pshapes)
    pk = [_pack([src[n] for n in PACKED], prows) for src in (w, grads, mom, var)]
    for res, packed in zip((delta, new_m, new_v), _adamw(pk[0], pk[1], pk[2], pk[3], "adamw_small")):
        res.update(dict(zip(PACKED, _unpack(packed, pshapes))))

    outs = [loss, dx.reshape(x.shape)]
    for res in (grads, delta, new_m, new_v):
        outs += [res[n].reshape(w[n].shape) for n in WEIGHTS]
    return tuple(outs)


def _gather_small(shard, name):
    def body(s_ref, o_ref, send_sems, recv_sems):
        e = _Env((s_ref,), (o_ref,), send_sems, recv_sems)
        o_ref[e.me] = s_ref[...]
        for k, (cx, cy) in enumerate(e.chips):
            e.copy(s_ref, o_ref.at[e.me], k, (cx, cy, e.c)).start()
        for k, (cx, cy) in enumerate(e.chips):
            e.copy(s_ref, o_ref.at[2 * cx + cy], k, (cx, cy, e.c)).wait_recv()
        for k, (cx, cy) in enumerate(e.chips):
            e.copy(s_ref, o_ref.at[e.me], k, (cx, cy, e.c)).wait_send()

    vm = pl.BlockSpec(memory_space=pltpu.VMEM)
    return pl.pallas_call(body, name=name, out_shape=jax.ShapeDtypeStruct((N_CHIPS,) + shard.shape, shard.dtype),
                          in_specs=[vm], out_specs=vm,
                          scratch_shapes=[pltpu.SemaphoreType.DMA((3,)), pltpu.SemaphoreType.DMA((3,))])(shard)
```

```python
import functools
import math

import jax
import jax.numpy as jnp
from jax import lax
from jax.experimental import pallas as pl
from jax.experimental.pallas import tpu as pltpu

F32 = jnp.float32
BF16 = jnp.bfloat16
MESH = pl.DeviceIdType.MESH

WINDOW = 128
BLK = 128
ROPE_THETA = 500000.0
LRU_C = 8.0
EPS = 1e-6
NEG = -1e30
ADAM_LR = 0.001
ADAM_B1 = 0.9
ADAM_B2 = 0.999
ADAM_EPS = 1e-08
ADAM_WD = 0.01
ADAM_STEP = 10

VMEM_LIMIT_BYTES = 52 * 1024 * 1024
LANE = 128
SUBLANE = 8
N_CHIPS = 4
SMALL_PACK_COLS = 512


def _params(**kw):
    return pltpu.CompilerParams(vmem_limit_bytes=VMEM_LIMIT_BYTES, **kw)


def _pick(dim, cands):
    for c in cands:
        if dim % c == 0:
            return c
    return dim


def _sigmoid(x):
    return 1.0 / (1.0 + jnp.exp(-x))


ANY = pl.BlockSpec(memory_space=pl.ANY)


class _Env:
    def __init__(self, ins, outs, send, recv, sem0=0, place=None):
        self.ins, self.outs, self.send, self.recv, self.sem0 = ins, outs, send, recv, sem0
        self.x, self.y, self.c = place or (lax.axis_index("x"), lax.axis_index("y"), lax.axis_index("c"))
        self.me = 2 * self.x + self.y
        self.chips = [(1 - self.x, self.y), (self.x, 1 - self.y), (1 - self.x, 1 - self.y)]
        self.sibling = (self.x, self.y, 1 - self.c)

    def sub(self, i0, n_in, o0, n_out, sem0):
        return _Env(self.ins[i0:i0 + n_in], self.outs[o0:o0 + n_out], self.send, self.recv, self.sem0 + sem0,
                    (self.x, self.y, self.c))

    def copy(self, src, dst, sem, to):
        return pltpu.make_async_remote_copy(src_ref=src, dst_ref=dst, send_sem=self.send.at[self.sem0 + sem],
                                            recv_sem=self.recv.at[self.sem0 + sem], device_id=to, device_id_type=MESH)


class _Exchange:
    inputs, out_shapes, aliases, n_sems = (), (), {}, 0

    def start(self, e):
        raise NotImplementedError

    def finish(self, e):
        raise NotImplementedError


class _Jobs(_Exchange):
    def __init__(self, *jobs):
        self.jobs, self.inputs, self.out_shapes, self.aliases, self.n_sems, self.at = jobs, [], [], {}, 0, []
        for job in jobs:
            self.at.append((len(self.inputs), len(self.out_shapes), self.n_sems))
            self.aliases.update({len(self.inputs) + i: len(self.out_shapes) + o for i, o in job.aliases.items()})
            self.inputs += list(job.inputs)
            self.out_shapes += list(job.out_shapes)
            self.n_sems += job.n_sems

    def _each(self, e):
        for job, (i0, o0, s0) in zip(self.jobs, self.at):
            yield job, e.sub(i0, len(job.inputs), o0, len(job.out_shapes), s0)

    def split(self, outs):
        return [tuple(outs[o0:o0 + len(job.out_shapes)]) for job, (_, o0, _) in zip(self.jobs, self.at)]

    def start(self, e):
        for job, se in self._each(e):
            job.start(se)

    def finish(self, e):
        for job, se in self._each(e):
            job.finish(se)


def _call(body, name, out_shape, grid, in_specs, out_specs, args, scratch_shapes=(), job=None, aliases=None):
    aliases = dict(aliases or {})
    if job is None:
        return pl.pallas_call(body, name=name, out_shape=out_shape, grid=grid, in_specs=list(in_specs),
                              out_specs=out_specs, scratch_shapes=list(scratch_shapes), input_output_aliases=aliases,
                              compiler_params=_params())(*args), ()
    single = not isinstance(out_shape, (tuple, list))
    shapes = [out_shape] if single else list(out_shape)
    ospecs = [out_specs] if single else list(out_specs)
    n_in, n_out, n_scr = len(args), len(shapes), len(scratch_shapes)
    j_in, j_out = len(job.inputs), len(job.out_shapes)

    def hosted(*refs):
        ins, jins = refs[:n_in], refs[n_in:n_in + j_in]
        outs = refs[n_in + j_in:n_in + j_in + n_out]
        jouts = refs[n_in + j_in + n_out:n_in + j_in + n_out + j_out]
        rest = refs[n_in + j_in + n_out + j_out:]
        e = _Env(jins, jouts, rest[n_scr], rest[n_scr + 1])
        first = functools.reduce(jnp.logical_and, [pl.program_id(d) == 0 for d in range(len(grid))])
        last = functools.reduce(jnp.logical_and, [pl.program_id(d) == g - 1 for d, g in enumerate(grid)])

        @pl.when(first)
        def _():
            job.start(e)

        body(*ins, *outs, *rest[:n_scr])

        @pl.when(last)
        def _():
            job.finish(e)

    res = pl.pallas_call(
        hosted, name=name, out_shape=tuple(shapes + list(job.out_shapes)), grid=grid,
        in_specs=list(in_specs) + [ANY] * j_in, out_specs=tuple(ospecs + [ANY] * j_out),
        scratch_shapes=list(scratch_shapes) + [pltpu.SemaphoreType.DMA((job.n_sems,)),
                                               pltpu.SemaphoreType.DMA((job.n_sems,))],
        input_output_aliases={**aliases, **{n_in + i: n_out + o for i, o in job.aliases.items()}},
        compiler_params=_params())(*args, *job.inputs)
    return (res[0] if single else tuple(res[:n_out])), tuple(res[n_out:])


def _run_exchange(job, name):
    n_in, n_out = len(job.inputs), len(job.out_shapes)

    def body(*refs):
        e = _Env(refs[:n_in], refs[n_in:n_in + n_out], refs[n_in + n_out], refs[n_in + n_out + 1])
        job.start(e)
        job.finish(e)

    return pl.pallas_call(
        body, name=name, out_shape=tuple(job.out_shapes), in_specs=[ANY] * n_in, out_specs=tuple([ANY] * n_out),
        input_output_aliases=dict(job.aliases),
        scratch_shapes=[pltpu.SemaphoreType.DMA((job.n_sems,)), pltpu.SemaphoreType.DMA((job.n_sems,))],
    )(*job.inputs)


_M_TILES = (1024, 1408, 1280, 512, 256, 128)
_N_TILES = (1408, 1280, 1024, 640, 512, 256, 128)
MXU_FULL_ROWS = 1024
MATMUL_VMEM_BUDGET = 42 * 1024 * 1024
MXU_FLOPS_PER_HBM_BYTE = 500


def _matmul_tiles(m, n, k, sa, sb, so, has_add):
    best = None
    for tm in [c for c in _M_TILES if m % c == 0] or [m]:
        for tn in [c for c in _N_TILES if n % c == 0] or [n]:
            for nk in range(1, 17):
                tk = k // nk
                if k % nk or tk % LANE:
                    continue
                need = 2 * (tm * tk * sa + tk * tn * sb) + 2 * tm * tn * (so + (4 if has_add else 0))
                need += tm * tn * 4 if nk > 1 else 0
                fetched = tk * tn * sb + tm * tk * sa // (1 if nk > 1 else n // tn)
                if need > MATMUL_VMEM_BUDGET:
                    continue
                mxu_bound = fetched * MXU_FLOPS_PER_HBM_BYTE <= 2 * tm * tn * tk
                key = (mxu_bound, min(tm, MXU_FULL_ROWS), -nk, tn, tm)
                if best is None or key > best[0]:
                    best = (key, (tm, tn, tk))
    assert best is not None, (m, n, k)
    return best[1]


def _matmul(a, b, mode, name, add=None, out_dtype=F32, job=None, m_window=None, into=None):
    if mode == "nn":
        (m, k), (k2, n) = a.shape, b.shape
    elif mode == "nt":
        (m, k), (n, k2) = a.shape, b.shape
    else:
        (k, m), (k2, n) = a.shape, b.shape
    assert k == k2, (a.shape, b.shape, mode)
    m0, m = m_window or (0, m)
    tm, tn, tk = _matmul_tiles(math.gcd(m, m0) if m0 else m, n, k, a.dtype.itemsize, b.dtype.itemsize,
                               jnp.dtype(out_dtype).itemsize, add is not None)
    nk, mb0 = k // tk, m0 // tm
    if mode == "nn":
        a_spec = pl.BlockSpec((tm, tk), lambda i, j, kk: (mb0 + i, kk))
        b_spec = pl.BlockSpec((tk, tn), lambda i, j, kk: (kk, j))
        dims = (((1,), (0,)), ((), ()))
    elif mode == "nt":
        a_spec = pl.BlockSpec((tm, tk), lambda i, j, kk: (mb0 + i, kk))
        b_spec = pl.BlockSpec((tn, tk), lambda i, j, kk: (j, kk))
        dims = (((1,), (1,)), ((), ()))
    else:
        a_spec = pl.BlockSpec((tk, tm), lambda i, j, kk: (kk, mb0 + i))
        b_spec = pl.BlockSpec((tk, tn), lambda i, j, kk: (kk, j))
        dims = (((0,), (0,)), ((), ()))
    out_rows, ob0 = (into[1], mb0) if into is not None else (m, 0)
    o_spec = pl.BlockSpec((tm, tn), lambda i, j, kk: (ob0 + i, j))
    has_add = add is not None
    begun = into is not None and into[0] is not None

    def body(*refs):
        a_ref, b_ref = refs[:2]
        add_ref = refs[2] if has_add else None
        part = lax.dot_general(a_ref[...].astype(BF16), b_ref[...].astype(BF16), dims, preferred_element_type=F32)
        if nk == 1:
            o_ref = refs[-1]
            o_ref[...] = (part + add_ref[...] if has_add else part).astype(out_dtype)
            return
        o_ref, acc = refs[-2:]
        kk = pl.program_id(2)

        @pl.when(kk == 0)
        def _():
            acc[...] = part

        @pl.when(kk > 0)
        def _():
            acc[...] += part

        @pl.when(kk == nk - 1)
        def _():
            r = acc[...]
            if has_add:
                r = r + add_ref[...]
            o_ref[...] = r.astype(out_dtype)

    in_specs = [a_spec, b_spec] + ([pl.BlockSpec((tm, tn), lambda i, j, kk: (mb0 + i, j))] if has_add else [])
    args = (a, b) + ((add,) if has_add else ())
    aliases = None
    if begun:
        aliases = {len(args): 0}
        in_specs, args = in_specs + [ANY], args + (into[0],)
    res, extra = _call(body, name, jax.ShapeDtypeStruct((out_rows, n), out_dtype), (m // tm, n // tn, nk), in_specs,
                       o_spec, args, [pltpu.VMEM((tm, tn), F32)] if nk > 1 else [], job, aliases)
    return res if job is None else (res, extra)


def _row_tile(rows, cols, budget_elems=512 * 1024):
    cands = [c for c in (1024, 704, 512, 352, 256, 128, 64, 32, 16) if c * cols <= budget_elems]
    return _pick(rows, cands or (16,))


_EW_COLS = (1280, 1408, 1024, 640, 512, 256, 128)


def _tile2d(rows, cols, max_elems):
    tc = _pick(cols, _EW_COLS)
    return _row_tile(rows, tc, max_elems), tc


def _rms_fwd(x, g, name):
    t, d = x.shape
    tr = _row_tile(t, d)

    def body(x_ref, g_ref, o_ref):
        xv = x_ref[...]
        rstd = lax.rsqrt(jnp.mean(xv * xv, axis=-1, keepdims=True) + EPS)
        o_ref[...] = (xv * rstd * g_ref[...]).astype(BF16)

    spec = pl.BlockSpec((tr, d), lambda i: (i, 0))
    return pl.pallas_call(body, name=name, out_shape=jax.ShapeDtypeStruct((t, d), BF16), grid=(t // tr,),
                          in_specs=[spec, pl.BlockSpec((1, d), lambda i: (0, 0))], out_specs=spec,
                          compiler_params=_params())(x, g)


def _rms_bwd(dxn, x, g, resid, name, job=None, mxu_copy=False):
    t, d = x.shape
    tr = _row_tile(t, d, 256 * 1024)

    def body(dxn_ref, x_ref, g_ref, r_ref, dx_ref, dg_ref, *dx16_ref):
        @pl.when(pl.program_id(0) == 0)
        def _():
            dg_ref[...] = jnp.zeros_like(dg_ref)

        xv = x_ref[...]
        rstd = lax.rsqrt(jnp.mean(xv * xv, axis=-1, keepdims=True) + EPS)
        xhat = xv * rstd
        dy = dxn_ref[...]
        dg_ref[...] += jnp.sum(dy * xhat, axis=0, keepdims=True)
        dxhat = dy * g_ref[...]
        dx = r_ref[...] + rstd * (dxhat - xhat * jnp.mean(dxhat * xhat, axis=-1, keepdims=True))
        dx_ref[...] = dx
        if mxu_copy:
            dx16_ref[0][...] = dx.astype(BF16)

    spec = pl.BlockSpec((tr, d), lambda i: (i, 0))
    vec = pl.BlockSpec((1, d), lambda i: (0, 0))
    shapes = (jax.ShapeDtypeStruct((t, d), F32), jax.ShapeDtypeStruct((1, d), F32))
    shapes += (jax.ShapeDtypeStruct((t, d), BF16),) if mxu_copy else ()
    res, extra = _call(body, name, shapes, (t // tr,), [spec, spec, vec, spec],
                       (spec, vec) + ((spec,) if mxu_copy else ()), (dxn, x, g, resid), (), job)
    return res if job is None else (res, extra)


def _swiglu_fwd(gate, up, name):
    t, f = gate.shape
    tr, tc = _tile2d(t, f, 1024 * 1024)

    def body(g_ref, u_ref, o_ref):
        gv = g_ref[...]
        o_ref[...] = (gv * _sigmoid(gv) * u_ref[...]).astype(BF16)

    spec = pl.BlockSpec((tr, tc), lambda i, j: (i, j))
    return pl.pallas_call(body, name=name, out_shape=jax.ShapeDtypeStruct((t, f), BF16), grid=(t // tr, f // tc),
                          in_specs=[spec, spec], out_specs=spec, compiler_params=_params())(gate, up)


def _swiglu_bwd(dact, gate, up, name, job=None):
    t, f = gate.shape
    tr, tc = _tile2d(t, f, 768 * 1024)

    def body(d_ref, g_ref, u_ref, dg_ref, du_ref):
        gv, dv = g_ref[...], d_ref[...]
        sg = _sigmoid(gv)
        dg_ref[...] = (dv * u_ref[...] * (sg * (1.0 + gv * (1.0 - sg)))).astype(BF16)
        du_ref[...] = (dv * (gv * sg)).astype(BF16)

    spec = pl.BlockSpec((tr, tc), lambda i, j: (i, j))
    shp = jax.ShapeDtypeStruct((t, f), BF16)
    res, extra = _call(body, name, (shp, shp), (t // tr, f // tc), [spec, spec, spec], (spec, spec), (dact, gate, up),
                       (), job)
    return res if job is None else (res, extra)


def _merge_fwd(z, b_gates, pa, plru, ga_off, name):
    t, d = pa.shape
    cw = _pick(math.gcd(ga_off, d), (512, 256, 128))
    tr = _row_tile(t, cw, 256 * 1024)
    oa, ol, nd = ga_off // cw, (ga_off + d) // cw, d // cw

    def body(ga_ref, gl_ref, ba_ref, bl_ref, pa_ref, pl_ref, o_ref):
        sa = _sigmoid(ga_ref[...] + ba_ref[...])
        sl = _sigmoid(gl_ref[...] + bl_ref[...])
        o_ref[...] = (sa * pa_ref[...] + sl * pl_ref[...]).astype(BF16)

    blk = pl.BlockSpec((tr, cw), lambda i, j: (i, j))
    return pl.pallas_call(
        body, name=name, out_shape=jax.ShapeDtypeStruct((t, d), BF16), grid=(t // tr, nd),
        in_specs=[pl.BlockSpec((tr, cw), lambda i, j: (i, oa + j)), pl.BlockSpec((tr, cw), lambda i, j: (i, ol + j)),
                  pl.BlockSpec((1, cw), lambda i, j: (0, j)), pl.BlockSpec((1, cw), lambda i, j: (0, nd + j)),
                  blk, blk],
        out_specs=blk, compiler_params=_params(),
    )(z, z, b_gates, b_gates, pa, plru)


def _merge_bwd(dmerged, z, b_gates, pa, plru, ga_off, name, job=None):
    t, d = pa.shape
    cw = _pick(math.gcd(ga_off, d), (512, 256, 128))
    tr = _row_tile(t, cw, 256 * 1024)
    oa, ol, nd = ga_off // cw, (ga_off + d) // cw, d // cw

    def body(dm_ref, ga_ref, gl_ref, ba_ref, bl_ref, pa_ref, pl_ref, dpa_ref, dpl_ref, dga_ref, dgl_ref, sa_ref, sl_ref):
        @pl.when(pl.program_id(1) == 0)
        def _():
            sa_ref[...] = jnp.zeros_like(sa_ref)
            sl_ref[...] = jnp.zeros_like(sl_ref)

        dm = dm_ref[...]
        sa = _sigmoid(ga_ref[...] + ba_ref[...])
        sl = _sigmoid(gl_ref[...] + bl_ref[...])
        dpa_ref[...] = (dm * sa).astype(BF16)
        dpl_ref[...] = (dm * sl).astype(BF16)
        dga = dm * pa_ref[...] * (sa * (1.0 - sa))
        dgl = dm * pl_ref[...] * (sl * (1.0 - sl))
        dga_ref[...] = dga
        dgl_ref[...] = dgl
        sa_ref[...] += jnp.sum(dga, axis=0, keepdims=True)
        sl_ref[...] += jnp.sum(dgl, axis=0, keepdims=True)

    blk = pl.BlockSpec((tr, cw), lambda j, i: (i, j))
    vec = pl.BlockSpec((1, cw), lambda j, i: (0, j))
    big16, big32, v32 = (jax.ShapeDtypeStruct((t, d), BF16), jax.ShapeDtypeStruct((t, d), F32),
                         jax.ShapeDtypeStruct((1, d), F32))
    res, extra = _call(
        body, name, (big16, big16, big32, big32, v32, v32), (nd, t // tr),
        [blk, pl.BlockSpec((tr, cw), lambda j, i: (i, oa + j)), pl.BlockSpec((tr, cw), lambda j, i: (i, ol + j)),
         vec, pl.BlockSpec((1, cw), lambda j, i: (0, nd + j)), blk, blk],
        (blk, blk, blk, blk, vec, vec), (dmerged, z, z, b_gates, b_gates, pa, plru), (), job)
    return res if job is None else (res, extra)


def _loss_head(y, target, name):
    t, d = y.shape
    tr = _row_tile(t, d, 256 * 1024)
    nt = t // tr

    def body(y_ref, t_ref, dy_ref, dy16_ref, loss_ref, acc):
        i = pl.program_id(0)

        @pl.when(i == 0)
        def _():
            acc[...] = jnp.zeros_like(acc)

        e = y_ref[...] - t_ref[...]
        dy = e * (1.0 / d)
        dy_ref[...] = dy
        dy16_ref[...] = dy.astype(BF16)
        acc[...] += jnp.sum(e * e, axis=0, keepdims=True)

        @pl.when(i == nt - 1)
        def _():
            loss_ref[...] = (0.5 / d) * jnp.sum(acc[...], axis=-1, keepdims=True)

    spec = pl.BlockSpec((tr, d), lambda i: (i, 0))
    return pl.pallas_call(
        body, name=name, out_shape=(jax.ShapeDtypeStruct((t, d), F32), jax.ShapeDtypeStruct((t, d), BF16),
                                    jax.ShapeDtypeStruct((1, 1), F32)),
        grid=(nt,), in_specs=[spec, spec], out_specs=(spec, spec, pl.BlockSpec((1, 1), lambda i: (0, 0))),
        scratch_shapes=[pltpu.VMEM((1, d), F32)], compiler_params=_params(),
    )(y, target)


def _adamw(w, g, m, v, name):
    r, c = w.shape
    tr, tc = _tile2d(r, c, 512 * 1024)
    c1 = 1.0 - ADAM_B1 ** ADAM_STEP
    c2 = 1.0 - ADAM_B2 ** ADAM_STEP

    def body(w_ref, g_ref, m_ref, v_ref, d_ref, nm_ref, nv_ref):
        gv = g_ref[...]
        mn = ADAM_B1 * m_ref[...] + (1.0 - ADAM_B1) * gv
        vn = ADAM_B2 * v_ref[...] + (1.0 - ADAM_B2) * (gv * gv)
        d_ref[...] = -ADAM_LR * ((mn / c1) / (jnp.sqrt(vn / c2) + ADAM_EPS) + ADAM_WD * w_ref[...])
        nm_ref[...] = mn
        nv_ref[...] = vn

    spec = pl.BlockSpec((tr, tc), lambda i, j: (i, j))
    shp = jax.ShapeDtypeStruct((r, c), F32)
    return pl.pallas_call(body, name=name, out_shape=(shp, shp, shp), grid=(r // tr, c // tc), in_specs=[spec] * 4,
                          out_specs=(spec, spec, spec), compiler_params=_params())(w, g, m, v)


def _swap_halves(v, lane, half):
    n = v.shape[-1]
    return jnp.where(lane < half, pltpu.roll(v, n - half, 1),
                     jnp.where(lane < 2 * half, pltpu.roll(v, half, 1), 0.0))


def _norm_fwd(xraw, g):
    rstd = lax.rsqrt(jnp.mean(xraw * xraw, axis=-1, keepdims=True) + EPS)
    xhat = xraw * rstd
    return xhat, rstd, xhat * g


def _norm_bwd(dy, xhat, rstd, g):
    dxhat = dy * g
    dx = rstd * (dxhat - xhat * jnp.mean(dxhat * xhat, axis=-1, keepdims=True))
    return dx, jnp.sum(dy * xhat, axis=0, keepdims=True)


def _attn_specs(nb, grp, hd, kv, clamp):
    qo, ko, vo = 0, (kv * grp), (kv * grp + kv)
    cur = (lambda i: jnp.minimum(i, nb - 1)) if clamp else (lambda i: i)
    prev = lambda i: jnp.maximum(cur(i) - 1, 0)
    zq = pl.BlockSpec((BLK, grp * hd), lambda h, i: (cur(i), h))
    kc = pl.BlockSpec((BLK, hd), lambda h, i: (cur(i), ko + h))
    kp = pl.BlockSpec((BLK, hd), lambda h, i: (prev(i), ko + h))
    vc = pl.BlockSpec((BLK, hd), lambda h, i: (cur(i), vo + h))
    vp = pl.BlockSpec((BLK, hd), lambda h, i: (prev(i), vo + h))
    tc = pl.BlockSpec((BLK, hd), lambda h, i: (cur(i), 0))
    tp = pl.BlockSpec((BLK, hd), lambda h, i: (prev(i), 0))
    gs = pl.BlockSpec((1, hd), lambda h, i: (0, 0))
    return zq, kc, kp, vc, vp, tc, tp, gs


def _attn_mask(i):
    qi = lax.broadcasted_iota(jnp.int32, (BLK, 2 * BLK), 0)
    kj = lax.broadcasted_iota(jnp.int32, (BLK, 2 * BLK), 1)
    rel = qi + BLK - kj
    return (rel >= 0) & (rel < WINDOW) & ((kj >= BLK) | (i > 0))


def _attn_fwd(z, cos_t, sin_t, qg, kg, sinks, kv, grp, hd, name, job=None):
    t = z.shape[0]
    nb = t // BLK
    half = hd // 8
    scale = 1.0 / math.sqrt(hd)
    zq, kc, kp, vc, vp, tc, tp, gs = _attn_specs(nb, grp, hd, kv, False)

    def body(sink_ref, zq_ref, kc_ref, kp_ref, vc_ref, vp_ref, cc_ref, sc_ref, cp_ref, sp_ref, qg_ref, kg_ref, o_ref):
        h, i = pl.program_id(0), pl.program_id(1)
        lane = lax.broadcasted_iota(jnp.int32, (BLK, hd), 1)

        def normrope(xraw, g, c, s):
            y = _norm_fwd(xraw, g)[2]
            return y * c + _swap_halves(y, lane, half) * s

        cc, sc = cc_ref[...], sc_ref[...]
        kcur = normrope(kc_ref[...], kg_ref[...], cc, sc)
        kprev = normrope(kp_ref[...], kg_ref[...], cp_ref[...], sp_ref[...])
        kk = jnp.concatenate([kprev, kcur], axis=0).astype(BF16)
        vv = jnp.concatenate([vp_ref[...], vc_ref[...]], axis=0).astype(BF16)
        mask = _attn_mask(i)
        for g in range(grp):
            q = normrope(zq_ref[:, g * hd:(g + 1) * hd], qg_ref[...], cc, sc).astype(BF16)
            s = lax.dot_general(q, kk, (((1,), (1,)), ((), ())), preferred_element_type=F32) * scale
            s = jnp.where(mask, s, NEG)
            sk = sink_ref[h * grp + g]
            mx = jnp.maximum(jnp.max(s, axis=-1, keepdims=True), sk)
            p = jnp.exp(s - mx)
            den = jnp.sum(p, axis=-1, keepdims=True) + jnp.exp(sk - mx)
            p = p / den
            o_ref[:, g * hd:(g + 1) * hd] = jnp.dot(p.astype(BF16), vv, preferred_element_type=F32).astype(BF16)

    res, extra = _call(
        body, name, jax.ShapeDtypeStruct((t, kv * grp * hd), BF16), (kv, nb),
        [pl.BlockSpec(memory_space=pltpu.SMEM), zq, kc, kp, vc, vp, tc, tc, tp, tp, gs, gs],
        pl.BlockSpec((BLK, grp * hd), lambda h, i: (i, h)),
        (sinks, z, z, z, z, z, cos_t, sin_t, cos_t, sin_t, qg, kg), (), job)
    return res if job is None else (res, extra)


def _attn_bwd(dattn, z, cos_t, sin_t, qg, kg, sinks, kv, grp, hd, name):
    t = z.shape[0]
    nb = t // BLK
    half = hd // 8
    scale = 1.0 / math.sqrt(hd)
    zq, kc, kp, vc, vp, tc, tp, gs = _attn_specs(nb, grp, hd, kv, True)

    def body(sink_ref, zq_ref, kc_ref, kp_ref, vc_ref, vp_ref, cc_ref, sc_ref, cp_ref, sp_ref, qg_ref, kg_ref, do_ref,
             dq_ref, dk_ref, dv_ref, dqg_ref, dkg_ref, dsk_ref, dk_carry, dv_carry):
        h, i = pl.program_id(0), pl.program_id(1)
        lane = lax.broadcasted_iota(jnp.int32, (BLK, hd), 1)
        lane1 = lax.broadcasted_iota(jnp.int32, (1, LANE), 1)

        @pl.when((h == 0) & (i == 0))
        def _():
            dqg_ref[...] = jnp.zeros_like(dqg_ref)
            dkg_ref[...] = jnp.zeros_like(dkg_ref)
            dsk_ref[...] = jnp.zeros_like(dsk_ref)

        @pl.when(i == 0)
        def _():
            dk_carry[...] = jnp.zeros_like(dk_carry)
            dv_carry[...] = jnp.zeros_like(dv_carry)

        def rope(y, c, s):
            return y * c + _swap_halves(y, lane, half) * s

        def rope_bwd(dout, c, s):
            return dout * c + _swap_halves(dout * s, lane, half)

        @pl.when(i < nb)
        def _():
            cc, sc, cp, sp = cc_ref[...], sc_ref[...], cp_ref[...], sp_ref[...]
            qgv, kgv = qg_ref[...], kg_ref[...]
            xh_kc, rs_kc, y_kc = _norm_fwd(kc_ref[...], kgv)
            xh_kp, rs_kp, y_kp = _norm_fwd(kp_ref[...], kgv)
            kk = jnp.concatenate([rope(y_kp, cp, sp), rope(y_kc, cc, sc)], axis=0).astype(BF16)
            vv = jnp.concatenate([vp_ref[...], vc_ref[...]], axis=0).astype(BF16)
            mask = _attn_mask(i)
            dkk = jnp.zeros((2 * BLK, hd), F32)
            dvv = jnp.zeros((2 * BLK, hd), F32)
            dqg = jnp.zeros((1, hd), F32)
            dsk = jnp.zeros((1, LANE), F32)
            for g in range(grp):
                xh_q, rs_q, y_q = _norm_fwd(zq_ref[:, g * hd:(g + 1) * hd], qgv)
                q = rope(y_q, cc, sc).astype(BF16)
                s = lax.dot_general(q, kk, (((1,), (1,)), ((), ())), preferred_element_type=F32) * scale
                s = jnp.where(mask, s, NEG)
                sk = sink_ref[h * grp + g]
                mx = jnp.maximum(jnp.max(s, axis=-1, keepdims=True), sk)
                p = jnp.exp(s - mx)
                den = jnp.sum(p, axis=-1, keepdims=True) + jnp.exp(sk - mx)
                p = p / den
                psink = jnp.exp(sk - mx) / den
                dog = do_ref[:, g * hd:(g + 1) * hd].astype(BF16)
                dp = lax.dot_general(dog, vv, (((1,), (1,)), ((), ())), preferred_element_type=F32)
                rsum = jnp.sum(p * dp, axis=-1, keepdims=True)
                ds = (p * (dp - rsum) * scale).astype(BF16)
                dsk = dsk + jnp.where(lane1 == h * grp + g, jnp.sum(-psink * rsum, axis=0, keepdims=True), 0.0)
                dqn = jnp.dot(ds, kk, preferred_element_type=F32)
                dkk = dkk + lax.dot_general(ds, q, (((0,), (0,)), ((), ())), preferred_element_type=F32)
                dvv = dvv + lax.dot_general(p.astype(BF16), dog, (((0,), (0,)), ((), ())), preferred_element_type=F32)
                dxq, dg_q = _norm_bwd(rope_bwd(dqn, cc, sc), xh_q, rs_q, qgv)
                dq_ref[:, g * hd:(g + 1) * hd] = dxq
                dqg = dqg + dg_q
            dkp_raw, dg_kp = _norm_bwd(rope_bwd(dkk[:BLK], cp, sp), xh_kp, rs_kp, kgv)
            dkc_raw, dg_kc = _norm_bwd(rope_bwd(dkk[BLK:], cc, sc), xh_kc, rs_kc, kgv)
            dk_ref[...] = dk_carry[...] + dkp_raw
            dv_ref[...] = dv_carry[...] + dvv[:BLK]
            dk_carry[...] = dkc_raw
            dv_carry[...] = dvv[BLK:]
            dqg_ref[...] += dqg
            dkg_ref[...] += dg_kp + dg_kc
            dsk_ref[...] += dsk

        @pl.when(i == nb)
        def _():
            dk_ref[...] = dk_carry[...]
            dv_ref[...] = dv_carry[...]

    kvw = kv * hd
    vec = pl.BlockSpec((1, hd), lambda h, i: (0, 0))
    shifted = pl.BlockSpec((BLK, hd), lambda h, i: (jnp.maximum(i - 1, 0), h))
    return pl.pallas_call(
        body, name=name,
        out_shape=(jax.ShapeDtypeStruct((t, kv * grp * hd), F32), jax.ShapeDtypeStruct((t, kvw), F32),
                   jax.ShapeDtypeStruct((t, kvw), F32), jax.ShapeDtypeStruct((1, hd), F32),
                   jax.ShapeDtypeStruct((1, hd), F32), jax.ShapeDtypeStruct((1, LANE), F32)),
        grid=(kv, nb + 1),
        in_specs=[pl.BlockSpec(memory_space=pltpu.SMEM), zq, kc, kp, vc, vp, tc, tc, tp, tp, gs, gs,
                  pl.BlockSpec((BLK, grp * hd), lambda h, i: (jnp.minimum(i, nb - 1), h))],
        out_specs=(pl.BlockSpec((BLK, grp * hd), lambda h, i: (jnp.minimum(i, nb - 1), h)), shifted, shifted, vec, vec,
                   pl.BlockSpec((1, LANE), lambda h, i: (0, 0))),
        scratch_shapes=[pltpu.VMEM((BLK, hd), F32), pltpu.VMEM((BLK, hd), F32)], compiler_params=_params(),
    )(sinks, z, z, z, z, z, cos_t, sin_t, cos_t, sin_t, qg, kg, dattn)


def _conv_fwd(u, w, b, name):
    t, c = u.shape
    taps = w.shape[0]
    cb = _pick(c, (1408, 1024, 512, 256, 128))
    tr = _row_tile(t, cb, 256 * 1024)
    hb = tr // SUBLANE

    def body(u_ref, halo_ref, w_ref, b_ref, o_ref):
        i = pl.program_id(0)
        x = u_ref[...]
        acc = b_ref[...] + w_ref[taps - 1:taps, :] * x
        for k in range(taps - 1):
            acc = acc + w_ref[k:k + 1, :] * pltpu.roll(x, taps - 1 - k, 0)
        o_ref[...] = acc
        row = lax.broadcasted_iota(jnp.int32, (SUBLANE, cb), 0)
        hp = jnp.where(i > 0, halo_ref[...], 0.0)
        x8 = u_ref[0:SUBLANE, :]
        acc8 = b_ref[...] + w_ref[taps - 1:taps, :] * x8
        for k in range(taps - 1):
            s = taps - 1 - k
            acc8 = acc8 + w_ref[k:k + 1, :] * jnp.where(row < s, pltpu.roll(hp, s, 0), pltpu.roll(x8, s, 0))
        o_ref[0:SUBLANE, :] = acc8

    blk = pl.BlockSpec((tr, cb), lambda i, j: (i, j))
    return pl.pallas_call(
        body, name=name, out_shape=jax.ShapeDtypeStruct((t, c), F32), grid=(t // tr, c // cb),
        in_specs=[blk, pl.BlockSpec((SUBLANE, cb), lambda i, j: (jnp.maximum(i * hb - 1, 0), j)),
                  pl.BlockSpec((taps, cb), lambda i, j: (0, j)), pl.BlockSpec((1, cb), lambda i, j: (0, j))],
        out_specs=blk, compiler_params=_params(),
    )(u, u, w, b)


def _conv_bwd(duc, u, w, name):
    t, c = u.shape
    taps = w.shape[0]
    cb = _pick(c, (1408, 1024, 512, 256, 128))
    tr = _row_tile(t, cb, 256 * 1024)
    hb, nt = tr // SUBLANE, t // tr

    def body(g_ref, gnext_ref, u_ref, uprev_ref, w_ref, du_ref, dw_ref, db_ref):
        i = pl.program_id(1)

        @pl.when(i == 0)
        def _():
            dw_ref[...] = jnp.zeros_like(dw_ref)
            db_ref[...] = jnp.zeros_like(db_ref)

        row = lax.broadcasted_iota(jnp.int32, (SUBLANE, cb), 0)
        g, x = g_ref[...], u_ref[...]
        du = w_ref[taps - 1:taps, :] * g
        for k in range(taps - 1):
            du = du + w_ref[k:k + 1, :] * pltpu.roll(g, tr - (taps - 1 - k), 0)
        du_ref[...] = du
        hn = jnp.where(i < nt - 1, gnext_ref[...], 0.0)
        g8 = g_ref[tr - SUBLANE:tr, :]
        du8 = w_ref[taps - 1:taps, :] * g8
        for k in range(taps - 1):
            s = taps - 1 - k
            du8 = du8 + w_ref[k:k + 1, :] * jnp.where(row >= SUBLANE - s, pltpu.roll(hn, SUBLANE - s, 0),
                                                     pltpu.roll(g8, SUBLANE - s, 0))
        du_ref[tr - SUBLANE:tr, :] = du8

        hp = jnp.where(i > 0, uprev_ref[...], 0.0)
        xl8, gf8 = u_ref[tr - SUBLANE:tr, :], g_ref[0:SUBLANE, :]
        db_ref[...] += jnp.sum(g, axis=0, keepdims=True)
        dw_ref[taps - 1:taps, :] += jnp.sum(g * x, axis=0, keepdims=True)
        for k in range(taps - 1):
            s = taps - 1 - k
            fix = jnp.where(row < s, pltpu.roll(hp, s, 0) - pltpu.roll(xl8, s, 0), 0.0)
            dw_ref[k:k + 1, :] += (jnp.sum(g * pltpu.roll(x, s, 0), axis=0, keepdims=True)
                                   + jnp.sum(gf8 * fix, axis=0, keepdims=True))

    blk = pl.BlockSpec((tr, cb), lambda j, i: (i, j))
    nh = t // SUBLANE
    return pl.pallas_call(
        body, name=name,
        out_shape=(jax.ShapeDtypeStruct((t, c), F32), jax.ShapeDtypeStruct((taps, c), F32),
                   jax.ShapeDtypeStruct((1, c), F32)),
        grid=(c // cb, nt),
        in_specs=[blk, pl.BlockSpec((SUBLANE, cb), lambda j, i: (jnp.minimum((i + 1) * hb, nh - 1), j)),
                  blk, pl.BlockSpec((SUBLANE, cb), lambda j, i: (jnp.maximum(i * hb - 1, 0), j)),
                  pl.BlockSpec((taps, cb), lambda j, i: (0, j))],
        out_specs=(blk, pl.BlockSpec((taps, cb), lambda j, i: (0, j)), pl.BlockSpec((1, cb), lambda j, i: (0, j))),
        compiler_params=_params(),
    )(duc, duc, u, u, w)


def _dense_groups(w, gw):
    n, bw, _ = w.shape
    per = gw // bw
    w4 = w.reshape(n // per, per, bw, bw).astype(BF16)
    eye = jnp.eye(per, dtype=BF16)
    return (w4[:, :, :, None, :] * eye[None, :, None, :, None]).reshape(n // per, gw, gw)


def _diag_blocks(dense, n, bw):
    ng, gw, _ = dense.shape
    per = gw // bw
    diag = jnp.diagonal(dense.reshape(ng, per, bw, per, bw), axis1=1, axis2=3)
    return jnp.moveaxis(diag, -1, 1).reshape(n, bw, bw)


def _gates_fwd(uc, wr, wi, name):
    t, c = uc.shape
    ng, gw, _ = wr.shape
    tr = _pick(t, (512, 256, 128))

    def body(u_ref, wr_ref, wi_ref, r_ref, i_ref):
        a = u_ref[...].astype(BF16)
        r_ref[...] = jnp.dot(a, wr_ref[...], preferred_element_type=F32)
        i_ref[...] = jnp.dot(a, wi_ref[...], preferred_element_type=F32)

    blk = pl.BlockSpec((tr, gw), lambda h, i: (i, h))
    wsp = pl.BlockSpec((None, gw, gw), lambda h, i: (h, 0, 0))
    shp = jax.ShapeDtypeStruct((t, c), F32)
    return pl.pallas_call(body, name=name, out_shape=(shp, shp), grid=(ng, t // tr), in_specs=[blk, wsp, wsp],
                          out_specs=(blk, blk), compiler_params=_params())(uc, wr, wi)


def _gates_bwd_x(duc, drp, dip, wr, wi, name):
    t, c = duc.shape
    ng, gw, _ = wr.shape
    tr = _pick(t, (512, 256, 128))
    dims = (((1,), (1,)), ((), ()))

    def body(d_ref, r_ref, i_ref, wr_ref, wi_ref, o_ref):
        o_ref[...] = (d_ref[...]
                      + lax.dot_general(r_ref[...].astype(BF16), wr_ref[...], dims, preferred_element_type=F32)
                      + lax.dot_general(i_ref[...].astype(BF16), wi_ref[...], dims, preferred_element_type=F32))

    blk = pl.BlockSpec((tr, gw), lambda h, i: (i, h))
    wsp = pl.BlockSpec((None, gw, gw), lambda h, i: (h, 0, 0))
    return pl.pallas_call(body, name=name, out_shape=jax.ShapeDtypeStruct((t, c), F32), grid=(ng, t // tr),
                          in_specs=[blk, blk, blk, wsp, wsp], out_specs=blk, compiler_params=_params())(duc, drp, dip, wr, wi)


def _gates_bwd_w(uc, dpre, ng, gw, name):
    t, c = uc.shape
    tk = _pick(t, (512, 256, 128))
    dims = (((0,), (0,)), ((), ()))

    def body(u_ref, d_ref, o_ref):
        @pl.when(pl.program_id(1) == 0)
        def _():
            o_ref[...] = jnp.zeros_like(o_ref)

        o_ref[...] += lax.dot_general(u_ref[...].astype(BF16), d_ref[...].astype(BF16), dims, preferred_element_type=F32)

    blk = pl.BlockSpec((tk, gw), lambda h, i: (i, h))
    return pl.pallas_call(body, name=name, out_shape=jax.ShapeDtypeStruct((ng, gw, gw), F32), grid=(ng, t // tk),
                          in_specs=[blk, blk], out_specs=pl.BlockSpec((None, gw, gw), lambda h, i: (h, 0, 0)),
                          compiler_params=_params())(uc, dpre)


def _softplus(x):
    return jnp.maximum(x, 0.0) + jnp.log(1.0 + jnp.exp(-jnp.abs(x)))


def _neg_expm1(x):
    series = x * (1.0 + x * (0.5 + x * (1.0 / 6.0 + x * (1.0 / 24.0 + x * (1.0 / 120.0)))))
    return -jnp.where(x > -0.05, series, jnp.exp(x) - 1.0)


_GELU_C = math.sqrt(2.0 / math.pi)


def _gelu_parts(x):
    inner = _GELU_C * (x + 0.044715 * (x * x * x))
    th = jnp.tanh(inner)
    gelu = 0.5 * x * (1.0 + th)
    dgelu = 0.5 * (1.0 + th) + 0.5 * x * (1.0 - th * th) * (_GELU_C * (1.0 + 3.0 * 0.044715 * (x * x)))
    return gelu, dgelu


def _lru_gate_values(uc, rpre, ipre, br, bi, sp):
    r = _sigmoid(rpre + br)
    ig = _sigmoid(ipre + bi)
    a = jnp.exp(-LRU_C * r * sp)
    mult = jnp.sqrt(jnp.maximum(_neg_expm1(2.0 * (-LRU_C * r * sp)), 0.0))
    return r, ig, a, mult


def _lru_fwd(uc, rpre, ipre, gr, br, bi, lam, name, job=None):
    t, c = uc.shape
    cb = _pick(c, (1408, 1024, 512, 256, 128))
    tb = _pick(t, (512, 256, 128))
    ntile = tb // SUBLANE

    def body(uc_ref, r_ref, i_ref, gr_ref, br_ref, bi_ref, lam_ref, h_ref, rec16_ref, carry, rec_ref):
        @pl.when(pl.program_id(1) == 0)
        def _():
            carry[...] = jnp.zeros_like(carry)

        sp = _softplus(-lam_ref[...])
        br, bi = br_ref[...], bi_ref[...]
        row = lax.broadcasted_iota(jnp.int32, (SUBLANE, cb), 0)

        def tile(k, c_in):
            sl = pl.ds(pl.multiple_of(k * SUBLANE, SUBLANE), SUBLANE)
            ucv = uc_ref[sl, :]
            _, ig, a, mult = _lru_gate_values(ucv, r_ref[sl, :], i_ref[sl, :], br, bi, sp)
            b = mult * (ig * ucv)
            for d in (1, 2, 4):
                a_s = jnp.where(row >= d, pltpu.roll(a, d, 0), 1.0)
                b_s = jnp.where(row >= d, pltpu.roll(b, d, 0), 0.0)
                b = a * b_s + b
                a = a * a_s
            hv = b + a * c_in
            h_ref[sl, :] = hv
            rec_ref[sl, :] = hv * _gelu_parts(gr_ref[sl, :])[0]
            return hv[SUBLANE - 1:SUBLANE, :]

        c_out = lax.fori_loop(0, ntile, tile, carry[0:1, :])
        carry[...] = jnp.broadcast_to(c_out, (SUBLANE, cb))
        rec16_ref[...] = rec_ref[...].astype(BF16)

    blk = pl.BlockSpec((tb, cb), lambda j, i: (i, j))
    vec = pl.BlockSpec((1, cb), lambda j, i: (0, j))
    res, extra = _call(body, name, (jax.ShapeDtypeStruct((t, c), F32), jax.ShapeDtypeStruct((t, c), BF16)),
                       (c // cb, t // tb), [blk, blk, blk, blk, vec, vec, vec], (blk, blk),
                       (uc, rpre, ipre, gr, br, bi, lam), [pltpu.VMEM((SUBLANE, cb), F32), pltpu.VMEM((tb, cb), F32)], job)
    return res if job is None else (res, extra)


def _lru_bwd(drec, hst, uc, rpre, ipre, gr, br, bi, lam, name, job=None):
    t, c = uc.shape
    cb = _pick(c, (1408, 1024, 512, 256, 128))
    tb = _pick(t, (256, 128))
    ntile, nt, hb = tb // SUBLANE, t // tb, tb // SUBLANE

    def body(drec_ref, h_ref, hprev_ref, uc_ref, r_ref, i_ref, gr_ref, br_ref, bi_ref, lam_ref,
             dgr_ref, drp_ref, dip_ref, duc_ref, dlam_ref, dbr_ref, dbi_ref, carry):
        step = pl.program_id(1)
        first_block = step == nt - 1

        @pl.when(step == 0)
        def _():
            carry[...] = jnp.zeros_like(carry)
            dlam_ref[...] = jnp.zeros_like(dlam_ref)
            dbr_ref[...] = jnp.zeros_like(dbr_ref)
            dbi_ref[...] = jnp.zeros_like(dbi_ref)

        lam = lam_ref[...]
        sp = _softplus(-lam)
        br, bi = br_ref[...], bi_ref[...]
        row = lax.broadcasted_iota(jnp.int32, (SUBLANE, cb), 0)
        halo = jnp.where(first_block, 0.0, hprev_ref[...])

        def tile(kk, state):
            c_p, acc_sp, acc_br, acc_bi = state
            k = ntile - 1 - kk
            sl = pl.ds(pl.multiple_of(k * SUBLANE, SUBLANE), SUBLANE)
            slp = pl.ds(pl.multiple_of(jnp.maximum(k - 1, 0) * SUBLANE, SUBLANE), SUBLANE)
            ucv = uc_ref[sl, :]
            r, ig, a, mult = _lru_gate_values(ucv, r_ref[sl, :], i_ref[sl, :], br, bi, sp)
            hv = h_ref[sl, :]
            below = jnp.where(k > 0, h_ref[slp, :], halo)
            hprev = jnp.where(row == 0, pltpu.roll(below, 1, 0), pltpu.roll(hv, 1, 0))
            gelu, dgelu = _gelu_parts(gr_ref[sl, :])
            drec = drec_ref[sl, :]
            dh = drec * gelu
            dgr_ref[sl, :] = drec * hv * dgelu
            pa, pb = a, a * dh
            for d in (1, 2, 4):
                a_s = jnp.where(row < SUBLANE - d, pltpu.roll(pa, SUBLANE - d, 0), 1.0)
                b_s = jnp.where(row < SUBLANE - d, pltpu.roll(pb, SUBLANE - d, 0), 0.0)
                pb = pa * b_s + pb
                pa = pa * a_s
            pv = pb + pa * c_p
            gt = dh + jnp.where(row == SUBLANE - 1, c_p, pltpu.roll(pv, SUBLANE - 1, 0))
            da = gt * hprev
            duc_ref[sl, :] = gt * mult * ig
            dmult = gt * ig * ucv
            dig = gt * mult * ucv
            dla = da * a - jnp.where(mult > 0.0, dmult * (a * a) / mult, 0.0)
            drp = dla * (-LRU_C * sp) * (r * (1.0 - r))
            dip = dig * (ig * (1.0 - ig))
            drp_ref[sl, :] = drp
            dip_ref[sl, :] = dip
            return pv[0:1, :], acc_sp + dla * (-LRU_C * r), acc_br + drp, acc_bi + dip

        zero = jnp.zeros((SUBLANE, cb), F32)
        c_out, acc_sp, acc_br, acc_bi = lax.fori_loop(0, ntile, tile, (carry[0:1, :], zero, zero, zero))
        carry[...] = jnp.broadcast_to(c_out, (SUBLANE, cb))
        dlam_ref[...] += jnp.sum(acc_sp, axis=0, keepdims=True) * (-_sigmoid(-lam))
        dbr_ref[...] += jnp.sum(acc_br, axis=0, keepdims=True)
        dbi_ref[...] += jnp.sum(acc_bi, axis=0, keepdims=True)

    blk = pl.BlockSpec((tb, cb), lambda j, i: (nt - 1 - i, j))
    vec = pl.BlockSpec((1, cb), lambda j, i: (0, j))
    halo_spec = pl.BlockSpec((SUBLANE, cb), lambda j, i: (jnp.maximum((nt - 1 - i) * hb - 1, 0), j))
    big, small = jax.ShapeDtypeStruct((t, c), F32), jax.ShapeDtypeStruct((1, c), F32)
    res, extra = _call(
        body, name, (big, big, big, big, small, small, small), (c // cb, nt),
        [blk, blk, halo_spec, blk, blk, blk, blk, vec, vec, vec], (blk, blk, blk, blk, vec, vec, vec),
        (drec, hst, hst, uc, rpre, ipre, gr, br, bi, lam), [pltpu.VMEM((SUBLANE, cb), F32)], job)
    return res if job is None else (res, extra)


def _shard_region(ref, kind, chip, half, rh, width):
    if kind == "col":
        return ref.at[pl.ds(half * rh, rh), pl.ds(chip * width, width)]
    return ref.at[pl.ds(chip * (2 * rh) + half * rh, rh), :]


class _AllGather(_Exchange):
    def __init__(self, fulls, kinds):
        self.inputs, self.kinds = list(fulls), kinds
        self.out_shapes = [jax.ShapeDtypeStruct(f.shape, f.dtype) for f in fulls]
        self.aliases = {a: a for a in range(len(fulls))}
        self.n_sems = 6 * len(fulls)
        self.geo = [(f.shape[0] // 2, f.shape[1] // N_CHIPS) if k == "col" else (f.shape[0] // (2 * N_CHIPS), f.shape[1])
                    for f, k in zip(fulls, kinds)]

    def _region(self, ref, a, chip, half):
        return _shard_region(ref, self.kinds[a], chip, half, *self.geo[a])

    def _ici(self, e, a, k, chip):
        cx, cy = e.chips[k]
        return e.copy(self._region(e.ins[a], a, chip, e.c), self._region(e.outs[a], a, chip, e.c), a * 6 + k,
                      (cx, cy, e.c))

    def _d2d(self, e, a, k, half):
        cx, cy = e.chips[k]
        region = self._region(e.outs[a], a, 2 * cx + cy, half)
        return e.copy(region, region, a * 6 + 3 + k, e.sibling)

    def start(self, e):
        for a in range(len(self.inputs)):
            for k in range(3):
                self._ici(e, a, k, e.me).start()

    def finish(self, e):
        n = len(self.inputs)
        for a in range(n):
            for k, (cx, cy) in enumerate(e.chips):
                self._ici(e, a, k, 2 * cx + cy).wait_recv()
                self._d2d(e, a, k, e.c).start()
        for a in range(n):
            for k in range(3):
                self._d2d(e, a, k, 1 - e.c).wait_recv()
        for a in range(n):
            for k in range(3):
                self._ici(e, a, k, e.me).wait_send()
                self._d2d(e, a, k, e.c).wait_send()


class _SiblingExchange(_Exchange):
    def __init__(self, grads):
        self.inputs = list(grads)
        self.out_shapes = [jax.ShapeDtypeStruct((g.shape[0],) + g.shape[2:], g.dtype) for g in grads]
        self.n_sems = len(grads)

    def _copy(self, e, a):
        return e.copy(e.ins[a].at[:, 1 - e.c], e.outs[a], a, e.sibling)

    def start(self, e):
        for a in range(len(self.inputs)):
            self._copy(e, a).start()

    def finish(self, e):
        for a in range(len(self.inputs)):
            self._copy(e, a).wait()


def _piece(ref, kind, chip, width):
    if kind == "col":
        return ref.at[0, :, pl.ds(chip * width, width)]
    return ref.at[chip]


class _ChipExchange(_Exchange):
    def __init__(self, sums, kinds):
        self.inputs, self.kinds = list(sums), kinds
        self.widths = [s.shape[2] // N_CHIPS if k == "col" else s.shape[2] for s, k in zip(sums, kinds)]
        self.out_shapes = [jax.ShapeDtypeStruct((3, s.shape[1], w), s.dtype) for s, w in zip(sums, self.widths)]
        self.n_sems = 3 * len(sums)

    def _copy(self, e, a, k, chip):
        cx, cy = e.chips[k]
        return e.copy(_piece(e.ins[a], self.kinds[a], chip, self.widths[a]), e.outs[a].at[k], a * 3 + k, (cx, cy, e.c))

    def start(self, e):
        for a in range(len(self.inputs)):
            for k, (cx, cy) in enumerate(e.chips):
                self._copy(e, a, k, 2 * cx + cy).start()

    def finish(self, e):
        for a in range(len(self.inputs)):
            for k, (cx, cy) in enumerate(e.chips):
                self._copy(e, a, k, 2 * cx + cy).wait()


class _FinishExchange(_Exchange):
    def __init__(self, finals, to_all):
        self.inputs, self.to_all = list(finals), list(to_all)
        self.out_shapes = [jax.ShapeDtypeStruct(f.shape, f.dtype) for f in finals]
        self.aliases = {a: a for a in range(len(finals))}
        self.first_sem, self.n_sems = [], 0
        for all8 in self.to_all:
            self.first_sem.append(self.n_sems)
            self.n_sems += 7 if all8 else 1
        self.rel = [(fx, fy, fc) for fx in (0, 1) for fy in (0, 1) for fc in (0, 1)][1:]

    def _copies(self, e, mine):
        for a, all8 in enumerate(self.to_all):
            src = e.ins[a] if mine else e.outs[a]
            if not all8:
                rh = self.inputs[a].shape[0] // 2
                rows = pl.ds((e.c if mine else 1 - e.c) * rh, rh)
                yield e.copy(src.at[rows, :], e.outs[a].at[rows, :], self.first_sem[a], e.sibling)
                continue
            rh = self.inputs[a].shape[0] // (2 * N_CHIPS)
            for r, (fx, fy, fc) in enumerate(self.rel):
                px, py, pc = (1 - e.x if fx else e.x), (1 - e.y if fy else e.y), (1 - e.c if fc else e.c)
                rows = pl.ds(((2 * e.me + e.c) if mine else (2 * (2 * px + py) + pc)) * rh, rh)
                yield e.copy(src.at[rows, :], e.outs[a].at[rows, :], self.first_sem[a] + r, (px, py, pc))

    def start(self, e):
        for cp in self._copies(e, True):
            cp.start()

    def finish(self, e):
        for cp in self._copies(e, False):
            cp.wait_recv()
        for cp in self._copies(e, True):
            cp.wait_send()


def _cast_into_full(w, kind, idx, name):
    r, c = w.shape
    tr = _row_tile(r, c)
    nrb = r // tr

    def body(idx_ref, w_ref, o_ref):
        o_ref[...] = w_ref[...].astype(BF16)

    if kind == "col":
        full, out_map = (r, N_CHIPS * c), (lambda i, idx_ref: (i, idx_ref[1]))
    else:
        full, out_map = (N_CHIPS * r, c), (lambda i, idx_ref: (idx_ref[1] * nrb + i, 0))
    return pl.pallas_call(
        body, name=name, out_shape=jax.ShapeDtypeStruct(full, BF16),
        grid_spec=pltpu.PrefetchScalarGridSpec(
            num_scalar_prefetch=1, grid=(nrb,), in_specs=[pl.BlockSpec((tr, c), lambda i, idx_ref: (i, 0))],
            out_specs=pl.BlockSpec((tr, c), out_map)),
        compiler_params=_params(),
    )(idx, w)


def _matmul_gathering(a, placed, order, name):
    t, k = a.shape
    n = placed.shape[1]
    w = n // N_CHIPS
    tm, tn = _pick(t, _M_TILES), _pick(w, _N_TILES)
    ni, nj = t // tm, w // tn
    per_shard, total = ni * nj, N_CHIPS * ni * nj
    gather = _AllGather([placed], ["col"])

    def body(ord_ref, a_ref, w_own_ref, o_ref, w_ref, wbuf, fetch_sem, send, recv):
        s, i, j = pl.program_id(0), pl.program_id(1), pl.program_id(2)
        step = (s * ni + i) * nj + j
        e = _Env((w_own_ref,), (w_ref,), send, recv)

        def fetch(src, st):
            col = pl.multiple_of((ord_ref[st // per_shard] * nj + st % nj) * tn, LANE)
            return pltpu.make_async_copy(src.at[:, pl.ds(col, tn)], wbuf.at[st % 2], fetch_sem.at[st % 2])

        @pl.when(step == 0)
        def _():
            gather.start(e)
            fetch(w_own_ref, step).start()

        nxt = step + 1
        for kk, (cx, cy) in enumerate(e.chips):
            @pl.when(nxt == (kk + 1) * per_shard)
            def _():
                gather._ici(e, 0, kk, 2 * cx + cy).wait_recv()
                gather._d2d(e, 0, kk, e.c).start()
                gather._d2d(e, 0, kk, 1 - e.c).wait_recv()

        @pl.when(nxt < per_shard)
        def _():
            fetch(w_own_ref, nxt).start()

        @pl.when((nxt >= per_shard) & (nxt < total))
        def _():
            fetch(w_ref, nxt).start()

        fetch(w_ref, step).wait()
        o_ref[...] = jnp.dot(a_ref[...], wbuf[step % 2], preferred_element_type=F32)

        @pl.when(step == total - 1)
        def _():
            for kk in range(3):
                gather._ici(e, 0, kk, e.me).wait_send()
                gather._d2d(e, 0, kk, e.c).wait_send()

    z, full = pl.pallas_call(
        body, name=name, out_shape=(jax.ShapeDtypeStruct((t, n), F32), jax.ShapeDtypeStruct(placed.shape, placed.dtype)),
        grid_spec=pltpu.PrefetchScalarGridSpec(
            num_scalar_prefetch=1, grid=(N_CHIPS, ni, nj),
            in_specs=[pl.BlockSpec((tm, k), lambda s, i, j, ord_ref: (i, 0)), ANY],
            out_specs=(pl.BlockSpec((tm, tn), lambda s, i, j, ord_ref: (i, ord_ref[s] * nj + j)), ANY),
            scratch_shapes=[pltpu.VMEM((2, k, tn), placed.dtype), pltpu.SemaphoreType.DMA((2,)),
                            pltpu.SemaphoreType.DMA((gather.n_sems,)), pltpu.SemaphoreType.DMA((gather.n_sems,))]),
        input_output_aliases={2: 1}, compiler_params=_params(),
    )(order, a, placed)
    return z, full


def _add_own_half(g4, recv, idx, out_dtype, name):
    p, _, rh, n = g4.shape
    tr, tc = _tile2d(rh, n, 1024 * 1024)

    def body(idx_ref, g_ref, r_ref, o_ref):
        o_ref[...] = (g_ref[...] + r_ref[...]).astype(out_dtype)

    return pl.pallas_call(
        body, name=name, out_shape=jax.ShapeDtypeStruct((p, rh, n), out_dtype),
        grid_spec=pltpu.PrefetchScalarGridSpec(
            num_scalar_prefetch=1, grid=(p, rh // tr, n // tc),
            in_specs=[pl.BlockSpec((None, None, tr, tc), lambda q, i, j, idx_ref: (q, idx_ref[0], i, j)),
                      pl.BlockSpec((None, tr, tc), lambda q, i, j, idx_ref: (q, i, j))],
            out_specs=pl.BlockSpec((None, tr, tc), lambda q, i, j, idx_ref: (q, i, j))),
        compiler_params=_params(),
    )(idx, g4, recv)


def _sum_chips(own, kind, parts, idx, slots, to_all, name):
    _, rh, w = parts.shape
    tr, tc = _tile2d(rh, w, 512 * 1024)
    nrb, ncb = rh // tr, w // tc

    def body(idx_ref, own_ref, p0, p1, p2, o_ref):
        o_ref[...] = ((own_ref[...].astype(F32) + p0[...].astype(F32)) + p1[...].astype(F32)) + p2[...].astype(F32)

    if kind == "col":
        own_spec = pl.BlockSpec((None, tr, tc), lambda i, j, idx_ref: (0, i, idx_ref[1] * ncb + j))
    else:
        own_spec = pl.BlockSpec((None, tr, tc), lambda i, j, idx_ref: (idx_ref[1], i, j))
    if to_all:
        out_map = lambda i, j, idx_ref: ((2 * idx_ref[1] + idx_ref[0]) * nrb + i, j)
    else:
        out_map = lambda i, j, idx_ref: (idx_ref[0] * nrb + i, j)

    def part(k):
        return pl.BlockSpec((None, tr, tc), lambda i, j, idx_ref: (k, i, j))

    return pl.pallas_call(
        body, name=name, out_shape=jax.ShapeDtypeStruct((slots * rh, w), F32),
        grid_spec=pltpu.PrefetchScalarGridSpec(
            num_scalar_prefetch=1, grid=(nrb, ncb), in_specs=[own_spec, part(0), part(1), part(2)],
            out_specs=pl.BlockSpec((tr, tc), out_map)),
        compiler_params=_params(),
    )(idx, own, parts, parts, parts)


class _Reduce:
    def __init__(self, name, g, kind, idx, wire, to_all):
        r, c = g.shape
        self.name, self.kind, self.idx, self.wire, self.to_all = name, kind, idx, wire, to_all
        self.view = g.reshape(1, 2, r // 2, c) if kind == "col" else g.reshape(N_CHIPS, 2, r // (2 * N_CHIPS), c)

    def sibling(self):
        return _SiblingExchange([self.view])

    def got_sibling(self, outs):
        self.sum = _add_own_half(self.view, outs[0], self.idx, self.wire, "grad_chip_sum_" + self.name)

    def chips(self):
        return _ChipExchange([self.sum], [self.kind])

    def got_chips(self, outs):
        self.total = _sum_chips(self.sum, self.kind, outs[0], self.idx, 2 * N_CHIPS if self.to_all else 2,
                                self.to_all, "grad_total_" + self.name)


def _pack(arrays, rows):
    flat = jnp.concatenate([a.reshape(-1) for a in arrays])
    return jnp.pad(flat, (0, rows * SMALL_PACK_COLS - flat.shape[0])).reshape(rows, SMALL_PACK_COLS)


def _unpack(packed, shapes):
    flat = packed.reshape(-1)
    out, o = [], 0
    for shp in shapes:
        size = math.prod(shp)
        out.append(flat[o:o + size].reshape(shp))
        o += size
    return out


def _pack_rows(shapes):
    total = sum(math.prod(s) for s in shapes)
    unit = SMALL_PACK_COLS * N_CHIPS * 2 * SUBLANE
    return -(-total // unit) * (N_CHIPS * 2 * SUBLANE)


BIG = ("w_in", "w_attn_proj", "w_lru_proj", "w_out", "w_ffn_gate", "w_ffn_up", "w_ffn_down")
BIG_KIND = {"w_in": "col", "w_attn_proj": "row", "w_lru_proj": "row", "w_out": "row", "w_ffn_gate": "col",
            "w_ffn_up": "col", "w_ffn_down": "row"}
SMALL = ("norm1_g", "b_gates", "q_norm_g", "k_norm_g", "sinks", "conv_w", "conv_b", "w_rgate", "b_rgate",
         "w_igate", "b_igate", "lru_lambda", "norm2_g")
PACKED = tuple(n for n in SMALL if n not in ("w_rgate", "w_igate"))
WEIGHTS = ("norm1_g", "w_in", "b_gates", "q_norm_g", "k_norm_g", "sinks", "conv_w", "conv_b", "w_rgate", "b_rgate",
           "w_igate", "b_igate", "lru_lambda", "w_attn_proj", "w_lru_proj", "w_out", "norm2_g", "w_ffn_gate",
           "w_ffn_up", "w_ffn_down")


def kernel(x, positions, norm1_g, w_in, b_gates, q_norm_g, k_norm_g, sinks, conv_w, conv_b, w_rgate, b_rgate, w_igate, b_igate, lru_lambda, w_attn_proj, w_lru_proj, w_out, norm2_g, w_ffn_gate, w_ffn_up, w_ffn_down, loss_target, m_norm1_g, m_w_in, m_b_gates, m_q_norm_g, m_k_norm_g, m_sinks, m_conv_w, m_conv_b, m_w_rgate, m_b_rgate, m_w_igate, m_b_igate, m_lru_lambda, m_w_attn_proj, m_w_lru_proj, m_w_out, m_norm2_g, m_w_ffn_gate, m_w_ffn_up, m_w_ffn_down, v_norm1_g, v_w_in, v_b_gates, v_q_norm_g, v_k_norm_g, v_sinks, v_conv_w, v_conv_b, v_w_rgate, v_b_rgate, v_w_igate, v_b_igate, v_lru_lambda, v_w_attn_proj, v_w_lru_proj, v_w_out, v_norm2_g, v_w_ffn_gate, v_w_ffn_up, v_w_ffn_down):
    args = dict(locals())
    w = {n: args[n] for n in WEIGHTS}
    mom = {n: args["m_" + n] for n in WEIGHTS}
    var = {n: args["v_" + n] for n in WEIGHTS}

    t, d = x.shape[1], x.shape[2]
    hd = q_norm_g.shape[-1]
    nq = sinks.shape[-1]
    q_w = nq * hd
    d_rnn = conv_b.shape[-1]
    taps = conv_w.shape[1]
    n_blocks, bw = w_rgate.shape[1], w_rgate.shape[2]
    in_w = w_in.shape[-1] * N_CHIPS
    kv_w = (in_w - q_w - 2 * d_rnn - 2 * d) // 2
    kv = kv_w // hd
    grp = nq // kv
    u_off = q_w + 2 * kv_w
    gr_off = u_off + d_rnn
    ga_off = gr_off + d_rnn
    gw = bw * LANE // math.gcd(bw, LANE)
    ng = d_rnn // gw
    chip = 2 * lax.axis_index("x") + lax.axis_index("y")
    idx = jnp.stack([lax.axis_index("c"), chip]).astype(jnp.int32)

    x2, tgt = x[0], loss_target[0]

    placed = {n: _cast_into_full(w[n][0], BIG_KIND[n], idx, "cast_" + n) for n in BIG}

    def gather(*names):
        return _AllGather([placed[n] for n in names], [BIG_KIND[n] for n in names])

    mx, my = lax.axis_index("x"), lax.axis_index("y")
    order = jnp.stack([chip, 2 * (1 - mx) + my, 2 * mx + (1 - my), 2 * (1 - mx) + (1 - my)]).astype(jnp.int32)
    conv_w_full = _gather_small(conv_w[0], "allgather_conv_w")
    conv_w_full = jnp.transpose(conv_w_full, (1, 0, 2)).reshape(taps, d_rnn)
    wr_dense = _dense_groups(w_rgate[0], gw)
    wi_dense = _dense_groups(w_igate[0], gw)

    inv_freq = ROPE_THETA ** (-jnp.arange(0, hd // 4, 2, dtype=F32) / (hd // 4))
    ang = positions[0].astype(F32)[:, None] * inv_freq
    cos, sin = jnp.cos(ang), jnp.sin(ang)
    rest = hd - 2 * cos.shape[1]
    cos_t = jnp.concatenate([cos, cos, jnp.ones((t, rest), F32)], axis=1)
    sin_t = jnp.concatenate([-sin, sin, jnp.zeros((t, rest), F32)], axis=1)
    sinks1 = sinks[0]

    xn = _rms_fwd(x2, norm1_g, "rms1_fwd")
    z, win_f = _matmul_gathering(xn, placed["w_in"], order, "in_proj")
    zu, zgr = z[:, u_off:u_off + d_rnn], z[:, gr_off:gr_off + d_rnn]
    attn, (wap_f, wlp_f, wout_f) = _attn_fwd(z, cos_t, sin_t, q_norm_g, k_norm_g, sinks1, kv, grp, hd, "attn_fwd",
                                             job=gather("w_attn_proj", "w_lru_proj", "w_out"))
    uc = _conv_fwd(zu, conv_w_full, conv_b, "conv_fwd")
    rpre, ipre = _gates_fwd(uc, wr_dense, wi_dense, "gates_fwd")
    (hst, rec), (wg_f,) = _lru_fwd(uc, rpre, ipre, zgr, b_rgate, b_igate, lru_lambda, "lru_fwd",
                                   job=gather("w_ffn_gate"))
    pa = _matmul(attn, wap_f, "nn", "attn_proj")
    plru = _matmul(rec, wlp_f, "nn", "lru_proj")
    merged = _merge_fwd(z, b_gates, pa, plru, ga_off, "merge_fwd")
    h1 = _matmul(merged, wout_f, "nn", "out_proj", add=x2)
    hn = _rms_fwd(h1, norm2_g, "rms2_fwd")
    gate, (wu_f,) = _matmul(hn, wg_f, "nn", "ffn_gate", job=gather("w_ffn_up"))
    up, (wd_f,) = _matmul(hn, wu_f, "nn", "ffn_up", job=gather("w_ffn_down"))
    act = _swiglu_fwd(gate, up, "swiglu_fwd")
    yout = _matmul(act, wd_f, "nn", "ffn_down", add=h1)
    dy, dy16, loss_part = _loss_head(yout, tgt, "loss_head")
    loss = lax.psum(loss_part[0, 0], ("x", "y", "c"))

    def reduction(n, g):
        return _Reduce(n, g, BIG_KIND[n], idx, BF16, False)

    r_wd = reduction("w_ffn_down", _matmul(act, dy16, "tn", "d_w_ffn_down"))
    dact, got = _matmul(dy16, wd_f, "nt", "d_act", job=r_wd.sibling())
    r_wd.got_sibling(got)
    (dgate, dup), got = _swiglu_bwd(dact, gate, up, "swiglu_bwd", job=r_wd.chips())
    r_wd.got_chips(got)
    r_wg = reduction("w_ffn_gate", _matmul(hn, dgate, "tn", "d_w_ffn_gate"))
    g_wu, got = _matmul(hn, dup, "tn", "d_w_ffn_up", job=r_wg.sibling())
    r_wg.got_sibling(got)
    r_wu = reduction("w_ffn_up", g_wu)
    both = _Jobs(r_wu.sibling(), r_wg.chips())
    dhn, got = _matmul(dgate, wg_f, "nt", "d_hn_gate", job=both)
    got_wu, got_wg = both.split(got)
    r_wu.got_sibling(got_wu)
    r_wg.got_chips(got_wg)
    dhn, got = _matmul(dup, wu_f, "nt", "d_hn_up", add=dhn, job=r_wu.chips())
    r_wu.got_chips(got)
    dh1, g_norm2, dh1_16 = _rms_bwd(dhn, h1, norm2_g, dy, "rms2_bwd", mxu_copy=True)
    r_wout = reduction("w_out", _matmul(merged, dh1_16, "tn", "d_w_out"))
    dmerged, got = _matmul(dh1_16, wout_f, "nt", "d_merged", job=r_wout.sibling())
    r_wout.got_sibling(got)
    (dpa, dpl, dga, dgl, g_ba, g_bl), got = _merge_bwd(dmerged, z, b_gates, pa, plru, ga_off, "merge_bwd",
                                                       job=r_wout.chips())
    r_wout.got_chips(got)
    r_wap = reduction("w_attn_proj", _matmul(attn, dpa, "tn", "d_w_attn_proj"))
    dattn, got = _matmul(dpa, wap_f, "nt", "d_attn", job=r_wap.sibling())
    r_wap.got_sibling(got)
    g_wlp, got = _matmul(rec, dpl, "tn", "d_w_lru_proj", job=r_wap.chips())
    r_wap.got_chips(got)
    r_wlp = reduction("w_lru_proj", g_wlp)
    drec, got = _matmul(dpl, wlp_f, "nt", "d_rec", job=r_wlp.sibling())
    r_wlp.got_sibling(got)
    (dgr, drp, dip, duc_direct, g_lam, g_br, g_bi), got = _lru_bwd(
        drec, hst, uc, rpre, ipre, zgr, b_rgate, b_igate, lru_lambda, "lru_bwd", job=r_wlp.chips())
    r_wlp.got_chips(got)
    duc = _gates_bwd_x(duc_direct, drp, dip, wr_dense, wi_dense, "gates_bwd_x")
    g_wr = _diag_blocks(_gates_bwd_w(uc, drp, ng, gw, "gates_bwd_wr"), n_blocks, bw)
    g_wi = _diag_blocks(_gates_bwd_w(uc, dip, ng, gw, "gates_bwd_wi"), n_blocks, bw)
    du, g_convw, g_convb = _conv_bwd(duc, zu, conv_w_full, "conv_bwd")
    dq, dk, dv, g_qg, g_kg, g_sinks = _attn_bwd(dattn, z, cos_t, sin_t, q_norm_g, k_norm_g, sinks1, kv, grp, hd,
                                                 "attn_bwd")
    dz = jnp.concatenate([dq, dk, dv, du, dgr, dga, dgl], axis=1).astype(BF16)
    r_wr = _Reduce("w_rgate", g_wr.reshape(n_blocks * bw, bw), "row", idx, F32, True)
    r_wi = _Reduce("w_igate", g_wi.reshape(n_blocks * bw, bw), "row", idx, F32, True)
    both = _Jobs(r_wr.sibling(), r_wi.sibling())
    g_top, got = _matmul(xn, dz, "tn", "d_w_in_top", m_window=(0, d // 2), job=both)
    got_wr, got_wi = both.split(got)
    r_wr.got_sibling(got_wr)
    r_wi.got_sibling(got_wi)
    r_top = _Reduce("w_in_top", g_top, "col", idx, BF16, False)
    three = _Jobs(r_top.sibling(), r_wr.chips(), r_wi.chips())
    g_bot, got = _matmul(xn, dz, "tn", "d_w_in_bot", m_window=(d // 2, d // 2), job=three)
    got_top, got_wr, got_wi = three.split(got)
    r_top.got_sibling(got_top)
    r_wr.got_chips(got_wr)
    r_wi.got_chips(got_wi)
    r_bot = _Reduce("w_in_bot", g_bot, "col", idx, BF16, False)
    both = _Jobs(r_top.chips(), r_bot.sibling())
    dxn, got = _matmul(dz, win_f, "nt", "d_xn_a", m_window=(0, t // 2), into=(None, t), job=both)
    got_top, got_bot = both.split(got)
    r_top.got_chips(got_top)
    r_bot.got_sibling(got_bot)
    dxn, got = _matmul(dz, win_f, "nt", "d_xn_b", m_window=(t // 2, t // 2), into=(dxn, t), job=r_bot.chips())
    r_bot.got_chips(got)
    dx, g_norm1 = _rms_bwd(dxn, x2, norm1_g, dh1, "rms1_bwd")

    small_grads = {"norm1_g": g_norm1, "b_gates": jnp.concatenate([g_ba, g_bl], axis=1), "q_norm_g": g_qg,
                   "k_norm_g": g_kg, "sinks": g_sinks[:, :nq], "conv_w": g_convw, "conv_b": g_convb,
                   "b_rgate": g_br, "b_igate": g_bi, "lru_lambda": g_lam, "norm2_g": g_norm2}
    gshapes = [small_grads[n].shape for n in PACKED]
    r_small = _Reduce("small", _pack([small_grads[n] for n in PACKED], _pack_rows(gshapes)), "row", idx, F32, True)
    r_small.got_sibling(_run_exchange(r_small.sibling(), "grad_sibling_exchange_small"))
    r_small.got_chips(_run_exchange(r_small.chips(), "grad_chip_exchange_small"))
    sharded = [r_top, r_bot, r_wap, r_wlp, r_wout, r_wg, r_wu, r_wd]
    everywhere = [r_wr, r_wi, r_small]
    reduced = _run_exchange(_FinishExchange([r.total for r in sharded + everywhere],
                                            [False] * len(sharded) + [True] * len(everywhere)), "grad_finish_exchange")
    grads = dict(zip(BIG[1:], reduced[2:len(sharded)]))
    grads["w_in"] = jnp.concatenate(reduced[:2], axis=0)
    grads["w_rgate"], grads["w_igate"] = reduced[len(sharded)], reduced[len(sharded) + 1]
    small_full = dict(zip(PACKED, _unpack(reduced[-1], gshapes)))
    per = d_rnn // N_CHIPS
    small_full["conv_w"] = lax.dynamic_slice(small_full["conv_w"], (0, chip * per), (taps, per))
    grads.update(small_full)

    delta, new_m, new_v = {}, {}, {}
    for n in BIG + ("w_rgate", "w_igate"):
        as2d = (lambda a: a[0]) if n in BIG else (lambda a: a.reshape(n_blocks * bw, bw))
        delta[n], new_m[n], new_v[n] = _adamw(as2d(w[n]), grads[n], as2d(mom[n]), as2d(var[n]), "adamw_" + n)
    pshapes = [w[n].shape for n in PACKED]
    prows = _pack_rows(pshapes)
    pk = [_pack([src[n] for n in PACKED], prows) for src in (w, grads, mom, var)]
    for res, packed in zip((delta, new_m, new_v), _adamw(pk[0], pk[1], pk[2], pk[3], "adamw_small")):
        res.update(dict(zip(PACKED, _unpack(packed, pshapes))))

    outs = [loss, dx.reshape(x.shape)]
    for res in (grads, delta, new_m, new_v):
        outs += [res[n].reshape(w[n].shape) for n in WEIGHTS]
    return tuple(outs)


def _gather_small(shard, name):
    def body(s_ref, o_ref, send_sems, recv_sems):
        e = _Env((s_ref,), (o_ref,), send_sems, recv_sems)
        o_ref[e.me] = s_ref[...]
        for k, (cx, cy) in enumerate(e.chips):
            e.copy(s_ref, o_ref.at[e.me], k, (cx, cy, e.c)).start()
        for k, (cx, cy) in enumerate(e.chips):
            e.copy(s_ref, o_ref.at[2 * cx + cy], k, (cx, cy, e.c)).wait_recv()
        for k, (cx, cy) in enumerate(e.chips):
            e.copy(s_ref, o_ref.at[e.me], k, (cx, cy, e.c)).wait_send()

    vm = pl.BlockSpec(memory_space=pltpu.VMEM)
    return pl.pallas_call(body, name=name, out_shape=jax.ShapeDtypeStruct((N_CHIPS,) + shard.shape, shard.dtype),
                          in_specs=[vm], out_specs=vm,
                          scratch_shapes=[pltpu.SemaphoreType.DMA((3,)), pltpu.SemaphoreType.DMA((3,))])(shard)
```

```python
import functools
import math

import jax
import jax.numpy as jnp
from jax import lax
from jax.experimental import pallas as pl
from jax.experimental.pallas import tpu as pltpu

F32 = jnp.float32
BF16 = jnp.bfloat16
MESH = pl.DeviceIdType.MESH

WINDOW = 128
BLK = 128
ROPE_THETA = 500000.0
LRU_C = 8.0
EPS = 1e-6
NEG = -1e30
ADAM_LR = 0.001
ADAM_B1 = 0.9
ADAM_B2 = 0.999
ADAM_EPS = 1e-08
ADAM_WD = 0.01
ADAM_STEP = 10

VMEM_LIMIT_BYTES = 52 * 1024 * 1024
LANE = 128
SUBLANE = 8
N_CHIPS = 4
SMALL_PACK_COLS = 512


def _params(**kw):
    return pltpu.CompilerParams(vmem_limit_bytes=VMEM_LIMIT_BYTES, **kw)


def _pick(dim, cands):
    for c in cands:
        if dim % c == 0:
            return c
    return dim


def _sigmoid(x):
    return 1.0 / (1.0 + jnp.exp(-x))


ANY = pl.BlockSpec(memory_space=pl.ANY)


class _Env:
    def __init__(self, ins, outs, send, recv, sem0=0, place=None):
        self.ins, self.outs, self.send, self.recv, self.sem0 = ins, outs, send, recv, sem0
        self.x, self.y, self.c = place or (lax.axis_index("x"), lax.axis_index("y"), lax.axis_index("c"))
        self.me = 2 * self.x + self.y
        self.chips = [(1 - self.x, self.y), (self.x, 1 - self.y), (1 - self.x, 1 - self.y)]
        self.sibling = (self.x, self.y, 1 - self.c)

    def sub(self, i0, n_in, o0, n_out, sem0):
        return _Env(self.ins[i0:i0 + n_in], self.outs[o0:o0 + n_out], self.send, self.recv, self.sem0 + sem0,
                    (self.x, self.y, self.c))

    def copy(self, src, dst, sem, to):
        return pltpu.make_async_remote_copy(src_ref=src, dst_ref=dst, send_sem=self.send.at[self.sem0 + sem],
                                            recv_sem=self.recv.at[self.sem0 + sem], device_id=to, device_id_type=MESH)


class _Exchange:
    inputs, out_shapes, aliases, n_sems = (), (), {}, 0

    def start(self, e):
        raise NotImplementedError

    def finish(self, e):
        raise NotImplementedError


class _Jobs(_Exchange):
    def __init__(self, *jobs):
        self.jobs, self.inputs, self.out_shapes, self.aliases, self.n_sems, self.at = jobs, [], [], {}, 0, []
        for job in jobs:
            self.at.append((len(self.inputs), len(self.out_shapes), self.n_sems))
            self.aliases.update({len(self.inputs) + i: len(self.out_shapes) + o for i, o in job.aliases.items()})
            self.inputs += list(job.inputs)
            self.out_shapes += list(job.out_shapes)
            self.n_sems += job.n_sems

    def _each(self, e):
        for job, (i0, o0, s0) in zip(self.jobs, self.at):
            yield job, e.sub(i0, len(job.inputs), o0, len(job.out_shapes), s0)

    def split(self, outs):
        return [tuple(outs[o0:o0 + len(job.out_shapes)]) for job, (_, o0, _) in zip(self.jobs, self.at)]

    def start(self, e):
        for job, se in self._each(e):
            job.start(se)

    def finish(self, e):
        for job, se in self._each(e):
            job.finish(se)


def _call(body, name, out_shape, grid, in_specs, out_specs, args, scratch_shapes=(), job=None, aliases=None):
    aliases = dict(aliases or {})
    if job is None:
        return pl.pallas_call(body, name=name, out_shape=out_shape, grid=grid, in_specs=list(in_specs),
                              out_specs=out_specs, scratch_shapes=list(scratch_shapes), input_output_aliases=aliases,
                              compiler_params=_params())(*args), ()
    single = not isinstance(out_shape, (tuple, list))
    shapes = [out_shape] if single else list(out_shape)
    ospecs = [out_specs] if single else list(out_specs)
    n_in, n_out, n_scr = len(args), len(shapes), len(scratch_shapes)
    j_in, j_out = len(job.inputs), len(job.out_shapes)

    def hosted(*refs):
        ins, jins = refs[:n_in], refs[n_in:n_in + j_in]
        outs = refs[n_in + j_in:n_in + j_in + n_out]
        jouts = refs[n_in + j_in + n_out:n_in + j_in + n_out + j_out]
        rest = refs[n_in + j_in + n_out + j_out:]
        e = _Env(jins, jouts, rest[n_scr], rest[n_scr + 1])
        first = functools.reduce(jnp.logical_and, [pl.program_id(d) == 0 for d in range(len(grid))])
        last = functools.reduce(jnp.logical_and, [pl.program_id(d) == g - 1 for d, g in enumerate(grid)])

        @pl.when(first)
        def _():
            job.start(e)

        body(*ins, *outs, *rest[:n_scr])

        @pl.when(last)
        def _():
            job.finish(e)

    res = pl.pallas_call(
        hosted, name=name, out_shape=tuple(shapes + list(job.out_shapes)), grid=grid,
        in_specs=list(in_specs) + [ANY] * j_in, out_specs=tuple(ospecs + [ANY] * j_out),
        scratch_shapes=list(scratch_shapes) + [pltpu.SemaphoreType.DMA((job.n_sems,)),
                                               pltpu.SemaphoreType.DMA((job.n_sems,))],
        input_output_aliases={**aliases, **{n_in + i: n_out + o for i, o in job.aliases.items()}},
        compiler_params=_params())(*args, *job.inputs)
    return (res[0] if single else tuple(res[:n_out])), tuple(res[n_out:])


def _run_exchange(job, name):
    n_in, n_out = len(job.inputs), len(job.out_shapes)

    def body(*refs):
        e = _Env(refs[:n_in], refs[n_in:n_in + n_out], refs[n_in + n_out], refs[n_in + n_out + 1])
        job.start(e)
        job.finish(e)

    return pl.pallas_call(
        body, name=name, out_shape=tuple(job.out_shapes), in_specs=[ANY] * n_in, out_specs=tuple([ANY] * n_out),
        input_output_aliases=dict(job.aliases),
        scratch_shapes=[pltpu.SemaphoreType.DMA((job.n_sems,)), pltpu.SemaphoreType.DMA((job.n_sems,))],
    )(*job.inputs)


_M_TILES = (1024, 1408, 1280, 512, 256, 128)
_N_TILES = (1408, 1280, 1024, 640, 512, 256, 128)
MXU_FULL_ROWS = 1024
MATMUL_VMEM_BUDGET = 42 * 1024 * 1024
MXU_FLOPS_PER_HBM_BYTE = 500


def _matmul_tiles(m, n, k, sa, sb, so, has_add):
    best = None
    for tm in [c for c in _M_TILES if m % c == 0] or [m]:
        for tn in [c for c in _N_TILES if n % c == 0] or [n]:
            for nk in range(1, 17):
                tk = k // nk
                if k % nk or tk % LANE:
                    continue
                need = 2 * (tm * tk * sa + tk * tn * sb) + 2 * tm * tn * (so + (4 if has_add else 0))
                need += tm * tn * 4 if nk > 1 else 0
                fetched = tk * tn * sb + tm * tk * sa // (1 if nk > 1 else n // tn)
                if need > MATMUL_VMEM_BUDGET:
                    continue
                mxu_bound = fetched * MXU_FLOPS_PER_HBM_BYTE <= 2 * tm * tn * tk
                key = (mxu_bound, min(tm, MXU_FULL_ROWS), -nk, tn, tm)
                if best is None or key > best[0]:
                    best = (key, (tm, tn, tk))
    assert best is not None, (m, n, k)
    return best[1]


def _matmul(a, b, mode, name, add=None, out_dtype=F32, job=None, m_window=None, into=None):
    if mode == "nn":
        (m, k), (k2, n) = a.shape, b.shape
    elif mode == "nt":
        (m, k), (n, k2) = a.shape, b.shape
    else:
        (k, m), (k2, n) = a.shape, b.shape
    assert k == k2, (a.shape, b.shape, mode)
    m0, m = m_window or (0, m)
    tm, tn, tk = _matmul_tiles(math.gcd(m, m0) if m0 else m, n, k, a.dtype.itemsize, b.dtype.itemsize,
                               jnp.dtype(out_dtype).itemsize, add is not None)
    nk, mb0 = k // tk, m0 // tm
    if mode == "nn":
        a_spec = pl.BlockSpec((tm, tk), lambda i, j, kk: (mb0 + i, kk))
        b_spec = pl.BlockSpec((tk, tn), lambda i, j, kk: (kk, j))
        dims = (((1,), (0,)), ((), ()))
    elif mode == "nt":
        a_spec = pl.BlockSpec((tm, tk), lambda i, j, kk: (mb0 + i, kk))
        b_spec = pl.BlockSpec((tn, tk), lambda i, j, kk: (j, kk))
        dims = (((1,), (1,)), ((), ()))
    else:
        a_spec = pl.BlockSpec((tk, tm), lambda i, j, kk: (kk, mb0 + i))
        b_spec = pl.BlockSpec((tk, tn), lambda i, j, kk: (kk, j))
        dims = (((0,), (0,)), ((), ()))
    out_rows, ob0 = (into[1], mb0) if into is not None else (m, 0)
    o_spec = pl.BlockSpec((tm, tn), lambda i, j, kk: (ob0 + i, j))
    has_add = add is not None
    begun = into is not None and into[0] is not None

    def body(*refs):
        a_ref, b_ref = refs[:2]
        add_ref = refs[2] if has_add else None
        part = lax.dot_general(a_ref[...].astype(BF16), b_ref[...].astype(BF16), dims, preferred_element_type=F32)
        if nk == 1:
            o_ref = refs[-1]
            o_ref[...] = (part + add_ref[...] if has_add else part).astype(out_dtype)
            return
        o_ref, acc = refs[-2:]
        kk = pl.program_id(2)

        @pl.when(kk == 0)
        def _():
            acc[...] = part

        @pl.when(kk > 0)
        def _():
            acc[...] += part

        @pl.when(kk == nk - 1)
        def _():
            r = acc[...]
            if has_add:
                r = r + add_ref[...]
            o_ref[...] = r.astype(out_dtype)

    in_specs = [a_spec, b_spec] + ([pl.BlockSpec((tm, tn), lambda i, j, kk: (mb0 + i, j))] if has_add else [])
    args = (a, b) + ((add,) if has_add else ())
    aliases = None
    if begun:
        aliases = {len(args): 0}
        in_specs, args = in_specs + [ANY], args + (into[0],)
    res, extra = _call(body, name, jax.ShapeDtypeStruct((out_rows, n), out_dtype), (m // tm, n // tn, nk), in_specs,
                       o_spec, args, [pltpu.VMEM((tm, tn), F32)] if nk > 1 else [], job, aliases)
    return res if job is None else (res, extra)


def _row_tile(rows, cols, budget_elems=512 * 1024):
    cands = [c for c in (1024, 704, 512, 352, 256, 128, 64, 32, 16) if c * cols <= budget_elems]
    return _pick(rows, cands or (16,))


_EW_COLS = (1280, 1408, 1024, 640, 512, 256, 128)


def _tile2d(rows, cols, max_elems):
    tc = _pick(cols, _EW_COLS)
    return _row_tile(rows, tc, max_elems), tc


def _rms_fwd(x, g, name):
    t, d = x.shape
    tr = _row_tile(t, d)

    def body(x_ref, g_ref, o_ref):
        xv = x_ref[...]
        rstd = lax.rsqrt(jnp.mean(xv * xv, axis=-1, keepdims=True) + EPS)
        o_ref[...] = (xv * rstd * g_ref[...]).astype(BF16)

    spec = pl.BlockSpec((tr, d), lambda i: (i, 0))
    return pl.pallas_call(body, name=name, out_shape=jax.ShapeDtypeStruct((t, d), BF16), grid=(t // tr,),
                          in_specs=[spec, pl.BlockSpec((1, d), lambda i: (0, 0))], out_specs=spec,
                          compiler_params=_params())(x, g)


def _rms_bwd(dxn, x, g, resid, name, job=None, mxu_copy=False):
    t, d = x.shape
    tr = _row_tile(t, d, 256 * 1024)

    def body(dxn_ref, x_ref, g_ref, r_ref, dx_ref, dg_ref, *dx16_ref):
        @pl.when(pl.program_id(0) == 0)
        def _():
            dg_ref[...] = jnp.zeros_like(dg_ref)

        xv = x_ref[...]
        rstd = lax.rsqrt(jnp.mean(xv * xv, axis=-1, keepdims=True) + EPS)
        xhat = xv * rstd
        dy = dxn_ref[...]
        dg_ref[...] += jnp.sum(dy * xhat, axis=0, keepdims=True)
        dxhat = dy * g_ref[...]
        dx = r_ref[...] + rstd * (dxhat - xhat * jnp.mean(dxhat * xhat, axis=-1, keepdims=True))
        dx_ref[...] = dx
        if mxu_copy:
            dx16_ref[0][...] = dx.astype(BF16)

    spec = pl.BlockSpec((tr, d), lambda i: (i, 0))
    vec = pl.BlockSpec((1, d), lambda i: (0, 0))
    shapes = (jax.ShapeDtypeStruct((t, d), F32), jax.ShapeDtypeStruct((1, d), F32))
    shapes += (jax.ShapeDtypeStruct((t, d), BF16),) if mxu_copy else ()
    res, extra = _call(body, name, shapes, (t // tr,), [spec, spec, vec, spec],
                       (spec, vec) + ((spec,) if mxu_copy else ()), (dxn, x, g, resid), (), job)
    return res if job is None else (res, extra)


def _swiglu_fwd(gate, up, name):
    t, f = gate.shape
    tr, tc = _tile2d(t, f, 1024 * 1024)

    def body(g_ref, u_ref, o_ref):
        gv = g_ref[...]
        o_ref[...] = (gv * _sigmoid(gv) * u_ref[...]).astype(BF16)

    spec = pl.BlockSpec((tr, tc), lambda i, j: (i, j))
    return pl.pallas_call(body, name=name, out_shape=jax.ShapeDtypeStruct((t, f), BF16), grid=(t // tr, f // tc),
                          in_specs=[spec, spec], out_specs=spec, compiler_params=_params())(gate, up)


def _swiglu_bwd(dact, gate, up, name, job=None):
    t, f = gate.shape
    tr, tc = _tile2d(t, f, 768 * 1024)

    def body(d_ref, g_ref, u_ref, dg_ref, du_ref):
        gv, dv = g_ref[...], d_ref[...]
        sg = _sigmoid(gv)
        dg_ref[...] = (dv * u_ref[...] * (sg * (1.0 + gv * (1.0 - sg)))).astype(BF16)
        du_ref[...] = (dv * (gv * sg)).astype(BF16)

    spec = pl.BlockSpec((tr, tc), lambda i, j: (i, j))
    shp = jax.ShapeDtypeStruct((t, f), BF16)
    res, extra = _call(body, name, (shp, shp), (t // tr, f // tc), [spec, spec, spec], (spec, spec), (dact, gate, up),
                       (), job)
    return res if job is None else (res, extra)


def _merge_fwd(z, b_gates, pa, plru, ga_off, name):
    t, d = pa.shape
    cw = _pick(math.gcd(ga_off, d), (512, 256, 128))
    tr = _row_tile(t, cw, 256 * 1024)
    oa, ol, nd = ga_off // cw, (ga_off + d) // cw, d // cw

    def body(ga_ref, gl_ref, ba_ref, bl_ref, pa_ref, pl_ref, o_ref):
        sa = _sigmoid(ga_ref[...] + ba_ref[...])
        sl = _sigmoid(gl_ref[...] + bl_ref[...])
        o_ref[...] = (sa * pa_ref[...] + sl * pl_ref[...]).astype(BF16)

    blk = pl.BlockSpec((tr, cw), lambda i, j: (i, j))
    return pl.pallas_call(
        body, name=name, out_shape=jax.ShapeDtypeStruct((t, d), BF16), grid=(t // tr, nd),
        in_specs=[pl.BlockSpec((tr, cw), lambda i, j: (i, oa + j)), pl.BlockSpec((tr, cw), lambda i, j: (i, ol + j)),
                  pl.BlockSpec((1, cw), lambda i, j: (0, j)), pl.BlockSpec((1, cw), lambda i, j: (0, nd + j)),
                  blk, blk],
        out_specs=blk, compiler_params=_params(),
    )(z, z, b_gates, b_gates, pa, plru)


def _merge_bwd(dmerged, z, b_gates, pa, plru, ga_off, name, job=None):
    t, d = pa.shape
    cw = _pick(math.gcd(ga_off, d), (512, 256, 128))
    tr = _row_tile(t, cw, 256 * 1024)
    oa, ol, nd = ga_off // cw, (ga_off + d) // cw, d // cw

    def body(dm_ref, ga_ref, gl_ref, ba_ref, bl_ref, pa_ref, pl_ref, dpa_ref, dpl_ref, dga_ref, dgl_ref, sa_ref, sl_ref):
        @pl.when(pl.program_id(1) == 0)
        def _():
            sa_ref[...] = jnp.zeros_like(sa_ref)
            sl_ref[...] = jnp.zeros_like(sl_ref)

        dm = dm_ref[...]
        sa = _sigmoid(ga_ref[...] + ba_ref[...])
        sl = _sigmoid(gl_ref[...] + bl_ref[...])
        dpa_ref[...] = (dm * sa).astype(BF16)
        dpl_ref[...] = (dm * sl).astype(BF16)
        dga = dm * pa_ref[...] * (sa * (1.0 - sa))
        dgl = dm * pl_ref[...] * (sl * (1.0 - sl))
        dga_ref[...] = dga.astype(BF16)
        dgl_ref[...] = dgl.astype(BF16)
        sa_ref[...] += jnp.sum(dga, axis=0, keepdims=True)
        sl_ref[...] += jnp.sum(dgl, axis=0, keepdims=True)

    blk = pl.BlockSpec((tr, cw), lambda j, i: (i, j))
    vec = pl.BlockSpec((1, cw), lambda j, i: (0, j))
    big16, v32 = jax.ShapeDtypeStruct((t, d), BF16), jax.ShapeDtypeStruct((1, d), F32)
    res, extra = _call(
        body, name, (big16, big16, big16, big16, v32, v32), (nd, t // tr),
        [blk, pl.BlockSpec((tr, cw), lambda j, i: (i, oa + j)), pl.BlockSpec((tr, cw), lambda j, i: (i, ol + j)),
         vec, pl.BlockSpec((1, cw), lambda j, i: (0, nd + j)), blk, blk],
        (blk, blk, blk, blk, vec, vec), (dmerged, z, z, b_gates, b_gates, pa, plru), (), job)
    return res if job is None else (res, extra)


def _loss_head(y, target, name):
    t, d = y.shape
    tr = _row_tile(t, d, 256 * 1024)
    nt = t // tr

    def body(y_ref, t_ref, dy_ref, dy16_ref, loss_ref, acc):
        i = pl.program_id(0)

        @pl.when(i == 0)
        def _():
            acc[...] = jnp.zeros_like(acc)

        e = y_ref[...] - t_ref[...]
        dy = e * (1.0 / d)
        dy_ref[...] = dy
        dy16_ref[...] = dy.astype(BF16)
        acc[...] += jnp.sum(e * e, axis=0, keepdims=True)

        @pl.when(i == nt - 1)
        def _():
            loss_ref[...] = (0.5 / d) * jnp.sum(acc[...], axis=-1, keepdims=True)

    spec = pl.BlockSpec((tr, d), lambda i: (i, 0))
    return pl.pallas_call(
        body, name=name, out_shape=(jax.ShapeDtypeStruct((t, d), F32), jax.ShapeDtypeStruct((t, d), BF16),
                                    jax.ShapeDtypeStruct((1, 1), F32)),
        grid=(nt,), in_specs=[spec, spec], out_specs=(spec, spec, pl.BlockSpec((1, 1), lambda i: (0, 0))),
        scratch_shapes=[pltpu.VMEM((1, d), F32)], compiler_params=_params(),
    )(y, target)


def _adamw(w, g, m, v, name):
    r, c = w.shape
    tr, tc = _tile2d(r, c, 512 * 1024)
    c1 = 1.0 - ADAM_B1 ** ADAM_STEP
    c2 = 1.0 - ADAM_B2 ** ADAM_STEP

    def body(w_ref, g_ref, m_ref, v_ref, d_ref, nm_ref, nv_ref):
        gv = g_ref[...]
        mn = ADAM_B1 * m_ref[...] + (1.0 - ADAM_B1) * gv
        vn = ADAM_B2 * v_ref[...] + (1.0 - ADAM_B2) * (gv * gv)
        d_ref[...] = -ADAM_LR * ((mn / c1) / (jnp.sqrt(vn / c2) + ADAM_EPS) + ADAM_WD * w_ref[...])
        nm_ref[...] = mn
        nv_ref[...] = vn

    spec = pl.BlockSpec((tr, tc), lambda i, j: (i, j))
    shp = jax.ShapeDtypeStruct((r, c), F32)
    return pl.pallas_call(body, name=name, out_shape=(shp, shp, shp), grid=(r // tr, c // tc), in_specs=[spec] * 4,
                          out_specs=(spec, spec, spec), compiler_params=_params())(w, g, m, v)


def _swap_halves(v, lane, half):
    n = v.shape[-1]
    return jnp.where(lane < half, pltpu.roll(v, n - half, 1),
                     jnp.where(lane < 2 * half, pltpu.roll(v, half, 1), 0.0))


def _norm_fwd(xraw, g):
    rstd = lax.rsqrt(jnp.mean(xraw * xraw, axis=-1, keepdims=True) + EPS)
    xhat = xraw * rstd
    return xhat, rstd, xhat * g


def _norm_bwd(dy, xhat, rstd, g):
    dxhat = dy * g
    dx = rstd * (dxhat - xhat * jnp.mean(dxhat * xhat, axis=-1, keepdims=True))
    return dx, jnp.sum(dy * xhat, axis=0, keepdims=True)


def _attn_specs(nb, grp, hd, kv, clamp):
    qo, ko, vo = 0, (kv * grp), (kv * grp + kv)
    cur = (lambda i: jnp.minimum(i, nb - 1)) if clamp else (lambda i: i)
    prev = lambda i: jnp.maximum(cur(i) - 1, 0)
    zq = pl.BlockSpec((BLK, grp * hd), lambda h, i: (cur(i), h))
    kc = pl.BlockSpec((BLK, hd), lambda h, i: (cur(i), ko + h))
    kp = pl.BlockSpec((BLK, hd), lambda h, i: (prev(i), ko + h))
    vc = pl.BlockSpec((BLK, hd), lambda h, i: (cur(i), vo + h))
    vp = pl.BlockSpec((BLK, hd), lambda h, i: (prev(i), vo + h))
    tc = pl.BlockSpec((BLK, hd), lambda h, i: (cur(i), 0))
    tp = pl.BlockSpec((BLK, hd), lambda h, i: (prev(i), 0))
    gs = pl.BlockSpec((1, hd), lambda h, i: (0, 0))
    return zq, kc, kp, vc, vp, tc, tp, gs


def _attn_mask(i):
    qi = lax.broadcasted_iota(jnp.int32, (BLK, 2 * BLK), 0)
    kj = lax.broadcasted_iota(jnp.int32, (BLK, 2 * BLK), 1)
    rel = qi + BLK - kj
    return (rel >= 0) & (rel < WINDOW) & ((kj >= BLK) | (i > 0))


def _attn_fwd(z, cos_t, sin_t, qg, kg, sinks, kv, grp, hd, name, job=None):
    t = z.shape[0]
    nb = t // BLK
    half = hd // 8
    scale = 1.0 / math.sqrt(hd)
    zq, kc, kp, vc, vp, tc, tp, gs = _attn_specs(nb, grp, hd, kv, False)

    def body(sink_ref, zq_ref, kc_ref, kp_ref, vc_ref, vp_ref, cc_ref, sc_ref, cp_ref, sp_ref, qg_ref, kg_ref, o_ref):
        h, i = pl.program_id(0), pl.program_id(1)
        lane = lax.broadcasted_iota(jnp.int32, (BLK, hd), 1)

        def normrope(xraw, g, c, s):
            y = _norm_fwd(xraw, g)[2]
            return y * c + _swap_halves(y, lane, half) * s

        cc, sc = cc_ref[...], sc_ref[...]
        kcur = normrope(kc_ref[...], kg_ref[...], cc, sc)
        kprev = normrope(kp_ref[...], kg_ref[...], cp_ref[...], sp_ref[...])
        kk = jnp.concatenate([kprev, kcur], axis=0).astype(BF16)
        vv = jnp.concatenate([vp_ref[...], vc_ref[...]], axis=0).astype(BF16)
        mask = _attn_mask(i)
        for g in range(grp):
            q = normrope(zq_ref[:, g * hd:(g + 1) * hd], qg_ref[...], cc, sc).astype(BF16)
            s = lax.dot_general(q, kk, (((1,), (1,)), ((), ())), preferred_element_type=F32) * scale
            s = jnp.where(mask, s, NEG)
            sk = sink_ref[h * grp + g]
            mx = jnp.maximum(jnp.max(s, axis=-1, keepdims=True), sk)
            p = jnp.exp(s - mx)
            den = jnp.sum(p, axis=-1, keepdims=True) + jnp.exp(sk - mx)
            p = p / den
            o_ref[:, g * hd:(g + 1) * hd] = jnp.dot(p.astype(BF16), vv, preferred_element_type=F32).astype(BF16)

    res, extra = _call(
        body, name, jax.ShapeDtypeStruct((t, kv * grp * hd), BF16), (kv, nb),
        [pl.BlockSpec(memory_space=pltpu.SMEM), zq, kc, kp, vc, vp, tc, tc, tp, tp, gs, gs],
        pl.BlockSpec((BLK, grp * hd), lambda h, i: (i, h)),
        (sinks, z, z, z, z, z, cos_t, sin_t, cos_t, sin_t, qg, kg), (), job)
    return res if job is None else (res, extra)


def _attn_bwd(dattn, z, cos_t, sin_t, qg, kg, sinks, kv, grp, hd, name):
    t = z.shape[0]
    nb = t // BLK
    half = hd // 8
    scale = 1.0 / math.sqrt(hd)
    zq, kc, kp, vc, vp, tc, tp, gs = _attn_specs(nb, grp, hd, kv, True)

    def body(sink_ref, zq_ref, kc_ref, kp_ref, vc_ref, vp_ref, cc_ref, sc_ref, cp_ref, sp_ref, qg_ref, kg_ref, do_ref,
             dq_ref, dk_ref, dv_ref, dqg_ref, dkg_ref, dsk_ref, dk_carry, dv_carry):
        h, i = pl.program_id(0), pl.program_id(1)
        lane = lax.broadcasted_iota(jnp.int32, (BLK, hd), 1)
        lane1 = lax.broadcasted_iota(jnp.int32, (1, LANE), 1)

        @pl.when((h == 0) & (i == 0))
        def _():
            dqg_ref[...] = jnp.zeros_like(dqg_ref)
            dkg_ref[...] = jnp.zeros_like(dkg_ref)
            dsk_ref[...] = jnp.zeros_like(dsk_ref)

        @pl.when(i == 0)
        def _():
            dk_carry[...] = jnp.zeros_like(dk_carry)
            dv_carry[...] = jnp.zeros_like(dv_carry)

        def rope(y, c, s):
            return y * c + _swap_halves(y, lane, half) * s

        def rope_bwd(dout, c, s):
            return dout * c + _swap_halves(dout * s, lane, half)

        @pl.when(i < nb)
        def _():
            cc, sc, cp, sp = cc_ref[...], sc_ref[...], cp_ref[...], sp_ref[...]
            qgv, kgv = qg_ref[...], kg_ref[...]
            xh_kc, rs_kc, y_kc = _norm_fwd(kc_ref[...], kgv)
            xh_kp, rs_kp, y_kp = _norm_fwd(kp_ref[...], kgv)
            kk = jnp.concatenate([rope(y_kp, cp, sp), rope(y_kc, cc, sc)], axis=0).astype(BF16)
            vv = jnp.concatenate([vp_ref[...], vc_ref[...]], axis=0).astype(BF16)
            mask = _attn_mask(i)
            dkk = jnp.zeros((2 * BLK, hd), F32)
            dvv = jnp.zeros((2 * BLK, hd), F32)
            dqg = jnp.zeros((1, hd), F32)
            dsk = jnp.zeros((1, LANE), F32)
            for g in range(grp):
                xh_q, rs_q, y_q = _norm_fwd(zq_ref[:, g * hd:(g + 1) * hd], qgv)
                q = rope(y_q, cc, sc).astype(BF16)
                s = lax.dot_general(q, kk, (((1,), (1,)), ((), ())), preferred_element_type=F32) * scale
                s = jnp.where(mask, s, NEG)
                sk = sink_ref[h * grp + g]
                mx = jnp.maximum(jnp.max(s, axis=-1, keepdims=True), sk)
                p = jnp.exp(s - mx)
                den = jnp.sum(p, axis=-1, keepdims=True) + jnp.exp(sk - mx)
                p = p / den
                psink = jnp.exp(sk - mx) / den
                dog = do_ref[:, g * hd:(g + 1) * hd].astype(BF16)
                dp = lax.dot_general(dog, vv, (((1,), (1,)), ((), ())), preferred_element_type=F32)
                rsum = jnp.sum(p * dp, axis=-1, keepdims=True)
                ds = (p * (dp - rsum) * scale).astype(BF16)
                dsk = dsk + jnp.where(lane1 == h * grp + g, jnp.sum(-psink * rsum, axis=0, keepdims=True), 0.0)
                dqn = jnp.dot(ds, kk, preferred_element_type=F32)
                dkk = dkk + lax.dot_general(ds, q, (((0,), (0,)), ((), ())), preferred_element_type=F32)
                dvv = dvv + lax.dot_general(p.astype(BF16), dog, (((0,), (0,)), ((), ())), preferred_element_type=F32)
                dxq, dg_q = _norm_bwd(rope_bwd(dqn, cc, sc), xh_q, rs_q, qgv)
                dq_ref[:, g * hd:(g + 1) * hd] = dxq.astype(BF16)
                dqg = dqg + dg_q
            dkp_raw, dg_kp = _norm_bwd(rope_bwd(dkk[:BLK], cp, sp), xh_kp, rs_kp, kgv)
            dkc_raw, dg_kc = _norm_bwd(rope_bwd(dkk[BLK:], cc, sc), xh_kc, rs_kc, kgv)
            dk_ref[...] = (dk_carry[...] + dkp_raw).astype(BF16)
            dv_ref[...] = (dv_carry[...] + dvv[:BLK]).astype(BF16)
            dk_carry[...] = dkc_raw
            dv_carry[...] = dvv[BLK:]
            dqg_ref[...] += dqg
            dkg_ref[...] += dg_kp + dg_kc
            dsk_ref[...] += dsk

        @pl.when(i == nb)
        def _():
            dk_ref[...] = dk_carry[...].astype(BF16)
            dv_ref[...] = dv_carry[...].astype(BF16)

    kvw = kv * hd
    vec = pl.BlockSpec((1, hd), lambda h, i: (0, 0))
    shifted = pl.BlockSpec((BLK, hd), lambda h, i: (jnp.maximum(i - 1, 0), h))
    return pl.pallas_call(
        body, name=name,
        out_shape=(jax.ShapeDtypeStruct((t, kv * grp * hd), BF16), jax.ShapeDtypeStruct((t, kvw), BF16),
                   jax.ShapeDtypeStruct((t, kvw), BF16), jax.ShapeDtypeStruct((1, hd), F32),
                   jax.ShapeDtypeStruct((1, hd), F32), jax.ShapeDtypeStruct((1, LANE), F32)),
        grid=(kv, nb + 1),
        in_specs=[pl.BlockSpec(memory_space=pltpu.SMEM), zq, kc, kp, vc, vp, tc, tc, tp, tp, gs, gs,
                  pl.BlockSpec((BLK, grp * hd), lambda h, i: (jnp.minimum(i, nb - 1), h))],
        out_specs=(pl.BlockSpec((BLK, grp * hd), lambda h, i: (jnp.minimum(i, nb - 1), h)), shifted, shifted, vec, vec,
                   pl.BlockSpec((1, LANE), lambda h, i: (0, 0))),
        scratch_shapes=[pltpu.VMEM((BLK, hd), F32), pltpu.VMEM((BLK, hd), F32)], compiler_params=_params(),
    )(sinks, z, z, z, z, z, cos_t, sin_t, cos_t, sin_t, qg, kg, dattn)


def _window(rows, cb, c0, row_of, col_of):
    assert rows % SUBLANE == 0 and cb % LANE == 0 and c0 % LANE == 0, (rows, cb, c0)
    return pl.BlockSpec((pl.Element(rows), pl.Element(cb)),
                        lambda *g: (pl.multiple_of(row_of(*g) * rows, SUBLANE), pl.multiple_of(c0 + col_of(*g) * cb, LANE)))


def _conv_fwd(z, c0, c, w, b, name):
    t = z.shape[0]
    taps = w.shape[0]
    cb = _pick(c, (1408, 1024, 512, 256, 128))
    tr = _row_tile(t, cb, 256 * 1024)
    hb = tr // SUBLANE

    def body(u_ref, halo_ref, w_ref, b_ref, o_ref):
        i = pl.program_id(0)
        x = u_ref[...]
        acc = b_ref[...] + w_ref[taps - 1:taps, :] * x
        for k in range(taps - 1):
            acc = acc + w_ref[k:k + 1, :] * pltpu.roll(x, taps - 1 - k, 0)
        o_ref[...] = acc
        row = lax.broadcasted_iota(jnp.int32, (SUBLANE, cb), 0)
        hp = jnp.where(i > 0, halo_ref[...], 0.0)
        x8 = u_ref[0:SUBLANE, :]
        acc8 = b_ref[...] + w_ref[taps - 1:taps, :] * x8
        for k in range(taps - 1):
            s = taps - 1 - k
            acc8 = acc8 + w_ref[k:k + 1, :] * jnp.where(row < s, pltpu.roll(hp, s, 0), pltpu.roll(x8, s, 0))
        o_ref[0:SUBLANE, :] = acc8

    blk = pl.BlockSpec((tr, cb), lambda i, j: (i, j))
    return pl.pallas_call(
        body, name=name, out_shape=jax.ShapeDtypeStruct((t, c), F32), grid=(t // tr, c // cb),
        in_specs=[_window(tr, cb, c0, lambda i, j: i, lambda i, j: j),
                  _window(SUBLANE, cb, c0, lambda i, j: jnp.maximum(i * hb - 1, 0), lambda i, j: j),
                  pl.BlockSpec((taps, cb), lambda i, j: (0, j)), pl.BlockSpec((1, cb), lambda i, j: (0, j))],
        out_specs=blk, compiler_params=_params(),
    )(z, z, w, b)


def _conv_bwd(duc, z, c0, w, name):
    t, c = duc.shape
    taps = w.shape[0]
    cb = _pick(c, (1408, 1024, 512, 256, 128))
    tr = _row_tile(t, cb, 256 * 1024)
    hb, nt = tr // SUBLANE, t // tr

    def body(g_ref, gnext_ref, u_ref, uprev_ref, w_ref, du16_ref, dw_ref, db_ref, du_ref):
        i = pl.program_id(1)

        @pl.when(i == 0)
        def _():
            dw_ref[...] = jnp.zeros_like(dw_ref)
            db_ref[...] = jnp.zeros_like(db_ref)

        row = lax.broadcasted_iota(jnp.int32, (SUBLANE, cb), 0)
        g, x = g_ref[...], u_ref[...]
        du = w_ref[taps - 1:taps, :] * g
        for k in range(taps - 1):
            du = du + w_ref[k:k + 1, :] * pltpu.roll(g, tr - (taps - 1 - k), 0)
        du_ref[...] = du
        hn = jnp.where(i < nt - 1, gnext_ref[...], 0.0)
        g8 = g_ref[tr - SUBLANE:tr, :]
        du8 = w_ref[taps - 1:taps, :] * g8
        for k in range(taps - 1):
            s = taps - 1 - k
            du8 = du8 + w_ref[k:k + 1, :] * jnp.where(row >= SUBLANE - s, pltpu.roll(hn, SUBLANE - s, 0),
                                                     pltpu.roll(g8, SUBLANE - s, 0))
        du_ref[tr - SUBLANE:tr, :] = du8
        du16_ref[...] = du_ref[...].astype(BF16)

        hp = jnp.where(i > 0, uprev_ref[...], 0.0)
        xl8, gf8 = u_ref[tr - SUBLANE:tr, :], g_ref[0:SUBLANE, :]
        db_ref[...] += jnp.sum(g, axis=0, keepdims=True)
        dw_ref[taps - 1:taps, :] += jnp.sum(g * x, axis=0, keepdims=True)
        for k in range(taps - 1):
            s = taps - 1 - k
            fix = jnp.where(row < s, pltpu.roll(hp, s, 0) - pltpu.roll(xl8, s, 0), 0.0)
            dw_ref[k:k + 1, :] += (jnp.sum(g * pltpu.roll(x, s, 0), axis=0, keepdims=True)
                                   + jnp.sum(gf8 * fix, axis=0, keepdims=True))

    blk = pl.BlockSpec((tr, cb), lambda j, i: (i, j))
    nh = t // SUBLANE
    return pl.pallas_call(
        body, name=name,
        out_shape=(jax.ShapeDtypeStruct((t, c), BF16), jax.ShapeDtypeStruct((taps, c), F32),
                   jax.ShapeDtypeStruct((1, c), F32)),
        grid=(c // cb, nt),
        in_specs=[blk, pl.BlockSpec((SUBLANE, cb), lambda j, i: (jnp.minimum((i + 1) * hb, nh - 1), j)),
                  _window(tr, cb, c0, lambda j, i: i, lambda j, i: j),
                  _window(SUBLANE, cb, c0, lambda j, i: jnp.maximum(i * hb - 1, 0), lambda j, i: j),
                  pl.BlockSpec((taps, cb), lambda j, i: (0, j))],
        out_specs=(blk, pl.BlockSpec((taps, cb), lambda j, i: (0, j)), pl.BlockSpec((1, cb), lambda j, i: (0, j))),
        scratch_shapes=[pltpu.VMEM((tr, cb), F32)], compiler_params=_params(),
    )(duc, duc, z, z, w)


def _gates_fwd(uc, wr, wi, gw, name):
    t, c = uc.shape
    n, bw, _ = wr.shape
    per, ng = gw // bw, c // gw
    tr = _pick(t, (512, 256, 128))

    def body(u_ref, wr_ref, wi_ref, r_ref, i_ref):
        for b in range(per):
            cols = slice(b * bw, (b + 1) * bw)
            a = u_ref[:, cols].astype(BF16)
            r_ref[:, cols] = jnp.dot(a, wr_ref[b].astype(BF16), preferred_element_type=F32)
            i_ref[:, cols] = jnp.dot(a, wi_ref[b].astype(BF16), preferred_element_type=F32)

    blk = pl.BlockSpec((tr, gw), lambda h, i: (i, h))
    wsp = pl.BlockSpec((per, bw, bw), lambda h, i: (h, 0, 0))
    shp = jax.ShapeDtypeStruct((t, c), F32)
    return pl.pallas_call(body, name=name, out_shape=(shp, shp), grid=(ng, t // tr), in_specs=[blk, wsp, wsp],
                          out_specs=(blk, blk), compiler_params=_params())(uc, wr, wi)


def _gates_bwd_x(duc, drp, dip, wr, wi, gw, name):
    t, c = duc.shape
    n, bw, _ = wr.shape
    per, ng = gw // bw, c // gw
    tr = _pick(t, (512, 256, 128))
    dims = (((1,), (1,)), ((), ()))

    def body(d_ref, r_ref, i_ref, wr_ref, wi_ref, o_ref):
        for b in range(per):
            cols = slice(b * bw, (b + 1) * bw)
            o_ref[:, cols] = (
                d_ref[:, cols]
                + lax.dot_general(r_ref[:, cols].astype(BF16), wr_ref[b].astype(BF16), dims, preferred_element_type=F32)
                + lax.dot_general(i_ref[:, cols].astype(BF16), wi_ref[b].astype(BF16), dims, preferred_element_type=F32))

    blk = pl.BlockSpec((tr, gw), lambda h, i: (i, h))
    wsp = pl.BlockSpec((per, bw, bw), lambda h, i: (h, 0, 0))
    return pl.pallas_call(body, name=name, out_shape=jax.ShapeDtypeStruct((t, c), F32), grid=(ng, t // tr),
                          in_specs=[blk, blk, blk, wsp, wsp], out_specs=blk, compiler_params=_params())(duc, drp, dip, wr, wi)


def _gates_bwd_w(uc, dpre, n, bw, gw, name):
    t, c = uc.shape
    per, ng = gw // bw, c // gw
    tk = _pick(t, (512, 256, 128))
    dims = (((0,), (0,)), ((), ()))

    def body(u_ref, d_ref, o_ref):
        @pl.when(pl.program_id(1) == 0)
        def _():
            o_ref[...] = jnp.zeros_like(o_ref)

        for b in range(per):
            cols = slice(b * bw, (b + 1) * bw)
            o_ref[b] += lax.dot_general(u_ref[:, cols].astype(BF16), d_ref[:, cols].astype(BF16), dims,
                                        preferred_element_type=F32)

    blk = pl.BlockSpec((tk, gw), lambda h, i: (i, h))
    return pl.pallas_call(body, name=name, out_shape=jax.ShapeDtypeStruct((n, bw, bw), F32), grid=(ng, t // tk),
                          in_specs=[blk, blk], out_specs=pl.BlockSpec((per, bw, bw), lambda h, i: (h, 0, 0)),
                          compiler_params=_params())(uc, dpre)


def _softplus(x):
    return jnp.maximum(x, 0.0) + jnp.log(1.0 + jnp.exp(-jnp.abs(x)))


def _neg_expm1(x):
    series = x * (1.0 + x * (0.5 + x * (1.0 / 6.0 + x * (1.0 / 24.0 + x * (1.0 / 120.0)))))
    return -jnp.where(x > -0.05, series, jnp.exp(x) - 1.0)


_GELU_C = math.sqrt(2.0 / math.pi)


def _gelu_parts(x):
    inner = _GELU_C * (x + 0.044715 * (x * x * x))
    th = jnp.tanh(inner)
    gelu = 0.5 * x * (1.0 + th)
    dgelu = 0.5 * (1.0 + th) + 0.5 * x * (1.0 - th * th) * (_GELU_C * (1.0 + 3.0 * 0.044715 * (x * x)))
    return gelu, dgelu


def _lru_gate_values(uc, rpre, ipre, br, bi, sp):
    r = _sigmoid(rpre + br)
    ig = _sigmoid(ipre + bi)
    a = jnp.exp(-LRU_C * r * sp)
    mult = jnp.sqrt(jnp.maximum(_neg_expm1(2.0 * (-LRU_C * r * sp)), 0.0))
    return r, ig, a, mult


def _lru_fwd(uc, rpre, ipre, z, gr0, br, bi, lam, name, job=None):
    t, c = uc.shape
    cb = _pick(c, (1408, 1024, 512, 256, 128))
    tb = _pick(t, (512, 256, 128))
    ntile = tb // SUBLANE

    def body(uc_ref, r_ref, i_ref, gr_ref, br_ref, bi_ref, lam_ref, h_ref, rec16_ref, carry, rec_ref):
        @pl.when(pl.program_id(1) == 0)
        def _():
            carry[...] = jnp.zeros_like(carry)

        sp = _softplus(-lam_ref[...])
        br, bi = br_ref[...], bi_ref[...]
        row = lax.broadcasted_iota(jnp.int32, (SUBLANE, cb), 0)

        def tile(k, c_in):
            sl = pl.ds(pl.multiple_of(k * SUBLANE, SUBLANE), SUBLANE)
            ucv = uc_ref[sl, :]
            _, ig, a, mult = _lru_gate_values(ucv, r_ref[sl, :], i_ref[sl, :], br, bi, sp)
            b = mult * (ig * ucv)
            for d in (1, 2, 4):
                a_s = jnp.where(row >= d, pltpu.roll(a, d, 0), 1.0)
                b_s = jnp.where(row >= d, pltpu.roll(b, d, 0), 0.0)
                b = a * b_s + b
                a = a * a_s
            hv = b + a * c_in
            h_ref[sl, :] = hv
            rec_ref[sl, :] = hv * _gelu_parts(gr_ref[sl, :])[0]
            return hv[SUBLANE - 1:SUBLANE, :]

        c_out = lax.fori_loop(0, ntile, tile, carry[0:1, :])
        carry[...] = jnp.broadcast_to(c_out, (SUBLANE, cb))
        rec16_ref[...] = rec_ref[...].astype(BF16)

    blk = pl.BlockSpec((tb, cb), lambda j, i: (i, j))
    vec = pl.BlockSpec((1, cb), lambda j, i: (0, j))
    res, extra = _call(body, name, (jax.ShapeDtypeStruct((t, c), F32), jax.ShapeDtypeStruct((t, c), BF16)),
                       (c // cb, t // tb),
                       [blk, blk, blk, _window(tb, cb, gr0, lambda j, i: i, lambda j, i: j), vec, vec, vec], (blk, blk),
                       (uc, rpre, ipre, z, br, bi, lam), [pltpu.VMEM((SUBLANE, cb), F32), pltpu.VMEM((tb, cb), F32)], job)
    return res if job is None else (res, extra)


def _lru_bwd(drec, hst, uc, rpre, ipre, z, gr0, br, bi, lam, name, job=None):
    t, c = uc.shape
    cb = _pick(c, (1408, 1024, 512, 256, 128))
    tb = _pick(t, (256, 128))
    ntile, nt, hb = tb // SUBLANE, t // tb, tb // SUBLANE

    def body(drec_ref, h_ref, hprev_ref, uc_ref, r_ref, i_ref, gr_ref, br_ref, bi_ref, lam_ref,
             dgr16_ref, drp_ref, dip_ref, duc_ref, dlam_ref, dbr_ref, dbi_ref, carry, dgr_ref):
        step = pl.program_id(1)
        first_block = step == nt - 1

        @pl.when(step == 0)
        def _():
            carry[...] = jnp.zeros_like(carry)
            dlam_ref[...] = jnp.zeros_like(dlam_ref)
            dbr_ref[...] = jnp.zeros_like(dbr_ref)
            dbi_ref[...] = jnp.zeros_like(dbi_ref)

        lam = lam_ref[...]
        sp = _softplus(-lam)
        br, bi = br_ref[...], bi_ref[...]
        row = lax.broadcasted_iota(jnp.int32, (SUBLANE, cb), 0)
        halo = jnp.where(first_block, 0.0, hprev_ref[...])

        def tile(kk, state):
            c_p, acc_sp, acc_br, acc_bi = state
            k = ntile - 1 - kk
            sl = pl.ds(pl.multiple_of(k * SUBLANE, SUBLANE), SUBLANE)
            slp = pl.ds(pl.multiple_of(jnp.maximum(k - 1, 0) * SUBLANE, SUBLANE), SUBLANE)
            ucv = uc_ref[sl, :]
            r, ig, a, mult = _lru_gate_values(ucv, r_ref[sl, :], i_ref[sl, :], br, bi, sp)
            hv = h_ref[sl, :]
            below = jnp.where(k > 0, h_ref[slp, :], halo)
            hprev = jnp.where(row == 0, pltpu.roll(below, 1, 0), pltpu.roll(hv, 1, 0))
            gelu, dgelu = _gelu_parts(gr_ref[sl, :])
            drec = drec_ref[sl, :]
            dh = drec * gelu
            dgr_ref[sl, :] = drec * hv * dgelu
            pa, pb = a, a * dh
            for d in (1, 2, 4):
                a_s = jnp.where(row < SUBLANE - d, pltpu.roll(pa, SUBLANE - d, 0), 1.0)
                b_s = jnp.where(row < SUBLANE - d, pltpu.roll(pb, SUBLANE - d, 0), 0.0)
                pb = pa * b_s + pb
                pa = pa * a_s
            pv = pb + pa * c_p
            gt = dh + jnp.where(row == SUBLANE - 1, c_p, pltpu.roll(pv, SUBLANE - 1, 0))
            da = gt * hprev
            duc_ref[sl, :] = gt * mult * ig
            dmult = gt * ig * ucv
            dig = gt * mult * ucv
            dla = da * a - jnp.where(mult > 0.0, dmult * (a * a) / mult, 0.0)
            drp = dla * (-LRU_C * sp) * (r * (1.0 - r))
            dip = dig * (ig * (1.0 - ig))
            drp_ref[sl, :] = drp
            dip_ref[sl, :] = dip
            return pv[0:1, :], acc_sp + dla * (-LRU_C * r), acc_br + drp, acc_bi + dip

        zero = jnp.zeros((SUBLANE, cb), F32)
        c_out, acc_sp, acc_br, acc_bi = lax.fori_loop(0, ntile, tile, (carry[0:1, :], zero, zero, zero))
        carry[...] = jnp.broadcast_to(c_out, (SUBLANE, cb))
        dlam_ref[...] += jnp.sum(acc_sp, axis=0, keepdims=True) * (-_sigmoid(-lam))
        dbr_ref[...] += jnp.sum(acc_br, axis=0, keepdims=True)
        dbi_ref[...] += jnp.sum(acc_bi, axis=0, keepdims=True)
        dgr16_ref[...] = dgr_ref[...].astype(BF16)

    blk = pl.BlockSpec((tb, cb), lambda j, i: (nt - 1 - i, j))
    vec = pl.BlockSpec((1, cb), lambda j, i: (0, j))
    halo_spec = pl.BlockSpec((SUBLANE, cb), lambda j, i: (jnp.maximum((nt - 1 - i) * hb - 1, 0), j))
    big, small = jax.ShapeDtypeStruct((t, c), F32), jax.ShapeDtypeStruct((1, c), F32)
    res, extra = _call(
        body, name, (jax.ShapeDtypeStruct((t, c), BF16), big, big, big, small, small, small), (c // cb, nt),
        [blk, blk, halo_spec, blk, blk, blk, _window(tb, cb, gr0, lambda j, i: nt - 1 - i, lambda j, i: j),
         vec, vec, vec], (blk, blk, blk, blk, vec, vec, vec),
        (drec, hst, hst, uc, rpre, ipre, z, br, bi, lam),
        [pltpu.VMEM((SUBLANE, cb), F32), pltpu.VMEM((tb, cb), F32)], job)
    return res if job is None else (res, extra)


def _shard_region(ref, kind, chip, half, rh, width):
    if kind == "col":
        return ref.at[pl.ds(half * rh, rh), pl.ds(chip * width, width)]
    return ref.at[pl.ds(chip * (2 * rh) + half * rh, rh), :]


class _AllGather(_Exchange):
    def __init__(self, fulls, kinds):
        self.inputs, self.kinds = list(fulls), kinds
        self.out_shapes = [jax.ShapeDtypeStruct(f.shape, f.dtype) for f in fulls]
        self.aliases = {a: a for a in range(len(fulls))}
        self.n_sems = 6 * len(fulls)
        self.geo = [(f.shape[0] // 2, f.shape[1] // N_CHIPS) if k == "col" else (f.shape[0] // (2 * N_CHIPS), f.shape[1])
                    for f, k in zip(fulls, kinds)]

    def _region(self, ref, a, chip, half):
        return _shard_region(ref, self.kinds[a], chip, half, *self.geo[a])

    def _ici(self, e, a, k, chip):
        cx, cy = e.chips[k]
        return e.copy(self._region(e.ins[a], a, chip, e.c), self._region(e.outs[a], a, chip, e.c), a * 6 + k,
                      (cx, cy, e.c))

    def _d2d(self, e, a, k, half):
        cx, cy = e.chips[k]
        region = self._region(e.outs[a], a, 2 * cx + cy, half)
        return e.copy(region, region, a * 6 + 3 + k, e.sibling)

    def start(self, e):
        for a in range(len(self.inputs)):
            for k in range(3):
                self._ici(e, a, k, e.me).start()

    def finish(self, e):
        n = len(self.inputs)
        for a in range(n):
            for k, (cx, cy) in enumerate(e.chips):
                self._ici(e, a, k, 2 * cx + cy).wait_recv()
                self._d2d(e, a, k, e.c).start()
        for a in range(n):
            for k in range(3):
                self._d2d(e, a, k, 1 - e.c).wait_recv()
        for a in range(n):
            for k in range(3):
                self._ici(e, a, k, e.me).wait_send()
                self._d2d(e, a, k, e.c).wait_send()


class _SiblingExchange(_Exchange):
    def __init__(self, grads):
        self.inputs = list(grads)
        self.out_shapes = [jax.ShapeDtypeStruct((g.shape[0],) + g.shape[2:], g.dtype) for g in grads]
        self.n_sems = len(grads)

    def _copy(self, e, a):
        return e.copy(e.ins[a].at[:, 1 - e.c], e.outs[a], a, e.sibling)

    def start(self, e):
        for a in range(len(self.inputs)):
            self._copy(e, a).start()

    def finish(self, e):
        for a in range(len(self.inputs)):
            self._copy(e, a).wait()


def _piece(ref, kind, chip, width):
    if kind == "col":
        return ref.at[0, :, pl.ds(chip * width, width)]
    return ref.at[chip]


class _ChipExchange(_Exchange):
    def __init__(self, sums, kinds):
        self.inputs, self.kinds = list(sums), kinds
        self.widths = [s.shape[2] // N_CHIPS if k == "col" else s.shape[2] for s, k in zip(sums, kinds)]
        self.out_shapes = [jax.ShapeDtypeStruct((3, s.shape[1], w), s.dtype) for s, w in zip(sums, self.widths)]
        self.n_sems = 3 * len(sums)

    def _copy(self, e, a, k, chip):
        cx, cy = e.chips[k]
        return e.copy(_piece(e.ins[a], self.kinds[a], chip, self.widths[a]), e.outs[a].at[k], a * 3 + k, (cx, cy, e.c))

    def start(self, e):
        for a in range(len(self.inputs)):
            for k, (cx, cy) in enumerate(e.chips):
                self._copy(e, a, k, 2 * cx + cy).start()

    def finish(self, e):
        for a in range(len(self.inputs)):
            for k, (cx, cy) in enumerate(e.chips):
                self._copy(e, a, k, 2 * cx + cy).wait()


class _FinishExchange(_Exchange):
    def __init__(self, finals, to_all):
        self.inputs, self.to_all = list(finals), list(to_all)
        self.out_shapes = [jax.ShapeDtypeStruct(f.shape, f.dtype) for f in finals]
        self.aliases = {a: a for a in range(len(finals))}
        self.first_sem, self.n_sems = [], 0
        for all8 in self.to_all:
            self.first_sem.append(self.n_sems)
            self.n_sems += 7 if all8 else 1
        self.rel = [(fx, fy, fc) for fx in (0, 1) for fy in (0, 1) for fc in (0, 1)][1:]

    def _copies(self, e, mine):
        for a, all8 in enumerate(self.to_all):
            src = e.ins[a] if mine else e.outs[a]
            if not all8:
                rh = self.inputs[a].shape[0] // 2
                rows = pl.ds((e.c if mine else 1 - e.c) * rh, rh)
                yield e.copy(src.at[rows, :], e.outs[a].at[rows, :], self.first_sem[a], e.sibling)
                continue
            rh = self.inputs[a].shape[0] // (2 * N_CHIPS)
            for r, (fx, fy, fc) in enumerate(self.rel):
                px, py, pc = (1 - e.x if fx else e.x), (1 - e.y if fy else e.y), (1 - e.c if fc else e.c)
                rows = pl.ds(((2 * e.me + e.c) if mine else (2 * (2 * px + py) + pc)) * rh, rh)
                yield e.copy(src.at[rows, :], e.outs[a].at[rows, :], self.first_sem[a] + r, (px, py, pc))

    def start(self, e):
        for cp in self._copies(e, True):
            cp.start()

    def finish(self, e):
        for cp in self._copies(e, False):
            cp.wait_recv()
        for cp in self._copies(e, True):
            cp.wait_send()


def _cast_into_full(w, kind, idx, name):
    r, c = w.shape
    tr = _row_tile(r, c)
    nrb = r // tr

    def body(idx_ref, w_ref, o_ref):
        o_ref[...] = w_ref[...].astype(BF16)

    if kind == "col":
        full, out_map = (r, N_CHIPS * c), (lambda i, idx_ref: (i, idx_ref[1]))
    else:
        full, out_map = (N_CHIPS * r, c), (lambda i, idx_ref: (idx_ref[1] * nrb + i, 0))
    return pl.pallas_call(
        body, name=name, out_shape=jax.ShapeDtypeStruct(full, BF16),
        grid_spec=pltpu.PrefetchScalarGridSpec(
            num_scalar_prefetch=1, grid=(nrb,), in_specs=[pl.BlockSpec((tr, c), lambda i, idx_ref: (i, 0))],
            out_specs=pl.BlockSpec((tr, c), out_map)),
        compiler_params=_params(),
    )(idx, w)


def _matmul_gathering(a, placed, order, name):
    t, k = a.shape
    n = placed.shape[1]
    w = n // N_CHIPS
    tm, tn = _pick(t, _M_TILES), _pick(w, _N_TILES)
    ni, nj = t // tm, w // tn
    per_shard, total = ni * nj, N_CHIPS * ni * nj
    gather = _AllGather([placed], ["col"])

    def body(ord_ref, a_ref, w_own_ref, o_ref, w_ref, wbuf, fetch_sem, send, recv):
        s, i, j = pl.program_id(0), pl.program_id(1), pl.program_id(2)
        step = (s * ni + i) * nj + j
        e = _Env((w_own_ref,), (w_ref,), send, recv)

        def fetch(src, st):
            col = pl.multiple_of((ord_ref[st // per_shard] * nj + st % nj) * tn, LANE)
            return pltpu.make_async_copy(src.at[:, pl.ds(col, tn)], wbuf.at[st % 2], fetch_sem.at[st % 2])

        @pl.when(step == 0)
        def _():
            gather.start(e)
            fetch(w_own_ref, step).start()

        nxt = step + 1
        for kk, (cx, cy) in enumerate(e.chips):
            @pl.when(nxt == (kk + 1) * per_shard)
            def _():
                gather._ici(e, 0, kk, 2 * cx + cy).wait_recv()
                gather._d2d(e, 0, kk, e.c).start()
                gather._d2d(e, 0, kk, 1 - e.c).wait_recv()

        @pl.when(nxt < per_shard)
        def _():
            fetch(w_own_ref, nxt).start()

        @pl.when((nxt >= per_shard) & (nxt < total))
        def _():
            fetch(w_ref, nxt).start()

        fetch(w_ref, step).wait()
        o_ref[...] = jnp.dot(a_ref[...], wbuf[step % 2], preferred_element_type=F32)

        @pl.when(step == total - 1)
        def _():
            for kk in range(3):
                gather._ici(e, 0, kk, e.me).wait_send()
                gather._d2d(e, 0, kk, e.c).wait_send()

    z, full = pl.pallas_call(
        body, name=name, out_shape=(jax.ShapeDtypeStruct((t, n), F32), jax.ShapeDtypeStruct(placed.shape, placed.dtype)),
        grid_spec=pltpu.PrefetchScalarGridSpec(
            num_scalar_prefetch=1, grid=(N_CHIPS, ni, nj),
            in_specs=[pl.BlockSpec((tm, k), lambda s, i, j, ord_ref: (i, 0)), ANY],
            out_specs=(pl.BlockSpec((tm, tn), lambda s, i, j, ord_ref: (i, ord_ref[s] * nj + j)), ANY),
            scratch_shapes=[pltpu.VMEM((2, k, tn), placed.dtype), pltpu.SemaphoreType.DMA((2,)),
                            pltpu.SemaphoreType.DMA((gather.n_sems,)), pltpu.SemaphoreType.DMA((gather.n_sems,))]),
        input_output_aliases={2: 1}, compiler_params=_params(),
    )(order, a, placed)
    return z, full


def _add_own_half(g4, recv, idx, out_dtype, name):
    p, _, rh, n = g4.shape
    tr, tc = _tile2d(rh, n, 1024 * 1024)

    def body(idx_ref, g_ref, r_ref, o_ref):
        o_ref[...] = (g_ref[...] + r_ref[...]).astype(out_dtype)

    return pl.pallas_call(
        body, name=name, out_shape=jax.ShapeDtypeStruct((p, rh, n), out_dtype),
        grid_spec=pltpu.PrefetchScalarGridSpec(
            num_scalar_prefetch=1, grid=(p, rh // tr, n // tc),
            in_specs=[pl.BlockSpec((None, None, tr, tc), lambda q, i, j, idx_ref: (q, idx_ref[0], i, j)),
                      pl.BlockSpec((None, tr, tc), lambda q, i, j, idx_ref: (q, i, j))],
            out_specs=pl.BlockSpec((None, tr, tc), lambda q, i, j, idx_ref: (q, i, j))),
        compiler_params=_params(),
    )(idx, g4, recv)


def _sum_chips(own, kind, parts, idx, slots, to_all, name):
    _, rh, w = parts.shape
    tr, tc = _tile2d(rh, w, 512 * 1024)
    nrb, ncb = rh // tr, w // tc

    def body(idx_ref, own_ref, p0, p1, p2, o_ref):
        o_ref[...] = ((own_ref[...].astype(F32) + p0[...].astype(F32)) + p1[...].astype(F32)) + p2[...].astype(F32)

    if kind == "col":
        own_spec = pl.BlockSpec((None, tr, tc), lambda i, j, idx_ref: (0, i, idx_ref[1] * ncb + j))
    else:
        own_spec = pl.BlockSpec((None, tr, tc), lambda i, j, idx_ref: (idx_ref[1], i, j))
    if to_all:
        out_map = lambda i, j, idx_ref: ((2 * idx_ref[1] + idx_ref[0]) * nrb + i, j)
    else:
        out_map = lambda i, j, idx_ref: (idx_ref[0] * nrb + i, j)

    def part(k):
        return pl.BlockSpec((None, tr, tc), lambda i, j, idx_ref: (k, i, j))

    return pl.pallas_call(
        body, name=name, out_shape=jax.ShapeDtypeStruct((slots * rh, w), F32),
        grid_spec=pltpu.PrefetchScalarGridSpec(
            num_scalar_prefetch=1, grid=(nrb, ncb), in_specs=[own_spec, part(0), part(1), part(2)],
            out_specs=pl.BlockSpec((tr, tc), out_map)),
        compiler_params=_params(),
    )(idx, own, parts, parts, parts)


class _Reduce:
    def __init__(self, name, g, kind, idx, wire, to_all):
        r, c = g.shape
        self.name, self.kind, self.idx, self.wire, self.to_all = name, kind, idx, wire, to_all
        self.view = g.reshape(1, 2, r // 2, c) if kind == "col" else g.reshape(N_CHIPS, 2, r // (2 * N_CHIPS), c)

    def sibling(self):
        return _SiblingExchange([self.view])

    def got_sibling(self, outs):
        self.sum = _add_own_half(self.view, outs[0], self.idx, self.wire, "grad_chip_sum_" + self.name)

    def chips(self):
        return _ChipExchange([self.sum], [self.kind])

    def got_chips(self, outs):
        self.total = _sum_chips(self.sum, self.kind, outs[0], self.idx, 2 * N_CHIPS if self.to_all else 2,
                                self.to_all, "grad_total_" + self.name)


def _pack(arrays, rows):
    flat = jnp.concatenate([a.reshape(-1) for a in arrays])
    return jnp.pad(flat, (0, rows * SMALL_PACK_COLS - flat.shape[0])).reshape(rows, SMALL_PACK_COLS)


def _unpack(packed, shapes):
    flat = packed.reshape(-1)
    out, o = [], 0
    for shp in shapes:
        size = math.prod(shp)
        out.append(flat[o:o + size].reshape(shp))
        o += size
    return out


def _pack_rows(shapes):
    total = sum(math.prod(s) for s in shapes)
    unit = SMALL_PACK_COLS * N_CHIPS * 2 * SUBLANE
    return -(-total // unit) * (N_CHIPS * 2 * SUBLANE)


BIG = ("w_in", "w_attn_proj", "w_lru_proj", "w_out", "w_ffn_gate", "w_ffn_up", "w_ffn_down")
BIG_KIND = {"w_in": "col", "w_attn_proj": "row", "w_lru_proj": "row", "w_out": "row", "w_ffn_gate": "col",
            "w_ffn_up": "col", "w_ffn_down": "row"}
SMALL = ("norm1_g", "b_gates", "q_norm_g", "k_norm_g", "sinks", "conv_w", "conv_b", "w_rgate", "b_rgate",
         "w_igate", "b_igate", "lru_lambda", "norm2_g")
PACKED = tuple(n for n in SMALL if n not in ("w_rgate", "w_igate"))
WEIGHTS = ("norm1_g", "w_in", "b_gates", "q_norm_g", "k_norm_g", "sinks", "conv_w", "conv_b", "w_rgate", "b_rgate",
           "w_igate", "b_igate", "lru_lambda", "w_attn_proj", "w_lru_proj", "w_out", "norm2_g", "w_ffn_gate",
           "w_ffn_up", "w_ffn_down")


def kernel(x, positions, norm1_g, w_in, b_gates, q_norm_g, k_norm_g, sinks, conv_w, conv_b, w_rgate, b_rgate, w_igate, b_igate, lru_lambda, w_attn_proj, w_lru_proj, w_out, norm2_g, w_ffn_gate, w_ffn_up, w_ffn_down, loss_target, m_norm1_g, m_w_in, m_b_gates, m_q_norm_g, m_k_norm_g, m_sinks, m_conv_w, m_conv_b, m_w_rgate, m_b_rgate, m_w_igate, m_b_igate, m_lru_lambda, m_w_attn_proj, m_w_lru_proj, m_w_out, m_norm2_g, m_w_ffn_gate, m_w_ffn_up, m_w_ffn_down, v_norm1_g, v_w_in, v_b_gates, v_q_norm_g, v_k_norm_g, v_sinks, v_conv_w, v_conv_b, v_w_rgate, v_b_rgate, v_w_igate, v_b_igate, v_lru_lambda, v_w_attn_proj, v_w_lru_proj, v_w_out, v_norm2_g, v_w_ffn_gate, v_w_ffn_up, v_w_ffn_down):
    args = dict(locals())
    w = {n: args[n] for n in WEIGHTS}
    mom = {n: args["m_" + n] for n in WEIGHTS}
    var = {n: args["v_" + n] for n in WEIGHTS}

    t, d = x.shape[1], x.shape[2]
    hd = q_norm_g.shape[-1]
    nq = sinks.shape[-1]
    q_w = nq * hd
    d_rnn = conv_b.shape[-1]
    taps = conv_w.shape[1]
    n_blocks, bw = w_rgate.shape[1], w_rgate.shape[2]
    in_w = w_in.shape[-1] * N_CHIPS
    kv_w = (in_w - q_w - 2 * d_rnn - 2 * d) // 2
    kv = kv_w // hd
    grp = nq // kv
    u_off = q_w + 2 * kv_w
    gr_off = u_off + d_rnn
    ga_off = gr_off + d_rnn
    gw = bw * LANE // math.gcd(bw, LANE)
    chip = 2 * lax.axis_index("x") + lax.axis_index("y")
    idx = jnp.stack([lax.axis_index("c"), chip]).astype(jnp.int32)

    x2, tgt = x[0], loss_target[0]

    placed = {n: _cast_into_full(w[n][0], BIG_KIND[n], idx, "cast_" + n) for n in BIG}

    def gather(*names):
        return _AllGather([placed[n] for n in names], [BIG_KIND[n] for n in names])

    mx, my = lax.axis_index("x"), lax.axis_index("y")
    order = jnp.stack([chip, 2 * (1 - mx) + my, 2 * mx + (1 - my), 2 * (1 - mx) + (1 - my)]).astype(jnp.int32)
    conv_w_full = _gather_small(conv_w[0], "allgather_conv_w")
    conv_w_full = jnp.transpose(conv_w_full, (1, 0, 2)).reshape(taps, d_rnn)

    inv_freq = ROPE_THETA ** (-jnp.arange(0, hd // 4, 2, dtype=F32) / (hd // 4))
    ang = positions[0].astype(F32)[:, None] * inv_freq
    cos, sin = jnp.cos(ang), jnp.sin(ang)
    rest = hd - 2 * cos.shape[1]
    cos_t = jnp.concatenate([cos, cos, jnp.ones((t, rest), F32)], axis=1)
    sin_t = jnp.concatenate([-sin, sin, jnp.zeros((t, rest), F32)], axis=1)
    sinks1 = sinks[0]

    xn = _rms_fwd(x2, norm1_g, "rms1_fwd")
    z, win_f = _matmul_gathering(xn, placed["w_in"], order, "in_proj")
    attn, (wap_f, wlp_f, wout_f) = _attn_fwd(z, cos_t, sin_t, q_norm_g, k_norm_g, sinks1, kv, grp, hd, "attn_fwd",
                                             job=gather("w_attn_proj", "w_lru_proj", "w_out"))
    uc = _conv_fwd(z, u_off, d_rnn, conv_w_full, conv_b, "conv_fwd")
    rpre, ipre = _gates_fwd(uc, w_rgate[0], w_igate[0], gw, "gates_fwd")
    (hst, rec), (wg_f,) = _lru_fwd(uc, rpre, ipre, z, gr_off, b_rgate, b_igate, lru_lambda, "lru_fwd",
                                   job=gather("w_ffn_gate"))
    pa = _matmul(attn, wap_f, "nn", "attn_proj")
    plru = _matmul(rec, wlp_f, "nn", "lru_proj")
    merged = _merge_fwd(z, b_gates, pa, plru, ga_off, "merge_fwd")
    h1 = _matmul(merged, wout_f, "nn", "out_proj", add=x2)
    hn = _rms_fwd(h1, norm2_g, "rms2_fwd")
    gate, (wu_f,) = _matmul(hn, wg_f, "nn", "ffn_gate", job=gather("w_ffn_up"))
    up, (wd_f,) = _matmul(hn, wu_f, "nn", "ffn_up", job=gather("w_ffn_down"))
    act = _swiglu_fwd(gate, up, "swiglu_fwd")
    yout = _matmul(act, wd_f, "nn", "ffn_down", add=h1)
    dy, dy16, loss_part = _loss_head(yout, tgt, "loss_head")
    loss = lax.psum(loss_part[0, 0], ("x", "y", "c"))

    def reduction(n, g):
        return _Reduce(n, g, BIG_KIND[n], idx, BF16, False)

    r_wd = reduction("w_ffn_down", _matmul(act, dy16, "tn", "d_w_ffn_down"))
    dact, got = _matmul(dy16, wd_f, "nt", "d_act", job=r_wd.sibling())
    r_wd.got_sibling(got)
    (dgate, dup), got = _swiglu_bwd(dact, gate, up, "swiglu_bwd", job=r_wd.chips())
    r_wd.got_chips(got)
    r_wg = reduction("w_ffn_gate", _matmul(hn, dgate, "tn", "d_w_ffn_gate"))
    g_wu, got = _matmul(hn, dup, "tn", "d_w_ffn_up", job=r_wg.sibling())
    r_wg.got_sibling(got)
    r_wu = reduction("w_ffn_up", g_wu)
    both = _Jobs(r_wu.sibling(), r_wg.chips())
    dhn, got = _matmul(dgate, wg_f, "nt", "d_hn_gate", job=both)
    got_wu, got_wg = both.split(got)
    r_wu.got_sibling(got_wu)
    r_wg.got_chips(got_wg)
    dhn, got = _matmul(dup, wu_f, "nt", "d_hn_up", add=dhn, job=r_wu.chips())
    r_wu.got_chips(got)
    dh1, g_norm2, dh1_16 = _rms_bwd(dhn, h1, norm2_g, dy, "rms2_bwd", mxu_copy=True)
    r_wout = reduction("w_out", _matmul(merged, dh1_16, "tn", "d_w_out"))
    dmerged, got = _matmul(dh1_16, wout_f, "nt", "d_merged", job=r_wout.sibling())
    r_wout.got_sibling(got)
    (dpa, dpl, dga, dgl, g_ba, g_bl), got = _merge_bwd(dmerged, z, b_gates, pa, plru, ga_off, "merge_bwd",
                                                       job=r_wout.chips())
    r_wout.got_chips(got)
    r_wap = reduction("w_attn_proj", _matmul(attn, dpa, "tn", "d_w_attn_proj"))
    dattn, got = _matmul(dpa, wap_f, "nt", "d_attn", job=r_wap.sibling())
    r_wap.got_sibling(got)
    g_wlp, got = _matmul(rec, dpl, "tn", "d_w_lru_proj", job=r_wap.chips())
    r_wap.got_chips(got)
    r_wlp = reduction("w_lru_proj", g_wlp)
    drec, got = _matmul(dpl, wlp_f, "nt", "d_rec", job=r_wlp.sibling())
    r_wlp.got_sibling(got)
    (dgr, drp, dip, duc_direct, g_lam, g_br, g_bi), got = _lru_bwd(
        drec, hst, uc, rpre, ipre, z, gr_off, b_rgate, b_igate, lru_lambda, "lru_bwd", job=r_wlp.chips())
    r_wlp.got_chips(got)
    duc = _gates_bwd_x(duc_direct, drp, dip, w_rgate[0], w_igate[0], gw, "gates_bwd_x")
    g_wr = _gates_bwd_w(uc, drp, n_blocks, bw, gw, "gates_bwd_wr")
    g_wi = _gates_bwd_w(uc, dip, n_blocks, bw, gw, "gates_bwd_wi")
    du, g_convw, g_convb = _conv_bwd(duc, z, u_off, conv_w_full, "conv_bwd")
    dq, dk, dv, g_qg, g_kg, g_sinks = _attn_bwd(dattn, z, cos_t, sin_t, q_norm_g, k_norm_g, sinks1, kv, grp, hd,
                                                 "attn_bwd")
    dz = jnp.concatenate([dq, dk, dv, du, dgr, dga, dgl], axis=1)
    r_wr = _Reduce("w_rgate", g_wr.reshape(n_blocks * bw, bw), "row", idx, F32, True)
    r_wi = _Reduce("w_igate", g_wi.reshape(n_blocks * bw, bw), "row", idx, F32, True)
    both = _Jobs(r_wr.sibling(), r_wi.sibling())
    g_top, got = _matmul(xn, dz, "tn", "d_w_in_top", m_window=(0, d // 2), job=both)
    got_wr, got_wi = both.split(got)
    r_wr.got_sibling(got_wr)
    r_wi.got_sibling(got_wi)
    r_top = _Reduce("w_in_top", g_top, "col", idx, BF16, False)
    three = _Jobs(r_top.sibling(), r_wr.chips(), r_wi.chips())
    g_bot, got = _matmul(xn, dz, "tn", "d_w_in_bot", m_window=(d // 2, d // 2), job=three)
    got_top, got_wr, got_wi = three.split(got)
    r_top.got_sibling(got_top)
    r_wr.got_chips(got_wr)
    r_wi.got_chips(got_wi)
    r_bot = _Reduce("w_in_bot", g_bot, "col", idx, BF16, False)
    both = _Jobs(r_top.chips(), r_bot.sibling())
    dxn, got = _matmul(dz, win_f, "nt", "d_xn_a", m_window=(0, t // 2), into=(None, t), job=both)
    got_top, got_bot = both.split(got)
    r_top.got_chips(got_top)
    r_bot.got_sibling(got_bot)
    dxn, got = _matmul(dz, win_f, "nt", "d_xn_b", m_window=(t // 2, t // 2), into=(dxn, t), job=r_bot.chips())
    r_bot.got_chips(got)
    dx, g_norm1 = _rms_bwd(dxn, x2, norm1_g, dh1, "rms1_bwd")

    small_grads = {"norm1_g": g_norm1, "b_gates": jnp.concatenate([g_ba, g_bl], axis=1), "q_norm_g": g_qg,
                   "k_norm_g": g_kg, "sinks": g_sinks[:, :nq], "conv_w": g_convw, "conv_b": g_convb,
                   "b_rgate": g_br, "b_igate": g_bi, "lru_lambda": g_lam, "norm2_g": g_norm2}
    gshapes = [small_grads[n].shape for n in PACKED]
    r_small = _Reduce("small", _pack([small_grads[n] for n in PACKED], _pack_rows(gshapes)), "row", idx, F32, True)
    r_small.got_sibling(_run_exchange(r_small.sibling(), "grad_sibling_exchange_small"))
    r_small.got_chips(_run_exchange(r_small.chips(), "grad_chip_exchange_small"))
    sharded = [r_top, r_bot, r_wap, r_wlp, r_wout, r_wg, r_wu, r_wd]
    everywhere = [r_wr, r_wi, r_small]
    reduced = _run_exchange(_FinishExchange([r.total for r in sharded + everywhere],
                                            [False] * len(sharded) + [True] * len(everywhere)), "grad_finish_exchange")
    grads = dict(zip(BIG[1:], reduced[2:len(sharded)]))
    grads["w_in"] = jnp.concatenate(reduced[:2], axis=0)
    grads["w_rgate"], grads["w_igate"] = reduced[len(sharded)], reduced[len(sharded) + 1]
    small_full = dict(zip(PACKED, _unpack(reduced[-1], gshapes)))
    per = d_rnn // N_CHIPS
    small_full["conv_w"] = lax.dynamic_slice(small_full["conv_w"], (0, chip * per), (taps, per))
    grads.update(small_full)

    delta, new_m, new_v = {}, {}, {}
    for n in BIG + ("w_rgate", "w_igate"):
        as2d = (lambda a: a[0]) if n in BIG else (lambda a: a.reshape(n_blocks * bw, bw))
        delta[n], new_m[n], new_v[n] = _adamw(as2d(w[n]), grads[n], as2d(mom[n]), as2d(var[n]), "adamw_" + n)
    pshapes = [w[n].shape for n in PACKED]
    prows = _pack_rows(pshapes)
    pk = [_pack([src[n] for n in PACKED], prows) for src in (w, grads, mom, var)]
    for res, packed in zip((delta, new_m, new_v), _adamw(pk[0], pk[1], pk[2], pk[3], "adamw_small")):
        res.update(dict(zip(PACKED, _unpack(packed, pshapes))))

    outs = [loss, dx.reshape(x.shape)]
    for res in (grads, delta, new_m, new_v):
        outs += [res[n].reshape(w[n].shape) for n in WEIGHTS]
    return tuple(outs)


def _gather_small(shard, name):
    def body(s_ref, o_ref, send_sems, recv_sems):
        e = _Env((s_ref,), (o_ref,), send_sems, recv_sems)
        o_ref[e.me] = s_ref[...]
        for k, (cx, cy) in enumerate(e.chips):
            e.copy(s_ref, o_ref.at[e.me], k, (cx, cy, e.c)).start()
        for k, (cx, cy) in enumerate(e.chips):
            e.copy(s_ref, o_ref.at[2 * cx + cy], k, (cx, cy, e.c)).wait_recv()
        for k, (cx, cy) in enumerate(e.chips):
            e.copy(s_ref, o_ref.at[e.me], k, (cx, cy, e.c)).wait_send()

    vm = pl.BlockSpec(memory_space=pltpu.VMEM)
    return pl.pallas_call(body, name=name, out_shape=jax.ShapeDtypeStruct((N_CHIPS,) + shard.shape, shard.dtype),
                          in_specs=[vm], out_specs=vm,
                          scratch_shapes=[pltpu.SemaphoreType.DMA((3,)), pltpu.SemaphoreType.DMA((3,))])(shard)
```

```python
import functools
import math

import jax
import jax.numpy as jnp
from jax import lax
from jax.experimental import pallas as pl
from jax.experimental.pallas import tpu as pltpu

F32 = jnp.float32
BF16 = jnp.bfloat16
MESH = pl.DeviceIdType.MESH

WINDOW = 128
BLK = 128
ROPE_THETA = 500000.0
LRU_C = 8.0
EPS = 1e-6
NEG = -1e30
ADAM_LR = 0.001
ADAM_B1 = 0.9
ADAM_B2 = 0.999
ADAM_EPS = 1e-08
ADAM_WD = 0.01
ADAM_STEP = 10

VMEM_LIMIT_BYTES = 52 * 1024 * 1024
LANE = 128
SUBLANE = 8
N_CHIPS = 4
SMALL_PACK_COLS = 512


def _params(**kw):
    return pltpu.CompilerParams(vmem_limit_bytes=VMEM_LIMIT_BYTES, **kw)


def _pick(dim, cands):
    for c in cands:
        if dim % c == 0:
            return c
    return dim


def _sigmoid(x):
    return 0.5 * jnp.tanh(0.5 * x) + 0.5


ANY = pl.BlockSpec(memory_space=pl.ANY)


class _Env:
    def __init__(self, ins, outs, send, recv, sem0=0, place=None):
        self.ins, self.outs, self.send, self.recv, self.sem0 = ins, outs, send, recv, sem0
        self.x, self.y, self.c = place or (lax.axis_index("x"), lax.axis_index("y"), lax.axis_index("c"))
        self.me = 2 * self.x + self.y
        self.chips = [(1 - self.x, self.y), (self.x, 1 - self.y), (1 - self.x, 1 - self.y)]
        self.sibling = (self.x, self.y, 1 - self.c)

    def sub(self, i0, n_in, o0, n_out, sem0):
        return _Env(self.ins[i0:i0 + n_in], self.outs[o0:o0 + n_out], self.send, self.recv, self.sem0 + sem0,
                    (self.x, self.y, self.c))

    def copy(self, src, dst, sem, to):
        return pltpu.make_async_remote_copy(src_ref=src, dst_ref=dst, send_sem=self.send.at[self.sem0 + sem],
                                            recv_sem=self.recv.at[self.sem0 + sem], device_id=to, device_id_type=MESH)


class _Exchange:
    inputs, out_shapes, aliases, n_sems = (), (), {}, 0

    def start(self, e):
        raise NotImplementedError

    def finish(self, e):
        raise NotImplementedError


class _Jobs(_Exchange):
    def __init__(self, *jobs):
        self.jobs, self.inputs, self.out_shapes, self.aliases, self.n_sems, self.at = jobs, [], [], {}, 0, []
        for job in jobs:
            self.at.append((len(self.inputs), len(self.out_shapes), self.n_sems))
            self.aliases.update({len(self.inputs) + i: len(self.out_shapes) + o for i, o in job.aliases.items()})
            self.inputs += list(job.inputs)
            self.out_shapes += list(job.out_shapes)
            self.n_sems += job.n_sems

    def _each(self, e):
        for job, (i0, o0, s0) in zip(self.jobs, self.at):
            yield job, e.sub(i0, len(job.inputs), o0, len(job.out_shapes), s0)

    def split(self, outs):
        return [tuple(outs[o0:o0 + len(job.out_shapes)]) for job, (_, o0, _) in zip(self.jobs, self.at)]

    def start(self, e):
        for job, se in self._each(e):
            job.start(se)

    def finish(self, e):
        for job, se in self._each(e):
            job.finish(se)


def _call(body, name, out_shape, grid, in_specs, out_specs, args, scratch_shapes=(), job=None, aliases=None):
    aliases = dict(aliases or {})
    if job is None:
        return pl.pallas_call(body, name=name, out_shape=out_shape, grid=grid, in_specs=list(in_specs),
                              out_specs=out_specs, scratch_shapes=list(scratch_shapes), input_output_aliases=aliases,
                              compiler_params=_params())(*args), ()
    single = not isinstance(out_shape, (tuple, list))
    shapes = [out_shape] if single else list(out_shape)
    ospecs = [out_specs] if single else list(out_specs)
    n_in, n_out, n_scr = len(args), len(shapes), len(scratch_shapes)
    j_in, j_out = len(job.inputs), len(job.out_shapes)

    def hosted(*refs):
        ins, jins = refs[:n_in], refs[n_in:n_in + j_in]
        outs = refs[n_in + j_in:n_in + j_in + n_out]
        jouts = refs[n_in + j_in + n_out:n_in + j_in + n_out + j_out]
        rest = refs[n_in + j_in + n_out + j_out:]
        e = _Env(jins, jouts, rest[n_scr], rest[n_scr + 1])
        first = functools.reduce(jnp.logical_and, [pl.program_id(d) == 0 for d in range(len(grid))])
        last = functools.reduce(jnp.logical_and, [pl.program_id(d) == g - 1 for d, g in enumerate(grid)])

        @pl.when(first)
        def _():
            job.start(e)

        body(*ins, *outs, *rest[:n_scr])

        @pl.when(last)
        def _():
            job.finish(e)

    res = pl.pallas_call(
        hosted, name=name, out_shape=tuple(shapes + list(job.out_shapes)), grid=grid,
        in_specs=list(in_specs) + [ANY] * j_in, out_specs=tuple(ospecs + [ANY] * j_out),
        scratch_shapes=list(scratch_shapes) + [pltpu.SemaphoreType.DMA((job.n_sems,)),
                                               pltpu.SemaphoreType.DMA((job.n_sems,))],
        input_output_aliases={**aliases, **{n_in + i: n_out + o for i, o in job.aliases.items()}},
        compiler_params=_params())(*args, *job.inputs)
    return (res[0] if single else tuple(res[:n_out])), tuple(res[n_out:])


def _run_exchange(job, name):
    n_in, n_out = len(job.inputs), len(job.out_shapes)

    def body(*refs):
        e = _Env(refs[:n_in], refs[n_in:n_in + n_out], refs[n_in + n_out], refs[n_in + n_out + 1])
        job.start(e)
        job.finish(e)

    return pl.pallas_call(
        body, name=name, out_shape=tuple(job.out_shapes), in_specs=[ANY] * n_in, out_specs=tuple([ANY] * n_out),
        input_output_aliases=dict(job.aliases),
        scratch_shapes=[pltpu.SemaphoreType.DMA((job.n_sems,)), pltpu.SemaphoreType.DMA((job.n_sems,))],
    )(*job.inputs)


_M_TILES = (1024, 1408, 1280, 512, 256, 128)
_N_TILES = (1408, 1280, 1024, 640, 512, 256, 128)
MXU_FULL_ROWS = 1024
MATMUL_VMEM_BUDGET = 42 * 1024 * 1024
MXU_FLOPS_PER_HBM_BYTE = 500


def _matmul_tiles(m, n, k, sa, sb, so, has_add):
    best = None
    for tm in [c for c in _M_TILES if m % c == 0] or [m]:
        for tn in [c for c in _N_TILES if n % c == 0] or [n]:
            for nk in range(1, 17):
                tk = k // nk
                if k % nk or tk % LANE:
                    continue
                need = 2 * (tm * tk * sa + tk * tn * sb) + 2 * tm * tn * (so + (4 if has_add else 0))
                need += tm * tn * 4 if nk > 1 else 0
                fetched = tk * tn * sb + tm * tk * sa // (1 if nk > 1 else n // tn)
                if need > MATMUL_VMEM_BUDGET:
                    continue
                mxu_bound = fetched * MXU_FLOPS_PER_HBM_BYTE <= 2 * tm * tn * tk
                key = (mxu_bound, min(tm, MXU_FULL_ROWS), -nk, tn, tm)
                if best is None or key > best[0]:
                    best = (key, (tm, tn, tk))
    assert best is not None, (m, n, k)
    return best[1]


def _matmul(a, b, mode, name, add=None, out_dtype=F32, job=None, m_window=None, into=None):
    if mode == "nn":
        (m, k), (k2, n) = a.shape, b.shape
    elif mode == "nt":
        (m, k), (n, k2) = a.shape, b.shape
    else:
        (k, m), (k2, n) = a.shape, b.shape
    assert k == k2, (a.shape, b.shape, mode)
    m0, m = m_window or (0, m)
    tm, tn, tk = _matmul_tiles(math.gcd(m, m0) if m0 else m, n, k, a.dtype.itemsize, b.dtype.itemsize,
                               jnp.dtype(out_dtype).itemsize, add is not None)
    nk, mb0 = k // tk, m0 // tm
    if mode == "nn":
        a_spec = pl.BlockSpec((tm, tk), lambda i, j, kk: (mb0 + i, kk))
        b_spec = pl.BlockSpec((tk, tn), lambda i, j, kk: (kk, j))
        dims = (((1,), (0,)), ((), ()))
    elif mode == "nt":
        a_spec = pl.BlockSpec((tm, tk), lambda i, j, kk: (mb0 + i, kk))
        b_spec = pl.BlockSpec((tn, tk), lambda i, j, kk: (j, kk))
        dims = (((1,), (1,)), ((), ()))
    else:
        a_spec = pl.BlockSpec((tk, tm), lambda i, j, kk: (kk, mb0 + i))
        b_spec = pl.BlockSpec((tk, tn), lambda i, j, kk: (kk, j))
        dims = (((0,), (0,)), ((), ()))
    out_rows, ob0 = (into[1], mb0) if into is not None else (m, 0)
    o_spec = pl.BlockSpec((tm, tn), lambda i, j, kk: (ob0 + i, j))
    has_add = add is not None
    begun = into is not None and into[0] is not None

    def body(*refs):
        a_ref, b_ref = refs[:2]
        add_ref = refs[2] if has_add else None
        part = lax.dot_general(a_ref[...].astype(BF16), b_ref[...].astype(BF16), dims, preferred_element_type=F32)
        if nk == 1:
            o_ref = refs[-1]
            o_ref[...] = (part + add_ref[...] if has_add else part).astype(out_dtype)
            return
        o_ref, acc = refs[-2:]
        kk = pl.program_id(2)

        @pl.when(kk == 0)
        def _():
            acc[...] = part

        @pl.when(kk > 0)
        def _():
            acc[...] += part

        @pl.when(kk == nk - 1)
        def _():
            r = acc[...]
            if has_add:
                r = r + add_ref[...]
            o_ref[...] = r.astype(out_dtype)

    in_specs = [a_spec, b_spec] + ([pl.BlockSpec((tm, tn), lambda i, j, kk: (mb0 + i, j))] if has_add else [])
    args = (a, b) + ((add,) if has_add else ())
    aliases = None
    if begun:
        aliases = {len(args): 0}
        in_specs, args = in_specs + [ANY], args + (into[0],)
    res, extra = _call(body, name, jax.ShapeDtypeStruct((out_rows, n), out_dtype), (m // tm, n // tn, nk), in_specs,
                       o_spec, args, [pltpu.VMEM((tm, tn), F32)] if nk > 1 else [], job, aliases)
    return res if job is None else (res, extra)


def _row_tile(rows, cols, budget_elems=512 * 1024):
    cands = [c for c in (1024, 704, 512, 352, 256, 128, 64, 32, 16) if c * cols <= budget_elems]
    return _pick(rows, cands or (16,))


_EW_COLS = (1280, 1408, 1024, 640, 512, 256, 128)


def _tile2d(rows, cols, max_elems):
    tc = _pick(cols, _EW_COLS)
    return _row_tile(rows, tc, max_elems), tc


def _rms_fwd(x, g, name):
    t, d = x.shape
    tr = _row_tile(t, d)

    def body(x_ref, g_ref, o_ref):
        xv = x_ref[...]
        rstd = lax.rsqrt(jnp.mean(xv * xv, axis=-1, keepdims=True) + EPS)
        o_ref[...] = (xv * rstd * g_ref[...]).astype(BF16)

    spec = pl.BlockSpec((tr, d), lambda i: (i, 0))
    return pl.pallas_call(body, name=name, out_shape=jax.ShapeDtypeStruct((t, d), BF16), grid=(t // tr,),
                          in_specs=[spec, pl.BlockSpec((1, d), lambda i: (0, 0))], out_specs=spec,
                          compiler_params=_params())(x, g)


def _rms_bwd(dxn, x, g, resid, name, job=None, mxu_copy=False):
    t, d = x.shape
    tr = _row_tile(t, d, 256 * 1024)

    def body(dxn_ref, x_ref, g_ref, r_ref, dx_ref, dg_ref, *dx16_ref):
        @pl.when(pl.program_id(0) == 0)
        def _():
            dg_ref[...] = jnp.zeros_like(dg_ref)

        xv = x_ref[...]
        rstd = lax.rsqrt(jnp.mean(xv * xv, axis=-1, keepdims=True) + EPS)
        xhat = xv * rstd
        dy = dxn_ref[...]
        dg_ref[...] += jnp.sum(dy * xhat, axis=0, keepdims=True)
        dxhat = dy * g_ref[...]
        dx = r_ref[...] + rstd * (dxhat - xhat * jnp.mean(dxhat * xhat, axis=-1, keepdims=True))
        dx_ref[...] = dx
        if mxu_copy:
            dx16_ref[0][...] = dx.astype(BF16)

    spec = pl.BlockSpec((tr, d), lambda i: (i, 0))
    vec = pl.BlockSpec((1, d), lambda i: (0, 0))
    shapes = (jax.ShapeDtypeStruct((t, d), F32), jax.ShapeDtypeStruct((1, d), F32))
    shapes += (jax.ShapeDtypeStruct((t, d), BF16),) if mxu_copy else ()
    res, extra = _call(body, name, shapes, (t // tr,), [spec, spec, vec, spec],
                       (spec, vec) + ((spec,) if mxu_copy else ()), (dxn, x, g, resid), (), job)
    return res if job is None else (res, extra)


def _swiglu_fwd(gate, up, name):
    t, f = gate.shape
    tr, tc = _tile2d(t, f, 1024 * 1024)

    def body(g_ref, u_ref, o_ref):
        gv = g_ref[...]
        o_ref[...] = (gv * _sigmoid(gv) * u_ref[...]).astype(BF16)

    spec = pl.BlockSpec((tr, tc), lambda i, j: (i, j))
    return pl.pallas_call(body, name=name, out_shape=jax.ShapeDtypeStruct((t, f), BF16), grid=(t // tr, f // tc),
                          in_specs=[spec, spec], out_specs=spec, compiler_params=_params())(gate, up)


def _swiglu_bwd(dact, gate, up, name, job=None):
    t, f = gate.shape
    tr, tc = _tile2d(t, f, 768 * 1024)

    def body(d_ref, g_ref, u_ref, dg_ref, du_ref):
        gv, dv = g_ref[...], d_ref[...]
        sg = _sigmoid(gv)
        dg_ref[...] = (dv * u_ref[...] * (sg * (1.0 + gv * (1.0 - sg)))).astype(BF16)
        du_ref[...] = (dv * (gv * sg)).astype(BF16)

    spec = pl.BlockSpec((tr, tc), lambda i, j: (i, j))
    shp = jax.ShapeDtypeStruct((t, f), BF16)
    res, extra = _call(body, name, (shp, shp), (t // tr, f // tc), [spec, spec, spec], (spec, spec), (dact, gate, up),
                       (), job)
    return res if job is None else (res, extra)


def _merge_fwd(z, b_gates, pa, plru, ga_off, name):
    t, d = pa.shape
    cw = _pick(math.gcd(ga_off, d), (512, 256, 128))
    tr = _row_tile(t, cw, 256 * 1024)
    oa, ol, nd = ga_off // cw, (ga_off + d) // cw, d // cw

    def body(ga_ref, gl_ref, ba_ref, bl_ref, pa_ref, pl_ref, o_ref):
        sa = _sigmoid(ga_ref[...] + ba_ref[...])
        sl = _sigmoid(gl_ref[...] + bl_ref[...])
        o_ref[...] = (sa * pa_ref[...] + sl * pl_ref[...]).astype(BF16)

    blk = pl.BlockSpec((tr, cw), lambda i, j: (i, j))
    return pl.pallas_call(
        body, name=name, out_shape=jax.ShapeDtypeStruct((t, d), BF16), grid=(t // tr, nd),
        in_specs=[pl.BlockSpec((tr, cw), lambda i, j: (i, oa + j)), pl.BlockSpec((tr, cw), lambda i, j: (i, ol + j)),
                  pl.BlockSpec((1, cw), lambda i, j: (0, j)), pl.BlockSpec((1, cw), lambda i, j: (0, nd + j)),
                  blk, blk],
        out_specs=blk, compiler_params=_params(),
    )(z, z, b_gates, b_gates, pa, plru)


def _merge_bwd(dmerged, z, b_gates, pa, plru, ga_off, name, job=None):
    t, d = pa.shape
    cw = _pick(math.gcd(ga_off, d), (512, 256, 128))
    tr = _row_tile(t, cw, 256 * 1024)
    oa, ol, nd = ga_off // cw, (ga_off + d) // cw, d // cw

    def body(dm_ref, ga_ref, gl_ref, ba_ref, bl_ref, pa_ref, pl_ref, dpa_ref, dpl_ref, dga_ref, dgl_ref, sa_ref, sl_ref):
        @pl.when(pl.program_id(1) == 0)
        def _():
            sa_ref[...] = jnp.zeros_like(sa_ref)
            sl_ref[...] = jnp.zeros_like(sl_ref)

        dm = dm_ref[...]
        sa = _sigmoid(ga_ref[...] + ba_ref[...])
        sl = _sigmoid(gl_ref[...] + bl_ref[...])
        dpa_ref[...] = (dm * sa).astype(BF16)
        dpl_ref[...] = (dm * sl).astype(BF16)
        dga = dm * pa_ref[...] * (sa * (1.0 - sa))
        dgl = dm * pl_ref[...] * (sl * (1.0 - sl))
        dga_ref[...] = dga.astype(BF16)
        dgl_ref[...] = dgl.astype(BF16)
        sa_ref[...] += jnp.sum(dga, axis=0, keepdims=True)
        sl_ref[...] += jnp.sum(dgl, axis=0, keepdims=True)

    blk = pl.BlockSpec((tr, cw), lambda j, i: (i, j))
    vec = pl.BlockSpec((1, cw), lambda j, i: (0, j))
    big16, v32 = jax.ShapeDtypeStruct((t, d), BF16), jax.ShapeDtypeStruct((1, d), F32)
    res, extra = _call(
        body, name, (big16, big16, big16, big16, v32, v32), (nd, t // tr),
        [blk, pl.BlockSpec((tr, cw), lambda j, i: (i, oa + j)), pl.BlockSpec((tr, cw), lambda j, i: (i, ol + j)),
         vec, pl.BlockSpec((1, cw), lambda j, i: (0, nd + j)), blk, blk],
        (blk, blk, blk, blk, vec, vec), (dmerged, z, z, b_gates, b_gates, pa, plru), (), job)
    return res if job is None else (res, extra)


def _loss_head(y, target, name):
    t, d = y.shape
    tr = _row_tile(t, d, 256 * 1024)
    nt = t // tr

    def body(y_ref, t_ref, dy_ref, dy16_ref, loss_ref, acc):
        i = pl.program_id(0)

        @pl.when(i == 0)
        def _():
            acc[...] = jnp.zeros_like(acc)

        e = y_ref[...] - t_ref[...]
        dy = e * (1.0 / d)
        dy_ref[...] = dy
        dy16_ref[...] = dy.astype(BF16)
        acc[...] += jnp.sum(e * e, axis=0, keepdims=True)

        @pl.when(i == nt - 1)
        def _():
            loss_ref[...] = (0.5 / d) * jnp.sum(acc[...], axis=-1, keepdims=True)

    spec = pl.BlockSpec((tr, d), lambda i: (i, 0))
    return pl.pallas_call(
        body, name=name, out_shape=(jax.ShapeDtypeStruct((t, d), F32), jax.ShapeDtypeStruct((t, d), BF16),
                                    jax.ShapeDtypeStruct((1, 1), F32)),
        grid=(nt,), in_specs=[spec, spec], out_specs=(spec, spec, pl.BlockSpec((1, 1), lambda i: (0, 0))),
        scratch_shapes=[pltpu.VMEM((1, d), F32)], compiler_params=_params(),
    )(y, target)


def _adamw(w, g, m, v, name):
    r, c = w.shape
    tr, tc = _tile2d(r, c, 512 * 1024)
    c1 = 1.0 - ADAM_B1 ** ADAM_STEP
    c2 = 1.0 - ADAM_B2 ** ADAM_STEP

    def body(w_ref, g_ref, m_ref, v_ref, d_ref, nm_ref, nv_ref):
        gv = g_ref[...]
        mn = ADAM_B1 * m_ref[...] + (1.0 - ADAM_B1) * gv
        vn = ADAM_B2 * v_ref[...] + (1.0 - ADAM_B2) * (gv * gv)
        d_ref[...] = -ADAM_LR * ((mn / c1) / (jnp.sqrt(vn / c2) + ADAM_EPS) + ADAM_WD * w_ref[...])
        nm_ref[...] = mn
        nv_ref[...] = vn

    spec = pl.BlockSpec((tr, tc), lambda i, j: (i, j))
    shp = jax.ShapeDtypeStruct((r, c), F32)
    return pl.pallas_call(body, name=name, out_shape=(shp, shp, shp), grid=(r // tr, c // tc), in_specs=[spec] * 4,
                          out_specs=(spec, spec, spec), compiler_params=_params())(w, g, m, v)


def _swap_halves(v, lane, half):
    n = v.shape[-1]
    return jnp.where(lane < half, pltpu.roll(v, n - half, 1),
                     jnp.where(lane < 2 * half, pltpu.roll(v, half, 1), 0.0))


def _norm_fwd(xraw, g):
    rstd = lax.rsqrt(jnp.mean(xraw * xraw, axis=-1, keepdims=True) + EPS)
    xhat = xraw * rstd
    return xhat, rstd, xhat * g


def _norm_bwd(dy, xhat, rstd, g):
    dxhat = dy * g
    dx = rstd * (dxhat - xhat * jnp.mean(dxhat * xhat, axis=-1, keepdims=True))
    return dx, jnp.sum(dy * xhat, axis=0, keepdims=True)


def _attn_specs(nb, grp, hd, kv, clamp):
    qo, ko, vo = 0, (kv * grp), (kv * grp + kv)
    cur = (lambda i: jnp.minimum(i, nb - 1)) if clamp else (lambda i: i)
    prev = lambda i: jnp.maximum(cur(i) - 1, 0)
    zq = pl.BlockSpec((BLK, grp * hd), lambda h, i: (cur(i), h))
    kc = pl.BlockSpec((BLK, hd), lambda h, i: (cur(i), ko + h))
    kp = pl.BlockSpec((BLK, hd), lambda h, i: (prev(i), ko + h))
    vc = pl.BlockSpec((BLK, hd), lambda h, i: (cur(i), vo + h))
    vp = pl.BlockSpec((BLK, hd), lambda h, i: (prev(i), vo + h))
    tc = pl.BlockSpec((BLK, hd), lambda h, i: (cur(i), 0))
    tp = pl.BlockSpec((BLK, hd), lambda h, i: (prev(i), 0))
    gs = pl.BlockSpec((1, hd), lambda h, i: (0, 0))
    return zq, kc, kp, vc, vp, tc, tp, gs


def _attn_mask(i):
    qi = lax.broadcasted_iota(jnp.int32, (BLK, 2 * BLK), 0)
    kj = lax.broadcasted_iota(jnp.int32, (BLK, 2 * BLK), 1)
    rel = qi + BLK - kj
    return (rel >= 0) & (rel < WINDOW) & ((kj >= BLK) | (i > 0))


def _attn_fwd(z, cos_t, sin_t, qg, kg, sinks, kv, grp, hd, name, job=None):
    t = z.shape[0]
    nb = t // BLK
    half = hd // 8
    scale = 1.0 / math.sqrt(hd)
    zq, kc, kp, vc, vp, tc, tp, gs = _attn_specs(nb, grp, hd, kv, False)

    def body(sink_ref, zq_ref, kc_ref, kp_ref, vc_ref, vp_ref, cc_ref, sc_ref, cp_ref, sp_ref, qg_ref, kg_ref, o_ref):
        h, i = pl.program_id(0), pl.program_id(1)
        lane = lax.broadcasted_iota(jnp.int32, (BLK, hd), 1)

        def normrope(xraw, g, c, s):
            y = _norm_fwd(xraw, g)[2]
            return y * c + _swap_halves(y, lane, half) * s

        cc, sc = cc_ref[...], sc_ref[...]
        kcur = normrope(kc_ref[...], kg_ref[...], cc, sc)
        kprev = normrope(kp_ref[...], kg_ref[...], cp_ref[...], sp_ref[...])
        kk = jnp.concatenate([kprev, kcur], axis=0).astype(BF16)
        vv = jnp.concatenate([vp_ref[...], vc_ref[...]], axis=0).astype(BF16)
        mask = _attn_mask(i)
        for g in range(grp):
            q = normrope(zq_ref[:, g * hd:(g + 1) * hd], qg_ref[...], cc, sc).astype(BF16)
            s = lax.dot_general(q, kk, (((1,), (1,)), ((), ())), preferred_element_type=F32) * scale
            s = jnp.where(mask, s, NEG)
            sk = sink_ref[h * grp + g]
            mx = jnp.maximum(jnp.max(s, axis=-1, keepdims=True), sk)
            p = jnp.exp(s - mx)
            den = jnp.sum(p, axis=-1, keepdims=True) + jnp.exp(sk - mx)
            p = p * (1.0 / den)
            o_ref[:, g * hd:(g + 1) * hd] = jnp.dot(p.astype(BF16), vv, preferred_element_type=F32).astype(BF16)

    res, extra = _call(
        body, name, jax.ShapeDtypeStruct((t, kv * grp * hd), BF16), (kv, nb),
        [pl.BlockSpec(memory_space=pltpu.SMEM), zq, kc, kp, vc, vp, tc, tc, tp, tp, gs, gs],
        pl.BlockSpec((BLK, grp * hd), lambda h, i: (i, h)),
        (sinks, z, z, z, z, z, cos_t, sin_t, cos_t, sin_t, qg, kg), (), job)
    return res if job is None else (res, extra)


def _attn_bwd(dattn, z, cos_t, sin_t, qg, kg, sinks, kv, grp, hd, name):
    t = z.shape[0]
    nb = t // BLK
    half = hd // 8
    scale = 1.0 / math.sqrt(hd)
    zq, kc, kp, vc, vp, tc, tp, gs = _attn_specs(nb, grp, hd, kv, True)

    def body(sink_ref, zq_ref, kc_ref, kp_ref, vc_ref, vp_ref, cc_ref, sc_ref, cp_ref, sp_ref, qg_ref, kg_ref, do_ref,
             dq_ref, dk_ref, dv_ref, dqg_ref, dkg_ref, dsk_ref, dk_carry, dv_carry):
        h, i = pl.program_id(0), pl.program_id(1)
        lane = lax.broadcasted_iota(jnp.int32, (BLK, hd), 1)
        lane1 = lax.broadcasted_iota(jnp.int32, (1, LANE), 1)

        @pl.when((h == 0) & (i == 0))
        def _():
            dqg_ref[...] = jnp.zeros_like(dqg_ref)
            dkg_ref[...] = jnp.zeros_like(dkg_ref)
            dsk_ref[...] = jnp.zeros_like(dsk_ref)

        @pl.when(i == 0)
        def _():
            dk_carry[...] = jnp.zeros_like(dk_carry)
            dv_carry[...] = jnp.zeros_like(dv_carry)

        def rope(y, c, s):
            return y * c + _swap_halves(y, lane, half) * s

        def rope_bwd(dout, c, s):
            return dout * c + _swap_halves(dout * s, lane, half)

        @pl.when(i < nb)
        def _():
            cc, sc, cp, sp = cc_ref[...], sc_ref[...], cp_ref[...], sp_ref[...]
            qgv, kgv = qg_ref[...], kg_ref[...]
            xh_kc, rs_kc, y_kc = _norm_fwd(kc_ref[...], kgv)
            xh_kp, rs_kp, y_kp = _norm_fwd(kp_ref[...], kgv)
            kk = jnp.concatenate([rope(y_kp, cp, sp), rope(y_kc, cc, sc)], axis=0).astype(BF16)
            vv = jnp.concatenate([vp_ref[...], vc_ref[...]], axis=0).astype(BF16)
            mask = _attn_mask(i)
            dkk = jnp.zeros((2 * BLK, hd), F32)
            dvv = jnp.zeros((2 * BLK, hd), F32)
            dqg = jnp.zeros((1, hd), F32)
            dsk = jnp.zeros((1, LANE), F32)
            for g in range(grp):
                xh_q, rs_q, y_q = _norm_fwd(zq_ref[:, g * hd:(g + 1) * hd], qgv)
                q = rope(y_q, cc, sc).astype(BF16)
                s = lax.dot_general(q, kk, (((1,), (1,)), ((), ())), preferred_element_type=F32) * scale
                s = jnp.where(mask, s, NEG)
                sk = sink_ref[h * grp + g]
                mx = jnp.maximum(jnp.max(s, axis=-1, keepdims=True), sk)
                p = jnp.exp(s - mx)
                den = jnp.sum(p, axis=-1, keepdims=True) + jnp.exp(sk - mx)
                inv_den = 1.0 / den
                p = p * inv_den
                psink = jnp.exp(sk - mx) * inv_den
                dog = do_ref[:, g * hd:(g + 1) * hd].astype(BF16)
                dp = lax.dot_general(dog, vv, (((1,), (1,)), ((), ())), preferred_element_type=F32)
                rsum = jnp.sum(p * dp, axis=-1, keepdims=True)
                ds = (p * (dp - rsum) * scale).astype(BF16)
                dsk = dsk + jnp.where(lane1 == h * grp + g, jnp.sum(-psink * rsum, axis=0, keepdims=True), 0.0)
                dqn = jnp.dot(ds, kk, preferred_element_type=F32)
                dkk = dkk + lax.dot_general(ds, q, (((0,), (0,)), ((), ())), preferred_element_type=F32)
                dvv = dvv + lax.dot_general(p.astype(BF16), dog, (((0,), (0,)), ((), ())), preferred_element_type=F32)
                dxq, dg_q = _norm_bwd(rope_bwd(dqn, cc, sc), xh_q, rs_q, qgv)
                dq_ref[:, g * hd:(g + 1) * hd] = dxq.astype(BF16)
                dqg = dqg + dg_q
            dkp_raw, dg_kp = _norm_bwd(rope_bwd(dkk[:BLK], cp, sp), xh_kp, rs_kp, kgv)
            dkc_raw, dg_kc = _norm_bwd(rope_bwd(dkk[BLK:], cc, sc), xh_kc, rs_kc, kgv)
            dk_ref[...] = (dk_carry[...] + dkp_raw).astype(BF16)
            dv_ref[...] = (dv_carry[...] + dvv[:BLK]).astype(BF16)
            dk_carry[...] = dkc_raw
            dv_carry[...] = dvv[BLK:]
            dqg_ref[...] += dqg
            dkg_ref[...] += dg_kp + dg_kc
            dsk_ref[...] += dsk

        @pl.when(i == nb)
        def _():
            dk_ref[...] = dk_carry[...].astype(BF16)
            dv_ref[...] = dv_carry[...].astype(BF16)

    kvw = kv * hd
    vec = pl.BlockSpec((1, hd), lambda h, i: (0, 0))
    shifted = pl.BlockSpec((BLK, hd), lambda h, i: (jnp.maximum(i - 1, 0), h))
    return pl.pallas_call(
        body, name=name,
        out_shape=(jax.ShapeDtypeStruct((t, kv * grp * hd), BF16), jax.ShapeDtypeStruct((t, kvw), BF16),
                   jax.ShapeDtypeStruct((t, kvw), BF16), jax.ShapeDtypeStruct((1, hd), F32),
                   jax.ShapeDtypeStruct((1, hd), F32), jax.ShapeDtypeStruct((1, LANE), F32)),
        grid=(kv, nb + 1),
        in_specs=[pl.BlockSpec(memory_space=pltpu.SMEM), zq, kc, kp, vc, vp, tc, tc, tp, tp, gs, gs,
                  pl.BlockSpec((BLK, grp * hd), lambda h, i: (jnp.minimum(i, nb - 1), h))],
        out_specs=(pl.BlockSpec((BLK, grp * hd), lambda h, i: (jnp.minimum(i, nb - 1), h)), shifted, shifted, vec, vec,
                   pl.BlockSpec((1, LANE), lambda h, i: (0, 0))),
        scratch_shapes=[pltpu.VMEM((BLK, hd), F32), pltpu.VMEM((BLK, hd), F32)], compiler_params=_params(),
    )(sinks, z, z, z, z, z, cos_t, sin_t, cos_t, sin_t, qg, kg, dattn)


def _window(rows, cb, c0, row_of, col_of):
    assert rows % SUBLANE == 0 and cb % LANE == 0 and c0 % LANE == 0, (rows, cb, c0)
    return pl.BlockSpec((pl.Element(rows), pl.Element(cb)),
                        lambda *g: (pl.multiple_of(row_of(*g) * rows, SUBLANE), pl.multiple_of(c0 + col_of(*g) * cb, LANE)))


def _conv_fwd(z, c0, c, w, b, name):
    t = z.shape[0]
    taps = w.shape[0]
    cb = _pick(c, (1408, 1024, 512, 256, 128))
    tr = _row_tile(t, cb, 256 * 1024)
    hb = tr // SUBLANE

    def body(u_ref, halo_ref, w_ref, b_ref, o_ref):
        i = pl.program_id(0)
        x = u_ref[...]
        acc = b_ref[...] + w_ref[taps - 1:taps, :] * x
        for k in range(taps - 1):
            acc = acc + w_ref[k:k + 1, :] * pltpu.roll(x, taps - 1 - k, 0)
        o_ref[...] = acc
        row = lax.broadcasted_iota(jnp.int32, (SUBLANE, cb), 0)
        hp = jnp.where(i > 0, halo_ref[...], 0.0)
        x8 = u_ref[0:SUBLANE, :]
        acc8 = b_ref[...] + w_ref[taps - 1:taps, :] * x8
        for k in range(taps - 1):
            s = taps - 1 - k
            acc8 = acc8 + w_ref[k:k + 1, :] * jnp.where(row < s, pltpu.roll(hp, s, 0), pltpu.roll(x8, s, 0))
        o_ref[0:SUBLANE, :] = acc8

    blk = pl.BlockSpec((tr, cb), lambda i, j: (i, j))
    return pl.pallas_call(
        body, name=name, out_shape=jax.ShapeDtypeStruct((t, c), F32), grid=(t // tr, c // cb),
        in_specs=[_window(tr, cb, c0, lambda i, j: i, lambda i, j: j),
                  _window(SUBLANE, cb, c0, lambda i, j: jnp.maximum(i * hb - 1, 0), lambda i, j: j),
                  pl.BlockSpec((taps, cb), lambda i, j: (0, j)), pl.BlockSpec((1, cb), lambda i, j: (0, j))],
        out_specs=blk, compiler_params=_params(),
    )(z, z, w, b)


def _conv_bwd(duc, z, c0, w, name):
    t, c = duc.shape
    taps = w.shape[0]
    cb = _pick(c, (1408, 1024, 512, 256, 128))
    tr = _row_tile(t, cb, 256 * 1024)
    hb, nt = tr // SUBLANE, t // tr

    def body(g_ref, gnext_ref, u_ref, uprev_ref, w_ref, du16_ref, dw_ref, db_ref, du_ref):
        i = pl.program_id(1)

        @pl.when(i == 0)
        def _():
            dw_ref[...] = jnp.zeros_like(dw_ref)
            db_ref[...] = jnp.zeros_like(db_ref)

        row = lax.broadcasted_iota(jnp.int32, (SUBLANE, cb), 0)
        g, x = g_ref[...], u_ref[...]
        du = w_ref[taps - 1:taps, :] * g
        for k in range(taps - 1):
            du = du + w_ref[k:k + 1, :] * pltpu.roll(g, tr - (taps - 1 - k), 0)
        du_ref[...] = du
        hn = jnp.where(i < nt - 1, gnext_ref[...], 0.0)
        g8 = g_ref[tr - SUBLANE:tr, :]
        du8 = w_ref[taps - 1:taps, :] * g8
        for k in range(taps - 1):
            s = taps - 1 - k
            du8 = du8 + w_ref[k:k + 1, :] * jnp.where(row >= SUBLANE - s, pltpu.roll(hn, SUBLANE - s, 0),
                                                     pltpu.roll(g8, SUBLANE - s, 0))
        du_ref[tr - SUBLANE:tr, :] = du8
        du16_ref[...] = du_ref[...].astype(BF16)

        hp = jnp.where(i > 0, uprev_ref[...], 0.0)
        xl8, gf8 = u_ref[tr - SUBLANE:tr, :], g_ref[0:SUBLANE, :]
        db_ref[...] += jnp.sum(g, axis=0, keepdims=True)
        dw_ref[taps - 1:taps, :] += jnp.sum(g * x, axis=0, keepdims=True)
        for k in range(taps - 1):
            s = taps - 1 - k
            fix = jnp.where(row < s, pltpu.roll(hp, s, 0) - pltpu.roll(xl8, s, 0), 0.0)
            dw_ref[k:k + 1, :] += (jnp.sum(g * pltpu.roll(x, s, 0), axis=0, keepdims=True)
                                   + jnp.sum(gf8 * fix, axis=0, keepdims=True))

    blk = pl.BlockSpec((tr, cb), lambda j, i: (i, j))
    nh = t // SUBLANE
    return pl.pallas_call(
        body, name=name,
        out_shape=(jax.ShapeDtypeStruct((t, c), BF16), jax.ShapeDtypeStruct((taps, c), F32),
                   jax.ShapeDtypeStruct((1, c), F32)),
        grid=(c // cb, nt),
        in_specs=[blk, pl.BlockSpec((SUBLANE, cb), lambda j, i: (jnp.minimum((i + 1) * hb, nh - 1), j)),
                  _window(tr, cb, c0, lambda j, i: i, lambda j, i: j),
                  _window(SUBLANE, cb, c0, lambda j, i: jnp.maximum(i * hb - 1, 0), lambda j, i: j),
                  pl.BlockSpec((taps, cb), lambda j, i: (0, j))],
        out_specs=(blk, pl.BlockSpec((taps, cb), lambda j, i: (0, j)), pl.BlockSpec((1, cb), lambda j, i: (0, j))),
        scratch_shapes=[pltpu.VMEM((tr, cb), F32)], compiler_params=_params(),
    )(duc, duc, z, z, w)


def _gates_fwd(uc, wr, wi, gw, name):
    t, c = uc.shape
    n, bw, _ = wr.shape
    per, ng = gw // bw, c // gw
    tr = _pick(t, (512, 256, 128))

    def body(u_ref, wr_ref, wi_ref, r_ref, i_ref):
        for b in range(per):
            cols = slice(b * bw, (b + 1) * bw)
            a = u_ref[:, cols].astype(BF16)
            r_ref[:, cols] = jnp.dot(a, wr_ref[b].astype(BF16), preferred_element_type=F32)
            i_ref[:, cols] = jnp.dot(a, wi_ref[b].astype(BF16), preferred_element_type=F32)

    blk = pl.BlockSpec((tr, gw), lambda h, i: (i, h))
    wsp = pl.BlockSpec((per, bw, bw), lambda h, i: (h, 0, 0))
    shp = jax.ShapeDtypeStruct((t, c), F32)
    return pl.pallas_call(body, name=name, out_shape=(shp, shp), grid=(ng, t // tr), in_specs=[blk, wsp, wsp],
                          out_specs=(blk, blk), compiler_params=_params())(uc, wr, wi)


def _gates_bwd_x(duc, drp, dip, wr, wi, gw, name):
    t, c = duc.shape
    n, bw, _ = wr.shape
    per, ng = gw // bw, c // gw
    tr = _pick(t, (512, 256, 128))
    dims = (((1,), (1,)), ((), ()))

    def body(d_ref, r_ref, i_ref, wr_ref, wi_ref, o_ref):
        for b in range(per):
            cols = slice(b * bw, (b + 1) * bw)
            o_ref[:, cols] = (
                d_ref[:, cols]
                + lax.dot_general(r_ref[:, cols].astype(BF16), wr_ref[b].astype(BF16), dims, preferred_element_type=F32)
                + lax.dot_general(i_ref[:, cols].astype(BF16), wi_ref[b].astype(BF16), dims, preferred_element_type=F32))

    blk = pl.BlockSpec((tr, gw), lambda h, i: (i, h))
    wsp = pl.BlockSpec((per, bw, bw), lambda h, i: (h, 0, 0))
    return pl.pallas_call(body, name=name, out_shape=jax.ShapeDtypeStruct((t, c), F32), grid=(ng, t // tr),
                          in_specs=[blk, blk, blk, wsp, wsp], out_specs=blk, compiler_params=_params())(duc, drp, dip, wr, wi)


def _gates_bwd_w(uc, dpre, n, bw, gw, name):
    t, c = uc.shape
    per, ng = gw // bw, c // gw
    tk = _pick(t, (512, 256, 128))
    dims = (((0,), (0,)), ((), ()))

    def body(u_ref, d_ref, o_ref):
        @pl.when(pl.program_id(1) == 0)
        def _():
            o_ref[...] = jnp.zeros_like(o_ref)

        for b in range(per):
            cols = slice(b * bw, (b + 1) * bw)
            o_ref[b] += lax.dot_general(u_ref[:, cols].astype(BF16), d_ref[:, cols].astype(BF16), dims,
                                        preferred_element_type=F32)

    blk = pl.BlockSpec((tk, gw), lambda h, i: (i, h))
    return pl.pallas_call(body, name=name, out_shape=jax.ShapeDtypeStruct((n, bw, bw), F32), grid=(ng, t // tk),
                          in_specs=[blk, blk], out_specs=pl.BlockSpec((per, bw, bw), lambda h, i: (h, 0, 0)),
                          compiler_params=_params())(uc, dpre)


def _softplus(x):
    return jnp.maximum(x, 0.0) + jnp.log(1.0 + jnp.exp(-jnp.abs(x)))


_GELU_C = math.sqrt(2.0 / math.pi)


def _gelu_parts(x):
    inner = _GELU_C * (x + 0.044715 * (x * x * x))
    th = jnp.tanh(inner)
    gelu = 0.5 * x * (1.0 + th)
    dgelu = 0.5 * (1.0 + th) + 0.5 * x * (1.0 - th * th) * (_GELU_C * (1.0 + 3.0 * 0.044715 * (x * x)))
    return gelu, dgelu


def _lru_gate_values(rpre, ipre, br, bi, sp):
    r = _sigmoid(rpre + br)
    ig = _sigmoid(ipre + bi)
    log_a = -LRU_C * r * sp
    a = jnp.exp(log_a)
    e2 = jnp.tanh(-log_a) * (1.0 + a * a)
    inv = lax.rsqrt(jnp.maximum(e2, 1e-30))
    return r, ig, a, e2 * inv, inv


def _lru_fwd(uc, rpre, ipre, z, gr0, br, bi, lam, name, job=None):
    t, c = uc.shape
    cb = _pick(c, (1408, 1024, 512, 256, 128))
    tb = _pick(t, (512, 256, 128))
    ntile = tb // SUBLANE

    def body(uc_ref, r_ref, i_ref, gr_ref, br_ref, bi_ref, lam_ref, h_ref, rec16_ref, carry, rec_ref):
        @pl.when(pl.program_id(1) == 0)
        def _():
            carry[...] = jnp.zeros_like(carry)

        sp = _softplus(-lam_ref[...])
        br, bi = br_ref[...], bi_ref[...]
        row = lax.broadcasted_iota(jnp.int32, (SUBLANE, cb), 0)

        def tile(k, c_in):
            sl = pl.ds(pl.multiple_of(k * SUBLANE, SUBLANE), SUBLANE)
            ucv = uc_ref[sl, :]
            _, ig, a, mult, _ = _lru_gate_values(r_ref[sl, :], i_ref[sl, :], br, bi, sp)
            b = mult * (ig * ucv)
            for d in (1, 2, 4):
                a_s = jnp.where(row >= d, pltpu.roll(a, d, 0), 1.0)
                b_s = jnp.where(row >= d, pltpu.roll(b, d, 0), 0.0)
                b = a * b_s + b
                a = a * a_s
            hv = b + a * c_in
            h_ref[sl, :] = hv
            rec_ref[sl, :] = hv * _gelu_parts(gr_ref[sl, :])[0]
            return hv[SUBLANE - 1:SUBLANE, :]

        c_out = lax.fori_loop(0, ntile, tile, carry[0:1, :])
        carry[...] = jnp.broadcast_to(c_out, (SUBLANE, cb))
        rec16_ref[...] = rec_ref[...].astype(BF16)

    blk = pl.BlockSpec((tb, cb), lambda j, i: (i, j))
    vec = pl.BlockSpec((1, cb), lambda j, i: (0, j))
    res, extra = _call(body, name, (jax.ShapeDtypeStruct((t, c), F32), jax.ShapeDtypeStruct((t, c), BF16)),
                       (c // cb, t // tb),
                       [blk, blk, blk, _window(tb, cb, gr0, lambda j, i: i, lambda j, i: j), vec, vec, vec], (blk, blk),
                       (uc, rpre, ipre, z, br, bi, lam), [pltpu.VMEM((SUBLANE, cb), F32), pltpu.VMEM((tb, cb), F32)], job)
    return res if job is None else (res, extra)


def _lru_bwd(drec, hst, uc, rpre, ipre, z, gr0, br, bi, lam, name, job=None):
    t, c = uc.shape
    cb = _pick(c, (1408, 1024, 512, 256, 128))
    tb = _pick(t, (256, 128))
    ntile, nt, hb = tb // SUBLANE, t // tb, tb // SUBLANE

    def body(drec_ref, h_ref, hprev_ref, uc_ref, r_ref, i_ref, gr_ref, br_ref, bi_ref, lam_ref,
             dgr16_ref, drp_ref, dip_ref, duc_ref, dlam_ref, dbr_ref, dbi_ref, carry, dgr_ref):
        step = pl.program_id(1)
        first_block = step == nt - 1

        @pl.when(step == 0)
        def _():
            carry[...] = jnp.zeros_like(carry)
            dlam_ref[...] = jnp.zeros_like(dlam_ref)
            dbr_ref[...] = jnp.zeros_like(dbr_ref)
            dbi_ref[...] = jnp.zeros_like(dbi_ref)

        lam = lam_ref[...]
        sp = _softplus(-lam)
        br, bi = br_ref[...], bi_ref[...]
        row = lax.broadcasted_iota(jnp.int32, (SUBLANE, cb), 0)
        halo = jnp.where(first_block, 0.0, hprev_ref[...])

        def tile(kk, state):
            c_p, acc_sp, acc_br, acc_bi = state
            k = ntile - 1 - kk
            sl = pl.ds(pl.multiple_of(k * SUBLANE, SUBLANE), SUBLANE)
            slp = pl.ds(pl.multiple_of(jnp.maximum(k - 1, 0) * SUBLANE, SUBLANE), SUBLANE)
            ucv = uc_ref[sl, :]
            r, ig, a, mult, inv_mult = _lru_gate_values(r_ref[sl, :], i_ref[sl, :], br, bi, sp)
            hv = h_ref[sl, :]
            below = jnp.where(k > 0, h_ref[slp, :], halo)
            hprev = jnp.where(row == 0, pltpu.roll(below, 1, 0), pltpu.roll(hv, 1, 0))
            gelu, dgelu = _gelu_parts(gr_ref[sl, :])
            drec = drec_ref[sl, :]
            dh = drec * gelu
            dgr_ref[sl, :] = drec * hv * dgelu
            pa, pb = a, a * dh
            for d in (1, 2, 4):
                a_s = jnp.where(row < SUBLANE - d, pltpu.roll(pa, SUBLANE - d, 0), 1.0)
                b_s = jnp.where(row < SUBLANE - d, pltpu.roll(pb, SUBLANE - d, 0), 0.0)
                pb = pa * b_s + pb
                pa = pa * a_s
            pv = pb + pa * c_p
            gt = dh + jnp.where(row == SUBLANE - 1, c_p, pltpu.roll(pv, SUBLANE - 1, 0))
            da = gt * hprev
            duc_ref[sl, :] = gt * mult * ig
            dmult = gt * ig * ucv
            dig = gt * mult * ucv
            dla = da * a - jnp.where(mult > 0.0, dmult * (a * a) * inv_mult, 0.0)
            drp = dla * (-LRU_C * sp) * (r * (1.0 - r))
            dip = dig * (ig * (1.0 - ig))
            drp_ref[sl, :] = drp
            dip_ref[sl, :] = dip
            return pv[0:1, :], acc_sp + dla * (-LRU_C * r), acc_br + drp, acc_bi + dip

        zero = jnp.zeros((SUBLANE, cb), F32)
        c_out, acc_sp, acc_br, acc_bi = lax.fori_loop(0, ntile, tile, (carry[0:1, :], zero, zero, zero))
        carry[...] = jnp.broadcast_to(c_out, (SUBLANE, cb))
        dlam_ref[...] += jnp.sum(acc_sp, axis=0, keepdims=True) * (-_sigmoid(-lam))
        dbr_ref[...] += jnp.sum(acc_br, axis=0, keepdims=True)
        dbi_ref[...] += jnp.sum(acc_bi, axis=0, keepdims=True)
        dgr16_ref[...] = dgr_ref[...].astype(BF16)

    blk = pl.BlockSpec((tb, cb), lambda j, i: (nt - 1 - i, j))
    vec = pl.BlockSpec((1, cb), lambda j, i: (0, j))
    halo_spec = pl.BlockSpec((SUBLANE, cb), lambda j, i: (jnp.maximum((nt - 1 - i) * hb - 1, 0), j))
    big, small = jax.ShapeDtypeStruct((t, c), F32), jax.ShapeDtypeStruct((1, c), F32)
    res, extra = _call(
        body, name, (jax.ShapeDtypeStruct((t, c), BF16), big, big, big, small, small, small), (c // cb, nt),
        [blk, blk, halo_spec, blk, blk, blk, _window(tb, cb, gr0, lambda j, i: nt - 1 - i, lambda j, i: j),
         vec, vec, vec], (blk, blk, blk, blk, vec, vec, vec),
        (drec, hst, hst, uc, rpre, ipre, z, br, bi, lam),
        [pltpu.VMEM((SUBLANE, cb), F32), pltpu.VMEM((tb, cb), F32)], job)
    return res if job is None else (res, extra)


def _shard_region(ref, kind, chip, half, rh, width):
    if kind == "col":
        return ref.at[pl.ds(half * rh, rh), pl.ds(chip * width, width)]
    return ref.at[pl.ds(chip * (2 * rh) + half * rh, rh), :]


class _AllGather(_Exchange):
    def __init__(self, fulls, kinds):
        self.inputs, self.kinds = list(fulls), kinds
        self.out_shapes = [jax.ShapeDtypeStruct(f.shape, f.dtype) for f in fulls]
        self.aliases = {a: a for a in range(len(fulls))}
        self.n_sems = 6 * len(fulls)
        self.geo = [(f.shape[0] // 2, f.shape[1] // N_CHIPS) if k == "col" else (f.shape[0] // (2 * N_CHIPS), f.shape[1])
                    for f, k in zip(fulls, kinds)]

    def _region(self, ref, a, chip, half):
        return _shard_region(ref, self.kinds[a], chip, half, *self.geo[a])

    def _ici(self, e, a, k, chip):
        cx, cy = e.chips[k]
        return e.copy(self._region(e.ins[a], a, chip, e.c), self._region(e.outs[a], a, chip, e.c), a * 6 + k,
                      (cx, cy, e.c))

    def _d2d(self, e, a, k, half):
        cx, cy = e.chips[k]
        region = self._region(e.outs[a], a, 2 * cx + cy, half)
        return e.copy(region, region, a * 6 + 3 + k, e.sibling)

    def start(self, e):
        for a in range(len(self.inputs)):
            for k in range(3):
                self._ici(e, a, k, e.me).start()

    def finish(self, e):
        n = len(self.inputs)
        for a in range(n):
            for k, (cx, cy) in enumerate(e.chips):
                self._ici(e, a, k, 2 * cx + cy).wait_recv()
                self._d2d(e, a, k, e.c).start()
        for a in range(n):
            for k in range(3):
                self._d2d(e, a, k, 1 - e.c).wait_recv()
        for a in range(n):
            for k in range(3):
                self._ici(e, a, k, e.me).wait_send()
                self._d2d(e, a, k, e.c).wait_send()


class _SiblingExchange(_Exchange):
    def __init__(self, grads):
        self.inputs = list(grads)
        self.out_shapes = [jax.ShapeDtypeStruct((g.shape[0],) + g.shape[2:], g.dtype) for g in grads]
        self.n_sems = len(grads)

    def _copy(self, e, a):
        return e.copy(e.ins[a].at[:, 1 - e.c], e.outs[a], a, e.sibling)

    def start(self, e):
        for a in range(len(self.inputs)):
            self._copy(e, a).start()

    def finish(self, e):
        for a in range(len(self.inputs)):
            self._copy(e, a).wait()


def _piece(ref, kind, chip, width):
    if kind == "col":
        return ref.at[0, :, pl.ds(chip * width, width)]
    return ref.at[chip]


class _ChipExchange(_Exchange):
    def __init__(self, sums, kinds):
        self.inputs, self.kinds = list(sums), kinds
        self.widths = [s.shape[2] // N_CHIPS if k == "col" else s.shape[2] for s, k in zip(sums, kinds)]
        self.out_shapes = [jax.ShapeDtypeStruct((3, s.shape[1], w), s.dtype) for s, w in zip(sums, self.widths)]
        self.n_sems = 3 * len(sums)

    def _copy(self, e, a, k, chip):
        cx, cy = e.chips[k]
        return e.copy(_piece(e.ins[a], self.kinds[a], chip, self.widths[a]), e.outs[a].at[k], a * 3 + k, (cx, cy, e.c))

    def start(self, e):
        for a in range(len(self.inputs)):
            for k, (cx, cy) in enumerate(e.chips):
                self._copy(e, a, k, 2 * cx + cy).start()

    def finish(self, e):
        for a in range(len(self.inputs)):
            for k, (cx, cy) in enumerate(e.chips):
                self._copy(e, a, k, 2 * cx + cy).wait()


class _FinishExchange(_Exchange):
    def __init__(self, finals, to_all):
        self.inputs, self.to_all = list(finals), list(to_all)
        self.out_shapes = [jax.ShapeDtypeStruct(f.shape, f.dtype) for f in finals]
        self.aliases = {a: a for a in range(len(finals))}
        self.first_sem, self.n_sems = [], 0
        for all8 in self.to_all:
            self.first_sem.append(self.n_sems)
            self.n_sems += 7 if all8 else 1
        self.rel = [(fx, fy, fc) for fx in (0, 1) for fy in (0, 1) for fc in (0, 1)][1:]

    def _copies(self, e, mine):
        for a, all8 in enumerate(self.to_all):
            src = e.ins[a] if mine else e.outs[a]
            if not all8:
                rh = self.inputs[a].shape[0] // 2
                rows = pl.ds((e.c if mine else 1 - e.c) * rh, rh)
                yield e.copy(src.at[rows, :], e.outs[a].at[rows, :], self.first_sem[a], e.sibling)
                continue
            rh = self.inputs[a].shape[0] // (2 * N_CHIPS)
            for r, (fx, fy, fc) in enumerate(self.rel):
                px, py, pc = (1 - e.x if fx else e.x), (1 - e.y if fy else e.y), (1 - e.c if fc else e.c)
                rows = pl.ds(((2 * e.me + e.c) if mine else (2 * (2 * px + py) + pc)) * rh, rh)
                yield e.copy(src.at[rows, :], e.outs[a].at[rows, :], self.first_sem[a] + r, (px, py, pc))

    def start(self, e):
        for cp in self._copies(e, True):
            cp.start()

    def finish(self, e):
        for cp in self._copies(e, False):
            cp.wait_recv()
        for cp in self._copies(e, True):
            cp.wait_send()


def _cast_into_full(w, kind, idx, name):
    r, c = w.shape
    tr = _row_tile(r, c)
    nrb = r // tr

    def body(idx_ref, w_ref, o_ref):
        o_ref[...] = w_ref[...].astype(BF16)

    if kind == "col":
        full, out_map = (r, N_CHIPS * c), (lambda i, idx_ref: (i, idx_ref[1]))
    else:
        full, out_map = (N_CHIPS * r, c), (lambda i, idx_ref: (idx_ref[1] * nrb + i, 0))
    return pl.pallas_call(
        body, name=name, out_shape=jax.ShapeDtypeStruct(full, BF16),
        grid_spec=pltpu.PrefetchScalarGridSpec(
            num_scalar_prefetch=1, grid=(nrb,), in_specs=[pl.BlockSpec((tr, c), lambda i, idx_ref: (i, 0))],
            out_specs=pl.BlockSpec((tr, c), out_map)),
        compiler_params=_params(),
    )(idx, w)


def _matmul_gathering(a, placed, order, name):
    t, k = a.shape
    n = placed.shape[1]
    w = n // N_CHIPS
    tm, tn = _pick(t, _M_TILES), _pick(w, _N_TILES)
    ni, nj = t // tm, w // tn
    per_shard, total = ni * nj, N_CHIPS * ni * nj
    gather = _AllGather([placed], ["col"])

    def body(ord_ref, a_ref, w_own_ref, o_ref, w_ref, wbuf, fetch_sem, send, recv):
        s, i, j = pl.program_id(0), pl.program_id(1), pl.program_id(2)
        step = (s * ni + i) * nj + j
        e = _Env((w_own_ref,), (w_ref,), send, recv)

        def fetch(src, st):
            col = pl.multiple_of((ord_ref[st // per_shard] * nj + st % nj) * tn, LANE)
            return pltpu.make_async_copy(src.at[:, pl.ds(col, tn)], wbuf.at[st % 2], fetch_sem.at[st % 2])

        @pl.when(step == 0)
        def _():
            gather.start(e)
            fetch(w_own_ref, step).start()

        nxt = step + 1
        for kk, (cx, cy) in enumerate(e.chips):
            @pl.when(nxt == (kk + 1) * per_shard)
            def _():
                gather._ici(e, 0, kk, 2 * cx + cy).wait_recv()
                gather._d2d(e, 0, kk, e.c).start()
                gather._d2d(e, 0, kk, 1 - e.c).wait_recv()

        @pl.when(nxt < per_shard)
        def _():
            fetch(w_own_ref, nxt).start()

        @pl.when((nxt >= per_shard) & (nxt < total))
        def _():
            fetch(w_ref, nxt).start()

        fetch(w_ref, step).wait()
        o_ref[...] = jnp.dot(a_ref[...], wbuf[step % 2], preferred_element_type=F32)

        @pl.when(step == total - 1)
        def _():
            for kk in range(3):
                gather._ici(e, 0, kk, e.me).wait_send()
                gather._d2d(e, 0, kk, e.c).wait_send()

    z, full = pl.pallas_call(
        body, name=name, out_shape=(jax.ShapeDtypeStruct((t, n), F32), jax.ShapeDtypeStruct(placed.shape, placed.dtype)),
        grid_spec=pltpu.PrefetchScalarGridSpec(
            num_scalar_prefetch=1, grid=(N_CHIPS, ni, nj),
            in_specs=[pl.BlockSpec((tm, k), lambda s, i, j, ord_ref: (i, 0)), ANY],
            out_specs=(pl.BlockSpec((tm, tn), lambda s, i, j, ord_ref: (i, ord_ref[s] * nj + j)), ANY),
            scratch_shapes=[pltpu.VMEM((2, k, tn), placed.dtype), pltpu.SemaphoreType.DMA((2,)),
                            pltpu.SemaphoreType.DMA((gather.n_sems,)), pltpu.SemaphoreType.DMA((gather.n_sems,))]),
        input_output_aliases={2: 1}, compiler_params=_params(),
    )(order, a, placed)
    return z, full


def _add_own_half(g4, recv, idx, out_dtype, name):
    p, _, rh, n = g4.shape
    tr, tc = _tile2d(rh, n, 1024 * 1024)

    def body(idx_ref, g_ref, r_ref, o_ref):
        o_ref[...] = (g_ref[...] + r_ref[...]).astype(out_dtype)

    return pl.pallas_call(
        body, name=name, out_shape=jax.ShapeDtypeStruct((p, rh, n), out_dtype),
        grid_spec=pltpu.PrefetchScalarGridSpec(
            num_scalar_prefetch=1, grid=(p, rh // tr, n // tc),
            in_specs=[pl.BlockSpec((None, None, tr, tc), lambda q, i, j, idx_ref: (q, idx_ref[0], i, j)),
                      pl.BlockSpec((None, tr, tc), lambda q, i, j, idx_ref: (q, i, j))],
            out_specs=pl.BlockSpec((None, tr, tc), lambda q, i, j, idx_ref: (q, i, j))),
        compiler_params=_params(),
    )(idx, g4, recv)


def _sum_chips(own, kind, parts, idx, slots, to_all, name):
    _, rh, w = parts.shape
    tr, tc = _tile2d(rh, w, 512 * 1024)
    nrb, ncb = rh // tr, w // tc

    def body(idx_ref, own_ref, p0, p1, p2, o_ref):
        o_ref[...] = ((own_ref[...].astype(F32) + p0[...].astype(F32)) + p1[...].astype(F32)) + p2[...].astype(F32)

    if kind == "col":
        own_spec = pl.BlockSpec((None, tr, tc), lambda i, j, idx_ref: (0, i, idx_ref[1] * ncb + j))
    else:
        own_spec = pl.BlockSpec((None, tr, tc), lambda i, j, idx_ref: (idx_ref[1], i, j))
    if to_all:
        out_map = lambda i, j, idx_ref: ((2 * idx_ref[1] + idx_ref[0]) * nrb + i, j)
    else:
        out_map = lambda i, j, idx_ref: (idx_ref[0] * nrb + i, j)

    def part(k):
        return pl.BlockSpec((None, tr, tc), lambda i, j, idx_ref: (k, i, j))

    return pl.pallas_call(
        body, name=name, out_shape=jax.ShapeDtypeStruct((slots * rh, w), F32),
        grid_spec=pltpu.PrefetchScalarGridSpec(
            num_scalar_prefetch=1, grid=(nrb, ncb), in_specs=[own_spec, part(0), part(1), part(2)],
            out_specs=pl.BlockSpec((tr, tc), out_map)),
        compiler_params=_params(),
    )(idx, own, parts, parts, parts)


class _Reduce:
    def __init__(self, name, g, kind, idx, wire, to_all):
        r, c = g.shape
        self.name, self.kind, self.idx, self.wire, self.to_all = name, kind, idx, wire, to_all
        self.view = g.reshape(1, 2, r // 2, c) if kind == "col" else g.reshape(N_CHIPS, 2, r // (2 * N_CHIPS), c)

    def sibling(self):
        return _SiblingExchange([self.view])

    def got_sibling(self, outs):
        self.sum = _add_own_half(self.view, outs[0], self.idx, self.wire, "grad_chip_sum_" + self.name)

    def chips(self):
        return _ChipExchange([self.sum], [self.kind])

    def got_chips(self, outs):
        self.total = _sum_chips(self.sum, self.kind, outs[0], self.idx, 2 * N_CHIPS if self.to_all else 2,
                                self.to_all, "grad_total_" + self.name)


def _pack(arrays, rows):
    flat = jnp.concatenate([a.reshape(-1) for a in arrays])
    return jnp.pad(flat, (0, rows * SMALL_PACK_COLS - flat.shape[0])).reshape(rows, SMALL_PACK_COLS)


def _unpack(packed, shapes):
    flat = packed.reshape(-1)
    out, o = [], 0
    for shp in shapes:
        size = math.prod(shp)
        out.append(flat[o:o + size].reshape(shp))
        o += size
    return out


def _pack_rows(shapes):
    total = sum(math.prod(s) for s in shapes)
    unit = SMALL_PACK_COLS * N_CHIPS * 2 * SUBLANE
    return -(-total // unit) * (N_CHIPS * 2 * SUBLANE)


BIG = ("w_in", "w_attn_proj", "w_lru_proj", "w_out", "w_ffn_gate", "w_ffn_up", "w_ffn_down")
BIG_KIND = {"w_in": "col", "w_attn_proj": "row", "w_lru_proj": "row", "w_out": "row", "w_ffn_gate": "col",
            "w_ffn_up": "col", "w_ffn_down": "row"}
SMALL = ("norm1_g", "b_gates", "q_norm_g", "k_norm_g", "sinks", "conv_w", "conv_b", "w_rgate", "b_rgate",
         "w_igate", "b_igate", "lru_lambda", "norm2_g")
PACKED = tuple(n for n in SMALL if n not in ("w_rgate", "w_igate"))
WEIGHTS = ("norm1_g", "w_in", "b_gates", "q_norm_g", "k_norm_g", "sinks", "conv_w", "conv_b", "w_rgate", "b_rgate",
           "w_igate", "b_igate", "lru_lambda", "w_attn_proj", "w_lru_proj", "w_out", "norm2_g", "w_ffn_gate",
           "w_ffn_up", "w_ffn_down")


def kernel(x, positions, norm1_g, w_in, b_gates, q_norm_g, k_norm_g, sinks, conv_w, conv_b, w_rgate, b_rgate, w_igate, b_igate, lru_lambda, w_attn_proj, w_lru_proj, w_out, norm2_g, w_ffn_gate, w_ffn_up, w_ffn_down, loss_target, m_norm1_g, m_w_in, m_b_gates, m_q_norm_g, m_k_norm_g, m_sinks, m_conv_w, m_conv_b, m_w_rgate, m_b_rgate, m_w_igate, m_b_igate, m_lru_lambda, m_w_attn_proj, m_w_lru_proj, m_w_out, m_norm2_g, m_w_ffn_gate, m_w_ffn_up, m_w_ffn_down, v_norm1_g, v_w_in, v_b_gates, v_q_norm_g, v_k_norm_g, v_sinks, v_conv_w, v_conv_b, v_w_rgate, v_b_rgate, v_w_igate, v_b_igate, v_lru_lambda, v_w_attn_proj, v_w_lru_proj, v_w_out, v_norm2_g, v_w_ffn_gate, v_w_ffn_up, v_w_ffn_down):
    args = dict(locals())
    w = {n: args[n] for n in WEIGHTS}
    mom = {n: args["m_" + n] for n in WEIGHTS}
    var = {n: args["v_" + n] for n in WEIGHTS}

    t, d = x.shape[1], x.shape[2]
    hd = q_norm_g.shape[-1]
    nq = sinks.shape[-1]
    q_w = nq * hd
    d_rnn = conv_b.shape[-1]
    taps = conv_w.shape[1]
    n_blocks, bw = w_rgate.shape[1], w_rgate.shape[2]
    in_w = w_in.shape[-1] * N_CHIPS
    kv_w = (in_w - q_w - 2 * d_rnn - 2 * d) // 2
    kv = kv_w // hd
    grp = nq // kv
    u_off = q_w + 2 * kv_w
    gr_off = u_off + d_rnn
    ga_off = gr_off + d_rnn
    gw = bw * LANE // math.gcd(bw, LANE)
    chip = 2 * lax.axis_index("x") + lax.axis_index("y")
    idx = jnp.stack([lax.axis_index("c"), chip]).astype(jnp.int32)

    x2, tgt = x[0], loss_target[0]

    placed = {n: _cast_into_full(w[n][0], BIG_KIND[n], idx, "cast_" + n) for n in BIG}

    def gather(*names):
        return _AllGather([placed[n] for n in names], [BIG_KIND[n] for n in names])

    mx, my = lax.axis_index("x"), lax.axis_index("y")
    order = jnp.stack([chip, 2 * (1 - mx) + my, 2 * mx + (1 - my), 2 * (1 - mx) + (1 - my)]).astype(jnp.int32)
    conv_w_full = _gather_small(conv_w[0], "allgather_conv_w")
    conv_w_full = jnp.transpose(conv_w_full, (1, 0, 2)).reshape(taps, d_rnn)

    inv_freq = ROPE_THETA ** (-jnp.arange(0, hd // 4, 2, dtype=F32) / (hd // 4))
    ang = positions[0].astype(F32)[:, None] * inv_freq
    cos, sin = jnp.cos(ang), jnp.sin(ang)
    rest = hd - 2 * cos.shape[1]
    cos_t = jnp.concatenate([cos, cos, jnp.ones((t, rest), F32)], axis=1)
    sin_t = jnp.concatenate([-sin, sin, jnp.zeros((t, rest), F32)], axis=1)
    sinks1 = sinks[0]

    xn = _rms_fwd(x2, norm1_g, "rms1_fwd")
    z, win_f = _matmul_gathering(xn, placed["w_in"], order, "in_proj")
    attn, (wap_f, wlp_f, wout_f) = _attn_fwd(z, cos_t, sin_t, q_norm_g, k_norm_g, sinks1, kv, grp, hd, "attn_fwd",
                                             job=gather("w_attn_proj", "w_lru_proj", "w_out"))
    uc = _conv_fwd(z, u_off, d_rnn, conv_w_full, conv_b, "conv_fwd")
    rpre, ipre = _gates_fwd(uc, w_rgate[0], w_igate[0], gw, "gates_fwd")
    (hst, rec), (wg_f,) = _lru_fwd(uc, rpre, ipre, z, gr_off, b_rgate, b_igate, lru_lambda, "lru_fwd",
                                   job=gather("w_ffn_gate"))
    pa = _matmul(attn, wap_f, "nn", "attn_proj")
    plru = _matmul(rec, wlp_f, "nn", "lru_proj")
    merged = _merge_fwd(z, b_gates, pa, plru, ga_off, "merge_fwd")
    h1 = _matmul(merged, wout_f, "nn", "out_proj", add=x2)
    hn = _rms_fwd(h1, norm2_g, "rms2_fwd")
    gate, (wu_f,) = _matmul(hn, wg_f, "nn", "ffn_gate", job=gather("w_ffn_up"))
    up, (wd_f,) = _matmul(hn, wu_f, "nn", "ffn_up", job=gather("w_ffn_down"))
    act = _swiglu_fwd(gate, up, "swiglu_fwd")
    yout = _matmul(act, wd_f, "nn", "ffn_down", add=h1)
    dy, dy16, loss_part = _loss_head(yout, tgt, "loss_head")
    loss = lax.psum(loss_part[0, 0], ("x", "y", "c"))

    def reduction(n, g):
        return _Reduce(n, g, BIG_KIND[n], idx, BF16, False)

    r_wd = reduction("w_ffn_down", _matmul(act, dy16, "tn", "d_w_ffn_down"))
    dact, got = _matmul(dy16, wd_f, "nt", "d_act", job=r_wd.sibling())
    r_wd.got_sibling(got)
    (dgate, dup), got = _swiglu_bwd(dact, gate, up, "swiglu_bwd", job=r_wd.chips())
    r_wd.got_chips(got)
    r_wg = reduction("w_ffn_gate", _matmul(hn, dgate, "tn", "d_w_ffn_gate"))
    g_wu, got = _matmul(hn, dup, "tn", "d_w_ffn_up", job=r_wg.sibling())
    r_wg.got_sibling(got)
    r_wu = reduction("w_ffn_up", g_wu)
    both = _Jobs(r_wu.sibling(), r_wg.chips())
    dhn, got = _matmul(dgate, wg_f, "nt", "d_hn_gate", job=both)
    got_wu, got_wg = both.split(got)
    r_wu.got_sibling(got_wu)
    r_wg.got_chips(got_wg)
    dhn, got = _matmul(dup, wu_f, "nt", "d_hn_up", add=dhn, job=r_wu.chips())
    r_wu.got_chips(got)
    dh1, g_norm2, dh1_16 = _rms_bwd(dhn, h1, norm2_g, dy, "rms2_bwd", mxu_copy=True)
    r_wout = reduction("w_out", _matmul(merged, dh1_16, "tn", "d_w_out"))
    dmerged, got = _matmul(dh1_16, wout_f, "nt", "d_merged", job=r_wout.sibling())
    r_wout.got_sibling(got)
    (dpa, dpl, dga, dgl, g_ba, g_bl), got = _merge_bwd(dmerged, z, b_gates, pa, plru, ga_off, "merge_bwd",
                                                       job=r_wout.chips())
    r_wout.got_chips(got)
    r_wap = reduction("w_attn_proj", _matmul(attn, dpa, "tn", "d_w_attn_proj"))
    dattn, got = _matmul(dpa, wap_f, "nt", "d_attn", job=r_wap.sibling())
    r_wap.got_sibling(got)
    g_wlp, got = _matmul(rec, dpl, "tn", "d_w_lru_proj", job=r_wap.chips())
    r_wap.got_chips(got)
    r_wlp = reduction("w_lru_proj", g_wlp)
    drec, got = _matmul(dpl, wlp_f, "nt", "d_rec", job=r_wlp.sibling())
    r_wlp.got_sibling(got)
    (dgr, drp, dip, duc_direct, g_lam, g_br, g_bi), got = _lru_bwd(
        drec, hst, uc, rpre, ipre, z, gr_off, b_rgate, b_igate, lru_lambda, "lru_bwd", job=r_wlp.chips())
    r_wlp.got_chips(got)
    duc = _gates_bwd_x(duc_direct, drp, dip, w_rgate[0], w_igate[0], gw, "gates_bwd_x")
    g_wr = _gates_bwd_w(uc, drp, n_blocks, bw, gw, "gates_bwd_wr")
    g_wi = _gates_bwd_w(uc, dip, n_blocks, bw, gw, "gates_bwd_wi")
    du, g_convw, g_convb = _conv_bwd(duc, z, u_off, conv_w_full, "conv_bwd")
    dq, dk, dv, g_qg, g_kg, g_sinks = _attn_bwd(dattn, z, cos_t, sin_t, q_norm_g, k_norm_g, sinks1, kv, grp, hd,
                                                 "attn_bwd")
    dz = jnp.concatenate([dq, dk, dv, du, dgr, dga, dgl], axis=1)
    r_wr = _Reduce("w_rgate", g_wr.reshape(n_blocks * bw, bw), "row", idx, F32, True)
    r_wi = _Reduce("w_igate", g_wi.reshape(n_blocks * bw, bw), "row", idx, F32, True)
    both = _Jobs(r_wr.sibling(), r_wi.sibling())
    g_top, got = _matmul(xn, dz, "tn", "d_w_in_top", m_window=(0, d // 2), job=both)
    got_wr, got_wi = both.split(got)
    r_wr.got_sibling(got_wr)
    r_wi.got_sibling(got_wi)
    r_top = _Reduce("w_in_top", g_top, "col", idx, BF16, False)
    three = _Jobs(r_top.sibling(), r_wr.chips(), r_wi.chips())
    g_bot, got = _matmul(xn, dz, "tn", "d_w_in_bot", m_window=(d // 2, d // 2), job=three)
    got_top, got_wr, got_wi = three.split(got)
    r_top.got_sibling(got_top)
    r_wr.got_chips(got_wr)
    r_wi.got_chips(got_wi)
    r_bot = _Reduce("w_in_bot", g_bot, "col", idx, BF16, False)
    both = _Jobs(r_top.chips(), r_bot.sibling())
    dxn, got = _matmul(dz, win_f, "nt", "d_xn_a", m_window=(0, t // 2), into=(None, t), job=both)
    got_top, got_bot = both.split(got)
    r_top.got_chips(got_top)
    r_bot.got_sibling(got_bot)
    dxn, got = _matmul(dz, win_f, "nt", "d_xn_b", m_window=(t // 2, t // 2), into=(dxn, t), job=r_bot.chips())
    r_bot.got_chips(got)
    dx, g_norm1 = _rms_bwd(dxn, x2, norm1_g, dh1, "rms1_bwd")

    small_grads = {"norm1_g": g_norm1, "b_gates": jnp.concatenate([g_ba, g_bl], axis=1), "q_norm_g": g_qg,
                   "k_norm_g": g_kg, "sinks": g_sinks[:, :nq], "conv_w": g_convw, "conv_b": g_convb,
                   "b_rgate": g_br, "b_igate": g_bi, "lru_lambda": g_lam, "norm2_g": g_norm2}
    gshapes = [small_grads[n].shape for n in PACKED]
    r_small = _Reduce("small", _pack([small_grads[n] for n in PACKED], _pack_rows(gshapes)), "row", idx, F32, True)
    r_small.got_sibling(_run_exchange(r_small.sibling(), "grad_sibling_exchange_small"))
    r_small.got_chips(_run_exchange(r_small.chips(), "grad_chip_exchange_small"))
    sharded = [r_top, r_bot, r_wap, r_wlp, r_wout, r_wg, r_wu, r_wd]
    everywhere = [r_wr, r_wi, r_small]
    reduced = _run_exchange(_FinishExchange([r.total for r in sharded + everywhere],
                                            [False] * len(sharded) + [True] * len(everywhere)), "grad_finish_exchange")
    grads = dict(zip(BIG[1:], reduced[2:len(sharded)]))
    grads["w_in"] = jnp.concatenate(reduced[:2], axis=0)
    grads["w_rgate"], grads["w_igate"] = reduced[len(sharded)], reduced[len(sharded) + 1]
    small_full = dict(zip(PACKED, _unpack(reduced[-1], gshapes)))
    per = d_rnn // N_CHIPS
    small_full["conv_w"] = lax.dynamic_slice(small_full["conv_w"], (0, chip * per), (taps, per))
    grads.update(small_full)

    delta, new_m, new_v = {}, {}, {}
    for n in BIG + ("w_rgate", "w_igate"):
        as2d = (lambda a: a[0]) if n in BIG else (lambda a: a.reshape(n_blocks * bw, bw))
        delta[n], new_m[n], new_v[n] = _adamw(as2d(w[n]), grads[n], as2d(mom[n]), as2d(var[n]), "adamw_" + n)
    pshapes = [w[n].shape for n in PACKED]
    prows = _pack_rows(pshapes)
    pk = [_pack([src[n] for n in PACKED], prows) for src in (w, grads, mom, var)]
    for res, packed in zip((delta, new_m, new_v), _adamw(pk[0], pk[1], pk[2], pk[3], "adamw_small")):
        res.update(dict(zip(PACKED, _unpack(packed, pshapes))))

    outs = [loss, dx.reshape(x.shape)]
    for res in (grads, delta, new_m, new_v):
        outs += [res[n].reshape(w[n].shape) for n in WEIGHTS]
    return tuple(outs)


def _gather_small(shard, name):
    def body(s_ref, o_ref, send_sems, recv_sems):
        e = _Env((s_ref,), (o_ref,), send_sems, recv_sems)
        o_ref[e.me] = s_ref[...]
        for k, (cx, cy) in enumerate(e.chips):
            e.copy(s_ref, o_ref.at[e.me], k, (cx, cy, e.c)).start()
        for k, (cx, cy) in enumerate(e.chips):
            e.copy(s_ref, o_ref.at[2 * cx + cy], k, (cx, cy, e.c)).wait_recv()
        for k, (cx, cy) in enumerate(e.chips):
            e.copy(s_ref, o_ref.at[e.me], k, (cx, cy, e.c)).wait_send()

    vm = pl.BlockSpec(memory_space=pltpu.VMEM)
    return pl.pallas_call(body, name=name, out_shape=jax.ShapeDtypeStruct((N_CHIPS,) + shard.shape, shard.dtype),
                          in_specs=[vm], out_specs=vm,
                          scratch_shapes=[pltpu.SemaphoreType.DMA((3,)), pltpu.SemaphoreType.DMA((3,))])(shard)
```

```python
import functools
import math

import jax
import jax.numpy as jnp
from jax import lax
from jax.experimental import pallas as pl
from jax.experimental.pallas import tpu as pltpu

F32 = jnp.float32
BF16 = jnp.bfloat16
MESH = pl.DeviceIdType.MESH

WINDOW = 128
BLK = 128
ROPE_THETA = 500000.0
LRU_C = 8.0
EPS = 1e-6
NEG = -1e30
ADAM_LR = 0.001
ADAM_B1 = 0.9
ADAM_B2 = 0.999
ADAM_EPS = 1e-08
ADAM_WD = 0.01
ADAM_STEP = 10

VMEM_LIMIT_BYTES = 52 * 1024 * 1024
LANE = 128
SUBLANE = 8
N_CHIPS = 4
SMALL_PACK_COLS = 512


def _params(**kw):
    return pltpu.CompilerParams(vmem_limit_bytes=VMEM_LIMIT_BYTES, **kw)


def _pick(dim, cands):
    for c in cands:
        if dim % c == 0:
            return c
    return dim


def _sigmoid(x):
    return 0.5 * jnp.tanh(0.5 * x) + 0.5


ANY = pl.BlockSpec(memory_space=pl.ANY)


class _Env:
    def __init__(self, ins, outs, send, recv, sem0=0, place=None):
        self.ins, self.outs, self.send, self.recv, self.sem0 = ins, outs, send, recv, sem0
        self.x, self.y, self.c = place or (lax.axis_index("x"), lax.axis_index("y"), lax.axis_index("c"))
        self.me = 2 * self.x + self.y
        self.chips = [(1 - self.x, self.y), (self.x, 1 - self.y), (1 - self.x, 1 - self.y)]
        self.sibling = (self.x, self.y, 1 - self.c)

    def sub(self, i0, n_in, o0, n_out, sem0):
        return _Env(self.ins[i0:i0 + n_in], self.outs[o0:o0 + n_out], self.send, self.recv, self.sem0 + sem0,
                    (self.x, self.y, self.c))

    def copy(self, src, dst, sem, to):
        return pltpu.make_async_remote_copy(src_ref=src, dst_ref=dst, send_sem=self.send.at[self.sem0 + sem],
                                            recv_sem=self.recv.at[self.sem0 + sem], device_id=to, device_id_type=MESH)


class _Exchange:
    inputs, out_shapes, aliases, n_sems = (), (), {}, 0

    def start(self, e):
        raise NotImplementedError

    def finish(self, e):
        raise NotImplementedError


class _Jobs(_Exchange):
    def __init__(self, *jobs):
        self.jobs, self.inputs, self.out_shapes, self.aliases, self.n_sems, self.at = jobs, [], [], {}, 0, []
        for job in jobs:
            self.at.append((len(self.inputs), len(self.out_shapes), self.n_sems))
            self.aliases.update({len(self.inputs) + i: len(self.out_shapes) + o for i, o in job.aliases.items()})
            self.inputs += list(job.inputs)
            self.out_shapes += list(job.out_shapes)
            self.n_sems += job.n_sems

    def _each(self, e):
        for job, (i0, o0, s0) in zip(self.jobs, self.at):
            yield job, e.sub(i0, len(job.inputs), o0, len(job.out_shapes), s0)

    def split(self, outs):
        return [tuple(outs[o0:o0 + len(job.out_shapes)]) for job, (_, o0, _) in zip(self.jobs, self.at)]

    def start(self, e):
        for job, se in self._each(e):
            job.start(se)

    def finish(self, e):
        for job, se in self._each(e):
            job.finish(se)


def _call(body, name, out_shape, grid, in_specs, out_specs, args, scratch_shapes=(), job=None, aliases=None):
    aliases = dict(aliases or {})
    if job is None:
        return pl.pallas_call(body, name=name, out_shape=out_shape, grid=grid, in_specs=list(in_specs),
                              out_specs=out_specs, scratch_shapes=list(scratch_shapes), input_output_aliases=aliases,
                              compiler_params=_params())(*args), ()
    single = not isinstance(out_shape, (tuple, list))
    shapes = [out_shape] if single else list(out_shape)
    ospecs = [out_specs] if single else list(out_specs)
    n_in, n_out, n_scr = len(args), len(shapes), len(scratch_shapes)
    j_in, j_out = len(job.inputs), len(job.out_shapes)

    def hosted(*refs):
        ins, jins = refs[:n_in], refs[n_in:n_in + j_in]
        outs = refs[n_in + j_in:n_in + j_in + n_out]
        jouts = refs[n_in + j_in + n_out:n_in + j_in + n_out + j_out]
        rest = refs[n_in + j_in + n_out + j_out:]
        e = _Env(jins, jouts, rest[n_scr], rest[n_scr + 1])
        first = functools.reduce(jnp.logical_and, [pl.program_id(d) == 0 for d in range(len(grid))])
        last = functools.reduce(jnp.logical_and, [pl.program_id(d) == g - 1 for d, g in enumerate(grid)])

        @pl.when(first)
        def _():
            job.start(e)

        body(*ins, *outs, *rest[:n_scr])

        @pl.when(last)
        def _():
            job.finish(e)

    res = pl.pallas_call(
        hosted, name=name, out_shape=tuple(shapes + list(job.out_shapes)), grid=grid,
        in_specs=list(in_specs) + [ANY] * j_in, out_specs=tuple(ospecs + [ANY] * j_out),
        scratch_shapes=list(scratch_shapes) + [pltpu.SemaphoreType.DMA((job.n_sems,)),
                                               pltpu.SemaphoreType.DMA((job.n_sems,))],
        input_output_aliases={**aliases, **{n_in + i: n_out + o for i, o in job.aliases.items()}},
        compiler_params=_params())(*args, *job.inputs)
    return (res[0] if single else tuple(res[:n_out])), tuple(res[n_out:])


def _run_exchange(job, name):
    n_in, n_out = len(job.inputs), len(job.out_shapes)

    def body(*refs):
        e = _Env(refs[:n_in], refs[n_in:n_in + n_out], refs[n_in + n_out], refs[n_in + n_out + 1])
        job.start(e)
        job.finish(e)

    return pl.pallas_call(
        body, name=name, out_shape=tuple(job.out_shapes), in_specs=[ANY] * n_in, out_specs=tuple([ANY] * n_out),
        input_output_aliases=dict(job.aliases),
        scratch_shapes=[pltpu.SemaphoreType.DMA((job.n_sems,)), pltpu.SemaphoreType.DMA((job.n_sems,))],
    )(*job.inputs)


_M_TILES = (1024, 1408, 1280, 512, 256, 128)
_N_TILES = (1408, 1280, 1024, 640, 512, 256, 128)
MXU_FULL_ROWS = 1024
MATMUL_VMEM_BUDGET = 42 * 1024 * 1024
MXU_FLOPS_PER_HBM_BYTE = 500


def _matmul_tiles(m, n, k, sa, sb, so, has_add):
    best = None
    for tm in [c for c in _M_TILES if m % c == 0] or [m]:
        for tn in [c for c in _N_TILES if n % c == 0] or [n]:
            for nk in range(1, 17):
                tk = k // nk
                if k % nk or tk % LANE:
                    continue
                need = 2 * (tm * tk * sa + tk * tn * sb) + 2 * tm * tn * (so + (4 if has_add else 0))
                need += tm * tn * 4 if nk > 1 else 0
                fetched = tk * tn * sb + tm * tk * sa // (1 if nk > 1 else n // tn)
                if need > MATMUL_VMEM_BUDGET:
                    continue
                mxu_bound = fetched * MXU_FLOPS_PER_HBM_BYTE <= 2 * tm * tn * tk
                key = (mxu_bound, min(tm, MXU_FULL_ROWS), -nk, tn, tm)
                if best is None or key > best[0]:
                    best = (key, (tm, tn, tk))
    assert best is not None, (m, n, k)
    return best[1]


def _matmul(a, b, mode, name, add=None, out_dtype=F32, job=None, m_window=None, into=None):
    if mode == "nn":
        (m, k), (k2, n) = a.shape, b.shape
    elif mode == "nt":
        (m, k), (n, k2) = a.shape, b.shape
    else:
        (k, m), (k2, n) = a.shape, b.shape
    assert k == k2, (a.shape, b.shape, mode)
    m0, m = m_window or (0, m)
    tm, tn, tk = _matmul_tiles(math.gcd(m, m0) if m0 else m, n, k, a.dtype.itemsize, b.dtype.itemsize,
                               jnp.dtype(out_dtype).itemsize, add is not None)
    nk, mb0 = k // tk, m0 // tm
    if mode == "nn":
        a_spec = pl.BlockSpec((tm, tk), lambda i, j, kk: (mb0 + i, kk))
        b_spec = pl.BlockSpec((tk, tn), lambda i, j, kk: (kk, j))
        dims = (((1,), (0,)), ((), ()))
    elif mode == "nt":
        a_spec = pl.BlockSpec((tm, tk), lambda i, j, kk: (mb0 + i, kk))
        b_spec = pl.BlockSpec((tn, tk), lambda i, j, kk: (j, kk))
        dims = (((1,), (1,)), ((), ()))
    else:
        a_spec = pl.BlockSpec((tk, tm), lambda i, j, kk: (kk, mb0 + i))
        b_spec = pl.BlockSpec((tk, tn), lambda i, j, kk: (kk, j))
        dims = (((0,), (0,)), ((), ()))
    out_rows, ob0 = (into[1], mb0) if into is not None else (m, 0)
    o_spec = pl.BlockSpec((tm, tn), lambda i, j, kk: (ob0 + i, j))
    has_add = add is not None
    begun = into is not None and into[0] is not None

    def body(*refs):
        a_ref, b_ref = refs[:2]
        add_ref = refs[2] if has_add else None
        part = lax.dot_general(a_ref[...].astype(BF16), b_ref[...].astype(BF16), dims, preferred_element_type=F32)
        if nk == 1:
            o_ref = refs[-1]
            o_ref[...] = (part + add_ref[...] if has_add else part).astype(out_dtype)
            return
        o_ref, acc = refs[-2:]
        kk = pl.program_id(2)

        @pl.when(kk == 0)
        def _():
            acc[...] = part

        @pl.when(kk > 0)
        def _():
            acc[...] += part

        @pl.when(kk == nk - 1)
        def _():
            r = acc[...]
            if has_add:
                r = r + add_ref[...]
            o_ref[...] = r.astype(out_dtype)

    in_specs = [a_spec, b_spec] + ([pl.BlockSpec((tm, tn), lambda i, j, kk: (mb0 + i, j))] if has_add else [])
    args = (a, b) + ((add,) if has_add else ())
    aliases = None
    if begun:
        aliases = {len(args): 0}
        in_specs, args = in_specs + [ANY], args + (into[0],)
    res, extra = _call(body, name, jax.ShapeDtypeStruct((out_rows, n), out_dtype), (m // tm, n // tn, nk), in_specs,
                       o_spec, args, [pltpu.VMEM((tm, tn), F32)] if nk > 1 else [], job, aliases)
    return res if job is None else (res, extra)


def _row_tile(rows, cols, budget_elems=512 * 1024):
    cands = [c for c in (1024, 704, 512, 352, 256, 128, 64, 32, 16) if c * cols <= budget_elems]
    return _pick(rows, cands or (16,))


_EW_COLS = (1280, 1408, 1024, 640, 512, 256, 128)


def _tile2d(rows, cols, max_elems):
    tc = _pick(cols, _EW_COLS)
    return _row_tile(rows, tc, max_elems), tc


def _rms_fwd(x, g, name):
    t, d = x.shape
    tr = _row_tile(t, d)

    def body(x_ref, g_ref, o_ref):
        xv = x_ref[...]
        rstd = lax.rsqrt(jnp.mean(xv * xv, axis=-1, keepdims=True) + EPS)
        o_ref[...] = (xv * rstd * g_ref[...]).astype(BF16)

    spec = pl.BlockSpec((tr, d), lambda i: (i, 0))
    return pl.pallas_call(body, name=name, out_shape=jax.ShapeDtypeStruct((t, d), BF16), grid=(t // tr,),
                          in_specs=[spec, pl.BlockSpec((1, d), lambda i: (0, 0))], out_specs=spec,
                          compiler_params=_params())(x, g)


def _rms_bwd(dxn, x, g, resid, name, job=None, mxu_copy=False):
    t, d = x.shape
    tr = _row_tile(t, d, 256 * 1024)

    def body(dxn_ref, x_ref, g_ref, r_ref, dx_ref, dg_ref, *dx16_ref):
        @pl.when(pl.program_id(0) == 0)
        def _():
            dg_ref[...] = jnp.zeros_like(dg_ref)

        xv = x_ref[...]
        rstd = lax.rsqrt(jnp.mean(xv * xv, axis=-1, keepdims=True) + EPS)
        xhat = xv * rstd
        dy = dxn_ref[...]
        dg_ref[...] += jnp.sum(dy * xhat, axis=0, keepdims=True)
        dxhat = dy * g_ref[...]
        dx = r_ref[...] + rstd * (dxhat - xhat * jnp.mean(dxhat * xhat, axis=-1, keepdims=True))
        dx_ref[...] = dx
        if mxu_copy:
            dx16_ref[0][...] = dx.astype(BF16)

    spec = pl.BlockSpec((tr, d), lambda i: (i, 0))
    vec = pl.BlockSpec((1, d), lambda i: (0, 0))
    shapes = (jax.ShapeDtypeStruct((t, d), F32), jax.ShapeDtypeStruct((1, d), F32))
    shapes += (jax.ShapeDtypeStruct((t, d), BF16),) if mxu_copy else ()
    res, extra = _call(body, name, shapes, (t // tr,), [spec, spec, vec, spec],
                       (spec, vec) + ((spec,) if mxu_copy else ()), (dxn, x, g, resid), (), job)
    return res if job is None else (res, extra)


def _swiglu_fwd(gate, up, name):
    t, f = gate.shape
    tr, tc = _tile2d(t, f, 1024 * 1024)

    def body(g_ref, u_ref, o_ref):
        gv = g_ref[...]
        o_ref[...] = (gv * _sigmoid(gv) * u_ref[...]).astype(BF16)

    spec = pl.BlockSpec((tr, tc), lambda i, j: (i, j))
    return pl.pallas_call(body, name=name, out_shape=jax.ShapeDtypeStruct((t, f), BF16), grid=(t // tr, f // tc),
                          in_specs=[spec, spec], out_specs=spec, compiler_params=_params())(gate, up)


def _swiglu_bwd(dact, gate, up, name, job=None):
    t, f = gate.shape
    tr, tc = _tile2d(t, f, 768 * 1024)

    def body(d_ref, g_ref, u_ref, dg_ref, du_ref):
        gv, dv = g_ref[...], d_ref[...]
        sg = _sigmoid(gv)
        dg_ref[...] = (dv * u_ref[...] * (sg * (1.0 + gv * (1.0 - sg)))).astype(BF16)
        du_ref[...] = (dv * (gv * sg)).astype(BF16)

    spec = pl.BlockSpec((tr, tc), lambda i, j: (i, j))
    shp = jax.ShapeDtypeStruct((t, f), BF16)
    res, extra = _call(body, name, (shp, shp), (t // tr, f // tc), [spec, spec, spec], (spec, spec), (dact, gate, up),
                       (), job)
    return res if job is None else (res, extra)


def _merge_fwd(z, b_gates, pa, plru, ga_off, name):
    t, d = pa.shape
    cw = _pick(math.gcd(ga_off, d), (512, 256, 128))
    tr = _row_tile(t, cw, 256 * 1024)
    oa, ol, nd = ga_off // cw, (ga_off + d) // cw, d // cw

    def body(ga_ref, gl_ref, ba_ref, bl_ref, pa_ref, pl_ref, o_ref):
        sa = _sigmoid(ga_ref[...] + ba_ref[...])
        sl = _sigmoid(gl_ref[...] + bl_ref[...])
        o_ref[...] = (sa * pa_ref[...] + sl * pl_ref[...]).astype(BF16)

    blk = pl.BlockSpec((tr, cw), lambda i, j: (i, j))
    return pl.pallas_call(
        body, name=name, out_shape=jax.ShapeDtypeStruct((t, d), BF16), grid=(t // tr, nd),
        in_specs=[pl.BlockSpec((tr, cw), lambda i, j: (i, oa + j)), pl.BlockSpec((tr, cw), lambda i, j: (i, ol + j)),
                  pl.BlockSpec((1, cw), lambda i, j: (0, j)), pl.BlockSpec((1, cw), lambda i, j: (0, nd + j)),
                  blk, blk],
        out_specs=blk, compiler_params=_params(),
    )(z, z, b_gates, b_gates, pa, plru)


def _merge_bwd(dmerged, z, b_gates, pa, plru, ga_off, name, job=None):
    t, d = pa.shape
    cw = _pick(math.gcd(ga_off, d), (512, 256, 128))
    tr = _row_tile(t, cw, 256 * 1024)
    oa, ol, nd = ga_off // cw, (ga_off + d) // cw, d // cw

    def body(dm_ref, ga_ref, gl_ref, ba_ref, bl_ref, pa_ref, pl_ref, dpa_ref, dpl_ref, dga_ref, dgl_ref, sa_ref, sl_ref):
        @pl.when(pl.program_id(1) == 0)
        def _():
            sa_ref[...] = jnp.zeros_like(sa_ref)
            sl_ref[...] = jnp.zeros_like(sl_ref)

        dm = dm_ref[...]
        sa = _sigmoid(ga_ref[...] + ba_ref[...])
        sl = _sigmoid(gl_ref[...] + bl_ref[...])
        dpa_ref[...] = (dm * sa).astype(BF16)
        dpl_ref[...] = (dm * sl).astype(BF16)
        dga = dm * pa_ref[...] * (sa * (1.0 - sa))
        dgl = dm * pl_ref[...] * (sl * (1.0 - sl))
        dga_ref[...] = dga.astype(BF16)
        dgl_ref[...] = dgl.astype(BF16)
        sa_ref[...] += jnp.sum(dga, axis=0, keepdims=True)
        sl_ref[...] += jnp.sum(dgl, axis=0, keepdims=True)

    blk = pl.BlockSpec((tr, cw), lambda j, i: (i, j))
    vec = pl.BlockSpec((1, cw), lambda j, i: (0, j))
    big16, v32 = jax.ShapeDtypeStruct((t, d), BF16), jax.ShapeDtypeStruct((1, d), F32)
    res, extra = _call(
        body, name, (big16, big16, big16, big16, v32, v32), (nd, t // tr),
        [blk, pl.BlockSpec((tr, cw), lambda j, i: (i, oa + j)), pl.BlockSpec((tr, cw), lambda j, i: (i, ol + j)),
         vec, pl.BlockSpec((1, cw), lambda j, i: (0, nd + j)), blk, blk],
        (blk, blk, blk, blk, vec, vec), (dmerged, z, z, b_gates, b_gates, pa, plru), (), job)
    return res if job is None else (res, extra)


def _loss_head(y, target, name):
    t, d = y.shape
    tr = _row_tile(t, d, 256 * 1024)
    nt = t // tr

    def body(y_ref, t_ref, dy_ref, dy16_ref, loss_ref, acc):
        i = pl.program_id(0)

        @pl.when(i == 0)
        def _():
            acc[...] = jnp.zeros_like(acc)

        e = y_ref[...] - t_ref[...]
        dy = e * (1.0 / d)
        dy_ref[...] = dy
        dy16_ref[...] = dy.astype(BF16)
        acc[...] += jnp.sum(e * e, axis=0, keepdims=True)

        @pl.when(i == nt - 1)
        def _():
            loss_ref[...] = (0.5 / d) * jnp.sum(acc[...], axis=-1, keepdims=True)

    spec = pl.BlockSpec((tr, d), lambda i: (i, 0))
    return pl.pallas_call(
        body, name=name, out_shape=(jax.ShapeDtypeStruct((t, d), F32), jax.ShapeDtypeStruct((t, d), BF16),
                                    jax.ShapeDtypeStruct((1, 1), F32)),
        grid=(nt,), in_specs=[spec, spec], out_specs=(spec, spec, pl.BlockSpec((1, 1), lambda i: (0, 0))),
        scratch_shapes=[pltpu.VMEM((1, d), F32)], compiler_params=_params(),
    )(y, target)


def _adamw(w, g, m, v, name):
    r, c = w.shape
    tr, tc = _tile2d(r, c, 512 * 1024)
    c1 = 1.0 - ADAM_B1 ** ADAM_STEP
    c2 = 1.0 - ADAM_B2 ** ADAM_STEP

    def body(w_ref, g_ref, m_ref, v_ref, d_ref, nm_ref, nv_ref):
        gv = g_ref[...]
        mn = ADAM_B1 * m_ref[...] + (1.0 - ADAM_B1) * gv
        vn = ADAM_B2 * v_ref[...] + (1.0 - ADAM_B2) * (gv * gv)
        d_ref[...] = -ADAM_LR * ((mn / c1) / (jnp.sqrt(vn / c2) + ADAM_EPS) + ADAM_WD * w_ref[...])
        nm_ref[...] = mn
        nv_ref[...] = vn

    spec = pl.BlockSpec((tr, tc), lambda i, j: (i, j))
    shp = jax.ShapeDtypeStruct((r, c), F32)
    return pl.pallas_call(body, name=name, out_shape=(shp, shp, shp), grid=(r // tr, c // tc), in_specs=[spec] * 4,
                          out_specs=(spec, spec, spec), compiler_params=_params())(w, g, m, v)


def _swap_halves(v, lane, half):
    n = v.shape[-1]
    return jnp.where(lane < half, pltpu.roll(v, n - half, 1),
                     jnp.where(lane < 2 * half, pltpu.roll(v, half, 1), 0.0))


def _norm_fwd(xraw, g):
    rstd = lax.rsqrt(jnp.mean(xraw * xraw, axis=-1, keepdims=True) + EPS)
    xhat = xraw * rstd
    return xhat, rstd, xhat * g


def _norm_bwd(dy, xhat, rstd, g):
    dxhat = dy * g
    dx = rstd * (dxhat - xhat * jnp.mean(dxhat * xhat, axis=-1, keepdims=True))
    return dx, jnp.sum(dy * xhat, axis=0, keepdims=True)


def _attn_specs(nb, grp, hd, kv, clamp):
    qo, ko, vo = 0, (kv * grp), (kv * grp + kv)
    cur = (lambda i: jnp.minimum(i, nb - 1)) if clamp else (lambda i: i)
    prev = lambda i: jnp.maximum(cur(i) - 1, 0)
    zq = pl.BlockSpec((BLK, grp * hd), lambda h, i: (cur(i), h))
    kc = pl.BlockSpec((BLK, hd), lambda h, i: (cur(i), ko + h))
    kp = pl.BlockSpec((BLK, hd), lambda h, i: (prev(i), ko + h))
    vc = pl.BlockSpec((BLK, hd), lambda h, i: (cur(i), vo + h))
    vp = pl.BlockSpec((BLK, hd), lambda h, i: (prev(i), vo + h))
    tc = pl.BlockSpec((BLK, hd), lambda h, i: (cur(i), 0))
    tp = pl.BlockSpec((BLK, hd), lambda h, i: (prev(i), 0))
    gs = pl.BlockSpec((1, hd), lambda h, i: (0, 0))
    return zq, kc, kp, vc, vp, tc, tp, gs


def _attn_mask(i):
    qi = lax.broadcasted_iota(jnp.int32, (BLK, 2 * BLK), 0)
    kj = lax.broadcasted_iota(jnp.int32, (BLK, 2 * BLK), 1)
    rel = qi + BLK - kj
    return (rel >= 0) & (rel < WINDOW) & ((kj >= BLK) | (i > 0))


def _attn_fwd(z, cos_t, sin_t, qg, kg, sinks, kv, grp, hd, name, job=None):
    t = z.shape[0]
    nb = t // BLK
    half = hd // 8
    scale = 1.0 / math.sqrt(hd)
    zq, kc, kp, vc, vp, tc, tp, gs = _attn_specs(nb, grp, hd, kv, False)

    def body(sink_ref, zq_ref, kc_ref, kp_ref, vc_ref, vp_ref, cc_ref, sc_ref, cp_ref, sp_ref, qg_ref, kg_ref, o_ref):
        h, i = pl.program_id(0), pl.program_id(1)
        lane = lax.broadcasted_iota(jnp.int32, (BLK, hd), 1)

        def normrope(xraw, g, c, s):
            y = _norm_fwd(xraw, g)[2]
            return y * c + _swap_halves(y, lane, half) * s

        cc, sc = cc_ref[...], sc_ref[...]
        kcur = normrope(kc_ref[...], kg_ref[...], cc, sc)
        kprev = normrope(kp_ref[...], kg_ref[...], cp_ref[...], sp_ref[...])
        kk = jnp.concatenate([kprev, kcur], axis=0).astype(BF16)
        vv = jnp.concatenate([vp_ref[...], vc_ref[...]], axis=0).astype(BF16)
        mask = _attn_mask(i)
        for g in range(grp):
            q = normrope(zq_ref[:, g * hd:(g + 1) * hd], qg_ref[...], cc, sc).astype(BF16)
            s = lax.dot_general(q, kk, (((1,), (1,)), ((), ())), preferred_element_type=F32) * scale
            s = jnp.where(mask, s, NEG)
            sk = sink_ref[h * grp + g]
            mx = jnp.maximum(jnp.max(s, axis=-1, keepdims=True), sk)
            p = jnp.exp(s - mx)
            den = jnp.sum(p, axis=-1, keepdims=True) + jnp.exp(sk - mx)
            p = p * (1.0 / den)
            o_ref[:, g * hd:(g + 1) * hd] = jnp.dot(p.astype(BF16), vv, preferred_element_type=F32).astype(BF16)

    res, extra = _call(
        body, name, jax.ShapeDtypeStruct((t, kv * grp * hd), BF16), (kv, nb),
        [pl.BlockSpec(memory_space=pltpu.SMEM), zq, kc, kp, vc, vp, tc, tc, tp, tp, gs, gs],
        pl.BlockSpec((BLK, grp * hd), lambda h, i: (i, h)),
        (sinks, z, z, z, z, z, cos_t, sin_t, cos_t, sin_t, qg, kg), (), job)
    return res if job is None else (res, extra)


def _attn_bwd(dattn, z, cos_t, sin_t, qg, kg, sinks, kv, grp, hd, name, job=None):
    t = z.shape[0]
    nb = t // BLK
    half = hd // 8
    scale = 1.0 / math.sqrt(hd)
    zq, kc, kp, vc, vp, tc, tp, gs = _attn_specs(nb, grp, hd, kv, True)

    def body(sink_ref, zq_ref, kc_ref, kp_ref, vc_ref, vp_ref, cc_ref, sc_ref, cp_ref, sp_ref, qg_ref, kg_ref, do_ref,
             dq_ref, dk_ref, dv_ref, dqg_ref, dkg_ref, dsk_ref, dk_carry, dv_carry):
        h, i = pl.program_id(0), pl.program_id(1)
        lane = lax.broadcasted_iota(jnp.int32, (BLK, hd), 1)
        lane1 = lax.broadcasted_iota(jnp.int32, (1, LANE), 1)

        @pl.when((h == 0) & (i == 0))
        def _():
            dqg_ref[...] = jnp.zeros_like(dqg_ref)
            dkg_ref[...] = jnp.zeros_like(dkg_ref)
            dsk_ref[...] = jnp.zeros_like(dsk_ref)

        @pl.when(i == 0)
        def _():
            dk_carry[...] = jnp.zeros_like(dk_carry)
            dv_carry[...] = jnp.zeros_like(dv_carry)

        def rope(y, c, s):
            return y * c + _swap_halves(y, lane, half) * s

        def rope_bwd(dout, c, s):
            return dout * c + _swap_halves(dout * s, lane, half)

        @pl.when(i < nb)
        def _():
            cc, sc, cp, sp = cc_ref[...], sc_ref[...], cp_ref[...], sp_ref[...]
            qgv, kgv = qg_ref[...], kg_ref[...]
            xh_kc, rs_kc, y_kc = _norm_fwd(kc_ref[...], kgv)
            xh_kp, rs_kp, y_kp = _norm_fwd(kp_ref[...], kgv)
            kk = jnp.concatenate([rope(y_kp, cp, sp), rope(y_kc, cc, sc)], axis=0).astype(BF16)
            vv = jnp.concatenate([vp_ref[...], vc_ref[...]], axis=0).astype(BF16)
            mask = _attn_mask(i)
            dkk = jnp.zeros((2 * BLK, hd), F32)
            dvv = jnp.zeros((2 * BLK, hd), F32)
            dqg = jnp.zeros((1, hd), F32)
            dsk = jnp.zeros((1, LANE), F32)
            for g in range(grp):
                xh_q, rs_q, y_q = _norm_fwd(zq_ref[:, g * hd:(g + 1) * hd], qgv)
                q = rope(y_q, cc, sc).astype(BF16)
                s = lax.dot_general(q, kk, (((1,), (1,)), ((), ())), preferred_element_type=F32) * scale
                s = jnp.where(mask, s, NEG)
                sk = sink_ref[h * grp + g]
                mx = jnp.maximum(jnp.max(s, axis=-1, keepdims=True), sk)
                p = jnp.exp(s - mx)
                den = jnp.sum(p, axis=-1, keepdims=True) + jnp.exp(sk - mx)
                inv_den = 1.0 / den
                p = p * inv_den
                psink = jnp.exp(sk - mx) * inv_den
                dog = do_ref[:, g * hd:(g + 1) * hd].astype(BF16)
                dp = lax.dot_general(dog, vv, (((1,), (1,)), ((), ())), preferred_element_type=F32)
                rsum = jnp.sum(p * dp, axis=-1, keepdims=True)
                ds = (p * (dp - rsum) * scale).astype(BF16)
                dsk = dsk + jnp.where(lane1 == h * grp + g, jnp.sum(-psink * rsum, axis=0, keepdims=True), 0.0)
                dqn = jnp.dot(ds, kk, preferred_element_type=F32)
                dkk = dkk + lax.dot_general(ds, q, (((0,), (0,)), ((), ())), preferred_element_type=F32)
                dvv = dvv + lax.dot_general(p.astype(BF16), dog, (((0,), (0,)), ((), ())), preferred_element_type=F32)
                dxq, dg_q = _norm_bwd(rope_bwd(dqn, cc, sc), xh_q, rs_q, qgv)
                dq_ref[:, g * hd:(g + 1) * hd] = dxq.astype(BF16)
                dqg = dqg + dg_q
            dkp_raw, dg_kp = _norm_bwd(rope_bwd(dkk[:BLK], cp, sp), xh_kp, rs_kp, kgv)
            dkc_raw, dg_kc = _norm_bwd(rope_bwd(dkk[BLK:], cc, sc), xh_kc, rs_kc, kgv)
            dk_ref[...] = (dk_carry[...] + dkp_raw).astype(BF16)
            dv_ref[...] = (dv_carry[...] + dvv[:BLK]).astype(BF16)
            dk_carry[...] = dkc_raw
            dv_carry[...] = dvv[BLK:]
            dqg_ref[...] += dqg
            dkg_ref[...] += dg_kp + dg_kc
            dsk_ref[...] += dsk

        @pl.when(i == nb)
        def _():
            dk_ref[...] = dk_carry[...].astype(BF16)
            dv_ref[...] = dv_carry[...].astype(BF16)

    kvw = kv * hd
    vec = pl.BlockSpec((1, hd), lambda h, i: (0, 0))
    shifted = pl.BlockSpec((BLK, hd), lambda h, i: (jnp.maximum(i - 1, 0), h))
    res, extra = _call(
        body, name,
        (jax.ShapeDtypeStruct((t, kv * grp * hd), BF16), jax.ShapeDtypeStruct((t, kvw), BF16),
         jax.ShapeDtypeStruct((t, kvw), BF16), jax.ShapeDtypeStruct((1, hd), F32),
         jax.ShapeDtypeStruct((1, hd), F32), jax.ShapeDtypeStruct((1, LANE), F32)),
        (kv, nb + 1),
        [pl.BlockSpec(memory_space=pltpu.SMEM), zq, kc, kp, vc, vp, tc, tc, tp, tp, gs, gs,
         pl.BlockSpec((BLK, grp * hd), lambda h, i: (jnp.minimum(i, nb - 1), h))],
        (pl.BlockSpec((BLK, grp * hd), lambda h, i: (jnp.minimum(i, nb - 1), h)), shifted, shifted, vec, vec,
         pl.BlockSpec((1, LANE), lambda h, i: (0, 0))),
        (sinks, z, z, z, z, z, cos_t, sin_t, cos_t, sin_t, qg, kg, dattn),
        [pltpu.VMEM((BLK, hd), F32), pltpu.VMEM((BLK, hd), F32)], job)
    return res if job is None else (res, extra)


def _window(rows, cb, c0, row_of, col_of):
    assert rows % SUBLANE == 0 and cb % LANE == 0 and c0 % LANE == 0, (rows, cb, c0)
    return pl.BlockSpec((pl.Element(rows), pl.Element(cb)),
                        lambda *g: (pl.multiple_of(row_of(*g) * rows, SUBLANE), pl.multiple_of(c0 + col_of(*g) * cb, LANE)))


def _conv_fwd(z, c0, c, w, b, name):
    t = z.shape[0]
    taps = w.shape[0]
    cb = _pick(c, (1408, 1024, 512, 256, 128))
    tr = _row_tile(t, cb, 256 * 1024)
    hb = tr // SUBLANE

    def body(u_ref, halo_ref, w_ref, b_ref, o_ref):
        i = pl.program_id(0)
        x = u_ref[...]
        acc = b_ref[...] + w_ref[taps - 1:taps, :] * x
        for k in range(taps - 1):
            acc = acc + w_ref[k:k + 1, :] * pltpu.roll(x, taps - 1 - k, 0)
        o_ref[...] = acc
        row = lax.broadcasted_iota(jnp.int32, (SUBLANE, cb), 0)
        hp = jnp.where(i > 0, halo_ref[...], 0.0)
        x8 = u_ref[0:SUBLANE, :]
        acc8 = b_ref[...] + w_ref[taps - 1:taps, :] * x8
        for k in range(taps - 1):
            s = taps - 1 - k
            acc8 = acc8 + w_ref[k:k + 1, :] * jnp.where(row < s, pltpu.roll(hp, s, 0), pltpu.roll(x8, s, 0))
        o_ref[0:SUBLANE, :] = acc8

    blk = pl.BlockSpec((tr, cb), lambda i, j: (i, j))
    return pl.pallas_call(
        body, name=name, out_shape=jax.ShapeDtypeStruct((t, c), F32), grid=(t // tr, c // cb),
        in_specs=[_window(tr, cb, c0, lambda i, j: i, lambda i, j: j),
                  _window(SUBLANE, cb, c0, lambda i, j: jnp.maximum(i * hb - 1, 0), lambda i, j: j),
                  pl.BlockSpec((taps, cb), lambda i, j: (0, j)), pl.BlockSpec((1, cb), lambda i, j: (0, j))],
        out_specs=blk, compiler_params=_params(),
    )(z, z, w, b)


def _conv_bwd(duc, z, c0, w, name):
    t, c = duc.shape
    taps = w.shape[0]
    cb = _pick(c, (1408, 1024, 512, 256, 128))
    tr = _row_tile(t, cb, 256 * 1024)
    hb, nt = tr // SUBLANE, t // tr

    def body(g_ref, gnext_ref, u_ref, uprev_ref, w_ref, du16_ref, dw_ref, db_ref, du_ref):
        i = pl.program_id(1)

        @pl.when(i == 0)
        def _():
            dw_ref[...] = jnp.zeros_like(dw_ref)
            db_ref[...] = jnp.zeros_like(db_ref)

        row = lax.broadcasted_iota(jnp.int32, (SUBLANE, cb), 0)
        g, x = g_ref[...], u_ref[...]
        du = w_ref[taps - 1:taps, :] * g
        for k in range(taps - 1):
            du = du + w_ref[k:k + 1, :] * pltpu.roll(g, tr - (taps - 1 - k), 0)
        du_ref[...] = du
        hn = jnp.where(i < nt - 1, gnext_ref[...], 0.0)
        g8 = g_ref[tr - SUBLANE:tr, :]
        du8 = w_ref[taps - 1:taps, :] * g8
        for k in range(taps - 1):
            s = taps - 1 - k
            du8 = du8 + w_ref[k:k + 1, :] * jnp.where(row >= SUBLANE - s, pltpu.roll(hn, SUBLANE - s, 0),
                                                     pltpu.roll(g8, SUBLANE - s, 0))
        du_ref[tr - SUBLANE:tr, :] = du8
        du16_ref[...] = du_ref[...].astype(BF16)

        hp = jnp.where(i > 0, uprev_ref[...], 0.0)
        xl8, gf8 = u_ref[tr - SUBLANE:tr, :], g_ref[0:SUBLANE, :]
        db_ref[...] += jnp.sum(g, axis=0, keepdims=True)
        dw_ref[taps - 1:taps, :] += jnp.sum(g * x, axis=0, keepdims=True)
        for k in range(taps - 1):
            s = taps - 1 - k
            fix = jnp.where(row < s, pltpu.roll(hp, s, 0) - pltpu.roll(xl8, s, 0), 0.0)
            dw_ref[k:k + 1, :] += (jnp.sum(g * pltpu.roll(x, s, 0), axis=0, keepdims=True)
                                   + jnp.sum(gf8 * fix, axis=0, keepdims=True))

    blk = pl.BlockSpec((tr, cb), lambda j, i: (i, j))
    nh = t // SUBLANE
    return pl.pallas_call(
        body, name=name,
        out_shape=(jax.ShapeDtypeStruct((t, c), BF16), jax.ShapeDtypeStruct((taps, c), F32),
                   jax.ShapeDtypeStruct((1, c), F32)),
        grid=(c // cb, nt),
        in_specs=[blk, pl.BlockSpec((SUBLANE, cb), lambda j, i: (jnp.minimum((i + 1) * hb, nh - 1), j)),
                  _window(tr, cb, c0, lambda j, i: i, lambda j, i: j),
                  _window(SUBLANE, cb, c0, lambda j, i: jnp.maximum(i * hb - 1, 0), lambda j, i: j),
                  pl.BlockSpec((taps, cb), lambda j, i: (0, j))],
        out_specs=(blk, pl.BlockSpec((taps, cb), lambda j, i: (0, j)), pl.BlockSpec((1, cb), lambda j, i: (0, j))),
        scratch_shapes=[pltpu.VMEM((tr, cb), F32)], compiler_params=_params(),
    )(duc, duc, z, z, w)


def _gates_fwd(uc, wr, wi, gw, name):
    t, c = uc.shape
    n, bw, _ = wr.shape
    per, ng = gw // bw, c // gw
    tr = _pick(t, (512, 256, 128))

    def body(u_ref, wr_ref, wi_ref, r_ref, i_ref):
        for b in range(per):
            cols = slice(b * bw, (b + 1) * bw)
            a = u_ref[:, cols].astype(BF16)
            r_ref[:, cols] = jnp.dot(a, wr_ref[b].astype(BF16), preferred_element_type=F32)
            i_ref[:, cols] = jnp.dot(a, wi_ref[b].astype(BF16), preferred_element_type=F32)

    blk = pl.BlockSpec((tr, gw), lambda h, i: (i, h))
    wsp = pl.BlockSpec((per, bw, bw), lambda h, i: (h, 0, 0))
    shp = jax.ShapeDtypeStruct((t, c), F32)
    return pl.pallas_call(body, name=name, out_shape=(shp, shp), grid=(ng, t // tr), in_specs=[blk, wsp, wsp],
                          out_specs=(blk, blk), compiler_params=_params())(uc, wr, wi)


def _gates_bwd_x(duc, drp, dip, wr, wi, gw, name):
    t, c = duc.shape
    n, bw, _ = wr.shape
    per, ng = gw // bw, c // gw
    tr = _pick(t, (512, 256, 128))
    dims = (((1,), (1,)), ((), ()))

    def body(d_ref, r_ref, i_ref, wr_ref, wi_ref, o_ref):
        for b in range(per):
            cols = slice(b * bw, (b + 1) * bw)
            o_ref[:, cols] = (
                d_ref[:, cols]
                + lax.dot_general(r_ref[:, cols].astype(BF16), wr_ref[b].astype(BF16), dims, preferred_element_type=F32)
                + lax.dot_general(i_ref[:, cols].astype(BF16), wi_ref[b].astype(BF16), dims, preferred_element_type=F32))

    blk = pl.BlockSpec((tr, gw), lambda h, i: (i, h))
    wsp = pl.BlockSpec((per, bw, bw), lambda h, i: (h, 0, 0))
    return pl.pallas_call(body, name=name, out_shape=jax.ShapeDtypeStruct((t, c), F32), grid=(ng, t // tr),
                          in_specs=[blk, blk, blk, wsp, wsp], out_specs=blk, compiler_params=_params())(duc, drp, dip, wr, wi)


def _gates_bwd_w(uc, dpre, n, bw, gw, name):
    t, c = uc.shape
    per, ng = gw // bw, c // gw
    tk = _pick(t, (512, 256, 128))
    dims = (((0,), (0,)), ((), ()))

    def body(u_ref, d_ref, o_ref):
        @pl.when(pl.program_id(1) == 0)
        def _():
            o_ref[...] = jnp.zeros_like(o_ref)

        for b in range(per):
            cols = slice(b * bw, (b + 1) * bw)
            o_ref[b] += lax.dot_general(u_ref[:, cols].astype(BF16), d_ref[:, cols].astype(BF16), dims,
                                        preferred_element_type=F32)

    blk = pl.BlockSpec((tk, gw), lambda h, i: (i, h))
    return pl.pallas_call(body, name=name, out_shape=jax.ShapeDtypeStruct((n, bw, bw), F32), grid=(ng, t // tk),
                          in_specs=[blk, blk], out_specs=pl.BlockSpec((per, bw, bw), lambda h, i: (h, 0, 0)),
                          compiler_params=_params())(uc, dpre)


def _softplus(x):
    return jnp.maximum(x, 0.0) + jnp.log(1.0 + jnp.exp(-jnp.abs(x)))


_GELU_C = math.sqrt(2.0 / math.pi)


def _gelu_parts(x):
    inner = _GELU_C * (x + 0.044715 * (x * x * x))
    th = jnp.tanh(inner)
    gelu = 0.5 * x * (1.0 + th)
    dgelu = 0.5 * (1.0 + th) + 0.5 * x * (1.0 - th * th) * (_GELU_C * (1.0 + 3.0 * 0.044715 * (x * x)))
    return gelu, dgelu


def _lru_gate_values(rpre, ipre, br, bi, sp):
    r = _sigmoid(rpre + br)
    ig = _sigmoid(ipre + bi)
    log_a = -LRU_C * r * sp
    a = jnp.exp(log_a)
    e2 = jnp.tanh(-log_a) * (1.0 + a * a)
    inv = lax.rsqrt(jnp.maximum(e2, 1e-30))
    return r, ig, a, e2 * inv, inv


def _lru_fwd(uc, rpre, ipre, z, gr0, br, bi, lam, name, job=None):
    t, c = uc.shape
    cb = _pick(c, (1408, 1024, 512, 256, 128))
    tb = _pick(t, (512, 256, 128))
    ntile = tb // SUBLANE

    def body(uc_ref, r_ref, i_ref, gr_ref, br_ref, bi_ref, lam_ref, h_ref, rec16_ref, carry, rec_ref):
        @pl.when(pl.program_id(1) == 0)
        def _():
            carry[...] = jnp.zeros_like(carry)

        sp = _softplus(-lam_ref[...])
        br, bi = br_ref[...], bi_ref[...]
        row = lax.broadcasted_iota(jnp.int32, (SUBLANE, cb), 0)

        def tile(k, c_in):
            sl = pl.ds(pl.multiple_of(k * SUBLANE, SUBLANE), SUBLANE)
            ucv = uc_ref[sl, :]
            _, ig, a, mult, _ = _lru_gate_values(r_ref[sl, :], i_ref[sl, :], br, bi, sp)
            b = mult * (ig * ucv)
            for d in (1, 2, 4):
                a_s = jnp.where(row >= d, pltpu.roll(a, d, 0), 1.0)
                b_s = jnp.where(row >= d, pltpu.roll(b, d, 0), 0.0)
                b = a * b_s + b
                a = a * a_s
            hv = b + a * c_in
            h_ref[sl, :] = hv
            rec_ref[sl, :] = hv * _gelu_parts(gr_ref[sl, :])[0]
            return hv[SUBLANE - 1:SUBLANE, :]

        c_out = lax.fori_loop(0, ntile, tile, carry[0:1, :])
        carry[...] = jnp.broadcast_to(c_out, (SUBLANE, cb))
        rec16_ref[...] = rec_ref[...].astype(BF16)

    blk = pl.BlockSpec((tb, cb), lambda j, i: (i, j))
    vec = pl.BlockSpec((1, cb), lambda j, i: (0, j))
    res, extra = _call(body, name, (jax.ShapeDtypeStruct((t, c), F32), jax.ShapeDtypeStruct((t, c), BF16)),
                       (c // cb, t // tb),
                       [blk, blk, blk, _window(tb, cb, gr0, lambda j, i: i, lambda j, i: j), vec, vec, vec], (blk, blk),
                       (uc, rpre, ipre, z, br, bi, lam), [pltpu.VMEM((SUBLANE, cb), F32), pltpu.VMEM((tb, cb), F32)], job)
    return res if job is None else (res, extra)


def _lru_bwd(drec, hst, uc, rpre, ipre, z, gr0, br, bi, lam, name, job=None):
    t, c = uc.shape
    cb = _pick(c, (1408, 1024, 512, 256, 128))
    tb = _pick(t, (256, 128))
    ntile, nt, hb = tb // SUBLANE, t // tb, tb // SUBLANE

    def body(drec_ref, h_ref, hprev_ref, uc_ref, r_ref, i_ref, gr_ref, br_ref, bi_ref, lam_ref,
             dgr16_ref, drp_ref, dip_ref, duc_ref, dlam_ref, dbr_ref, dbi_ref, carry, dgr_ref):
        step = pl.program_id(1)
        first_block = step == nt - 1

        @pl.when(step == 0)
        def _():
            carry[...] = jnp.zeros_like(carry)
            dlam_ref[...] = jnp.zeros_like(dlam_ref)
            dbr_ref[...] = jnp.zeros_like(dbr_ref)
            dbi_ref[...] = jnp.zeros_like(dbi_ref)

        lam = lam_ref[...]
        sp = _softplus(-lam)
        br, bi = br_ref[...], bi_ref[...]
        row = lax.broadcasted_iota(jnp.int32, (SUBLANE, cb), 0)
        halo = jnp.where(first_block, 0.0, hprev_ref[...])

        def tile(kk, state):
            c_p, acc_sp, acc_br, acc_bi = state
            k = ntile - 1 - kk
            sl = pl.ds(pl.multiple_of(k * SUBLANE, SUBLANE), SUBLANE)
            slp = pl.ds(pl.multiple_of(jnp.maximum(k - 1, 0) * SUBLANE, SUBLANE), SUBLANE)
            ucv = uc_ref[sl, :]
            r, ig, a, mult, inv_mult = _lru_gate_values(r_ref[sl, :], i_ref[sl, :], br, bi, sp)
            hv = h_ref[sl, :]
            below = jnp.where(k > 0, h_ref[slp, :], halo)
            hprev = jnp.where(row == 0, pltpu.roll(below, 1, 0), pltpu.roll(hv, 1, 0))
            gelu, dgelu = _gelu_parts(gr_ref[sl, :])
            drec = drec_ref[sl, :]
            dh = drec * gelu
            dgr_ref[sl, :] = drec * hv * dgelu
            pa, pb = a, a * dh
            for d in (1, 2, 4):
                a_s = jnp.where(row < SUBLANE - d, pltpu.roll(pa, SUBLANE - d, 0), 1.0)
                b_s = jnp.where(row < SUBLANE - d, pltpu.roll(pb, SUBLANE - d, 0), 0.0)
                pb = pa * b_s + pb
                pa = pa * a_s
            pv = pb + pa * c_p
            gt = dh + jnp.where(row == SUBLANE - 1, c_p, pltpu.roll(pv, SUBLANE - 1, 0))
            da = gt * hprev
            duc_ref[sl, :] = gt * mult * ig
            dmult = gt * ig * ucv
            dig = gt * mult * ucv
            dla = da * a - jnp.where(mult > 0.0, dmult * (a * a) * inv_mult, 0.0)
            drp = dla * (-LRU_C * sp) * (r * (1.0 - r))
            dip = dig * (ig * (1.0 - ig))
            drp_ref[sl, :] = drp
            dip_ref[sl, :] = dip
            return pv[0:1, :], acc_sp + dla * (-LRU_C * r), acc_br + drp, acc_bi + dip

        zero = jnp.zeros((SUBLANE, cb), F32)
        c_out, acc_sp, acc_br, acc_bi = lax.fori_loop(0, ntile, tile, (carry[0:1, :], zero, zero, zero))
        carry[...] = jnp.broadcast_to(c_out, (SUBLANE, cb))
        dlam_ref[...] += jnp.sum(acc_sp, axis=0, keepdims=True) * (-_sigmoid(-lam))
        dbr_ref[...] += jnp.sum(acc_br, axis=0, keepdims=True)
        dbi_ref[...] += jnp.sum(acc_bi, axis=0, keepdims=True)
        dgr16_ref[...] = dgr_ref[...].astype(BF16)

    blk = pl.BlockSpec((tb, cb), lambda j, i: (nt - 1 - i, j))
    vec = pl.BlockSpec((1, cb), lambda j, i: (0, j))
    halo_spec = pl.BlockSpec((SUBLANE, cb), lambda j, i: (jnp.maximum((nt - 1 - i) * hb - 1, 0), j))
    big, small = jax.ShapeDtypeStruct((t, c), F32), jax.ShapeDtypeStruct((1, c), F32)
    res, extra = _call(
        body, name, (jax.ShapeDtypeStruct((t, c), BF16), big, big, big, small, small, small), (c // cb, nt),
        [blk, blk, halo_spec, blk, blk, blk, _window(tb, cb, gr0, lambda j, i: nt - 1 - i, lambda j, i: j),
         vec, vec, vec], (blk, blk, blk, blk, vec, vec, vec),
        (drec, hst, hst, uc, rpre, ipre, z, br, bi, lam),
        [pltpu.VMEM((SUBLANE, cb), F32), pltpu.VMEM((tb, cb), F32)], job)
    return res if job is None else (res, extra)


def _shard_region(ref, kind, chip, half, rh, width):
    if kind == "col":
        return ref.at[pl.ds(half * rh, rh), pl.ds(chip * width, width)]
    return ref.at[pl.ds(chip * (2 * rh) + half * rh, rh), :]


class _AllGather(_Exchange):
    def __init__(self, fulls, kinds):
        self.inputs, self.kinds = list(fulls), kinds
        self.out_shapes = [jax.ShapeDtypeStruct(f.shape, f.dtype) for f in fulls]
        self.aliases = {a: a for a in range(len(fulls))}
        self.n_sems = 6 * len(fulls)
        self.geo = [(f.shape[0] // 2, f.shape[1] // N_CHIPS) if k == "col" else (f.shape[0] // (2 * N_CHIPS), f.shape[1])
                    for f, k in zip(fulls, kinds)]

    def _region(self, ref, a, chip, half):
        return _shard_region(ref, self.kinds[a], chip, half, *self.geo[a])

    def _ici(self, e, a, k, chip):
        cx, cy = e.chips[k]
        return e.copy(self._region(e.ins[a], a, chip, e.c), self._region(e.outs[a], a, chip, e.c), a * 6 + k,
                      (cx, cy, e.c))

    def _d2d(self, e, a, k, half):
        cx, cy = e.chips[k]
        region = self._region(e.outs[a], a, 2 * cx + cy, half)
        return e.copy(region, region, a * 6 + 3 + k, e.sibling)

    def start(self, e):
        for a in range(len(self.inputs)):
            for k in range(3):
                self._ici(e, a, k, e.me).start()

    def finish(self, e):
        n = len(self.inputs)
        for a in range(n):
            for k, (cx, cy) in enumerate(e.chips):
                self._ici(e, a, k, 2 * cx + cy).wait_recv()
                self._d2d(e, a, k, e.c).start()
        for a in range(n):
            for k in range(3):
                self._d2d(e, a, k, 1 - e.c).wait_recv()
        for a in range(n):
            for k in range(3):
                self._ici(e, a, k, e.me).wait_send()
                self._d2d(e, a, k, e.c).wait_send()


class _SiblingExchange(_Exchange):
    def __init__(self, grads):
        self.inputs = list(grads)
        self.out_shapes = [jax.ShapeDtypeStruct((g.shape[0],) + g.shape[2:], g.dtype) for g in grads]
        self.n_sems = len(grads)

    def _copy(self, e, a):
        return e.copy(e.ins[a].at[:, 1 - e.c], e.outs[a], a, e.sibling)

    def start(self, e):
        for a in range(len(self.inputs)):
            self._copy(e, a).start()

    def finish(self, e):
        for a in range(len(self.inputs)):
            self._copy(e, a).wait()


def _piece(ref, kind, chip, width):
    if kind == "col":
        return ref.at[0, :, pl.ds(chip * width, width)]
    return ref.at[chip]


class _ChipExchange(_Exchange):
    def __init__(self, sums, kinds):
        self.inputs, self.kinds = list(sums), kinds
        self.widths = [s.shape[2] // N_CHIPS if k == "col" else s.shape[2] for s, k in zip(sums, kinds)]
        self.out_shapes = [jax.ShapeDtypeStruct((3, s.shape[1], w), s.dtype) for s, w in zip(sums, self.widths)]
        self.n_sems = 3 * len(sums)

    def _copy(self, e, a, k, chip):
        cx, cy = e.chips[k]
        return e.copy(_piece(e.ins[a], self.kinds[a], chip, self.widths[a]), e.outs[a].at[k], a * 3 + k, (cx, cy, e.c))

    def start(self, e):
        for a in range(len(self.inputs)):
            for k, (cx, cy) in enumerate(e.chips):
                self._copy(e, a, k, 2 * cx + cy).start()

    def finish(self, e):
        for a in range(len(self.inputs)):
            for k, (cx, cy) in enumerate(e.chips):
                self._copy(e, a, k, 2 * cx + cy).wait()


class _FinishExchange(_Exchange):
    def __init__(self, finals, to_all):
        self.inputs, self.to_all = list(finals), list(to_all)
        self.out_shapes = [jax.ShapeDtypeStruct(f.shape, f.dtype) for f in finals]
        self.aliases = {a: a for a in range(len(finals))}
        self.first_sem, self.n_sems = [], 0
        for all8 in self.to_all:
            self.first_sem.append(self.n_sems)
            self.n_sems += 7 if all8 else 1
        self.rel = [(fx, fy, fc) for fx in (0, 1) for fy in (0, 1) for fc in (0, 1)][1:]

    def _copies(self, e, mine):
        for a, all8 in enumerate(self.to_all):
            src = e.ins[a] if mine else e.outs[a]
            if not all8:
                rh = self.inputs[a].shape[0] // 2
                rows = pl.ds((e.c if mine else 1 - e.c) * rh, rh)
                yield e.copy(src.at[rows, :], e.outs[a].at[rows, :], self.first_sem[a], e.sibling)
                continue
            rh = self.inputs[a].shape[0] // (2 * N_CHIPS)
            for r, (fx, fy, fc) in enumerate(self.rel):
                px, py, pc = (1 - e.x if fx else e.x), (1 - e.y if fy else e.y), (1 - e.c if fc else e.c)
                rows = pl.ds(((2 * e.me + e.c) if mine else (2 * (2 * px + py) + pc)) * rh, rh)
                yield e.copy(src.at[rows, :], e.outs[a].at[rows, :], self.first_sem[a] + r, (px, py, pc))

    def start(self, e):
        for cp in self._copies(e, True):
            cp.start()

    def finish(self, e):
        for cp in self._copies(e, False):
            cp.wait_recv()
        for cp in self._copies(e, True):
            cp.wait_send()


def _cast_into_full(w, kind, idx, name):
    r, c = w.shape
    tr = _row_tile(r, c)
    nrb = r // tr

    def body(idx_ref, w_ref, o_ref):
        o_ref[...] = w_ref[...].astype(BF16)

    if kind == "col":
        full, out_map = (r, N_CHIPS * c), (lambda i, idx_ref: (i, idx_ref[1]))
    else:
        full, out_map = (N_CHIPS * r, c), (lambda i, idx_ref: (idx_ref[1] * nrb + i, 0))
    return pl.pallas_call(
        body, name=name, out_shape=jax.ShapeDtypeStruct(full, BF16),
        grid_spec=pltpu.PrefetchScalarGridSpec(
            num_scalar_prefetch=1, grid=(nrb,), in_specs=[pl.BlockSpec((tr, c), lambda i, idx_ref: (i, 0))],
            out_specs=pl.BlockSpec((tr, c), out_map)),
        compiler_params=_params(),
    )(idx, w)


def _matmul_gathering(a, placed, order, name):
    t, k = a.shape
    n = placed.shape[1]
    w = n // N_CHIPS
    tm, tn = _pick(t, _M_TILES), _pick(w, _N_TILES)
    ni, nj = t // tm, w // tn
    per_shard, total = ni * nj, N_CHIPS * ni * nj
    gather = _AllGather([placed], ["col"])

    def body(ord_ref, a_ref, w_own_ref, o_ref, w_ref, wbuf, fetch_sem, send, recv):
        s, i, j = pl.program_id(0), pl.program_id(1), pl.program_id(2)
        step = (s * ni + i) * nj + j
        e = _Env((w_own_ref,), (w_ref,), send, recv)

        def fetch(src, st):
            col = pl.multiple_of((ord_ref[st // per_shard] * nj + st % nj) * tn, LANE)
            return pltpu.make_async_copy(src.at[:, pl.ds(col, tn)], wbuf.at[st % 2], fetch_sem.at[st % 2])

        @pl.when(step == 0)
        def _():
            gather.start(e)
            fetch(w_own_ref, step).start()

        nxt = step + 1
        for kk, (cx, cy) in enumerate(e.chips):
            @pl.when(nxt == (kk + 1) * per_shard)
            def _():
                gather._ici(e, 0, kk, 2 * cx + cy).wait_recv()
                gather._d2d(e, 0, kk, e.c).start()
                gather._d2d(e, 0, kk, 1 - e.c).wait_recv()

        @pl.when(nxt < per_shard)
        def _():
            fetch(w_own_ref, nxt).start()

        @pl.when((nxt >= per_shard) & (nxt < total))
        def _():
            fetch(w_ref, nxt).start()

        fetch(w_ref, step).wait()
        o_ref[...] = jnp.dot(a_ref[...], wbuf[step % 2], preferred_element_type=F32)

        @pl.when(step == total - 1)
        def _():
            for kk in range(3):
                gather._ici(e, 0, kk, e.me).wait_send()
                gather._d2d(e, 0, kk, e.c).wait_send()

    z, full = pl.pallas_call(
        body, name=name, out_shape=(jax.ShapeDtypeStruct((t, n), F32), jax.ShapeDtypeStruct(placed.shape, placed.dtype)),
        grid_spec=pltpu.PrefetchScalarGridSpec(
            num_scalar_prefetch=1, grid=(N_CHIPS, ni, nj),
            in_specs=[pl.BlockSpec((tm, k), lambda s, i, j, ord_ref: (i, 0)), ANY],
            out_specs=(pl.BlockSpec((tm, tn), lambda s, i, j, ord_ref: (i, ord_ref[s] * nj + j)), ANY),
            scratch_shapes=[pltpu.VMEM((2, k, tn), placed.dtype), pltpu.SemaphoreType.DMA((2,)),
                            pltpu.SemaphoreType.DMA((gather.n_sems,)), pltpu.SemaphoreType.DMA((gather.n_sems,))]),
        input_output_aliases={2: 1}, compiler_params=_params(),
    )(order, a, placed)
    return z, full


def _add_own_half(g4, recv, idx, out_dtype, name):
    p, _, rh, n = g4.shape
    tr, tc = _tile2d(rh, n, 1024 * 1024)

    def body(idx_ref, g_ref, r_ref, o_ref):
        o_ref[...] = (g_ref[...] + r_ref[...]).astype(out_dtype)

    return pl.pallas_call(
        body, name=name, out_shape=jax.ShapeDtypeStruct((p, rh, n), out_dtype),
        grid_spec=pltpu.PrefetchScalarGridSpec(
            num_scalar_prefetch=1, grid=(p, rh // tr, n // tc),
            in_specs=[pl.BlockSpec((None, None, tr, tc), lambda q, i, j, idx_ref: (q, idx_ref[0], i, j)),
                      pl.BlockSpec((None, tr, tc), lambda q, i, j, idx_ref: (q, i, j))],
            out_specs=pl.BlockSpec((None, tr, tc), lambda q, i, j, idx_ref: (q, i, j))),
        compiler_params=_params(),
    )(idx, g4, recv)


def _sum_chips(own, kind, parts, idx, slots, to_all, name):
    _, rh, w = parts.shape
    tr, tc = _tile2d(rh, w, 512 * 1024)
    nrb, ncb = rh // tr, w // tc

    def body(idx_ref, own_ref, p0, p1, p2, o_ref):
        o_ref[...] = ((own_ref[...].astype(F32) + p0[...].astype(F32)) + p1[...].astype(F32)) + p2[...].astype(F32)

    if kind == "col":
        own_spec = pl.BlockSpec((None, tr, tc), lambda i, j, idx_ref: (0, i, idx_ref[1] * ncb + j))
    else:
        own_spec = pl.BlockSpec((None, tr, tc), lambda i, j, idx_ref: (idx_ref[1], i, j))
    if to_all:
        out_map = lambda i, j, idx_ref: ((2 * idx_ref[1] + idx_ref[0]) * nrb + i, j)
    else:
        out_map = lambda i, j, idx_ref: (idx_ref[0] * nrb + i, j)

    def part(k):
        return pl.BlockSpec((None, tr, tc), lambda i, j, idx_ref: (k, i, j))

    return pl.pallas_call(
        body, name=name, out_shape=jax.ShapeDtypeStruct((slots * rh, w), F32),
        grid_spec=pltpu.PrefetchScalarGridSpec(
            num_scalar_prefetch=1, grid=(nrb, ncb), in_specs=[own_spec, part(0), part(1), part(2)],
            out_specs=pl.BlockSpec((tr, tc), out_map)),
        compiler_params=_params(),
    )(idx, own, parts, parts, parts)


class _Reduce:
    def __init__(self, name, g, kind, idx, wire, to_all):
        r, c = g.shape
        self.name, self.kind, self.idx, self.wire, self.to_all = name, kind, idx, wire, to_all
        self.view = g.reshape(1, 2, r // 2, c) if kind == "col" else g.reshape(N_CHIPS, 2, r // (2 * N_CHIPS), c)

    def sibling(self):
        return _SiblingExchange([self.view])

    def got_sibling(self, outs):
        self.sum = _add_own_half(self.view, outs[0], self.idx, self.wire, "grad_chip_sum_" + self.name)

    def chips(self):
        return _ChipExchange([self.sum], [self.kind])

    def got_chips(self, outs):
        self.total = _sum_chips(self.sum, self.kind, outs[0], self.idx, 2 * N_CHIPS if self.to_all else 2,
                                self.to_all, "grad_total_" + self.name)


def _pack(arrays, rows):
    flat = jnp.concatenate([a.reshape(-1) for a in arrays])
    return jnp.pad(flat, (0, rows * SMALL_PACK_COLS - flat.shape[0])).reshape(rows, SMALL_PACK_COLS)


def _unpack(packed, shapes):
    flat = packed.reshape(-1)
    out, o = [], 0
    for shp in shapes:
        size = math.prod(shp)
        out.append(flat[o:o + size].reshape(shp))
        o += size
    return out


def _pack_rows(shapes):
    total = sum(math.prod(s) for s in shapes)
    unit = SMALL_PACK_COLS * N_CHIPS * 2 * SUBLANE
    return -(-total // unit) * (N_CHIPS * 2 * SUBLANE)


BIG = ("w_in", "w_attn_proj", "w_lru_proj", "w_out", "w_ffn_gate", "w_ffn_up", "w_ffn_down")
BIG_KIND = {"w_in": "col", "w_attn_proj": "row", "w_lru_proj": "row", "w_out": "row", "w_ffn_gate": "col",
            "w_ffn_up": "col", "w_ffn_down": "row"}
SMALL = ("norm1_g", "b_gates", "q_norm_g", "k_norm_g", "sinks", "conv_w", "conv_b", "w_rgate", "b_rgate",
         "w_igate", "b_igate", "lru_lambda", "norm2_g")
PACKED = tuple(n for n in SMALL if n not in ("w_rgate", "w_igate"))
WEIGHTS = ("norm1_g", "w_in", "b_gates", "q_norm_g", "k_norm_g", "sinks", "conv_w", "conv_b", "w_rgate", "b_rgate",
           "w_igate", "b_igate", "lru_lambda", "w_attn_proj", "w_lru_proj", "w_out", "norm2_g", "w_ffn_gate",
           "w_ffn_up", "w_ffn_down")


def kernel(x, positions, norm1_g, w_in, b_gates, q_norm_g, k_norm_g, sinks, conv_w, conv_b, w_rgate, b_rgate, w_igate, b_igate, lru_lambda, w_attn_proj, w_lru_proj, w_out, norm2_g, w_ffn_gate, w_ffn_up, w_ffn_down, loss_target, m_norm1_g, m_w_in, m_b_gates, m_q_norm_g, m_k_norm_g, m_sinks, m_conv_w, m_conv_b, m_w_rgate, m_b_rgate, m_w_igate, m_b_igate, m_lru_lambda, m_w_attn_proj, m_w_lru_proj, m_w_out, m_norm2_g, m_w_ffn_gate, m_w_ffn_up, m_w_ffn_down, v_norm1_g, v_w_in, v_b_gates, v_q_norm_g, v_k_norm_g, v_sinks, v_conv_w, v_conv_b, v_w_rgate, v_b_rgate, v_w_igate, v_b_igate, v_lru_lambda, v_w_attn_proj, v_w_lru_proj, v_w_out, v_norm2_g, v_w_ffn_gate, v_w_ffn_up, v_w_ffn_down):
    args = dict(locals())
    w = {n: args[n] for n in WEIGHTS}
    mom = {n: args["m_" + n] for n in WEIGHTS}
    var = {n: args["v_" + n] for n in WEIGHTS}

    t, d = x.shape[1], x.shape[2]
    hd = q_norm_g.shape[-1]
    nq = sinks.shape[-1]
    q_w = nq * hd
    d_rnn = conv_b.shape[-1]
    taps = conv_w.shape[1]
    n_blocks, bw = w_rgate.shape[1], w_rgate.shape[2]
    in_w = w_in.shape[-1] * N_CHIPS
    kv_w = (in_w - q_w - 2 * d_rnn - 2 * d) // 2
    kv = kv_w // hd
    grp = nq // kv
    u_off = q_w + 2 * kv_w
    gr_off = u_off + d_rnn
    ga_off = gr_off + d_rnn
    gw = bw * LANE // math.gcd(bw, LANE)
    chip = 2 * lax.axis_index("x") + lax.axis_index("y")
    idx = jnp.stack([lax.axis_index("c"), chip]).astype(jnp.int32)

    x2, tgt = x[0], loss_target[0]

    placed = {n: _cast_into_full(w[n][0], BIG_KIND[n], idx, "cast_" + n) for n in BIG}

    def gather(*names):
        return _AllGather([placed[n] for n in names], [BIG_KIND[n] for n in names])

    mx, my = lax.axis_index("x"), lax.axis_index("y")
    order = jnp.stack([chip, 2 * (1 - mx) + my, 2 * mx + (1 - my), 2 * (1 - mx) + (1 - my)]).astype(jnp.int32)
    conv_w_full = _gather_small(conv_w[0], "allgather_conv_w")
    conv_w_full = jnp.transpose(conv_w_full, (1, 0, 2)).reshape(taps, d_rnn)

    inv_freq = ROPE_THETA ** (-jnp.arange(0, hd // 4, 2, dtype=F32) / (hd // 4))
    ang = positions[0].astype(F32)[:, None] * inv_freq
    cos, sin = jnp.cos(ang), jnp.sin(ang)
    rest = hd - 2 * cos.shape[1]
    cos_t = jnp.concatenate([cos, cos, jnp.ones((t, rest), F32)], axis=1)
    sin_t = jnp.concatenate([-sin, sin, jnp.zeros((t, rest), F32)], axis=1)
    sinks1 = sinks[0]

    xn = _rms_fwd(x2, norm1_g, "rms1_fwd")
    z, win_f = _matmul_gathering(xn, placed["w_in"], order, "in_proj")
    attn, (wap_f, wlp_f, wout_f) = _attn_fwd(z, cos_t, sin_t, q_norm_g, k_norm_g, sinks1, kv, grp, hd, "attn_fwd",
                                             job=gather("w_attn_proj", "w_lru_proj", "w_out"))
    uc = _conv_fwd(z, u_off, d_rnn, conv_w_full, conv_b, "conv_fwd")
    rpre, ipre = _gates_fwd(uc, w_rgate[0], w_igate[0], gw, "gates_fwd")
    (hst, rec), (wg_f,) = _lru_fwd(uc, rpre, ipre, z, gr_off, b_rgate, b_igate, lru_lambda, "lru_fwd",
                                   job=gather("w_ffn_gate"))
    pa = _matmul(attn, wap_f, "nn", "attn_proj")
    plru = _matmul(rec, wlp_f, "nn", "lru_proj")
    merged = _merge_fwd(z, b_gates, pa, plru, ga_off, "merge_fwd")
    h1 = _matmul(merged, wout_f, "nn", "out_proj", add=x2)
    hn = _rms_fwd(h1, norm2_g, "rms2_fwd")
    gate, (wu_f,) = _matmul(hn, wg_f, "nn", "ffn_gate", job=gather("w_ffn_up"))
    up, (wd_f,) = _matmul(hn, wu_f, "nn", "ffn_up", job=gather("w_ffn_down"))
    act = _swiglu_fwd(gate, up, "swiglu_fwd")
    yout = _matmul(act, wd_f, "nn", "ffn_down", add=h1)
    dy, dy16, loss_part = _loss_head(yout, tgt, "loss_head")
    loss = lax.psum(loss_part[0, 0], ("x", "y", "c"))

    def reduction(n, g):
        return _Reduce(n, g, BIG_KIND[n], idx, BF16, False)

    r_wd = reduction("w_ffn_down", _matmul(act, dy16, "tn", "d_w_ffn_down"))
    dact, got = _matmul(dy16, wd_f, "nt", "d_act", job=r_wd.sibling())
    r_wd.got_sibling(got)
    (dgate, dup), got = _swiglu_bwd(dact, gate, up, "swiglu_bwd", job=r_wd.chips())
    r_wd.got_chips(got)
    r_wg = reduction("w_ffn_gate", _matmul(hn, dgate, "tn", "d_w_ffn_gate"))
    g_wu, got = _matmul(hn, dup, "tn", "d_w_ffn_up", job=r_wg.sibling())
    r_wg.got_sibling(got)
    r_wu = reduction("w_ffn_up", g_wu)
    dhn, got = _matmul(dgate, wg_f, "nt", "d_hn_gate", job=r_wu.sibling())
    r_wu.got_sibling(got)
    dhn = _matmul(dup, wu_f, "nt", "d_hn_up", add=dhn)
    dh1, g_norm2, dh1_16 = _rms_bwd(dhn, h1, norm2_g, dy, "rms2_bwd", mxu_copy=True)
    r_wout = reduction("w_out", _matmul(merged, dh1_16, "tn", "d_w_out"))
    dmerged, got = _matmul(dh1_16, wout_f, "nt", "d_merged", job=r_wout.sibling())
    r_wout.got_sibling(got)
    (dpa, dpl, dga, dgl, g_ba, g_bl), got = _merge_bwd(dmerged, z, b_gates, pa, plru, ga_off, "merge_bwd",
                                                       job=r_wout.chips())
    r_wout.got_chips(got)
    r_wap = reduction("w_attn_proj", _matmul(attn, dpa, "tn", "d_w_attn_proj"))
    dattn, got = _matmul(dpa, wap_f, "nt", "d_attn", job=r_wap.sibling())
    r_wap.got_sibling(got)
    g_wlp, got = _matmul(rec, dpl, "tn", "d_w_lru_proj", job=r_wap.chips())
    r_wap.got_chips(got)
    r_wlp = reduction("w_lru_proj", g_wlp)
    drec, got = _matmul(dpl, wlp_f, "nt", "d_rec", job=r_wlp.sibling())
    r_wlp.got_sibling(got)
    both = _Jobs(r_wlp.chips(), r_wg.chips())
    (dgr, drp, dip, duc_direct, g_lam, g_br, g_bi), got = _lru_bwd(
        drec, hst, uc, rpre, ipre, z, gr_off, b_rgate, b_igate, lru_lambda, "lru_bwd", job=both)
    got_wlp, got_wg = both.split(got)
    r_wlp.got_chips(got_wlp)
    r_wg.got_chips(got_wg)
    duc = _gates_bwd_x(duc_direct, drp, dip, w_rgate[0], w_igate[0], gw, "gates_bwd_x")
    g_wr = _gates_bwd_w(uc, drp, n_blocks, bw, gw, "gates_bwd_wr")
    g_wi = _gates_bwd_w(uc, dip, n_blocks, bw, gw, "gates_bwd_wi")
    du, g_convw, g_convb = _conv_bwd(duc, z, u_off, conv_w_full, "conv_bwd")
    (dq, dk, dv, g_qg, g_kg, g_sinks), got = _attn_bwd(dattn, z, cos_t, sin_t, q_norm_g, k_norm_g, sinks1, kv, grp, hd,
                                                        "attn_bwd", job=r_wu.chips())
    r_wu.got_chips(got)
    dz = jnp.concatenate([dq, dk, dv, du, dgr, dga, dgl], axis=1)
    r_wr = _Reduce("w_rgate", g_wr.reshape(n_blocks * bw, bw), "row", idx, F32, True)
    r_wi = _Reduce("w_igate", g_wi.reshape(n_blocks * bw, bw), "row", idx, F32, True)
    early = [r_wap, r_wlp, r_wout, r_wg, r_wu, r_wd]
    three = _Jobs(r_wr.sibling(), r_wi.sibling(), _FinishExchange([r.total for r in early], [False] * len(early)))
    g_top, got = _matmul(xn, dz, "tn", "d_w_in_top", m_window=(0, d // 2), job=three)
    got_wr, got_wi, finished = three.split(got)
    r_wr.got_sibling(got_wr)
    r_wi.got_sibling(got_wi)
    r_top = _Reduce("w_in_top", g_top, "col", idx, BF16, False)
    three = _Jobs(r_top.sibling(), r_wr.chips(), r_wi.chips())
    g_bot, got = _matmul(xn, dz, "tn", "d_w_in_bot", m_window=(d // 2, d // 2), job=three)
    got_top, got_wr, got_wi = three.split(got)
    r_top.got_sibling(got_top)
    r_wr.got_chips(got_wr)
    r_wi.got_chips(got_wi)
    r_bot = _Reduce("w_in_bot", g_bot, "col", idx, BF16, False)
    both = _Jobs(r_top.chips(), r_bot.sibling())
    dxn, got = _matmul(dz, win_f, "nt", "d_xn_a", m_window=(0, t // 2), into=(None, t), job=both)
    got_top, got_bot = both.split(got)
    r_top.got_chips(got_top)
    r_bot.got_sibling(got_bot)
    dxn, got = _matmul(dz, win_f, "nt", "d_xn_b", m_window=(t // 2, t // 2), into=(dxn, t), job=r_bot.chips())
    r_bot.got_chips(got)
    dx, g_norm1 = _rms_bwd(dxn, x2, norm1_g, dh1, "rms1_bwd")

    small_grads = {"norm1_g": g_norm1, "b_gates": jnp.concatenate([g_ba, g_bl], axis=1), "q_norm_g": g_qg,
                   "k_norm_g": g_kg, "sinks": g_sinks[:, :nq], "conv_w": g_convw, "conv_b": g_convb,
                   "b_rgate": g_br, "b_igate": g_bi, "lru_lambda": g_lam, "norm2_g": g_norm2}
    gshapes = [small_grads[n].shape for n in PACKED]
    small_sum = _allreduce_small(_pack([small_grads[n] for n in PACKED], _pack_rows(gshapes)), "allreduce_small")
    top, bot, grads_wr, grads_wi = _run_exchange(
        _FinishExchange([r.total for r in (r_top, r_bot, r_wr, r_wi)], [False, False, True, True]),
        "grad_finish_exchange")
    grads = dict(zip(BIG[1:], finished))
    grads["w_in"] = jnp.concatenate([top, bot], axis=0)
    grads["w_rgate"], grads["w_igate"] = grads_wr, grads_wi
    small_full = dict(zip(PACKED, _unpack(small_sum, gshapes)))
    per = d_rnn // N_CHIPS
    small_full["conv_w"] = lax.dynamic_slice(small_full["conv_w"], (0, chip * per), (taps, per))
    grads.update(small_full)

    delta, new_m, new_v = {}, {}, {}
    for n in BIG + ("w_rgate", "w_igate"):
        as2d = (lambda a: a[0]) if n in BIG else (lambda a: a.reshape(n_blocks * bw, bw))
        delta[n], new_m[n], new_v[n] = _adamw(as2d(w[n]), grads[n], as2d(mom[n]), as2d(var[n]), "adamw_" + n)
    pshapes = [w[n].shape for n in PACKED]
    prows = _pack_rows(pshapes)
    pk = [_pack([src[n] for n in PACKED], prows) for src in (w, grads, mom, var)]
    for res, packed in zip((delta, new_m, new_v), _adamw(pk[0], pk[1], pk[2], pk[3], "adamw_small")):
        res.update(dict(zip(PACKED, _unpack(packed, pshapes))))

    outs = [loss, dx.reshape(x.shape)]
    for res in (grads, delta, new_m, new_v):
        outs += [res[n].reshape(w[n].shape) for n in WEIGHTS]
    return tuple(outs)


def _allreduce_small(x, name):
    n_dev = 2 * N_CHIPS
    rel = [(fx, fy, fc) for fx in (0, 1) for fy in (0, 1) for fc in (0, 1)][1:]

    def body(x_ref, all_ref, o_ref, send_sems, recv_sems):
        e = _Env((x_ref,), (all_ref,), send_sems, recv_sems)
        mine = 2 * e.me + e.c

        def peer(r):
            fx, fy, fc = rel[r]
            return (1 - e.x if fx else e.x), (1 - e.y if fy else e.y), (1 - e.c if fc else e.c)

        all_ref[mine] = x_ref[...]
        for r in range(len(rel)):
            e.copy(x_ref, all_ref.at[mine], r, peer(r)).start()
        for r in range(len(rel)):
            px, py, pc = peer(r)
            e.copy(x_ref, all_ref.at[2 * (2 * px + py) + pc], r, peer(r)).wait_recv()
        total = all_ref[0]
        for dev in range(1, n_dev):
            total = total + all_ref[dev]
        o_ref[...] = total
        for r in range(len(rel)):
            e.copy(x_ref, all_ref.at[mine], r, peer(r)).wait_send()

    vm = pl.BlockSpec(memory_space=pltpu.VMEM)
    return pl.pallas_call(
        body, name=name, out_shape=(jax.ShapeDtypeStruct((n_dev,) + x.shape, x.dtype), jax.ShapeDtypeStruct(x.shape, x.dtype)),
        in_specs=[vm], out_specs=(vm, vm),
        scratch_shapes=[pltpu.SemaphoreType.DMA((len(rel),)), pltpu.SemaphoreType.DMA((len(rel),))])(x)[1]


def _gather_small(shard, name):
    def body(s_ref, o_ref, send_sems, recv_sems):
        e = _Env((s_ref,), (o_ref,), send_sems, recv_sems)
        o_ref[e.me] = s_ref[...]
        for k, (cx, cy) in enumerate(e.chips):
            e.copy(s_ref, o_ref.at[e.me], k, (cx, cy, e.c)).start()
        for k, (cx, cy) in enumerate(e.chips):
            e.copy(s_ref, o_ref.at[2 * cx + cy], k, (cx, cy, e.c)).wait_recv()
        for k, (cx, cy) in enumerate(e.chips):
            e.copy(s_ref, o_ref.at[e.me], k, (cx, cy, e.c)).wait_send()

    vm = pl.BlockSpec(memory_space=pltpu.VMEM)
    return pl.pallas_call(body, name=name, out_shape=jax.ShapeDtypeStruct((N_CHIPS,) + shard.shape, shard.dtype),
                          in_specs=[vm], out_specs=vm,
                          scratch_shapes=[pltpu.SemaphoreType.DMA((3,)), pltpu.SemaphoreType.DMA((3,))])(shard)
```

```python
import functools
import math

import jax
import jax.numpy as jnp
from jax import lax
from jax.experimental import pallas as pl
from jax.experimental.pallas import tpu as pltpu

F32 = jnp.float32
BF16 = jnp.bfloat16
MESH = pl.DeviceIdType.MESH

WINDOW = 128
BLK = 128
ROPE_THETA = 500000.0
LRU_C = 8.0
EPS = 1e-6
NEG = -1e30
ADAM_LR = 0.001
ADAM_B1 = 0.9
ADAM_B2 = 0.999
ADAM_EPS = 1e-08
ADAM_WD = 0.01
ADAM_STEP = 10

VMEM_LIMIT_BYTES = 52 * 1024 * 1024
LANE = 128
SUBLANE = 8
N_CHIPS = 4
SMALL_PACK_COLS = 512


def _params(**kw):
    return pltpu.CompilerParams(vmem_limit_bytes=VMEM_LIMIT_BYTES, **kw)


def _pick(dim, cands):
    for c in cands:
        if dim % c == 0:
            return c
    return dim


def _sigmoid(x):
    return 0.5 * jnp.tanh(0.5 * x) + 0.5


ANY = pl.BlockSpec(memory_space=pl.ANY)


class _Env:
    def __init__(self, ins, outs, send, recv, sem0=0, place=None):
        self.ins, self.outs, self.send, self.recv, self.sem0 = ins, outs, send, recv, sem0
        self.x, self.y, self.c = place or (lax.axis_index("x"), lax.axis_index("y"), lax.axis_index("c"))
        self.me = 2 * self.x + self.y
        self.chips = [(1 - self.x, self.y), (self.x, 1 - self.y), (1 - self.x, 1 - self.y)]
        self.sibling = (self.x, self.y, 1 - self.c)

    def sub(self, i0, n_in, o0, n_out, sem0):
        return _Env(self.ins[i0:i0 + n_in], self.outs[o0:o0 + n_out], self.send, self.recv, self.sem0 + sem0,
                    (self.x, self.y, self.c))

    def copy(self, src, dst, sem, to):
        return pltpu.make_async_remote_copy(src_ref=src, dst_ref=dst, send_sem=self.send.at[self.sem0 + sem],
                                            recv_sem=self.recv.at[self.sem0 + sem], device_id=to, device_id_type=MESH)


class _Exchange:
    inputs, out_shapes, aliases, n_sems = (), (), {}, 0

    def start(self, e):
        raise NotImplementedError

    def finish(self, e):
        raise NotImplementedError


class _Jobs(_Exchange):
    def __init__(self, *jobs):
        self.jobs, self.inputs, self.out_shapes, self.aliases, self.n_sems, self.at = jobs, [], [], {}, 0, []
        for job in jobs:
            self.at.append((len(self.inputs), len(self.out_shapes), self.n_sems))
            self.aliases.update({len(self.inputs) + i: len(self.out_shapes) + o for i, o in job.aliases.items()})
            self.inputs += list(job.inputs)
            self.out_shapes += list(job.out_shapes)
            self.n_sems += job.n_sems

    def _each(self, e):
        for job, (i0, o0, s0) in zip(self.jobs, self.at):
            yield job, e.sub(i0, len(job.inputs), o0, len(job.out_shapes), s0)

    def split(self, outs):
        return [tuple(outs[o0:o0 + len(job.out_shapes)]) for job, (_, o0, _) in zip(self.jobs, self.at)]

    def start(self, e):
        for job, se in self._each(e):
            job.start(se)

    def finish(self, e):
        for job, se in self._each(e):
            job.finish(se)


def _call(body, name, out_shape, grid, in_specs, out_specs, args, scratch_shapes=(), job=None, aliases=None):
    aliases = dict(aliases or {})
    if job is None:
        return pl.pallas_call(body, name=name, out_shape=out_shape, grid=grid, in_specs=list(in_specs),
                              out_specs=out_specs, scratch_shapes=list(scratch_shapes), input_output_aliases=aliases,
                              compiler_params=_params())(*args), ()
    single = not isinstance(out_shape, (tuple, list))
    shapes = [out_shape] if single else list(out_shape)
    ospecs = [out_specs] if single else list(out_specs)
    n_in, n_out, n_scr = len(args), len(shapes), len(scratch_shapes)
    j_in, j_out = len(job.inputs), len(job.out_shapes)

    def hosted(*refs):
        ins, jins = refs[:n_in], refs[n_in:n_in + j_in]
        outs = refs[n_in + j_in:n_in + j_in + n_out]
        jouts = refs[n_in + j_in + n_out:n_in + j_in + n_out + j_out]
        rest = refs[n_in + j_in + n_out + j_out:]
        e = _Env(jins, jouts, rest[n_scr], rest[n_scr + 1])
        first = functools.reduce(jnp.logical_and, [pl.program_id(d) == 0 for d in range(len(grid))])
        last = functools.reduce(jnp.logical_and, [pl.program_id(d) == g - 1 for d, g in enumerate(grid)])

        @pl.when(first)
        def _():
            job.start(e)

        body(*ins, *outs, *rest[:n_scr])

        @pl.when(last)
        def _():
            job.finish(e)

    res = pl.pallas_call(
        hosted, name=name, out_shape=tuple(shapes + list(job.out_shapes)), grid=grid,
        in_specs=list(in_specs) + [ANY] * j_in, out_specs=tuple(ospecs + [ANY] * j_out),
        scratch_shapes=list(scratch_shapes) + [pltpu.SemaphoreType.DMA((job.n_sems,)),
                                               pltpu.SemaphoreType.DMA((job.n_sems,))],
        input_output_aliases={**aliases, **{n_in + i: n_out + o for i, o in job.aliases.items()}},
        compiler_params=_params())(*args, *job.inputs)
    return (res[0] if single else tuple(res[:n_out])), tuple(res[n_out:])


def _run_exchange(job, name):
    n_in, n_out = len(job.inputs), len(job.out_shapes)

    def body(*refs):
        e = _Env(refs[:n_in], refs[n_in:n_in + n_out], refs[n_in + n_out], refs[n_in + n_out + 1])
        job.start(e)
        job.finish(e)

    return pl.pallas_call(
        body, name=name, out_shape=tuple(job.out_shapes), in_specs=[ANY] * n_in, out_specs=tuple([ANY] * n_out),
        input_output_aliases=dict(job.aliases),
        scratch_shapes=[pltpu.SemaphoreType.DMA((job.n_sems,)), pltpu.SemaphoreType.DMA((job.n_sems,))],
    )(*job.inputs)


_M_TILES = (1024, 1408, 1280, 512, 256, 128)
_N_TILES = (1408, 1280, 1024, 640, 512, 256, 128)
MXU_FULL_ROWS = 1024
MATMUL_VMEM_BUDGET = 42 * 1024 * 1024
MXU_FLOPS_PER_HBM_BYTE = 500


def _matmul_tiles(m, n, k, sa, sb, so, has_add):
    best = None
    for tm in [c for c in _M_TILES if m % c == 0] or [m]:
        for tn in [c for c in _N_TILES if n % c == 0] or [n]:
            for nk in range(1, 17):
                tk = k // nk
                if k % nk or tk % LANE:
                    continue
                need = 2 * (tm * tk * sa + tk * tn * sb) + 2 * tm * tn * (so + (4 if has_add else 0))
                need += tm * tn * 4 if nk > 1 else 0
                fetched = tk * tn * sb + tm * tk * sa // (1 if nk > 1 else n // tn)
                if need > MATMUL_VMEM_BUDGET:
                    continue
                mxu_bound = fetched * MXU_FLOPS_PER_HBM_BYTE <= 2 * tm * tn * tk
                key = (mxu_bound, min(tm, MXU_FULL_ROWS), -nk, tn, tm)
                if best is None or key > best[0]:
                    best = (key, (tm, tn, tk))
    assert best is not None, (m, n, k)
    return best[1]


def _matmul(a, b, mode, name, add=None, out_dtype=F32, job=None, m_window=None, into=None, fused=None):
    if mode == "nn":
        (m, k), (k2, n) = a.shape, b.shape
    elif mode == "nt":
        (m, k), (n, k2) = a.shape, b.shape
    else:
        (k, m), (k2, n) = a.shape, b.shape
    assert k == k2, (a.shape, b.shape, mode)
    m0, m = m_window or (0, m)
    tiles, fuse_fn, out_dtypes = fused or ((), None, (out_dtype,))
    tm, tn, tk = _matmul_tiles(math.gcd(m, m0) if m0 else m, n, k, a.dtype.itemsize, b.dtype.itemsize,
                               sum(jnp.dtype(dt).itemsize for dt in out_dtypes) + sum(x.dtype.itemsize for x in tiles),
                               add is not None)
    nk, mb0 = k // tk, m0 // tm
    if mode == "nn":
        a_spec = pl.BlockSpec((tm, tk), lambda i, j, kk: (mb0 + i, kk))
        b_spec = pl.BlockSpec((tk, tn), lambda i, j, kk: (kk, j))
        dims = (((1,), (0,)), ((), ()))
    elif mode == "nt":
        a_spec = pl.BlockSpec((tm, tk), lambda i, j, kk: (mb0 + i, kk))
        b_spec = pl.BlockSpec((tn, tk), lambda i, j, kk: (j, kk))
        dims = (((1,), (1,)), ((), ()))
    else:
        a_spec = pl.BlockSpec((tk, tm), lambda i, j, kk: (kk, mb0 + i))
        b_spec = pl.BlockSpec((tk, tn), lambda i, j, kk: (kk, j))
        dims = (((0,), (0,)), ((), ()))
    out_rows, ob0 = (into[1], mb0) if into is not None else (m, 0)
    o_spec = pl.BlockSpec((tm, tn), lambda i, j, kk: (ob0 + i, j))
    has_add = add is not None
    begun = into is not None and into[0] is not None

    n_side, n_out = has_add + len(tiles), len(out_dtypes)

    def body(*refs):
        a_ref, b_ref = refs[:2]
        side = refs[2:2 + n_side]
        o_refs = refs[len(refs) - n_out - (nk > 1):len(refs) - (nk > 1)]
        part = lax.dot_general(a_ref[...].astype(BF16), b_ref[...].astype(BF16), dims, preferred_element_type=F32)

        def finish(r):
            if has_add:
                r = r + side[0][...]
            vals = fuse_fn(r, *[x[...] for x in side[has_add:]]) if fuse_fn else (r,)
            for o_ref, val, dt in zip(o_refs, vals, out_dtypes):
                o_ref[...] = val.astype(dt)

        if nk == 1:
            finish(part)
            return
        acc = refs[-1]
        kk = pl.program_id(2)

        @pl.when(kk == 0)
        def _():
            acc[...] = part

        @pl.when(kk > 0)
        def _():
            acc[...] += part

        @pl.when(kk == nk - 1)
        def _():
            finish(acc[...])

    side_spec = pl.BlockSpec((tm, tn), lambda i, j, kk: (mb0 + i, j))
    in_specs = [a_spec, b_spec] + [side_spec] * n_side
    args = (a, b) + ((add,) if has_add else ()) + tuple(tiles)
    aliases = None
    if begun:
        aliases = {len(args): 0}
        in_specs, args = in_specs + [ANY], args + (into[0],)
    shapes = tuple(jax.ShapeDtypeStruct((out_rows, n), dt) for dt in out_dtypes)
    res, extra = _call(body, name, shapes if fused else shapes[0], (m // tm, n // tn, nk), in_specs,
                       (o_spec,) * n_out if fused else o_spec, args, [pltpu.VMEM((tm, tn), F32)] if nk > 1 else [],
                       job, aliases)
    return res if job is None else (res, extra)


def _row_tile(rows, cols, budget_elems=512 * 1024):
    cands = [c for c in (1024, 704, 512, 352, 256, 128, 64, 32, 16) if c * cols <= budget_elems]
    return _pick(rows, cands or (16,))


_EW_COLS = (1280, 1408, 1024, 640, 512, 256, 128)


def _tile2d(rows, cols, max_elems):
    tc = _pick(cols, _EW_COLS)
    return _row_tile(rows, tc, max_elems), tc


def _rms_fwd(x, g, name):
    t, d = x.shape
    tr = _row_tile(t, d)

    def body(x_ref, g_ref, o_ref):
        xv = x_ref[...]
        rstd = lax.rsqrt(jnp.mean(xv * xv, axis=-1, keepdims=True) + EPS)
        o_ref[...] = (xv * rstd * g_ref[...]).astype(BF16)

    spec = pl.BlockSpec((tr, d), lambda i: (i, 0))
    return pl.pallas_call(body, name=name, out_shape=jax.ShapeDtypeStruct((t, d), BF16), grid=(t // tr,),
                          in_specs=[spec, pl.BlockSpec((1, d), lambda i: (0, 0))], out_specs=spec,
                          compiler_params=_params())(x, g)


def _rms_bwd(dxn, x, g, resid, name, job=None, mxu_copy=False):
    t, d = x.shape
    tr = _row_tile(t, d, 256 * 1024)

    def body(dxn_ref, x_ref, g_ref, r_ref, dx_ref, dg_ref, *dx16_ref):
        @pl.when(pl.program_id(0) == 0)
        def _():
            dg_ref[...] = jnp.zeros_like(dg_ref)

        xv = x_ref[...]
        rstd = lax.rsqrt(jnp.mean(xv * xv, axis=-1, keepdims=True) + EPS)
        xhat = xv * rstd
        dy = dxn_ref[...]
        dg_ref[...] += jnp.sum(dy * xhat, axis=0, keepdims=True)
        dxhat = dy * g_ref[...]
        dx = r_ref[...] + rstd * (dxhat - xhat * jnp.mean(dxhat * xhat, axis=-1, keepdims=True))
        dx_ref[...] = dx
        if mxu_copy:
            dx16_ref[0][...] = dx.astype(BF16)

    spec = pl.BlockSpec((tr, d), lambda i: (i, 0))
    vec = pl.BlockSpec((1, d), lambda i: (0, 0))
    shapes = (jax.ShapeDtypeStruct((t, d), F32), jax.ShapeDtypeStruct((1, d), F32))
    shapes += (jax.ShapeDtypeStruct((t, d), BF16),) if mxu_copy else ()
    res, extra = _call(body, name, shapes, (t // tr,), [spec, spec, vec, spec],
                       (spec, vec) + ((spec,) if mxu_copy else ()), (dxn, x, g, resid), (), job)
    return res if job is None else (res, extra)


def _swiglu_after_up(up, gate):
    return up, gate * _sigmoid(gate) * up


def _swiglu_bwd_after_dact(dact, gate, up):
    sg = _sigmoid(gate)
    return dact * up * (sg * (1.0 + gate * (1.0 - sg))), dact * (gate * sg)


def _merge_fwd(z, b_gates, pa, plru, ga_off, name):
    t, d = pa.shape
    cw = _pick(math.gcd(ga_off, d), (512, 256, 128))
    tr = _row_tile(t, cw, 256 * 1024)
    oa, ol, nd = ga_off // cw, (ga_off + d) // cw, d // cw

    def body(ga_ref, gl_ref, ba_ref, bl_ref, pa_ref, pl_ref, o_ref):
        sa = _sigmoid(ga_ref[...] + ba_ref[...])
        sl = _sigmoid(gl_ref[...] + bl_ref[...])
        o_ref[...] = (sa * pa_ref[...] + sl * pl_ref[...]).astype(BF16)

    blk = pl.BlockSpec((tr, cw), lambda i, j: (i, j))
    return pl.pallas_call(
        body, name=name, out_shape=jax.ShapeDtypeStruct((t, d), BF16), grid=(t // tr, nd),
        in_specs=[pl.BlockSpec((tr, cw), lambda i, j: (i, oa + j)), pl.BlockSpec((tr, cw), lambda i, j: (i, ol + j)),
                  pl.BlockSpec((1, cw), lambda i, j: (0, j)), pl.BlockSpec((1, cw), lambda i, j: (0, nd + j)),
                  blk, blk],
        out_specs=blk, compiler_params=_params(),
    )(z, z, b_gates, b_gates, pa, plru)


def _merge_bwd(dmerged, z, b_gates, pa, plru, ga_off, name, job=None):
    t, d = pa.shape
    cw = _pick(math.gcd(ga_off, d), (512, 256, 128))
    tr = _row_tile(t, cw, 256 * 1024)
    oa, ol, nd = ga_off // cw, (ga_off + d) // cw, d // cw

    def body(dm_ref, ga_ref, gl_ref, ba_ref, bl_ref, pa_ref, pl_ref, dpa_ref, dpl_ref, dga_ref, dgl_ref, sa_ref, sl_ref):
        @pl.when(pl.program_id(1) == 0)
        def _():
            sa_ref[...] = jnp.zeros_like(sa_ref)
            sl_ref[...] = jnp.zeros_like(sl_ref)

        dm = dm_ref[...]
        sa = _sigmoid(ga_ref[...] + ba_ref[...])
        sl = _sigmoid(gl_ref[...] + bl_ref[...])
        dpa_ref[...] = (dm * sa).astype(BF16)
        dpl_ref[...] = (dm * sl).astype(BF16)
        dga = dm * pa_ref[...] * (sa * (1.0 - sa))
        dgl = dm * pl_ref[...] * (sl * (1.0 - sl))
        dga_ref[...] = dga.astype(BF16)
        dgl_ref[...] = dgl.astype(BF16)
        sa_ref[...] += jnp.sum(dga, axis=0, keepdims=True)
        sl_ref[...] += jnp.sum(dgl, axis=0, keepdims=True)

    blk = pl.BlockSpec((tr, cw), lambda j, i: (i, j))
    vec = pl.BlockSpec((1, cw), lambda j, i: (0, j))
    big16, v32 = jax.ShapeDtypeStruct((t, d), BF16), jax.ShapeDtypeStruct((1, d), F32)
    res, extra = _call(
        body, name, (big16, big16, big16, big16, v32, v32), (nd, t // tr),
        [blk, pl.BlockSpec((tr, cw), lambda j, i: (i, oa + j)), pl.BlockSpec((tr, cw), lambda j, i: (i, ol + j)),
         vec, pl.BlockSpec((1, cw), lambda j, i: (0, nd + j)), blk, blk],
        (blk, blk, blk, blk, vec, vec), (dmerged, z, z, b_gates, b_gates, pa, plru), (), job)
    return res if job is None else (res, extra)


def _loss_head(y, target, name):
    t, d = y.shape
    tr = _row_tile(t, d, 256 * 1024)
    nt = t // tr

    def body(y_ref, t_ref, dy_ref, dy16_ref, loss_ref, acc):
        i = pl.program_id(0)

        @pl.when(i == 0)
        def _():
            acc[...] = jnp.zeros_like(acc)

        e = y_ref[...] - t_ref[...]
        dy = e * (1.0 / d)
        dy_ref[...] = dy
        dy16_ref[...] = dy.astype(BF16)
        acc[...] += jnp.sum(e * e, axis=0, keepdims=True)

        @pl.when(i == nt - 1)
        def _():
            loss_ref[...] = (0.5 / d) * jnp.sum(acc[...], axis=-1, keepdims=True)

    spec = pl.BlockSpec((tr, d), lambda i: (i, 0))
    return pl.pallas_call(
        body, name=name, out_shape=(jax.ShapeDtypeStruct((t, d), F32), jax.ShapeDtypeStruct((t, d), BF16),
                                    jax.ShapeDtypeStruct((1, 1), F32)),
        grid=(nt,), in_specs=[spec, spec], out_specs=(spec, spec, pl.BlockSpec((1, 1), lambda i: (0, 0))),
        scratch_shapes=[pltpu.VMEM((1, d), F32)], compiler_params=_params(),
    )(y, target)


def _adamw(w, g, m, v, name):
    r, c = w.shape
    tr, tc = _tile2d(r, c, 512 * 1024)
    c1 = 1.0 - ADAM_B1 ** ADAM_STEP
    c2 = 1.0 - ADAM_B2 ** ADAM_STEP

    def body(w_ref, g_ref, m_ref, v_ref, d_ref, nm_ref, nv_ref):
        gv = g_ref[...]
        mn = ADAM_B1 * m_ref[...] + (1.0 - ADAM_B1) * gv
        vn = ADAM_B2 * v_ref[...] + (1.0 - ADAM_B2) * (gv * gv)
        d_ref[...] = -ADAM_LR * ((mn / c1) / (jnp.sqrt(vn / c2) + ADAM_EPS) + ADAM_WD * w_ref[...])
        nm_ref[...] = mn
        nv_ref[...] = vn

    spec = pl.BlockSpec((tr, tc), lambda i, j: (i, j))
    shp = jax.ShapeDtypeStruct((r, c), F32)
    return pl.pallas_call(body, name=name, out_shape=(shp, shp, shp), grid=(r // tr, c // tc), in_specs=[spec] * 4,
                          out_specs=(spec, spec, spec), compiler_params=_params())(w, g, m, v)


def _swap_halves(v, lane, half):
    n = v.shape[-1]
    return jnp.where(lane < half, pltpu.roll(v, n - half, 1),
                     jnp.where(lane < 2 * half, pltpu.roll(v, half, 1), 0.0))


def _norm_fwd(xraw, g):
    rstd = lax.rsqrt(jnp.mean(xraw * xraw, axis=-1, keepdims=True) + EPS)
    xhat = xraw * rstd
    return xhat, rstd, xhat * g


def _norm_bwd(dy, xhat, rstd, g):
    dxhat = dy * g
    dx = rstd * (dxhat - xhat * jnp.mean(dxhat * xhat, axis=-1, keepdims=True))
    return dx, jnp.sum(dy * xhat, axis=0, keepdims=True)


def _attn_specs(nb, grp, hd, kv, clamp):
    qo, ko, vo = 0, (kv * grp), (kv * grp + kv)
    cur = (lambda i: jnp.minimum(i, nb - 1)) if clamp else (lambda i: i)
    prev = lambda i: jnp.maximum(cur(i) - 1, 0)
    zq = pl.BlockSpec((BLK, grp * hd), lambda h, i: (cur(i), h))
    kc = pl.BlockSpec((BLK, hd), lambda h, i: (cur(i), ko + h))
    kp = pl.BlockSpec((BLK, hd), lambda h, i: (prev(i), ko + h))
    vc = pl.BlockSpec((BLK, hd), lambda h, i: (cur(i), vo + h))
    vp = pl.BlockSpec((BLK, hd), lambda h, i: (prev(i), vo + h))
    tc = pl.BlockSpec((BLK, hd), lambda h, i: (cur(i), 0))
    tp = pl.BlockSpec((BLK, hd), lambda h, i: (prev(i), 0))
    gs = pl.BlockSpec((1, hd), lambda h, i: (0, 0))
    return zq, kc, kp, vc, vp, tc, tp, gs


def _attn_mask(i):
    qi = lax.broadcasted_iota(jnp.int32, (BLK, 2 * BLK), 0)
    kj = lax.broadcasted_iota(jnp.int32, (BLK, 2 * BLK), 1)
    rel = qi + BLK - kj
    return (rel >= 0) & (rel < WINDOW) & ((kj >= BLK) | (i > 0))


def _attn_fwd(z, cos_t, sin_t, qg, kg, sinks, kv, grp, hd, name, job=None):
    t = z.shape[0]
    nb = t // BLK
    half = hd // 8
    scale = 1.0 / math.sqrt(hd)
    zq, kc, kp, vc, vp, tc, tp, gs = _attn_specs(nb, grp, hd, kv, False)

    def body(sink_ref, zq_ref, kc_ref, kp_ref, vc_ref, vp_ref, cc_ref, sc_ref, cp_ref, sp_ref, qg_ref, kg_ref, o_ref):
        h, i = pl.program_id(0), pl.program_id(1)
        lane = lax.broadcasted_iota(jnp.int32, (BLK, hd), 1)

        def normrope(xraw, g, c, s):
            y = _norm_fwd(xraw, g)[2]
            return y * c + _swap_halves(y, lane, half) * s

        cc, sc = cc_ref[...], sc_ref[...]
        kcur = normrope(kc_ref[...], kg_ref[...], cc, sc)
        kprev = normrope(kp_ref[...], kg_ref[...], cp_ref[...], sp_ref[...])
        kk = jnp.concatenate([kprev, kcur], axis=0).astype(BF16)
        vv = jnp.concatenate([vp_ref[...], vc_ref[...]], axis=0).astype(BF16)
        mask = _attn_mask(i)
        for g in range(grp):
            q = normrope(zq_ref[:, g * hd:(g + 1) * hd], qg_ref[...], cc, sc).astype(BF16)
            s = lax.dot_general(q, kk, (((1,), (1,)), ((), ())), preferred_element_type=F32) * scale
            s = jnp.where(mask, s, NEG)
            sk = sink_ref[h * grp + g]
            mx = jnp.maximum(jnp.max(s, axis=-1, keepdims=True), sk)
            p = jnp.exp(s - mx)
            den = jnp.sum(p, axis=-1, keepdims=True) + jnp.exp(sk - mx)
            p = p * (1.0 / den)
            o_ref[:, g * hd:(g + 1) * hd] = jnp.dot(p.astype(BF16), vv, preferred_element_type=F32).astype(BF16)

    res, extra = _call(
        body, name, jax.ShapeDtypeStruct((t, kv * grp * hd), BF16), (kv, nb),
        [pl.BlockSpec(memory_space=pltpu.SMEM), zq, kc, kp, vc, vp, tc, tc, tp, tp, gs, gs],
        pl.BlockSpec((BLK, grp * hd), lambda h, i: (i, h)),
        (sinks, z, z, z, z, z, cos_t, sin_t, cos_t, sin_t, qg, kg), (), job)
    return res if job is None else (res, extra)


def _attn_bwd(dattn, z, cos_t, sin_t, qg, kg, sinks, kv, grp, hd, name, job=None):
    t = z.shape[0]
    nb = t // BLK
    half = hd // 8
    scale = 1.0 / math.sqrt(hd)
    zq, kc, kp, vc, vp, tc, tp, gs = _attn_specs(nb, grp, hd, kv, True)

    def body(sink_ref, zq_ref, kc_ref, kp_ref, vc_ref, vp_ref, cc_ref, sc_ref, cp_ref, sp_ref, qg_ref, kg_ref, do_ref,
             dq_ref, dk_ref, dv_ref, dqg_ref, dkg_ref, dsk_ref, dk_carry, dv_carry):
        h, i = pl.program_id(0), pl.program_id(1)
        lane = lax.broadcasted_iota(jnp.int32, (BLK, hd), 1)
        lane1 = lax.broadcasted_iota(jnp.int32, (1, LANE), 1)

        @pl.when((h == 0) & (i == 0))
        def _():
            dqg_ref[...] = jnp.zeros_like(dqg_ref)
            dkg_ref[...] = jnp.zeros_like(dkg_ref)
            dsk_ref[...] = jnp.zeros_like(dsk_ref)

        @pl.when(i == 0)
        def _():
            dk_carry[...] = jnp.zeros_like(dk_carry)
            dv_carry[...] = jnp.zeros_like(dv_carry)

        def rope(y, c, s):
            return y * c + _swap_halves(y, lane, half) * s

        def rope_bwd(dout, c, s):
            return dout * c + _swap_halves(dout * s, lane, half)

        @pl.when(i < nb)
        def _():
            cc, sc, cp, sp = cc_ref[...], sc_ref[...], cp_ref[...], sp_ref[...]
            qgv, kgv = qg_ref[...], kg_ref[...]
            xh_kc, rs_kc, y_kc = _norm_fwd(kc_ref[...], kgv)
            xh_kp, rs_kp, y_kp = _norm_fwd(kp_ref[...], kgv)
            kk = jnp.concatenate([rope(y_kp, cp, sp), rope(y_kc, cc, sc)], axis=0).astype(BF16)
            vv = jnp.concatenate([vp_ref[...], vc_ref[...]], axis=0).astype(BF16)
            mask = _attn_mask(i)
            dkk = jnp.zeros((2 * BLK, hd), F32)
            dvv = jnp.zeros((2 * BLK, hd), F32)
            dqg = jnp.zeros((1, hd), F32)
            dsk = jnp.zeros((1, LANE), F32)
            for g in range(grp):
                xh_q, rs_q, y_q = _norm_fwd(zq_ref[:, g * hd:(g + 1) * hd], qgv)
                q = rope(y_q, cc, sc).astype(BF16)
                s = lax.dot_general(q, kk, (((1,), (1,)), ((), ())), preferred_element_type=F32) * scale
                s = jnp.where(mask, s, NEG)
                sk = sink_ref[h * grp + g]
                mx = jnp.maximum(jnp.max(s, axis=-1, keepdims=True), sk)
                p = jnp.exp(s - mx)
                den = jnp.sum(p, axis=-1, keepdims=True) + jnp.exp(sk - mx)
                inv_den = 1.0 / den
                p = p * inv_den
                psink = jnp.exp(sk - mx) * inv_den
                dog = do_ref[:, g * hd:(g + 1) * hd].astype(BF16)
                dp = lax.dot_general(dog, vv, (((1,), (1,)), ((), ())), preferred_element_type=F32)
                rsum = jnp.sum(p * dp, axis=-1, keepdims=True)
                ds = (p * (dp - rsum) * scale).astype(BF16)
                dsk = dsk + jnp.where(lane1 == h * grp + g, jnp.sum(-psink * rsum, axis=0, keepdims=True), 0.0)
                dqn = jnp.dot(ds, kk, preferred_element_type=F32)
                dkk = dkk + lax.dot_general(ds, q, (((0,), (0,)), ((), ())), preferred_element_type=F32)
                dvv = dvv + lax.dot_general(p.astype(BF16), dog, (((0,), (0,)), ((), ())), preferred_element_type=F32)
                dxq, dg_q = _norm_bwd(rope_bwd(dqn, cc, sc), xh_q, rs_q, qgv)
                dq_ref[:, g * hd:(g + 1) * hd] = dxq.astype(BF16)
                dqg = dqg + dg_q
            dkp_raw, dg_kp = _norm_bwd(rope_bwd(dkk[:BLK], cp, sp), xh_kp, rs_kp, kgv)
            dkc_raw, dg_kc = _norm_bwd(rope_bwd(dkk[BLK:], cc, sc), xh_kc, rs_kc, kgv)
            dk_ref[...] = (dk_carry[...] + dkp_raw).astype(BF16)
            dv_ref[...] = (dv_carry[...] + dvv[:BLK]).astype(BF16)
            dk_carry[...] = dkc_raw
            dv_carry[...] = dvv[BLK:]
            dqg_ref[...] += dqg
            dkg_ref[...] += dg_kp + dg_kc
            dsk_ref[...] += dsk

        @pl.when(i == nb)
        def _():
            dk_ref[...] = dk_carry[...].astype(BF16)
            dv_ref[...] = dv_carry[...].astype(BF16)

    kvw = kv * hd
    vec = pl.BlockSpec((1, hd), lambda h, i: (0, 0))
    shifted = pl.BlockSpec((BLK, hd), lambda h, i: (jnp.maximum(i - 1, 0), h))
    res, extra = _call(
        body, name,
        (jax.ShapeDtypeStruct((t, kv * grp * hd), BF16), jax.ShapeDtypeStruct((t, kvw), BF16),
         jax.ShapeDtypeStruct((t, kvw), BF16), jax.ShapeDtypeStruct((1, hd), F32),
         jax.ShapeDtypeStruct((1, hd), F32), jax.ShapeDtypeStruct((1, LANE), F32)),
        (kv, nb + 1),
        [pl.BlockSpec(memory_space=pltpu.SMEM), zq, kc, kp, vc, vp, tc, tc, tp, tp, gs, gs,
         pl.BlockSpec((BLK, grp * hd), lambda h, i: (jnp.minimum(i, nb - 1), h))],
        (pl.BlockSpec((BLK, grp * hd), lambda h, i: (jnp.minimum(i, nb - 1), h)), shifted, shifted, vec, vec,
         pl.BlockSpec((1, LANE), lambda h, i: (0, 0))),
        (sinks, z, z, z, z, z, cos_t, sin_t, cos_t, sin_t, qg, kg, dattn),
        [pltpu.VMEM((BLK, hd), F32), pltpu.VMEM((BLK, hd), F32)], job)
    return res if job is None else (res, extra)


def _window(rows, cb, c0, row_of, col_of):
    assert rows % SUBLANE == 0 and cb % LANE == 0 and c0 % LANE == 0, (rows, cb, c0)
    return pl.BlockSpec((pl.Element(rows), pl.Element(cb)),
                        lambda *g: (pl.multiple_of(row_of(*g) * rows, SUBLANE), pl.multiple_of(c0 + col_of(*g) * cb, LANE)))


def _conv_fwd(z, c0, c, w, b, name):
    t = z.shape[0]
    taps = w.shape[0]
    cb = _pick(c, (1408, 1024, 512, 256, 128))
    tr = _row_tile(t, cb, 256 * 1024)
    hb = tr // SUBLANE

    def body(u_ref, halo_ref, w_ref, b_ref, o_ref):
        i = pl.program_id(0)
        x = u_ref[...]
        acc = b_ref[...] + w_ref[taps - 1:taps, :] * x
        for k in range(taps - 1):
            acc = acc + w_ref[k:k + 1, :] * pltpu.roll(x, taps - 1 - k, 0)
        o_ref[...] = acc
        row = lax.broadcasted_iota(jnp.int32, (SUBLANE, cb), 0)
        hp = jnp.where(i > 0, halo_ref[...], 0.0)
        x8 = u_ref[0:SUBLANE, :]
        acc8 = b_ref[...] + w_ref[taps - 1:taps, :] * x8
        for k in range(taps - 1):
            s = taps - 1 - k
            acc8 = acc8 + w_ref[k:k + 1, :] * jnp.where(row < s, pltpu.roll(hp, s, 0), pltpu.roll(x8, s, 0))
        o_ref[0:SUBLANE, :] = acc8

    blk = pl.BlockSpec((tr, cb), lambda i, j: (i, j))
    return pl.pallas_call(
        body, name=name, out_shape=jax.ShapeDtypeStruct((t, c), F32), grid=(t // tr, c // cb),
        in_specs=[_window(tr, cb, c0, lambda i, j: i, lambda i, j: j),
                  _window(SUBLANE, cb, c0, lambda i, j: jnp.maximum(i * hb - 1, 0), lambda i, j: j),
                  pl.BlockSpec((taps, cb), lambda i, j: (0, j)), pl.BlockSpec((1, cb), lambda i, j: (0, j))],
        out_specs=blk, compiler_params=_params(),
    )(z, z, w, b)


def _conv_bwd(duc, z, c0, w, name):
    t, c = duc.shape
    taps = w.shape[0]
    cb = _pick(c, (1408, 1024, 512, 256, 128))
    tr = _row_tile(t, cb, 256 * 1024)
    hb, nt = tr // SUBLANE, t // tr

    def body(g_ref, gnext_ref, u_ref, uprev_ref, w_ref, du16_ref, dw_ref, db_ref, du_ref):
        i = pl.program_id(1)

        @pl.when(i == 0)
        def _():
            dw_ref[...] = jnp.zeros_like(dw_ref)
            db_ref[...] = jnp.zeros_like(db_ref)

        row = lax.broadcasted_iota(jnp.int32, (SUBLANE, cb), 0)
        g, x = g_ref[...], u_ref[...]
        du = w_ref[taps - 1:taps, :] * g
        for k in range(taps - 1):
            du = du + w_ref[k:k + 1, :] * pltpu.roll(g, tr - (taps - 1 - k), 0)
        du_ref[...] = du
        hn = jnp.where(i < nt - 1, gnext_ref[...], 0.0)
        g8 = g_ref[tr - SUBLANE:tr, :]
        du8 = w_ref[taps - 1:taps, :] * g8
        for k in range(taps - 1):
            s = taps - 1 - k
            du8 = du8 + w_ref[k:k + 1, :] * jnp.where(row >= SUBLANE - s, pltpu.roll(hn, SUBLANE - s, 0),
                                                     pltpu.roll(g8, SUBLANE - s, 0))
        du_ref[tr - SUBLANE:tr, :] = du8
        du16_ref[...] = du_ref[...].astype(BF16)

        hp = jnp.where(i > 0, uprev_ref[...], 0.0)
        xl8, gf8 = u_ref[tr - SUBLANE:tr, :], g_ref[0:SUBLANE, :]
        db_ref[...] += jnp.sum(g, axis=0, keepdims=True)
        dw_ref[taps - 1:taps, :] += jnp.sum(g * x, axis=0, keepdims=True)
        for k in range(taps - 1):
            s = taps - 1 - k
            fix = jnp.where(row < s, pltpu.roll(hp, s, 0) - pltpu.roll(xl8, s, 0), 0.0)
            dw_ref[k:k + 1, :] += (jnp.sum(g * pltpu.roll(x, s, 0), axis=0, keepdims=True)
                                   + jnp.sum(gf8 * fix, axis=0, keepdims=True))

    blk = pl.BlockSpec((tr, cb), lambda j, i: (i, j))
    nh = t // SUBLANE
    return pl.pallas_call(
        body, name=name,
        out_shape=(jax.ShapeDtypeStruct((t, c), BF16), jax.ShapeDtypeStruct((taps, c), F32),
                   jax.ShapeDtypeStruct((1, c), F32)),
        grid=(c // cb, nt),
        in_specs=[blk, pl.BlockSpec((SUBLANE, cb), lambda j, i: (jnp.minimum((i + 1) * hb, nh - 1), j)),
                  _window(tr, cb, c0, lambda j, i: i, lambda j, i: j),
                  _window(SUBLANE, cb, c0, lambda j, i: jnp.maximum(i * hb - 1, 0), lambda j, i: j),
                  pl.BlockSpec((taps, cb), lambda j, i: (0, j))],
        out_specs=(blk, pl.BlockSpec((taps, cb), lambda j, i: (0, j)), pl.BlockSpec((1, cb), lambda j, i: (0, j))),
        scratch_shapes=[pltpu.VMEM((tr, cb), F32)], compiler_params=_params(),
    )(duc, duc, z, z, w)


def _gates_fwd(uc, wr, wi, gw, name):
    t, c = uc.shape
    n, bw, _ = wr.shape
    per, ng = gw // bw, c // gw
    tr = _pick(t, (512, 256, 128))

    def body(u_ref, wr_ref, wi_ref, r_ref, i_ref):
        for b in range(per):
            cols = slice(b * bw, (b + 1) * bw)
            a = u_ref[:, cols].astype(BF16)
            r_ref[:, cols] = jnp.dot(a, wr_ref[b].astype(BF16), preferred_element_type=F32)
            i_ref[:, cols] = jnp.dot(a, wi_ref[b].astype(BF16), preferred_element_type=F32)

    blk = pl.BlockSpec((tr, gw), lambda h, i: (i, h))
    wsp = pl.BlockSpec((per, bw, bw), lambda h, i: (h, 0, 0))
    shp = jax.ShapeDtypeStruct((t, c), F32)
    return pl.pallas_call(body, name=name, out_shape=(shp, shp), grid=(ng, t // tr), in_specs=[blk, wsp, wsp],
                          out_specs=(blk, blk), compiler_params=_params())(uc, wr, wi)


def _gates_bwd_x(duc, drp, dip, wr, wi, gw, name):
    t, c = duc.shape
    n, bw, _ = wr.shape
    per, ng = gw // bw, c // gw
    tr = _pick(t, (512, 256, 128))
    dims = (((1,), (1,)), ((), ()))

    def body(d_ref, r_ref, i_ref, wr_ref, wi_ref, o_ref):
        for b in range(per):
            cols = slice(b * bw, (b + 1) * bw)
            o_ref[:, cols] = (
                d_ref[:, cols]
                + lax.dot_general(r_ref[:, cols].astype(BF16), wr_ref[b].astype(BF16), dims, preferred_element_type=F32)
                + lax.dot_general(i_ref[:, cols].astype(BF16), wi_ref[b].astype(BF16), dims, preferred_element_type=F32))

    blk = pl.BlockSpec((tr, gw), lambda h, i: (i, h))
    wsp = pl.BlockSpec((per, bw, bw), lambda h, i: (h, 0, 0))
    return pl.pallas_call(body, name=name, out_shape=jax.ShapeDtypeStruct((t, c), F32), grid=(ng, t // tr),
                          in_specs=[blk, blk, blk, wsp, wsp], out_specs=blk, compiler_params=_params())(duc, drp, dip, wr, wi)


def _gates_bwd_w(uc, dpre, n, bw, gw, name):
    t, c = uc.shape
    per, ng = gw // bw, c // gw
    tk = _pick(t, (512, 256, 128))
    dims = (((0,), (0,)), ((), ()))

    def body(u_ref, d_ref, o_ref):
        @pl.when(pl.program_id(1) == 0)
        def _():
            o_ref[...] = jnp.zeros_like(o_ref)

        for b in range(per):
            cols = slice(b * bw, (b + 1) * bw)
            o_ref[b] += lax.dot_general(u_ref[:, cols].astype(BF16), d_ref[:, cols].astype(BF16), dims,
                                        preferred_element_type=F32)

    blk = pl.BlockSpec((tk, gw), lambda h, i: (i, h))
    return pl.pallas_call(body, name=name, out_shape=jax.ShapeDtypeStruct((n, bw, bw), F32), grid=(ng, t // tk),
                          in_specs=[blk, blk], out_specs=pl.BlockSpec((per, bw, bw), lambda h, i: (h, 0, 0)),
                          compiler_params=_params())(uc, dpre)


def _softplus(x):
    return jnp.maximum(x, 0.0) + jnp.log(1.0 + jnp.exp(-jnp.abs(x)))


_GELU_C = math.sqrt(2.0 / math.pi)


def _gelu_parts(x):
    inner = _GELU_C * (x + 0.044715 * (x * x * x))
    th = jnp.tanh(inner)
    gelu = 0.5 * x * (1.0 + th)
    dgelu = 0.5 * (1.0 + th) + 0.5 * x * (1.0 - th * th) * (_GELU_C * (1.0 + 3.0 * 0.044715 * (x * x)))
    return gelu, dgelu


def _lru_gate_values(rpre, ipre, br, bi, sp):
    r = _sigmoid(rpre + br)
    ig = _sigmoid(ipre + bi)
    log_a = -LRU_C * r * sp
    a = jnp.exp(log_a)
    e2 = jnp.tanh(-log_a) * (1.0 + a * a)
    inv = lax.rsqrt(jnp.maximum(e2, 1e-30))
    return r, ig, a, e2 * inv, inv


def _lru_fwd(uc, rpre, ipre, z, gr0, br, bi, lam, name, job=None):
    t, c = uc.shape
    cb = _pick(c, (1408, 1024, 512, 256, 128))
    tb = _pick(t, (512, 256, 128))
    ntile = tb // SUBLANE

    def body(uc_ref, r_ref, i_ref, gr_ref, br_ref, bi_ref, lam_ref, h_ref, rec16_ref, carry, rec_ref):
        @pl.when(pl.program_id(1) == 0)
        def _():
            carry[...] = jnp.zeros_like(carry)

        sp = _softplus(-lam_ref[...])
        br, bi = br_ref[...], bi_ref[...]
        row = lax.broadcasted_iota(jnp.int32, (SUBLANE, cb), 0)

        def tile(k, c_in):
            sl = pl.ds(pl.multiple_of(k * SUBLANE, SUBLANE), SUBLANE)
            ucv = uc_ref[sl, :]
            _, ig, a, mult, _ = _lru_gate_values(r_ref[sl, :], i_ref[sl, :], br, bi, sp)
            b = mult * (ig * ucv)
            for d in (1, 2, 4):
                a_s = jnp.where(row >= d, pltpu.roll(a, d, 0), 1.0)
                b_s = jnp.where(row >= d, pltpu.roll(b, d, 0), 0.0)
                b = a * b_s + b
                a = a * a_s
            hv = b + a * c_in
            h_ref[sl, :] = hv
            rec_ref[sl, :] = hv * _gelu_parts(gr_ref[sl, :])[0]
            return hv[SUBLANE - 1:SUBLANE, :]

        c_out = lax.fori_loop(0, ntile, tile, carry[0:1, :])
        carry[...] = jnp.broadcast_to(c_out, (SUBLANE, cb))
        rec16_ref[...] = rec_ref[...].astype(BF16)

    blk = pl.BlockSpec((tb, cb), lambda j, i: (i, j))
    vec = pl.BlockSpec((1, cb), lambda j, i: (0, j))
    res, extra = _call(body, name, (jax.ShapeDtypeStruct((t, c), F32), jax.ShapeDtypeStruct((t, c), BF16)),
                       (c // cb, t // tb),
                       [blk, blk, blk, _window(tb, cb, gr0, lambda j, i: i, lambda j, i: j), vec, vec, vec], (blk, blk),
                       (uc, rpre, ipre, z, br, bi, lam), [pltpu.VMEM((SUBLANE, cb), F32), pltpu.VMEM((tb, cb), F32)], job)
    return res if job is None else (res, extra)


def _lru_bwd(drec, hst, uc, rpre, ipre, z, gr0, br, bi, lam, name, job=None):
    t, c = uc.shape
    cb = _pick(c, (1408, 1024, 512, 256, 128))
    tb = _pick(t, (256, 128))
    ntile, nt, hb = tb // SUBLANE, t // tb, tb // SUBLANE

    def body(drec_ref, h_ref, hprev_ref, uc_ref, r_ref, i_ref, gr_ref, br_ref, bi_ref, lam_ref,
             dgr16_ref, drp_ref, dip_ref, duc_ref, dlam_ref, dbr_ref, dbi_ref, carry, dgr_ref):
        step = pl.program_id(1)
        first_block = step == nt - 1

        @pl.when(step == 0)
        def _():
            carry[...] = jnp.zeros_like(carry)
            dlam_ref[...] = jnp.zeros_like(dlam_ref)
            dbr_ref[...] = jnp.zeros_like(dbr_ref)
            dbi_ref[...] = jnp.zeros_like(dbi_ref)

        lam = lam_ref[...]
        sp = _softplus(-lam)
        br, bi = br_ref[...], bi_ref[...]
        row = lax.broadcasted_iota(jnp.int32, (SUBLANE, cb), 0)
        halo = jnp.where(first_block, 0.0, hprev_ref[...])

        def tile(kk, state):
            c_p, acc_sp, acc_br, acc_bi = state
            k = ntile - 1 - kk
            sl = pl.ds(pl.multiple_of(k * SUBLANE, SUBLANE), SUBLANE)
            slp = pl.ds(pl.multiple_of(jnp.maximum(k - 1, 0) * SUBLANE, SUBLANE), SUBLANE)
            ucv = uc_ref[sl, :]
            r, ig, a, mult, inv_mult = _lru_gate_values(r_ref[sl, :], i_ref[sl, :], br, bi, sp)
            hv = h_ref[sl, :]
            below = jnp.where(k > 0, h_ref[slp, :], halo)
            hprev = jnp.where(row == 0, pltpu.roll(below, 1, 0), pltpu.roll(hv, 1, 0))
            gelu, dgelu = _gelu_parts(gr_ref[sl, :])
            drec = drec_ref[sl, :]
            dh = drec * gelu
            dgr_ref[sl, :] = drec * hv * dgelu
            pa, pb = a, a * dh
            for d in (1, 2, 4):
                a_s = jnp.where(row < SUBLANE - d, pltpu.roll(pa, SUBLANE - d, 0), 1.0)
                b_s = jnp.where(row < SUBLANE - d, pltpu.roll(pb, SUBLANE - d, 0), 0.0)
                pb = pa * b_s + pb
                pa = pa * a_s
            pv = pb + pa * c_p
            gt = dh + jnp.where(row == SUBLANE - 1, c_p, pltpu.roll(pv, SUBLANE - 1, 0))
            da = gt * hprev
            duc_ref[sl, :] = gt * mult * ig
            dmult = gt * ig * ucv
            dig = gt * mult * ucv
            dla = da * a - jnp.where(mult > 0.0, dmult * (a * a) * inv_mult, 0.0)
            drp = dla * (-LRU_C * sp) * (r * (1.0 - r))
            dip = dig * (ig * (1.0 - ig))
            drp_ref[sl, :] = drp
            dip_ref[sl, :] = dip
            return pv[0:1, :], acc_sp + dla * (-LRU_C * r), acc_br + drp, acc_bi + dip

        zero = jnp.zeros((SUBLANE, cb), F32)
        c_out, acc_sp, acc_br, acc_bi = lax.fori_loop(0, ntile, tile, (carry[0:1, :], zero, zero, zero))
        carry[...] = jnp.broadcast_to(c_out, (SUBLANE, cb))
        dlam_ref[...] += jnp.sum(acc_sp, axis=0, keepdims=True) * (-_sigmoid(-lam))
        dbr_ref[...] += jnp.sum(acc_br, axis=0, keepdims=True)
        dbi_ref[...] += jnp.sum(acc_bi, axis=0, keepdims=True)
        dgr16_ref[...] = dgr_ref[...].astype(BF16)

    blk = pl.BlockSpec((tb, cb), lambda j, i: (nt - 1 - i, j))
    vec = pl.BlockSpec((1, cb), lambda j, i: (0, j))
    halo_spec = pl.BlockSpec((SUBLANE, cb), lambda j, i: (jnp.maximum((nt - 1 - i) * hb - 1, 0), j))
    big, small = jax.ShapeDtypeStruct((t, c), F32), jax.ShapeDtypeStruct((1, c), F32)
    res, extra = _call(
        body, name, (jax.ShapeDtypeStruct((t, c), BF16), big, big, big, small, small, small), (c // cb, nt),
        [blk, blk, halo_spec, blk, blk, blk, _window(tb, cb, gr0, lambda j, i: nt - 1 - i, lambda j, i: j),
         vec, vec, vec], (blk, blk, blk, blk, vec, vec, vec),
        (drec, hst, hst, uc, rpre, ipre, z, br, bi, lam),
        [pltpu.VMEM((SUBLANE, cb), F32), pltpu.VMEM((tb, cb), F32)], job)
    return res if job is None else (res, extra)


def _shard_region(ref, kind, chip, half, rh, width):
    if kind == "col":
        return ref.at[pl.ds(half * rh, rh), pl.ds(chip * width, width)]
    return ref.at[pl.ds(chip * (2 * rh) + half * rh, rh), :]


class _AllGather(_Exchange):
    def __init__(self, fulls, kinds):
        self.inputs, self.kinds = list(fulls), kinds
        self.out_shapes = [jax.ShapeDtypeStruct(f.shape, f.dtype) for f in fulls]
        self.aliases = {a: a for a in range(len(fulls))}
        self.n_sems = 6 * len(fulls)
        self.geo = [(f.shape[0] // 2, f.shape[1] // N_CHIPS) if k == "col" else (f.shape[0] // (2 * N_CHIPS), f.shape[1])
                    for f, k in zip(fulls, kinds)]

    def _region(self, ref, a, chip, half):
        return _shard_region(ref, self.kinds[a], chip, half, *self.geo[a])

    def _ici(self, e, a, k, chip):
        cx, cy = e.chips[k]
        return e.copy(self._region(e.ins[a], a, chip, e.c), self._region(e.outs[a], a, chip, e.c), a * 6 + k,
                      (cx, cy, e.c))

    def _d2d(self, e, a, k, half):
        cx, cy = e.chips[k]
        region = self._region(e.outs[a], a, 2 * cx + cy, half)
        return e.copy(region, region, a * 6 + 3 + k, e.sibling)

    def start(self, e):
        for a in range(len(self.inputs)):
            for k in range(3):
                self._ici(e, a, k, e.me).start()

    def finish(self, e):
        n = len(self.inputs)
        for a in range(n):
            for k, (cx, cy) in enumerate(e.chips):
                self._ici(e, a, k, 2 * cx + cy).wait_recv()
                self._d2d(e, a, k, e.c).start()
        for a in range(n):
            for k in range(3):
                self._d2d(e, a, k, 1 - e.c).wait_recv()
        for a in range(n):
            for k in range(3):
                self._ici(e, a, k, e.me).wait_send()
                self._d2d(e, a, k, e.c).wait_send()


class _SiblingExchange(_Exchange):
    def __init__(self, grads):
        self.inputs = list(grads)
        self.out_shapes = [jax.ShapeDtypeStruct((g.shape[0],) + g.shape[2:], g.dtype) for g in grads]
        self.n_sems = len(grads)

    def _copy(self, e, a):
        return e.copy(e.ins[a].at[:, 1 - e.c], e.outs[a], a, e.sibling)

    def start(self, e):
        for a in range(len(self.inputs)):
            self._copy(e, a).start()

    def finish(self, e):
        for a in range(len(self.inputs)):
            self._copy(e, a).wait()


def _piece(ref, kind, chip, width):
    if kind == "col":
        return ref.at[0, :, pl.ds(chip * width, width)]
    return ref.at[chip]


class _ChipExchange(_Exchange):
    def __init__(self, sums, kinds):
        self.inputs, self.kinds = list(sums), kinds
        self.widths = [s.shape[2] // N_CHIPS if k == "col" else s.shape[2] for s, k in zip(sums, kinds)]
        self.out_shapes = [jax.ShapeDtypeStruct((3, s.shape[1], w), s.dtype) for s, w in zip(sums, self.widths)]
        self.n_sems = 3 * len(sums)

    def _copy(self, e, a, k, chip):
        cx, cy = e.chips[k]
        return e.copy(_piece(e.ins[a], self.kinds[a], chip, self.widths[a]), e.outs[a].at[k], a * 3 + k, (cx, cy, e.c))

    def start(self, e):
        for a in range(len(self.inputs)):
            for k, (cx, cy) in enumerate(e.chips):
                self._copy(e, a, k, 2 * cx + cy).start()

    def finish(self, e):
        for a in range(len(self.inputs)):
            for k, (cx, cy) in enumerate(e.chips):
                self._copy(e, a, k, 2 * cx + cy).wait()


class _FinishExchange(_Exchange):
    def __init__(self, finals, to_all):
        self.inputs, self.to_all = list(finals), list(to_all)
        self.out_shapes = [jax.ShapeDtypeStruct(f.shape, f.dtype) for f in finals]
        self.aliases = {a: a for a in range(len(finals))}
        self.first_sem, self.n_sems = [], 0
        for all8 in self.to_all:
            self.first_sem.append(self.n_sems)
            self.n_sems += 7 if all8 else 1
        self.rel = [(fx, fy, fc) for fx in (0, 1) for fy in (0, 1) for fc in (0, 1)][1:]

    def _copies(self, e, mine):
        for a, all8 in enumerate(self.to_all):
            src = e.ins[a] if mine else e.outs[a]
            if not all8:
                rh = self.inputs[a].shape[0] // 2
                rows = pl.ds((e.c if mine else 1 - e.c) * rh, rh)
                yield e.copy(src.at[rows, :], e.outs[a].at[rows, :], self.first_sem[a], e.sibling)
                continue
            rh = self.inputs[a].shape[0] // (2 * N_CHIPS)
            for r, (fx, fy, fc) in enumerate(self.rel):
                px, py, pc = (1 - e.x if fx else e.x), (1 - e.y if fy else e.y), (1 - e.c if fc else e.c)
                rows = pl.ds(((2 * e.me + e.c) if mine else (2 * (2 * px + py) + pc)) * rh, rh)
                yield e.copy(src.at[rows, :], e.outs[a].at[rows, :], self.first_sem[a] + r, (px, py, pc))

    def start(self, e):
        for cp in self._copies(e, True):
            cp.start()

    def finish(self, e):
        for cp in self._copies(e, False):
            cp.wait_recv()
        for cp in self._copies(e, True):
            cp.wait_send()


def _cast_into_full(w, kind, idx, name):
    r, c = w.shape
    tr = _row_tile(r, c)
    nrb = r // tr

    def body(idx_ref, w_ref, o_ref):
        o_ref[...] = w_ref[...].astype(BF16)

    if kind == "col":
        full, out_map = (r, N_CHIPS * c), (lambda i, idx_ref: (i, idx_ref[1]))
    else:
        full, out_map = (N_CHIPS * r, c), (lambda i, idx_ref: (idx_ref[1] * nrb + i, 0))
    return pl.pallas_call(
        body, name=name, out_shape=jax.ShapeDtypeStruct(full, BF16),
        grid_spec=pltpu.PrefetchScalarGridSpec(
            num_scalar_prefetch=1, grid=(nrb,), in_specs=[pl.BlockSpec((tr, c), lambda i, idx_ref: (i, 0))],
            out_specs=pl.BlockSpec((tr, c), out_map)),
        compiler_params=_params(),
    )(idx, w)


def _matmul_gathering(a, placed, order, name):
    t, k = a.shape
    n = placed.shape[1]
    w = n // N_CHIPS
    tm, tn = _pick(t, _M_TILES), _pick(w, _N_TILES)
    ni, nj = t // tm, w // tn
    per_shard, total = ni * nj, N_CHIPS * ni * nj
    gather = _AllGather([placed], ["col"])

    def body(ord_ref, a_ref, w_own_ref, o_ref, w_ref, wbuf, fetch_sem, send, recv):
        s, i, j = pl.program_id(0), pl.program_id(1), pl.program_id(2)
        step = (s * ni + i) * nj + j
        e = _Env((w_own_ref,), (w_ref,), send, recv)

        def fetch(src, st):
            col = pl.multiple_of((ord_ref[st // per_shard] * nj + st % nj) * tn, LANE)
            return pltpu.make_async_copy(src.at[:, pl.ds(col, tn)], wbuf.at[st % 2], fetch_sem.at[st % 2])

        @pl.when(step == 0)
        def _():
            gather.start(e)
            fetch(w_own_ref, step).start()

        nxt = step + 1
        for kk, (cx, cy) in enumerate(e.chips):
            @pl.when(nxt == (kk + 1) * per_shard)
            def _():
                gather._ici(e, 0, kk, 2 * cx + cy).wait_recv()
                gather._d2d(e, 0, kk, e.c).start()
                gather._d2d(e, 0, kk, 1 - e.c).wait_recv()

        @pl.when(nxt < per_shard)
        def _():
            fetch(w_own_ref, nxt).start()

        @pl.when((nxt >= per_shard) & (nxt < total))
        def _():
            fetch(w_ref, nxt).start()

        fetch(w_ref, step).wait()
        o_ref[...] = jnp.dot(a_ref[...], wbuf[step % 2], preferred_element_type=F32)

        @pl.when(step == total - 1)
        def _():
            for kk in range(3):
                gather._ici(e, 0, kk, e.me).wait_send()
                gather._d2d(e, 0, kk, e.c).wait_send()

    z, full = pl.pallas_call(
        body, name=name, out_shape=(jax.ShapeDtypeStruct((t, n), F32), jax.ShapeDtypeStruct(placed.shape, placed.dtype)),
        grid_spec=pltpu.PrefetchScalarGridSpec(
            num_scalar_prefetch=1, grid=(N_CHIPS, ni, nj),
            in_specs=[pl.BlockSpec((tm, k), lambda s, i, j, ord_ref: (i, 0)), ANY],
            out_specs=(pl.BlockSpec((tm, tn), lambda s, i, j, ord_ref: (i, ord_ref[s] * nj + j)), ANY),
            scratch_shapes=[pltpu.VMEM((2, k, tn), placed.dtype), pltpu.SemaphoreType.DMA((2,)),
                            pltpu.SemaphoreType.DMA((gather.n_sems,)), pltpu.SemaphoreType.DMA((gather.n_sems,))]),
        input_output_aliases={2: 1}, compiler_params=_params(),
    )(order, a, placed)
    return z, full


def _add_own_half(g4, recv, idx, out_dtype, name):
    p, _, rh, n = g4.shape
    tr, tc = _tile2d(rh, n, 1024 * 1024)

    def body(idx_ref, g_ref, r_ref, o_ref):
        o_ref[...] = (g_ref[...] + r_ref[...]).astype(out_dtype)

    return pl.pallas_call(
        body, name=name, out_shape=jax.ShapeDtypeStruct((p, rh, n), out_dtype),
        grid_spec=pltpu.PrefetchScalarGridSpec(
            num_scalar_prefetch=1, grid=(p, rh // tr, n // tc),
            in_specs=[pl.BlockSpec((None, None, tr, tc), lambda q, i, j, idx_ref: (q, idx_ref[0], i, j)),
                      pl.BlockSpec((None, tr, tc), lambda q, i, j, idx_ref: (q, i, j))],
            out_specs=pl.BlockSpec((None, tr, tc), lambda q, i, j, idx_ref: (q, i, j))),
        compiler_params=_params(),
    )(idx, g4, recv)


def _sum_chips(own, kind, parts, idx, slots, to_all, name):
    _, rh, w = parts.shape
    tr, tc = _tile2d(rh, w, 512 * 1024)
    nrb, ncb = rh // tr, w // tc

    def body(idx_ref, own_ref, p0, p1, p2, o_ref):
        o_ref[...] = ((own_ref[...].astype(F32) + p0[...].astype(F32)) + p1[...].astype(F32)) + p2[...].astype(F32)

    if kind == "col":
        own_spec = pl.BlockSpec((None, tr, tc), lambda i, j, idx_ref: (0, i, idx_ref[1] * ncb + j))
    else:
        own_spec = pl.BlockSpec((None, tr, tc), lambda i, j, idx_ref: (idx_ref[1], i, j))
    if to_all:
        out_map = lambda i, j, idx_ref: ((2 * idx_ref[1] + idx_ref[0]) * nrb + i, j)
    else:
        out_map = lambda i, j, idx_ref: (idx_ref[0] * nrb + i, j)

    def part(k):
        return pl.BlockSpec((None, tr, tc), lambda i, j, idx_ref: (k, i, j))

    return pl.pallas_call(
        body, name=name, out_shape=jax.ShapeDtypeStruct((slots * rh, w), F32),
        grid_spec=pltpu.PrefetchScalarGridSpec(
            num_scalar_prefetch=1, grid=(nrb, ncb), in_specs=[own_spec, part(0), part(1), part(2)],
            out_specs=pl.BlockSpec((tr, tc), out_map)),
        compiler_params=_params(),
    )(idx, own, parts, parts, parts)


class _Reduce:
    def __init__(self, name, g, kind, idx, wire, to_all):
        r, c = g.shape
        self.name, self.kind, self.idx, self.wire, self.to_all = name, kind, idx, wire, to_all
        self.view = g.reshape(1, 2, r // 2, c) if kind == "col" else g.reshape(N_CHIPS, 2, r // (2 * N_CHIPS), c)

    def sibling(self):
        return _SiblingExchange([self.view])

    def got_sibling(self, outs):
        self.sum = _add_own_half(self.view, outs[0], self.idx, self.wire, "grad_chip_sum_" + self.name)

    def chips(self):
        return _ChipExchange([self.sum], [self.kind])

    def got_chips(self, outs):
        self.total = _sum_chips(self.sum, self.kind, outs[0], self.idx, 2 * N_CHIPS if self.to_all else 2,
                                self.to_all, "grad_total_" + self.name)


def _pack(arrays, rows):
    flat = jnp.concatenate([a.reshape(-1) for a in arrays])
    return jnp.pad(flat, (0, rows * SMALL_PACK_COLS - flat.shape[0])).reshape(rows, SMALL_PACK_COLS)


def _unpack(packed, shapes):
    flat = packed.reshape(-1)
    out, o = [], 0
    for shp in shapes:
        size = math.prod(shp)
        out.append(flat[o:o + size].reshape(shp))
        o += size
    return out


def _pack_rows(shapes):
    total = sum(math.prod(s) for s in shapes)
    unit = SMALL_PACK_COLS * N_CHIPS * 2 * SUBLANE
    return -(-total // unit) * (N_CHIPS * 2 * SUBLANE)


BIG = ("w_in", "w_attn_proj", "w_lru_proj", "w_out", "w_ffn_gate", "w_ffn_up", "w_ffn_down")
BIG_KIND = {"w_in": "col", "w_attn_proj": "row", "w_lru_proj": "row", "w_out": "row", "w_ffn_gate": "col",
            "w_ffn_up": "col", "w_ffn_down": "row"}
SMALL = ("norm1_g", "b_gates", "q_norm_g", "k_norm_g", "sinks", "conv_w", "conv_b", "w_rgate", "b_rgate",
         "w_igate", "b_igate", "lru_lambda", "norm2_g")
PACKED = tuple(n for n in SMALL if n not in ("w_rgate", "w_igate"))
WEIGHTS = ("norm1_g", "w_in", "b_gates", "q_norm_g", "k_norm_g", "sinks", "conv_w", "conv_b", "w_rgate", "b_rgate",
           "w_igate", "b_igate", "lru_lambda", "w_attn_proj", "w_lru_proj", "w_out", "norm2_g", "w_ffn_gate",
           "w_ffn_up", "w_ffn_down")


def kernel(x, positions, norm1_g, w_in, b_gates, q_norm_g, k_norm_g, sinks, conv_w, conv_b, w_rgate, b_rgate, w_igate, b_igate, lru_lambda, w_attn_proj, w_lru_proj, w_out, norm2_g, w_ffn_gate, w_ffn_up, w_ffn_down, loss_target, m_norm1_g, m_w_in, m_b_gates, m_q_norm_g, m_k_norm_g, m_sinks, m_conv_w, m_conv_b, m_w_rgate, m_b_rgate, m_w_igate, m_b_igate, m_lru_lambda, m_w_attn_proj, m_w_lru_proj, m_w_out, m_norm2_g, m_w_ffn_gate, m_w_ffn_up, m_w_ffn_down, v_norm1_g, v_w_in, v_b_gates, v_q_norm_g, v_k_norm_g, v_sinks, v_conv_w, v_conv_b, v_w_rgate, v_b_rgate, v_w_igate, v_b_igate, v_lru_lambda, v_w_attn_proj, v_w_lru_proj, v_w_out, v_norm2_g, v_w_ffn_gate, v_w_ffn_up, v_w_ffn_down):
    args = dict(locals())
    w = {n: args[n] for n in WEIGHTS}
    mom = {n: args["m_" + n] for n in WEIGHTS}
    var = {n: args["v_" + n] for n in WEIGHTS}

    t, d = x.shape[1], x.shape[2]
    hd = q_norm_g.shape[-1]
    nq = sinks.shape[-1]
    q_w = nq * hd
    d_rnn = conv_b.shape[-1]
    taps = conv_w.shape[1]
    n_blocks, bw = w_rgate.shape[1], w_rgate.shape[2]
    in_w = w_in.shape[-1] * N_CHIPS
    kv_w = (in_w - q_w - 2 * d_rnn - 2 * d) // 2
    kv = kv_w // hd
    grp = nq // kv
    u_off = q_w + 2 * kv_w
    gr_off = u_off + d_rnn
    ga_off = gr_off + d_rnn
    gw = bw * LANE // math.gcd(bw, LANE)
    chip = 2 * lax.axis_index("x") + lax.axis_index("y")
    idx = jnp.stack([lax.axis_index("c"), chip]).astype(jnp.int32)

    x2, tgt = x[0], loss_target[0]

    placed = {n: _cast_into_full(w[n][0], BIG_KIND[n], idx, "cast_" + n) for n in BIG}

    def gather(*names):
        return _AllGather([placed[n] for n in names], [BIG_KIND[n] for n in names])

    mx, my = lax.axis_index("x"), lax.axis_index("y")
    order = jnp.stack([chip, 2 * (1 - mx) + my, 2 * mx + (1 - my), 2 * (1 - mx) + (1 - my)]).astype(jnp.int32)
    conv_w_full = _gather_small(conv_w[0], "allgather_conv_w")
    conv_w_full = jnp.transpose(conv_w_full, (1, 0, 2)).reshape(taps, d_rnn)

    inv_freq = ROPE_THETA ** (-jnp.arange(0, hd // 4, 2, dtype=F32) / (hd // 4))
    ang = positions[0].astype(F32)[:, None] * inv_freq
    cos, sin = jnp.cos(ang), jnp.sin(ang)
    rest = hd - 2 * cos.shape[1]
    cos_t = jnp.concatenate([cos, cos, jnp.ones((t, rest), F32)], axis=1)
    sin_t = jnp.concatenate([-sin, sin, jnp.zeros((t, rest), F32)], axis=1)
    sinks1 = sinks[0]

    xn = _rms_fwd(x2, norm1_g, "rms1_fwd")
    z, win_f = _matmul_gathering(xn, placed["w_in"], order, "in_proj")
    attn, (wap_f, wlp_f, wout_f) = _attn_fwd(z, cos_t, sin_t, q_norm_g, k_norm_g, sinks1, kv, grp, hd, "attn_fwd",
                                             job=gather("w_attn_proj", "w_lru_proj", "w_out"))
    uc = _conv_fwd(z, u_off, d_rnn, conv_w_full, conv_b, "conv_fwd")
    rpre, ipre = _gates_fwd(uc, w_rgate[0], w_igate[0], gw, "gates_fwd")
    (hst, rec), (wg_f,) = _lru_fwd(uc, rpre, ipre, z, gr_off, b_rgate, b_igate, lru_lambda, "lru_fwd",
                                   job=gather("w_ffn_gate"))
    pa = _matmul(attn, wap_f, "nn", "attn_proj")
    plru = _matmul(rec, wlp_f, "nn", "lru_proj")
    merged = _merge_fwd(z, b_gates, pa, plru, ga_off, "merge_fwd")
    h1 = _matmul(merged, wout_f, "nn", "out_proj", add=x2)
    hn = _rms_fwd(h1, norm2_g, "rms2_fwd")
    gate, (wu_f,) = _matmul(hn, wg_f, "nn", "ffn_gate", job=gather("w_ffn_up"))
    (up, act), (wd_f,) = _matmul(hn, wu_f, "nn", "ffn_up", job=gather("w_ffn_down"),
                                 fused=([gate], _swiglu_after_up, (F32, BF16)))
    yout = _matmul(act, wd_f, "nn", "ffn_down", add=h1)
    dy, dy16, loss_part = _loss_head(yout, tgt, "loss_head")
    loss = lax.psum(loss_part[0, 0], ("x", "y", "c"))

    def reduction(n, g):
        return _Reduce(n, g, BIG_KIND[n], idx, BF16, False)

    r_wd = reduction("w_ffn_down", _matmul(act, dy16, "tn", "d_w_ffn_down"))
    (dgate, dup), got = _matmul(dy16, wd_f, "nt", "d_act", job=r_wd.sibling(),
                                fused=([gate, up], _swiglu_bwd_after_dact, (BF16, BF16)))
    r_wd.got_sibling(got)
    r_wg = reduction("w_ffn_gate", _matmul(hn, dgate, "tn", "d_w_ffn_gate"))
    g_wu, got = _matmul(hn, dup, "tn", "d_w_ffn_up", job=r_wg.sibling())
    r_wg.got_sibling(got)
    r_wu = reduction("w_ffn_up", g_wu)
    dhn, got = _matmul(dgate, wg_f, "nt", "d_hn_gate", job=r_wu.sibling())
    r_wu.got_sibling(got)
    dhn = _matmul(dup, wu_f, "nt", "d_hn_up", add=dhn)
    dh1, g_norm2, dh1_16 = _rms_bwd(dhn, h1, norm2_g, dy, "rms2_bwd", mxu_copy=True)
    r_wout = reduction("w_out", _matmul(merged, dh1_16, "tn", "d_w_out"))
    dmerged, got = _matmul(dh1_16, wout_f, "nt", "d_merged", job=r_wout.sibling())
    r_wout.got_sibling(got)
    (dpa, dpl, dga, dgl, g_ba, g_bl), got = _merge_bwd(dmerged, z, b_gates, pa, plru, ga_off, "merge_bwd",
                                                       job=r_wout.chips())
    r_wout.got_chips(got)
    r_wap = reduction("w_attn_proj", _matmul(attn, dpa, "tn", "d_w_attn_proj"))
    dattn, got = _matmul(dpa, wap_f, "nt", "d_attn", job=r_wap.sibling())
    r_wap.got_sibling(got)
    g_wlp, got = _matmul(rec, dpl, "tn", "d_w_lru_proj", job=r_wap.chips())
    r_wap.got_chips(got)
    r_wlp = reduction("w_lru_proj", g_wlp)
    drec, got = _matmul(dpl, wlp_f, "nt", "d_rec", job=r_wlp.sibling())
    r_wlp.got_sibling(got)
    both = _Jobs(r_wlp.chips(), r_wg.chips())
    (dgr, drp, dip, duc_direct, g_lam, g_br, g_bi), got = _lru_bwd(
        drec, hst, uc, rpre, ipre, z, gr_off, b_rgate, b_igate, lru_lambda, "lru_bwd", job=both)
    got_wlp, got_wg = both.split(got)
    r_wlp.got_chips(got_wlp)
    r_wg.got_chips(got_wg)
    duc = _gates_bwd_x(duc_direct, drp, dip, w_rgate[0], w_igate[0], gw, "gates_bwd_x")
    g_wr = _gates_bwd_w(uc, drp, n_blocks, bw, gw, "gates_bwd_wr")
    g_wi = _gates_bwd_w(uc, dip, n_blocks, bw, gw, "gates_bwd_wi")
    du, g_convw, g_convb = _conv_bwd(duc, z, u_off, conv_w_full, "conv_bwd")
    (dq, dk, dv, g_qg, g_kg, g_sinks), got = _attn_bwd(dattn, z, cos_t, sin_t, q_norm_g, k_norm_g, sinks1, kv, grp, hd,
                                                        "attn_bwd", job=_Jobs(r_wu.chips(), r_wd.chips()))
    r_wu.got_chips(got[:1])
    r_wd.got_chips(got[1:])
    dz = jnp.concatenate([dq, dk, dv, du, dgr, dga, dgl], axis=1)
    r_wr = _Reduce("w_rgate", g_wr.reshape(n_blocks * bw, bw), "row", idx, F32, True)
    r_wi = _Reduce("w_igate", g_wi.reshape(n_blocks * bw, bw), "row", idx, F32, True)
    early = [r_wap, r_wlp, r_wout, r_wg, r_wu, r_wd]
    three = _Jobs(r_wr.sibling(), r_wi.sibling(), _FinishExchange([r.total for r in early], [False] * len(early)))
    g_top, got = _matmul(xn, dz, "tn", "d_w_in_top", m_window=(0, d // 2), job=three)
    got_wr, got_wi, finished = three.split(got)
    r_wr.got_sibling(got_wr)
    r_wi.got_sibling(got_wi)
    r_top = _Reduce("w_in_top", g_top, "col", idx, BF16, False)
    three = _Jobs(r_top.sibling(), r_wr.chips(), r_wi.chips())
    g_bot, got = _matmul(xn, dz, "tn", "d_w_in_bot", m_window=(d // 2, d // 2), job=three)
    got_top, got_wr, got_wi = three.split(got)
    r_top.got_sibling(got_top)
    r_wr.got_chips(got_wr)
    r_wi.got_chips(got_wi)
    r_bot = _Reduce("w_in_bot", g_bot, "col", idx, BF16, False)
    both = _Jobs(r_top.chips(), r_bot.sibling())
    dxn, got = _matmul(dz, win_f, "nt", "d_xn_a", m_window=(0, t // 2), into=(None, t), job=both)
    got_top, got_bot = both.split(got)
    r_top.got_chips(got_top)
    r_bot.got_sibling(got_bot)
    dxn, got = _matmul(dz, win_f, "nt", "d_xn_b", m_window=(t // 2, t // 2), into=(dxn, t), job=r_bot.chips())
    r_bot.got_chips(got)
    dx, g_norm1 = _rms_bwd(dxn, x2, norm1_g, dh1, "rms1_bwd")

    small_grads = {"norm1_g": g_norm1, "b_gates": jnp.concatenate([g_ba, g_bl], axis=1), "q_norm_g": g_qg,
                   "k_norm_g": g_kg, "sinks": g_sinks[:, :nq], "conv_w": g_convw, "conv_b": g_convb,
                   "b_rgate": g_br, "b_igate": g_bi, "lru_lambda": g_lam, "norm2_g": g_norm2}
    gshapes = [small_grads[n].shape for n in PACKED]
    small_sum = _allreduce_small(_pack([small_grads[n] for n in PACKED], _pack_rows(gshapes)), "allreduce_small")
    top, bot, grads_wr, grads_wi = _run_exchange(
        _FinishExchange([r.total for r in (r_top, r_bot, r_wr, r_wi)], [False, False, True, True]),
        "grad_finish_exchange")
    grads = dict(zip(BIG[1:], finished))
    grads["w_in"] = jnp.concatenate([top, bot], axis=0)
    grads["w_rgate"], grads["w_igate"] = grads_wr, grads_wi
    small_full = dict(zip(PACKED, _unpack(small_sum, gshapes)))
    per = d_rnn // N_CHIPS
    small_full["conv_w"] = lax.dynamic_slice(small_full["conv_w"], (0, chip * per), (taps, per))
    grads.update(small_full)

    delta, new_m, new_v = {}, {}, {}
    for n in BIG + ("w_rgate", "w_igate"):
        as2d = (lambda a: a[0]) if n in BIG else (lambda a: a.reshape(n_blocks * bw, bw))
        delta[n], new_m[n], new_v[n] = _adamw(as2d(w[n]), grads[n], as2d(mom[n]), as2d(var[n]), "adamw_" + n)
    pshapes = [w[n].shape for n in PACKED]
    prows = _pack_rows(pshapes)
    pk = [_pack([src[n] for n in PACKED], prows) for src in (w, grads, mom, var)]
    for res, packed in zip((delta, new_m, new_v), _adamw(pk[0], pk[1], pk[2], pk[3], "adamw_small")):
        res.update(dict(zip(PACKED, _unpack(packed, pshapes))))

    outs = [loss, dx.reshape(x.shape)]
    for res in (grads, delta, new_m, new_v):
        outs += [res[n].reshape(w[n].shape) for n in WEIGHTS]
    return tuple(outs)


def _allreduce_small(x, name):
    n_dev = 2 * N_CHIPS
    rel = [(fx, fy, fc) for fx in (0, 1) for fy in (0, 1) for fc in (0, 1)][1:]

    def body(x_ref, all_ref, o_ref, send_sems, recv_sems):
        e = _Env((x_ref,), (all_ref,), send_sems, recv_sems)
        mine = 2 * e.me + e.c

        def peer(r):
            fx, fy, fc = rel[r]
            return (1 - e.x if fx else e.x), (1 - e.y if fy else e.y), (1 - e.c if fc else e.c)

        all_ref[mine] = x_ref[...]
        for r in range(len(rel)):
            e.copy(x_ref, all_ref.at[mine], r, peer(r)).start()
        for r in range(len(rel)):
            px, py, pc = peer(r)
            e.copy(x_ref, all_ref.at[2 * (2 * px + py) + pc], r, peer(r)).wait_recv()
        total = all_ref[0]
        for dev in range(1, n_dev):
            total = total + all_ref[dev]
        o_ref[...] = total
        for r in range(len(rel)):
            e.copy(x_ref, all_ref.at[mine], r, peer(r)).wait_send()

    vm = pl.BlockSpec(memory_space=pltpu.VMEM)
    return pl.pallas_call(
        body, name=name, out_shape=(jax.ShapeDtypeStruct((n_dev,) + x.shape, x.dtype), jax.ShapeDtypeStruct(x.shape, x.dtype)),
        in_specs=[vm], out_specs=(vm, vm),
        scratch_shapes=[pltpu.SemaphoreType.DMA((len(rel),)), pltpu.SemaphoreType.DMA((len(rel),))])(x)[1]


def _gather_small(shard, name):
    def body(s_ref, o_ref, send_sems, recv_sems):
        e = _Env((s_ref,), (o_ref,), send_sems, recv_sems)
        o_ref[e.me] = s_ref[...]
        for k, (cx, cy) in enumerate(e.chips):
            e.copy(s_ref, o_ref.at[e.me], k, (cx, cy, e.c)).start()
        for k, (cx, cy) in enumerate(e.chips):
            e.copy(s_ref, o_ref.at[2 * cx + cy], k, (cx, cy, e.c)).wait_recv()
        for k, (cx, cy) in enumerate(e.chips):
            e.copy(s_ref, o_ref.at[e.me], k, (cx, cy, e.c)).wait_send()

    vm = pl.BlockSpec(memory_space=pltpu.VMEM)
    return pl.pallas_call(body, name=name, out_shape=jax.ShapeDtypeStruct((N_CHIPS,) + shard.shape, shard.dtype),
                          in_specs=[vm], out_specs=vm,
                          scratch_shapes=[pltpu.SemaphoreType.DMA((3,)), pltpu.SemaphoreType.DMA((3,))])(shard)
```

```python
import functools
import math

import jax
import jax.numpy as jnp
from jax import lax
from jax.experimental import pallas as pl
from jax.experimental.pallas import tpu as pltpu

F32 = jnp.float32
BF16 = jnp.bfloat16
MESH = pl.DeviceIdType.MESH

WINDOW = 128
BLK = 128
ROPE_THETA = 500000.0
LRU_C = 8.0
EPS = 1e-6
NEG = -1e30
ADAM_LR = 0.001
ADAM_B1 = 0.9
ADAM_B2 = 0.999
ADAM_EPS = 1e-08
ADAM_WD = 0.01
ADAM_STEP = 10

VMEM_LIMIT_BYTES = 52 * 1024 * 1024
LANE = 128
SUBLANE = 8
N_CHIPS = 4
SMALL_PACK_COLS = 512


def _params(**kw):
    return pltpu.CompilerParams(vmem_limit_bytes=VMEM_LIMIT_BYTES, **kw)


def _pick(dim, cands):
    for c in cands:
        if dim % c == 0:
            return c
    return dim


def _sigmoid(x):
    return 0.5 * jnp.tanh(0.5 * x) + 0.5


ANY = pl.BlockSpec(memory_space=pl.ANY)


class _Env:
    def __init__(self, ins, outs, send, recv, sem0=0, place=None):
        self.ins, self.outs, self.send, self.recv, self.sem0 = ins, outs, send, recv, sem0
        self.x, self.y, self.c = place or (lax.axis_index("x"), lax.axis_index("y"), lax.axis_index("c"))
        self.me = 2 * self.x + self.y
        self.chips = [(1 - self.x, self.y), (self.x, 1 - self.y), (1 - self.x, 1 - self.y)]
        self.sibling = (self.x, self.y, 1 - self.c)

    def sub(self, i0, n_in, o0, n_out, sem0):
        return _Env(self.ins[i0:i0 + n_in], self.outs[o0:o0 + n_out], self.send, self.recv, self.sem0 + sem0,
                    (self.x, self.y, self.c))

    def copy(self, src, dst, sem, to):
        return pltpu.make_async_remote_copy(src_ref=src, dst_ref=dst, send_sem=self.send.at[self.sem0 + sem],
                                            recv_sem=self.recv.at[self.sem0 + sem], device_id=to, device_id_type=MESH)


class _Exchange:
    inputs, out_shapes, aliases, n_sems = (), (), {}, 0

    def start(self, e):
        raise NotImplementedError

    def finish(self, e):
        raise NotImplementedError


class _Jobs(_Exchange):
    def __init__(self, *jobs):
        self.jobs, self.inputs, self.out_shapes, self.aliases, self.n_sems, self.at = jobs, [], [], {}, 0, []
        for job in jobs:
            self.at.append((len(self.inputs), len(self.out_shapes), self.n_sems))
            self.aliases.update({len(self.inputs) + i: len(self.out_shapes) + o for i, o in job.aliases.items()})
            self.inputs += list(job.inputs)
            self.out_shapes += list(job.out_shapes)
            self.n_sems += job.n_sems

    def _each(self, e):
        for job, (i0, o0, s0) in zip(self.jobs, self.at):
            yield job, e.sub(i0, len(job.inputs), o0, len(job.out_shapes), s0)

    def split(self, outs):
        return [tuple(outs[o0:o0 + len(job.out_shapes)]) for job, (_, o0, _) in zip(self.jobs, self.at)]

    def start(self, e):
        for job, se in self._each(e):
            job.start(se)

    def finish(self, e):
        for job, se in self._each(e):
            job.finish(se)


def _call(body, name, out_shape, grid, in_specs, out_specs, args, scratch_shapes=(), job=None, aliases=None):
    aliases = dict(aliases or {})
    if job is None:
        return pl.pallas_call(body, name=name, out_shape=out_shape, grid=grid, in_specs=list(in_specs),
                              out_specs=out_specs, scratch_shapes=list(scratch_shapes), input_output_aliases=aliases,
                              compiler_params=_params())(*args), ()
    single = not isinstance(out_shape, (tuple, list))
    shapes = [out_shape] if single else list(out_shape)
    ospecs = [out_specs] if single else list(out_specs)
    n_in, n_out, n_scr = len(args), len(shapes), len(scratch_shapes)
    j_in, j_out = len(job.inputs), len(job.out_shapes)

    def hosted(*refs):
        ins, jins = refs[:n_in], refs[n_in:n_in + j_in]
        outs = refs[n_in + j_in:n_in + j_in + n_out]
        jouts = refs[n_in + j_in + n_out:n_in + j_in + n_out + j_out]
        rest = refs[n_in + j_in + n_out + j_out:]
        e = _Env(jins, jouts, rest[n_scr], rest[n_scr + 1])
        first = functools.reduce(jnp.logical_and, [pl.program_id(d) == 0 for d in range(len(grid))])
        last = functools.reduce(jnp.logical_and, [pl.program_id(d) == g - 1 for d, g in enumerate(grid)])

        @pl.when(first)
        def _():
            job.start(e)

        body(*ins, *outs, *rest[:n_scr])

        @pl.when(last)
        def _():
            job.finish(e)

    res = pl.pallas_call(
        hosted, name=name, out_shape=tuple(shapes + list(job.out_shapes)), grid=grid,
        in_specs=list(in_specs) + [ANY] * j_in, out_specs=tuple(ospecs + [ANY] * j_out),
        scratch_shapes=list(scratch_shapes) + [pltpu.SemaphoreType.DMA((job.n_sems,)),
                                               pltpu.SemaphoreType.DMA((job.n_sems,))],
        input_output_aliases={**aliases, **{n_in + i: n_out + o for i, o in job.aliases.items()}},
        compiler_params=_params())(*args, *job.inputs)
    return (res[0] if single else tuple(res[:n_out])), tuple(res[n_out:])


def _run_exchange(job, name):
    n_in, n_out = len(job.inputs), len(job.out_shapes)

    def body(*refs):
        e = _Env(refs[:n_in], refs[n_in:n_in + n_out], refs[n_in + n_out], refs[n_in + n_out + 1])
        job.start(e)
        job.finish(e)

    return pl.pallas_call(
        body, name=name, out_shape=tuple(job.out_shapes), in_specs=[ANY] * n_in, out_specs=tuple([ANY] * n_out),
        input_output_aliases=dict(job.aliases),
        scratch_shapes=[pltpu.SemaphoreType.DMA((job.n_sems,)), pltpu.SemaphoreType.DMA((job.n_sems,))],
    )(*job.inputs)


_M_TILES = (1024, 1408, 1280, 512, 256, 128)
_N_TILES = (1408, 1280, 1024, 640, 512, 256, 128)
MXU_FULL_ROWS = 1024
MATMUL_VMEM_BUDGET = 42 * 1024 * 1024
MXU_FLOPS_PER_HBM_BYTE = 500


def _matmul_tiles(m, n, k, sa, sb, so, has_add):
    best = None
    for tm in [c for c in _M_TILES if m % c == 0] or [m]:
        for tn in [c for c in _N_TILES if n % c == 0] or [n]:
            for nk in range(1, 17):
                tk = k // nk
                if k % nk or tk % LANE:
                    continue
                need = 2 * (tm * tk * sa + tk * tn * sb) + 2 * tm * tn * (so + (4 if has_add else 0))
                need += tm * tn * 4 if nk > 1 else 0
                fetched = tk * tn * sb + tm * tk * sa // (1 if nk > 1 else n // tn)
                if need > MATMUL_VMEM_BUDGET:
                    continue
                mxu_bound = fetched * MXU_FLOPS_PER_HBM_BYTE <= 2 * tm * tn * tk
                key = (mxu_bound, min(tm, MXU_FULL_ROWS), -nk, tn, tm)
                if best is None or key > best[0]:
                    best = (key, (tm, tn, tk))
    assert best is not None, (m, n, k)
    return best[1]


def _matmul(a, b, mode, name, add=None, out_dtype=F32, job=None, m_window=None, into=None, fused=None):
    if mode == "nn":
        (m, k), (k2, n) = a.shape, b.shape
    elif mode == "nt":
        (m, k), (n, k2) = a.shape, b.shape
    else:
        (k, m), (k2, n) = a.shape, b.shape
    assert k == k2, (a.shape, b.shape, mode)
    m0, m = m_window or (0, m)
    tiles, fuse_fn, out_dtypes = fused or ((), None, (out_dtype,))
    tm, tn, tk = _matmul_tiles(math.gcd(m, m0) if m0 else m, n, k, a.dtype.itemsize, b.dtype.itemsize,
                               sum(jnp.dtype(dt).itemsize for dt in out_dtypes) + sum(x.dtype.itemsize for x in tiles),
                               add is not None)
    nk, mb0 = k // tk, m0 // tm
    if mode == "nn":
        a_spec = pl.BlockSpec((tm, tk), lambda i, j, kk: (mb0 + i, kk))
        b_spec = pl.BlockSpec((tk, tn), lambda i, j, kk: (kk, j))
        dims = (((1,), (0,)), ((), ()))
    elif mode == "nt":
        a_spec = pl.BlockSpec((tm, tk), lambda i, j, kk: (mb0 + i, kk))
        b_spec = pl.BlockSpec((tn, tk), lambda i, j, kk: (j, kk))
        dims = (((1,), (1,)), ((), ()))
    else:
        a_spec = pl.BlockSpec((tk, tm), lambda i, j, kk: (kk, mb0 + i))
        b_spec = pl.BlockSpec((tk, tn), lambda i, j, kk: (kk, j))
        dims = (((0,), (0,)), ((), ()))
    out_rows, ob0 = (into[1], mb0) if into is not None else (m, 0)
    o_spec = pl.BlockSpec((tm, tn), lambda i, j, kk: (ob0 + i, j))
    has_add = add is not None
    begun = into is not None and into[0] is not None

    n_side, n_out = has_add + len(tiles), len(out_dtypes)

    def body(*refs):
        a_ref, b_ref = refs[:2]
        side = refs[2:2 + n_side]
        o_refs = refs[len(refs) - n_out - (nk > 1):len(refs) - (nk > 1)]
        part = lax.dot_general(a_ref[...].astype(BF16), b_ref[...].astype(BF16), dims, preferred_element_type=F32)

        def finish(r):
            if has_add:
                r = r + side[0][...]
            vals = fuse_fn(r, *[x[...] for x in side[has_add:]]) if fuse_fn else (r,)
            for o_ref, val, dt in zip(o_refs, vals, out_dtypes):
                o_ref[...] = val.astype(dt)

        if nk == 1:
            finish(part)
            return
        acc = refs[-1]
        kk = pl.program_id(2)

        @pl.when(kk == 0)
        def _():
            acc[...] = part

        @pl.when(kk > 0)
        def _():
            acc[...] += part

        @pl.when(kk == nk - 1)
        def _():
            finish(acc[...])

    side_spec = pl.BlockSpec((tm, tn), lambda i, j, kk: (mb0 + i, j))
    in_specs = [a_spec, b_spec] + [side_spec] * n_side
    args = (a, b) + ((add,) if has_add else ()) + tuple(tiles)
    aliases = None
    if begun:
        aliases = {len(args): 0}
        in_specs, args = in_specs + [ANY], args + (into[0],)
    shapes = tuple(jax.ShapeDtypeStruct((out_rows, n), dt) for dt in out_dtypes)
    res, extra = _call(body, name, shapes if fused else shapes[0], (m // tm, n // tn, nk), in_specs,
                       (o_spec,) * n_out if fused else o_spec, args, [pltpu.VMEM((tm, tn), F32)] if nk > 1 else [],
                       job, aliases)
    return res if job is None else (res, extra)


def _row_tile(rows, cols, budget_elems=512 * 1024):
    cands = [c for c in (1024, 704, 512, 352, 256, 128, 64, 32, 16) if c * cols <= budget_elems]
    return _pick(rows, cands or (16,))


_EW_COLS = (1280, 1408, 1024, 640, 512, 256, 128)


def _tile2d(rows, cols, max_elems):
    tc = _pick(cols, _EW_COLS)
    return _row_tile(rows, tc, max_elems), tc


def _rms_fwd(x, g, name):
    t, d = x.shape
    tr = _row_tile(t, d)

    def body(x_ref, g_ref, o_ref):
        xv = x_ref[...]
        rstd = lax.rsqrt(jnp.mean(xv * xv, axis=-1, keepdims=True) + EPS)
        o_ref[...] = (xv * rstd * g_ref[...]).astype(BF16)

    spec = pl.BlockSpec((tr, d), lambda i: (i, 0))
    return pl.pallas_call(body, name=name, out_shape=jax.ShapeDtypeStruct((t, d), BF16), grid=(t // tr,),
                          in_specs=[spec, pl.BlockSpec((1, d), lambda i: (0, 0))], out_specs=spec,
                          compiler_params=_params())(x, g)


def _rms_bwd(dxn, x, g, resid, name, job=None, mxu_copy=False):
    t, d = x.shape
    tr = _row_tile(t, d, 256 * 1024)

    def body(dxn_ref, x_ref, g_ref, r_ref, dx_ref, dg_ref, *dx16_ref):
        @pl.when(pl.program_id(0) == 0)
        def _():
            dg_ref[...] = jnp.zeros_like(dg_ref)

        xv = x_ref[...]
        rstd = lax.rsqrt(jnp.mean(xv * xv, axis=-1, keepdims=True) + EPS)
        xhat = xv * rstd
        dy = dxn_ref[...]
        dg_ref[...] += jnp.sum(dy * xhat, axis=0, keepdims=True)
        dxhat = dy * g_ref[...]
        dx = r_ref[...] + rstd * (dxhat - xhat * jnp.mean(dxhat * xhat, axis=-1, keepdims=True))
        dx_ref[...] = dx
        if mxu_copy:
            dx16_ref[0][...] = dx.astype(BF16)

    spec = pl.BlockSpec((tr, d), lambda i: (i, 0))
    vec = pl.BlockSpec((1, d), lambda i: (0, 0))
    shapes = (jax.ShapeDtypeStruct((t, d), F32), jax.ShapeDtypeStruct((1, d), F32))
    shapes += (jax.ShapeDtypeStruct((t, d), BF16),) if mxu_copy else ()
    res, extra = _call(body, name, shapes, (t // tr,), [spec, spec, vec, spec],
                       (spec, vec) + ((spec,) if mxu_copy else ()), (dxn, x, g, resid), (), job)
    return res if job is None else (res, extra)


def _swiglu_after_up(up, gate):
    return up, gate * _sigmoid(gate) * up


def _swiglu_bwd_after_dact(dact, gate, up):
    sg = _sigmoid(gate)
    return dact * up * (sg * (1.0 + gate * (1.0 - sg))), dact * (gate * sg)


def _merge_fwd(z, b_gates, pa, plru, ga_off, name):
    t, d = pa.shape
    cw = _pick(math.gcd(ga_off, d), (512, 256, 128))
    tr = _row_tile(t, cw, 256 * 1024)
    oa, ol, nd = ga_off // cw, (ga_off + d) // cw, d // cw

    def body(ga_ref, gl_ref, ba_ref, bl_ref, pa_ref, pl_ref, o_ref):
        sa = _sigmoid(ga_ref[...] + ba_ref[...])
        sl = _sigmoid(gl_ref[...] + bl_ref[...])
        o_ref[...] = (sa * pa_ref[...] + sl * pl_ref[...]).astype(BF16)

    blk = pl.BlockSpec((tr, cw), lambda i, j: (i, j))
    return pl.pallas_call(
        body, name=name, out_shape=jax.ShapeDtypeStruct((t, d), BF16), grid=(t // tr, nd),
        in_specs=[pl.BlockSpec((tr, cw), lambda i, j: (i, oa + j)), pl.BlockSpec((tr, cw), lambda i, j: (i, ol + j)),
                  pl.BlockSpec((1, cw), lambda i, j: (0, j)), pl.BlockSpec((1, cw), lambda i, j: (0, nd + j)),
                  blk, blk],
        out_specs=blk, compiler_params=_params(),
    )(z, z, b_gates, b_gates, pa, plru)


def _merge_bwd(dmerged, z, b_gates, pa, plru, ga_off, name, job=None):
    t, d = pa.shape
    cw = _pick(math.gcd(ga_off, d), (512, 256, 128))
    tr = _row_tile(t, cw, 256 * 1024)
    oa, ol, nd = ga_off // cw, (ga_off + d) // cw, d // cw

    def body(dm_ref, ga_ref, gl_ref, ba_ref, bl_ref, pa_ref, pl_ref, dpa_ref, dpl_ref, dga_ref, dgl_ref, sa_ref, sl_ref):
        @pl.when(pl.program_id(1) == 0)
        def _():
            sa_ref[...] = jnp.zeros_like(sa_ref)
            sl_ref[...] = jnp.zeros_like(sl_ref)

        dm = dm_ref[...]
        sa = _sigmoid(ga_ref[...] + ba_ref[...])
        sl = _sigmoid(gl_ref[...] + bl_ref[...])
        dpa_ref[...] = (dm * sa).astype(BF16)
        dpl_ref[...] = (dm * sl).astype(BF16)
        dga = dm * pa_ref[...] * (sa * (1.0 - sa))
        dgl = dm * pl_ref[...] * (sl * (1.0 - sl))
        dga_ref[...] = dga.astype(BF16)
        dgl_ref[...] = dgl.astype(BF16)
        sa_ref[...] += jnp.sum(dga, axis=0, keepdims=True)
        sl_ref[...] += jnp.sum(dgl, axis=0, keepdims=True)

    blk = pl.BlockSpec((tr, cw), lambda j, i: (i, j))
    vec = pl.BlockSpec((1, cw), lambda j, i: (0, j))
    big16, v32 = jax.ShapeDtypeStruct((t, d), BF16), jax.ShapeDtypeStruct((1, d), F32)
    res, extra = _call(
        body, name, (big16, big16, big16, big16, v32, v32), (nd, t // tr),
        [blk, pl.BlockSpec((tr, cw), lambda j, i: (i, oa + j)), pl.BlockSpec((tr, cw), lambda j, i: (i, ol + j)),
         vec, pl.BlockSpec((1, cw), lambda j, i: (0, nd + j)), blk, blk],
        (blk, blk, blk, blk, vec, vec), (dmerged, z, z, b_gates, b_gates, pa, plru), (), job)
    return res if job is None else (res, extra)


def _loss_head(y, target, name):
    t, d = y.shape
    tr = _row_tile(t, d, 256 * 1024)
    nt = t // tr

    def body(y_ref, t_ref, dy_ref, dy16_ref, loss_ref, acc):
        i = pl.program_id(0)

        @pl.when(i == 0)
        def _():
            acc[...] = jnp.zeros_like(acc)

        e = y_ref[...] - t_ref[...]
        dy = e * (1.0 / d)
        dy_ref[...] = dy
        dy16_ref[...] = dy.astype(BF16)
        acc[...] += jnp.sum(e * e, axis=0, keepdims=True)

        @pl.when(i == nt - 1)
        def _():
            loss_ref[...] = (0.5 / d) * jnp.sum(acc[...], axis=-1, keepdims=True)

    spec = pl.BlockSpec((tr, d), lambda i: (i, 0))
    return pl.pallas_call(
        body, name=name, out_shape=(jax.ShapeDtypeStruct((t, d), F32), jax.ShapeDtypeStruct((t, d), BF16),
                                    jax.ShapeDtypeStruct((1, 1), F32)),
        grid=(nt,), in_specs=[spec, spec], out_specs=(spec, spec, pl.BlockSpec((1, 1), lambda i: (0, 0))),
        scratch_shapes=[pltpu.VMEM((1, d), F32)], compiler_params=_params(),
    )(y, target)


def _adamw(w, g, m, v, name):
    r, c = w.shape
    tr, tc = _tile2d(r, c, 512 * 1024)
    c1 = 1.0 - ADAM_B1 ** ADAM_STEP
    c2 = 1.0 - ADAM_B2 ** ADAM_STEP

    def body(w_ref, g_ref, m_ref, v_ref, d_ref, nm_ref, nv_ref):
        gv = g_ref[...]
        mn = ADAM_B1 * m_ref[...] + (1.0 - ADAM_B1) * gv
        vn = ADAM_B2 * v_ref[...] + (1.0 - ADAM_B2) * (gv * gv)
        d_ref[...] = -ADAM_LR * ((mn / c1) / (jnp.sqrt(vn / c2) + ADAM_EPS) + ADAM_WD * w_ref[...])
        nm_ref[...] = mn
        nv_ref[...] = vn

    spec = pl.BlockSpec((tr, tc), lambda i, j: (i, j))
    shp = jax.ShapeDtypeStruct((r, c), F32)
    return pl.pallas_call(body, name=name, out_shape=(shp, shp, shp), grid=(r // tr, c // tc), in_specs=[spec] * 4,
                          out_specs=(spec, spec, spec), compiler_params=_params())(w, g, m, v)


def _swap_halves(v, half):
    n = v.shape[-1]
    lane = lax.broadcasted_iota(jnp.int32, v.shape, 1)
    return jnp.where(lane < half, pltpu.roll(v, n - half, 1),
                     jnp.where(lane < 2 * half, pltpu.roll(v, half, 1), 0.0))


def _rope(y, c, s, half):
    return y * c + _swap_halves(y, half) * s


def _rope_bwd(dout, c, s, half):
    return dout * c + _swap_halves(dout * s, half)


def _stack_heads(ref, grp, hd):
    return jnp.concatenate([ref[:, g * hd:(g + 1) * hd] for g in range(grp)], axis=0)


def _softmax_with_sinks(s, sink_ref, first, grp, i):
    rows = s.shape[0]
    qi = lax.broadcasted_iota(jnp.int32, s.shape, 0) & (BLK - 1)
    kj = lax.broadcasted_iota(jnp.int32, s.shape, 1)
    rel = qi + BLK - kj
    s = jnp.where((rel >= 0) & (rel < WINDOW) & ((kj >= BLK) | (i > 0)), s, NEG)
    head = lax.broadcasted_iota(jnp.int32, (rows, 1), 0) // BLK
    sk = jnp.zeros((rows, 1), F32)
    for g in range(grp):
        sk = jnp.where(head == g, sink_ref[first + g], sk)
    mx = jnp.maximum(jnp.max(s, axis=-1, keepdims=True), sk)
    p = jnp.exp(s - mx)
    esk = jnp.exp(sk - mx)
    inv_den = 1.0 / (jnp.sum(p, axis=-1, keepdims=True) + esk)
    return p * inv_den, esk * inv_den, head


def _norm_fwd(xraw, g):
    rstd = lax.rsqrt(jnp.mean(xraw * xraw, axis=-1, keepdims=True) + EPS)
    xhat = xraw * rstd
    return xhat, rstd, xhat * g


def _norm_bwd(dy, xhat, rstd, g):
    dxhat = dy * g
    dx = rstd * (dxhat - xhat * jnp.mean(dxhat * xhat, axis=-1, keepdims=True))
    return dx, jnp.sum(dy * xhat, axis=0, keepdims=True)


def _attn_specs(nb, grp, hd, kv, clamp):
    qo, ko, vo = 0, (kv * grp), (kv * grp + kv)
    cur = (lambda i: jnp.minimum(i, nb - 1)) if clamp else (lambda i: i)
    prev = lambda i: jnp.maximum(cur(i) - 1, 0)
    zq = pl.BlockSpec((BLK, grp * hd), lambda h, i: (cur(i), h))
    kc = pl.BlockSpec((BLK, hd), lambda h, i: (cur(i), ko + h))
    kp = pl.BlockSpec((BLK, hd), lambda h, i: (prev(i), ko + h))
    vc = pl.BlockSpec((BLK, hd), lambda h, i: (cur(i), vo + h))
    vp = pl.BlockSpec((BLK, hd), lambda h, i: (prev(i), vo + h))
    tc = pl.BlockSpec((BLK, hd), lambda h, i: (cur(i), 0))
    tp = pl.BlockSpec((BLK, hd), lambda h, i: (prev(i), 0))
    gs = pl.BlockSpec((1, hd), lambda h, i: (0, 0))
    return zq, kc, kp, vc, vp, tc, tp, gs


def _attn_fwd(z, cos_t, sin_t, qg, kg, sinks, kv, grp, hd, name, job=None):
    t = z.shape[0]
    nb = t // BLK
    half = hd // 8
    scale = 1.0 / math.sqrt(hd)
    zq, kc, kp, vc, vp, tc, tp, gs = _attn_specs(nb, grp, hd, kv, False)

    def body(sink_ref, zq_ref, kc_ref, kp_ref, vc_ref, vp_ref, cc_ref, sc_ref, cp_ref, sp_ref, qg_ref, kg_ref, o_ref):
        h, i = pl.program_id(0), pl.program_id(1)

        def normrope(xraw, g, c, s):
            return _rope(_norm_fwd(xraw, g)[2], c, s, half)

        cc, sc = cc_ref[...], sc_ref[...]
        kcur = normrope(kc_ref[...], kg_ref[...], cc, sc)
        kprev = normrope(kp_ref[...], kg_ref[...], cp_ref[...], sp_ref[...])
        kk = jnp.concatenate([kprev, kcur], axis=0).astype(BF16)
        vv = jnp.concatenate([vp_ref[...], vc_ref[...]], axis=0).astype(BF16)
        for g in range(grp):
            q = normrope(zq_ref[:, g * hd:(g + 1) * hd], qg_ref[...], cc, sc).astype(BF16)
            s = lax.dot_general(q, kk, (((1,), (1,)), ((), ())), preferred_element_type=F32) * scale
            p, _, _ = _softmax_with_sinks(s, sink_ref, h * grp + g, 1, i)
            o_ref[:, g * hd:(g + 1) * hd] = jnp.dot(p.astype(BF16), vv, preferred_element_type=F32).astype(BF16)

    res, extra = _call(
        body, name, jax.ShapeDtypeStruct((t, kv * grp * hd), BF16), (kv, nb),
        [pl.BlockSpec(memory_space=pltpu.SMEM), zq, kc, kp, vc, vp, tc, tc, tp, tp, gs, gs],
        pl.BlockSpec((BLK, grp * hd), lambda h, i: (i, h)),
        (sinks, z, z, z, z, z, cos_t, sin_t, cos_t, sin_t, qg, kg), (), job)
    return res if job is None else (res, extra)


def _attn_bwd(dattn, z, cos_t, sin_t, qg, kg, sinks, kv, grp, hd, name, job=None):
    t = z.shape[0]
    nb = t // BLK
    half = hd // 8
    scale = 1.0 / math.sqrt(hd)
    zq, kc, kp, vc, vp, tc, tp, gs = _attn_specs(nb, grp, hd, kv, True)

    def body(sink_ref, zq_ref, kc_ref, kp_ref, vc_ref, vp_ref, cc_ref, sc_ref, cp_ref, sp_ref, qg_ref, kg_ref, do_ref,
             dq_ref, dk_ref, dv_ref, dqg_ref, dkg_ref, dsk_ref, dk_carry, dv_carry):
        h, i = pl.program_id(0), pl.program_id(1)
        lane1 = lax.broadcasted_iota(jnp.int32, (1, LANE), 1)

        @pl.when((h == 0) & (i == 0))
        def _():
            dqg_ref[...] = jnp.zeros_like(dqg_ref)
            dkg_ref[...] = jnp.zeros_like(dkg_ref)
            dsk_ref[...] = jnp.zeros_like(dsk_ref)

        @pl.when(i == 0)
        def _():
            dk_carry[...] = jnp.zeros_like(dk_carry)
            dv_carry[...] = jnp.zeros_like(dv_carry)

        @pl.when(i < nb)
        def _():
            cc, sc, cp, sp = cc_ref[...], sc_ref[...], cp_ref[...], sp_ref[...]
            qgv, kgv = qg_ref[...], kg_ref[...]
            xh_kc, rs_kc, y_kc = _norm_fwd(kc_ref[...], kgv)
            xh_kp, rs_kp, y_kp = _norm_fwd(kp_ref[...], kgv)
            kk = jnp.concatenate([_rope(y_kp, cp, sp, half), _rope(y_kc, cc, sc, half)], axis=0).astype(BF16)
            vv = jnp.concatenate([vp_ref[...], vc_ref[...]], axis=0).astype(BF16)
            cq, sq = jnp.concatenate([cc] * grp, axis=0), jnp.concatenate([sc] * grp, axis=0)
            xh_q, rs_q, y_q = _norm_fwd(_stack_heads(zq_ref, grp, hd), qgv)
            q = _rope(y_q, cq, sq, half).astype(BF16)
            s = lax.dot_general(q, kk, (((1,), (1,)), ((), ())), preferred_element_type=F32) * scale
            p, psink, head = _softmax_with_sinks(s, sink_ref, h * grp, grp, i)
            dog = _stack_heads(do_ref, grp, hd).astype(BF16)
            dp = lax.dot_general(dog, vv, (((1,), (1,)), ((), ())), preferred_element_type=F32)
            rsum = jnp.sum(p * dp, axis=-1, keepdims=True)
            ds = (p * (dp - rsum) * scale).astype(BF16)
            dsink = -psink * rsum
            dsk = jnp.zeros((1, LANE), F32)
            for g in range(grp):
                dsk = dsk + jnp.where(lane1 == h * grp + g,
                                      jnp.sum(jnp.where(head == g, dsink, 0.0), axis=0, keepdims=True), 0.0)
            dqn = jnp.dot(ds, kk, preferred_element_type=F32)
            dkk = lax.dot_general(ds, q, (((0,), (0,)), ((), ())), preferred_element_type=F32)
            dvv = lax.dot_general(p.astype(BF16), dog, (((0,), (0,)), ((), ())), preferred_element_type=F32)
            dxq, dqg = _norm_bwd(_rope_bwd(dqn, cq, sq, half), xh_q, rs_q, qgv)
            dxq = dxq.astype(BF16)
            for g in range(grp):
                dq_ref[:, g * hd:(g + 1) * hd] = dxq[g * BLK:(g + 1) * BLK]
            dkp_raw, dg_kp = _norm_bwd(_rope_bwd(dkk[:BLK], cp, sp, half), xh_kp, rs_kp, kgv)
            dkc_raw, dg_kc = _norm_bwd(_rope_bwd(dkk[BLK:], cc, sc, half), xh_kc, rs_kc, kgv)
            dk_ref[...] = (dk_carry[...] + dkp_raw).astype(BF16)
            dv_ref[...] = (dv_carry[...] + dvv[:BLK]).astype(BF16)
            dk_carry[...] = dkc_raw
            dv_carry[...] = dvv[BLK:]
            dqg_ref[...] += dqg
            dkg_ref[...] += dg_kp + dg_kc
            dsk_ref[...] += dsk

        @pl.when(i == nb)
        def _():
            dk_ref[...] = dk_carry[...].astype(BF16)
            dv_ref[...] = dv_carry[...].astype(BF16)

    kvw = kv * hd
    vec = pl.BlockSpec((1, hd), lambda h, i: (0, 0))
    shifted = pl.BlockSpec((BLK, hd), lambda h, i: (jnp.maximum(i - 1, 0), h))
    res, extra = _call(
        body, name,
        (jax.ShapeDtypeStruct((t, kv * grp * hd), BF16), jax.ShapeDtypeStruct((t, kvw), BF16),
         jax.ShapeDtypeStruct((t, kvw), BF16), jax.ShapeDtypeStruct((1, hd), F32),
         jax.ShapeDtypeStruct((1, hd), F32), jax.ShapeDtypeStruct((1, LANE), F32)),
        (kv, nb + 1),
        [pl.BlockSpec(memory_space=pltpu.SMEM), zq, kc, kp, vc, vp, tc, tc, tp, tp, gs, gs,
         pl.BlockSpec((BLK, grp * hd), lambda h, i: (jnp.minimum(i, nb - 1), h))],
        (pl.BlockSpec((BLK, grp * hd), lambda h, i: (jnp.minimum(i, nb - 1), h)), shifted, shifted, vec, vec,
         pl.BlockSpec((1, LANE), lambda h, i: (0, 0))),
        (sinks, z, z, z, z, z, cos_t, sin_t, cos_t, sin_t, qg, kg, dattn),
        [pltpu.VMEM((BLK, hd), F32), pltpu.VMEM((BLK, hd), F32)], job)
    return res if job is None else (res, extra)


def _window(rows, cb, c0, row_of, col_of):
    assert rows % SUBLANE == 0 and cb % LANE == 0 and c0 % LANE == 0, (rows, cb, c0)
    return pl.BlockSpec((pl.Element(rows), pl.Element(cb)),
                        lambda *g: (pl.multiple_of(row_of(*g) * rows, SUBLANE), pl.multiple_of(c0 + col_of(*g) * cb, LANE)))


def _conv_fwd(z, c0, c, w, b, name):
    t = z.shape[0]
    taps = w.shape[0]
    cb = _pick(c, (1408, 1024, 512, 256, 128))
    tr = _row_tile(t, cb, 256 * 1024)
    hb = tr // SUBLANE

    def body(u_ref, halo_ref, w_ref, b_ref, o_ref):
        i = pl.program_id(0)
        x = u_ref[...]
        acc = b_ref[...] + w_ref[taps - 1:taps, :] * x
        for k in range(taps - 1):
            acc = acc + w_ref[k:k + 1, :] * pltpu.roll(x, taps - 1 - k, 0)
        o_ref[...] = acc
        row = lax.broadcasted_iota(jnp.int32, (SUBLANE, cb), 0)
        hp = jnp.where(i > 0, halo_ref[...], 0.0)
        x8 = u_ref[0:SUBLANE, :]
        acc8 = b_ref[...] + w_ref[taps - 1:taps, :] * x8
        for k in range(taps - 1):
            s = taps - 1 - k
            acc8 = acc8 + w_ref[k:k + 1, :] * jnp.where(row < s, pltpu.roll(hp, s, 0), pltpu.roll(x8, s, 0))
        o_ref[0:SUBLANE, :] = acc8

    blk = pl.BlockSpec((tr, cb), lambda i, j: (i, j))
    return pl.pallas_call(
        body, name=name, out_shape=jax.ShapeDtypeStruct((t, c), F32), grid=(t // tr, c // cb),
        in_specs=[_window(tr, cb, c0, lambda i, j: i, lambda i, j: j),
                  _window(SUBLANE, cb, c0, lambda i, j: jnp.maximum(i * hb - 1, 0), lambda i, j: j),
                  pl.BlockSpec((taps, cb), lambda i, j: (0, j)), pl.BlockSpec((1, cb), lambda i, j: (0, j))],
        out_specs=blk, compiler_params=_params(),
    )(z, z, w, b)


def _conv_bwd(duc, z, c0, w, name):
    t, c = duc.shape
    taps = w.shape[0]
    cb = _pick(c, (1408, 1024, 512, 256, 128))
    tr = _row_tile(t, cb, 256 * 1024)
    hb, nt = tr // SUBLANE, t // tr

    def body(g_ref, gnext_ref, u_ref, uprev_ref, w_ref, du16_ref, dw_ref, db_ref, du_ref):
        i = pl.program_id(1)

        @pl.when(i == 0)
        def _():
            dw_ref[...] = jnp.zeros_like(dw_ref)
            db_ref[...] = jnp.zeros_like(db_ref)

        row = lax.broadcasted_iota(jnp.int32, (SUBLANE, cb), 0)
        g, x = g_ref[...], u_ref[...]
        du = w_ref[taps - 1:taps, :] * g
        for k in range(taps - 1):
            du = du + w_ref[k:k + 1, :] * pltpu.roll(g, tr - (taps - 1 - k), 0)
        du_ref[...] = du
        hn = jnp.where(i < nt - 1, gnext_ref[...], 0.0)
        g8 = g_ref[tr - SUBLANE:tr, :]
        du8 = w_ref[taps - 1:taps, :] * g8
        for k in range(taps - 1):
            s = taps - 1 - k
            du8 = du8 + w_ref[k:k + 1, :] * jnp.where(row >= SUBLANE - s, pltpu.roll(hn, SUBLANE - s, 0),
                                                     pltpu.roll(g8, SUBLANE - s, 0))
        du_ref[tr - SUBLANE:tr, :] = du8
        du16_ref[...] = du_ref[...].astype(BF16)

        hp = jnp.where(i > 0, uprev_ref[...], 0.0)
        xl8, gf8 = u_ref[tr - SUBLANE:tr, :], g_ref[0:SUBLANE, :]
        db_ref[...] += jnp.sum(g, axis=0, keepdims=True)
        dw_ref[taps - 1:taps, :] += jnp.sum(g * x, axis=0, keepdims=True)
        for k in range(taps - 1):
            s = taps - 1 - k
            fix = jnp.where(row < s, pltpu.roll(hp, s, 0) - pltpu.roll(xl8, s, 0), 0.0)
            dw_ref[k:k + 1, :] += (jnp.sum(g * pltpu.roll(x, s, 0), axis=0, keepdims=True)
                                   + jnp.sum(gf8 * fix, axis=0, keepdims=True))

    blk = pl.BlockSpec((tr, cb), lambda j, i: (i, j))
    nh = t // SUBLANE
    return pl.pallas_call(
        body, name=name,
        out_shape=(jax.ShapeDtypeStruct((t, c), BF16), jax.ShapeDtypeStruct((taps, c), F32),
                   jax.ShapeDtypeStruct((1, c), F32)),
        grid=(c // cb, nt),
        in_specs=[blk, pl.BlockSpec((SUBLANE, cb), lambda j, i: (jnp.minimum((i + 1) * hb, nh - 1), j)),
                  _window(tr, cb, c0, lambda j, i: i, lambda j, i: j),
                  _window(SUBLANE, cb, c0, lambda j, i: jnp.maximum(i * hb - 1, 0), lambda j, i: j),
                  pl.BlockSpec((taps, cb), lambda j, i: (0, j))],
        out_specs=(blk, pl.BlockSpec((taps, cb), lambda j, i: (0, j)), pl.BlockSpec((1, cb), lambda j, i: (0, j))),
        scratch_shapes=[pltpu.VMEM((tr, cb), F32)], compiler_params=_params(),
    )(duc, duc, z, z, w)


def _gates_fwd(uc, wr, wi, gw, name):
    t, c = uc.shape
    n, bw, _ = wr.shape
    per, ng = gw // bw, c // gw
    tr = _pick(t, (512, 256, 128))

    def body(u_ref, wr_ref, wi_ref, r_ref, i_ref):
        for b in range(per):
            cols = slice(b * bw, (b + 1) * bw)
            a = u_ref[:, cols].astype(BF16)
            r_ref[:, cols] = jnp.dot(a, wr_ref[b].astype(BF16), preferred_element_type=F32)
            i_ref[:, cols] = jnp.dot(a, wi_ref[b].astype(BF16), preferred_element_type=F32)

    blk = pl.BlockSpec((tr, gw), lambda h, i: (i, h))
    wsp = pl.BlockSpec((per, bw, bw), lambda h, i: (h, 0, 0))
    shp = jax.ShapeDtypeStruct((t, c), F32)
    return pl.pallas_call(body, name=name, out_shape=(shp, shp), grid=(ng, t // tr), in_specs=[blk, wsp, wsp],
                          out_specs=(blk, blk), compiler_params=_params())(uc, wr, wi)


def _gates_bwd_x(duc, drp, dip, wr, wi, gw, name):
    t, c = duc.shape
    n, bw, _ = wr.shape
    per, ng = gw // bw, c // gw
    tr = _pick(t, (512, 256, 128))
    dims = (((1,), (1,)), ((), ()))

    def body(d_ref, r_ref, i_ref, wr_ref, wi_ref, o_ref):
        for b in range(per):
            cols = slice(b * bw, (b + 1) * bw)
            o_ref[:, cols] = (
                d_ref[:, cols]
                + lax.dot_general(r_ref[:, cols].astype(BF16), wr_ref[b].astype(BF16), dims, preferred_element_type=F32)
                + lax.dot_general(i_ref[:, cols].astype(BF16), wi_ref[b].astype(BF16), dims, preferred_element_type=F32))

    blk = pl.BlockSpec((tr, gw), lambda h, i: (i, h))
    wsp = pl.BlockSpec((per, bw, bw), lambda h, i: (h, 0, 0))
    return pl.pallas_call(body, name=name, out_shape=jax.ShapeDtypeStruct((t, c), F32), grid=(ng, t // tr),
                          in_specs=[blk, blk, blk, wsp, wsp], out_specs=blk, compiler_params=_params())(duc, drp, dip, wr, wi)


def _gates_bwd_w(uc, dpre, n, bw, gw, name):
    t, c = uc.shape
    per, ng = gw // bw, c // gw
    tk = _pick(t, (512, 256, 128))
    dims = (((0,), (0,)), ((), ()))

    def body(u_ref, d_ref, o_ref):
        @pl.when(pl.program_id(1) == 0)
        def _():
            o_ref[...] = jnp.zeros_like(o_ref)

        for b in range(per):
            cols = slice(b * bw, (b + 1) * bw)
            o_ref[b] += lax.dot_general(u_ref[:, cols].astype(BF16), d_ref[:, cols].astype(BF16), dims,
                                        preferred_element_type=F32)

    blk = pl.BlockSpec((tk, gw), lambda h, i: (i, h))
    return pl.pallas_call(body, name=name, out_shape=jax.ShapeDtypeStruct((n, bw, bw), F32), grid=(ng, t // tk),
                          in_specs=[blk, blk], out_specs=pl.BlockSpec((per, bw, bw), lambda h, i: (h, 0, 0)),
                          compiler_params=_params())(uc, dpre)


def _softplus(x):
    return jnp.maximum(x, 0.0) + jnp.log(1.0 + jnp.exp(-jnp.abs(x)))


_GELU_C = math.sqrt(2.0 / math.pi)


def _gelu_parts(x):
    inner = _GELU_C * (x + 0.044715 * (x * x * x))
    th = jnp.tanh(inner)
    gelu = 0.5 * x * (1.0 + th)
    dgelu = 0.5 * (1.0 + th) + 0.5 * x * (1.0 - th * th) * (_GELU_C * (1.0 + 3.0 * 0.044715 * (x * x)))
    return gelu, dgelu


def _lru_gate_values(rpre, ipre, br, bi, sp):
    r = _sigmoid(rpre + br)
    ig = _sigmoid(ipre + bi)
    log_a = -LRU_C * r * sp
    a = jnp.exp(log_a)
    e2 = jnp.tanh(-log_a) * (1.0 + a * a)
    inv = lax.rsqrt(jnp.maximum(e2, 1e-30))
    return r, ig, a, e2 * inv, inv


def _lru_fwd(uc, rpre, ipre, z, gr0, br, bi, lam, name, job=None):
    t, c = uc.shape
    cb = _pick(c, (1408, 1024, 512, 256, 128))
    tb = _pick(t, (512, 256, 128))
    ntile = tb // SUBLANE

    def body(uc_ref, r_ref, i_ref, gr_ref, br_ref, bi_ref, lam_ref, h_ref, rec16_ref, carry, rec_ref):
        @pl.when(pl.program_id(1) == 0)
        def _():
            carry[...] = jnp.zeros_like(carry)

        sp = _softplus(-lam_ref[...])
        br, bi = br_ref[...], bi_ref[...]
        row = lax.broadcasted_iota(jnp.int32, (SUBLANE, cb), 0)

        def tile(k, c_in):
            sl = pl.ds(pl.multiple_of(k * SUBLANE, SUBLANE), SUBLANE)
            ucv = uc_ref[sl, :]
            _, ig, a, mult, _ = _lru_gate_values(r_ref[sl, :], i_ref[sl, :], br, bi, sp)
            b = mult * (ig * ucv)
            for d in (1, 2, 4):
                a_s = jnp.where(row >= d, pltpu.roll(a, d, 0), 1.0)
                b_s = jnp.where(row >= d, pltpu.roll(b, d, 0), 0.0)
                b = a * b_s + b
                a = a * a_s
            hv = b + a * c_in
            h_ref[sl, :] = hv
            rec_ref[sl, :] = hv * _gelu_parts(gr_ref[sl, :])[0]
            return hv[SUBLANE - 1:SUBLANE, :]

        c_out = lax.fori_loop(0, ntile, tile, carry[0:1, :])
        carry[...] = jnp.broadcast_to(c_out, (SUBLANE, cb))
        rec16_ref[...] = rec_ref[...].astype(BF16)

    blk = pl.BlockSpec((tb, cb), lambda j, i: (i, j))
    vec = pl.BlockSpec((1, cb), lambda j, i: (0, j))
    res, extra = _call(body, name, (jax.ShapeDtypeStruct((t, c), F32), jax.ShapeDtypeStruct((t, c), BF16)),
                       (c // cb, t // tb),
                       [blk, blk, blk, _window(tb, cb, gr0, lambda j, i: i, lambda j, i: j), vec, vec, vec], (blk, blk),
                       (uc, rpre, ipre, z, br, bi, lam), [pltpu.VMEM((SUBLANE, cb), F32), pltpu.VMEM((tb, cb), F32)], job)
    return res if job is None else (res, extra)


def _lru_bwd(drec, hst, uc, rpre, ipre, z, gr0, br, bi, lam, name, job=None):
    t, c = uc.shape
    cb = _pick(c, (1408, 1024, 512, 256, 128))
    tb = _pick(t, (256, 128))
    ntile, nt, hb = tb // SUBLANE, t // tb, tb // SUBLANE

    def body(drec_ref, h_ref, hprev_ref, uc_ref, r_ref, i_ref, gr_ref, br_ref, bi_ref, lam_ref,
             dgr16_ref, drp_ref, dip_ref, duc_ref, dlam_ref, dbr_ref, dbi_ref, carry, dgr_ref):
        step = pl.program_id(1)
        first_block = step == nt - 1

        @pl.when(step == 0)
        def _():
            carry[...] = jnp.zeros_like(carry)
            dlam_ref[...] = jnp.zeros_like(dlam_ref)
            dbr_ref[...] = jnp.zeros_like(dbr_ref)
            dbi_ref[...] = jnp.zeros_like(dbi_ref)

        lam = lam_ref[...]
        sp = _softplus(-lam)
        br, bi = br_ref[...], bi_ref[...]
        row = lax.broadcasted_iota(jnp.int32, (SUBLANE, cb), 0)
        halo = jnp.where(first_block, 0.0, hprev_ref[...])

        def tile(kk, state):
            c_p, acc_sp, acc_br, acc_bi = state
            k = ntile - 1 - kk
            sl = pl.ds(pl.multiple_of(k * SUBLANE, SUBLANE), SUBLANE)
            slp = pl.ds(pl.multiple_of(jnp.maximum(k - 1, 0) * SUBLANE, SUBLANE), SUBLANE)
            ucv = uc_ref[sl, :]
            r, ig, a, mult, inv_mult = _lru_gate_values(r_ref[sl, :], i_ref[sl, :], br, bi, sp)
            hv = h_ref[sl, :]
            below = jnp.where(k > 0, h_ref[slp, :], halo)
            hprev = jnp.where(row == 0, pltpu.roll(below, 1, 0), pltpu.roll(hv, 1, 0))
            gelu, dgelu = _gelu_parts(gr_ref[sl, :])
            drec = drec_ref[sl, :]
            dh = drec * gelu
            dgr_ref[sl, :] = drec * hv * dgelu
            pa, pb = a, a * dh
            for d in (1, 2, 4):
                a_s = jnp.where(row < SUBLANE - d, pltpu.roll(pa, SUBLANE - d, 0), 1.0)
                b_s = jnp.where(row < SUBLANE - d, pltpu.roll(pb, SUBLANE - d, 0), 0.0)
                pb = pa * b_s + pb
                pa = pa * a_s
            pv = pb + pa * c_p
            gt = dh + jnp.where(row == SUBLANE - 1, c_p, pltpu.roll(pv, SUBLANE - 1, 0))
            da = gt * hprev
            duc_ref[sl, :] = gt * mult * ig
            dmult = gt * ig * ucv
            dig = gt * mult * ucv
            dla = da * a - jnp.where(mult > 0.0, dmult * (a * a) * inv_mult, 0.0)
            drp = dla * (-LRU_C * sp) * (r * (1.0 - r))
            dip = dig * (ig * (1.0 - ig))
            drp_ref[sl, :] = drp
            dip_ref[sl, :] = dip
            return pv[0:1, :], acc_sp + dla * (-LRU_C * r), acc_br + drp, acc_bi + dip

        zero = jnp.zeros((SUBLANE, cb), F32)
        c_out, acc_sp, acc_br, acc_bi = lax.fori_loop(0, ntile, tile, (carry[0:1, :], zero, zero, zero))
        carry[...] = jnp.broadcast_to(c_out, (SUBLANE, cb))
        dlam_ref[...] += jnp.sum(acc_sp, axis=0, keepdims=True) * (-_sigmoid(-lam))
        dbr_ref[...] += jnp.sum(acc_br, axis=0, keepdims=True)
        dbi_ref[...] += jnp.sum(acc_bi, axis=0, keepdims=True)
        dgr16_ref[...] = dgr_ref[...].astype(BF16)

    blk = pl.BlockSpec((tb, cb), lambda j, i: (nt - 1 - i, j))
    vec = pl.BlockSpec((1, cb), lambda j, i: (0, j))
    halo_spec = pl.BlockSpec((SUBLANE, cb), lambda j, i: (jnp.maximum((nt - 1 - i) * hb - 1, 0), j))
    big, small = jax.ShapeDtypeStruct((t, c), F32), jax.ShapeDtypeStruct((1, c), F32)
    res, extra = _call(
        body, name, (jax.ShapeDtypeStruct((t, c), BF16), big, big, big, small, small, small), (c // cb, nt),
        [blk, blk, halo_spec, blk, blk, blk, _window(tb, cb, gr0, lambda j, i: nt - 1 - i, lambda j, i: j),
         vec, vec, vec], (blk, blk, blk, blk, vec, vec, vec),
        (drec, hst, hst, uc, rpre, ipre, z, br, bi, lam),
        [pltpu.VMEM((SUBLANE, cb), F32), pltpu.VMEM((tb, cb), F32)], job)
    return res if job is None else (res, extra)


def _shard_region(ref, kind, chip, half, rh, width):
    if kind == "col":
        return ref.at[pl.ds(half * rh, rh), pl.ds(chip * width, width)]
    return ref.at[pl.ds(chip * (2 * rh) + half * rh, rh), :]


class _AllGather(_Exchange):
    def __init__(self, fulls, kinds):
        self.inputs, self.kinds = list(fulls), kinds
        self.out_shapes = [jax.ShapeDtypeStruct(f.shape, f.dtype) for f in fulls]
        self.aliases = {a: a for a in range(len(fulls))}
        self.n_sems = 6 * len(fulls)
        self.geo = [(f.shape[0] // 2, f.shape[1] // N_CHIPS) if k == "col" else (f.shape[0] // (2 * N_CHIPS), f.shape[1])
                    for f, k in zip(fulls, kinds)]

    def _region(self, ref, a, chip, half):
        return _shard_region(ref, self.kinds[a], chip, half, *self.geo[a])

    def _ici(self, e, a, k, chip):
        cx, cy = e.chips[k]
        return e.copy(self._region(e.ins[a], a, chip, e.c), self._region(e.outs[a], a, chip, e.c), a * 6 + k,
                      (cx, cy, e.c))

    def _d2d(self, e, a, k, half):
        cx, cy = e.chips[k]
        region = self._region(e.outs[a], a, 2 * cx + cy, half)
        return e.copy(region, region, a * 6 + 3 + k, e.sibling)

    def start(self, e):
        for a in range(len(self.inputs)):
            for k in range(3):
                self._ici(e, a, k, e.me).start()

    def finish(self, e):
        n = len(self.inputs)
        for a in range(n):
            for k, (cx, cy) in enumerate(e.chips):
                self._ici(e, a, k, 2 * cx + cy).wait_recv()
                self._d2d(e, a, k, e.c).start()
        for a in range(n):
            for k in range(3):
                self._d2d(e, a, k, 1 - e.c).wait_recv()
        for a in range(n):
            for k in range(3):
                self._ici(e, a, k, e.me).wait_send()
                self._d2d(e, a, k, e.c).wait_send()


class _SiblingExchange(_Exchange):
    def __init__(self, grads):
        self.inputs = list(grads)
        self.out_shapes = [jax.ShapeDtypeStruct((g.shape[0],) + g.shape[2:], g.dtype) for g in grads]
        self.n_sems = len(grads)

    def _copy(self, e, a):
        return e.copy(e.ins[a].at[:, 1 - e.c], e.outs[a], a, e.sibling)

    def start(self, e):
        for a in range(len(self.inputs)):
            self._copy(e, a).start()

    def finish(self, e):
        for a in range(len(self.inputs)):
            self._copy(e, a).wait()


def _piece(ref, kind, chip, width):
    if kind == "col":
        return ref.at[0, :, pl.ds(chip * width, width)]
    return ref.at[chip]


class _ChipExchange(_Exchange):
    def __init__(self, sums, kinds):
        self.inputs, self.kinds = list(sums), kinds
        self.widths = [s.shape[2] // N_CHIPS if k == "col" else s.shape[2] for s, k in zip(sums, kinds)]
        self.out_shapes = [jax.ShapeDtypeStruct((3, s.shape[1], w), s.dtype) for s, w in zip(sums, self.widths)]
        self.n_sems = 3 * len(sums)

    def _copy(self, e, a, k, chip):
        cx, cy = e.chips[k]
        return e.copy(_piece(e.ins[a], self.kinds[a], chip, self.widths[a]), e.outs[a].at[k], a * 3 + k, (cx, cy, e.c))

    def start(self, e):
        for a in range(len(self.inputs)):
            for k, (cx, cy) in enumerate(e.chips):
                self._copy(e, a, k, 2 * cx + cy).start()

    def finish(self, e):
        for a in range(len(self.inputs)):
            for k, (cx, cy) in enumerate(e.chips):
                self._copy(e, a, k, 2 * cx + cy).wait()


class _FinishExchange(_Exchange):
    def __init__(self, finals, to_all):
        self.inputs, self.to_all = list(finals), list(to_all)
        self.out_shapes = [jax.ShapeDtypeStruct(f.shape, f.dtype) for f in finals]
        self.aliases = {a: a for a in range(len(finals))}
        self.first_sem, self.n_sems = [], 0
        for all8 in self.to_all:
            self.first_sem.append(self.n_sems)
            self.n_sems += 7 if all8 else 1
        self.rel = [(fx, fy, fc) for fx in (0, 1) for fy in (0, 1) for fc in (0, 1)][1:]

    def _copies(self, e, mine):
        for a, all8 in enumerate(self.to_all):
            src = e.ins[a] if mine else e.outs[a]
            if not all8:
                rh = self.inputs[a].shape[0] // 2
                rows = pl.ds((e.c if mine else 1 - e.c) * rh, rh)
                yield e.copy(src.at[rows, :], e.outs[a].at[rows, :], self.first_sem[a], e.sibling)
                continue
            rh = self.inputs[a].shape[0] // (2 * N_CHIPS)
            for r, (fx, fy, fc) in enumerate(self.rel):
                px, py, pc = (1 - e.x if fx else e.x), (1 - e.y if fy else e.y), (1 - e.c if fc else e.c)
                rows = pl.ds(((2 * e.me + e.c) if mine else (2 * (2 * px + py) + pc)) * rh, rh)
                yield e.copy(src.at[rows, :], e.outs[a].at[rows, :], self.first_sem[a] + r, (px, py, pc))

    def start(self, e):
        for cp in self._copies(e, True):
            cp.start()

    def finish(self, e):
        for cp in self._copies(e, False):
            cp.wait_recv()
        for cp in self._copies(e, True):
            cp.wait_send()


def _cast_into_full(w, kind, idx, name):
    r, c = w.shape
    tr = _row_tile(r, c)
    nrb = r // tr

    def body(idx_ref, w_ref, o_ref):
        o_ref[...] = w_ref[...].astype(BF16)

    if kind == "col":
        full, out_map = (r, N_CHIPS * c), (lambda i, idx_ref: (i, idx_ref[1]))
    else:
        full, out_map = (N_CHIPS * r, c), (lambda i, idx_ref: (idx_ref[1] * nrb + i, 0))
    return pl.pallas_call(
        body, name=name, out_shape=jax.ShapeDtypeStruct(full, BF16),
        grid_spec=pltpu.PrefetchScalarGridSpec(
            num_scalar_prefetch=1, grid=(nrb,), in_specs=[pl.BlockSpec((tr, c), lambda i, idx_ref: (i, 0))],
            out_specs=pl.BlockSpec((tr, c), out_map)),
        compiler_params=_params(),
    )(idx, w)


def _matmul_gathering(a, placed, order, name):
    t, k = a.shape
    n = placed.shape[1]
    w = n // N_CHIPS
    tm, tn = _pick(t, _M_TILES), _pick(w, _N_TILES)
    ni, nj = t // tm, w // tn
    per_shard, total = ni * nj, N_CHIPS * ni * nj
    gather = _AllGather([placed], ["col"])

    def body(ord_ref, a_ref, w_own_ref, o_ref, w_ref, wbuf, fetch_sem, send, recv):
        s, i, j = pl.program_id(0), pl.program_id(1), pl.program_id(2)
        step = (s * ni + i) * nj + j
        e = _Env((w_own_ref,), (w_ref,), send, recv)

        def fetch(src, st):
            col = pl.multiple_of((ord_ref[st // per_shard] * nj + st % nj) * tn, LANE)
            return pltpu.make_async_copy(src.at[:, pl.ds(col, tn)], wbuf.at[st % 2], fetch_sem.at[st % 2])

        @pl.when(step == 0)
        def _():
            gather.start(e)
            fetch(w_own_ref, step).start()

        nxt = step + 1
        for kk, (cx, cy) in enumerate(e.chips):
            @pl.when(nxt == (kk + 1) * per_shard)
            def _():
                gather._ici(e, 0, kk, 2 * cx + cy).wait_recv()
                gather._d2d(e, 0, kk, e.c).start()
                gather._d2d(e, 0, kk, 1 - e.c).wait_recv()

        @pl.when(nxt < per_shard)
        def _():
            fetch(w_own_ref, nxt).start()

        @pl.when((nxt >= per_shard) & (nxt < total))
        def _():
            fetch(w_ref, nxt).start()

        fetch(w_ref, step).wait()
        o_ref[...] = jnp.dot(a_ref[...], wbuf[step % 2], preferred_element_type=F32)

        @pl.when(step == total - 1)
        def _():
            for kk in range(3):
                gather._ici(e, 0, kk, e.me).wait_send()
                gather._d2d(e, 0, kk, e.c).wait_send()

    z, full = pl.pallas_call(
        body, name=name, out_shape=(jax.ShapeDtypeStruct((t, n), F32), jax.ShapeDtypeStruct(placed.shape, placed.dtype)),
        grid_spec=pltpu.PrefetchScalarGridSpec(
            num_scalar_prefetch=1, grid=(N_CHIPS, ni, nj),
            in_specs=[pl.BlockSpec((tm, k), lambda s, i, j, ord_ref: (i, 0)), ANY],
            out_specs=(pl.BlockSpec((tm, tn), lambda s, i, j, ord_ref: (i, ord_ref[s] * nj + j)), ANY),
            scratch_shapes=[pltpu.VMEM((2, k, tn), placed.dtype), pltpu.SemaphoreType.DMA((2,)),
                            pltpu.SemaphoreType.DMA((gather.n_sems,)), pltpu.SemaphoreType.DMA((gather.n_sems,))]),
        input_output_aliases={2: 1}, compiler_params=_params(),
    )(order, a, placed)
    return z, full


def _add_own_half(g4, recv, idx, out_dtype, name):
    p, _, rh, n = g4.shape
    tr, tc = _tile2d(rh, n, 1024 * 1024)

    def body(idx_ref, g_ref, r_ref, o_ref):
        o_ref[...] = (g_ref[...] + r_ref[...]).astype(out_dtype)

    return pl.pallas_call(
        body, name=name, out_shape=jax.ShapeDtypeStruct((p, rh, n), out_dtype),
        grid_spec=pltpu.PrefetchScalarGridSpec(
            num_scalar_prefetch=1, grid=(p, rh // tr, n // tc),
            in_specs=[pl.BlockSpec((None, None, tr, tc), lambda q, i, j, idx_ref: (q, idx_ref[0], i, j)),
                      pl.BlockSpec((None, tr, tc), lambda q, i, j, idx_ref: (q, i, j))],
            out_specs=pl.BlockSpec((None, tr, tc), lambda q, i, j, idx_ref: (q, i, j))),
        compiler_params=_params(),
    )(idx, g4, recv)


def _sum_chips(own, kind, parts, idx, slots, to_all, name):
    _, rh, w = parts.shape
    tr, tc = _tile2d(rh, w, 512 * 1024)
    nrb, ncb = rh // tr, w // tc

    def body(idx_ref, own_ref, p0, p1, p2, o_ref):
        o_ref[...] = ((own_ref[...].astype(F32) + p0[...].astype(F32)) + p1[...].astype(F32)) + p2[...].astype(F32)

    if kind == "col":
        own_spec = pl.BlockSpec((None, tr, tc), lambda i, j, idx_ref: (0, i, idx_ref[1] * ncb + j))
    else:
        own_spec = pl.BlockSpec((None, tr, tc), lambda i, j, idx_ref: (idx_ref[1], i, j))
    if to_all:
        out_map = lambda i, j, idx_ref: ((2 * idx_ref[1] + idx_ref[0]) * nrb + i, j)
    else:
        out_map = lambda i, j, idx_ref: (idx_ref[0] * nrb + i, j)

    def part(k):
        return pl.BlockSpec((None, tr, tc), lambda i, j, idx_ref: (k, i, j))

    return pl.pallas_call(
        body, name=name, out_shape=jax.ShapeDtypeStruct((slots * rh, w), F32),
        grid_spec=pltpu.PrefetchScalarGridSpec(
            num_scalar_prefetch=1, grid=(nrb, ncb), in_specs=[own_spec, part(0), part(1), part(2)],
            out_specs=pl.BlockSpec((tr, tc), out_map)),
        compiler_params=_params(),
    )(idx, own, parts, parts, parts)


class _Reduce:
    def __init__(self, name, g, kind, idx, wire, to_all):
        r, c = g.shape
        self.name, self.kind, self.idx, self.wire, self.to_all = name, kind, idx, wire, to_all
        self.view = g.reshape(1, 2, r // 2, c) if kind == "col" else g.reshape(N_CHIPS, 2, r // (2 * N_CHIPS), c)

    def sibling(self):
        return _SiblingExchange([self.view])

    def got_sibling(self, outs):
        self.sum = _add_own_half(self.view, outs[0], self.idx, self.wire, "grad_chip_sum_" + self.name)

    def chips(self):
        return _ChipExchange([self.sum], [self.kind])

    def got_chips(self, outs):
        self.total = _sum_chips(self.sum, self.kind, outs[0], self.idx, 2 * N_CHIPS if self.to_all else 2,
                                self.to_all, "grad_total_" + self.name)


def _pack(arrays, rows):
    flat = jnp.concatenate([a.reshape(-1) for a in arrays])
    return jnp.pad(flat, (0, rows * SMALL_PACK_COLS - flat.shape[0])).reshape(rows, SMALL_PACK_COLS)


def _unpack(packed, shapes):
    flat = packed.reshape(-1)
    out, o = [], 0
    for shp in shapes:
        size = math.prod(shp)
        out.append(flat[o:o + size].reshape(shp))
        o += size
    return out


def _pack_rows(shapes):
    total = sum(math.prod(s) for s in shapes)
    unit = SMALL_PACK_COLS * N_CHIPS * 2 * SUBLANE
    return -(-total // unit) * (N_CHIPS * 2 * SUBLANE)


BIG = ("w_in", "w_attn_proj", "w_lru_proj", "w_out", "w_ffn_gate", "w_ffn_up", "w_ffn_down")
BIG_KIND = {"w_in": "col", "w_attn_proj": "row", "w_lru_proj": "row", "w_out": "row", "w_ffn_gate": "col",
            "w_ffn_up": "col", "w_ffn_down": "row"}
SMALL = ("norm1_g", "b_gates", "q_norm_g", "k_norm_g", "sinks", "conv_w", "conv_b", "w_rgate", "b_rgate",
         "w_igate", "b_igate", "lru_lambda", "norm2_g")
PACKED = tuple(n for n in SMALL if n not in ("w_rgate", "w_igate"))
WEIGHTS = ("norm1_g", "w_in", "b_gates", "q_norm_g", "k_norm_g", "sinks", "conv_w", "conv_b", "w_rgate", "b_rgate",
           "w_igate", "b_igate", "lru_lambda", "w_attn_proj", "w_lru_proj", "w_out", "norm2_g", "w_ffn_gate",
           "w_ffn_up", "w_ffn_down")


def kernel(x, positions, norm1_g, w_in, b_gates, q_norm_g, k_norm_g, sinks, conv_w, conv_b, w_rgate, b_rgate, w_igate, b_igate, lru_lambda, w_attn_proj, w_lru_proj, w_out, norm2_g, w_ffn_gate, w_ffn_up, w_ffn_down, loss_target, m_norm1_g, m_w_in, m_b_gates, m_q_norm_g, m_k_norm_g, m_sinks, m_conv_w, m_conv_b, m_w_rgate, m_b_rgate, m_w_igate, m_b_igate, m_lru_lambda, m_w_attn_proj, m_w_lru_proj, m_w_out, m_norm2_g, m_w_ffn_gate, m_w_ffn_up, m_w_ffn_down, v_norm1_g, v_w_in, v_b_gates, v_q_norm_g, v_k_norm_g, v_sinks, v_conv_w, v_conv_b, v_w_rgate, v_b_rgate, v_w_igate, v_b_igate, v_lru_lambda, v_w_attn_proj, v_w_lru_proj, v_w_out, v_norm2_g, v_w_ffn_gate, v_w_ffn_up, v_w_ffn_down):
    args = dict(locals())
    w = {n: args[n] for n in WEIGHTS}
    mom = {n: args["m_" + n] for n in WEIGHTS}
    var = {n: args["v_" + n] for n in WEIGHTS}

    t, d = x.shape[1], x.shape[2]
    hd = q_norm_g.shape[-1]
    nq = sinks.shape[-1]
    q_w = nq * hd
    d_rnn = conv_b.shape[-1]
    taps = conv_w.shape[1]
    n_blocks, bw = w_rgate.shape[1], w_rgate.shape[2]
    in_w = w_in.shape[-1] * N_CHIPS
    kv_w = (in_w - q_w - 2 * d_rnn - 2 * d) // 2
    kv = kv_w // hd
    grp = nq // kv
    u_off = q_w + 2 * kv_w
    gr_off = u_off + d_rnn
    ga_off = gr_off + d_rnn
    gw = bw * LANE // math.gcd(bw, LANE)
    chip = 2 * lax.axis_index("x") + lax.axis_index("y")
    idx = jnp.stack([lax.axis_index("c"), chip]).astype(jnp.int32)

    x2, tgt = x[0], loss_target[0]

    placed = {n: _cast_into_full(w[n][0], BIG_KIND[n], idx, "cast_" + n) for n in BIG}

    def gather(*names):
        return _AllGather([placed[n] for n in names], [BIG_KIND[n] for n in names])

    mx, my = lax.axis_index("x"), lax.axis_index("y")
    order = jnp.stack([chip, 2 * (1 - mx) + my, 2 * mx + (1 - my), 2 * (1 - mx) + (1 - my)]).astype(jnp.int32)
    conv_w_full = _gather_small(conv_w[0], "allgather_conv_w")
    conv_w_full = jnp.transpose(conv_w_full, (1, 0, 2)).reshape(taps, d_rnn)

    inv_freq = ROPE_THETA ** (-jnp.arange(0, hd // 4, 2, dtype=F32) / (hd // 4))
    ang = positions[0].astype(F32)[:, None] * inv_freq
    cos, sin = jnp.cos(ang), jnp.sin(ang)
    rest = hd - 2 * cos.shape[1]
    cos_t = jnp.concatenate([cos, cos, jnp.ones((t, rest), F32)], axis=1)
    sin_t = jnp.concatenate([-sin, sin, jnp.zeros((t, rest), F32)], axis=1)
    sinks1 = sinks[0]

    xn = _rms_fwd(x2, norm1_g, "rms1_fwd")
    z, win_f = _matmul_gathering(xn, placed["w_in"], order, "in_proj")
    attn, (wap_f, wlp_f, wout_f) = _attn_fwd(z, cos_t, sin_t, q_norm_g, k_norm_g, sinks1, kv, grp, hd, "attn_fwd",
                                             job=gather("w_attn_proj", "w_lru_proj", "w_out"))
    uc = _conv_fwd(z, u_off, d_rnn, conv_w_full, conv_b, "conv_fwd")
    rpre, ipre = _gates_fwd(uc, w_rgate[0], w_igate[0], gw, "gates_fwd")
    (hst, rec), (wg_f,) = _lru_fwd(uc, rpre, ipre, z, gr_off, b_rgate, b_igate, lru_lambda, "lru_fwd",
                                   job=gather("w_ffn_gate"))
    pa = _matmul(attn, wap_f, "nn", "attn_proj")
    plru = _matmul(rec, wlp_f, "nn", "lru_proj")
    merged = _merge_fwd(z, b_gates, pa, plru, ga_off, "merge_fwd")
    h1 = _matmul(merged, wout_f, "nn", "out_proj", add=x2)
    hn = _rms_fwd(h1, norm2_g, "rms2_fwd")
    gate, (wu_f,) = _matmul(hn, wg_f, "nn", "ffn_gate", job=gather("w_ffn_up"))
    (up, act), (wd_f,) = _matmul(hn, wu_f, "nn", "ffn_up", job=gather("w_ffn_down"),
                                 fused=([gate], _swiglu_after_up, (F32, BF16)))
    yout = _matmul(act, wd_f, "nn", "ffn_down", add=h1)
    dy, dy16, loss_part = _loss_head(yout, tgt, "loss_head")
    loss = lax.psum(loss_part[0, 0], ("x", "y", "c"))

    def reduction(n, g):
        return _Reduce(n, g, BIG_KIND[n], idx, BF16, False)

    r_wd = reduction("w_ffn_down", _matmul(act, dy16, "tn", "d_w_ffn_down"))
    (dgate, dup), got = _matmul(dy16, wd_f, "nt", "d_act", job=r_wd.sibling(),
                                fused=([gate, up], _swiglu_bwd_after_dact, (BF16, BF16)))
    r_wd.got_sibling(got)
    r_wg = reduction("w_ffn_gate", _matmul(hn, dgate, "tn", "d_w_ffn_gate"))
    g_wu, got = _matmul(hn, dup, "tn", "d_w_ffn_up", job=r_wg.sibling())
    r_wg.got_sibling(got)
    r_wu = reduction("w_ffn_up", g_wu)
    dhn, got = _matmul(dgate, wg_f, "nt", "d_hn_gate", job=r_wu.sibling())
    r_wu.got_sibling(got)
    dhn = _matmul(dup, wu_f, "nt", "d_hn_up", add=dhn)
    dh1, g_norm2, dh1_16 = _rms_bwd(dhn, h1, norm2_g, dy, "rms2_bwd", mxu_copy=True)
    r_wout = reduction("w_out", _matmul(merged, dh1_16, "tn", "d_w_out"))
    dmerged, got = _matmul(dh1_16, wout_f, "nt", "d_merged", job=r_wout.sibling())
    r_wout.got_sibling(got)
    (dpa, dpl, dga, dgl, g_ba, g_bl), got = _merge_bwd(dmerged, z, b_gates, pa, plru, ga_off, "merge_bwd",
                                                       job=r_wout.chips())
    r_wout.got_chips(got)
    r_wap = reduction("w_attn_proj", _matmul(attn, dpa, "tn", "d_w_attn_proj"))
    dattn, got = _matmul(dpa, wap_f, "nt", "d_attn", job=r_wap.sibling())
    r_wap.got_sibling(got)
    g_wlp, got = _matmul(rec, dpl, "tn", "d_w_lru_proj", job=r_wap.chips())
    r_wap.got_chips(got)
    r_wlp = reduction("w_lru_proj", g_wlp)
    drec, got = _matmul(dpl, wlp_f, "nt", "d_rec", job=r_wlp.sibling())
    r_wlp.got_sibling(got)
    both = _Jobs(r_wlp.chips(), r_wg.chips())
    (dgr, drp, dip, duc_direct, g_lam, g_br, g_bi), got = _lru_bwd(
        drec, hst, uc, rpre, ipre, z, gr_off, b_rgate, b_igate, lru_lambda, "lru_bwd", job=both)
    got_wlp, got_wg = both.split(got)
    r_wlp.got_chips(got_wlp)
    r_wg.got_chips(got_wg)
    duc = _gates_bwd_x(duc_direct, drp, dip, w_rgate[0], w_igate[0], gw, "gates_bwd_x")
    g_wr = _gates_bwd_w(uc, drp, n_blocks, bw, gw, "gates_bwd_wr")
    g_wi = _gates_bwd_w(uc, dip, n_blocks, bw, gw, "gates_bwd_wi")
    du, g_convw, g_convb = _conv_bwd(duc, z, u_off, conv_w_full, "conv_bwd")
    (dq, dk, dv, g_qg, g_kg, g_sinks), got = _attn_bwd(dattn, z, cos_t, sin_t, q_norm_g, k_norm_g, sinks1, kv, grp, hd,
                                                        "attn_bwd", job=_Jobs(r_wu.chips(), r_wd.chips()))
    r_wu.got_chips(got[:1])
    r_wd.got_chips(got[1:])
    dz = jnp.concatenate([dq, dk, dv, du, dgr, dga, dgl], axis=1)
    r_wr = _Reduce("w_rgate", g_wr.reshape(n_blocks * bw, bw), "row", idx, F32, True)
    r_wi = _Reduce("w_igate", g_wi.reshape(n_blocks * bw, bw), "row", idx, F32, True)
    early = [r_wap, r_wlp, r_wout, r_wg, r_wu, r_wd]
    three = _Jobs(r_wr.sibling(), r_wi.sibling(), _FinishExchange([r.total for r in early], [False] * len(early)))
    g_top, got = _matmul(xn, dz, "tn", "d_w_in_top", m_window=(0, d // 2), job=three)
    got_wr, got_wi, finished = three.split(got)
    r_wr.got_sibling(got_wr)
    r_wi.got_sibling(got_wi)
    r_top = _Reduce("w_in_top", g_top, "col", idx, BF16, False)
    three = _Jobs(r_top.sibling(), r_wr.chips(), r_wi.chips())
    g_bot, got = _matmul(xn, dz, "tn", "d_w_in_bot", m_window=(d // 2, d // 2), job=three)
    got_top, got_wr, got_wi = three.split(got)
    r_top.got_sibling(got_top)
    r_wr.got_chips(got_wr)
    r_wi.got_chips(got_wi)
    r_bot = _Reduce("w_in_bot", g_bot, "col", idx, BF16, False)
    both = _Jobs(r_top.chips(), r_bot.sibling())
    dxn, got = _matmul(dz, win_f, "nt", "d_xn_a", m_window=(0, t // 2), into=(None, t), job=both)
    got_top, got_bot = both.split(got)
    r_top.got_chips(got_top)
    r_bot.got_sibling(got_bot)
    dxn, got = _matmul(dz, win_f, "nt", "d_xn_b", m_window=(t // 2, t // 2), into=(dxn, t), job=r_bot.chips())
    r_bot.got_chips(got)
    dx, g_norm1 = _rms_bwd(dxn, x2, norm1_g, dh1, "rms1_bwd")

    small_grads = {"norm1_g": g_norm1, "b_gates": jnp.concatenate([g_ba, g_bl], axis=1), "q_norm_g": g_qg,
                   "k_norm_g": g_kg, "sinks": g_sinks[:, :nq], "conv_w": g_convw, "conv_b": g_convb,
                   "b_rgate": g_br, "b_igate": g_bi, "lru_lambda": g_lam, "norm2_g": g_norm2}
    gshapes = [small_grads[n].shape for n in PACKED]
    small_sum = _allreduce_small(_pack([small_grads[n] for n in PACKED], _pack_rows(gshapes)), "allreduce_small")
    top, bot, grads_wr, grads_wi = _run_exchange(
        _FinishExchange([r.total for r in (r_top, r_bot, r_wr, r_wi)], [False, False, True, True]),
        "grad_finish_exchange")
    grads = dict(zip(BIG[1:], finished))
    grads["w_in"] = jnp.concatenate([top, bot], axis=0)
    grads["w_rgate"], grads["w_igate"] = grads_wr, grads_wi
    small_full = dict(zip(PACKED, _unpack(small_sum, gshapes)))
    per = d_rnn // N_CHIPS
    small_full["conv_w"] = lax.dynamic_slice(small_full["conv_w"], (0, chip * per), (taps, per))
    grads.update(small_full)

    delta, new_m, new_v = {}, {}, {}
    for n in BIG + ("w_rgate", "w_igate"):
        as2d = (lambda a: a[0]) if n in BIG else (lambda a: a.reshape(n_blocks * bw, bw))
        delta[n], new_m[n], new_v[n] = _adamw(as2d(w[n]), grads[n], as2d(mom[n]), as2d(var[n]), "adamw_" + n)
    pshapes = [w[n].shape for n in PACKED]
    prows = _pack_rows(pshapes)
    pk = [_pack([src[n] for n in PACKED], prows) for src in (w, grads, mom, var)]
    for res, packed in zip((delta, new_m, new_v), _adamw(pk[0], pk[1], pk[2], pk[3], "adamw_small")):
        res.update(dict(zip(PACKED, _unpack(packed, pshapes))))

    outs = [loss, dx.reshape(x.shape)]
    for res in (grads, delta, new_m, new_v):
        outs += [res[n].reshape(w[n].shape) for n in WEIGHTS]
    return tuple(outs)


def _allreduce_small(x, name):
    n_dev = 2 * N_CHIPS
    rel = [(fx, fy, fc) for fx in (0, 1) for fy in (0, 1) for fc in (0, 1)][1:]

    def body(x_ref, all_ref, o_ref, send_sems, recv_sems):
        e = _Env((x_ref,), (all_ref,), send_sems, recv_sems)
        mine = 2 * e.me + e.c

        def peer(r):
            fx, fy, fc = rel[r]
            return (1 - e.x if fx else e.x), (1 - e.y if fy else e.y), (1 - e.c if fc else e.c)

        all_ref[mine] = x_ref[...]
        for r in range(len(rel)):
            e.copy(x_ref, all_ref.at[mine], r, peer(r)).start()
        for r in range(len(rel)):
            px, py, pc = peer(r)
            e.copy(x_ref, all_ref.at[2 * (2 * px + py) + pc], r, peer(r)).wait_recv()
        total = all_ref[0]
        for dev in range(1, n_dev):
            total = total + all_ref[dev]
        o_ref[...] = total
        for r in range(len(rel)):
            e.copy(x_ref, all_ref.at[mine], r, peer(r)).wait_send()

    vm = pl.BlockSpec(memory_space=pltpu.VMEM)
    return pl.pallas_call(
        body, name=name, out_shape=(jax.ShapeDtypeStruct((n_dev,) + x.shape, x.dtype), jax.ShapeDtypeStruct(x.shape, x.dtype)),
        in_specs=[vm], out_specs=(vm, vm),
        scratch_shapes=[pltpu.SemaphoreType.DMA((len(rel),)), pltpu.SemaphoreType.DMA((len(rel),))])(x)[1]


def _gather_small(shard, name):
    def body(s_ref, o_ref, send_sems, recv_sems):
        e = _Env((s_ref,), (o_ref,), send_sems, recv_sems)
        o_ref[e.me] = s_ref[...]
        for k, (cx, cy) in enumerate(e.chips):
            e.copy(s_ref, o_ref.at[e.me], k, (cx, cy, e.c)).start()
        for k, (cx, cy) in enumerate(e.chips):
            e.copy(s_ref, o_ref.at[2 * cx + cy], k, (cx, cy, e.c)).wait_recv()
        for k, (cx, cy) in enumerate(e.chips):
            e.copy(s_ref, o_ref.at[e.me], k, (cx, cy, e.c)).wait_send()

    vm = pl.BlockSpec(memory_space=pltpu.VMEM)
    return pl.pallas_call(body, name=name, out_shape=jax.ShapeDtypeStruct((N_CHIPS,) + shard.shape, shard.dtype),
                          in_specs=[vm], out_specs=vm,
                          scratch_shapes=[pltpu.SemaphoreType.DMA((3,)), pltpu.SemaphoreType.DMA((3,))])(shard)
```

```python
import functools
import math

import jax
import jax.numpy as jnp
from jax import lax
from jax.experimental import pallas as pl
from jax.experimental.pallas import tpu as pltpu

F32 = jnp.float32
BF16 = jnp.bfloat16
MESH = pl.DeviceIdType.MESH

WINDOW = 128
BLK = 128
ROPE_THETA = 500000.0
LRU_C = 8.0
EPS = 1e-6
NEG = -1e30
ADAM_LR = 0.001
ADAM_B1 = 0.9
ADAM_B2 = 0.999
ADAM_EPS = 1e-08
ADAM_WD = 0.01
ADAM_STEP = 10

VMEM_LIMIT_BYTES = 52 * 1024 * 1024
LANE = 128
SUBLANE = 8
N_CHIPS = 4
SMALL_PACK_COLS = 512


def _params(**kw):
    return pltpu.CompilerParams(vmem_limit_bytes=VMEM_LIMIT_BYTES, **kw)


def _pick(dim, cands):
    for c in cands:
        if dim % c == 0:
            return c
    return dim


def _sigmoid(x):
    return 0.5 * jnp.tanh(0.5 * x) + 0.5


ANY = pl.BlockSpec(memory_space=pl.ANY)


class _Env:
    def __init__(self, ins, outs, send, recv, sem0=0, place=None):
        self.ins, self.outs, self.send, self.recv, self.sem0 = ins, outs, send, recv, sem0
        self.x, self.y, self.c = place or (lax.axis_index("x"), lax.axis_index("y"), lax.axis_index("c"))
        self.me = 2 * self.x + self.y
        self.chips = [(1 - self.x, self.y), (self.x, 1 - self.y), (1 - self.x, 1 - self.y)]
        self.sibling = (self.x, self.y, 1 - self.c)

    def sub(self, i0, n_in, o0, n_out, sem0):
        return _Env(self.ins[i0:i0 + n_in], self.outs[o0:o0 + n_out], self.send, self.recv, self.sem0 + sem0,
                    (self.x, self.y, self.c))

    def copy(self, src, dst, sem, to):
        return pltpu.make_async_remote_copy(src_ref=src, dst_ref=dst, send_sem=self.send.at[self.sem0 + sem],
                                            recv_sem=self.recv.at[self.sem0 + sem], device_id=to, device_id_type=MESH)


class _Exchange:
    inputs, out_shapes, aliases, n_sems = (), (), {}, 0

    def start(self, e):
        raise NotImplementedError

    def finish(self, e):
        raise NotImplementedError


class _Jobs(_Exchange):
    def __init__(self, *jobs):
        self.jobs, self.inputs, self.out_shapes, self.aliases, self.n_sems, self.at = jobs, [], [], {}, 0, []
        for job in jobs:
            self.at.append((len(self.inputs), len(self.out_shapes), self.n_sems))
            self.aliases.update({len(self.inputs) + i: len(self.out_shapes) + o for i, o in job.aliases.items()})
            self.inputs += list(job.inputs)
            self.out_shapes += list(job.out_shapes)
            self.n_sems += job.n_sems

    def _each(self, e):
        for job, (i0, o0, s0) in zip(self.jobs, self.at):
            yield job, e.sub(i0, len(job.inputs), o0, len(job.out_shapes), s0)

    def split(self, outs):
        return [tuple(outs[o0:o0 + len(job.out_shapes)]) for job, (_, o0, _) in zip(self.jobs, self.at)]

    def start(self, e):
        for job, se in self._each(e):
            job.start(se)

    def finish(self, e):
        for job, se in self._each(e):
            job.finish(se)


def _call(body, name, out_shape, grid, in_specs, out_specs, args, scratch_shapes=(), job=None, aliases=None):
    aliases = dict(aliases or {})
    if job is None:
        return pl.pallas_call(body, name=name, out_shape=out_shape, grid=grid, in_specs=list(in_specs),
                              out_specs=out_specs, scratch_shapes=list(scratch_shapes), input_output_aliases=aliases,
                              compiler_params=_params())(*args), ()
    single = not isinstance(out_shape, (tuple, list))
    shapes = [out_shape] if single else list(out_shape)
    ospecs = [out_specs] if single else list(out_specs)
    n_in, n_out, n_scr = len(args), len(shapes), len(scratch_shapes)
    j_in, j_out = len(job.inputs), len(job.out_shapes)

    def hosted(*refs):
        ins, jins = refs[:n_in], refs[n_in:n_in + j_in]
        outs = refs[n_in + j_in:n_in + j_in + n_out]
        jouts = refs[n_in + j_in + n_out:n_in + j_in + n_out + j_out]
        rest = refs[n_in + j_in + n_out + j_out:]
        e = _Env(jins, jouts, rest[n_scr], rest[n_scr + 1])
        first = functools.reduce(jnp.logical_and, [pl.program_id(d) == 0 for d in range(len(grid))])
        last = functools.reduce(jnp.logical_and, [pl.program_id(d) == g - 1 for d, g in enumerate(grid)])

        @pl.when(first)
        def _():
            job.start(e)

        body(*ins, *outs, *rest[:n_scr])

        @pl.when(last)
        def _():
            job.finish(e)

    res = pl.pallas_call(
        hosted, name=name, out_shape=tuple(shapes + list(job.out_shapes)), grid=grid,
        in_specs=list(in_specs) + [ANY] * j_in, out_specs=tuple(ospecs + [ANY] * j_out),
        scratch_shapes=list(scratch_shapes) + [pltpu.SemaphoreType.DMA((job.n_sems,)),
                                               pltpu.SemaphoreType.DMA((job.n_sems,))],
        input_output_aliases={**aliases, **{n_in + i: n_out + o for i, o in job.aliases.items()}},
        compiler_params=_params())(*args, *job.inputs)
    return (res[0] if single else tuple(res[:n_out])), tuple(res[n_out:])


def _run_exchange(job, name):
    n_in, n_out = len(job.inputs), len(job.out_shapes)

    def body(*refs):
        e = _Env(refs[:n_in], refs[n_in:n_in + n_out], refs[n_in + n_out], refs[n_in + n_out + 1])
        job.start(e)
        job.finish(e)

    return pl.pallas_call(
        body, name=name, out_shape=tuple(job.out_shapes), in_specs=[ANY] * n_in, out_specs=tuple([ANY] * n_out),
        input_output_aliases=dict(job.aliases),
        scratch_shapes=[pltpu.SemaphoreType.DMA((job.n_sems,)), pltpu.SemaphoreType.DMA((job.n_sems,))],
    )(*job.inputs)


_M_TILES = (1024, 1408, 1280, 512, 256, 128)
_N_TILES = (1408, 1280, 1024, 640, 512, 256, 128)
MXU_FULL_ROWS = 1024
MATMUL_VMEM_BUDGET = 42 * 1024 * 1024
MXU_FLOPS_PER_HBM_BYTE = 500


def _matmul_tiles(m, n, k, sa, sb, so, has_add, tn_divides=0):
    best = None
    for tm in [c for c in _M_TILES if m % c == 0] or [m]:
        for tn in [c for c in _N_TILES if n % c == 0 and tn_divides % c == 0] or [n]:
            for nk in range(1, 17):
                tk = k // nk
                if k % nk or tk % LANE:
                    continue
                need = 2 * (tm * tk * sa + tk * tn * sb) + 2 * tm * tn * (so + (4 if has_add else 0))
                need += tm * tn * 4 if nk > 1 else 0
                fetched = tk * tn * sb + tm * tk * sa // (1 if nk > 1 else n // tn)
                if need > MATMUL_VMEM_BUDGET:
                    continue
                mxu_bound = fetched * MXU_FLOPS_PER_HBM_BYTE <= 2 * tm * tn * tk
                key = (mxu_bound, min(tm, MXU_FULL_ROWS), -nk, tn, tm)
                if best is None or key > best[0]:
                    best = (key, (tm, tn, tk))
    assert best is not None, (m, n, k)
    return best[1]


def _matmul(a, b, mode, name, add=None, out_dtype=F32, job=None, m_window=None, into=None, fused=None):
    if mode == "nn":
        (m, k), (k2, n) = a.shape, b.shape
    elif mode == "nt":
        (m, k), (n, k2) = a.shape, b.shape
    else:
        (k, m), (k2, n) = a.shape, b.shape
    assert k == k2, (a.shape, b.shape, mode)
    m0, m = m_window or (0, m)
    tiles, fuse_fn, out_dtypes = fused or ((), None, (out_dtype,))
    tiles = [x if isinstance(x, tuple) else (x, 0) for x in tiles]
    tm, tn, tk = _matmul_tiles(math.gcd(m, m0) if m0 else m, n, k, a.dtype.itemsize, b.dtype.itemsize,
                               sum(jnp.dtype(dt).itemsize for dt in out_dtypes)
                               + sum(x.dtype.itemsize for x, _ in tiles if x.shape[0] > 1),
                               add is not None, math.gcd(*[c0 for _, c0 in tiles], 0))
    nk, mb0 = k // tk, m0 // tm
    if mode == "nn":
        a_spec = pl.BlockSpec((tm, tk), lambda i, j, kk: (mb0 + i, kk))
        b_spec = pl.BlockSpec((tk, tn), lambda i, j, kk: (kk, j))
        dims = (((1,), (0,)), ((), ()))
    elif mode == "nt":
        a_spec = pl.BlockSpec((tm, tk), lambda i, j, kk: (mb0 + i, kk))
        b_spec = pl.BlockSpec((tn, tk), lambda i, j, kk: (j, kk))
        dims = (((1,), (1,)), ((), ()))
    else:
        a_spec = pl.BlockSpec((tk, tm), lambda i, j, kk: (kk, mb0 + i))
        b_spec = pl.BlockSpec((tk, tn), lambda i, j, kk: (kk, j))
        dims = (((0,), (0,)), ((), ()))
    out_rows, ob0 = (into[1], mb0) if into is not None else (m, 0)
    o_spec = pl.BlockSpec((tm, tn), lambda i, j, kk: (ob0 + i, j))
    has_add = add is not None
    begun = into is not None and into[0] is not None

    n_side, n_out = has_add + len(tiles), len(out_dtypes)

    def body(*refs):
        a_ref, b_ref = refs[:2]
        side = refs[2:2 + n_side]
        o_refs = refs[len(refs) - n_out - (nk > 1):len(refs) - (nk > 1)]
        part = lax.dot_general(a_ref[...].astype(BF16), b_ref[...].astype(BF16), dims, preferred_element_type=F32)

        def finish(r):
            if has_add:
                r = r + side[0][...]
            vals = fuse_fn(r, *[x[...] for x in side[has_add:]]) if fuse_fn else (r,)
            for o_ref, val, dt in zip(o_refs, vals, out_dtypes):
                o_ref[...] = val.astype(dt)

        if nk == 1:
            finish(part)
            return
        acc = refs[-1]
        kk = pl.program_id(2)

        @pl.when(kk == 0)
        def _():
            acc[...] = part

        @pl.when(kk > 0)
        def _():
            acc[...] += part

        @pl.when(kk == nk - 1)
        def _():
            finish(acc[...])

    def side_spec(x, c0):
        if x.shape[0] == 1:
            return pl.BlockSpec((1, tn), lambda i, j, kk: (0, c0 // tn + j))
        return pl.BlockSpec((tm, tn), lambda i, j, kk: (mb0 + i, c0 // tn + j))

    in_specs = [a_spec, b_spec] + ([side_spec(add, 0)] if has_add else []) + [side_spec(x, c0) for x, c0 in tiles]
    args = (a, b) + ((add,) if has_add else ()) + tuple(x for x, _ in tiles)
    aliases = None
    if begun:
        aliases = {len(args): 0}
        in_specs, args = in_specs + [ANY], args + (into[0],)
    shapes = tuple(jax.ShapeDtypeStruct((out_rows, n), dt) for dt in out_dtypes)
    res, extra = _call(body, name, shapes if fused else shapes[0], (m // tm, n // tn, nk), in_specs,
                       (o_spec,) * n_out if fused else o_spec, args, [pltpu.VMEM((tm, tn), F32)] if nk > 1 else [],
                       job, aliases)
    return res if job is None else (res, extra)


def _row_tile(rows, cols, budget_elems=512 * 1024):
    cands = [c for c in (1024, 704, 512, 352, 256, 128, 64, 32, 16) if c * cols <= budget_elems]
    return _pick(rows, cands or (16,))


_EW_COLS = (1280, 1408, 1024, 640, 512, 256, 128)


def _tile2d(rows, cols, max_elems):
    tc = _pick(cols, _EW_COLS)
    return _row_tile(rows, tc, max_elems), tc


def _rms_fwd(x, g, name):
    t, d = x.shape
    tr = _row_tile(t, d)

    def body(x_ref, g_ref, o_ref):
        xv = x_ref[...]
        rstd = lax.rsqrt(jnp.mean(xv * xv, axis=-1, keepdims=True) + EPS)
        o_ref[...] = (xv * rstd * g_ref[...]).astype(BF16)

    spec = pl.BlockSpec((tr, d), lambda i: (i, 0))
    return pl.pallas_call(body, name=name, out_shape=jax.ShapeDtypeStruct((t, d), BF16), grid=(t // tr,),
                          in_specs=[spec, pl.BlockSpec((1, d), lambda i: (0, 0))], out_specs=spec,
                          compiler_params=_params())(x, g)


def _rms_bwd(dxn, x, g, resid, name, job=None, mxu_copy=False):
    t, d = x.shape
    tr = _row_tile(t, d, 256 * 1024)

    def body(dxn_ref, x_ref, g_ref, r_ref, dx_ref, dg_ref, *dx16_ref):
        @pl.when(pl.program_id(0) == 0)
        def _():
            dg_ref[...] = jnp.zeros_like(dg_ref)

        xv = x_ref[...]
        rstd = lax.rsqrt(jnp.mean(xv * xv, axis=-1, keepdims=True) + EPS)
        xhat = xv * rstd
        dy = dxn_ref[...]
        dg_ref[...] += jnp.sum(dy * xhat, axis=0, keepdims=True)
        dxhat = dy * g_ref[...]
        dx = r_ref[...] + rstd * (dxhat - xhat * jnp.mean(dxhat * xhat, axis=-1, keepdims=True))
        dx_ref[...] = dx
        if mxu_copy:
            dx16_ref[0][...] = dx.astype(BF16)

    spec = pl.BlockSpec((tr, d), lambda i: (i, 0))
    vec = pl.BlockSpec((1, d), lambda i: (0, 0))
    shapes = (jax.ShapeDtypeStruct((t, d), F32), jax.ShapeDtypeStruct((1, d), F32))
    shapes += (jax.ShapeDtypeStruct((t, d), BF16),) if mxu_copy else ()
    res, extra = _call(body, name, shapes, (t // tr,), [spec, spec, vec, spec],
                       (spec, vec) + ((spec,) if mxu_copy else ()), (dxn, x, g, resid), (), job)
    return res if job is None else (res, extra)


def _swiglu_after_up(up, gate):
    return up, gate * _sigmoid(gate) * up


def _swiglu_bwd_after_dact(dact, gate, up):
    sg = _sigmoid(gate)
    return dact * up * (sg * (1.0 + gate * (1.0 - sg))), dact * (gate * sg)


def _merge_after_lru_proj(plru, pa, ga, gl, ba, bl):
    return plru, _sigmoid(ga + ba) * pa + _sigmoid(gl + bl) * plru


def _merge_bwd(dmerged, z, b_gates, pa, plru, ga_off, name, job=None):
    t, d = pa.shape
    cw = _pick(math.gcd(ga_off, d), (512, 256, 128))
    tr = _row_tile(t, cw, 256 * 1024)
    oa, ol, nd = ga_off // cw, (ga_off + d) // cw, d // cw

    def body(dm_ref, ga_ref, gl_ref, ba_ref, bl_ref, pa_ref, pl_ref, dpa_ref, dpl_ref, dga_ref, dgl_ref, sa_ref, sl_ref):
        @pl.when(pl.program_id(1) == 0)
        def _():
            sa_ref[...] = jnp.zeros_like(sa_ref)
            sl_ref[...] = jnp.zeros_like(sl_ref)

        dm = dm_ref[...]
        sa = _sigmoid(ga_ref[...] + ba_ref[...])
        sl = _sigmoid(gl_ref[...] + bl_ref[...])
        dpa_ref[...] = (dm * sa).astype(BF16)
        dpl_ref[...] = (dm * sl).astype(BF16)
        dga = dm * pa_ref[...] * (sa * (1.0 - sa))
        dgl = dm * pl_ref[...] * (sl * (1.0 - sl))
        dga_ref[...] = dga.astype(BF16)
        dgl_ref[...] = dgl.astype(BF16)
        sa_ref[...] += jnp.sum(dga, axis=0, keepdims=True)
        sl_ref[...] += jnp.sum(dgl, axis=0, keepdims=True)

    blk = pl.BlockSpec((tr, cw), lambda j, i: (i, j))
    vec = pl.BlockSpec((1, cw), lambda j, i: (0, j))
    big16, v32 = jax.ShapeDtypeStruct((t, d), BF16), jax.ShapeDtypeStruct((1, d), F32)
    res, extra = _call(
        body, name, (big16, big16, big16, big16, v32, v32), (nd, t // tr),
        [blk, pl.BlockSpec((tr, cw), lambda j, i: (i, oa + j)), pl.BlockSpec((tr, cw), lambda j, i: (i, ol + j)),
         vec, pl.BlockSpec((1, cw), lambda j, i: (0, nd + j)), blk, blk],
        (blk, blk, blk, blk, vec, vec), (dmerged, z, z, b_gates, b_gates, pa, plru), (), job)
    return res if job is None else (res, extra)


def _loss_head(y, target, name):
    t, d = y.shape
    tr = _row_tile(t, d, 256 * 1024)
    nt = t // tr

    def body(y_ref, t_ref, dy_ref, dy16_ref, loss_ref, acc):
        i = pl.program_id(0)

        @pl.when(i == 0)
        def _():
            acc[...] = jnp.zeros_like(acc)

        e = y_ref[...] - t_ref[...]
        dy = e * (1.0 / d)
        dy_ref[...] = dy
        dy16_ref[...] = dy.astype(BF16)
        acc[...] += jnp.sum(e * e, axis=0, keepdims=True)

        @pl.when(i == nt - 1)
        def _():
            loss_ref[...] = (0.5 / d) * jnp.sum(acc[...], axis=-1, keepdims=True)

    spec = pl.BlockSpec((tr, d), lambda i: (i, 0))
    return pl.pallas_call(
        body, name=name, out_shape=(jax.ShapeDtypeStruct((t, d), F32), jax.ShapeDtypeStruct((t, d), BF16),
                                    jax.ShapeDtypeStruct((1, 1), F32)),
        grid=(nt,), in_specs=[spec, spec], out_specs=(spec, spec, pl.BlockSpec((1, 1), lambda i: (0, 0))),
        scratch_shapes=[pltpu.VMEM((1, d), F32)], compiler_params=_params(),
    )(y, target)


def _adamw(w, g, m, v, name, pass_grad=False):
    r, c = w.shape
    tr, tc = _tile2d(r, c, 512 * 1024)
    c1 = 1.0 - ADAM_B1 ** ADAM_STEP
    c2 = 1.0 - ADAM_B2 ** ADAM_STEP

    def body(w_ref, g_ref, m_ref, v_ref, d_ref, nm_ref, nv_ref, *g_out):
        gv = g_ref[...]
        if pass_grad:
            g_out[0][...] = gv
        mn = ADAM_B1 * m_ref[...] + (1.0 - ADAM_B1) * gv
        vn = ADAM_B2 * v_ref[...] + (1.0 - ADAM_B2) * (gv * gv)
        d_ref[...] = -ADAM_LR * ((mn / c1) / (jnp.sqrt(vn / c2) + ADAM_EPS) + ADAM_WD * w_ref[...])
        nm_ref[...] = mn
        nv_ref[...] = vn

    spec = pl.BlockSpec((tr, tc), lambda i, j: (i, j))
    shp = jax.ShapeDtypeStruct((r, c), F32)
    n_out = 4 if pass_grad else 3
    return pl.pallas_call(body, name=name, out_shape=(shp,) * n_out, grid=(r // tr, c // tc), in_specs=[spec] * 4,
                          out_specs=(spec,) * n_out, compiler_params=_params())(w, g, m, v)


def _swap_halves(v, half):
    n = v.shape[-1]
    lane = lax.broadcasted_iota(jnp.int32, v.shape, 1)
    return jnp.where(lane < half, pltpu.roll(v, n - half, 1),
                     jnp.where(lane < 2 * half, pltpu.roll(v, half, 1), 0.0))


def _rope(y, c, s, half):
    return y * c + _swap_halves(y, half) * s


def _rope_bwd(dout, c, s, half):
    return dout * c + _swap_halves(dout * s, half)


def _stack_heads(ref, grp, hd):
    return jnp.concatenate([ref[:, g * hd:(g + 1) * hd] for g in range(grp)], axis=0)


def _softmax_with_sinks(s, sink_ref, first, grp, i):
    rows = s.shape[0]
    qi = lax.broadcasted_iota(jnp.int32, s.shape, 0) & (BLK - 1)
    kj = lax.broadcasted_iota(jnp.int32, s.shape, 1)
    rel = qi + BLK - kj
    s = jnp.where((rel >= 0) & (rel < WINDOW) & ((kj >= BLK) | (i > 0)), s, NEG)
    head = lax.broadcasted_iota(jnp.int32, (rows, 1), 0) // BLK
    sk = jnp.zeros((rows, 1), F32)
    for g in range(grp):
        sk = jnp.where(head == g, sink_ref[first + g], sk)
    mx = jnp.maximum(jnp.max(s, axis=-1, keepdims=True), sk)
    p = jnp.exp(s - mx)
    esk = jnp.exp(sk - mx)
    inv_den = 1.0 / (jnp.sum(p, axis=-1, keepdims=True) + esk)
    return p * inv_den, esk * inv_den, head


def _norm_fwd(xraw, g):
    rstd = lax.rsqrt(jnp.mean(xraw * xraw, axis=-1, keepdims=True) + EPS)
    xhat = xraw * rstd
    return xhat, rstd, xhat * g


def _norm_bwd(dy, xhat, rstd, g):
    dxhat = dy * g
    dx = rstd * (dxhat - xhat * jnp.mean(dxhat * xhat, axis=-1, keepdims=True))
    return dx, jnp.sum(dy * xhat, axis=0, keepdims=True)


def _attn_specs(nb, grp, hd, kv, clamp):
    qo, ko, vo = 0, (kv * grp), (kv * grp + kv)
    cur = (lambda i: jnp.minimum(i, nb - 1)) if clamp else (lambda i: i)
    prev = lambda i: jnp.maximum(cur(i) - 1, 0)
    zq = pl.BlockSpec((BLK, grp * hd), lambda h, i: (cur(i), h))
    kc = pl.BlockSpec((BLK, hd), lambda h, i: (cur(i), ko + h))
    kp = pl.BlockSpec((BLK, hd), lambda h, i: (prev(i), ko + h))
    vc = pl.BlockSpec((BLK, hd), lambda h, i: (cur(i), vo + h))
    vp = pl.BlockSpec((BLK, hd), lambda h, i: (prev(i), vo + h))
    tc = pl.BlockSpec((BLK, hd), lambda h, i: (cur(i), 0))
    tp = pl.BlockSpec((BLK, hd), lambda h, i: (prev(i), 0))
    gs = pl.BlockSpec((1, hd), lambda h, i: (0, 0))
    return zq, kc, kp, vc, vp, tc, tp, gs


def _attn_fwd(z, cos_t, sin_t, qg, kg, sinks, kv, grp, hd, name, job=None):
    t = z.shape[0]
    nb = t // BLK
    half = hd // 8
    scale = 1.0 / math.sqrt(hd)
    zq, kc, kp, vc, vp, tc, tp, gs = _attn_specs(nb, grp, hd, kv, False)

    def body(sink_ref, zq_ref, kc_ref, kp_ref, vc_ref, vp_ref, cc_ref, sc_ref, cp_ref, sp_ref, qg_ref, kg_ref, o_ref):
        h, i = pl.program_id(0), pl.program_id(1)

        def normrope(xraw, g, c, s):
            return _rope(_norm_fwd(xraw, g)[2], c, s, half)

        cc, sc = cc_ref[...], sc_ref[...]
        kcur = normrope(kc_ref[...], kg_ref[...], cc, sc)
        kprev = normrope(kp_ref[...], kg_ref[...], cp_ref[...], sp_ref[...])
        kk = jnp.concatenate([kprev, kcur], axis=0).astype(BF16)
        vv = jnp.concatenate([vp_ref[...], vc_ref[...]], axis=0).astype(BF16)
        for g in range(grp):
            q = normrope(zq_ref[:, g * hd:(g + 1) * hd], qg_ref[...], cc, sc).astype(BF16)
            s = lax.dot_general(q, kk, (((1,), (1,)), ((), ())), preferred_element_type=F32) * scale
            p, _, _ = _softmax_with_sinks(s, sink_ref, h * grp + g, 1, i)
            o_ref[:, g * hd:(g + 1) * hd] = jnp.dot(p.astype(BF16), vv, preferred_element_type=F32).astype(BF16)

    res, extra = _call(
        body, name, jax.ShapeDtypeStruct((t, kv * grp * hd), BF16), (kv, nb),
        [pl.BlockSpec(memory_space=pltpu.SMEM), zq, kc, kp, vc, vp, tc, tc, tp, tp, gs, gs],
        pl.BlockSpec((BLK, grp * hd), lambda h, i: (i, h)),
        (sinks, z, z, z, z, z, cos_t, sin_t, cos_t, sin_t, qg, kg), (), job)
    return res if job is None else (res, extra)


def _attn_bwd(dattn, z, cos_t, sin_t, qg, kg, sinks, kv, grp, hd, name, job=None):
    t = z.shape[0]
    nb = t // BLK
    half = hd // 8
    scale = 1.0 / math.sqrt(hd)
    zq, kc, kp, vc, vp, tc, tp, gs = _attn_specs(nb, grp, hd, kv, True)

    def body(sink_ref, zq_ref, kc_ref, kp_ref, vc_ref, vp_ref, cc_ref, sc_ref, cp_ref, sp_ref, qg_ref, kg_ref, do_ref,
             dq_ref, dk_ref, dv_ref, dqg_ref, dkg_ref, dsk_ref, dk_carry, dv_carry):
        h, i = pl.program_id(0), pl.program_id(1)
        lane1 = lax.broadcasted_iota(jnp.int32, (1, LANE), 1)

        @pl.when((h == 0) & (i == 0))
        def _():
            dqg_ref[...] = jnp.zeros_like(dqg_ref)
            dkg_ref[...] = jnp.zeros_like(dkg_ref)
            dsk_ref[...] = jnp.zeros_like(dsk_ref)

        @pl.when(i == 0)
        def _():
            dk_carry[...] = jnp.zeros_like(dk_carry)
            dv_carry[...] = jnp.zeros_like(dv_carry)

        @pl.when(i < nb)
        def _():
            cc, sc, cp, sp = cc_ref[...], sc_ref[...], cp_ref[...], sp_ref[...]
            qgv, kgv = qg_ref[...], kg_ref[...]
            xh_kc, rs_kc, y_kc = _norm_fwd(kc_ref[...], kgv)
            xh_kp, rs_kp, y_kp = _norm_fwd(kp_ref[...], kgv)
            kk = jnp.concatenate([_rope(y_kp, cp, sp, half), _rope(y_kc, cc, sc, half)], axis=0).astype(BF16)
            vv = jnp.concatenate([vp_ref[...], vc_ref[...]], axis=0).astype(BF16)
            cq, sq = jnp.concatenate([cc] * grp, axis=0), jnp.concatenate([sc] * grp, axis=0)
            xh_q, rs_q, y_q = _norm_fwd(_stack_heads(zq_ref, grp, hd), qgv)
            q = _rope(y_q, cq, sq, half).astype(BF16)
            s = lax.dot_general(q, kk, (((1,), (1,)), ((), ())), preferred_element_type=F32) * scale
            p, psink, head = _softmax_with_sinks(s, sink_ref, h * grp, grp, i)
            dog = _stack_heads(do_ref, grp, hd).astype(BF16)
            dp = lax.dot_general(dog, vv, (((1,), (1,)), ((), ())), preferred_element_type=F32)
            rsum = jnp.sum(p * dp, axis=-1, keepdims=True)
            ds = (p * (dp - rsum) * scale).astype(BF16)
            dsink = -psink * rsum
            dsk = jnp.zeros((1, LANE), F32)
            for g in range(grp):
                dsk = dsk + jnp.where(lane1 == h * grp + g,
                                      jnp.sum(jnp.where(head == g, dsink, 0.0), axis=0, keepdims=True), 0.0)
            dqn = jnp.dot(ds, kk, preferred_element_type=F32)
            dkk = lax.dot_general(ds, q, (((0,), (0,)), ((), ())), preferred_element_type=F32)
            dvv = lax.dot_general(p.astype(BF16), dog, (((0,), (0,)), ((), ())), preferred_element_type=F32)
            dxq, dqg = _norm_bwd(_rope_bwd(dqn, cq, sq, half), xh_q, rs_q, qgv)
            dxq = dxq.astype(BF16)
            for g in range(grp):
                dq_ref[:, g * hd:(g + 1) * hd] = dxq[g * BLK:(g + 1) * BLK]
            dkp_raw, dg_kp = _norm_bwd(_rope_bwd(dkk[:BLK], cp, sp, half), xh_kp, rs_kp, kgv)
            dkc_raw, dg_kc = _norm_bwd(_rope_bwd(dkk[BLK:], cc, sc, half), xh_kc, rs_kc, kgv)
            dk_ref[...] = (dk_carry[...] + dkp_raw).astype(BF16)
            dv_ref[...] = (dv_carry[...] + dvv[:BLK]).astype(BF16)
            dk_carry[...] = dkc_raw
            dv_carry[...] = dvv[BLK:]
            dqg_ref[...] += dqg
            dkg_ref[...] += dg_kp + dg_kc
            dsk_ref[...] += dsk

        @pl.when(i == nb)
        def _():
            dk_ref[...] = dk_carry[...].astype(BF16)
            dv_ref[...] = dv_carry[...].astype(BF16)

    kvw = kv * hd
    vec = pl.BlockSpec((1, hd), lambda h, i: (0, 0))
    shifted = pl.BlockSpec((BLK, hd), lambda h, i: (jnp.maximum(i - 1, 0), h))
    res, extra = _call(
        body, name,
        (jax.ShapeDtypeStruct((t, kv * grp * hd), BF16), jax.ShapeDtypeStruct((t, kvw), BF16),
         jax.ShapeDtypeStruct((t, kvw), BF16), jax.ShapeDtypeStruct((1, hd), F32),
         jax.ShapeDtypeStruct((1, hd), F32), jax.ShapeDtypeStruct((1, LANE), F32)),
        (kv, nb + 1),
        [pl.BlockSpec(memory_space=pltpu.SMEM), zq, kc, kp, vc, vp, tc, tc, tp, tp, gs, gs,
         pl.BlockSpec((BLK, grp * hd), lambda h, i: (jnp.minimum(i, nb - 1), h))],
        (pl.BlockSpec((BLK, grp * hd), lambda h, i: (jnp.minimum(i, nb - 1), h)), shifted, shifted, vec, vec,
         pl.BlockSpec((1, LANE), lambda h, i: (0, 0))),
        (sinks, z, z, z, z, z, cos_t, sin_t, cos_t, sin_t, qg, kg, dattn),
        [pltpu.VMEM((BLK, hd), F32), pltpu.VMEM((BLK, hd), F32)], job)
    return res if job is None else (res, extra)


def _window(rows, cb, c0, row_of, col_of):
    assert rows % SUBLANE == 0 and cb % LANE == 0 and c0 % LANE == 0, (rows, cb, c0)
    return pl.BlockSpec((pl.Element(rows), pl.Element(cb)),
                        lambda *g: (pl.multiple_of(row_of(*g) * rows, SUBLANE), pl.multiple_of(c0 + col_of(*g) * cb, LANE)))


def _conv_fwd(z, c0, c, w, b, name):
    t = z.shape[0]
    taps = w.shape[0]
    cb = _pick(c, (1408, 1024, 512, 256, 128))
    tr = _row_tile(t, cb, 256 * 1024)
    hb = tr // SUBLANE

    def body(u_ref, halo_ref, w_ref, b_ref, o_ref):
        i = pl.program_id(0)
        x = u_ref[...]
        acc = b_ref[...] + w_ref[taps - 1:taps, :] * x
        for k in range(taps - 1):
            acc = acc + w_ref[k:k + 1, :] * pltpu.roll(x, taps - 1 - k, 0)
        o_ref[...] = acc
        row = lax.broadcasted_iota(jnp.int32, (SUBLANE, cb), 0)
        hp = jnp.where(i > 0, halo_ref[...], 0.0)
        x8 = u_ref[0:SUBLANE, :]
        acc8 = b_ref[...] + w_ref[taps - 1:taps, :] * x8
        for k in range(taps - 1):
            s = taps - 1 - k
            acc8 = acc8 + w_ref[k:k + 1, :] * jnp.where(row < s, pltpu.roll(hp, s, 0), pltpu.roll(x8, s, 0))
        o_ref[0:SUBLANE, :] = acc8

    blk = pl.BlockSpec((tr, cb), lambda i, j: (i, j))
    return pl.pallas_call(
        body, name=name, out_shape=jax.ShapeDtypeStruct((t, c), F32), grid=(t // tr, c // cb),
        in_specs=[_window(tr, cb, c0, lambda i, j: i, lambda i, j: j),
                  _window(SUBLANE, cb, c0, lambda i, j: jnp.maximum(i * hb - 1, 0), lambda i, j: j),
                  pl.BlockSpec((taps, cb), lambda i, j: (0, j)), pl.BlockSpec((1, cb), lambda i, j: (0, j))],
        out_specs=blk, compiler_params=_params(),
    )(z, z, w, b)


def _conv_bwd(duc, z, c0, w, name):
    t, c = duc.shape
    taps = w.shape[0]
    cb = _pick(c, (1408, 1024, 512, 256, 128))
    tr = _row_tile(t, cb, 256 * 1024)
    hb, nt = tr // SUBLANE, t // tr

    def body(g_ref, gnext_ref, u_ref, uprev_ref, w_ref, du16_ref, dw_ref, db_ref, du_ref):
        i = pl.program_id(1)

        @pl.when(i == 0)
        def _():
            dw_ref[...] = jnp.zeros_like(dw_ref)
            db_ref[...] = jnp.zeros_like(db_ref)

        row = lax.broadcasted_iota(jnp.int32, (SUBLANE, cb), 0)
        g, x = g_ref[...], u_ref[...]
        du = w_ref[taps - 1:taps, :] * g
        for k in range(taps - 1):
            du = du + w_ref[k:k + 1, :] * pltpu.roll(g, tr - (taps - 1 - k), 0)
        du_ref[...] = du
        hn = jnp.where(i < nt - 1, gnext_ref[...], 0.0)
        g8 = g_ref[tr - SUBLANE:tr, :]
        du8 = w_ref[taps - 1:taps, :] * g8
        for k in range(taps - 1):
            s = taps - 1 - k
            du8 = du8 + w_ref[k:k + 1, :] * jnp.where(row >= SUBLANE - s, pltpu.roll(hn, SUBLANE - s, 0),
                                                     pltpu.roll(g8, SUBLANE - s, 0))
        du_ref[tr - SUBLANE:tr, :] = du8
        du16_ref[...] = du_ref[...].astype(BF16)

        hp = jnp.where(i > 0, uprev_ref[...], 0.0)
        xl8, gf8 = u_ref[tr - SUBLANE:tr, :], g_ref[0:SUBLANE, :]
        db_ref[...] += jnp.sum(g, axis=0, keepdims=True)
        dw_ref[taps - 1:taps, :] += jnp.sum(g * x, axis=0, keepdims=True)
        for k in range(taps - 1):
            s = taps - 1 - k
            fix = jnp.where(row < s, pltpu.roll(hp, s, 0) - pltpu.roll(xl8, s, 0), 0.0)
            dw_ref[k:k + 1, :] += (jnp.sum(g * pltpu.roll(x, s, 0), axis=0, keepdims=True)
                                   + jnp.sum(gf8 * fix, axis=0, keepdims=True))

    blk = pl.BlockSpec((tr, cb), lambda j, i: (i, j))
    nh = t // SUBLANE
    return pl.pallas_call(
        body, name=name,
        out_shape=(jax.ShapeDtypeStruct((t, c), BF16), jax.ShapeDtypeStruct((taps, c), F32),
                   jax.ShapeDtypeStruct((1, c), F32)),
        grid=(c // cb, nt),
        in_specs=[blk, pl.BlockSpec((SUBLANE, cb), lambda j, i: (jnp.minimum((i + 1) * hb, nh - 1), j)),
                  _window(tr, cb, c0, lambda j, i: i, lambda j, i: j),
                  _window(SUBLANE, cb, c0, lambda j, i: jnp.maximum(i * hb - 1, 0), lambda j, i: j),
                  pl.BlockSpec((taps, cb), lambda j, i: (0, j))],
        out_specs=(blk, pl.BlockSpec((taps, cb), lambda j, i: (0, j)), pl.BlockSpec((1, cb), lambda j, i: (0, j))),
        scratch_shapes=[pltpu.VMEM((tr, cb), F32)], compiler_params=_params(),
    )(duc, duc, z, z, w)


def _gates_fwd(uc, wr, wi, gw, name):
    t, c = uc.shape
    n, bw, _ = wr.shape
    per, ng = gw // bw, c // gw
    tr = _pick(t, (512, 256, 128))

    def body(u_ref, wr_ref, wi_ref, r_ref, i_ref):
        for b in range(per):
            cols = slice(b * bw, (b + 1) * bw)
            a = u_ref[:, cols].astype(BF16)
            r_ref[:, cols] = jnp.dot(a, wr_ref[b].astype(BF16), preferred_element_type=F32)
            i_ref[:, cols] = jnp.dot(a, wi_ref[b].astype(BF16), preferred_element_type=F32)

    blk = pl.BlockSpec((tr, gw), lambda h, i: (i, h))
    wsp = pl.BlockSpec((per, bw, bw), lambda h, i: (h, 0, 0))
    shp = jax.ShapeDtypeStruct((t, c), F32)
    return pl.pallas_call(body, name=name, out_shape=(shp, shp), grid=(ng, t // tr), in_specs=[blk, wsp, wsp],
                          out_specs=(blk, blk), compiler_params=_params())(uc, wr, wi)


def _gates_bwd_x(duc, drp, dip, wr, wi, gw, name):
    t, c = duc.shape
    n, bw, _ = wr.shape
    per, ng = gw // bw, c // gw
    tr = _pick(t, (512, 256, 128))
    dims = (((1,), (1,)), ((), ()))

    def body(d_ref, r_ref, i_ref, wr_ref, wi_ref, o_ref):
        for b in range(per):
            cols = slice(b * bw, (b + 1) * bw)
            o_ref[:, cols] = (
                d_ref[:, cols]
                + lax.dot_general(r_ref[:, cols].astype(BF16), wr_ref[b].astype(BF16), dims, preferred_element_type=F32)
                + lax.dot_general(i_ref[:, cols].astype(BF16), wi_ref[b].astype(BF16), dims, preferred_element_type=F32))

    blk = pl.BlockSpec((tr, gw), lambda h, i: (i, h))
    wsp = pl.BlockSpec((per, bw, bw), lambda h, i: (h, 0, 0))
    return pl.pallas_call(body, name=name, out_shape=jax.ShapeDtypeStruct((t, c), F32), grid=(ng, t // tr),
                          in_specs=[blk, blk, blk, wsp, wsp], out_specs=blk, compiler_params=_params())(duc, drp, dip, wr, wi)


def _gates_bwd_w(uc, dpre, n, bw, gw, name):
    t, c = uc.shape
    per, ng = gw // bw, c // gw
    tk = _pick(t, (512, 256, 128))
    dims = (((0,), (0,)), ((), ()))

    def body(u_ref, d_ref, o_ref):
        @pl.when(pl.program_id(1) == 0)
        def _():
            o_ref[...] = jnp.zeros_like(o_ref)

        for b in range(per):
            cols = slice(b * bw, (b + 1) * bw)
            o_ref[b] += lax.dot_general(u_ref[:, cols].astype(BF16), d_ref[:, cols].astype(BF16), dims,
                                        preferred_element_type=F32)

    blk = pl.BlockSpec((tk, gw), lambda h, i: (i, h))
    return pl.pallas_call(body, name=name, out_shape=jax.ShapeDtypeStruct((n, bw, bw), F32), grid=(ng, t // tk),
                          in_specs=[blk, blk], out_specs=pl.BlockSpec((per, bw, bw), lambda h, i: (h, 0, 0)),
                          compiler_params=_params())(uc, dpre)


def _softplus(x):
    return jnp.maximum(x, 0.0) + jnp.log(1.0 + jnp.exp(-jnp.abs(x)))


_GELU_C = math.sqrt(2.0 / math.pi)


def _gelu_parts(x):
    inner = _GELU_C * (x + 0.044715 * (x * x * x))
    th = jnp.tanh(inner)
    gelu = 0.5 * x * (1.0 + th)
    dgelu = 0.5 * (1.0 + th) + 0.5 * x * (1.0 - th * th) * (_GELU_C * (1.0 + 3.0 * 0.044715 * (x * x)))
    return gelu, dgelu


def _lru_gate_values(rpre, ipre, br, bi, sp):
    r = _sigmoid(rpre + br)
    ig = _sigmoid(ipre + bi)
    log_a = -LRU_C * r * sp
    a = jnp.exp(log_a)
    e2 = jnp.tanh(-log_a) * (1.0 + a * a)
    inv = lax.rsqrt(jnp.maximum(e2, 1e-30))
    return r, ig, a, e2 * inv, inv


def _lru_fwd(uc, rpre, ipre, z, gr0, br, bi, lam, name, job=None):
    t, c = uc.shape
    cb = _pick(c, (1408, 1024, 512, 256, 128))
    tb = _pick(t, (512, 256, 128))
    ntile = tb // SUBLANE

    def body(uc_ref, r_ref, i_ref, gr_ref, br_ref, bi_ref, lam_ref, h_ref, rec16_ref, carry, rec_ref):
        @pl.when(pl.program_id(1) == 0)
        def _():
            carry[...] = jnp.zeros_like(carry)

        sp = _softplus(-lam_ref[...])
        br, bi = br_ref[...], bi_ref[...]
        row = lax.broadcasted_iota(jnp.int32, (SUBLANE, cb), 0)

        def tile(k, c_in):
            sl = pl.ds(pl.multiple_of(k * SUBLANE, SUBLANE), SUBLANE)
            ucv = uc_ref[sl, :]
            _, ig, a, mult, _ = _lru_gate_values(r_ref[sl, :], i_ref[sl, :], br, bi, sp)
            b = mult * (ig * ucv)
            for d in (1, 2, 4):
                a_s = jnp.where(row >= d, pltpu.roll(a, d, 0), 1.0)
                b_s = jnp.where(row >= d, pltpu.roll(b, d, 0), 0.0)
                b = a * b_s + b
                a = a * a_s
            hv = b + a * c_in
            h_ref[sl, :] = hv
            rec_ref[sl, :] = hv * _gelu_parts(gr_ref[sl, :])[0]
            return hv[SUBLANE - 1:SUBLANE, :]

        c_out = lax.fori_loop(0, ntile, tile, carry[0:1, :])
        carry[...] = jnp.broadcast_to(c_out, (SUBLANE, cb))
        rec16_ref[...] = rec_ref[...].astype(BF16)

    blk = pl.BlockSpec((tb, cb), lambda j, i: (i, j))
    vec = pl.BlockSpec((1, cb), lambda j, i: (0, j))
    res, extra = _call(body, name, (jax.ShapeDtypeStruct((t, c), F32), jax.ShapeDtypeStruct((t, c), BF16)),
                       (c // cb, t // tb),
                       [blk, blk, blk, _window(tb, cb, gr0, lambda j, i: i, lambda j, i: j), vec, vec, vec], (blk, blk),
                       (uc, rpre, ipre, z, br, bi, lam), [pltpu.VMEM((SUBLANE, cb), F32), pltpu.VMEM((tb, cb), F32)], job)
    return res if job is None else (res, extra)


def _lru_bwd(drec, hst, uc, rpre, ipre, z, gr0, br, bi, lam, name, job=None):
    t, c = uc.shape
    cb = _pick(c, (1408, 1024, 512, 256, 128))
    tb = _pick(t, (256, 128))
    ntile, nt, hb = tb // SUBLANE, t // tb, tb // SUBLANE

    def body(drec_ref, h_ref, hprev_ref, uc_ref, r_ref, i_ref, gr_ref, br_ref, bi_ref, lam_ref,
             dgr16_ref, drp_ref, dip_ref, duc_ref, dlam_ref, dbr_ref, dbi_ref, carry, dgr_ref):
        step = pl.program_id(1)
        first_block = step == nt - 1

        @pl.when(step == 0)
        def _():
            carry[...] = jnp.zeros_like(carry)
            dlam_ref[...] = jnp.zeros_like(dlam_ref)
            dbr_ref[...] = jnp.zeros_like(dbr_ref)
            dbi_ref[...] = jnp.zeros_like(dbi_ref)

        lam = lam_ref[...]
        sp = _softplus(-lam)
        br, bi = br_ref[...], bi_ref[...]
        row = lax.broadcasted_iota(jnp.int32, (SUBLANE, cb), 0)
        halo = jnp.where(first_block, 0.0, hprev_ref[...])

        def tile(kk, state):
            c_p, acc_sp, acc_br, acc_bi = state
            k = ntile - 1 - kk
            sl = pl.ds(pl.multiple_of(k * SUBLANE, SUBLANE), SUBLANE)
            slp = pl.ds(pl.multiple_of(jnp.maximum(k - 1, 0) * SUBLANE, SUBLANE), SUBLANE)
            ucv = uc_ref[sl, :]
            r, ig, a, mult, inv_mult = _lru_gate_values(r_ref[sl, :], i_ref[sl, :], br, bi, sp)
            hv = h_ref[sl, :]
            below = jnp.where(k > 0, h_ref[slp, :], halo)
            hprev = jnp.where(row == 0, pltpu.roll(below, 1, 0), pltpu.roll(hv, 1, 0))
            gelu, dgelu = _gelu_parts(gr_ref[sl, :])
            drec = drec_ref[sl, :]
            dh = drec * gelu
            dgr_ref[sl, :] = drec * hv * dgelu
            pa, pb = a, a * dh
            for d in (1, 2, 4):
                a_s = jnp.where(row < SUBLANE - d, pltpu.roll(pa, SUBLANE - d, 0), 1.0)
                b_s = jnp.where(row < SUBLANE - d, pltpu.roll(pb, SUBLANE - d, 0), 0.0)
                pb = pa * b_s + pb
                pa = pa * a_s
            pv = pb + pa * c_p
            gt = dh + jnp.where(row == SUBLANE - 1, c_p, pltpu.roll(pv, SUBLANE - 1, 0))
            da = gt * hprev
            duc_ref[sl, :] = gt * mult * ig
            dmult = gt * ig * ucv
            dig = gt * mult * ucv
            dla = da * a - jnp.where(mult > 0.0, dmult * (a * a) * inv_mult, 0.0)
            drp = dla * (-LRU_C * sp) * (r * (1.0 - r))
            dip = dig * (ig * (1.0 - ig))
            drp_ref[sl, :] = drp
            dip_ref[sl, :] = dip
            return pv[0:1, :], acc_sp + dla * (-LRU_C * r), acc_br + drp, acc_bi + dip

        zero = jnp.zeros((SUBLANE, cb), F32)
        c_out, acc_sp, acc_br, acc_bi = lax.fori_loop(0, ntile, tile, (carry[0:1, :], zero, zero, zero))
        carry[...] = jnp.broadcast_to(c_out, (SUBLANE, cb))
        dlam_ref[...] += jnp.sum(acc_sp, axis=0, keepdims=True) * (-_sigmoid(-lam))
        dbr_ref[...] += jnp.sum(acc_br, axis=0, keepdims=True)
        dbi_ref[...] += jnp.sum(acc_bi, axis=0, keepdims=True)
        dgr16_ref[...] = dgr_ref[...].astype(BF16)

    blk = pl.BlockSpec((tb, cb), lambda j, i: (nt - 1 - i, j))
    vec = pl.BlockSpec((1, cb), lambda j, i: (0, j))
    halo_spec = pl.BlockSpec((SUBLANE, cb), lambda j, i: (jnp.maximum((nt - 1 - i) * hb - 1, 0), j))
    big, small = jax.ShapeDtypeStruct((t, c), F32), jax.ShapeDtypeStruct((1, c), F32)
    res, extra = _call(
        body, name, (jax.ShapeDtypeStruct((t, c), BF16), big, big, big, small, small, small), (c // cb, nt),
        [blk, blk, halo_spec, blk, blk, blk, _window(tb, cb, gr0, lambda j, i: nt - 1 - i, lambda j, i: j),
         vec, vec, vec], (blk, blk, blk, blk, vec, vec, vec),
        (drec, hst, hst, uc, rpre, ipre, z, br, bi, lam),
        [pltpu.VMEM((SUBLANE, cb), F32), pltpu.VMEM((tb, cb), F32)], job)
    return res if job is None else (res, extra)


def _shard_region(ref, kind, chip, half, rh, width):
    if kind == "col":
        return ref.at[pl.ds(half * rh, rh), pl.ds(chip * width, width)]
    return ref.at[pl.ds(chip * (2 * rh) + half * rh, rh), :]


class _AllGather(_Exchange):
    def __init__(self, fulls, kinds):
        self.inputs, self.kinds = list(fulls), kinds
        self.out_shapes = [jax.ShapeDtypeStruct(f.shape, f.dtype) for f in fulls]
        self.aliases = {a: a for a in range(len(fulls))}
        self.n_sems = 6 * len(fulls)
        self.geo = [(f.shape[0] // 2, f.shape[1] // N_CHIPS) if k == "col" else (f.shape[0] // (2 * N_CHIPS), f.shape[1])
                    for f, k in zip(fulls, kinds)]

    def _region(self, ref, a, chip, half):
        return _shard_region(ref, self.kinds[a], chip, half, *self.geo[a])

    def _ici(self, e, a, k, chip):
        cx, cy = e.chips[k]
        return e.copy(self._region(e.ins[a], a, chip, e.c), self._region(e.outs[a], a, chip, e.c), a * 6 + k,
                      (cx, cy, e.c))

    def _d2d(self, e, a, k, half):
        cx, cy = e.chips[k]
        region = self._region(e.outs[a], a, 2 * cx + cy, half)
        return e.copy(region, region, a * 6 + 3 + k, e.sibling)

    def start(self, e):
        for a in range(len(self.inputs)):
            for k in range(3):
                self._ici(e, a, k, e.me).start()

    def finish(self, e):
        n = len(self.inputs)
        for a in range(n):
            for k, (cx, cy) in enumerate(e.chips):
                self._ici(e, a, k, 2 * cx + cy).wait_recv()
                self._d2d(e, a, k, e.c).start()
        for a in range(n):
            for k in range(3):
                self._d2d(e, a, k, 1 - e.c).wait_recv()
        for a in range(n):
            for k in range(3):
                self._ici(e, a, k, e.me).wait_send()
                self._d2d(e, a, k, e.c).wait_send()


class _SiblingExchange(_Exchange):
    def __init__(self, grads):
        self.inputs = list(grads)
        self.out_shapes = [jax.ShapeDtypeStruct((g.shape[0],) + g.shape[2:], g.dtype) for g in grads]
        self.n_sems = len(grads)

    def _copy(self, e, a):
        return e.copy(e.ins[a].at[:, 1 - e.c], e.outs[a], a, e.sibling)

    def start(self, e):
        for a in range(len(self.inputs)):
            self._copy(e, a).start()

    def finish(self, e):
        for a in range(len(self.inputs)):
            self._copy(e, a).wait()


def _piece(ref, kind, chip, width):
    if kind == "col":
        return ref.at[0, :, pl.ds(chip * width, width)]
    return ref.at[chip]


class _ChipExchange(_Exchange):
    def __init__(self, sums, kinds):
        self.inputs, self.kinds = list(sums), kinds
        self.widths = [s.shape[2] // N_CHIPS if k == "col" else s.shape[2] for s, k in zip(sums, kinds)]
        self.out_shapes = [jax.ShapeDtypeStruct((3, s.shape[1], w), s.dtype) for s, w in zip(sums, self.widths)]
        self.n_sems = 3 * len(sums)

    def _copy(self, e, a, k, chip):
        cx, cy = e.chips[k]
        return e.copy(_piece(e.ins[a], self.kinds[a], chip, self.widths[a]), e.outs[a].at[k], a * 3 + k, (cx, cy, e.c))

    def start(self, e):
        for a in range(len(self.inputs)):
            for k, (cx, cy) in enumerate(e.chips):
                self._copy(e, a, k, 2 * cx + cy).start()

    def finish(self, e):
        for a in range(len(self.inputs)):
            for k, (cx, cy) in enumerate(e.chips):
                self._copy(e, a, k, 2 * cx + cy).wait()


class _FinishExchange(_Exchange):
    def __init__(self, finals, to_all):
        self.inputs, self.to_all = list(finals), list(to_all)
        self.out_shapes = [jax.ShapeDtypeStruct(f.shape, f.dtype) for f in finals]
        self.aliases = {a: a for a in range(len(finals))}
        self.first_sem, self.n_sems = [], 0
        for all8 in self.to_all:
            self.first_sem.append(self.n_sems)
            self.n_sems += 7 if all8 else 1
        self.rel = [(fx, fy, fc) for fx in (0, 1) for fy in (0, 1) for fc in (0, 1)][1:]

    def _copies(self, e, mine):
        for a, all8 in enumerate(self.to_all):
            src = e.ins[a] if mine else e.outs[a]
            if not all8:
                rh = self.inputs[a].shape[0] // 2
                rows = pl.ds((e.c if mine else 1 - e.c) * rh, rh)
                yield e.copy(src.at[rows, :], e.outs[a].at[rows, :], self.first_sem[a], e.sibling)
                continue
            rh = self.inputs[a].shape[0] // (2 * N_CHIPS)
            for r, (fx, fy, fc) in enumerate(self.rel):
                px, py, pc = (1 - e.x if fx else e.x), (1 - e.y if fy else e.y), (1 - e.c if fc else e.c)
                rows = pl.ds(((2 * e.me + e.c) if mine else (2 * (2 * px + py) + pc)) * rh, rh)
                yield e.copy(src.at[rows, :], e.outs[a].at[rows, :], self.first_sem[a] + r, (px, py, pc))

    def start(self, e):
        for cp in self._copies(e, True):
            cp.start()

    def finish(self, e):
        for cp in self._copies(e, False):
            cp.wait_recv()
        for cp in self._copies(e, True):
            cp.wait_send()


def _cast_into_full(w, kind, idx, name):
    r, c = w.shape
    tr = _row_tile(r, c)
    nrb = r // tr

    def body(idx_ref, w_ref, o_ref):
        o_ref[...] = w_ref[...].astype(BF16)

    if kind == "col":
        full, out_map = (r, N_CHIPS * c), (lambda i, idx_ref: (i, idx_ref[1]))
    else:
        full, out_map = (N_CHIPS * r, c), (lambda i, idx_ref: (idx_ref[1] * nrb + i, 0))
    return pl.pallas_call(
        body, name=name, out_shape=jax.ShapeDtypeStruct(full, BF16),
        grid_spec=pltpu.PrefetchScalarGridSpec(
            num_scalar_prefetch=1, grid=(nrb,), in_specs=[pl.BlockSpec((tr, c), lambda i, idx_ref: (i, 0))],
            out_specs=pl.BlockSpec((tr, c), out_map)),
        compiler_params=_params(),
    )(idx, w)


def _matmul_gathering(a, placed, order, name):
    t, k = a.shape
    n = placed.shape[1]
    w = n // N_CHIPS
    tm, tn = _pick(t, _M_TILES), _pick(w, _N_TILES)
    ni, nj = t // tm, w // tn
    per_shard, total = ni * nj, N_CHIPS * ni * nj
    gather = _AllGather([placed], ["col"])

    def body(ord_ref, a_ref, w_own_ref, o_ref, w_ref, wbuf, fetch_sem, send, recv):
        s, i, j = pl.program_id(0), pl.program_id(1), pl.program_id(2)
        step = (s * ni + i) * nj + j
        e = _Env((w_own_ref,), (w_ref,), send, recv)

        def fetch(src, st):
            col = pl.multiple_of((ord_ref[st // per_shard] * nj + st % nj) * tn, LANE)
            return pltpu.make_async_copy(src.at[:, pl.ds(col, tn)], wbuf.at[st % 2], fetch_sem.at[st % 2])

        @pl.when(step == 0)
        def _():
            gather.start(e)
            fetch(w_own_ref, step).start()

        nxt = step + 1
        for kk, (cx, cy) in enumerate(e.chips):
            @pl.when(nxt == (kk + 1) * per_shard)
            def _():
                gather._ici(e, 0, kk, 2 * cx + cy).wait_recv()
                gather._d2d(e, 0, kk, e.c).start()
                gather._d2d(e, 0, kk, 1 - e.c).wait_recv()

        @pl.when(nxt < per_shard)
        def _():
            fetch(w_own_ref, nxt).start()

        @pl.when((nxt >= per_shard) & (nxt < total))
        def _():
            fetch(w_ref, nxt).start()

        fetch(w_ref, step).wait()
        o_ref[...] = jnp.dot(a_ref[...], wbuf[step % 2], preferred_element_type=F32)

        @pl.when(step == total - 1)
        def _():
            for kk in range(3):
                gather._ici(e, 0, kk, e.me).wait_send()
                gather._d2d(e, 0, kk, e.c).wait_send()

    z, full = pl.pallas_call(
        body, name=name, out_shape=(jax.ShapeDtypeStruct((t, n), F32), jax.ShapeDtypeStruct(placed.shape, placed.dtype)),
        grid_spec=pltpu.PrefetchScalarGridSpec(
            num_scalar_prefetch=1, grid=(N_CHIPS, ni, nj),
            in_specs=[pl.BlockSpec((tm, k), lambda s, i, j, ord_ref: (i, 0)), ANY],
            out_specs=(pl.BlockSpec((tm, tn), lambda s, i, j, ord_ref: (i, ord_ref[s] * nj + j)), ANY),
            scratch_shapes=[pltpu.VMEM((2, k, tn), placed.dtype), pltpu.SemaphoreType.DMA((2,)),
                            pltpu.SemaphoreType.DMA((gather.n_sems,)), pltpu.SemaphoreType.DMA((gather.n_sems,))]),
        input_output_aliases={2: 1}, compiler_params=_params(),
    )(order, a, placed)
    return z, full


def _add_own_half(g4, recv, idx, out_dtype, name):
    p, _, rh, n = g4.shape
    tr, tc = _tile2d(rh, n, 1024 * 1024)

    def body(idx_ref, g_ref, r_ref, o_ref):
        o_ref[...] = (g_ref[...] + r_ref[...]).astype(out_dtype)

    return pl.pallas_call(
        body, name=name, out_shape=jax.ShapeDtypeStruct((p, rh, n), out_dtype),
        grid_spec=pltpu.PrefetchScalarGridSpec(
            num_scalar_prefetch=1, grid=(p, rh // tr, n // tc),
            in_specs=[pl.BlockSpec((None, None, tr, tc), lambda q, i, j, idx_ref: (q, idx_ref[0], i, j)),
                      pl.BlockSpec((None, tr, tc), lambda q, i, j, idx_ref: (q, i, j))],
            out_specs=pl.BlockSpec((None, tr, tc), lambda q, i, j, idx_ref: (q, i, j))),
        compiler_params=_params(),
    )(idx, g4, recv)


def _sum_chips(own, kind, parts, idx, slots, to_all, name):
    _, rh, w = parts.shape
    tr, tc = _tile2d(rh, w, 512 * 1024)
    nrb, ncb = rh // tr, w // tc

    def body(idx_ref, own_ref, p0, p1, p2, o_ref):
        o_ref[...] = ((own_ref[...].astype(F32) + p0[...].astype(F32)) + p1[...].astype(F32)) + p2[...].astype(F32)

    if kind == "col":
        own_spec = pl.BlockSpec((None, tr, tc), lambda i, j, idx_ref: (0, i, idx_ref[1] * ncb + j))
    else:
        own_spec = pl.BlockSpec((None, tr, tc), lambda i, j, idx_ref: (idx_ref[1], i, j))
    if to_all:
        out_map = lambda i, j, idx_ref: ((2 * idx_ref[1] + idx_ref[0]) * nrb + i, j)
    else:
        out_map = lambda i, j, idx_ref: (idx_ref[0] * nrb + i, j)

    def part(k):
        return pl.BlockSpec((None, tr, tc), lambda i, j, idx_ref: (k, i, j))

    return pl.pallas_call(
        body, name=name, out_shape=jax.ShapeDtypeStruct((slots * rh, w), F32),
        grid_spec=pltpu.PrefetchScalarGridSpec(
            num_scalar_prefetch=1, grid=(nrb, ncb), in_specs=[own_spec, part(0), part(1), part(2)],
            out_specs=pl.BlockSpec((tr, tc), out_map)),
        compiler_params=_params(),
    )(idx, own, parts, parts, parts)


class _Reduce:
    def __init__(self, name, g, kind, idx, wire, to_all):
        r, c = g.shape
        self.name, self.kind, self.idx, self.wire, self.to_all = name, kind, idx, wire, to_all
        self.view = g.reshape(1, 2, r // 2, c) if kind == "col" else g.reshape(N_CHIPS, 2, r // (2 * N_CHIPS), c)

    def sibling(self):
        return _SiblingExchange([self.view])

    def got_sibling(self, outs):
        self.sum = _add_own_half(self.view, outs[0], self.idx, self.wire, "grad_chip_sum_" + self.name)

    def chips(self):
        return _ChipExchange([self.sum], [self.kind])

    def got_chips(self, outs):
        self.total = _sum_chips(self.sum, self.kind, outs[0], self.idx, 2 * N_CHIPS if self.to_all else 2,
                                self.to_all, "grad_total_" + self.name)


def _pack(arrays, rows):
    flat = jnp.concatenate([a.reshape(-1) for a in arrays])
    return jnp.pad(flat, (0, rows * SMALL_PACK_COLS - flat.shape[0])).reshape(rows, SMALL_PACK_COLS)


def _unpack(packed, shapes):
    flat = packed.reshape(-1)
    out, o = [], 0
    for shp in shapes:
        size = math.prod(shp)
        out.append(flat[o:o + size].reshape(shp))
        o += size
    return out


def _pack_rows(shapes):
    total = sum(math.prod(s) for s in shapes)
    unit = SMALL_PACK_COLS * N_CHIPS * 2 * SUBLANE
    return -(-total // unit) * (N_CHIPS * 2 * SUBLANE)


BIG = ("w_in", "w_attn_proj", "w_lru_proj", "w_out", "w_ffn_gate", "w_ffn_up", "w_ffn_down")
BIG_KIND = {"w_in": "col", "w_attn_proj": "row", "w_lru_proj": "row", "w_out": "row", "w_ffn_gate": "col",
            "w_ffn_up": "col", "w_ffn_down": "row"}
SMALL = ("norm1_g", "b_gates", "q_norm_g", "k_norm_g", "sinks", "conv_w", "conv_b", "w_rgate", "b_rgate",
         "w_igate", "b_igate", "lru_lambda", "norm2_g")
PACKED = tuple(n for n in SMALL if n not in ("w_rgate", "w_igate"))
WEIGHTS = ("norm1_g", "w_in", "b_gates", "q_norm_g", "k_norm_g", "sinks", "conv_w", "conv_b", "w_rgate", "b_rgate",
           "w_igate", "b_igate", "lru_lambda", "w_attn_proj", "w_lru_proj", "w_out", "norm2_g", "w_ffn_gate",
           "w_ffn_up", "w_ffn_down")


def kernel(x, positions, norm1_g, w_in, b_gates, q_norm_g, k_norm_g, sinks, conv_w, conv_b, w_rgate, b_rgate, w_igate, b_igate, lru_lambda, w_attn_proj, w_lru_proj, w_out, norm2_g, w_ffn_gate, w_ffn_up, w_ffn_down, loss_target, m_norm1_g, m_w_in, m_b_gates, m_q_norm_g, m_k_norm_g, m_sinks, m_conv_w, m_conv_b, m_w_rgate, m_b_rgate, m_w_igate, m_b_igate, m_lru_lambda, m_w_attn_proj, m_w_lru_proj, m_w_out, m_norm2_g, m_w_ffn_gate, m_w_ffn_up, m_w_ffn_down, v_norm1_g, v_w_in, v_b_gates, v_q_norm_g, v_k_norm_g, v_sinks, v_conv_w, v_conv_b, v_w_rgate, v_b_rgate, v_w_igate, v_b_igate, v_lru_lambda, v_w_attn_proj, v_w_lru_proj, v_w_out, v_norm2_g, v_w_ffn_gate, v_w_ffn_up, v_w_ffn_down):
    args = dict(locals())
    w = {n: args[n] for n in WEIGHTS}
    mom = {n: args["m_" + n] for n in WEIGHTS}
    var = {n: args["v_" + n] for n in WEIGHTS}

    t, d = x.shape[1], x.shape[2]
    hd = q_norm_g.shape[-1]
    nq = sinks.shape[-1]
    q_w = nq * hd
    d_rnn = conv_b.shape[-1]
    taps = conv_w.shape[1]
    n_blocks, bw = w_rgate.shape[1], w_rgate.shape[2]
    in_w = w_in.shape[-1] * N_CHIPS
    kv_w = (in_w - q_w - 2 * d_rnn - 2 * d) // 2
    kv = kv_w // hd
    grp = nq // kv
    u_off = q_w + 2 * kv_w
    gr_off = u_off + d_rnn
    ga_off = gr_off + d_rnn
    gw = bw * LANE // math.gcd(bw, LANE)
    chip = 2 * lax.axis_index("x") + lax.axis_index("y")
    idx = jnp.stack([lax.axis_index("c"), chip]).astype(jnp.int32)

    x2, tgt = x[0], loss_target[0]

    placed = {n: _cast_into_full(w[n][0], BIG_KIND[n], idx, "cast_" + n) for n in BIG}

    def gather(*names):
        return _AllGather([placed[n] for n in names], [BIG_KIND[n] for n in names])

    mx, my = lax.axis_index("x"), lax.axis_index("y")
    order = jnp.stack([chip, 2 * (1 - mx) + my, 2 * mx + (1 - my), 2 * (1 - mx) + (1 - my)]).astype(jnp.int32)
    conv_w_full = _gather_small(conv_w[0], "allgather_conv_w")
    conv_w_full = jnp.transpose(conv_w_full, (1, 0, 2)).reshape(taps, d_rnn)

    inv_freq = ROPE_THETA ** (-jnp.arange(0, hd // 4, 2, dtype=F32) / (hd // 4))
    ang = positions[0].astype(F32)[:, None] * inv_freq
    cos, sin = jnp.cos(ang), jnp.sin(ang)
    rest = hd - 2 * cos.shape[1]
    cos_t = jnp.concatenate([cos, cos, jnp.ones((t, rest), F32)], axis=1)
    sin_t = jnp.concatenate([-sin, sin, jnp.zeros((t, rest), F32)], axis=1)
    sinks1 = sinks[0]

    xn = _rms_fwd(x2, norm1_g, "rms1_fwd")
    z, win_f = _matmul_gathering(xn, placed["w_in"], order, "in_proj")
    attn, (wap_f, wlp_f, wout_f) = _attn_fwd(z, cos_t, sin_t, q_norm_g, k_norm_g, sinks1, kv, grp, hd, "attn_fwd",
                                             job=gather("w_attn_proj", "w_lru_proj", "w_out"))
    uc = _conv_fwd(z, u_off, d_rnn, conv_w_full, conv_b, "conv_fwd")
    rpre, ipre = _gates_fwd(uc, w_rgate[0], w_igate[0], gw, "gates_fwd")
    (hst, rec), (wg_f,) = _lru_fwd(uc, rpre, ipre, z, gr_off, b_rgate, b_igate, lru_lambda, "lru_fwd",
                                   job=gather("w_ffn_gate"))
    pa = _matmul(attn, wap_f, "nn", "attn_proj")
    plru, merged = _matmul(rec, wlp_f, "nn", "lru_proj", fused=(
        [pa, (z, ga_off), (z, ga_off + d), (b_gates, 0), (b_gates, d)], _merge_after_lru_proj, (F32, BF16)))
    h1 = _matmul(merged, wout_f, "nn", "out_proj", add=x2)
    hn = _rms_fwd(h1, norm2_g, "rms2_fwd")
    gate, (wu_f,) = _matmul(hn, wg_f, "nn", "ffn_gate", job=gather("w_ffn_up"))
    (up, act), (wd_f,) = _matmul(hn, wu_f, "nn", "ffn_up", job=gather("w_ffn_down"),
                                 fused=([gate], _swiglu_after_up, (F32, BF16)))
    yout = _matmul(act, wd_f, "nn", "ffn_down", add=h1)
    dy, dy16, loss_part = _loss_head(yout, tgt, "loss_head")
    loss = lax.psum(loss_part[0, 0], ("x", "y", "c"))

    def reduction(n, g):
        return _Reduce(n, g, BIG_KIND[n], idx, BF16, False)

    r_wd = reduction("w_ffn_down", _matmul(act, dy16, "tn", "d_w_ffn_down"))
    (dgate, dup), got = _matmul(dy16, wd_f, "nt", "d_act", job=r_wd.sibling(),
                                fused=([gate, up], _swiglu_bwd_after_dact, (BF16, BF16)))
    r_wd.got_sibling(got)
    r_wg = reduction("w_ffn_gate", _matmul(hn, dgate, "tn", "d_w_ffn_gate"))
    g_wu, got = _matmul(hn, dup, "tn", "d_w_ffn_up", job=r_wg.sibling())
    r_wg.got_sibling(got)
    r_wu = reduction("w_ffn_up", g_wu)
    dhn, got = _matmul(dgate, wg_f, "nt", "d_hn_gate", job=r_wu.sibling())
    r_wu.got_sibling(got)
    dhn = _matmul(dup, wu_f, "nt", "d_hn_up", add=dhn)
    dh1, g_norm2, dh1_16 = _rms_bwd(dhn, h1, norm2_g, dy, "rms2_bwd", mxu_copy=True)
    r_wout = reduction("w_out", _matmul(merged, dh1_16, "tn", "d_w_out"))
    dmerged, got = _matmul(dh1_16, wout_f, "nt", "d_merged", job=r_wout.sibling())
    r_wout.got_sibling(got)
    (dpa, dpl, dga, dgl, g_ba, g_bl), got = _merge_bwd(dmerged, z, b_gates, pa, plru, ga_off, "merge_bwd",
                                                       job=r_wout.chips())
    r_wout.got_chips(got)
    r_wap = reduction("w_attn_proj", _matmul(attn, dpa, "tn", "d_w_attn_proj"))
    dattn, got = _matmul(dpa, wap_f, "nt", "d_attn", job=r_wap.sibling())
    r_wap.got_sibling(got)
    g_wlp, got = _matmul(rec, dpl, "tn", "d_w_lru_proj", job=r_wap.chips())
    r_wap.got_chips(got)
    r_wlp = reduction("w_lru_proj", g_wlp)
    drec, got = _matmul(dpl, wlp_f, "nt", "d_rec", job=r_wlp.sibling())
    r_wlp.got_sibling(got)
    both = _Jobs(r_wlp.chips(), r_wg.chips())
    (dgr, drp, dip, duc_direct, g_lam, g_br, g_bi), got = _lru_bwd(
        drec, hst, uc, rpre, ipre, z, gr_off, b_rgate, b_igate, lru_lambda, "lru_bwd", job=both)
    got_wlp, got_wg = both.split(got)
    r_wlp.got_chips(got_wlp)
    r_wg.got_chips(got_wg)
    duc = _gates_bwd_x(duc_direct, drp, dip, w_rgate[0], w_igate[0], gw, "gates_bwd_x")
    g_wr = _gates_bwd_w(uc, drp, n_blocks, bw, gw, "gates_bwd_wr")
    g_wi = _gates_bwd_w(uc, dip, n_blocks, bw, gw, "gates_bwd_wi")
    du, g_convw, g_convb = _conv_bwd(duc, z, u_off, conv_w_full, "conv_bwd")
    (dq, dk, dv, g_qg, g_kg, g_sinks), got = _attn_bwd(dattn, z, cos_t, sin_t, q_norm_g, k_norm_g, sinks1, kv, grp, hd,
                                                        "attn_bwd", job=_Jobs(r_wu.chips(), r_wd.chips()))
    r_wu.got_chips(got[:1])
    r_wd.got_chips(got[1:])
    dz = jnp.concatenate([dq, dk, dv, du, dgr, dga, dgl], axis=1)
    r_wr = _Reduce("w_rgate", g_wr.reshape(n_blocks * bw, bw), "row", idx, F32, True)
    r_wi = _Reduce("w_igate", g_wi.reshape(n_blocks * bw, bw), "row", idx, F32, True)
    early = [r_wap, r_wlp, r_wout, r_wg, r_wu, r_wd]
    three = _Jobs(r_wr.sibling(), r_wi.sibling(), _FinishExchange([r.total for r in early], [False] * len(early)))
    g_top, got = _matmul(xn, dz, "tn", "d_w_in_top", m_window=(0, d // 2), job=three)
    got_wr, got_wi, finished = three.split(got)
    r_wr.got_sibling(got_wr)
    r_wi.got_sibling(got_wi)
    r_top = _Reduce("w_in_top", g_top, "col", idx, BF16, False)
    three = _Jobs(r_top.sibling(), r_wr.chips(), r_wi.chips())
    g_bot, got = _matmul(xn, dz, "tn", "d_w_in_bot", m_window=(d // 2, d // 2), job=three)
    got_top, got_wr, got_wi = three.split(got)
    r_top.got_sibling(got_top)
    r_wr.got_chips(got_wr)
    r_wi.got_chips(got_wi)
    r_bot = _Reduce("w_in_bot", g_bot, "col", idx, BF16, False)
    both = _Jobs(r_top.chips(), r_bot.sibling())
    dxn, got = _matmul(dz, win_f, "nt", "d_xn_a", m_window=(0, t // 2), into=(None, t), job=both)
    got_top, got_bot = both.split(got)
    r_top.got_chips(got_top)
    r_bot.got_sibling(got_bot)
    dxn, got = _matmul(dz, win_f, "nt", "d_xn_b", m_window=(t // 2, t // 2), into=(dxn, t), job=r_bot.chips())
    r_bot.got_chips(got)
    dx, g_norm1 = _rms_bwd(dxn, x2, norm1_g, dh1, "rms1_bwd")

    small_grads = {"norm1_g": g_norm1, "b_gates": jnp.concatenate([g_ba, g_bl], axis=1), "q_norm_g": g_qg,
                   "k_norm_g": g_kg, "sinks": g_sinks[:, :nq], "conv_w": g_convw, "conv_b": g_convb,
                   "b_rgate": g_br, "b_igate": g_bi, "lru_lambda": g_lam, "norm2_g": g_norm2}
    gshapes = [small_grads[n].shape for n in PACKED]
    small_sum = _allreduce_small(_pack([small_grads[n] for n in PACKED], _pack_rows(gshapes)), "allreduce_small")
    top, bot, grads_wr, grads_wi = _run_exchange(
        _FinishExchange([r.total for r in (r_top, r_bot, r_wr, r_wi)], [False, False, True, True]),
        "grad_finish_exchange")
    grads = dict(zip(BIG[1:], finished))
    grads["w_in"] = jnp.concatenate([top, bot], axis=0)
    grads["w_rgate"], grads["w_igate"] = grads_wr, grads_wi
    small_full = dict(zip(PACKED, _unpack(small_sum, gshapes)))
    per = d_rnn // N_CHIPS
    small_full["conv_w"] = lax.dynamic_slice(small_full["conv_w"], (0, chip * per), (taps, per))
    grads.update(small_full)

    delta, new_m, new_v = {}, {}, {}
    for n in BIG + ("w_rgate", "w_igate"):
        as2d = (lambda a: a[0]) if n in BIG else (lambda a: a.reshape(n_blocks * bw, bw))
        if n == "w_in":
            delta[n], new_m[n], new_v[n] = _adamw(as2d(w[n]), grads[n], as2d(mom[n]), as2d(var[n]), "adamw_" + n)
        else:
            delta[n], new_m[n], new_v[n], grads[n] = _adamw(as2d(w[n]), grads[n], as2d(mom[n]), as2d(var[n]),
                                                            "adamw_" + n, pass_grad=True)
    pshapes = [w[n].shape for n in PACKED]
    prows = _pack_rows(pshapes)
    pk = [_pack([src[n] for n in PACKED], prows) for src in (w, grads, mom, var)]
    for res, packed in zip((delta, new_m, new_v), _adamw(pk[0], pk[1], pk[2], pk[3], "adamw_small")):
        res.update(dict(zip(PACKED, _unpack(packed, pshapes))))

    outs = [loss, dx.reshape(x.shape)]
    for res in (grads, delta, new_m, new_v):
        outs += [res[n].reshape(w[n].shape) for n in WEIGHTS]
    return tuple(outs)


def _allreduce_small(x, name):
    n_dev = 2 * N_CHIPS
    rel = [(fx, fy, fc) for fx in (0, 1) for fy in (0, 1) for fc in (0, 1)][1:]

    def body(x_ref, all_ref, o_ref, send_sems, recv_sems):
        e = _Env((x_ref,), (all_ref,), send_sems, recv_sems)
        mine = 2 * e.me + e.c

        def peer(r):
            fx, fy, fc = rel[r]
            return (1 - e.x if fx else e.x), (1 - e.y if fy else e.y), (1 - e.c if fc else e.c)

        all_ref[mine] = x_ref[...]
        for r in range(len(rel)):
            e.copy(x_ref, all_ref.at[mine], r, peer(r)).start()
        for r in range(len(rel)):
            px, py, pc = peer(r)
            e.copy(x_ref, all_ref.at[2 * (2 * px + py) + pc], r, peer(r)).wait_recv()
        total = all_ref[0]
        for dev in range(1, n_dev):
            total = total + all_ref[dev]
        o_ref[...] = total
        for r in range(len(rel)):
            e.copy(x_ref, all_ref.at[mine], r, peer(r)).wait_send()

    vm = pl.BlockSpec(memory_space=pltpu.VMEM)
    return pl.pallas_call(
        body, name=name, out_shape=(jax.ShapeDtypeStruct((n_dev,) + x.shape, x.dtype), jax.ShapeDtypeStruct(x.shape, x.dtype)),
        in_specs=[vm], out_specs=(vm, vm),
        scratch_shapes=[pltpu.SemaphoreType.DMA((len(rel),)), pltpu.SemaphoreType.DMA((len(rel),))])(x)[1]


def _gather_small(shard, name):
    def body(s_ref, o_ref, send_sems, recv_sems):
        e = _Env((s_ref,), (o_ref,), send_sems, recv_sems)
        o_ref[e.me] = s_ref[...]
        for k, (cx, cy) in enumerate(e.chips):
            e.copy(s_ref, o_ref.at[e.me], k, (cx, cy, e.c)).start()
        for k, (cx, cy) in enumerate(e.chips):
            e.copy(s_ref, o_ref.at[2 * cx + cy], k, (cx, cy, e.c)).wait_recv()
        for k, (cx, cy) in enumerate(e.chips):
            e.copy(s_ref, o_ref.at[e.me], k, (cx, cy, e.c)).wait_send()

    vm = pl.BlockSpec(memory_space=pltpu.VMEM)
    return pl.pallas_call(body, name=name, out_shape=jax.ShapeDtypeStruct((N_CHIPS,) + shard.shape, shard.dtype),
                          in_specs=[vm], out_specs=vm,
                          scratch_shapes=[pltpu.SemaphoreType.DMA((3,)), pltpu.SemaphoreType.DMA((3,))])(shard)
```

```python
import functools
import math

import jax
import jax.numpy as jnp
from jax import lax
from jax.experimental import pallas as pl
from jax.experimental.pallas import tpu as pltpu

F32 = jnp.float32
BF16 = jnp.bfloat16
MESH = pl.DeviceIdType.MESH

WINDOW = 128
BLK = 128
ROPE_THETA = 500000.0
LRU_C = 8.0
EPS = 1e-6
NEG = -1e30
ADAM_LR = 0.001
ADAM_B1 = 0.9
ADAM_B2 = 0.999
ADAM_EPS = 1e-08
ADAM_WD = 0.01
ADAM_STEP = 10

VMEM_LIMIT_BYTES = 52 * 1024 * 1024
LANE = 128
SUBLANE = 8
N_CHIPS = 4
SMALL_PACK_COLS = 512


def _params(**kw):
    return pltpu.CompilerParams(vmem_limit_bytes=VMEM_LIMIT_BYTES, **kw)


def _pick(dim, cands):
    for c in cands:
        if dim % c == 0:
            return c
    return dim


def _sigmoid(x):
    return 0.5 * jnp.tanh(0.5 * x) + 0.5


ANY = pl.BlockSpec(memory_space=pl.ANY)


class _Env:
    def __init__(self, ins, outs, send, recv, sem0=0, place=None):
        self.ins, self.outs, self.send, self.recv, self.sem0 = ins, outs, send, recv, sem0
        self.x, self.y, self.c = place or (lax.axis_index("x"), lax.axis_index("y"), lax.axis_index("c"))
        self.me = 2 * self.x + self.y
        self.chips = [(1 - self.x, self.y), (self.x, 1 - self.y), (1 - self.x, 1 - self.y)]
        self.sibling = (self.x, self.y, 1 - self.c)

    def sub(self, i0, n_in, o0, n_out, sem0):
        return _Env(self.ins[i0:i0 + n_in], self.outs[o0:o0 + n_out], self.send, self.recv, self.sem0 + sem0,
                    (self.x, self.y, self.c))

    def copy(self, src, dst, sem, to):
        return pltpu.make_async_remote_copy(src_ref=src, dst_ref=dst, send_sem=self.send.at[self.sem0 + sem],
                                            recv_sem=self.recv.at[self.sem0 + sem], device_id=to, device_id_type=MESH)


class _Exchange:
    inputs, out_shapes, aliases, n_sems = (), (), {}, 0

    def start(self, e):
        raise NotImplementedError

    def finish(self, e):
        raise NotImplementedError


class _Jobs(_Exchange):
    def __init__(self, *jobs):
        self.jobs, self.inputs, self.out_shapes, self.aliases, self.n_sems, self.at = jobs, [], [], {}, 0, []
        for job in jobs:
            self.at.append((len(self.inputs), len(self.out_shapes), self.n_sems))
            self.aliases.update({len(self.inputs) + i: len(self.out_shapes) + o for i, o in job.aliases.items()})
            self.inputs += list(job.inputs)
            self.out_shapes += list(job.out_shapes)
            self.n_sems += job.n_sems

    def _each(self, e):
        for job, (i0, o0, s0) in zip(self.jobs, self.at):
            yield job, e.sub(i0, len(job.inputs), o0, len(job.out_shapes), s0)

    def split(self, outs):
        return [tuple(outs[o0:o0 + len(job.out_shapes)]) for job, (_, o0, _) in zip(self.jobs, self.at)]

    def start(self, e):
        for job, se in self._each(e):
            job.start(se)

    def finish(self, e):
        for job, se in self._each(e):
            job.finish(se)


def _call(body, name, out_shape, grid, in_specs, out_specs, args, scratch_shapes=(), job=None, aliases=None):
    aliases = dict(aliases or {})
    if job is None:
        return pl.pallas_call(body, name=name, out_shape=out_shape, grid=grid, in_specs=list(in_specs),
                              out_specs=out_specs, scratch_shapes=list(scratch_shapes), input_output_aliases=aliases,
                              compiler_params=_params())(*args), ()
    single = not isinstance(out_shape, (tuple, list))
    shapes = [out_shape] if single else list(out_shape)
    ospecs = [out_specs] if single else list(out_specs)
    n_in, n_out, n_scr = len(args), len(shapes), len(scratch_shapes)
    j_in, j_out = len(job.inputs), len(job.out_shapes)

    def hosted(*refs):
        ins, jins = refs[:n_in], refs[n_in:n_in + j_in]
        outs = refs[n_in + j_in:n_in + j_in + n_out]
        jouts = refs[n_in + j_in + n_out:n_in + j_in + n_out + j_out]
        rest = refs[n_in + j_in + n_out + j_out:]
        e = _Env(jins, jouts, rest[n_scr], rest[n_scr + 1])
        first = functools.reduce(jnp.logical_and, [pl.program_id(d) == 0 for d in range(len(grid))])
        last = functools.reduce(jnp.logical_and, [pl.program_id(d) == g - 1 for d, g in enumerate(grid)])

        @pl.when(first)
        def _():
            job.start(e)

        body(*ins, *outs, *rest[:n_scr])

        @pl.when(last)
        def _():
            job.finish(e)

    res = pl.pallas_call(
        hosted, name=name, out_shape=tuple(shapes + list(job.out_shapes)), grid=grid,
        in_specs=list(in_specs) + [ANY] * j_in, out_specs=tuple(ospecs + [ANY] * j_out),
        scratch_shapes=list(scratch_shapes) + [pltpu.SemaphoreType.DMA((job.n_sems,)),
                                               pltpu.SemaphoreType.DMA((job.n_sems,))],
        input_output_aliases={**aliases, **{n_in + i: n_out + o for i, o in job.aliases.items()}},
        compiler_params=_params())(*args, *job.inputs)
    return (res[0] if single else tuple(res[:n_out])), tuple(res[n_out:])


def _run_exchange(job, name):
    n_in, n_out = len(job.inputs), len(job.out_shapes)

    def body(*refs):
        e = _Env(refs[:n_in], refs[n_in:n_in + n_out], refs[n_in + n_out], refs[n_in + n_out + 1])
        job.start(e)
        job.finish(e)

    return pl.pallas_call(
        body, name=name, out_shape=tuple(job.out_shapes), in_specs=[ANY] * n_in, out_specs=tuple([ANY] * n_out),
        input_output_aliases=dict(job.aliases),
        scratch_shapes=[pltpu.SemaphoreType.DMA((job.n_sems,)), pltpu.SemaphoreType.DMA((job.n_sems,))],
    )(*job.inputs)


_M_TILES = (1024, 1408, 1280, 512, 256, 128)
_N_TILES = (1408, 1280, 1024, 640, 512, 256, 128)
MXU_FULL_ROWS = 1024
MATMUL_VMEM_BUDGET = 42 * 1024 * 1024
MXU_FLOPS_PER_HBM_BYTE = 500


def _matmul_tiles(m, n, k, sa, sb, so, has_add, tn_divides=0):
    best = None
    for tm in [c for c in _M_TILES if m % c == 0] or [m]:
        for tn in [c for c in _N_TILES if n % c == 0 and tn_divides % c == 0] or [n]:
            for nk in range(1, 17):
                tk = k // nk
                if k % nk or tk % LANE:
                    continue
                need = 2 * (tm * tk * sa + tk * tn * sb) + 2 * tm * tn * (so + (4 if has_add else 0))
                need += tm * tn * 4 if nk > 1 else 0
                fetched = tk * tn * sb + tm * tk * sa // (1 if nk > 1 else n // tn)
                if need > MATMUL_VMEM_BUDGET:
                    continue
                mxu_bound = fetched * MXU_FLOPS_PER_HBM_BYTE <= 2 * tm * tn * tk
                key = (mxu_bound, min(tm, MXU_FULL_ROWS), -nk, tn, tm)
                if best is None or key > best[0]:
                    best = (key, (tm, tn, tk))
    assert best is not None, (m, n, k)
    return best[1]


def _matmul(a, b, mode, name, add=None, out_dtype=F32, job=None, m_window=None, into=None, fused=None):
    if mode == "nn":
        (m, k), (k2, n) = a.shape, b.shape
    elif mode == "nt":
        (m, k), (n, k2) = a.shape, b.shape
    else:
        (k, m), (k2, n) = a.shape, b.shape
    assert k == k2, (a.shape, b.shape, mode)
    m0, m = m_window or (0, m)
    tiles, fuse_fn, out_dtypes = fused or ((), None, (out_dtype,))
    tiles = [x if isinstance(x, tuple) else (x, 0) for x in tiles]
    tm, tn, tk = _matmul_tiles(math.gcd(m, m0) if m0 else m, n, k, a.dtype.itemsize, b.dtype.itemsize,
                               sum(jnp.dtype(dt).itemsize for dt in out_dtypes)
                               + sum(x.dtype.itemsize for x, _ in tiles if x.shape[0] > 1),
                               add is not None, math.gcd(*[c0 for _, c0 in tiles], 0))
    nk, mb0 = k // tk, m0 // tm
    if mode == "nn":
        a_spec = pl.BlockSpec((tm, tk), lambda i, j, kk: (mb0 + i, kk))
        b_spec = pl.BlockSpec((tk, tn), lambda i, j, kk: (kk, j))
        dims = (((1,), (0,)), ((), ()))
    elif mode == "nt":
        a_spec = pl.BlockSpec((tm, tk), lambda i, j, kk: (mb0 + i, kk))
        b_spec = pl.BlockSpec((tn, tk), lambda i, j, kk: (j, kk))
        dims = (((1,), (1,)), ((), ()))
    else:
        a_spec = pl.BlockSpec((tk, tm), lambda i, j, kk: (kk, mb0 + i))
        b_spec = pl.BlockSpec((tk, tn), lambda i, j, kk: (kk, j))
        dims = (((0,), (0,)), ((), ()))
    out_rows, ob0 = (into[1], mb0) if into is not None else (m, 0)
    o_spec = pl.BlockSpec((tm, tn), lambda i, j, kk: (ob0 + i, j))
    has_add = add is not None
    begun = into is not None and into[0] is not None

    n_side, n_out = has_add + len(tiles), len(out_dtypes)

    def body(*refs):
        a_ref, b_ref = refs[:2]
        side = refs[2:2 + n_side]
        o_refs = refs[len(refs) - n_out - (nk > 1):len(refs) - (nk > 1)]
        part = lax.dot_general(a_ref[...].astype(BF16), b_ref[...].astype(BF16), dims, preferred_element_type=F32)

        def finish(r):
            if has_add:
                r = r + side[0][...]
            vals = fuse_fn(r, *[x[...] for x in side[has_add:]]) if fuse_fn else (r,)
            for o_ref, val, dt in zip(o_refs, vals, out_dtypes):
                o_ref[...] = val.astype(dt)

        if nk == 1:
            finish(part)
            return
        acc = refs[-1]
        kk = pl.program_id(2)

        @pl.when(kk == 0)
        def _():
            acc[...] = part

        @pl.when(kk > 0)
        def _():
            acc[...] += part

        @pl.when(kk == nk - 1)
        def _():
            finish(acc[...])

    def side_spec(x, c0):
        if x.shape[0] == 1:
            return pl.BlockSpec((1, tn), lambda i, j, kk: (0, c0 // tn + j))
        return pl.BlockSpec((tm, tn), lambda i, j, kk: (mb0 + i, c0 // tn + j))

    in_specs = [a_spec, b_spec] + ([side_spec(add, 0)] if has_add else []) + [side_spec(x, c0) for x, c0 in tiles]
    args = (a, b) + ((add,) if has_add else ()) + tuple(x for x, _ in tiles)
    aliases = None
    if begun:
        aliases = {len(args): 0}
        in_specs, args = in_specs + [ANY], args + (into[0],)
    shapes = tuple(jax.ShapeDtypeStruct((out_rows, n), dt) for dt in out_dtypes)
    res, extra = _call(body, name, shapes if fused else shapes[0], (m // tm, n // tn, nk), in_specs,
                       (o_spec,) * n_out if fused else o_spec, args, [pltpu.VMEM((tm, tn), F32)] if nk > 1 else [],
                       job, aliases)
    return res if job is None else (res, extra)


def _row_tile(rows, cols, budget_elems=512 * 1024):
    cands = [c for c in (1024, 704, 512, 352, 256, 128, 64, 32, 16) if c * cols <= budget_elems]
    return _pick(rows, cands or (16,))


_EW_COLS = (1280, 1408, 1024, 640, 512, 256, 128)


def _tile2d(rows, cols, max_elems):
    tc = _pick(cols, _EW_COLS)
    return _row_tile(rows, tc, max_elems), tc


def _rms_fwd(x, g, name):
    t, d = x.shape
    tr = _row_tile(t, d)

    def body(x_ref, g_ref, o_ref):
        xv = x_ref[...]
        rstd = lax.rsqrt(jnp.mean(xv * xv, axis=-1, keepdims=True) + EPS)
        o_ref[...] = (xv * rstd * g_ref[...]).astype(BF16)

    spec = pl.BlockSpec((tr, d), lambda i: (i, 0))
    return pl.pallas_call(body, name=name, out_shape=jax.ShapeDtypeStruct((t, d), BF16), grid=(t // tr,),
                          in_specs=[spec, pl.BlockSpec((1, d), lambda i: (0, 0))], out_specs=spec,
                          compiler_params=_params())(x, g)


def _rms_bwd(dxn, x, g, resid, name, job=None, mxu_copy=False):
    t, d = x.shape
    tr = _row_tile(t, d, 256 * 1024)

    def body(dxn_ref, x_ref, g_ref, r_ref, dx_ref, dg_ref, *dx16_ref):
        @pl.when(pl.program_id(0) == 0)
        def _():
            dg_ref[...] = jnp.zeros_like(dg_ref)

        xv = x_ref[...]
        rstd = lax.rsqrt(jnp.mean(xv * xv, axis=-1, keepdims=True) + EPS)
        xhat = xv * rstd
        dy = dxn_ref[...]
        dg_ref[...] += jnp.sum(dy * xhat, axis=0, keepdims=True)
        dxhat = dy * g_ref[...]
        dx = r_ref[...] + rstd * (dxhat - xhat * jnp.mean(dxhat * xhat, axis=-1, keepdims=True))
        dx_ref[...] = dx
        if mxu_copy:
            dx16_ref[0][...] = dx.astype(BF16)

    spec = pl.BlockSpec((tr, d), lambda i: (i, 0))
    vec = pl.BlockSpec((1, d), lambda i: (0, 0))
    shapes = (jax.ShapeDtypeStruct((t, d), F32), jax.ShapeDtypeStruct((1, d), F32))
    shapes += (jax.ShapeDtypeStruct((t, d), BF16),) if mxu_copy else ()
    res, extra = _call(body, name, shapes, (t // tr,), [spec, spec, vec, spec],
                       (spec, vec) + ((spec,) if mxu_copy else ()), (dxn, x, g, resid), (), job)
    return res if job is None else (res, extra)


def _swiglu_after_up(up, gate):
    return up, gate * _sigmoid(gate) * up


def _swiglu_bwd_after_dact(dact, gate, up):
    sg = _sigmoid(gate)
    return dact * up * (sg * (1.0 + gate * (1.0 - sg))), dact * (gate * sg)


def _merge_after_lru_proj(plru, pa, ga, gl, ba, bl):
    return plru, _sigmoid(ga + ba) * pa + _sigmoid(gl + bl) * plru


def _merge_bwd(dmerged, z, b_gates, p, z0, b0, dz, name, job=None):
    t, d = p.shape
    cw = _pick(math.gcd(z0, d), (512, 256, 128))
    tr = _row_tile(t, cw, 256 * 1024)
    oz, ob, nd = z0 // cw, b0 // cw, d // cw

    def body(dm_ref, g_ref, b_ref, p_ref, *rest):
        dp_ref, dg_ref, sum_ref = rest[-3:]

        @pl.when(pl.program_id(1) == 0)
        def _():
            sum_ref[...] = jnp.zeros_like(sum_ref)

        dm = dm_ref[...]
        sg = _sigmoid(g_ref[...] + b_ref[...])
        dp_ref[...] = (dm * sg).astype(BF16)
        dg = dm * p_ref[...] * (sg * (1.0 - sg))
        dg_ref[...] = dg.astype(BF16)
        sum_ref[...] += jnp.sum(dg, axis=0, keepdims=True)

    blk = pl.BlockSpec((tr, cw), lambda j, i: (i, j))
    at_z = pl.BlockSpec((tr, cw), lambda j, i: (i, oz + j))
    in_specs = [blk, at_z, pl.BlockSpec((1, cw), lambda j, i: (0, ob + j)), blk]
    args, aliases = (dmerged, z, b_gates, p), None
    if dz is not None:
        in_specs, args, aliases = in_specs + [ANY], args + (dz,), {4: 1}
    res, extra = _call(
        body, name, (jax.ShapeDtypeStruct((t, d), BF16), jax.ShapeDtypeStruct(z.shape, BF16),
                     jax.ShapeDtypeStruct((1, d), F32)), (nd, t // tr), in_specs,
        (blk, at_z, pl.BlockSpec((1, cw), lambda j, i: (0, j))), args, (), job, aliases)
    return res if job is None else (res, extra)


def _loss_head(y, target, name):
    t, d = y.shape
    tr = _row_tile(t, d, 256 * 1024)
    nt = t // tr

    def body(y_ref, t_ref, dy_ref, dy16_ref, loss_ref, acc):
        i = pl.program_id(0)

        @pl.when(i == 0)
        def _():
            acc[...] = jnp.zeros_like(acc)

        e = y_ref[...] - t_ref[...]
        dy = e * (1.0 / d)
        dy_ref[...] = dy
        dy16_ref[...] = dy.astype(BF16)
        acc[...] += jnp.sum(e * e, axis=0, keepdims=True)

        @pl.when(i == nt - 1)
        def _():
            loss_ref[...] = (0.5 / d) * jnp.sum(acc[...], axis=-1, keepdims=True)

    spec = pl.BlockSpec((tr, d), lambda i: (i, 0))
    return pl.pallas_call(
        body, name=name, out_shape=(jax.ShapeDtypeStruct((t, d), F32), jax.ShapeDtypeStruct((t, d), BF16),
                                    jax.ShapeDtypeStruct((1, 1), F32)),
        grid=(nt,), in_specs=[spec, spec], out_specs=(spec, spec, pl.BlockSpec((1, 1), lambda i: (0, 0))),
        scratch_shapes=[pltpu.VMEM((1, d), F32)], compiler_params=_params(),
    )(y, target)


def _adamw(w, g, m, v, name, pass_grad=False):
    r, c = w.shape
    tr, tc = _tile2d(r, c, 512 * 1024)
    c1 = 1.0 - ADAM_B1 ** ADAM_STEP
    c2 = 1.0 - ADAM_B2 ** ADAM_STEP

    def body(w_ref, g_ref, m_ref, v_ref, d_ref, nm_ref, nv_ref, *g_out):
        gv = g_ref[...]
        if pass_grad:
            g_out[0][...] = gv
        mn = ADAM_B1 * m_ref[...] + (1.0 - ADAM_B1) * gv
        vn = ADAM_B2 * v_ref[...] + (1.0 - ADAM_B2) * (gv * gv)
        d_ref[...] = -ADAM_LR * ((mn / c1) / (jnp.sqrt(vn / c2) + ADAM_EPS) + ADAM_WD * w_ref[...])
        nm_ref[...] = mn
        nv_ref[...] = vn

    spec = pl.BlockSpec((tr, tc), lambda i, j: (i, j))
    shp = jax.ShapeDtypeStruct((r, c), F32)
    n_out = 4 if pass_grad else 3
    return pl.pallas_call(body, name=name, out_shape=(shp,) * n_out, grid=(r // tr, c // tc), in_specs=[spec] * 4,
                          out_specs=(spec,) * n_out, compiler_params=_params())(w, g, m, v)


def _swap_halves(v, half):
    n = v.shape[-1]
    lane = lax.broadcasted_iota(jnp.int32, v.shape, 1)
    return jnp.where(lane < half, pltpu.roll(v, n - half, 1),
                     jnp.where(lane < 2 * half, pltpu.roll(v, half, 1), 0.0))


def _rope(y, c, s, half):
    return y * c + _swap_halves(y, half) * s


def _rope_bwd(dout, c, s, half):
    return dout * c + _swap_halves(dout * s, half)


def _stack_heads(ref, grp, hd):
    return jnp.concatenate([ref[:, g * hd:(g + 1) * hd] for g in range(grp)], axis=0)


def _softmax_with_sinks(s, sink_ref, first, grp, i):
    rows = s.shape[0]
    qi = lax.broadcasted_iota(jnp.int32, s.shape, 0) & (BLK - 1)
    kj = lax.broadcasted_iota(jnp.int32, s.shape, 1)
    rel = qi + BLK - kj
    s = jnp.where((rel >= 0) & (rel < WINDOW) & ((kj >= BLK) | (i > 0)), s, NEG)
    head = lax.broadcasted_iota(jnp.int32, (rows, 1), 0) // BLK
    sk = jnp.zeros((rows, 1), F32)
    for g in range(grp):
        sk = jnp.where(head == g, sink_ref[first + g], sk)
    mx = jnp.maximum(jnp.max(s, axis=-1, keepdims=True), sk)
    p = jnp.exp(s - mx)
    esk = jnp.exp(sk - mx)
    inv_den = 1.0 / (jnp.sum(p, axis=-1, keepdims=True) + esk)
    return p * inv_den, esk * inv_den, head


def _norm_fwd(xraw, g):
    rstd = lax.rsqrt(jnp.mean(xraw * xraw, axis=-1, keepdims=True) + EPS)
    xhat = xraw * rstd
    return xhat, rstd, xhat * g


def _norm_bwd(dy, xhat, rstd, g):
    dxhat = dy * g
    dx = rstd * (dxhat - xhat * jnp.mean(dxhat * xhat, axis=-1, keepdims=True))
    return dx, jnp.sum(dy * xhat, axis=0, keepdims=True)


def _attn_specs(nb, grp, hd, kv, clamp):
    qo, ko, vo = 0, (kv * grp), (kv * grp + kv)
    cur = (lambda i: jnp.minimum(i, nb - 1)) if clamp else (lambda i: i)
    prev = lambda i: jnp.maximum(cur(i) - 1, 0)
    zq = pl.BlockSpec((BLK, grp * hd), lambda h, i: (cur(i), h))
    kc = pl.BlockSpec((BLK, hd), lambda h, i: (cur(i), ko + h))
    kp = pl.BlockSpec((BLK, hd), lambda h, i: (prev(i), ko + h))
    vc = pl.BlockSpec((BLK, hd), lambda h, i: (cur(i), vo + h))
    vp = pl.BlockSpec((BLK, hd), lambda h, i: (prev(i), vo + h))
    tc = pl.BlockSpec((BLK, hd), lambda h, i: (cur(i), 0))
    tp = pl.BlockSpec((BLK, hd), lambda h, i: (prev(i), 0))
    gs = pl.BlockSpec((1, hd), lambda h, i: (0, 0))
    return zq, kc, kp, vc, vp, tc, tp, gs


def _attn_fwd(z, cos_t, sin_t, qg, kg, sinks, kv, grp, hd, name, job=None):
    t = z.shape[0]
    nb = t // BLK
    half = hd // 8
    scale = 1.0 / math.sqrt(hd)
    zq, kc, kp, vc, vp, tc, tp, gs = _attn_specs(nb, grp, hd, kv, False)

    def body(sink_ref, zq_ref, kc_ref, kp_ref, vc_ref, vp_ref, cc_ref, sc_ref, cp_ref, sp_ref, qg_ref, kg_ref, o_ref):
        h, i = pl.program_id(0), pl.program_id(1)

        def normrope(xraw, g, c, s):
            return _rope(_norm_fwd(xraw, g)[2], c, s, half)

        cc, sc = cc_ref[...], sc_ref[...]
        kcur = normrope(kc_ref[...], kg_ref[...], cc, sc)
        kprev = normrope(kp_ref[...], kg_ref[...], cp_ref[...], sp_ref[...])
        kk = jnp.concatenate([kprev, kcur], axis=0).astype(BF16)
        vv = jnp.concatenate([vp_ref[...], vc_ref[...]], axis=0).astype(BF16)
        for g in range(grp):
            q = normrope(zq_ref[:, g * hd:(g + 1) * hd], qg_ref[...], cc, sc).astype(BF16)
            s = lax.dot_general(q, kk, (((1,), (1,)), ((), ())), preferred_element_type=F32) * scale
            p, _, _ = _softmax_with_sinks(s, sink_ref, h * grp + g, 1, i)
            o_ref[:, g * hd:(g + 1) * hd] = jnp.dot(p.astype(BF16), vv, preferred_element_type=F32).astype(BF16)

    res, extra = _call(
        body, name, jax.ShapeDtypeStruct((t, kv * grp * hd), BF16), (kv, nb),
        [pl.BlockSpec(memory_space=pltpu.SMEM), zq, kc, kp, vc, vp, tc, tc, tp, tp, gs, gs],
        pl.BlockSpec((BLK, grp * hd), lambda h, i: (i, h)),
        (sinks, z, z, z, z, z, cos_t, sin_t, cos_t, sin_t, qg, kg), (), job)
    return res if job is None else (res, extra)


def _attn_bwd(dattn, z, cos_t, sin_t, qg, kg, sinks, kv, grp, hd, name, job=None):
    t = z.shape[0]
    nb = t // BLK
    half = hd // 8
    scale = 1.0 / math.sqrt(hd)
    zq, kc, kp, vc, vp, tc, tp, gs = _attn_specs(nb, grp, hd, kv, True)

    def body(sink_ref, zq_ref, kc_ref, kp_ref, vc_ref, vp_ref, cc_ref, sc_ref, cp_ref, sp_ref, qg_ref, kg_ref, do_ref,
             dq_ref, dk_ref, dv_ref, dqg_ref, dkg_ref, dsk_ref, dk_carry, dv_carry):
        h, i = pl.program_id(0), pl.program_id(1)
        lane1 = lax.broadcasted_iota(jnp.int32, (1, LANE), 1)

        @pl.when((h == 0) & (i == 0))
        def _():
            dqg_ref[...] = jnp.zeros_like(dqg_ref)
            dkg_ref[...] = jnp.zeros_like(dkg_ref)
            dsk_ref[...] = jnp.zeros_like(dsk_ref)

        @pl.when(i == 0)
        def _():
            dk_carry[...] = jnp.zeros_like(dk_carry)
            dv_carry[...] = jnp.zeros_like(dv_carry)

        @pl.when(i < nb)
        def _():
            cc, sc, cp, sp = cc_ref[...], sc_ref[...], cp_ref[...], sp_ref[...]
            qgv, kgv = qg_ref[...], kg_ref[...]
            xh_kc, rs_kc, y_kc = _norm_fwd(kc_ref[...], kgv)
            xh_kp, rs_kp, y_kp = _norm_fwd(kp_ref[...], kgv)
            kk = jnp.concatenate([_rope(y_kp, cp, sp, half), _rope(y_kc, cc, sc, half)], axis=0).astype(BF16)
            vv = jnp.concatenate([vp_ref[...], vc_ref[...]], axis=0).astype(BF16)
            cq, sq = jnp.concatenate([cc] * grp, axis=0), jnp.concatenate([sc] * grp, axis=0)
            xh_q, rs_q, y_q = _norm_fwd(_stack_heads(zq_ref, grp, hd), qgv)
            q = _rope(y_q, cq, sq, half).astype(BF16)
            s = lax.dot_general(q, kk, (((1,), (1,)), ((), ())), preferred_element_type=F32) * scale
            p, psink, head = _softmax_with_sinks(s, sink_ref, h * grp, grp, i)
            dog = _stack_heads(do_ref, grp, hd).astype(BF16)
            dp = lax.dot_general(dog, vv, (((1,), (1,)), ((), ())), preferred_element_type=F32)
            rsum = jnp.sum(p * dp, axis=-1, keepdims=True)
            ds = (p * (dp - rsum) * scale).astype(BF16)
            dsink = -psink * rsum
            dsk = jnp.zeros((1, LANE), F32)
            for g in range(grp):
                dsk = dsk + jnp.where(lane1 == h * grp + g,
                                      jnp.sum(jnp.where(head == g, dsink, 0.0), axis=0, keepdims=True), 0.0)
            dqn = jnp.dot(ds, kk, preferred_element_type=F32)
            dkk = lax.dot_general(ds, q, (((0,), (0,)), ((), ())), preferred_element_type=F32)
            dvv = lax.dot_general(p.astype(BF16), dog, (((0,), (0,)), ((), ())), preferred_element_type=F32)
            dxq, dqg = _norm_bwd(_rope_bwd(dqn, cq, sq, half), xh_q, rs_q, qgv)
            dxq = dxq.astype(BF16)
            for g in range(grp):
                dq_ref[:, g * hd:(g + 1) * hd] = dxq[g * BLK:(g + 1) * BLK]
            dkp_raw, dg_kp = _norm_bwd(_rope_bwd(dkk[:BLK], cp, sp, half), xh_kp, rs_kp, kgv)
            dkc_raw, dg_kc = _norm_bwd(_rope_bwd(dkk[BLK:], cc, sc, half), xh_kc, rs_kc, kgv)
            dk_ref[...] = (dk_carry[...] + dkp_raw).astype(BF16)
            dv_ref[...] = (dv_carry[...] + dvv[:BLK]).astype(BF16)
            dk_carry[...] = dkc_raw
            dv_carry[...] = dvv[BLK:]
            dqg_ref[...] += dqg
            dkg_ref[...] += dg_kp + dg_kc
            dsk_ref[...] += dsk

        @pl.when(i == nb)
        def _():
            dk_ref[...] = dk_carry[...].astype(BF16)
            dv_ref[...] = dv_carry[...].astype(BF16)

    kvw = kv * hd
    vec = pl.BlockSpec((1, hd), lambda h, i: (0, 0))
    shifted = pl.BlockSpec((BLK, hd), lambda h, i: (jnp.maximum(i - 1, 0), h))
    res, extra = _call(
        body, name,
        (jax.ShapeDtypeStruct((t, kv * grp * hd), BF16), jax.ShapeDtypeStruct((t, kvw), BF16),
         jax.ShapeDtypeStruct((t, kvw), BF16), jax.ShapeDtypeStruct((1, hd), F32),
         jax.ShapeDtypeStruct((1, hd), F32), jax.ShapeDtypeStruct((1, LANE), F32)),
        (kv, nb + 1),
        [pl.BlockSpec(memory_space=pltpu.SMEM), zq, kc, kp, vc, vp, tc, tc, tp, tp, gs, gs,
         pl.BlockSpec((BLK, grp * hd), lambda h, i: (jnp.minimum(i, nb - 1), h))],
        (pl.BlockSpec((BLK, grp * hd), lambda h, i: (jnp.minimum(i, nb - 1), h)), shifted, shifted, vec, vec,
         pl.BlockSpec((1, LANE), lambda h, i: (0, 0))),
        (sinks, z, z, z, z, z, cos_t, sin_t, cos_t, sin_t, qg, kg, dattn),
        [pltpu.VMEM((BLK, hd), F32), pltpu.VMEM((BLK, hd), F32)], job)
    return res if job is None else (res, extra)


def _window(rows, cb, c0, row_of, col_of):
    assert rows % SUBLANE == 0 and cb % LANE == 0 and c0 % LANE == 0, (rows, cb, c0)
    return pl.BlockSpec((pl.Element(rows), pl.Element(cb)),
                        lambda *g: (pl.multiple_of(row_of(*g) * rows, SUBLANE), pl.multiple_of(c0 + col_of(*g) * cb, LANE)))


def _conv_fwd(z, c0, c, w, b, name):
    t = z.shape[0]
    taps = w.shape[0]
    cb = _pick(c, (1408, 1024, 512, 256, 128))
    tr = _row_tile(t, cb, 256 * 1024)
    hb = tr // SUBLANE

    def body(u_ref, halo_ref, w_ref, b_ref, o_ref):
        i = pl.program_id(0)
        x = u_ref[...]
        acc = b_ref[...] + w_ref[taps - 1:taps, :] * x
        for k in range(taps - 1):
            acc = acc + w_ref[k:k + 1, :] * pltpu.roll(x, taps - 1 - k, 0)
        o_ref[...] = acc
        row = lax.broadcasted_iota(jnp.int32, (SUBLANE, cb), 0)
        hp = jnp.where(i > 0, halo_ref[...], 0.0)
        x8 = u_ref[0:SUBLANE, :]
        acc8 = b_ref[...] + w_ref[taps - 1:taps, :] * x8
        for k in range(taps - 1):
            s = taps - 1 - k
            acc8 = acc8 + w_ref[k:k + 1, :] * jnp.where(row < s, pltpu.roll(hp, s, 0), pltpu.roll(x8, s, 0))
        o_ref[0:SUBLANE, :] = acc8

    blk = pl.BlockSpec((tr, cb), lambda i, j: (i, j))
    return pl.pallas_call(
        body, name=name, out_shape=jax.ShapeDtypeStruct((t, c), F32), grid=(t // tr, c // cb),
        in_specs=[_window(tr, cb, c0, lambda i, j: i, lambda i, j: j),
                  _window(SUBLANE, cb, c0, lambda i, j: jnp.maximum(i * hb - 1, 0), lambda i, j: j),
                  pl.BlockSpec((taps, cb), lambda i, j: (0, j)), pl.BlockSpec((1, cb), lambda i, j: (0, j))],
        out_specs=blk, compiler_params=_params(),
    )(z, z, w, b)


def _conv_bwd(duc, z, c0, w, dz, name):
    t, c = duc.shape
    taps = w.shape[0]
    cb = _pick(c, (1408, 1024, 512, 256, 128))
    tr = _row_tile(t, cb, 256 * 1024)
    hb, nt = tr // SUBLANE, t // tr

    def body(g_ref, gnext_ref, u_ref, uprev_ref, w_ref, dz_ref, du16_ref, dw_ref, db_ref, du_ref):
        i = pl.program_id(1)

        @pl.when(i == 0)
        def _():
            dw_ref[...] = jnp.zeros_like(dw_ref)
            db_ref[...] = jnp.zeros_like(db_ref)

        row = lax.broadcasted_iota(jnp.int32, (SUBLANE, cb), 0)
        g, x = g_ref[...], u_ref[...]
        du = w_ref[taps - 1:taps, :] * g
        for k in range(taps - 1):
            du = du + w_ref[k:k + 1, :] * pltpu.roll(g, tr - (taps - 1 - k), 0)
        du_ref[...] = du
        hn = jnp.where(i < nt - 1, gnext_ref[...], 0.0)
        g8 = g_ref[tr - SUBLANE:tr, :]
        du8 = w_ref[taps - 1:taps, :] * g8
        for k in range(taps - 1):
            s = taps - 1 - k
            du8 = du8 + w_ref[k:k + 1, :] * jnp.where(row >= SUBLANE - s, pltpu.roll(hn, SUBLANE - s, 0),
                                                     pltpu.roll(g8, SUBLANE - s, 0))
        du_ref[tr - SUBLANE:tr, :] = du8
        du16_ref[...] = du_ref[...].astype(BF16)

        hp = jnp.where(i > 0, uprev_ref[...], 0.0)
        xl8, gf8 = u_ref[tr - SUBLANE:tr, :], g_ref[0:SUBLANE, :]
        db_ref[...] += jnp.sum(g, axis=0, keepdims=True)
        dw_ref[taps - 1:taps, :] += jnp.sum(g * x, axis=0, keepdims=True)
        for k in range(taps - 1):
            s = taps - 1 - k
            fix = jnp.where(row < s, pltpu.roll(hp, s, 0) - pltpu.roll(xl8, s, 0), 0.0)
            dw_ref[k:k + 1, :] += (jnp.sum(g * pltpu.roll(x, s, 0), axis=0, keepdims=True)
                                   + jnp.sum(gf8 * fix, axis=0, keepdims=True))

    blk = pl.BlockSpec((tr, cb), lambda j, i: (i, j))
    nh = t // SUBLANE
    return pl.pallas_call(
        body, name=name,
        out_shape=(jax.ShapeDtypeStruct(dz.shape, BF16), jax.ShapeDtypeStruct((taps, c), F32),
                   jax.ShapeDtypeStruct((1, c), F32)),
        grid=(c // cb, nt),
        in_specs=[blk, pl.BlockSpec((SUBLANE, cb), lambda j, i: (jnp.minimum((i + 1) * hb, nh - 1), j)),
                  _window(tr, cb, c0, lambda j, i: i, lambda j, i: j),
                  _window(SUBLANE, cb, c0, lambda j, i: jnp.maximum(i * hb - 1, 0), lambda j, i: j),
                  pl.BlockSpec((taps, cb), lambda j, i: (0, j)), ANY],
        out_specs=(_window(tr, cb, c0, lambda j, i: i, lambda j, i: j), pl.BlockSpec((taps, cb), lambda j, i: (0, j)),
                   pl.BlockSpec((1, cb), lambda j, i: (0, j))),
        scratch_shapes=[pltpu.VMEM((tr, cb), F32)], input_output_aliases={5: 0}, compiler_params=_params(),
    )(duc, duc, z, z, w, dz)


def _gates_fwd(uc, wr, wi, gw, name):
    t, c = uc.shape
    n, bw, _ = wr.shape
    per, ng = gw // bw, c // gw
    tr = _pick(t, (512, 256, 128))

    def body(u_ref, wr_ref, wi_ref, r_ref, i_ref):
        for b in range(per):
            cols = slice(b * bw, (b + 1) * bw)
            a = u_ref[:, cols].astype(BF16)
            r_ref[:, cols] = jnp.dot(a, wr_ref[b].astype(BF16), preferred_element_type=F32)
            i_ref[:, cols] = jnp.dot(a, wi_ref[b].astype(BF16), preferred_element_type=F32)

    blk = pl.BlockSpec((tr, gw), lambda h, i: (i, h))
    wsp = pl.BlockSpec((per, bw, bw), lambda h, i: (h, 0, 0))
    shp = jax.ShapeDtypeStruct((t, c), F32)
    return pl.pallas_call(body, name=name, out_shape=(shp, shp), grid=(ng, t // tr), in_specs=[blk, wsp, wsp],
                          out_specs=(blk, blk), compiler_params=_params())(uc, wr, wi)


def _gates_bwd_x(duc, drp, dip, wr, wi, gw, name):
    t, c = duc.shape
    n, bw, _ = wr.shape
    per, ng = gw // bw, c // gw
    tr = _pick(t, (512, 256, 128))
    dims = (((1,), (1,)), ((), ()))

    def body(d_ref, r_ref, i_ref, wr_ref, wi_ref, o_ref):
        for b in range(per):
            cols = slice(b * bw, (b + 1) * bw)
            o_ref[:, cols] = (
                d_ref[:, cols]
                + lax.dot_general(r_ref[:, cols].astype(BF16), wr_ref[b].astype(BF16), dims, preferred_element_type=F32)
                + lax.dot_general(i_ref[:, cols].astype(BF16), wi_ref[b].astype(BF16), dims, preferred_element_type=F32))

    blk = pl.BlockSpec((tr, gw), lambda h, i: (i, h))
    wsp = pl.BlockSpec((per, bw, bw), lambda h, i: (h, 0, 0))
    return pl.pallas_call(body, name=name, out_shape=jax.ShapeDtypeStruct((t, c), F32), grid=(ng, t // tr),
                          in_specs=[blk, blk, blk, wsp, wsp], out_specs=blk, compiler_params=_params())(duc, drp, dip, wr, wi)


def _gates_bwd_w(uc, dpre, n, bw, gw, name):
    t, c = uc.shape
    per, ng = gw // bw, c // gw
    tk = _pick(t, (512, 256, 128))
    dims = (((0,), (0,)), ((), ()))

    def body(u_ref, d_ref, o_ref):
        @pl.when(pl.program_id(1) == 0)
        def _():
            o_ref[...] = jnp.zeros_like(o_ref)

        for b in range(per):
            cols = slice(b * bw, (b + 1) * bw)
            o_ref[b] += lax.dot_general(u_ref[:, cols].astype(BF16), d_ref[:, cols].astype(BF16), dims,
                                        preferred_element_type=F32)

    blk = pl.BlockSpec((tk, gw), lambda h, i: (i, h))
    return pl.pallas_call(body, name=name, out_shape=jax.ShapeDtypeStruct((n, bw, bw), F32), grid=(ng, t // tk),
                          in_specs=[blk, blk], out_specs=pl.BlockSpec((per, bw, bw), lambda h, i: (h, 0, 0)),
                          compiler_params=_params())(uc, dpre)


def _softplus(x):
    return jnp.maximum(x, 0.0) + jnp.log(1.0 + jnp.exp(-jnp.abs(x)))


_GELU_C = math.sqrt(2.0 / math.pi)


def _gelu_parts(x):
    inner = _GELU_C * (x + 0.044715 * (x * x * x))
    th = jnp.tanh(inner)
    gelu = 0.5 * x * (1.0 + th)
    dgelu = 0.5 * (1.0 + th) + 0.5 * x * (1.0 - th * th) * (_GELU_C * (1.0 + 3.0 * 0.044715 * (x * x)))
    return gelu, dgelu


def _lru_gate_values(rpre, ipre, br, bi, sp):
    r = _sigmoid(rpre + br)
    ig = _sigmoid(ipre + bi)
    log_a = -LRU_C * r * sp
    a = jnp.exp(log_a)
    e2 = jnp.tanh(-log_a) * (1.0 + a * a)
    inv = lax.rsqrt(jnp.maximum(e2, 1e-30))
    return r, ig, a, e2 * inv, inv


def _lru_fwd(uc, rpre, ipre, z, gr0, br, bi, lam, name, job=None):
    t, c = uc.shape
    cb = _pick(c, (1408, 1024, 512, 256, 128))
    tb = _pick(t, (512, 256, 128))
    ntile = tb // SUBLANE

    def body(uc_ref, r_ref, i_ref, gr_ref, br_ref, bi_ref, lam_ref, h_ref, rec16_ref, carry, rec_ref):
        @pl.when(pl.program_id(1) == 0)
        def _():
            carry[...] = jnp.zeros_like(carry)

        sp = _softplus(-lam_ref[...])
        br, bi = br_ref[...], bi_ref[...]
        row = lax.broadcasted_iota(jnp.int32, (SUBLANE, cb), 0)

        def tile(k, c_in):
            sl = pl.ds(pl.multiple_of(k * SUBLANE, SUBLANE), SUBLANE)
            ucv = uc_ref[sl, :]
            _, ig, a, mult, _ = _lru_gate_values(r_ref[sl, :], i_ref[sl, :], br, bi, sp)
            b = mult * (ig * ucv)
            for d in (1, 2, 4):
                a_s = jnp.where(row >= d, pltpu.roll(a, d, 0), 1.0)
                b_s = jnp.where(row >= d, pltpu.roll(b, d, 0), 0.0)
                b = a * b_s + b
                a = a * a_s
            hv = b + a * c_in
            h_ref[sl, :] = hv
            rec_ref[sl, :] = hv * _gelu_parts(gr_ref[sl, :])[0]
            return hv[SUBLANE - 1:SUBLANE, :]

        c_out = lax.fori_loop(0, ntile, tile, carry[0:1, :])
        carry[...] = jnp.broadcast_to(c_out, (SUBLANE, cb))
        rec16_ref[...] = rec_ref[...].astype(BF16)

    blk = pl.BlockSpec((tb, cb), lambda j, i: (i, j))
    vec = pl.BlockSpec((1, cb), lambda j, i: (0, j))
    res, extra = _call(body, name, (jax.ShapeDtypeStruct((t, c), F32), jax.ShapeDtypeStruct((t, c), BF16)),
                       (c // cb, t // tb),
                       [blk, blk, blk, _window(tb, cb, gr0, lambda j, i: i, lambda j, i: j), vec, vec, vec], (blk, blk),
                       (uc, rpre, ipre, z, br, bi, lam), [pltpu.VMEM((SUBLANE, cb), F32), pltpu.VMEM((tb, cb), F32)], job)
    return res if job is None else (res, extra)


def _lru_bwd(drec, hst, uc, rpre, ipre, z, gr0, br, bi, lam, dz, name, job=None):
    t, c = uc.shape
    cb = _pick(c, (1408, 1024, 512, 256, 128))
    tb = _pick(t, (256, 128))
    ntile, nt, hb = tb // SUBLANE, t // tb, tb // SUBLANE

    def body(drec_ref, h_ref, hprev_ref, uc_ref, r_ref, i_ref, gr_ref, br_ref, bi_ref, lam_ref, dz_ref,
             dgr16_ref, drp_ref, dip_ref, duc_ref, dlam_ref, dbr_ref, dbi_ref, carry, dgr_ref):
        step = pl.program_id(1)
        first_block = step == nt - 1

        @pl.when(step == 0)
        def _():
            carry[...] = jnp.zeros_like(carry)
            dlam_ref[...] = jnp.zeros_like(dlam_ref)
            dbr_ref[...] = jnp.zeros_like(dbr_ref)
            dbi_ref[...] = jnp.zeros_like(dbi_ref)

        lam = lam_ref[...]
        sp = _softplus(-lam)
        br, bi = br_ref[...], bi_ref[...]
        row = lax.broadcasted_iota(jnp.int32, (SUBLANE, cb), 0)
        halo = jnp.where(first_block, 0.0, hprev_ref[...])

        def tile(kk, state):
            c_p, acc_sp, acc_br, acc_bi = state
            k = ntile - 1 - kk
            sl = pl.ds(pl.multiple_of(k * SUBLANE, SUBLANE), SUBLANE)
            slp = pl.ds(pl.multiple_of(jnp.maximum(k - 1, 0) * SUBLANE, SUBLANE), SUBLANE)
            ucv = uc_ref[sl, :]
            r, ig, a, mult, inv_mult = _lru_gate_values(r_ref[sl, :], i_ref[sl, :], br, bi, sp)
            hv = h_ref[sl, :]
            below = jnp.where(k > 0, h_ref[slp, :], halo)
            hprev = jnp.where(row == 0, pltpu.roll(below, 1, 0), pltpu.roll(hv, 1, 0))
            gelu, dgelu = _gelu_parts(gr_ref[sl, :])
            drec = drec_ref[sl, :]
            dh = drec * gelu
            dgr_ref[sl, :] = drec * hv * dgelu
            pa, pb = a, a * dh
            for d in (1, 2, 4):
                a_s = jnp.where(row < SUBLANE - d, pltpu.roll(pa, SUBLANE - d, 0), 1.0)
                b_s = jnp.where(row < SUBLANE - d, pltpu.roll(pb, SUBLANE - d, 0), 0.0)
                pb = pa * b_s + pb
                pa = pa * a_s
            pv = pb + pa * c_p
            gt = dh + jnp.where(row == SUBLANE - 1, c_p, pltpu.roll(pv, SUBLANE - 1, 0))
            da = gt * hprev
            duc_ref[sl, :] = gt * mult * ig
            dmult = gt * ig * ucv
            dig = gt * mult * ucv
            dla = da * a - jnp.where(mult > 0.0, dmult * (a * a) * inv_mult, 0.0)
            drp = dla * (-LRU_C * sp) * (r * (1.0 - r))
            dip = dig * (ig * (1.0 - ig))
            drp_ref[sl, :] = drp
            dip_ref[sl, :] = dip
            return pv[0:1, :], acc_sp + dla * (-LRU_C * r), acc_br + drp, acc_bi + dip

        zero = jnp.zeros((SUBLANE, cb), F32)
        c_out, acc_sp, acc_br, acc_bi = lax.fori_loop(0, ntile, tile, (carry[0:1, :], zero, zero, zero))
        carry[...] = jnp.broadcast_to(c_out, (SUBLANE, cb))
        dlam_ref[...] += jnp.sum(acc_sp, axis=0, keepdims=True) * (-_sigmoid(-lam))
        dbr_ref[...] += jnp.sum(acc_br, axis=0, keepdims=True)
        dbi_ref[...] += jnp.sum(acc_bi, axis=0, keepdims=True)
        dgr16_ref[...] = dgr_ref[...].astype(BF16)

    blk = pl.BlockSpec((tb, cb), lambda j, i: (nt - 1 - i, j))
    vec = pl.BlockSpec((1, cb), lambda j, i: (0, j))
    halo_spec = pl.BlockSpec((SUBLANE, cb), lambda j, i: (jnp.maximum((nt - 1 - i) * hb - 1, 0), j))
    big, small = jax.ShapeDtypeStruct((t, c), F32), jax.ShapeDtypeStruct((1, c), F32)
    at_gr = _window(tb, cb, gr0, lambda j, i: nt - 1 - i, lambda j, i: j)
    res, extra = _call(
        body, name, (jax.ShapeDtypeStruct(dz.shape, BF16), big, big, big, small, small, small), (c // cb, nt),
        [blk, blk, halo_spec, blk, blk, blk, at_gr, vec, vec, vec, ANY], (at_gr, blk, blk, blk, vec, vec, vec),
        (drec, hst, hst, uc, rpre, ipre, z, br, bi, lam, dz),
        [pltpu.VMEM((SUBLANE, cb), F32), pltpu.VMEM((tb, cb), F32)], job, {10: 0})
    return res if job is None else (res, extra)


def _shard_region(ref, kind, chip, half, rh, width):
    if kind == "col":
        return ref.at[pl.ds(half * rh, rh), pl.ds(chip * width, width)]
    return ref.at[pl.ds(chip * (2 * rh) + half * rh, rh), :]


class _AllGather(_Exchange):
    def __init__(self, fulls, kinds):
        self.inputs, self.kinds = list(fulls), kinds
        self.out_shapes = [jax.ShapeDtypeStruct(f.shape, f.dtype) for f in fulls]
        self.aliases = {a: a for a in range(len(fulls))}
        self.n_sems = 6 * len(fulls)
        self.geo = [(f.shape[0] // 2, f.shape[1] // N_CHIPS) if k == "col" else (f.shape[0] // (2 * N_CHIPS), f.shape[1])
                    for f, k in zip(fulls, kinds)]

    def _region(self, ref, a, chip, half):
        return _shard_region(ref, self.kinds[a], chip, half, *self.geo[a])

    def _ici(self, e, a, k, chip):
        cx, cy = e.chips[k]
        return e.copy(self._region(e.ins[a], a, chip, e.c), self._region(e.outs[a], a, chip, e.c), a * 6 + k,
                      (cx, cy, e.c))

    def _d2d(self, e, a, k, half):
        cx, cy = e.chips[k]
        region = self._region(e.outs[a], a, 2 * cx + cy, half)
        return e.copy(region, region, a * 6 + 3 + k, e.sibling)

    def start(self, e):
        for a in range(len(self.inputs)):
            for k in range(3):
                self._ici(e, a, k, e.me).start()

    def finish(self, e):
        n = len(self.inputs)
        for a in range(n):
            for k, (cx, cy) in enumerate(e.chips):
                self._ici(e, a, k, 2 * cx + cy).wait_recv()
                self._d2d(e, a, k, e.c).start()
        for a in range(n):
            for k in range(3):
                self._d2d(e, a, k, 1 - e.c).wait_recv()
        for a in range(n):
            for k in range(3):
                self._ici(e, a, k, e.me).wait_send()
                self._d2d(e, a, k, e.c).wait_send()


class _SiblingExchange(_Exchange):
    def __init__(self, grads):
        self.inputs = list(grads)
        self.out_shapes = [jax.ShapeDtypeStruct((g.shape[0],) + g.shape[2:], g.dtype) for g in grads]
        self.n_sems = len(grads)

    def _copy(self, e, a):
        return e.copy(e.ins[a].at[:, 1 - e.c], e.outs[a], a, e.sibling)

    def start(self, e):
        for a in range(len(self.inputs)):
            self._copy(e, a).start()

    def finish(self, e):
        for a in range(len(self.inputs)):
            self._copy(e, a).wait()


def _piece(ref, kind, chip, width):
    if kind == "col":
        return ref.at[0, :, pl.ds(chip * width, width)]
    return ref.at[chip]


class _ChipExchange(_Exchange):
    def __init__(self, sums, kinds):
        self.inputs, self.kinds = list(sums), kinds
        self.widths = [s.shape[2] // N_CHIPS if k == "col" else s.shape[2] for s, k in zip(sums, kinds)]
        self.out_shapes = [jax.ShapeDtypeStruct((3, s.shape[1], w), s.dtype) for s, w in zip(sums, self.widths)]
        self.n_sems = 3 * len(sums)

    def _copy(self, e, a, k, chip):
        cx, cy = e.chips[k]
        return e.copy(_piece(e.ins[a], self.kinds[a], chip, self.widths[a]), e.outs[a].at[k], a * 3 + k, (cx, cy, e.c))

    def start(self, e):
        for a in range(len(self.inputs)):
            for k, (cx, cy) in enumerate(e.chips):
                self._copy(e, a, k, 2 * cx + cy).start()

    def finish(self, e):
        for a in range(len(self.inputs)):
            for k, (cx, cy) in enumerate(e.chips):
                self._copy(e, a, k, 2 * cx + cy).wait()


class _FinishExchange(_Exchange):
    def __init__(self, finals, to_all):
        self.inputs, self.to_all = list(finals), list(to_all)
        self.out_shapes = [jax.ShapeDtypeStruct(f.shape, f.dtype) for f in finals]
        self.aliases = {a: a for a in range(len(finals))}
        self.first_sem, self.n_sems = [], 0
        for all8 in self.to_all:
            self.first_sem.append(self.n_sems)
            self.n_sems += 7 if all8 else 1
        self.rel = [(fx, fy, fc) for fx in (0, 1) for fy in (0, 1) for fc in (0, 1)][1:]

    def _copies(self, e, mine):
        for a, all8 in enumerate(self.to_all):
            src = e.ins[a] if mine else e.outs[a]
            if not all8:
                rh = self.inputs[a].shape[0] // 2
                rows = pl.ds((e.c if mine else 1 - e.c) * rh, rh)
                yield e.copy(src.at[rows, :], e.outs[a].at[rows, :], self.first_sem[a], e.sibling)
                continue
            rh = self.inputs[a].shape[0] // (2 * N_CHIPS)
            for r, (fx, fy, fc) in enumerate(self.rel):
                px, py, pc = (1 - e.x if fx else e.x), (1 - e.y if fy else e.y), (1 - e.c if fc else e.c)
                rows = pl.ds(((2 * e.me + e.c) if mine else (2 * (2 * px + py) + pc)) * rh, rh)
                yield e.copy(src.at[rows, :], e.outs[a].at[rows, :], self.first_sem[a] + r, (px, py, pc))

    def start(self, e):
        for cp in self._copies(e, True):
            cp.start()

    def finish(self, e):
        for cp in self._copies(e, False):
            cp.wait_recv()
        for cp in self._copies(e, True):
            cp.wait_send()


def _cast_into_full(w, kind, idx, name):
    r, c = w.shape
    tr = _row_tile(r, c)
    nrb = r // tr

    def body(idx_ref, w_ref, o_ref):
        o_ref[...] = w_ref[...].astype(BF16)

    if kind == "col":
        full, out_map = (r, N_CHIPS * c), (lambda i, idx_ref: (i, idx_ref[1]))
    else:
        full, out_map = (N_CHIPS * r, c), (lambda i, idx_ref: (idx_ref[1] * nrb + i, 0))
    return pl.pallas_call(
        body, name=name, out_shape=jax.ShapeDtypeStruct(full, BF16),
        grid_spec=pltpu.PrefetchScalarGridSpec(
            num_scalar_prefetch=1, grid=(nrb,), in_specs=[pl.BlockSpec((tr, c), lambda i, idx_ref: (i, 0))],
            out_specs=pl.BlockSpec((tr, c), out_map)),
        compiler_params=_params(),
    )(idx, w)


def _matmul_gathering(a, placed, order, name):
    t, k = a.shape
    n = placed.shape[1]
    w = n // N_CHIPS
    tm, tn = _pick(t, _M_TILES), _pick(w, _N_TILES)
    ni, nj = t // tm, w // tn
    per_shard, total = ni * nj, N_CHIPS * ni * nj
    gather = _AllGather([placed], ["col"])

    def body(ord_ref, a_ref, w_own_ref, o_ref, w_ref, wbuf, fetch_sem, send, recv):
        s, i, j = pl.program_id(0), pl.program_id(1), pl.program_id(2)
        step = (s * ni + i) * nj + j
        e = _Env((w_own_ref,), (w_ref,), send, recv)

        def fetch(src, st):
            col = pl.multiple_of((ord_ref[st // per_shard] * nj + st % nj) * tn, LANE)
            return pltpu.make_async_copy(src.at[:, pl.ds(col, tn)], wbuf.at[st % 2], fetch_sem.at[st % 2])

        @pl.when(step == 0)
        def _():
            gather.start(e)
            fetch(w_own_ref, step).start()

        nxt = step + 1
        for kk, (cx, cy) in enumerate(e.chips):
            @pl.when(nxt == (kk + 1) * per_shard)
            def _():
                gather._ici(e, 0, kk, 2 * cx + cy).wait_recv()
                gather._d2d(e, 0, kk, e.c).start()
                gather._d2d(e, 0, kk, 1 - e.c).wait_recv()

        @pl.when(nxt < per_shard)
        def _():
            fetch(w_own_ref, nxt).start()

        @pl.when((nxt >= per_shard) & (nxt < total))
        def _():
            fetch(w_ref, nxt).start()

        fetch(w_ref, step).wait()
        o_ref[...] = jnp.dot(a_ref[...], wbuf[step % 2], preferred_element_type=F32)

        @pl.when(step == total - 1)
        def _():
            for kk in range(3):
                gather._ici(e, 0, kk, e.me).wait_send()
                gather._d2d(e, 0, kk, e.c).wait_send()

    z, full = pl.pallas_call(
        body, name=name, out_shape=(jax.ShapeDtypeStruct((t, n), F32), jax.ShapeDtypeStruct(placed.shape, placed.dtype)),
        grid_spec=pltpu.PrefetchScalarGridSpec(
            num_scalar_prefetch=1, grid=(N_CHIPS, ni, nj),
            in_specs=[pl.BlockSpec((tm, k), lambda s, i, j, ord_ref: (i, 0)), ANY],
            out_specs=(pl.BlockSpec((tm, tn), lambda s, i, j, ord_ref: (i, ord_ref[s] * nj + j)), ANY),
            scratch_shapes=[pltpu.VMEM((2, k, tn), placed.dtype), pltpu.SemaphoreType.DMA((2,)),
                            pltpu.SemaphoreType.DMA((gather.n_sems,)), pltpu.SemaphoreType.DMA((gather.n_sems,))]),
        input_output_aliases={2: 1}, compiler_params=_params(),
    )(order, a, placed)
    return z, full


def _add_own_half(g4, recv, idx, out_dtype, name):
    p, _, rh, n = g4.shape
    tr, tc = _tile2d(rh, n, 1024 * 1024)

    def body(idx_ref, g_ref, r_ref, o_ref):
        o_ref[...] = (g_ref[...] + r_ref[...]).astype(out_dtype)

    return pl.pallas_call(
        body, name=name, out_shape=jax.ShapeDtypeStruct((p, rh, n), out_dtype),
        grid_spec=pltpu.PrefetchScalarGridSpec(
            num_scalar_prefetch=1, grid=(p, rh // tr, n // tc),
            in_specs=[pl.BlockSpec((None, None, tr, tc), lambda q, i, j, idx_ref: (q, idx_ref[0], i, j)),
                      pl.BlockSpec((None, tr, tc), lambda q, i, j, idx_ref: (q, i, j))],
            out_specs=pl.BlockSpec((None, tr, tc), lambda q, i, j, idx_ref: (q, i, j))),
        compiler_params=_params(),
    )(idx, g4, recv)


def _sum_chips(own, kind, parts, idx, slots, to_all, name):
    _, rh, w = parts.shape
    tr, tc = _tile2d(rh, w, 512 * 1024)
    nrb, ncb = rh // tr, w // tc

    def body(idx_ref, own_ref, p0, p1, p2, o_ref):
        o_ref[...] = ((own_ref[...].astype(F32) + p0[...].astype(F32)) + p1[...].astype(F32)) + p2[...].astype(F32)

    if kind == "col":
        own_spec = pl.BlockSpec((None, tr, tc), lambda i, j, idx_ref: (0, i, idx_ref[1] * ncb + j))
    else:
        own_spec = pl.BlockSpec((None, tr, tc), lambda i, j, idx_ref: (idx_ref[1], i, j))
    if to_all:
        out_map = lambda i, j, idx_ref: ((2 * idx_ref[1] + idx_ref[0]) * nrb + i, j)
    else:
        out_map = lambda i, j, idx_ref: (idx_ref[0] * nrb + i, j)

    def part(k):
        return pl.BlockSpec((None, tr, tc), lambda i, j, idx_ref: (k, i, j))

    return pl.pallas_call(
        body, name=name, out_shape=jax.ShapeDtypeStruct((slots * rh, w), F32),
        grid_spec=pltpu.PrefetchScalarGridSpec(
            num_scalar_prefetch=1, grid=(nrb, ncb), in_specs=[own_spec, part(0), part(1), part(2)],
            out_specs=pl.BlockSpec((tr, tc), out_map)),
        compiler_params=_params(),
    )(idx, own, parts, parts, parts)


class _Reduce:
    def __init__(self, name, g, kind, idx, wire, to_all):
        r, c = g.shape
        self.name, self.kind, self.idx, self.wire, self.to_all = name, kind, idx, wire, to_all
        self.view = g.reshape(1, 2, r // 2, c) if kind == "col" else g.reshape(N_CHIPS, 2, r // (2 * N_CHIPS), c)

    def sibling(self):
        return _SiblingExchange([self.view])

    def got_sibling(self, outs):
        self.sum = _add_own_half(self.view, outs[0], self.idx, self.wire, "grad_chip_sum_" + self.name)

    def chips(self):
        return _ChipExchange([self.sum], [self.kind])

    def got_chips(self, outs):
        self.total = _sum_chips(self.sum, self.kind, outs[0], self.idx, 2 * N_CHIPS if self.to_all else 2,
                                self.to_all, "grad_total_" + self.name)


def _pack(arrays, rows):
    flat = jnp.concatenate([a.reshape(-1) for a in arrays])
    return jnp.pad(flat, (0, rows * SMALL_PACK_COLS - flat.shape[0])).reshape(rows, SMALL_PACK_COLS)


def _unpack(packed, shapes):
    flat = packed.reshape(-1)
    out, o = [], 0
    for shp in shapes:
        size = math.prod(shp)
        out.append(flat[o:o + size].reshape(shp))
        o += size
    return out


def _pack_rows(shapes):
    total = sum(math.prod(s) for s in shapes)
    unit = SMALL_PACK_COLS * N_CHIPS * 2 * SUBLANE
    return -(-total // unit) * (N_CHIPS * 2 * SUBLANE)


BIG = ("w_in", "w_attn_proj", "w_lru_proj", "w_out", "w_ffn_gate", "w_ffn_up", "w_ffn_down")
BIG_KIND = {"w_in": "col", "w_attn_proj": "row", "w_lru_proj": "row", "w_out": "row", "w_ffn_gate": "col",
            "w_ffn_up": "col", "w_ffn_down": "row"}
SMALL = ("norm1_g", "b_gates", "q_norm_g", "k_norm_g", "sinks", "conv_w", "conv_b", "w_rgate", "b_rgate",
         "w_igate", "b_igate", "lru_lambda", "norm2_g")
PACKED = tuple(n for n in SMALL if n not in ("w_rgate", "w_igate"))
WEIGHTS = ("norm1_g", "w_in", "b_gates", "q_norm_g", "k_norm_g", "sinks", "conv_w", "conv_b", "w_rgate", "b_rgate",
           "w_igate", "b_igate", "lru_lambda", "w_attn_proj", "w_lru_proj", "w_out", "norm2_g", "w_ffn_gate",
           "w_ffn_up", "w_ffn_down")


def kernel(x, positions, norm1_g, w_in, b_gates, q_norm_g, k_norm_g, sinks, conv_w, conv_b, w_rgate, b_rgate, w_igate, b_igate, lru_lambda, w_attn_proj, w_lru_proj, w_out, norm2_g, w_ffn_gate, w_ffn_up, w_ffn_down, loss_target, m_norm1_g, m_w_in, m_b_gates, m_q_norm_g, m_k_norm_g, m_sinks, m_conv_w, m_conv_b, m_w_rgate, m_b_rgate, m_w_igate, m_b_igate, m_lru_lambda, m_w_attn_proj, m_w_lru_proj, m_w_out, m_norm2_g, m_w_ffn_gate, m_w_ffn_up, m_w_ffn_down, v_norm1_g, v_w_in, v_b_gates, v_q_norm_g, v_k_norm_g, v_sinks, v_conv_w, v_conv_b, v_w_rgate, v_b_rgate, v_w_igate, v_b_igate, v_lru_lambda, v_w_attn_proj, v_w_lru_proj, v_w_out, v_norm2_g, v_w_ffn_gate, v_w_ffn_up, v_w_ffn_down):
    args = dict(locals())
    w = {n: args[n] for n in WEIGHTS}
    mom = {n: args["m_" + n] for n in WEIGHTS}
    var = {n: args["v_" + n] for n in WEIGHTS}

    t, d = x.shape[1], x.shape[2]
    hd = q_norm_g.shape[-1]
    nq = sinks.shape[-1]
    q_w = nq * hd
    d_rnn = conv_b.shape[-1]
    taps = conv_w.shape[1]
    n_blocks, bw = w_rgate.shape[1], w_rgate.shape[2]
    in_w = w_in.shape[-1] * N_CHIPS
    kv_w = (in_w - q_w - 2 * d_rnn - 2 * d) // 2
    kv = kv_w // hd
    grp = nq // kv
    u_off = q_w + 2 * kv_w
    gr_off = u_off + d_rnn
    ga_off = gr_off + d_rnn
    gw = bw * LANE // math.gcd(bw, LANE)
    chip = 2 * lax.axis_index("x") + lax.axis_index("y")
    idx = jnp.stack([lax.axis_index("c"), chip]).astype(jnp.int32)

    x2, tgt = x[0], loss_target[0]

    placed = {n: _cast_into_full(w[n][0], BIG_KIND[n], idx, "cast_" + n) for n in BIG}

    def gather(*names):
        return _AllGather([placed[n] for n in names], [BIG_KIND[n] for n in names])

    mx, my = lax.axis_index("x"), lax.axis_index("y")
    order = jnp.stack([chip, 2 * (1 - mx) + my, 2 * mx + (1 - my), 2 * (1 - mx) + (1 - my)]).astype(jnp.int32)
    conv_w_full = _gather_small(conv_w[0], "allgather_conv_w")
    conv_w_full = jnp.transpose(conv_w_full, (1, 0, 2)).reshape(taps, d_rnn)

    inv_freq = ROPE_THETA ** (-jnp.arange(0, hd // 4, 2, dtype=F32) / (hd // 4))
    ang = positions[0].astype(F32)[:, None] * inv_freq
    cos, sin = jnp.cos(ang), jnp.sin(ang)
    rest = hd - 2 * cos.shape[1]
    cos_t = jnp.concatenate([cos, cos, jnp.ones((t, rest), F32)], axis=1)
    sin_t = jnp.concatenate([-sin, sin, jnp.zeros((t, rest), F32)], axis=1)
    sinks1 = sinks[0]

    xn = _rms_fwd(x2, norm1_g, "rms1_fwd")
    z, win_f = _matmul_gathering(xn, placed["w_in"], order, "in_proj")
    attn, (wap_f, wlp_f, wout_f) = _attn_fwd(z, cos_t, sin_t, q_norm_g, k_norm_g, sinks1, kv, grp, hd, "attn_fwd",
                                             job=gather("w_attn_proj", "w_lru_proj", "w_out"))
    uc = _conv_fwd(z, u_off, d_rnn, conv_w_full, conv_b, "conv_fwd")
    rpre, ipre = _gates_fwd(uc, w_rgate[0], w_igate[0], gw, "gates_fwd")
    (hst, rec), (wg_f,) = _lru_fwd(uc, rpre, ipre, z, gr_off, b_rgate, b_igate, lru_lambda, "lru_fwd",
                                   job=gather("w_ffn_gate"))
    pa = _matmul(attn, wap_f, "nn", "attn_proj")
    plru, merged = _matmul(rec, wlp_f, "nn", "lru_proj", fused=(
        [pa, (z, ga_off), (z, ga_off + d), (b_gates, 0), (b_gates, d)], _merge_after_lru_proj, (F32, BF16)))
    h1 = _matmul(merged, wout_f, "nn", "out_proj", add=x2)
    hn = _rms_fwd(h1, norm2_g, "rms2_fwd")
    gate, (wu_f,) = _matmul(hn, wg_f, "nn", "ffn_gate", job=gather("w_ffn_up"))
    (up, act), (wd_f,) = _matmul(hn, wu_f, "nn", "ffn_up", job=gather("w_ffn_down"),
                                 fused=([gate], _swiglu_after_up, (F32, BF16)))
    yout = _matmul(act, wd_f, "nn", "ffn_down", add=h1)
    dy, dy16, loss_part = _loss_head(yout, tgt, "loss_head")
    loss = lax.psum(loss_part[0, 0], ("x", "y", "c"))

    def reduction(n, g):
        return _Reduce(n, g, BIG_KIND[n], idx, BF16, False)

    r_wd = reduction("w_ffn_down", _matmul(act, dy16, "tn", "d_w_ffn_down"))
    (dgate, dup), got = _matmul(dy16, wd_f, "nt", "d_act", job=r_wd.sibling(),
                                fused=([gate, up], _swiglu_bwd_after_dact, (BF16, BF16)))
    r_wd.got_sibling(got)
    r_wg = reduction("w_ffn_gate", _matmul(hn, dgate, "tn", "d_w_ffn_gate"))
    g_wu, got = _matmul(hn, dup, "tn", "d_w_ffn_up", job=r_wg.sibling())
    r_wg.got_sibling(got)
    r_wu = reduction("w_ffn_up", g_wu)
    dhn, got = _matmul(dgate, wg_f, "nt", "d_hn_gate", job=r_wu.sibling())
    r_wu.got_sibling(got)
    dhn = _matmul(dup, wu_f, "nt", "d_hn_up", add=dhn)
    dh1, g_norm2, dh1_16 = _rms_bwd(dhn, h1, norm2_g, dy, "rms2_bwd", mxu_copy=True)
    r_wout = reduction("w_out", _matmul(merged, dh1_16, "tn", "d_w_out"))
    dmerged, got = _matmul(dh1_16, wout_f, "nt", "d_merged", job=r_wout.sibling())
    r_wout.got_sibling(got)
    (dpa, dz, g_ba), got = _merge_bwd(dmerged, z, b_gates, pa, ga_off, 0, None, "merge_bwd_attn", job=r_wout.chips())
    r_wout.got_chips(got)
    dpl, dz, g_bl = _merge_bwd(dmerged, z, b_gates, plru, ga_off + d, d, dz, "merge_bwd_lru")
    r_wap = reduction("w_attn_proj", _matmul(attn, dpa, "tn", "d_w_attn_proj"))
    dattn, got = _matmul(dpa, wap_f, "nt", "d_attn", job=r_wap.sibling())
    r_wap.got_sibling(got)
    g_wlp, got = _matmul(rec, dpl, "tn", "d_w_lru_proj", job=r_wap.chips())
    r_wap.got_chips(got)
    r_wlp = reduction("w_lru_proj", g_wlp)
    drec, got = _matmul(dpl, wlp_f, "nt", "d_rec", job=r_wlp.sibling())
    r_wlp.got_sibling(got)
    both = _Jobs(r_wlp.chips(), r_wg.chips())
    (dz, drp, dip, duc_direct, g_lam, g_br, g_bi), got = _lru_bwd(
        drec, hst, uc, rpre, ipre, z, gr_off, b_rgate, b_igate, lru_lambda, dz, "lru_bwd", job=both)
    got_wlp, got_wg = both.split(got)
    r_wlp.got_chips(got_wlp)
    r_wg.got_chips(got_wg)
    duc = _gates_bwd_x(duc_direct, drp, dip, w_rgate[0], w_igate[0], gw, "gates_bwd_x")
    g_wr = _gates_bwd_w(uc, drp, n_blocks, bw, gw, "gates_bwd_wr")
    g_wi = _gates_bwd_w(uc, dip, n_blocks, bw, gw, "gates_bwd_wi")
    dz, g_convw, g_convb = _conv_bwd(duc, z, u_off, conv_w_full, dz, "conv_bwd")
    (dq, dk, dv, g_qg, g_kg, g_sinks), got = _attn_bwd(dattn, z, cos_t, sin_t, q_norm_g, k_norm_g, sinks1, kv, grp, hd,
                                                        "attn_bwd", job=_Jobs(r_wu.chips(), r_wd.chips()))
    r_wu.got_chips(got[:1])
    r_wd.got_chips(got[1:])
    for part, col in ((dq, 0), (dk, q_w), (dv, q_w + kv_w)):
        dz = lax.dynamic_update_slice(dz, part, (0, col))
    r_wr = _Reduce("w_rgate", g_wr.reshape(n_blocks * bw, bw), "row", idx, F32, True)
    r_wi = _Reduce("w_igate", g_wi.reshape(n_blocks * bw, bw), "row", idx, F32, True)
    early = [r_wap, r_wlp, r_wout, r_wg, r_wu, r_wd]
    three = _Jobs(r_wr.sibling(), r_wi.sibling(), _FinishExchange([r.total for r in early], [False] * len(early)))
    g_top, got = _matmul(xn, dz, "tn", "d_w_in_top", m_window=(0, d // 2), job=three)
    got_wr, got_wi, finished = three.split(got)
    r_wr.got_sibling(got_wr)
    r_wi.got_sibling(got_wi)
    r_top = _Reduce("w_in_top", g_top, "col", idx, BF16, False)
    three = _Jobs(r_top.sibling(), r_wr.chips(), r_wi.chips())
    g_bot, got = _matmul(xn, dz, "tn", "d_w_in_bot", m_window=(d // 2, d // 2), job=three)
    got_top, got_wr, got_wi = three.split(got)
    r_top.got_sibling(got_top)
    r_wr.got_chips(got_wr)
    r_wi.got_chips(got_wi)
    r_bot = _Reduce("w_in_bot", g_bot, "col", idx, BF16, False)
    both = _Jobs(r_top.chips(), r_bot.sibling())
    dxn, got = _matmul(dz, win_f, "nt", "d_xn_a", m_window=(0, t // 2), into=(None, t), job=both)
    got_top, got_bot = both.split(got)
    r_top.got_chips(got_top)
    r_bot.got_sibling(got_bot)
    dxn, got = _matmul(dz, win_f, "nt", "d_xn_b", m_window=(t // 2, t // 2), into=(dxn, t), job=r_bot.chips())
    r_bot.got_chips(got)
    dx, g_norm1 = _rms_bwd(dxn, x2, norm1_g, dh1, "rms1_bwd")

    small_grads = {"norm1_g": g_norm1, "b_gates": jnp.concatenate([g_ba, g_bl], axis=1), "q_norm_g": g_qg,
                   "k_norm_g": g_kg, "sinks": g_sinks[:, :nq], "conv_w": g_convw, "conv_b": g_convb,
                   "b_rgate": g_br, "b_igate": g_bi, "lru_lambda": g_lam, "norm2_g": g_norm2}
    gshapes = [small_grads[n].shape for n in PACKED]
    small_sum = _allreduce_small(_pack([small_grads[n] for n in PACKED], _pack_rows(gshapes)), "allreduce_small")
    top, bot, grads_wr, grads_wi = _run_exchange(
        _FinishExchange([r.total for r in (r_top, r_bot, r_wr, r_wi)], [False, False, True, True]),
        "grad_finish_exchange")
    grads = dict(zip(BIG[1:], finished))
    grads["w_in"] = jnp.concatenate([top, bot], axis=0)
    grads["w_rgate"], grads["w_igate"] = grads_wr, grads_wi
    small_full = dict(zip(PACKED, _unpack(small_sum, gshapes)))
    per = d_rnn // N_CHIPS
    small_full["conv_w"] = lax.dynamic_slice(small_full["conv_w"], (0, chip * per), (taps, per))
    grads.update(small_full)

    delta, new_m, new_v = {}, {}, {}
    for n in BIG + ("w_rgate", "w_igate"):
        as2d = (lambda a: a[0]) if n in BIG else (lambda a: a.reshape(n_blocks * bw, bw))
        if n == "w_in":
            delta[n], new_m[n], new_v[n] = _adamw(as2d(w[n]), grads[n], as2d(mom[n]), as2d(var[n]), "adamw_" + n)
        else:
            delta[n], new_m[n], new_v[n], grads[n] = _adamw(as2d(w[n]), grads[n], as2d(mom[n]), as2d(var[n]),
                                                            "adamw_" + n, pass_grad=True)
    pshapes = [w[n].shape for n in PACKED]
    prows = _pack_rows(pshapes)
    pk = [_pack([src[n] for n in PACKED], prows) for src in (w, grads, mom, var)]
    for res, packed in zip((delta, new_m, new_v), _adamw(pk[0], pk[1], pk[2], pk[3], "adamw_small")):
        res.update(dict(zip(PACKED, _unpack(packed, pshapes))))

    outs = [loss, dx.reshape(x.shape)]
    for res in (grads, delta, new_m, new_v):
        outs += [res[n].reshape(w[n].shape) for n in WEIGHTS]
    return tuple(outs)


def _allreduce_small(x, name):
    n_dev = 2 * N_CHIPS
    rel = [(fx, fy, fc) for fx in (0, 1) for fy in (0, 1) for fc in (0, 1)][1:]

    def body(x_ref, all_ref, o_ref, send_sems, recv_sems):
        e = _Env((x_ref,), (all_ref,), send_sems, recv_sems)
        mine = 2 * e.me + e.c

        def peer(r):
            fx, fy, fc = rel[r]
            return (1 - e.x if fx else e.x), (1 - e.y if fy else e.y), (1 - e.c if fc else e.c)

        all_ref[mine] = x_ref[...]
        for r in range(len(rel)):
            e.copy(x_ref, all_ref.at[mine], r, peer(r)).start()
        for r in range(len(rel)):
            px, py, pc = peer(r)
            e.copy(x_ref, all_ref.at[2 * (2 * px + py) + pc], r, peer(r)).wait_recv()
        total = all_ref[0]
        for dev in range(1, n_dev):
            total = total + all_ref[dev]
        o_ref[...] = total
        for r in range(len(rel)):
            e.copy(x_ref, all_ref.at[mine], r, peer(r)).wait_send()

    vm = pl.BlockSpec(memory_space=pltpu.VMEM)
    return pl.pallas_call(
        body, name=name, out_shape=(jax.ShapeDtypeStruct((n_dev,) + x.shape, x.dtype), jax.ShapeDtypeStruct(x.shape, x.dtype)),
        in_specs=[vm], out_specs=(vm, vm),
        scratch_shapes=[pltpu.SemaphoreType.DMA((len(rel),)), pltpu.SemaphoreType.DMA((len(rel),))])(x)[1]


def _gather_small(shard, name):
    def body(s_ref, o_ref, send_sems, recv_sems):
        e = _Env((s_ref,), (o_ref,), send_sems, recv_sems)
        o_ref[e.me] = s_ref[...]
        for k, (cx, cy) in enumerate(e.chips):
            e.copy(s_ref, o_ref.at[e.me], k, (cx, cy, e.c)).start()
        for k, (cx, cy) in enumerate(e.chips):
            e.copy(s_ref, o_ref.at[2 * cx + cy], k, (cx, cy, e.c)).wait_recv()
        for k, (cx, cy) in enumerate(e.chips):
            e.copy(s_ref, o_ref.at[e.me], k, (cx, cy, e.c)).wait_send()

    vm = pl.BlockSpec(memory_space=pltpu.VMEM)
    return pl.pallas_call(body, name=name, out_shape=jax.ShapeDtypeStruct((N_CHIPS,) + shard.shape, shard.dtype),
                          in_specs=[vm], out_specs=vm,
                          scratch_shapes=[pltpu.SemaphoreType.DMA((3,)), pltpu.SemaphoreType.DMA((3,))])(shard)
```

```python
import functools
import math

import jax
import jax.numpy as jnp
from jax import lax
from jax.experimental import pallas as pl
from jax.experimental.pallas import tpu as pltpu

F32 = jnp.float32
BF16 = jnp.bfloat16
MESH = pl.DeviceIdType.MESH

WINDOW = 128
BLK = 128
ROPE_THETA = 500000.0
LRU_C = 8.0
EPS = 1e-6
NEG = -1e30
ADAM_LR = 0.001
ADAM_B1 = 0.9
ADAM_B2 = 0.999
ADAM_EPS = 1e-08
ADAM_WD = 0.01
ADAM_STEP = 10

VMEM_LIMIT_BYTES = 52 * 1024 * 1024
LANE = 128
SUBLANE = 8
N_CHIPS = 4
SMALL_PACK_COLS = 512


def _params(**kw):
    return pltpu.CompilerParams(vmem_limit_bytes=VMEM_LIMIT_BYTES, **kw)


def _pick(dim, cands):
    for c in cands:
        if dim % c == 0:
            return c
    return dim


def _sigmoid(x):
    return 0.5 * jnp.tanh(0.5 * x) + 0.5


ANY = pl.BlockSpec(memory_space=pl.ANY)


class _Env:
    def __init__(self, ins, outs, send, recv, sem0=0, place=None):
        self.ins, self.outs, self.send, self.recv, self.sem0 = ins, outs, send, recv, sem0
        self.x, self.y, self.c = place or (lax.axis_index("x"), lax.axis_index("y"), lax.axis_index("c"))
        self.me = 2 * self.x + self.y
        self.chips = [(1 - self.x, self.y), (self.x, 1 - self.y), (1 - self.x, 1 - self.y)]
        self.sibling = (self.x, self.y, 1 - self.c)

    def sub(self, i0, n_in, o0, n_out, sem0):
        return _Env(self.ins[i0:i0 + n_in], self.outs[o0:o0 + n_out], self.send, self.recv, self.sem0 + sem0,
                    (self.x, self.y, self.c))

    def copy(self, src, dst, sem, to):
        return pltpu.make_async_remote_copy(src_ref=src, dst_ref=dst, send_sem=self.send.at[self.sem0 + sem],
                                            recv_sem=self.recv.at[self.sem0 + sem], device_id=to, device_id_type=MESH)


class _Exchange:
    inputs, out_shapes, aliases, n_sems = (), (), {}, 0

    def start(self, e):
        raise NotImplementedError

    def finish(self, e):
        raise NotImplementedError


class _Jobs(_Exchange):
    def __init__(self, *jobs):
        self.jobs, self.inputs, self.out_shapes, self.aliases, self.n_sems, self.at = jobs, [], [], {}, 0, []
        for job in jobs:
            self.at.append((len(self.inputs), len(self.out_shapes), self.n_sems))
            self.aliases.update({len(self.inputs) + i: len(self.out_shapes) + o for i, o in job.aliases.items()})
            self.inputs += list(job.inputs)
            self.out_shapes += list(job.out_shapes)
            self.n_sems += job.n_sems

    def _each(self, e):
        for job, (i0, o0, s0) in zip(self.jobs, self.at):
            yield job, e.sub(i0, len(job.inputs), o0, len(job.out_shapes), s0)

    def split(self, outs):
        return [tuple(outs[o0:o0 + len(job.out_shapes)]) for job, (_, o0, _) in zip(self.jobs, self.at)]

    def start(self, e):
        for job, se in self._each(e):
            job.start(se)

    def finish(self, e):
        for job, se in self._each(e):
            job.finish(se)


def _call(body, name, out_shape, grid, in_specs, out_specs, args, scratch_shapes=(), job=None, aliases=None):
    aliases = dict(aliases or {})
    if job is None:
        return pl.pallas_call(body, name=name, out_shape=out_shape, grid=grid, in_specs=list(in_specs),
                              out_specs=out_specs, scratch_shapes=list(scratch_shapes), input_output_aliases=aliases,
                              compiler_params=_params())(*args), ()
    single = not isinstance(out_shape, (tuple, list))
    shapes = [out_shape] if single else list(out_shape)
    ospecs = [out_specs] if single else list(out_specs)
    n_in, n_out, n_scr = len(args), len(shapes), len(scratch_shapes)
    j_in, j_out = len(job.inputs), len(job.out_shapes)

    def hosted(*refs):
        ins, jins = refs[:n_in], refs[n_in:n_in + j_in]
        outs = refs[n_in + j_in:n_in + j_in + n_out]
        jouts = refs[n_in + j_in + n_out:n_in + j_in + n_out + j_out]
        rest = refs[n_in + j_in + n_out + j_out:]
        e = _Env(jins, jouts, rest[n_scr], rest[n_scr + 1])
        first = functools.reduce(jnp.logical_and, [pl.program_id(d) == 0 for d in range(len(grid))])
        last = functools.reduce(jnp.logical_and, [pl.program_id(d) == g - 1 for d, g in enumerate(grid)])

        @pl.when(first)
        def _():
            job.start(e)

        body(*ins, *outs, *rest[:n_scr])

        @pl.when(last)
        def _():
            job.finish(e)

    res = pl.pallas_call(
        hosted, name=name, out_shape=tuple(shapes + list(job.out_shapes)), grid=grid,
        in_specs=list(in_specs) + [ANY] * j_in, out_specs=tuple(ospecs + [ANY] * j_out),
        scratch_shapes=list(scratch_shapes) + [pltpu.SemaphoreType.DMA((job.n_sems,)),
                                               pltpu.SemaphoreType.DMA((job.n_sems,))],
        input_output_aliases={**aliases, **{n_in + i: n_out + o for i, o in job.aliases.items()}},
        compiler_params=_params())(*args, *job.inputs)
    return (res[0] if single else tuple(res[:n_out])), tuple(res[n_out:])


def _run_exchange(job, name):
    n_in, n_out = len(job.inputs), len(job.out_shapes)

    def body(*refs):
        e = _Env(refs[:n_in], refs[n_in:n_in + n_out], refs[n_in + n_out], refs[n_in + n_out + 1])
        job.start(e)
        job.finish(e)

    return pl.pallas_call(
        body, name=name, out_shape=tuple(job.out_shapes), in_specs=[ANY] * n_in, out_specs=tuple([ANY] * n_out),
        input_output_aliases=dict(job.aliases),
        scratch_shapes=[pltpu.SemaphoreType.DMA((job.n_sems,)), pltpu.SemaphoreType.DMA((job.n_sems,))],
    )(*job.inputs)


_M_TILES = (1024, 1408, 1280, 512, 256, 128)
_N_TILES = (1408, 1280, 1024, 640, 512, 256, 128)
MXU_FULL_ROWS = 1024
MATMUL_VMEM_BUDGET = 42 * 1024 * 1024
MXU_FLOPS_PER_HBM_BYTE = 500


def _matmul_tiles(m, n, k, sa, sb, so, has_add, tn_divides=0):
    best = None
    for tm in [c for c in _M_TILES if m % c == 0] or [m]:
        for tn in [c for c in _N_TILES if n % c == 0 and tn_divides % c == 0] or [n]:
            for nk in range(1, 17):
                tk = k // nk
                if k % nk or tk % LANE:
                    continue
                need = 2 * (tm * tk * sa + tk * tn * sb) + 2 * tm * tn * (so + (4 if has_add else 0))
                need += tm * tn * 4 if nk > 1 else 0
                fetched = tk * tn * sb + tm * tk * sa // (1 if nk > 1 else n // tn)
                if need > MATMUL_VMEM_BUDGET:
                    continue
                mxu_bound = fetched * MXU_FLOPS_PER_HBM_BYTE <= 2 * tm * tn * tk
                key = (mxu_bound, min(tm, MXU_FULL_ROWS), -nk, tn, tm)
                if best is None or key > best[0]:
                    best = (key, (tm, tn, tk))
    assert best is not None, (m, n, k)
    return best[1]


def _matmul(a, b, mode, name, add=None, out_dtype=F32, job=None, m_window=None, into=None, fused=None):
    if mode == "nn":
        (m, k), (k2, n) = a.shape, b.shape
    elif mode == "nt":
        (m, k), (n, k2) = a.shape, b.shape
    else:
        (k, m), (k2, n) = a.shape, b.shape
    assert k == k2, (a.shape, b.shape, mode)
    m0, m = m_window or (0, m)
    tiles, fuse_fn, out_dtypes = fused or ((), None, (out_dtype,))
    tiles = [x if isinstance(x, tuple) else (x, 0) for x in tiles]
    tm, tn, tk = _matmul_tiles(math.gcd(m, m0) if m0 else m, n, k, a.dtype.itemsize, b.dtype.itemsize,
                               sum(jnp.dtype(dt).itemsize for dt in out_dtypes)
                               + sum(x.dtype.itemsize for x, _ in tiles if x.shape[0] > 1),
                               add is not None, math.gcd(*[c0 for _, c0 in tiles], 0))
    nk, mb0 = k // tk, m0 // tm
    if mode == "nn":
        a_spec = pl.BlockSpec((tm, tk), lambda i, j, kk: (mb0 + i, kk))
        b_spec = pl.BlockSpec((tk, tn), lambda i, j, kk: (kk, j))
        dims = (((1,), (0,)), ((), ()))
    elif mode == "nt":
        a_spec = pl.BlockSpec((tm, tk), lambda i, j, kk: (mb0 + i, kk))
        b_spec = pl.BlockSpec((tn, tk), lambda i, j, kk: (j, kk))
        dims = (((1,), (1,)), ((), ()))
    else:
        a_spec = pl.BlockSpec((tk, tm), lambda i, j, kk: (kk, mb0 + i))
        b_spec = pl.BlockSpec((tk, tn), lambda i, j, kk: (kk, j))
        dims = (((0,), (0,)), ((), ()))
    out_rows, ob0 = (into[1], mb0) if into is not None else (m, 0)
    o_spec = pl.BlockSpec((tm, tn), lambda i, j, kk: (ob0 + i, j))
    has_add = add is not None
    begun = into is not None and into[0] is not None

    n_side, n_out = has_add + len(tiles), len(out_dtypes)

    def body(*refs):
        a_ref, b_ref = refs[:2]
        side = refs[2:2 + n_side]
        o_refs = refs[len(refs) - n_out - (nk > 1):len(refs) - (nk > 1)]
        part = lax.dot_general(a_ref[...].astype(BF16), b_ref[...].astype(BF16), dims, preferred_element_type=F32)

        def finish(r):
            if has_add:
                r = r + side[0][...]
            vals = fuse_fn(r, *[x[...] for x in side[has_add:]]) if fuse_fn else (r,)
            for o_ref, val, dt in zip(o_refs, vals, out_dtypes):
                o_ref[...] = val.astype(dt)

        if nk == 1:
            finish(part)
            return
        acc = refs[-1]
        kk = pl.program_id(2)

        @pl.when(kk == 0)
        def _():
            acc[...] = part

        @pl.when(kk > 0)
        def _():
            acc[...] += part

        @pl.when(kk == nk - 1)
        def _():
            finish(acc[...])

    def side_spec(x, c0):
        if x.shape[0] == 1:
            return pl.BlockSpec((1, tn), lambda i, j, kk: (0, c0 // tn + j))
        return pl.BlockSpec((tm, tn), lambda i, j, kk: (mb0 + i, c0 // tn + j))

    in_specs = [a_spec, b_spec] + ([side_spec(add, 0)] if has_add else []) + [side_spec(x, c0) for x, c0 in tiles]
    args = (a, b) + ((add,) if has_add else ()) + tuple(x for x, _ in tiles)
    aliases = None
    if begun:
        aliases = {len(args): 0}
        in_specs, args = in_specs + [ANY], args + (into[0],)
    shapes = tuple(jax.ShapeDtypeStruct((out_rows, n), dt) for dt in out_dtypes)
    res, extra = _call(body, name, shapes if fused else shapes[0], (m // tm, n // tn, nk), in_specs,
                       (o_spec,) * n_out if fused else o_spec, args, [pltpu.VMEM((tm, tn), F32)] if nk > 1 else [],
                       job, aliases)
    return res if job is None else (res, extra)


def _row_tile(rows, cols, budget_elems=512 * 1024):
    cands = [c for c in (1024, 704, 512, 352, 256, 128, 64, 32, 16) if c * cols <= budget_elems]
    return _pick(rows, cands or (16,))


_EW_COLS = (1280, 1408, 1024, 640, 512, 256, 128)


def _tile2d(rows, cols, max_elems):
    tc = _pick(cols, _EW_COLS)
    return _row_tile(rows, tc, max_elems), tc


def _rms_fwd(x, g, name):
    t, d = x.shape
    tr = _row_tile(t, d)

    def body(x_ref, g_ref, o_ref):
        xv = x_ref[...]
        rstd = lax.rsqrt(jnp.mean(xv * xv, axis=-1, keepdims=True) + EPS)
        o_ref[...] = (xv * rstd * g_ref[...]).astype(BF16)

    spec = pl.BlockSpec((tr, d), lambda i: (i, 0))
    return pl.pallas_call(body, name=name, out_shape=jax.ShapeDtypeStruct((t, d), BF16), grid=(t // tr,),
                          in_specs=[spec, pl.BlockSpec((1, d), lambda i: (0, 0))], out_specs=spec,
                          compiler_params=_params())(x, g)


def _rms_bwd(dxn, x, g, resid, name, job=None, mxu_copy=False):
    t, d = x.shape
    tr = _row_tile(t, d, 512 * 1024)

    def body(dxn_ref, x_ref, g_ref, r_ref, dx_ref, dg_ref, *dx16_ref):
        @pl.when(pl.program_id(0) == 0)
        def _():
            dg_ref[...] = jnp.zeros_like(dg_ref)

        xv = x_ref[...]
        rstd = lax.rsqrt(jnp.mean(xv * xv, axis=-1, keepdims=True) + EPS)
        xhat = xv * rstd
        dy = dxn_ref[...]
        dg_ref[...] += jnp.sum(dy * xhat, axis=0, keepdims=True)
        dxhat = dy * g_ref[...]
        dx = r_ref[...] + rstd * (dxhat - xhat * jnp.mean(dxhat * xhat, axis=-1, keepdims=True))
        dx_ref[...] = dx
        if mxu_copy:
            dx16_ref[0][...] = dx.astype(BF16)

    spec = pl.BlockSpec((tr, d), lambda i: (i, 0))
    vec = pl.BlockSpec((1, d), lambda i: (0, 0))
    shapes = (jax.ShapeDtypeStruct((t, d), F32), jax.ShapeDtypeStruct((1, d), F32))
    shapes += (jax.ShapeDtypeStruct((t, d), BF16),) if mxu_copy else ()
    res, extra = _call(body, name, shapes, (t // tr,), [spec, spec, vec, spec],
                       (spec, vec) + ((spec,) if mxu_copy else ()), (dxn, x, g, resid), (), job)
    return res if job is None else (res, extra)


def _swiglu_after_up(up, gate):
    return up, gate * _sigmoid(gate) * up


def _swiglu_bwd_after_dact(dact, gate, up):
    sg = _sigmoid(gate)
    return dact * up * (sg * (1.0 + gate * (1.0 - sg))), dact * (gate * sg)


def _merge_after_lru_proj(plru, pa, ga, gl, ba, bl):
    return plru, _sigmoid(ga + ba) * pa + _sigmoid(gl + bl) * plru


def _merge_bwd(dmerged, z, b_gates, p, z0, b0, dz, name, job=None):
    t, d = p.shape
    cw = _pick(math.gcd(z0, d), (512, 256, 128))
    tr = _row_tile(t, cw, 256 * 1024)
    oz, ob, nd = z0 // cw, b0 // cw, d // cw

    def body(dm_ref, g_ref, b_ref, p_ref, *rest):
        dp_ref, dg_ref, sum_ref = rest[-3:]

        @pl.when(pl.program_id(1) == 0)
        def _():
            sum_ref[...] = jnp.zeros_like(sum_ref)

        dm = dm_ref[...]
        sg = _sigmoid(g_ref[...] + b_ref[...])
        dp_ref[...] = (dm * sg).astype(BF16)
        dg = dm * p_ref[...] * (sg * (1.0 - sg))
        dg_ref[...] = dg.astype(BF16)
        sum_ref[...] += jnp.sum(dg, axis=0, keepdims=True)

    blk = pl.BlockSpec((tr, cw), lambda j, i: (i, j))
    at_z = pl.BlockSpec((tr, cw), lambda j, i: (i, oz + j))
    in_specs = [blk, at_z, pl.BlockSpec((1, cw), lambda j, i: (0, ob + j)), blk]
    args, aliases = (dmerged, z, b_gates, p), None
    if dz is not None:
        in_specs, args, aliases = in_specs + [ANY], args + (dz,), {4: 1}
    res, extra = _call(
        body, name, (jax.ShapeDtypeStruct((t, d), BF16), jax.ShapeDtypeStruct(z.shape, BF16),
                     jax.ShapeDtypeStruct((1, d), F32)), (nd, t // tr), in_specs,
        (blk, at_z, pl.BlockSpec((1, cw), lambda j, i: (0, j))), args, (), job, aliases)
    return res if job is None else (res, extra)


def _loss_head(y, target, name):
    t, d = y.shape
    tr = _row_tile(t, d, 512 * 1024)
    nt = t // tr

    def body(y_ref, t_ref, dy_ref, dy16_ref, loss_ref, acc):
        i = pl.program_id(0)

        @pl.when(i == 0)
        def _():
            acc[...] = jnp.zeros_like(acc)

        e = y_ref[...] - t_ref[...]
        dy = e * (1.0 / d)
        dy_ref[...] = dy
        dy16_ref[...] = dy.astype(BF16)
        acc[...] += jnp.sum(e * e, axis=0, keepdims=True)

        @pl.when(i == nt - 1)
        def _():
            loss_ref[...] = (0.5 / d) * jnp.sum(acc[...], axis=-1, keepdims=True)

    spec = pl.BlockSpec((tr, d), lambda i: (i, 0))
    return pl.pallas_call(
        body, name=name, out_shape=(jax.ShapeDtypeStruct((t, d), F32), jax.ShapeDtypeStruct((t, d), BF16),
                                    jax.ShapeDtypeStruct((1, 1), F32)),
        grid=(nt,), in_specs=[spec, spec], out_specs=(spec, spec, pl.BlockSpec((1, 1), lambda i: (0, 0))),
        scratch_shapes=[pltpu.VMEM((1, d), F32)], compiler_params=_params(),
    )(y, target)


def _adamw(w, g, m, v, name, pass_grad=False):
    r, c = w.shape
    tr, tc = _tile2d(r, c, 512 * 1024)
    c1 = 1.0 - ADAM_B1 ** ADAM_STEP
    c2 = 1.0 - ADAM_B2 ** ADAM_STEP

    def body(w_ref, g_ref, m_ref, v_ref, d_ref, nm_ref, nv_ref, *g_out):
        gv = g_ref[...]
        if pass_grad:
            g_out[0][...] = gv
        mn = ADAM_B1 * m_ref[...] + (1.0 - ADAM_B1) * gv
        vn = ADAM_B2 * v_ref[...] + (1.0 - ADAM_B2) * (gv * gv)
        d_ref[...] = -ADAM_LR * ((mn / c1) / (jnp.sqrt(vn / c2) + ADAM_EPS) + ADAM_WD * w_ref[...])
        nm_ref[...] = mn
        nv_ref[...] = vn

    spec = pl.BlockSpec((tr, tc), lambda i, j: (i, j))
    shp = jax.ShapeDtypeStruct((r, c), F32)
    n_out = 4 if pass_grad else 3
    return pl.pallas_call(body, name=name, out_shape=(shp,) * n_out, grid=(r // tr, c // tc), in_specs=[spec] * 4,
                          out_specs=(spec,) * n_out, compiler_params=_params())(w, g, m, v)


def _swap_halves(v, half):
    n = v.shape[-1]
    lane = lax.broadcasted_iota(jnp.int32, v.shape, 1)
    return jnp.where(lane < half, pltpu.roll(v, n - half, 1),
                     jnp.where(lane < 2 * half, pltpu.roll(v, half, 1), 0.0))


def _rope(y, c, s, half):
    return y * c + _swap_halves(y, half) * s


def _rope_bwd(dout, c, s, half):
    return dout * c + _swap_halves(dout * s, half)


def _stack_heads(ref, grp, hd):
    return jnp.concatenate([ref[:, g * hd:(g + 1) * hd] for g in range(grp)], axis=0)


def _softmax_with_sinks(s, sink_ref, first, grp, i):
    rows = s.shape[0]
    qi = lax.broadcasted_iota(jnp.int32, s.shape, 0) & (BLK - 1)
    kj = lax.broadcasted_iota(jnp.int32, s.shape, 1)
    rel = qi + BLK - kj
    s = jnp.where((rel >= 0) & (rel < WINDOW) & ((kj >= BLK) | (i > 0)), s, NEG)
    head = lax.broadcasted_iota(jnp.int32, (rows, 1), 0) // BLK
    sk = jnp.zeros((rows, 1), F32)
    for g in range(grp):
        sk = jnp.where(head == g, sink_ref[first + g], sk)
    mx = jnp.maximum(jnp.max(s, axis=-1, keepdims=True), sk)
    p = jnp.exp(s - mx)
    esk = jnp.exp(sk - mx)
    inv_den = 1.0 / (jnp.sum(p, axis=-1, keepdims=True) + esk)
    return p * inv_den, esk * inv_den, head


def _norm_fwd(xraw, g):
    rstd = lax.rsqrt(jnp.mean(xraw * xraw, axis=-1, keepdims=True) + EPS)
    xhat = xraw * rstd
    return xhat, rstd, xhat * g


def _norm_bwd(dy, xhat, rstd, g):
    dxhat = dy * g
    dx = rstd * (dxhat - xhat * jnp.mean(dxhat * xhat, axis=-1, keepdims=True))
    return dx, jnp.sum(dy * xhat, axis=0, keepdims=True)


def _attn_specs(nb, grp, hd, kv, clamp):
    qo, ko, vo = 0, (kv * grp), (kv * grp + kv)
    cur = (lambda i: jnp.minimum(i, nb - 1)) if clamp else (lambda i: i)
    prev = lambda i: jnp.maximum(cur(i) - 1, 0)
    zq = pl.BlockSpec((BLK, grp * hd), lambda h, i: (cur(i), h))
    kc = pl.BlockSpec((BLK, hd), lambda h, i: (cur(i), ko + h))
    kp = pl.BlockSpec((BLK, hd), lambda h, i: (prev(i), ko + h))
    vc = pl.BlockSpec((BLK, hd), lambda h, i: (cur(i), vo + h))
    vp = pl.BlockSpec((BLK, hd), lambda h, i: (prev(i), vo + h))
    tc = pl.BlockSpec((BLK, hd), lambda h, i: (cur(i), 0))
    tp = pl.BlockSpec((BLK, hd), lambda h, i: (prev(i), 0))
    gs = pl.BlockSpec((1, hd), lambda h, i: (0, 0))
    return zq, kc, kp, vc, vp, tc, tp, gs


def _attn_fwd(z, cos_t, sin_t, qg, kg, sinks, kv, grp, hd, name, job=None):
    t = z.shape[0]
    nb = t // BLK
    half = hd // 8
    scale = 1.0 / math.sqrt(hd)
    zq, kc, kp, vc, vp, tc, tp, gs = _attn_specs(nb, grp, hd, kv, False)

    def body(sink_ref, zq_ref, kc_ref, kp_ref, vc_ref, vp_ref, cc_ref, sc_ref, cp_ref, sp_ref, qg_ref, kg_ref, o_ref):
        h, i = pl.program_id(0), pl.program_id(1)

        def normrope(xraw, g, c, s):
            return _rope(_norm_fwd(xraw, g)[2], c, s, half)

        cc, sc = cc_ref[...], sc_ref[...]
        kcur = normrope(kc_ref[...], kg_ref[...], cc, sc)
        kprev = normrope(kp_ref[...], kg_ref[...], cp_ref[...], sp_ref[...])
        kk = jnp.concatenate([kprev, kcur], axis=0).astype(BF16)
        vv = jnp.concatenate([vp_ref[...], vc_ref[...]], axis=0).astype(BF16)
        for g in range(grp):
            q = normrope(zq_ref[:, g * hd:(g + 1) * hd], qg_ref[...], cc, sc).astype(BF16)
            s = lax.dot_general(q, kk, (((1,), (1,)), ((), ())), preferred_element_type=F32) * scale
            p, _, _ = _softmax_with_sinks(s, sink_ref, h * grp + g, 1, i)
            o_ref[:, g * hd:(g + 1) * hd] = jnp.dot(p.astype(BF16), vv, preferred_element_type=F32).astype(BF16)

    res, extra = _call(
        body, name, jax.ShapeDtypeStruct((t, kv * grp * hd), BF16), (kv, nb),
        [pl.BlockSpec(memory_space=pltpu.SMEM), zq, kc, kp, vc, vp, tc, tc, tp, tp, gs, gs],
        pl.BlockSpec((BLK, grp * hd), lambda h, i: (i, h)),
        (sinks, z, z, z, z, z, cos_t, sin_t, cos_t, sin_t, qg, kg), (), job)
    return res if job is None else (res, extra)


def _attn_bwd(dattn, z, cos_t, sin_t, qg, kg, sinks, kv, grp, hd, name, job=None):
    t = z.shape[0]
    nb = t // BLK
    half = hd // 8
    scale = 1.0 / math.sqrt(hd)
    zq, kc, kp, vc, vp, tc, tp, gs = _attn_specs(nb, grp, hd, kv, True)

    def body(sink_ref, zq_ref, kc_ref, kp_ref, vc_ref, vp_ref, cc_ref, sc_ref, cp_ref, sp_ref, qg_ref, kg_ref, do_ref,
             dq_ref, dk_ref, dv_ref, dqg_ref, dkg_ref, dsk_ref, dk_carry, dv_carry):
        h, i = pl.program_id(0), pl.program_id(1)
        lane1 = lax.broadcasted_iota(jnp.int32, (1, LANE), 1)

        @pl.when((h == 0) & (i == 0))
        def _():
            dqg_ref[...] = jnp.zeros_like(dqg_ref)
            dkg_ref[...] = jnp.zeros_like(dkg_ref)
            dsk_ref[...] = jnp.zeros_like(dsk_ref)

        @pl.when(i == 0)
        def _():
            dk_carry[...] = jnp.zeros_like(dk_carry)
            dv_carry[...] = jnp.zeros_like(dv_carry)

        @pl.when(i < nb)
        def _():
            cc, sc, cp, sp = cc_ref[...], sc_ref[...], cp_ref[...], sp_ref[...]
            qgv, kgv = qg_ref[...], kg_ref[...]
            xh_kc, rs_kc, y_kc = _norm_fwd(kc_ref[...], kgv)
            xh_kp, rs_kp, y_kp = _norm_fwd(kp_ref[...], kgv)
            kk = jnp.concatenate([_rope(y_kp, cp, sp, half), _rope(y_kc, cc, sc, half)], axis=0).astype(BF16)
            vv = jnp.concatenate([vp_ref[...], vc_ref[...]], axis=0).astype(BF16)
            cq, sq = jnp.concatenate([cc] * grp, axis=0), jnp.concatenate([sc] * grp, axis=0)
            xh_q, rs_q, y_q = _norm_fwd(_stack_heads(zq_ref, grp, hd), qgv)
            q = _rope(y_q, cq, sq, half).astype(BF16)
            s = lax.dot_general(q, kk, (((1,), (1,)), ((), ())), preferred_element_type=F32) * scale
            p, psink, head = _softmax_with_sinks(s, sink_ref, h * grp, grp, i)
            dog = _stack_heads(do_ref, grp, hd).astype(BF16)
            dp = lax.dot_general(dog, vv, (((1,), (1,)), ((), ())), preferred_element_type=F32)
            rsum = jnp.sum(p * dp, axis=-1, keepdims=True)
            ds = (p * (dp - rsum) * scale).astype(BF16)
            dsink = -psink * rsum
            dsk = jnp.zeros((1, LANE), F32)
            for g in range(grp):
                dsk = dsk + jnp.where(lane1 == h * grp + g,
                                      jnp.sum(jnp.where(head == g, dsink, 0.0), axis=0, keepdims=True), 0.0)
            dqn = jnp.dot(ds, kk, preferred_element_type=F32)
            dkk = lax.dot_general(ds, q, (((0,), (0,)), ((), ())), preferred_element_type=F32)
            dvv = lax.dot_general(p.astype(BF16), dog, (((0,), (0,)), ((), ())), preferred_element_type=F32)
            dxq, dqg = _norm_bwd(_rope_bwd(dqn, cq, sq, half), xh_q, rs_q, qgv)
            dxq = dxq.astype(BF16)
            for g in range(grp):
                dq_ref[:, g * hd:(g + 1) * hd] = dxq[g * BLK:(g + 1) * BLK]
            dkp_raw, dg_kp = _norm_bwd(_rope_bwd(dkk[:BLK], cp, sp, half), xh_kp, rs_kp, kgv)
            dkc_raw, dg_kc = _norm_bwd(_rope_bwd(dkk[BLK:], cc, sc, half), xh_kc, rs_kc, kgv)
            dk_ref[...] = (dk_carry[...] + dkp_raw).astype(BF16)
            dv_ref[...] = (dv_carry[...] + dvv[:BLK]).astype(BF16)
            dk_carry[...] = dkc_raw
            dv_carry[...] = dvv[BLK:]
            dqg_ref[...] += dqg
            dkg_ref[...] += dg_kp + dg_kc
            dsk_ref[...] += dsk

        @pl.when(i == nb)
        def _():
            dk_ref[...] = dk_carry[...].astype(BF16)
            dv_ref[...] = dv_carry[...].astype(BF16)

    kvw = kv * hd
    vec = pl.BlockSpec((1, hd), lambda h, i: (0, 0))
    shifted = pl.BlockSpec((BLK, hd), lambda h, i: (jnp.maximum(i - 1, 0), h))
    res, extra = _call(
        body, name,
        (jax.ShapeDtypeStruct((t, kv * grp * hd), BF16), jax.ShapeDtypeStruct((t, kvw), BF16),
         jax.ShapeDtypeStruct((t, kvw), BF16), jax.ShapeDtypeStruct((1, hd), F32),
         jax.ShapeDtypeStruct((1, hd), F32), jax.ShapeDtypeStruct((1, LANE), F32)),
        (kv, nb + 1),
        [pl.BlockSpec(memory_space=pltpu.SMEM), zq, kc, kp, vc, vp, tc, tc, tp, tp, gs, gs,
         pl.BlockSpec((BLK, grp * hd), lambda h, i: (jnp.minimum(i, nb - 1), h))],
        (pl.BlockSpec((BLK, grp * hd), lambda h, i: (jnp.minimum(i, nb - 1), h)), shifted, shifted, vec, vec,
         pl.BlockSpec((1, LANE), lambda h, i: (0, 0))),
        (sinks, z, z, z, z, z, cos_t, sin_t, cos_t, sin_t, qg, kg, dattn),
        [pltpu.VMEM((BLK, hd), F32), pltpu.VMEM((BLK, hd), F32)], job)
    return res if job is None else (res, extra)


def _window(rows, cb, c0, row_of, col_of):
    assert rows % SUBLANE == 0 and cb % LANE == 0 and c0 % LANE == 0, (rows, cb, c0)
    return pl.BlockSpec((pl.Element(rows), pl.Element(cb)),
                        lambda *g: (pl.multiple_of(row_of(*g) * rows, SUBLANE), pl.multiple_of(c0 + col_of(*g) * cb, LANE)))


def _conv_fwd(z, c0, c, w, b, name):
    t = z.shape[0]
    taps = w.shape[0]
    cb = _pick(c, (1408, 1024, 512, 256, 128))
    tr = _row_tile(t, cb, 256 * 1024)
    hb = tr // SUBLANE

    def body(u_ref, halo_ref, w_ref, b_ref, o_ref):
        i = pl.program_id(0)
        x = u_ref[...]
        acc = b_ref[...] + w_ref[taps - 1:taps, :] * x
        for k in range(taps - 1):
            acc = acc + w_ref[k:k + 1, :] * pltpu.roll(x, taps - 1 - k, 0)
        o_ref[...] = acc
        row = lax.broadcasted_iota(jnp.int32, (SUBLANE, cb), 0)
        hp = jnp.where(i > 0, halo_ref[...], 0.0)
        x8 = u_ref[0:SUBLANE, :]
        acc8 = b_ref[...] + w_ref[taps - 1:taps, :] * x8
        for k in range(taps - 1):
            s = taps - 1 - k
            acc8 = acc8 + w_ref[k:k + 1, :] * jnp.where(row < s, pltpu.roll(hp, s, 0), pltpu.roll(x8, s, 0))
        o_ref[0:SUBLANE, :] = acc8

    blk = pl.BlockSpec((tr, cb), lambda i, j: (i, j))
    return pl.pallas_call(
        body, name=name, out_shape=jax.ShapeDtypeStruct((t, c), F32), grid=(t // tr, c // cb),
        in_specs=[_window(tr, cb, c0, lambda i, j: i, lambda i, j: j),
                  _window(SUBLANE, cb, c0, lambda i, j: jnp.maximum(i * hb - 1, 0), lambda i, j: j),
                  pl.BlockSpec((taps, cb), lambda i, j: (0, j)), pl.BlockSpec((1, cb), lambda i, j: (0, j))],
        out_specs=blk, compiler_params=_params(),
    )(z, z, w, b)


def _conv_bwd(duc, z, c0, w, dz, name):
    t, c = duc.shape
    taps = w.shape[0]
    cb = _pick(c, (1408, 1024, 512, 256, 128))
    tr = _row_tile(t, cb, 256 * 1024)
    hb, nt = tr // SUBLANE, t // tr

    def body(g_ref, gnext_ref, u_ref, uprev_ref, w_ref, dz_ref, du16_ref, dw_ref, db_ref, du_ref):
        i = pl.program_id(1)

        @pl.when(i == 0)
        def _():
            dw_ref[...] = jnp.zeros_like(dw_ref)
            db_ref[...] = jnp.zeros_like(db_ref)

        row = lax.broadcasted_iota(jnp.int32, (SUBLANE, cb), 0)
        g, x = g_ref[...], u_ref[...]
        du = w_ref[taps - 1:taps, :] * g
        for k in range(taps - 1):
            du = du + w_ref[k:k + 1, :] * pltpu.roll(g, tr - (taps - 1 - k), 0)
        du_ref[...] = du
        hn = jnp.where(i < nt - 1, gnext_ref[...], 0.0)
        g8 = g_ref[tr - SUBLANE:tr, :]
        du8 = w_ref[taps - 1:taps, :] * g8
        for k in range(taps - 1):
            s = taps - 1 - k
            du8 = du8 + w_ref[k:k + 1, :] * jnp.where(row >= SUBLANE - s, pltpu.roll(hn, SUBLANE - s, 0),
                                                     pltpu.roll(g8, SUBLANE - s, 0))
        du_ref[tr - SUBLANE:tr, :] = du8
        du16_ref[...] = du_ref[...].astype(BF16)

        hp = jnp.where(i > 0, uprev_ref[...], 0.0)
        xl8, gf8 = u_ref[tr - SUBLANE:tr, :], g_ref[0:SUBLANE, :]
        db_ref[...] += jnp.sum(g, axis=0, keepdims=True)
        dw_ref[taps - 1:taps, :] += jnp.sum(g * x, axis=0, keepdims=True)
        for k in range(taps - 1):
            s = taps - 1 - k
            fix = jnp.where(row < s, pltpu.roll(hp, s, 0) - pltpu.roll(xl8, s, 0), 0.0)
            dw_ref[k:k + 1, :] += (jnp.sum(g * pltpu.roll(x, s, 0), axis=0, keepdims=True)
                                   + jnp.sum(gf8 * fix, axis=0, keepdims=True))

    blk = pl.BlockSpec((tr, cb), lambda j, i: (i, j))
    nh = t // SUBLANE
    return pl.pallas_call(
        body, name=name,
        out_shape=(jax.ShapeDtypeStruct(dz.shape, BF16), jax.ShapeDtypeStruct((taps, c), F32),
                   jax.ShapeDtypeStruct((1, c), F32)),
        grid=(c // cb, nt),
        in_specs=[blk, pl.BlockSpec((SUBLANE, cb), lambda j, i: (jnp.minimum((i + 1) * hb, nh - 1), j)),
                  _window(tr, cb, c0, lambda j, i: i, lambda j, i: j),
                  _window(SUBLANE, cb, c0, lambda j, i: jnp.maximum(i * hb - 1, 0), lambda j, i: j),
                  pl.BlockSpec((taps, cb), lambda j, i: (0, j)), ANY],
        out_specs=(_window(tr, cb, c0, lambda j, i: i, lambda j, i: j), pl.BlockSpec((taps, cb), lambda j, i: (0, j)),
                   pl.BlockSpec((1, cb), lambda j, i: (0, j))),
        scratch_shapes=[pltpu.VMEM((tr, cb), F32)], input_output_aliases={5: 0}, compiler_params=_params(),
    )(duc, duc, z, z, w, dz)


def _gates_fwd(uc, wr, wi, gw, name):
    t, c = uc.shape
    n, bw, _ = wr.shape
    per, ng = gw // bw, c // gw
    tr = _pick(t, (512, 256, 128))

    def body(u_ref, wr_ref, wi_ref, r_ref, i_ref):
        for b in range(per):
            cols = slice(b * bw, (b + 1) * bw)
            a = u_ref[:, cols].astype(BF16)
            r_ref[:, cols] = jnp.dot(a, wr_ref[b].astype(BF16), preferred_element_type=F32)
            i_ref[:, cols] = jnp.dot(a, wi_ref[b].astype(BF16), preferred_element_type=F32)

    blk = pl.BlockSpec((tr, gw), lambda h, i: (i, h))
    wsp = pl.BlockSpec((per, bw, bw), lambda h, i: (h, 0, 0))
    shp = jax.ShapeDtypeStruct((t, c), F32)
    return pl.pallas_call(body, name=name, out_shape=(shp, shp), grid=(ng, t // tr), in_specs=[blk, wsp, wsp],
                          out_specs=(blk, blk), compiler_params=_params())(uc, wr, wi)


def _gates_bwd_x(duc, drp, dip, wr, wi, gw, name):
    t, c = duc.shape
    n, bw, _ = wr.shape
    per, ng = gw // bw, c // gw
    tr = _pick(t, (512, 256, 128))
    dims = (((1,), (1,)), ((), ()))

    def body(d_ref, r_ref, i_ref, wr_ref, wi_ref, o_ref):
        for b in range(per):
            cols = slice(b * bw, (b + 1) * bw)
            o_ref[:, cols] = (
                d_ref[:, cols]
                + lax.dot_general(r_ref[:, cols].astype(BF16), wr_ref[b].astype(BF16), dims, preferred_element_type=F32)
                + lax.dot_general(i_ref[:, cols].astype(BF16), wi_ref[b].astype(BF16), dims, preferred_element_type=F32))

    blk = pl.BlockSpec((tr, gw), lambda h, i: (i, h))
    wsp = pl.BlockSpec((per, bw, bw), lambda h, i: (h, 0, 0))
    return pl.pallas_call(body, name=name, out_shape=jax.ShapeDtypeStruct((t, c), F32), grid=(ng, t // tr),
                          in_specs=[blk, blk, blk, wsp, wsp], out_specs=blk, compiler_params=_params())(duc, drp, dip, wr, wi)


def _gates_bwd_w(uc, dpre, n, bw, gw, name):
    t, c = uc.shape
    per, ng = gw // bw, c // gw
    tk = _pick(t, (512, 256, 128))
    dims = (((0,), (0,)), ((), ()))

    def body(u_ref, d_ref, o_ref):
        @pl.when(pl.program_id(1) == 0)
        def _():
            o_ref[...] = jnp.zeros_like(o_ref)

        for b in range(per):
            cols = slice(b * bw, (b + 1) * bw)
            o_ref[b] += lax.dot_general(u_ref[:, cols].astype(BF16), d_ref[:, cols].astype(BF16), dims,
                                        preferred_element_type=F32)

    blk = pl.BlockSpec((tk, gw), lambda h, i: (i, h))
    return pl.pallas_call(body, name=name, out_shape=jax.ShapeDtypeStruct((n, bw, bw), F32), grid=(ng, t // tk),
                          in_specs=[blk, blk], out_specs=pl.BlockSpec((per, bw, bw), lambda h, i: (h, 0, 0)),
                          compiler_params=_params())(uc, dpre)


def _softplus(x):
    return jnp.maximum(x, 0.0) + jnp.log(1.0 + jnp.exp(-jnp.abs(x)))


_GELU_C = math.sqrt(2.0 / math.pi)


def _gelu_parts(x):
    inner = _GELU_C * (x + 0.044715 * (x * x * x))
    th = jnp.tanh(inner)
    gelu = 0.5 * x * (1.0 + th)
    dgelu = 0.5 * (1.0 + th) + 0.5 * x * (1.0 - th * th) * (_GELU_C * (1.0 + 3.0 * 0.044715 * (x * x)))
    return gelu, dgelu


def _lru_gate_values(rpre, ipre, br, bi, sp):
    r = _sigmoid(rpre + br)
    ig = _sigmoid(ipre + bi)
    log_a = -LRU_C * r * sp
    a = jnp.exp(log_a)
    e2 = jnp.tanh(-log_a) * (1.0 + a * a)
    inv = lax.rsqrt(jnp.maximum(e2, 1e-30))
    return r, ig, a, e2 * inv, inv


def _lru_fwd(uc, rpre, ipre, z, gr0, br, bi, lam, name, job=None):
    t, c = uc.shape
    cb = _pick(c, (1408, 1024, 512, 256, 128))
    tb = _pick(t, (512, 256, 128))
    ntile = tb // SUBLANE

    def body(uc_ref, r_ref, i_ref, gr_ref, br_ref, bi_ref, lam_ref, h_ref, rec16_ref, carry, rec_ref):
        @pl.when(pl.program_id(1) == 0)
        def _():
            carry[...] = jnp.zeros_like(carry)

        sp = _softplus(-lam_ref[...])
        br, bi = br_ref[...], bi_ref[...]
        row = lax.broadcasted_iota(jnp.int32, (SUBLANE, cb), 0)

        def tile(k, c_in):
            sl = pl.ds(pl.multiple_of(k * SUBLANE, SUBLANE), SUBLANE)
            ucv = uc_ref[sl, :]
            _, ig, a, mult, _ = _lru_gate_values(r_ref[sl, :], i_ref[sl, :], br, bi, sp)
            b = mult * (ig * ucv)
            for d in (1, 2, 4):
                a_s = jnp.where(row >= d, pltpu.roll(a, d, 0), 1.0)
                b_s = jnp.where(row >= d, pltpu.roll(b, d, 0), 0.0)
                b = a * b_s + b
                a = a * a_s
            hv = b + a * c_in
            h_ref[sl, :] = hv
            rec_ref[sl, :] = hv * _gelu_parts(gr_ref[sl, :])[0]
            return hv[SUBLANE - 1:SUBLANE, :]

        c_out = lax.fori_loop(0, ntile, tile, carry[0:1, :])
        carry[...] = jnp.broadcast_to(c_out, (SUBLANE, cb))
        rec16_ref[...] = rec_ref[...].astype(BF16)

    blk = pl.BlockSpec((tb, cb), lambda j, i: (i, j))
    vec = pl.BlockSpec((1, cb), lambda j, i: (0, j))
    res, extra = _call(body, name, (jax.ShapeDtypeStruct((t, c), F32), jax.ShapeDtypeStruct((t, c), BF16)),
                       (c // cb, t // tb),
                       [blk, blk, blk, _window(tb, cb, gr0, lambda j, i: i, lambda j, i: j), vec, vec, vec], (blk, blk),
                       (uc, rpre, ipre, z, br, bi, lam), [pltpu.VMEM((SUBLANE, cb), F32), pltpu.VMEM((tb, cb), F32)], job)
    return res if job is None else (res, extra)


def _lru_bwd(drec, hst, uc, rpre, ipre, z, gr0, br, bi, lam, dz, name, job=None):
    t, c = uc.shape
    cb = _pick(c, (1408, 1024, 512, 256, 128))
    tb = _pick(t, (256, 128))
    ntile, nt, hb = tb // SUBLANE, t // tb, tb // SUBLANE

    def body(drec_ref, h_ref, hprev_ref, uc_ref, r_ref, i_ref, gr_ref, br_ref, bi_ref, lam_ref, dz_ref,
             dgr16_ref, drp_ref, dip_ref, duc_ref, dlam_ref, dbr_ref, dbi_ref, carry, dgr_ref):
        step = pl.program_id(1)
        first_block = step == nt - 1

        @pl.when(step == 0)
        def _():
            carry[...] = jnp.zeros_like(carry)
            dlam_ref[...] = jnp.zeros_like(dlam_ref)
            dbr_ref[...] = jnp.zeros_like(dbr_ref)
            dbi_ref[...] = jnp.zeros_like(dbi_ref)

        lam = lam_ref[...]
        sp = _softplus(-lam)
        br, bi = br_ref[...], bi_ref[...]
        row = lax.broadcasted_iota(jnp.int32, (SUBLANE, cb), 0)
        halo = jnp.where(first_block, 0.0, hprev_ref[...])

        def tile(kk, state):
            c_p, acc_sp, acc_br, acc_bi = state
            k = ntile - 1 - kk
            sl = pl.ds(pl.multiple_of(k * SUBLANE, SUBLANE), SUBLANE)
            slp = pl.ds(pl.multiple_of(jnp.maximum(k - 1, 0) * SUBLANE, SUBLANE), SUBLANE)
            ucv = uc_ref[sl, :]
            r, ig, a, mult, inv_mult = _lru_gate_values(r_ref[sl, :], i_ref[sl, :], br, bi, sp)
            hv = h_ref[sl, :]
            below = jnp.where(k > 0, h_ref[slp, :], halo)
            hprev = jnp.where(row == 0, pltpu.roll(below, 1, 0), pltpu.roll(hv, 1, 0))
            gelu, dgelu = _gelu_parts(gr_ref[sl, :])
            drec = drec_ref[sl, :]
            dh = drec * gelu
            dgr_ref[sl, :] = drec * hv * dgelu
            pa, pb = a, a * dh
            for d in (1, 2, 4):
                a_s = jnp.where(row < SUBLANE - d, pltpu.roll(pa, SUBLANE - d, 0), 1.0)
                b_s = jnp.where(row < SUBLANE - d, pltpu.roll(pb, SUBLANE - d, 0), 0.0)
                pb = pa * b_s + pb
                pa = pa * a_s
            pv = pb + pa * c_p
            gt = dh + jnp.where(row == SUBLANE - 1, c_p, pltpu.roll(pv, SUBLANE - 1, 0))
            da = gt * hprev
            duc_ref[sl, :] = gt * mult * ig
            dmult = gt * ig * ucv
            dig = gt * mult * ucv
            dla = da * a - jnp.where(mult > 0.0, dmult * (a * a) * inv_mult, 0.0)
            drp = dla * (-LRU_C * sp) * (r * (1.0 - r))
            dip = dig * (ig * (1.0 - ig))
            drp_ref[sl, :] = drp
            dip_ref[sl, :] = dip
            return pv[0:1, :], acc_sp + dla * (-LRU_C * r), acc_br + drp, acc_bi + dip

        zero = jnp.zeros((SUBLANE, cb), F32)
        c_out, acc_sp, acc_br, acc_bi = lax.fori_loop(0, ntile, tile, (carry[0:1, :], zero, zero, zero))
        carry[...] = jnp.broadcast_to(c_out, (SUBLANE, cb))
        dlam_ref[...] += jnp.sum(acc_sp, axis=0, keepdims=True) * (-_sigmoid(-lam))
        dbr_ref[...] += jnp.sum(acc_br, axis=0, keepdims=True)
        dbi_ref[...] += jnp.sum(acc_bi, axis=0, keepdims=True)
        dgr16_ref[...] = dgr_ref[...].astype(BF16)

    blk = pl.BlockSpec((tb, cb), lambda j, i: (nt - 1 - i, j))
    vec = pl.BlockSpec((1, cb), lambda j, i: (0, j))
    halo_spec = pl.BlockSpec((SUBLANE, cb), lambda j, i: (jnp.maximum((nt - 1 - i) * hb - 1, 0), j))
    big, small = jax.ShapeDtypeStruct((t, c), F32), jax.ShapeDtypeStruct((1, c), F32)
    at_gr = _window(tb, cb, gr0, lambda j, i: nt - 1 - i, lambda j, i: j)
    res, extra = _call(
        body, name, (jax.ShapeDtypeStruct(dz.shape, BF16), big, big, big, small, small, small), (c // cb, nt),
        [blk, blk, halo_spec, blk, blk, blk, at_gr, vec, vec, vec, ANY], (at_gr, blk, blk, blk, vec, vec, vec),
        (drec, hst, hst, uc, rpre, ipre, z, br, bi, lam, dz),
        [pltpu.VMEM((SUBLANE, cb), F32), pltpu.VMEM((tb, cb), F32)], job, {10: 0})
    return res if job is None else (res, extra)


def _shard_region(ref, kind, chip, half, rh, width):
    if kind == "col":
        return ref.at[pl.ds(half * rh, rh), pl.ds(chip * width, width)]
    return ref.at[pl.ds(chip * (2 * rh) + half * rh, rh), :]


class _AllGather(_Exchange):
    def __init__(self, fulls, kinds):
        self.inputs, self.kinds = list(fulls), kinds
        self.out_shapes = [jax.ShapeDtypeStruct(f.shape, f.dtype) for f in fulls]
        self.aliases = {a: a for a in range(len(fulls))}
        self.n_sems = 6 * len(fulls)
        self.geo = [(f.shape[0] // 2, f.shape[1] // N_CHIPS) if k == "col" else (f.shape[0] // (2 * N_CHIPS), f.shape[1])
                    for f, k in zip(fulls, kinds)]

    def _region(self, ref, a, chip, half):
        return _shard_region(ref, self.kinds[a], chip, half, *self.geo[a])

    def _ici(self, e, a, k, chip):
        cx, cy = e.chips[k]
        return e.copy(self._region(e.ins[a], a, chip, e.c), self._region(e.outs[a], a, chip, e.c), a * 6 + k,
                      (cx, cy, e.c))

    def _d2d(self, e, a, k, half):
        cx, cy = e.chips[k]
        region = self._region(e.outs[a], a, 2 * cx + cy, half)
        return e.copy(region, region, a * 6 + 3 + k, e.sibling)

    def start(self, e):
        for a in range(len(self.inputs)):
            for k in range(3):
                self._ici(e, a, k, e.me).start()

    def finish(self, e):
        n = len(self.inputs)
        for a in range(n):
            for k, (cx, cy) in enumerate(e.chips):
                self._ici(e, a, k, 2 * cx + cy).wait_recv()
                self._d2d(e, a, k, e.c).start()
        for a in range(n):
            for k in range(3):
                self._d2d(e, a, k, 1 - e.c).wait_recv()
        for a in range(n):
            for k in range(3):
                self._ici(e, a, k, e.me).wait_send()
                self._d2d(e, a, k, e.c).wait_send()


class _SiblingExchange(_Exchange):
    def __init__(self, grads):
        self.inputs = list(grads)
        self.out_shapes = [jax.ShapeDtypeStruct((g.shape[0],) + g.shape[2:], g.dtype) for g in grads]
        self.n_sems = len(grads)

    def _copy(self, e, a):
        return e.copy(e.ins[a].at[:, 1 - e.c], e.outs[a], a, e.sibling)

    def start(self, e):
        for a in range(len(self.inputs)):
            self._copy(e, a).start()

    def finish(self, e):
        for a in range(len(self.inputs)):
            self._copy(e, a).wait()


def _piece(ref, kind, chip, width):
    if kind == "col":
        return ref.at[0, :, pl.ds(chip * width, width)]
    return ref.at[chip]


class _ChipExchange(_Exchange):
    def __init__(self, sums, kinds):
        self.inputs, self.kinds = list(sums), kinds
        self.widths = [s.shape[2] // N_CHIPS if k == "col" else s.shape[2] for s, k in zip(sums, kinds)]
        self.out_shapes = [jax.ShapeDtypeStruct((3, s.shape[1], w), s.dtype) for s, w in zip(sums, self.widths)]
        self.n_sems = 3 * len(sums)

    def _copy(self, e, a, k, chip):
        cx, cy = e.chips[k]
        return e.copy(_piece(e.ins[a], self.kinds[a], chip, self.widths[a]), e.outs[a].at[k], a * 3 + k, (cx, cy, e.c))

    def start(self, e):
        for a in range(len(self.inputs)):
            for k, (cx, cy) in enumerate(e.chips):
                self._copy(e, a, k, 2 * cx + cy).start()

    def finish(self, e):
        for a in range(len(self.inputs)):
            for k, (cx, cy) in enumerate(e.chips):
                self._copy(e, a, k, 2 * cx + cy).wait()


class _FinishExchange(_Exchange):
    def __init__(self, finals, to_all):
        self.inputs, self.to_all = list(finals), list(to_all)
        self.out_shapes = [jax.ShapeDtypeStruct(f.shape, f.dtype) for f in finals]
        self.aliases = {a: a for a in range(len(finals))}
        self.first_sem, self.n_sems = [], 0
        for all8 in self.to_all:
            self.first_sem.append(self.n_sems)
            self.n_sems += 7 if all8 else 1
        self.rel = [(fx, fy, fc) for fx in (0, 1) for fy in (0, 1) for fc in (0, 1)][1:]

    def _copies(self, e, mine):
        for a, all8 in enumerate(self.to_all):
            src = e.ins[a] if mine else e.outs[a]
            if not all8:
                rh = self.inputs[a].shape[0] // 2
                rows = pl.ds((e.c if mine else 1 - e.c) * rh, rh)
                yield e.copy(src.at[rows, :], e.outs[a].at[rows, :], self.first_sem[a], e.sibling)
                continue
            rh = self.inputs[a].shape[0] // (2 * N_CHIPS)
            for r, (fx, fy, fc) in enumerate(self.rel):
                px, py, pc = (1 - e.x if fx else e.x), (1 - e.y if fy else e.y), (1 - e.c if fc else e.c)
                rows = pl.ds(((2 * e.me + e.c) if mine else (2 * (2 * px + py) + pc)) * rh, rh)
                yield e.copy(src.at[rows, :], e.outs[a].at[rows, :], self.first_sem[a] + r, (px, py, pc))

    def start(self, e):
        for cp in self._copies(e, True):
            cp.start()

    def finish(self, e):
        for cp in self._copies(e, False):
            cp.wait_recv()
        for cp in self._copies(e, True):
            cp.wait_send()


def _cast_into_full(w, kind, idx, name):
    r, c = w.shape
    tr = _row_tile(r, c)
    nrb = r // tr

    def body(idx_ref, w_ref, o_ref):
        o_ref[...] = w_ref[...].astype(BF16)

    if kind == "col":
        full, out_map = (r, N_CHIPS * c), (lambda i, idx_ref: (i, idx_ref[1]))
    else:
        full, out_map = (N_CHIPS * r, c), (lambda i, idx_ref: (idx_ref[1] * nrb + i, 0))
    return pl.pallas_call(
        body, name=name, out_shape=jax.ShapeDtypeStruct(full, BF16),
        grid_spec=pltpu.PrefetchScalarGridSpec(
            num_scalar_prefetch=1, grid=(nrb,), in_specs=[pl.BlockSpec((tr, c), lambda i, idx_ref: (i, 0))],
            out_specs=pl.BlockSpec((tr, c), out_map)),
        compiler_params=_params(),
    )(idx, w)


def _matmul_gathering(a, placed, order, name):
    t, k = a.shape
    n = placed.shape[1]
    w = n // N_CHIPS
    tm, tn = _pick(t, _M_TILES), _pick(w, _N_TILES)
    ni, nj = t // tm, w // tn
    per_shard, total = ni * nj, N_CHIPS * ni * nj
    gather = _AllGather([placed], ["col"])

    def body(ord_ref, a_ref, w_own_ref, o_ref, w_ref, wbuf, fetch_sem, send, recv):
        s, i, j = pl.program_id(0), pl.program_id(1), pl.program_id(2)
        step = (s * ni + i) * nj + j
        e = _Env((w_own_ref,), (w_ref,), send, recv)

        def fetch(src, st):
            col = pl.multiple_of((ord_ref[st // per_shard] * nj + st % nj) * tn, LANE)
            return pltpu.make_async_copy(src.at[:, pl.ds(col, tn)], wbuf.at[st % 2], fetch_sem.at[st % 2])

        @pl.when(step == 0)
        def _():
            gather.start(e)
            fetch(w_own_ref, step).start()

        nxt = step + 1
        for kk, (cx, cy) in enumerate(e.chips):
            @pl.when(nxt == (kk + 1) * per_shard)
            def _():
                gather._ici(e, 0, kk, 2 * cx + cy).wait_recv()
                gather._d2d(e, 0, kk, e.c).start()
                gather._d2d(e, 0, kk, 1 - e.c).wait_recv()

        @pl.when(nxt < per_shard)
        def _():
            fetch(w_own_ref, nxt).start()

        @pl.when((nxt >= per_shard) & (nxt < total))
        def _():
            fetch(w_ref, nxt).start()

        fetch(w_ref, step).wait()
        o_ref[...] = jnp.dot(a_ref[...], wbuf[step % 2], preferred_element_type=F32)

        @pl.when(step == total - 1)
        def _():
            for kk in range(3):
                gather._ici(e, 0, kk, e.me).wait_send()
                gather._d2d(e, 0, kk, e.c).wait_send()

    z, full = pl.pallas_call(
        body, name=name, out_shape=(jax.ShapeDtypeStruct((t, n), F32), jax.ShapeDtypeStruct(placed.shape, placed.dtype)),
        grid_spec=pltpu.PrefetchScalarGridSpec(
            num_scalar_prefetch=1, grid=(N_CHIPS, ni, nj),
            in_specs=[pl.BlockSpec((tm, k), lambda s, i, j, ord_ref: (i, 0)), ANY],
            out_specs=(pl.BlockSpec((tm, tn), lambda s, i, j, ord_ref: (i, ord_ref[s] * nj + j)), ANY),
            scratch_shapes=[pltpu.VMEM((2, k, tn), placed.dtype), pltpu.SemaphoreType.DMA((2,)),
                            pltpu.SemaphoreType.DMA((gather.n_sems,)), pltpu.SemaphoreType.DMA((gather.n_sems,))]),
        input_output_aliases={2: 1}, compiler_params=_params(),
    )(order, a, placed)
    return z, full


def _add_own_half(g4, recv, idx, out_dtype, name):
    p, _, rh, n = g4.shape
    tr, tc = _tile2d(rh, n, 1024 * 1024)

    def body(idx_ref, g_ref, r_ref, o_ref):
        o_ref[...] = (g_ref[...] + r_ref[...]).astype(out_dtype)

    return pl.pallas_call(
        body, name=name, out_shape=jax.ShapeDtypeStruct((p, rh, n), out_dtype),
        grid_spec=pltpu.PrefetchScalarGridSpec(
            num_scalar_prefetch=1, grid=(p, rh // tr, n // tc),
            in_specs=[pl.BlockSpec((None, None, tr, tc), lambda q, i, j, idx_ref: (q, idx_ref[0], i, j)),
                      pl.BlockSpec((None, tr, tc), lambda q, i, j, idx_ref: (q, i, j))],
            out_specs=pl.BlockSpec((None, tr, tc), lambda q, i, j, idx_ref: (q, i, j))),
        compiler_params=_params(),
    )(idx, g4, recv)


def _sum_chips(own, kind, parts, idx, slots, to_all, name):
    _, rh, w = parts.shape
    tr, tc = _tile2d(rh, w, 512 * 1024)
    nrb, ncb = rh // tr, w // tc

    def body(idx_ref, own_ref, p0, p1, p2, o_ref):
        o_ref[...] = ((own_ref[...].astype(F32) + p0[...].astype(F32)) + p1[...].astype(F32)) + p2[...].astype(F32)

    if kind == "col":
        own_spec = pl.BlockSpec((None, tr, tc), lambda i, j, idx_ref: (0, i, idx_ref[1] * ncb + j))
    else:
        own_spec = pl.BlockSpec((None, tr, tc), lambda i, j, idx_ref: (idx_ref[1], i, j))
    if to_all:
        out_map = lambda i, j, idx_ref: ((2 * idx_ref[1] + idx_ref[0]) * nrb + i, j)
    else:
        out_map = lambda i, j, idx_ref: (idx_ref[0] * nrb + i, j)

    def part(k):
        return pl.BlockSpec((None, tr, tc), lambda i, j, idx_ref: (k, i, j))

    return pl.pallas_call(
        body, name=name, out_shape=jax.ShapeDtypeStruct((slots * rh, w), F32),
        grid_spec=pltpu.PrefetchScalarGridSpec(
            num_scalar_prefetch=1, grid=(nrb, ncb), in_specs=[own_spec, part(0), part(1), part(2)],
            out_specs=pl.BlockSpec((tr, tc), out_map)),
        compiler_params=_params(),
    )(idx, own, parts, parts, parts)


class _Reduce:
    def __init__(self, name, g, kind, idx, wire, to_all):
        r, c = g.shape
        self.name, self.kind, self.idx, self.wire, self.to_all = name, kind, idx, wire, to_all
        self.view = g.reshape(1, 2, r // 2, c) if kind == "col" else g.reshape(N_CHIPS, 2, r // (2 * N_CHIPS), c)

    def sibling(self):
        return _SiblingExchange([self.view])

    def got_sibling(self, outs):
        self.sum = _add_own_half(self.view, outs[0], self.idx, self.wire, "grad_chip_sum_" + self.name)

    def chips(self):
        return _ChipExchange([self.sum], [self.kind])

    def got_chips(self, outs):
        self.total = _sum_chips(self.sum, self.kind, outs[0], self.idx, 2 * N_CHIPS if self.to_all else 2,
                                self.to_all, "grad_total_" + self.name)


def _pack(arrays, rows):
    flat = jnp.concatenate([a.reshape(-1) for a in arrays])
    return jnp.pad(flat, (0, rows * SMALL_PACK_COLS - flat.shape[0])).reshape(rows, SMALL_PACK_COLS)


def _unpack(packed, shapes):
    flat = packed.reshape(-1)
    out, o = [], 0
    for shp in shapes:
        size = math.prod(shp)
        out.append(flat[o:o + size].reshape(shp))
        o += size
    return out


def _pack_rows(shapes):
    total = sum(math.prod(s) for s in shapes)
    unit = SMALL_PACK_COLS * N_CHIPS * 2 * SUBLANE
    return -(-total // unit) * (N_CHIPS * 2 * SUBLANE)


BIG = ("w_in", "w_attn_proj", "w_lru_proj", "w_out", "w_ffn_gate", "w_ffn_up", "w_ffn_down")
BIG_KIND = {"w_in": "col", "w_attn_proj": "row", "w_lru_proj": "row", "w_out": "row", "w_ffn_gate": "col",
            "w_ffn_up": "col", "w_ffn_down": "row"}
SMALL = ("norm1_g", "b_gates", "q_norm_g", "k_norm_g", "sinks", "conv_w", "conv_b", "w_rgate", "b_rgate",
         "w_igate", "b_igate", "lru_lambda", "norm2_g")
PACKED = tuple(n for n in SMALL if n not in ("w_rgate", "w_igate"))
WEIGHTS = ("norm1_g", "w_in", "b_gates", "q_norm_g", "k_norm_g", "sinks", "conv_w", "conv_b", "w_rgate", "b_rgate",
           "w_igate", "b_igate", "lru_lambda", "w_attn_proj", "w_lru_proj", "w_out", "norm2_g", "w_ffn_gate",
           "w_ffn_up", "w_ffn_down")


def kernel(x, positions, norm1_g, w_in, b_gates, q_norm_g, k_norm_g, sinks, conv_w, conv_b, w_rgate, b_rgate, w_igate, b_igate, lru_lambda, w_attn_proj, w_lru_proj, w_out, norm2_g, w_ffn_gate, w_ffn_up, w_ffn_down, loss_target, m_norm1_g, m_w_in, m_b_gates, m_q_norm_g, m_k_norm_g, m_sinks, m_conv_w, m_conv_b, m_w_rgate, m_b_rgate, m_w_igate, m_b_igate, m_lru_lambda, m_w_attn_proj, m_w_lru_proj, m_w_out, m_norm2_g, m_w_ffn_gate, m_w_ffn_up, m_w_ffn_down, v_norm1_g, v_w_in, v_b_gates, v_q_norm_g, v_k_norm_g, v_sinks, v_conv_w, v_conv_b, v_w_rgate, v_b_rgate, v_w_igate, v_b_igate, v_lru_lambda, v_w_attn_proj, v_w_lru_proj, v_w_out, v_norm2_g, v_w_ffn_gate, v_w_ffn_up, v_w_ffn_down):
    args = dict(locals())
    w = {n: args[n] for n in WEIGHTS}
    mom = {n: args["m_" + n] for n in WEIGHTS}
    var = {n: args["v_" + n] for n in WEIGHTS}

    t, d = x.shape[1], x.shape[2]
    hd = q_norm_g.shape[-1]
    nq = sinks.shape[-1]
    q_w = nq * hd
    d_rnn = conv_b.shape[-1]
    taps = conv_w.shape[1]
    n_blocks, bw = w_rgate.shape[1], w_rgate.shape[2]
    in_w = w_in.shape[-1] * N_CHIPS
    kv_w = (in_w - q_w - 2 * d_rnn - 2 * d) // 2
    kv = kv_w // hd
    grp = nq // kv
    u_off = q_w + 2 * kv_w
    gr_off = u_off + d_rnn
    ga_off = gr_off + d_rnn
    gw = bw * LANE // math.gcd(bw, LANE)
    chip = 2 * lax.axis_index("x") + lax.axis_index("y")
    idx = jnp.stack([lax.axis_index("c"), chip]).astype(jnp.int32)

    x2, tgt = x[0], loss_target[0]

    placed = {n: _cast_into_full(w[n][0], BIG_KIND[n], idx, "cast_" + n) for n in BIG}

    def gather(*names):
        return _AllGather([placed[n] for n in names], [BIG_KIND[n] for n in names])

    mx, my = lax.axis_index("x"), lax.axis_index("y")
    order = jnp.stack([chip, 2 * (1 - mx) + my, 2 * mx + (1 - my), 2 * (1 - mx) + (1 - my)]).astype(jnp.int32)
    conv_w_full = _gather_small(conv_w[0], "allgather_conv_w")
    conv_w_full = jnp.transpose(conv_w_full, (1, 0, 2)).reshape(taps, d_rnn)

    inv_freq = ROPE_THETA ** (-jnp.arange(0, hd // 4, 2, dtype=F32) / (hd // 4))
    ang = positions[0].astype(F32)[:, None] * inv_freq
    cos, sin = jnp.cos(ang), jnp.sin(ang)
    rest = hd - 2 * cos.shape[1]
    cos_t = jnp.concatenate([cos, cos, jnp.ones((t, rest), F32)], axis=1)
    sin_t = jnp.concatenate([-sin, sin, jnp.zeros((t, rest), F32)], axis=1)
    sinks1 = sinks[0]

    xn = _rms_fwd(x2, norm1_g, "rms1_fwd")
    z, win_f = _matmul_gathering(xn, placed["w_in"], order, "in_proj")
    attn, (wap_f, wlp_f, wout_f) = _attn_fwd(z, cos_t, sin_t, q_norm_g, k_norm_g, sinks1, kv, grp, hd, "attn_fwd",
                                             job=gather("w_attn_proj", "w_lru_proj", "w_out"))
    uc = _conv_fwd(z, u_off, d_rnn, conv_w_full, conv_b, "conv_fwd")
    rpre, ipre = _gates_fwd(uc, w_rgate[0], w_igate[0], gw, "gates_fwd")
    (hst, rec), (wg_f,) = _lru_fwd(uc, rpre, ipre, z, gr_off, b_rgate, b_igate, lru_lambda, "lru_fwd",
                                   job=gather("w_ffn_gate"))
    pa = _matmul(attn, wap_f, "nn", "attn_proj")
    plru, merged = _matmul(rec, wlp_f, "nn", "lru_proj", fused=(
        [pa, (z, ga_off), (z, ga_off + d), (b_gates, 0), (b_gates, d)], _merge_after_lru_proj, (F32, BF16)))
    h1 = _matmul(merged, wout_f, "nn", "out_proj", add=x2)
    hn = _rms_fwd(h1, norm2_g, "rms2_fwd")
    gate, (wu_f,) = _matmul(hn, wg_f, "nn", "ffn_gate", job=gather("w_ffn_up"))
    (up, act), (wd_f,) = _matmul(hn, wu_f, "nn", "ffn_up", job=gather("w_ffn_down"),
                                 fused=([gate], _swiglu_after_up, (F32, BF16)))
    yout = _matmul(act, wd_f, "nn", "ffn_down", add=h1)
    dy, dy16, loss_part = _loss_head(yout, tgt, "loss_head")
    loss = lax.psum(loss_part[0, 0], ("x", "y", "c"))

    def reduction(n, g):
        return _Reduce(n, g, BIG_KIND[n], idx, BF16, False)

    r_wd = reduction("w_ffn_down", _matmul(act, dy16, "tn", "d_w_ffn_down"))
    (dgate, dup), got = _matmul(dy16, wd_f, "nt", "d_act", job=r_wd.sibling(),
                                fused=([gate, up], _swiglu_bwd_after_dact, (BF16, BF16)))
    r_wd.got_sibling(got)
    r_wg = reduction("w_ffn_gate", _matmul(hn, dgate, "tn", "d_w_ffn_gate"))
    g_wu, got = _matmul(hn, dup, "tn", "d_w_ffn_up", job=r_wg.sibling())
    r_wg.got_sibling(got)
    r_wu = reduction("w_ffn_up", g_wu)
    dhn, got = _matmul(dgate, wg_f, "nt", "d_hn_gate", job=r_wu.sibling())
    r_wu.got_sibling(got)
    dhn = _matmul(dup, wu_f, "nt", "d_hn_up", add=dhn)
    dh1, g_norm2, dh1_16 = _rms_bwd(dhn, h1, norm2_g, dy, "rms2_bwd", mxu_copy=True)
    r_wout = reduction("w_out", _matmul(merged, dh1_16, "tn", "d_w_out"))
    dmerged, got = _matmul(dh1_16, wout_f, "nt", "d_merged", job=r_wout.sibling())
    r_wout.got_sibling(got)
    (dpa, dz, g_ba), got = _merge_bwd(dmerged, z, b_gates, pa, ga_off, 0, None, "merge_bwd_attn", job=r_wout.chips())
    r_wout.got_chips(got)
    dpl, dz, g_bl = _merge_bwd(dmerged, z, b_gates, plru, ga_off + d, d, dz, "merge_bwd_lru")
    r_wap = reduction("w_attn_proj", _matmul(attn, dpa, "tn", "d_w_attn_proj"))
    dattn, got = _matmul(dpa, wap_f, "nt", "d_attn", job=r_wap.sibling())
    r_wap.got_sibling(got)
    g_wlp, got = _matmul(rec, dpl, "tn", "d_w_lru_proj", job=r_wap.chips())
    r_wap.got_chips(got)
    r_wlp = reduction("w_lru_proj", g_wlp)
    drec, got = _matmul(dpl, wlp_f, "nt", "d_rec", job=r_wlp.sibling())
    r_wlp.got_sibling(got)
    both = _Jobs(r_wlp.chips(), r_wg.chips())
    (dz, drp, dip, duc_direct, g_lam, g_br, g_bi), got = _lru_bwd(
        drec, hst, uc, rpre, ipre, z, gr_off, b_rgate, b_igate, lru_lambda, dz, "lru_bwd", job=both)
    got_wlp, got_wg = both.split(got)
    r_wlp.got_chips(got_wlp)
    r_wg.got_chips(got_wg)
    duc = _gates_bwd_x(duc_direct, drp, dip, w_rgate[0], w_igate[0], gw, "gates_bwd_x")
    g_wr = _gates_bwd_w(uc, drp, n_blocks, bw, gw, "gates_bwd_wr")
    g_wi = _gates_bwd_w(uc, dip, n_blocks, bw, gw, "gates_bwd_wi")
    dz, g_convw, g_convb = _conv_bwd(duc, z, u_off, conv_w_full, dz, "conv_bwd")
    (dq, dk, dv, g_qg, g_kg, g_sinks), got = _attn_bwd(dattn, z, cos_t, sin_t, q_norm_g, k_norm_g, sinks1, kv, grp, hd,
                                                        "attn_bwd", job=_Jobs(r_wu.chips(), r_wd.chips()))
    r_wu.got_chips(got[:1])
    r_wd.got_chips(got[1:])
    for part, col in ((dq, 0), (dk, q_w), (dv, q_w + kv_w)):
        dz = lax.dynamic_update_slice(dz, part, (0, col))
    r_wr = _Reduce("w_rgate", g_wr.reshape(n_blocks * bw, bw), "row", idx, F32, True)
    r_wi = _Reduce("w_igate", g_wi.reshape(n_blocks * bw, bw), "row", idx, F32, True)
    early = [r_wap, r_wlp, r_wout, r_wg, r_wu, r_wd]
    three = _Jobs(r_wr.sibling(), r_wi.sibling(), _FinishExchange([r.total for r in early], [False] * len(early)))
    g_top, got = _matmul(xn, dz, "tn", "d_w_in_top", m_window=(0, d // 2), job=three)
    got_wr, got_wi, finished = three.split(got)
    r_wr.got_sibling(got_wr)
    r_wi.got_sibling(got_wi)
    r_top = _Reduce("w_in_top", g_top, "col", idx, BF16, False)
    three = _Jobs(r_top.sibling(), r_wr.chips(), r_wi.chips())
    g_bot, got = _matmul(xn, dz, "tn", "d_w_in_bot", m_window=(d // 2, d // 2), job=three)
    got_top, got_wr, got_wi = three.split(got)
    r_top.got_sibling(got_top)
    r_wr.got_chips(got_wr)
    r_wi.got_chips(got_wi)
    r_bot = _Reduce("w_in_bot", g_bot, "col", idx, BF16, False)
    both = _Jobs(r_top.chips(), r_bot.sibling())
    dxn, got = _matmul(dz, win_f, "nt", "d_xn_a", m_window=(0, t // 2), into=(None, t), job=both)
    got_top, got_bot = both.split(got)
    r_top.got_chips(got_top)
    r_bot.got_sibling(got_bot)
    dxn, got = _matmul(dz, win_f, "nt", "d_xn_b", m_window=(t // 2, t // 2), into=(dxn, t), job=r_bot.chips())
    r_bot.got_chips(got)
    dx, g_norm1 = _rms_bwd(dxn, x2, norm1_g, dh1, "rms1_bwd")

    small_grads = {"norm1_g": g_norm1, "b_gates": jnp.concatenate([g_ba, g_bl], axis=1), "q_norm_g": g_qg,
                   "k_norm_g": g_kg, "sinks": g_sinks[:, :nq], "conv_w": g_convw, "conv_b": g_convb,
                   "b_rgate": g_br, "b_igate": g_bi, "lru_lambda": g_lam, "norm2_g": g_norm2}
    gshapes = [small_grads[n].shape for n in PACKED]
    small_sum, (top, bot, grads_wr, grads_wi) = _allreduce_small(
        _pack([small_grads[n] for n in PACKED], _pack_rows(gshapes)),
        _FinishExchange([r.total for r in (r_top, r_bot, r_wr, r_wi)], [False, False, True, True]), "grad_last_exchange")
    grads = dict(zip(BIG[1:], finished))
    grads["w_in"] = jnp.concatenate([top, bot], axis=0)
    grads["w_rgate"], grads["w_igate"] = grads_wr, grads_wi
    small_full = dict(zip(PACKED, _unpack(small_sum, gshapes)))
    per = d_rnn // N_CHIPS
    small_full["conv_w"] = lax.dynamic_slice(small_full["conv_w"], (0, chip * per), (taps, per))
    grads.update(small_full)

    delta, new_m, new_v = {}, {}, {}
    for n in BIG + ("w_rgate", "w_igate"):
        as2d = (lambda a: a[0]) if n in BIG else (lambda a: a.reshape(n_blocks * bw, bw))
        if n == "w_in":
            delta[n], new_m[n], new_v[n] = _adamw(as2d(w[n]), grads[n], as2d(mom[n]), as2d(var[n]), "adamw_" + n)
        else:
            delta[n], new_m[n], new_v[n], grads[n] = _adamw(as2d(w[n]), grads[n], as2d(mom[n]), as2d(var[n]),
                                                            "adamw_" + n, pass_grad=True)
    pshapes = [w[n].shape for n in PACKED]
    prows = _pack_rows(pshapes)
    pk = [_pack([src[n] for n in PACKED], prows) for src in (w, grads, mom, var)]
    for res, packed in zip((delta, new_m, new_v), _adamw(pk[0], pk[1], pk[2], pk[3], "adamw_small")):
        res.update(dict(zip(PACKED, _unpack(packed, pshapes))))

    outs = [loss, dx.reshape(x.shape)]
    for res in (grads, delta, new_m, new_v):
        outs += [res[n].reshape(w[n].shape) for n in WEIGHTS]
    return tuple(outs)


def _allreduce_small(x, job, name):
    n_dev = 2 * N_CHIPS
    rel = [(fx, fy, fc) for fx in (0, 1) for fy in (0, 1) for fc in (0, 1)][1:]
    j_in, j_out = len(job.inputs), len(job.out_shapes)

    def body(x_ref, *refs):
        jins, all_ref, o_ref, jouts = refs[:j_in], refs[j_in], refs[j_in + 1], refs[j_in + 2:j_in + 2 + j_out]
        send_sems, recv_sems, jsend, jrecv = refs[j_in + 2 + j_out:]
        e = _Env((x_ref,), (all_ref,), send_sems, recv_sems)
        carried = _Env(jins, jouts, jsend, jrecv, 0, (e.x, e.y, e.c))
        job.start(carried)
        mine = 2 * e.me + e.c

        def peer(r):
            fx, fy, fc = rel[r]
            return (1 - e.x if fx else e.x), (1 - e.y if fy else e.y), (1 - e.c if fc else e.c)

        all_ref[mine] = x_ref[...]
        for r in range(len(rel)):
            e.copy(x_ref, all_ref.at[mine], r, peer(r)).start()
        for r in range(len(rel)):
            px, py, pc = peer(r)
            e.copy(x_ref, all_ref.at[2 * (2 * px + py) + pc], r, peer(r)).wait_recv()
        total = all_ref[0]
        for dev in range(1, n_dev):
            total = total + all_ref[dev]
        o_ref[...] = total
        for r in range(len(rel)):
            e.copy(x_ref, all_ref.at[mine], r, peer(r)).wait_send()
        job.finish(carried)

    vm = pl.BlockSpec(memory_space=pltpu.VMEM)
    res = pl.pallas_call(
        body, name=name,
        out_shape=(jax.ShapeDtypeStruct((n_dev,) + x.shape, x.dtype), jax.ShapeDtypeStruct(x.shape, x.dtype))
        + tuple(job.out_shapes),
        in_specs=[vm] + [ANY] * j_in, out_specs=(vm, vm) + (ANY,) * j_out,
        input_output_aliases={1 + i: 2 + o for i, o in job.aliases.items()},
        scratch_shapes=[pltpu.SemaphoreType.DMA((len(rel),)), pltpu.SemaphoreType.DMA((len(rel),)),
                        pltpu.SemaphoreType.DMA((job.n_sems,)), pltpu.SemaphoreType.DMA((job.n_sems,))])(x, *job.inputs)
    return res[1], tuple(res[2:])


def _gather_small(shard, name):
    def body(s_ref, o_ref, send_sems, recv_sems):
        e = _Env((s_ref,), (o_ref,), send_sems, recv_sems)
        o_ref[e.me] = s_ref[...]
        for k, (cx, cy) in enumerate(e.chips):
            e.copy(s_ref, o_ref.at[e.me], k, (cx, cy, e.c)).start()
        for k, (cx, cy) in enumerate(e.chips):
            e.copy(s_ref, o_ref.at[2 * cx + cy], k, (cx, cy, e.c)).wait_recv()
        for k, (cx, cy) in enumerate(e.chips):
            e.copy(s_ref, o_ref.at[e.me], k, (cx, cy, e.c)).wait_send()

    vm = pl.BlockSpec(memory_space=pltpu.VMEM)
    return pl.pallas_call(body, name=name, out_shape=jax.ShapeDtypeStruct((N_CHIPS,) + shard.shape, shard.dtype),
                          in_specs=[vm], out_specs=vm,
                          scratch_shapes=[pltpu.SemaphoreType.DMA((3,)), pltpu.SemaphoreType.DMA((3,))])(shard)
```

```python
import functools
import math

import jax
import jax.numpy as jnp
from jax import lax
from jax.experimental import pallas as pl
from jax.experimental.pallas import tpu as pltpu

F32 = jnp.float32
BF16 = jnp.bfloat16
MESH = pl.DeviceIdType.MESH

WINDOW = 128
BLK = 128
ROPE_THETA = 500000.0
LRU_C = 8.0
EPS = 1e-6
NEG = -1e30
ADAM_LR = 0.001
ADAM_B1 = 0.9
ADAM_B2 = 0.999
ADAM_EPS = 1e-08
ADAM_WD = 0.01
ADAM_STEP = 10

VMEM_LIMIT_BYTES = 52 * 1024 * 1024
LANE = 128
SUBLANE = 8
N_CHIPS = 4
SMALL_PACK_COLS = 512


def _params(**kw):
    return pltpu.CompilerParams(vmem_limit_bytes=VMEM_LIMIT_BYTES, **kw)


def _pick(dim, cands):
    for c in cands:
        if dim % c == 0:
            return c
    return dim


def _sigmoid(x):
    return 0.5 * jnp.tanh(0.5 * x) + 0.5


ANY = pl.BlockSpec(memory_space=pl.ANY)


class _Env:
    def __init__(self, ins, outs, send, recv, sem0=0, place=None):
        self.ins, self.outs, self.send, self.recv, self.sem0 = ins, outs, send, recv, sem0
        self.x, self.y, self.c = place or (lax.axis_index("x"), lax.axis_index("y"), lax.axis_index("c"))
        self.me = 2 * self.x + self.y
        self.chips = [(1 - self.x, self.y), (self.x, 1 - self.y), (1 - self.x, 1 - self.y)]
        self.sibling = (self.x, self.y, 1 - self.c)

    def sub(self, i0, n_in, o0, n_out, sem0):
        return _Env(self.ins[i0:i0 + n_in], self.outs[o0:o0 + n_out], self.send, self.recv, self.sem0 + sem0,
                    (self.x, self.y, self.c))

    def copy(self, src, dst, sem, to):
        return pltpu.make_async_remote_copy(src_ref=src, dst_ref=dst, send_sem=self.send.at[self.sem0 + sem],
                                            recv_sem=self.recv.at[self.sem0 + sem], device_id=to, device_id_type=MESH)


class _Exchange:
    inputs, out_shapes, aliases, n_sems = (), (), {}, 0

    def start(self, e):
        raise NotImplementedError

    def finish(self, e):
        raise NotImplementedError


class _Jobs(_Exchange):
    def __init__(self, *jobs):
        self.jobs, self.inputs, self.out_shapes, self.aliases, self.n_sems, self.at = jobs, [], [], {}, 0, []
        for job in jobs:
            self.at.append((len(self.inputs), len(self.out_shapes), self.n_sems))
            self.aliases.update({len(self.inputs) + i: len(self.out_shapes) + o for i, o in job.aliases.items()})
            self.inputs += list(job.inputs)
            self.out_shapes += list(job.out_shapes)
            self.n_sems += job.n_sems

    def _each(self, e):
        for job, (i0, o0, s0) in zip(self.jobs, self.at):
            yield job, e.sub(i0, len(job.inputs), o0, len(job.out_shapes), s0)

    def split(self, outs):
        return [tuple(outs[o0:o0 + len(job.out_shapes)]) for job, (_, o0, _) in zip(self.jobs, self.at)]

    def start(self, e):
        for job, se in self._each(e):
            job.start(se)

    def finish(self, e):
        for job, se in self._each(e):
            job.finish(se)


def _call(body, name, out_shape, grid, in_specs, out_specs, args, scratch_shapes=(), job=None, aliases=None):
    aliases = dict(aliases or {})
    if job is None:
        return pl.pallas_call(body, name=name, out_shape=out_shape, grid=grid, in_specs=list(in_specs),
                              out_specs=out_specs, scratch_shapes=list(scratch_shapes), input_output_aliases=aliases,
                              compiler_params=_params())(*args), ()
    single = not isinstance(out_shape, (tuple, list))
    shapes = [out_shape] if single else list(out_shape)
    ospecs = [out_specs] if single else list(out_specs)
    n_in, n_out, n_scr = len(args), len(shapes), len(scratch_shapes)
    j_in, j_out = len(job.inputs), len(job.out_shapes)

    def hosted(*refs):
        ins, jins = refs[:n_in], refs[n_in:n_in + j_in]
        outs = refs[n_in + j_in:n_in + j_in + n_out]
        jouts = refs[n_in + j_in + n_out:n_in + j_in + n_out + j_out]
        rest = refs[n_in + j_in + n_out + j_out:]
        e = _Env(jins, jouts, rest[n_scr], rest[n_scr + 1])
        first = functools.reduce(jnp.logical_and, [pl.program_id(d) == 0 for d in range(len(grid))])
        last = functools.reduce(jnp.logical_and, [pl.program_id(d) == g - 1 for d, g in enumerate(grid)])

        @pl.when(first)
        def _():
            job.start(e)

        body(*ins, *outs, *rest[:n_scr])

        @pl.when(last)
        def _():
            job.finish(e)

    res = pl.pallas_call(
        hosted, name=name, out_shape=tuple(shapes + list(job.out_shapes)), grid=grid,
        in_specs=list(in_specs) + [ANY] * j_in, out_specs=tuple(ospecs + [ANY] * j_out),
        scratch_shapes=list(scratch_shapes) + [pltpu.SemaphoreType.DMA((job.n_sems,)),
                                               pltpu.SemaphoreType.DMA((job.n_sems,))],
        input_output_aliases={**aliases, **{n_in + i: n_out + o for i, o in job.aliases.items()}},
        compiler_params=_params())(*args, *job.inputs)
    return (res[0] if single else tuple(res[:n_out])), tuple(res[n_out:])


_M_TILES = (1024, 1408, 1280, 512, 256, 128)
_N_TILES = (1408, 1280, 1024, 640, 512, 256, 128)
MXU_FULL_ROWS = 1024
MATMUL_VMEM_BUDGET = 42 * 1024 * 1024
MXU_FLOPS_PER_HBM_BYTE = 500


def _matmul_tiles(m, n, k, sa, sb, so, has_add, tn_divides=0):
    best = None
    for tm in [c for c in _M_TILES if m % c == 0] or [m]:
        for tn in [c for c in _N_TILES if n % c == 0 and tn_divides % c == 0] or [n]:
            for nk in range(1, 17):
                tk = k // nk
                if k % nk or tk % LANE:
                    continue
                need = 2 * (tm * tk * sa + tk * tn * sb) + 2 * tm * tn * (so + (4 if has_add else 0))
                need += tm * tn * 4 if nk > 1 else 0
                fetched = tk * tn * sb + tm * tk * sa // (1 if nk > 1 else n // tn)
                if need > MATMUL_VMEM_BUDGET:
                    continue
                mxu_bound = fetched * MXU_FLOPS_PER_HBM_BYTE <= 2 * tm * tn * tk
                key = (mxu_bound, min(tm, MXU_FULL_ROWS), -nk, tn, tm)
                if best is None or key > best[0]:
                    best = (key, (tm, tn, tk))
    assert best is not None, (m, n, k)
    return best[1]


def _matmul(a, b, mode, name, add=None, out_dtype=F32, job=None, m_window=None, into=None, fused=None):
    if mode == "nn":
        (m, k), (k2, n) = a.shape, b.shape
    elif mode == "nt":
        (m, k), (n, k2) = a.shape, b.shape
    else:
        (k, m), (k2, n) = a.shape, b.shape
    assert k == k2, (a.shape, b.shape, mode)
    m0, m = m_window or (0, m)
    tiles, fuse_fn, out_dtypes = fused or ((), None, (out_dtype,))
    tiles = [x if isinstance(x, tuple) else (x, 0) for x in tiles]
    tm, tn, tk = _matmul_tiles(math.gcd(m, m0) if m0 else m, n, k, a.dtype.itemsize, b.dtype.itemsize,
                               sum(jnp.dtype(dt).itemsize for dt in out_dtypes)
                               + sum(x.dtype.itemsize for x, _ in tiles if x.shape[0] > 1),
                               add is not None, math.gcd(*[c0 for _, c0 in tiles], 0))
    nk, mb0 = k // tk, m0 // tm
    if mode == "nn":
        a_spec = pl.BlockSpec((tm, tk), lambda i, j, kk: (mb0 + i, kk))
        b_spec = pl.BlockSpec((tk, tn), lambda i, j, kk: (kk, j))
        dims = (((1,), (0,)), ((), ()))
    elif mode == "nt":
        a_spec = pl.BlockSpec((tm, tk), lambda i, j, kk: (mb0 + i, kk))
        b_spec = pl.BlockSpec((tn, tk), lambda i, j, kk: (j, kk))
        dims = (((1,), (1,)), ((), ()))
    else:
        a_spec = pl.BlockSpec((tk, tm), lambda i, j, kk: (kk, mb0 + i))
        b_spec = pl.BlockSpec((tk, tn), lambda i, j, kk: (kk, j))
        dims = (((0,), (0,)), ((), ()))
    out_rows, ob0 = (into[1], mb0) if into is not None else (m, 0)
    o_spec = pl.BlockSpec((tm, tn), lambda i, j, kk: (ob0 + i, j))
    has_add = add is not None
    begun = into is not None and into[0] is not None

    n_side, n_out = has_add + len(tiles), len(out_dtypes)

    def body(*refs):
        a_ref, b_ref = refs[:2]
        side = refs[2:2 + n_side]
        o_refs = refs[len(refs) - n_out - (nk > 1):len(refs) - (nk > 1)]
        part = lax.dot_general(a_ref[...].astype(BF16), b_ref[...].astype(BF16), dims, preferred_element_type=F32)

        def finish(r):
            if has_add:
                r = r + side[0][...]
            vals = fuse_fn(r, *[x[...] for x in side[has_add:]]) if fuse_fn else (r,)
            for o_ref, val, dt in zip(o_refs, vals, out_dtypes):
                o_ref[...] = val.astype(dt)

        if nk == 1:
            finish(part)
            return
        acc = refs[-1]
        kk = pl.program_id(2)

        @pl.when(kk == 0)
        def _():
            acc[...] = part

        @pl.when(kk > 0)
        def _():
            acc[...] += part

        @pl.when(kk == nk - 1)
        def _():
            finish(acc[...])

    def side_spec(x, c0):
        if x.shape[0] == 1:
            return pl.BlockSpec((1, tn), lambda i, j, kk: (0, c0 // tn + j))
        return pl.BlockSpec((tm, tn), lambda i, j, kk: (mb0 + i, c0 // tn + j))

    in_specs = [a_spec, b_spec] + ([side_spec(add, 0)] if has_add else []) + [side_spec(x, c0) for x, c0 in tiles]
    args = (a, b) + ((add,) if has_add else ()) + tuple(x for x, _ in tiles)
    aliases = None
    if begun:
        aliases = {len(args): 0}
        in_specs, args = in_specs + [ANY], args + (into[0],)
    shapes = tuple(jax.ShapeDtypeStruct((out_rows, n), dt) for dt in out_dtypes)
    res, extra = _call(body, name, shapes if fused else shapes[0], (m // tm, n // tn, nk), in_specs,
                       (o_spec,) * n_out if fused else o_spec, args, [pltpu.VMEM((tm, tn), F32)] if nk > 1 else [],
                       job, aliases)
    return res if job is None else (res, extra)


def _row_tile(rows, cols, budget_elems=512 * 1024):
    cands = [c for c in (1024, 704, 512, 352, 256, 128, 64, 32, 16) if c * cols <= budget_elems]
    return _pick(rows, cands or (16,))


_EW_COLS = (1280, 1408, 1024, 640, 512, 256, 128)


def _tile2d(rows, cols, max_elems):
    tc = _pick(cols, _EW_COLS)
    return _row_tile(rows, tc, max_elems), tc


def _rms_fwd(x, g, name):
    t, d = x.shape
    tr = _row_tile(t, d)

    def body(x_ref, g_ref, o_ref):
        xv = x_ref[...]
        rstd = lax.rsqrt(jnp.mean(xv * xv, axis=-1, keepdims=True) + EPS)
        o_ref[...] = (xv * rstd * g_ref[...]).astype(BF16)

    spec = pl.BlockSpec((tr, d), lambda i: (i, 0))
    return pl.pallas_call(body, name=name, out_shape=jax.ShapeDtypeStruct((t, d), BF16), grid=(t // tr,),
                          in_specs=[spec, pl.BlockSpec((1, d), lambda i: (0, 0))], out_specs=spec,
                          compiler_params=_params())(x, g)


def _rms_bwd(dxn, x, g, resid, name, job=None, mxu_copy=False):
    t, d = x.shape
    tr = _row_tile(t, d, 512 * 1024)

    def body(dxn_ref, x_ref, g_ref, r_ref, dx_ref, dg_ref, *dx16_ref):
        @pl.when(pl.program_id(0) == 0)
        def _():
            dg_ref[...] = jnp.zeros_like(dg_ref)

        xv = x_ref[...]
        rstd = lax.rsqrt(jnp.mean(xv * xv, axis=-1, keepdims=True) + EPS)
        xhat = xv * rstd
        dy = dxn_ref[...]
        dg_ref[...] += jnp.sum(dy * xhat, axis=0, keepdims=True)
        dxhat = dy * g_ref[...]
        dx = r_ref[...] + rstd * (dxhat - xhat * jnp.mean(dxhat * xhat, axis=-1, keepdims=True))
        dx_ref[...] = dx
        if mxu_copy:
            dx16_ref[0][...] = dx.astype(BF16)

    spec = pl.BlockSpec((tr, d), lambda i: (i, 0))
    vec = pl.BlockSpec((1, d), lambda i: (0, 0))
    shapes = (jax.ShapeDtypeStruct((t, d), F32), jax.ShapeDtypeStruct((1, d), F32))
    shapes += (jax.ShapeDtypeStruct((t, d), BF16),) if mxu_copy else ()
    res, extra = _call(body, name, shapes, (t // tr,), [spec, spec, vec, spec],
                       (spec, vec) + ((spec,) if mxu_copy else ()), (dxn, x, g, resid), (), job)
    return res if job is None else (res, extra)


def _swiglu_after_up(up, gate):
    return up, gate * _sigmoid(gate) * up


def _swiglu_bwd_after_dact(dact, gate, up):
    sg = _sigmoid(gate)
    return dact * up * (sg * (1.0 + gate * (1.0 - sg))), dact * (gate * sg)


def _merge_after_lru_proj(plru, pa, ga, gl, ba, bl):
    return plru, _sigmoid(ga + ba) * pa + _sigmoid(gl + bl) * plru


def _merge_bwd(dmerged, z, b_gates, p, z0, b0, dz, name, job=None):
    t, d = p.shape
    cw = _pick(math.gcd(z0, d), (512, 256, 128))
    tr = _row_tile(t, cw, 256 * 1024)
    oz, ob, nd = z0 // cw, b0 // cw, d // cw

    def body(dm_ref, g_ref, b_ref, p_ref, *rest):
        dp_ref, dg_ref, sum_ref = rest[-3:]

        @pl.when(pl.program_id(1) == 0)
        def _():
            sum_ref[...] = jnp.zeros_like(sum_ref)

        dm = dm_ref[...]
        sg = _sigmoid(g_ref[...] + b_ref[...])
        dp_ref[...] = (dm * sg).astype(BF16)
        dg = dm * p_ref[...] * (sg * (1.0 - sg))
        dg_ref[...] = dg.astype(BF16)
        sum_ref[...] += jnp.sum(dg, axis=0, keepdims=True)

    blk = pl.BlockSpec((tr, cw), lambda j, i: (i, j))
    at_z = pl.BlockSpec((tr, cw), lambda j, i: (i, oz + j))
    in_specs = [blk, at_z, pl.BlockSpec((1, cw), lambda j, i: (0, ob + j)), blk]
    args, aliases = (dmerged, z, b_gates, p), None
    if dz is not None:
        in_specs, args, aliases = in_specs + [ANY], args + (dz,), {4: 1}
    res, extra = _call(
        body, name, (jax.ShapeDtypeStruct((t, d), BF16), jax.ShapeDtypeStruct(z.shape, BF16),
                     jax.ShapeDtypeStruct((1, d), F32)), (nd, t // tr), in_specs,
        (blk, at_z, pl.BlockSpec((1, cw), lambda j, i: (0, j))), args, (), job, aliases)
    return res if job is None else (res, extra)


def _loss_head(y, target, name):
    t, d = y.shape
    tr = _row_tile(t, d, 512 * 1024)
    nt = t // tr

    def body(y_ref, t_ref, dy_ref, dy16_ref, loss_ref, acc):
        i = pl.program_id(0)

        @pl.when(i == 0)
        def _():
            acc[...] = jnp.zeros_like(acc)

        e = y_ref[...] - t_ref[...]
        dy = e * (1.0 / d)
        dy_ref[...] = dy
        dy16_ref[...] = dy.astype(BF16)
        acc[...] += jnp.sum(e * e, axis=0, keepdims=True)

        @pl.when(i == nt - 1)
        def _():
            loss_ref[...] = (0.5 / d) * jnp.sum(acc[...], axis=-1, keepdims=True)

    spec = pl.BlockSpec((tr, d), lambda i: (i, 0))
    return pl.pallas_call(
        body, name=name, out_shape=(jax.ShapeDtypeStruct((t, d), F32), jax.ShapeDtypeStruct((t, d), BF16),
                                    jax.ShapeDtypeStruct((1, 1), F32)),
        grid=(nt,), in_specs=[spec, spec], out_specs=(spec, spec, pl.BlockSpec((1, 1), lambda i: (0, 0))),
        scratch_shapes=[pltpu.VMEM((1, d), F32)], compiler_params=_params(),
    )(y, target)


def _adamw(w, g, m, v, name, pass_grad=False):
    r, c = w.shape
    tr, tc = _tile2d(r, c, 512 * 1024)
    c1 = 1.0 - ADAM_B1 ** ADAM_STEP
    c2 = 1.0 - ADAM_B2 ** ADAM_STEP

    def body(w_ref, g_ref, m_ref, v_ref, d_ref, nm_ref, nv_ref, *g_out):
        gv = g_ref[...]
        if pass_grad:
            g_out[0][...] = gv
        mn = ADAM_B1 * m_ref[...] + (1.0 - ADAM_B1) * gv
        vn = ADAM_B2 * v_ref[...] + (1.0 - ADAM_B2) * (gv * gv)
        d_ref[...] = -ADAM_LR * ((mn / c1) / (jnp.sqrt(vn / c2) + ADAM_EPS) + ADAM_WD * w_ref[...])
        nm_ref[...] = mn
        nv_ref[...] = vn

    spec = pl.BlockSpec((tr, tc), lambda i, j: (i, j))
    shp = jax.ShapeDtypeStruct((r, c), F32)
    n_out = 4 if pass_grad else 3
    return pl.pallas_call(body, name=name, out_shape=(shp,) * n_out, grid=(r // tr, c // tc), in_specs=[spec] * 4,
                          out_specs=(spec,) * n_out, compiler_params=_params())(w, g, m, v)


def _swap_halves(v, half):
    n = v.shape[-1]
    lane = lax.broadcasted_iota(jnp.int32, v.shape, 1)
    return jnp.where(lane < half, pltpu.roll(v, n - half, 1),
                     jnp.where(lane < 2 * half, pltpu.roll(v, half, 1), 0.0))


def _rope(y, c, s, half):
    return y * c + _swap_halves(y, half) * s


def _rope_bwd(dout, c, s, half):
    return dout * c + _swap_halves(dout * s, half)


def _stack_heads(ref, grp, hd):
    return jnp.concatenate([ref[:, g * hd:(g + 1) * hd] for g in range(grp)], axis=0)


def _softmax_with_sinks(s, sink_ref, first, grp, i):
    rows = s.shape[0]
    qi = lax.broadcasted_iota(jnp.int32, s.shape, 0) & (BLK - 1)
    kj = lax.broadcasted_iota(jnp.int32, s.shape, 1)
    rel = qi + BLK - kj
    s = jnp.where((rel >= 0) & (rel < WINDOW) & ((kj >= BLK) | (i > 0)), s, NEG)
    head = lax.broadcasted_iota(jnp.int32, (rows, 1), 0) // BLK
    sk = jnp.zeros((rows, 1), F32)
    for g in range(grp):
        sk = jnp.where(head == g, sink_ref[first + g], sk)
    mx = jnp.maximum(jnp.max(s, axis=-1, keepdims=True), sk)
    p = jnp.exp(s - mx)
    esk = jnp.exp(sk - mx)
    inv_den = 1.0 / (jnp.sum(p, axis=-1, keepdims=True) + esk)
    return p * inv_den, esk * inv_den, head


def _norm_fwd(xraw, g):
    rstd = lax.rsqrt(jnp.mean(xraw * xraw, axis=-1, keepdims=True) + EPS)
    xhat = xraw * rstd
    return xhat, rstd, xhat * g


def _norm_bwd(dy, xhat, rstd, g):
    dxhat = dy * g
    dx = rstd * (dxhat - xhat * jnp.mean(dxhat * xhat, axis=-1, keepdims=True))
    return dx, jnp.sum(dy * xhat, axis=0, keepdims=True)


def _attn_specs(nb, grp, hd, kv, clamp):
    qo, ko, vo = 0, (kv * grp), (kv * grp + kv)
    cur = (lambda i: jnp.minimum(i, nb - 1)) if clamp else (lambda i: i)
    prev = lambda i: jnp.maximum(cur(i) - 1, 0)
    zq = pl.BlockSpec((BLK, grp * hd), lambda h, i: (cur(i), h))
    kc = pl.BlockSpec((BLK, hd), lambda h, i: (cur(i), ko + h))
    kp = pl.BlockSpec((BLK, hd), lambda h, i: (prev(i), ko + h))
    vc = pl.BlockSpec((BLK, hd), lambda h, i: (cur(i), vo + h))
    vp = pl.BlockSpec((BLK, hd), lambda h, i: (prev(i), vo + h))
    tc = pl.BlockSpec((BLK, hd), lambda h, i: (cur(i), 0))
    tp = pl.BlockSpec((BLK, hd), lambda h, i: (prev(i), 0))
    gs = pl.BlockSpec((1, hd), lambda h, i: (0, 0))
    return zq, kc, kp, vc, vp, tc, tp, gs


def _attn_fwd(z, cos_t, sin_t, qg, kg, sinks, kv, grp, hd, name, job=None):
    t = z.shape[0]
    nb = t // BLK
    half = hd // 8
    scale = 1.0 / math.sqrt(hd)
    zq, kc, kp, vc, vp, tc, tp, gs = _attn_specs(nb, grp, hd, kv, False)

    def body(sink_ref, zq_ref, kc_ref, kp_ref, vc_ref, vp_ref, cc_ref, sc_ref, cp_ref, sp_ref, qg_ref, kg_ref, o_ref):
        h, i = pl.program_id(0), pl.program_id(1)

        def normrope(xraw, g, c, s):
            return _rope(_norm_fwd(xraw, g)[2], c, s, half)

        cc, sc = cc_ref[...], sc_ref[...]
        kcur = normrope(kc_ref[...], kg_ref[...], cc, sc)
        kprev = normrope(kp_ref[...], kg_ref[...], cp_ref[...], sp_ref[...])
        kk = jnp.concatenate([kprev, kcur], axis=0).astype(BF16)
        vv = jnp.concatenate([vp_ref[...], vc_ref[...]], axis=0).astype(BF16)
        for g in range(grp):
            q = normrope(zq_ref[:, g * hd:(g + 1) * hd], qg_ref[...], cc, sc).astype(BF16)
            s = lax.dot_general(q, kk, (((1,), (1,)), ((), ())), preferred_element_type=F32) * scale
            p, _, _ = _softmax_with_sinks(s, sink_ref, h * grp + g, 1, i)
            o_ref[:, g * hd:(g + 1) * hd] = jnp.dot(p.astype(BF16), vv, preferred_element_type=F32).astype(BF16)

    res, extra = _call(
        body, name, jax.ShapeDtypeStruct((t, kv * grp * hd), BF16), (kv, nb),
        [pl.BlockSpec(memory_space=pltpu.SMEM), zq, kc, kp, vc, vp, tc, tc, tp, tp, gs, gs],
        pl.BlockSpec((BLK, grp * hd), lambda h, i: (i, h)),
        (sinks, z, z, z, z, z, cos_t, sin_t, cos_t, sin_t, qg, kg), (), job)
    return res if job is None else (res, extra)


def _attn_bwd(dattn, z, cos_t, sin_t, qg, kg, sinks, kv, grp, hd, name, job=None):
    t = z.shape[0]
    nb = t // BLK
    half = hd // 8
    scale = 1.0 / math.sqrt(hd)
    zq, kc, kp, vc, vp, tc, tp, gs = _attn_specs(nb, grp, hd, kv, True)

    def body(sink_ref, zq_ref, kc_ref, kp_ref, vc_ref, vp_ref, cc_ref, sc_ref, cp_ref, sp_ref, qg_ref, kg_ref, do_ref,
             dq_ref, dk_ref, dv_ref, dqg_ref, dkg_ref, dsk_ref, dk_carry, dv_carry):
        h, i = pl.program_id(0), pl.program_id(1)
        lane1 = lax.broadcasted_iota(jnp.int32, (1, LANE), 1)

        @pl.when((h == 0) & (i == 0))
        def _():
            dqg_ref[...] = jnp.zeros_like(dqg_ref)
            dkg_ref[...] = jnp.zeros_like(dkg_ref)
            dsk_ref[...] = jnp.zeros_like(dsk_ref)

        @pl.when(i == 0)
        def _():
            dk_carry[...] = jnp.zeros_like(dk_carry)
            dv_carry[...] = jnp.zeros_like(dv_carry)

        @pl.when(i < nb)
        def _():
            cc, sc, cp, sp = cc_ref[...], sc_ref[...], cp_ref[...], sp_ref[...]
            qgv, kgv = qg_ref[...], kg_ref[...]
            xh_kc, rs_kc, y_kc = _norm_fwd(kc_ref[...], kgv)
            xh_kp, rs_kp, y_kp = _norm_fwd(kp_ref[...], kgv)
            kk = jnp.concatenate([_rope(y_kp, cp, sp, half), _rope(y_kc, cc, sc, half)], axis=0).astype(BF16)
            vv = jnp.concatenate([vp_ref[...], vc_ref[...]], axis=0).astype(BF16)
            cq, sq = jnp.concatenate([cc] * grp, axis=0), jnp.concatenate([sc] * grp, axis=0)
            xh_q, rs_q, y_q = _norm_fwd(_stack_heads(zq_ref, grp, hd), qgv)
            q = _rope(y_q, cq, sq, half).astype(BF16)
            s = lax.dot_general(q, kk, (((1,), (1,)), ((), ())), preferred_element_type=F32) * scale
            p, psink, head = _softmax_with_sinks(s, sink_ref, h * grp, grp, i)
            dog = _stack_heads(do_ref, grp, hd).astype(BF16)
            dp = lax.dot_general(dog, vv, (((1,), (1,)), ((), ())), preferred_element_type=F32)
            rsum = jnp.sum(p * dp, axis=-1, keepdims=True)
            ds = (p * (dp - rsum) * scale).astype(BF16)
            dsink = -psink * rsum
            dsk = jnp.zeros((1, LANE), F32)
            for g in range(grp):
                dsk = dsk + jnp.where(lane1 == h * grp + g,
                                      jnp.sum(jnp.where(head == g, dsink, 0.0), axis=0, keepdims=True), 0.0)
            dqn = jnp.dot(ds, kk, preferred_element_type=F32)
            dkk = lax.dot_general(ds, q, (((0,), (0,)), ((), ())), preferred_element_type=F32)
            dvv = lax.dot_general(p.astype(BF16), dog, (((0,), (0,)), ((), ())), preferred_element_type=F32)
            dxq, dqg = _norm_bwd(_rope_bwd(dqn, cq, sq, half), xh_q, rs_q, qgv)
            dxq = dxq.astype(BF16)
            for g in range(grp):
                dq_ref[:, g * hd:(g + 1) * hd] = dxq[g * BLK:(g + 1) * BLK]
            dkp_raw, dg_kp = _norm_bwd(_rope_bwd(dkk[:BLK], cp, sp, half), xh_kp, rs_kp, kgv)
            dkc_raw, dg_kc = _norm_bwd(_rope_bwd(dkk[BLK:], cc, sc, half), xh_kc, rs_kc, kgv)
            dk_ref[...] = (dk_carry[...] + dkp_raw).astype(BF16)
            dv_ref[...] = (dv_carry[...] + dvv[:BLK]).astype(BF16)
            dk_carry[...] = dkc_raw
            dv_carry[...] = dvv[BLK:]
            dqg_ref[...] += dqg
            dkg_ref[...] += dg_kp + dg_kc
            dsk_ref[...] += dsk

        @pl.when(i == nb)
        def _():
            dk_ref[...] = dk_carry[...].astype(BF16)
            dv_ref[...] = dv_carry[...].astype(BF16)

    kvw = kv * hd
    vec = pl.BlockSpec((1, hd), lambda h, i: (0, 0))
    shifted = pl.BlockSpec((BLK, hd), lambda h, i: (jnp.maximum(i - 1, 0), h))
    res, extra = _call(
        body, name,
        (jax.ShapeDtypeStruct((t, kv * grp * hd), BF16), jax.ShapeDtypeStruct((t, kvw), BF16),
         jax.ShapeDtypeStruct((t, kvw), BF16), jax.ShapeDtypeStruct((1, hd), F32),
         jax.ShapeDtypeStruct((1, hd), F32), jax.ShapeDtypeStruct((1, LANE), F32)),
        (kv, nb + 1),
        [pl.BlockSpec(memory_space=pltpu.SMEM), zq, kc, kp, vc, vp, tc, tc, tp, tp, gs, gs,
         pl.BlockSpec((BLK, grp * hd), lambda h, i: (jnp.minimum(i, nb - 1), h))],
        (pl.BlockSpec((BLK, grp * hd), lambda h, i: (jnp.minimum(i, nb - 1), h)), shifted, shifted, vec, vec,
         pl.BlockSpec((1, LANE), lambda h, i: (0, 0))),
        (sinks, z, z, z, z, z, cos_t, sin_t, cos_t, sin_t, qg, kg, dattn),
        [pltpu.VMEM((BLK, hd), F32), pltpu.VMEM((BLK, hd), F32)], job)
    return res if job is None else (res, extra)


def _window(rows, cb, c0, row_of, col_of):
    assert rows % SUBLANE == 0 and cb % LANE == 0 and c0 % LANE == 0, (rows, cb, c0)
    return pl.BlockSpec((pl.Element(rows), pl.Element(cb)),
                        lambda *g: (pl.multiple_of(row_of(*g) * rows, SUBLANE), pl.multiple_of(c0 + col_of(*g) * cb, LANE)))


def _conv_fwd(z, c0, c, w, b, name):
    t = z.shape[0]
    taps = w.shape[0]
    cb = _pick(c, (1408, 1024, 512, 256, 128))
    tr = _row_tile(t, cb, 512 * 1024)
    hb = tr // SUBLANE

    def body(u_ref, halo_ref, w_ref, b_ref, o_ref):
        i = pl.program_id(0)
        x = u_ref[...]
        acc = b_ref[...] + w_ref[taps - 1:taps, :] * x
        for k in range(taps - 1):
            acc = acc + w_ref[k:k + 1, :] * pltpu.roll(x, taps - 1 - k, 0)
        o_ref[...] = acc
        row = lax.broadcasted_iota(jnp.int32, (SUBLANE, cb), 0)
        hp = jnp.where(i > 0, halo_ref[...], 0.0)
        x8 = u_ref[0:SUBLANE, :]
        acc8 = b_ref[...] + w_ref[taps - 1:taps, :] * x8
        for k in range(taps - 1):
            s = taps - 1 - k
            acc8 = acc8 + w_ref[k:k + 1, :] * jnp.where(row < s, pltpu.roll(hp, s, 0), pltpu.roll(x8, s, 0))
        o_ref[0:SUBLANE, :] = acc8

    blk = pl.BlockSpec((tr, cb), lambda i, j: (i, j))
    return pl.pallas_call(
        body, name=name, out_shape=jax.ShapeDtypeStruct((t, c), F32), grid=(t // tr, c // cb),
        in_specs=[_window(tr, cb, c0, lambda i, j: i, lambda i, j: j),
                  _window(SUBLANE, cb, c0, lambda i, j: jnp.maximum(i * hb - 1, 0), lambda i, j: j),
                  pl.BlockSpec((taps, cb), lambda i, j: (0, j)), pl.BlockSpec((1, cb), lambda i, j: (0, j))],
        out_specs=blk, compiler_params=_params(),
    )(z, z, w, b)


def _conv_bwd(duc, z, c0, w, dz, name):
    t, c = duc.shape
    taps = w.shape[0]
    cb = _pick(c, (1408, 1024, 512, 256, 128))
    tr = _row_tile(t, cb, 512 * 1024)
    hb, nt = tr // SUBLANE, t // tr

    def body(g_ref, gnext_ref, u_ref, uprev_ref, w_ref, dz_ref, du16_ref, dw_ref, db_ref, du_ref):
        i = pl.program_id(1)

        @pl.when(i == 0)
        def _():
            dw_ref[...] = jnp.zeros_like(dw_ref)
            db_ref[...] = jnp.zeros_like(db_ref)

        row = lax.broadcasted_iota(jnp.int32, (SUBLANE, cb), 0)
        g, x = g_ref[...], u_ref[...]
        du = w_ref[taps - 1:taps, :] * g
        for k in range(taps - 1):
            du = du + w_ref[k:k + 1, :] * pltpu.roll(g, tr - (taps - 1 - k), 0)
        du_ref[...] = du
        hn = jnp.where(i < nt - 1, gnext_ref[...], 0.0)
        g8 = g_ref[tr - SUBLANE:tr, :]
        du8 = w_ref[taps - 1:taps, :] * g8
        for k in range(taps - 1):
            s = taps - 1 - k
            du8 = du8 + w_ref[k:k + 1, :] * jnp.where(row >= SUBLANE - s, pltpu.roll(hn, SUBLANE - s, 0),
                                                     pltpu.roll(g8, SUBLANE - s, 0))
        du_ref[tr - SUBLANE:tr, :] = du8
        du16_ref[...] = du_ref[...].astype(BF16)

        hp = jnp.where(i > 0, uprev_ref[...], 0.0)
        xl8, gf8 = u_ref[tr - SUBLANE:tr, :], g_ref[0:SUBLANE, :]
        db_ref[...] += jnp.sum(g, axis=0, keepdims=True)
        dw_ref[taps - 1:taps, :] += jnp.sum(g * x, axis=0, keepdims=True)
        for k in range(taps - 1):
            s = taps - 1 - k
            fix = jnp.where(row < s, pltpu.roll(hp, s, 0) - pltpu.roll(xl8, s, 0), 0.0)
            dw_ref[k:k + 1, :] += (jnp.sum(g * pltpu.roll(x, s, 0), axis=0, keepdims=True)
                                   + jnp.sum(gf8 * fix, axis=0, keepdims=True))

    blk = pl.BlockSpec((tr, cb), lambda j, i: (i, j))
    nh = t // SUBLANE
    return pl.pallas_call(
        body, name=name,
        out_shape=(jax.ShapeDtypeStruct(dz.shape, BF16), jax.ShapeDtypeStruct((taps, c), F32),
                   jax.ShapeDtypeStruct((1, c), F32)),
        grid=(c // cb, nt),
        in_specs=[blk, pl.BlockSpec((SUBLANE, cb), lambda j, i: (jnp.minimum((i + 1) * hb, nh - 1), j)),
                  _window(tr, cb, c0, lambda j, i: i, lambda j, i: j),
                  _window(SUBLANE, cb, c0, lambda j, i: jnp.maximum(i * hb - 1, 0), lambda j, i: j),
                  pl.BlockSpec((taps, cb), lambda j, i: (0, j)), ANY],
        out_specs=(_window(tr, cb, c0, lambda j, i: i, lambda j, i: j), pl.BlockSpec((taps, cb), lambda j, i: (0, j)),
                   pl.BlockSpec((1, cb), lambda j, i: (0, j))),
        scratch_shapes=[pltpu.VMEM((tr, cb), F32)], input_output_aliases={5: 0}, compiler_params=_params(),
    )(duc, duc, z, z, w, dz)


def _gates_fwd(uc, wr, wi, gw, name):
    t, c = uc.shape
    n, bw, _ = wr.shape
    per, ng = gw // bw, c // gw
    tr = _pick(t, (512, 256, 128))

    def body(u_ref, wr_ref, wi_ref, r_ref, i_ref):
        for b in range(per):
            cols = slice(b * bw, (b + 1) * bw)
            a = u_ref[:, cols].astype(BF16)
            r_ref[:, cols] = jnp.dot(a, wr_ref[b].astype(BF16), preferred_element_type=F32)
            i_ref[:, cols] = jnp.dot(a, wi_ref[b].astype(BF16), preferred_element_type=F32)

    blk = pl.BlockSpec((tr, gw), lambda h, i: (i, h))
    wsp = pl.BlockSpec((per, bw, bw), lambda h, i: (h, 0, 0))
    shp = jax.ShapeDtypeStruct((t, c), F32)
    return pl.pallas_call(body, name=name, out_shape=(shp, shp), grid=(ng, t // tr), in_specs=[blk, wsp, wsp],
                          out_specs=(blk, blk), compiler_params=_params())(uc, wr, wi)


def _gates_bwd_x(duc, drp, dip, wr, wi, gw, name):
    t, c = duc.shape
    n, bw, _ = wr.shape
    per, ng = gw // bw, c // gw
    tr = _pick(t, (512, 256, 128))
    dims = (((1,), (1,)), ((), ()))

    def body(d_ref, r_ref, i_ref, wr_ref, wi_ref, o_ref):
        for b in range(per):
            cols = slice(b * bw, (b + 1) * bw)
            o_ref[:, cols] = (
                d_ref[:, cols]
                + lax.dot_general(r_ref[:, cols].astype(BF16), wr_ref[b].astype(BF16), dims, preferred_element_type=F32)
                + lax.dot_general(i_ref[:, cols].astype(BF16), wi_ref[b].astype(BF16), dims, preferred_element_type=F32))

    blk = pl.BlockSpec((tr, gw), lambda h, i: (i, h))
    wsp = pl.BlockSpec((per, bw, bw), lambda h, i: (h, 0, 0))
    return pl.pallas_call(body, name=name, out_shape=jax.ShapeDtypeStruct((t, c), F32), grid=(ng, t // tr),
                          in_specs=[blk, blk, blk, wsp, wsp], out_specs=blk, compiler_params=_params())(duc, drp, dip, wr, wi)


def _gates_bwd_w(uc, dpre, n, bw, gw, name):
    t, c = uc.shape
    per, ng = gw // bw, c // gw
    tk = _pick(t, (512, 256, 128))
    dims = (((0,), (0,)), ((), ()))

    def body(u_ref, d_ref, o_ref):
        @pl.when(pl.program_id(1) == 0)
        def _():
            o_ref[...] = jnp.zeros_like(o_ref)

        for b in range(per):
            cols = slice(b * bw, (b + 1) * bw)
            o_ref[b] += lax.dot_general(u_ref[:, cols].astype(BF16), d_ref[:, cols].astype(BF16), dims,
                                        preferred_element_type=F32)

    blk = pl.BlockSpec((tk, gw), lambda h, i: (i, h))
    return pl.pallas_call(body, name=name, out_shape=jax.ShapeDtypeStruct((n, bw, bw), F32), grid=(ng, t // tk),
                          in_specs=[blk, blk], out_specs=pl.BlockSpec((per, bw, bw), lambda h, i: (h, 0, 0)),
                          compiler_params=_params())(uc, dpre)


def _softplus(x):
    return jnp.maximum(x, 0.0) + jnp.log(1.0 + jnp.exp(-jnp.abs(x)))


_GELU_C = math.sqrt(2.0 / math.pi)


def _gelu_parts(x):
    inner = _GELU_C * (x + 0.044715 * (x * x * x))
    th = jnp.tanh(inner)
    gelu = 0.5 * x * (1.0 + th)
    dgelu = 0.5 * (1.0 + th) + 0.5 * x * (1.0 - th * th) * (_GELU_C * (1.0 + 3.0 * 0.044715 * (x * x)))
    return gelu, dgelu


def _lru_gate_values(rpre, ipre, br, bi, sp):
    r = _sigmoid(rpre + br)
    ig = _sigmoid(ipre + bi)
    log_a = -LRU_C * r * sp
    a = jnp.exp(log_a)
    e2 = jnp.tanh(-log_a) * (1.0 + a * a)
    inv = lax.rsqrt(jnp.maximum(e2, 1e-30))
    return r, ig, a, e2 * inv, inv


def _lru_fwd(uc, rpre, ipre, z, gr0, br, bi, lam, name, job=None):
    t, c = uc.shape
    cb = _pick(c, (1408, 1024, 512, 256, 128))
    tb = _pick(t, (512, 256, 128))
    ntile = tb // SUBLANE

    def body(uc_ref, r_ref, i_ref, gr_ref, br_ref, bi_ref, lam_ref, h_ref, rec16_ref, carry, rec_ref):
        @pl.when(pl.program_id(1) == 0)
        def _():
            carry[...] = jnp.zeros_like(carry)

        sp = _softplus(-lam_ref[...])
        br, bi = br_ref[...], bi_ref[...]
        row = lax.broadcasted_iota(jnp.int32, (SUBLANE, cb), 0)

        def tile(k, c_in):
            sl = pl.ds(pl.multiple_of(k * SUBLANE, SUBLANE), SUBLANE)
            ucv = uc_ref[sl, :]
            _, ig, a, mult, _ = _lru_gate_values(r_ref[sl, :], i_ref[sl, :], br, bi, sp)
            b = mult * (ig * ucv)
            for d in (1, 2, 4):
                a_s = jnp.where(row >= d, pltpu.roll(a, d, 0), 1.0)
                b_s = jnp.where(row >= d, pltpu.roll(b, d, 0), 0.0)
                b = a * b_s + b
                a = a * a_s
            hv = b + a * c_in
            h_ref[sl, :] = hv
            rec_ref[sl, :] = hv * _gelu_parts(gr_ref[sl, :])[0]
            return hv[SUBLANE - 1:SUBLANE, :]

        c_out = lax.fori_loop(0, ntile, tile, carry[0:1, :])
        carry[...] = jnp.broadcast_to(c_out, (SUBLANE, cb))
        rec16_ref[...] = rec_ref[...].astype(BF16)

    blk = pl.BlockSpec((tb, cb), lambda j, i: (i, j))
    vec = pl.BlockSpec((1, cb), lambda j, i: (0, j))
    res, extra = _call(body, name, (jax.ShapeDtypeStruct((t, c), F32), jax.ShapeDtypeStruct((t, c), BF16)),
                       (c // cb, t // tb),
                       [blk, blk, blk, _window(tb, cb, gr0, lambda j, i: i, lambda j, i: j), vec, vec, vec], (blk, blk),
                       (uc, rpre, ipre, z, br, bi, lam), [pltpu.VMEM((SUBLANE, cb), F32), pltpu.VMEM((tb, cb), F32)], job)
    return res if job is None else (res, extra)


def _lru_bwd(drec, hst, uc, rpre, ipre, z, gr0, br, bi, lam, dz, name, job=None):
    t, c = uc.shape
    cb = _pick(c, (1408, 1024, 512, 256, 128))
    tb = _pick(t, (256, 128))
    ntile, nt, hb = tb // SUBLANE, t // tb, tb // SUBLANE

    def body(drec_ref, h_ref, hprev_ref, uc_ref, r_ref, i_ref, gr_ref, br_ref, bi_ref, lam_ref, dz_ref,
             dgr16_ref, drp_ref, dip_ref, duc_ref, dlam_ref, dbr_ref, dbi_ref, carry, dgr_ref):
        step = pl.program_id(1)
        first_block = step == nt - 1

        @pl.when(step == 0)
        def _():
            carry[...] = jnp.zeros_like(carry)
            dlam_ref[...] = jnp.zeros_like(dlam_ref)
            dbr_ref[...] = jnp.zeros_like(dbr_ref)
            dbi_ref[...] = jnp.zeros_like(dbi_ref)

        lam = lam_ref[...]
        sp = _softplus(-lam)
        br, bi = br_ref[...], bi_ref[...]
        row = lax.broadcasted_iota(jnp.int32, (SUBLANE, cb), 0)
        halo = jnp.where(first_block, 0.0, hprev_ref[...])

        def tile(kk, state):
            c_p, acc_sp, acc_br, acc_bi = state
            k = ntile - 1 - kk
            sl = pl.ds(pl.multiple_of(k * SUBLANE, SUBLANE), SUBLANE)
            slp = pl.ds(pl.multiple_of(jnp.maximum(k - 1, 0) * SUBLANE, SUBLANE), SUBLANE)
            ucv = uc_ref[sl, :]
            r, ig, a, mult, inv_mult = _lru_gate_values(r_ref[sl, :], i_ref[sl, :], br, bi, sp)
            hv = h_ref[sl, :]
            below = jnp.where(k > 0, h_ref[slp, :], halo)
            hprev = jnp.where(row == 0, pltpu.roll(below, 1, 0), pltpu.roll(hv, 1, 0))
            gelu, dgelu = _gelu_parts(gr_ref[sl, :])
            drec = drec_ref[sl, :]
            dh = drec * gelu
            dgr_ref[sl, :] = drec * hv * dgelu
            pa, pb = a, a * dh
            for d in (1, 2, 4):
                a_s = jnp.where(row < SUBLANE - d, pltpu.roll(pa, SUBLANE - d, 0), 1.0)
                b_s = jnp.where(row < SUBLANE - d, pltpu.roll(pb, SUBLANE - d, 0), 0.0)
                pb = pa * b_s + pb
                pa = pa * a_s
            pv = pb + pa * c_p
            gt = dh + jnp.where(row == SUBLANE - 1, c_p, pltpu.roll(pv, SUBLANE - 1, 0))
            da = gt * hprev
            duc_ref[sl, :] = gt * mult * ig
            dmult = gt * ig * ucv
            dig = gt * mult * ucv
            dla = da * a - jnp.where(mult > 0.0, dmult * (a * a) * inv_mult, 0.0)
            drp = dla * (-LRU_C * sp) * (r * (1.0 - r))
            dip = dig * (ig * (1.0 - ig))
            drp_ref[sl, :] = drp
            dip_ref[sl, :] = dip
            return pv[0:1, :], acc_sp + dla * (-LRU_C * r), acc_br + drp, acc_bi + dip

        zero = jnp.zeros((SUBLANE, cb), F32)
        c_out, acc_sp, acc_br, acc_bi = lax.fori_loop(0, ntile, tile, (carry[0:1, :], zero, zero, zero))
        carry[...] = jnp.broadcast_to(c_out, (SUBLANE, cb))
        dlam_ref[...] += jnp.sum(acc_sp, axis=0, keepdims=True) * (-_sigmoid(-lam))
        dbr_ref[...] += jnp.sum(acc_br, axis=0, keepdims=True)
        dbi_ref[...] += jnp.sum(acc_bi, axis=0, keepdims=True)
        dgr16_ref[...] = dgr_ref[...].astype(BF16)

    blk = pl.BlockSpec((tb, cb), lambda j, i: (nt - 1 - i, j))
    vec = pl.BlockSpec((1, cb), lambda j, i: (0, j))
    halo_spec = pl.BlockSpec((SUBLANE, cb), lambda j, i: (jnp.maximum((nt - 1 - i) * hb - 1, 0), j))
    big, small = jax.ShapeDtypeStruct((t, c), F32), jax.ShapeDtypeStruct((1, c), F32)
    at_gr = _window(tb, cb, gr0, lambda j, i: nt - 1 - i, lambda j, i: j)
    res, extra = _call(
        body, name, (jax.ShapeDtypeStruct(dz.shape, BF16), big, big, big, small, small, small), (c // cb, nt),
        [blk, blk, halo_spec, blk, blk, blk, at_gr, vec, vec, vec, ANY], (at_gr, blk, blk, blk, vec, vec, vec),
        (drec, hst, hst, uc, rpre, ipre, z, br, bi, lam, dz),
        [pltpu.VMEM((SUBLANE, cb), F32), pltpu.VMEM((tb, cb), F32)], job, {10: 0})
    return res if job is None else (res, extra)


def _shard_region(ref, kind, chip, half, rh, width):
    if kind == "col":
        return ref.at[pl.ds(half * rh, rh), pl.ds(chip * width, width)]
    return ref.at[pl.ds(chip * (2 * rh) + half * rh, rh), :]


class _AllGather(_Exchange):
    def __init__(self, fulls, kinds):
        self.inputs, self.kinds = list(fulls), kinds
        self.out_shapes = [jax.ShapeDtypeStruct(f.shape, f.dtype) for f in fulls]
        self.aliases = {a: a for a in range(len(fulls))}
        self.n_sems = 6 * len(fulls)
        self.geo = [(f.shape[0] // 2, f.shape[1] // N_CHIPS) if k == "col" else (f.shape[0] // (2 * N_CHIPS), f.shape[1])
                    for f, k in zip(fulls, kinds)]

    def _region(self, ref, a, chip, half):
        return _shard_region(ref, self.kinds[a], chip, half, *self.geo[a])

    def _ici(self, e, a, k, chip):
        cx, cy = e.chips[k]
        return e.copy(self._region(e.ins[a], a, chip, e.c), self._region(e.outs[a], a, chip, e.c), a * 6 + k,
                      (cx, cy, e.c))

    def _d2d(self, e, a, k, half):
        cx, cy = e.chips[k]
        region = self._region(e.outs[a], a, 2 * cx + cy, half)
        return e.copy(region, region, a * 6 + 3 + k, e.sibling)

    def start(self, e):
        for a in range(len(self.inputs)):
            for k in range(3):
                self._ici(e, a, k, e.me).start()

    def finish(self, e):
        n = len(self.inputs)
        for a in range(n):
            for k, (cx, cy) in enumerate(e.chips):
                self._ici(e, a, k, 2 * cx + cy).wait_recv()
                self._d2d(e, a, k, e.c).start()
        for a in range(n):
            for k in range(3):
                self._d2d(e, a, k, 1 - e.c).wait_recv()
        for a in range(n):
            for k in range(3):
                self._ici(e, a, k, e.me).wait_send()
                self._d2d(e, a, k, e.c).wait_send()


class _SiblingExchange(_Exchange):
    def __init__(self, grads):
        self.inputs = list(grads)
        self.out_shapes = [jax.ShapeDtypeStruct((g.shape[0],) + g.shape[2:], g.dtype) for g in grads]
        self.n_sems = len(grads)

    def _copy(self, e, a):
        return e.copy(e.ins[a].at[:, 1 - e.c], e.outs[a], a, e.sibling)

    def start(self, e):
        for a in range(len(self.inputs)):
            self._copy(e, a).start()

    def finish(self, e):
        for a in range(len(self.inputs)):
            self._copy(e, a).wait()


def _piece(ref, kind, chip, width):
    if kind == "col":
        return ref.at[0, :, pl.ds(chip * width, width)]
    return ref.at[chip]


class _ChipExchange(_Exchange):
    def __init__(self, sums, kinds):
        self.inputs, self.kinds = list(sums), kinds
        self.widths = [s.shape[2] // N_CHIPS if k == "col" else s.shape[2] for s, k in zip(sums, kinds)]
        self.out_shapes = [jax.ShapeDtypeStruct((3, s.shape[1], w), s.dtype) for s, w in zip(sums, self.widths)]
        self.n_sems = 3 * len(sums)

    def _copy(self, e, a, k, chip):
        cx, cy = e.chips[k]
        return e.copy(_piece(e.ins[a], self.kinds[a], chip, self.widths[a]), e.outs[a].at[k], a * 3 + k, (cx, cy, e.c))

    def start(self, e):
        for a in range(len(self.inputs)):
            for k, (cx, cy) in enumerate(e.chips):
                self._copy(e, a, k, 2 * cx + cy).start()

    def finish(self, e):
        for a in range(len(self.inputs)):
            for k, (cx, cy) in enumerate(e.chips):
                self._copy(e, a, k, 2 * cx + cy).wait()


class _FinishExchange(_Exchange):
    def __init__(self, finals, to_all):
        self.inputs, self.to_all = list(finals), list(to_all)
        self.out_shapes = [jax.ShapeDtypeStruct(f.shape, f.dtype) for f in finals]
        self.aliases = {a: a for a in range(len(finals))}
        self.first_sem, self.n_sems = [], 0
        for all8 in self.to_all:
            self.first_sem.append(self.n_sems)
            self.n_sems += 7 if all8 else 1
        self.rel = [(fx, fy, fc) for fx in (0, 1) for fy in (0, 1) for fc in (0, 1)][1:]

    def _copies(self, e, mine):
        for a, all8 in enumerate(self.to_all):
            src = e.ins[a] if mine else e.outs[a]
            if not all8:
                rh = self.inputs[a].shape[0] // 2
                rows = pl.ds((e.c if mine else 1 - e.c) * rh, rh)
                yield e.copy(src.at[rows, :], e.outs[a].at[rows, :], self.first_sem[a], e.sibling)
                continue
            rh = self.inputs[a].shape[0] // (2 * N_CHIPS)
            for r, (fx, fy, fc) in enumerate(self.rel):
                px, py, pc = (1 - e.x if fx else e.x), (1 - e.y if fy else e.y), (1 - e.c if fc else e.c)
                rows = pl.ds(((2 * e.me + e.c) if mine else (2 * (2 * px + py) + pc)) * rh, rh)
                yield e.copy(src.at[rows, :], e.outs[a].at[rows, :], self.first_sem[a] + r, (px, py, pc))

    def start(self, e):
        for cp in self._copies(e, True):
            cp.start()

    def finish(self, e):
        for cp in self._copies(e, False):
            cp.wait_recv()
        for cp in self._copies(e, True):
            cp.wait_send()


def _cast_into_full(w, kind, idx, name):
    r, c = w.shape
    tr = _row_tile(r, c)
    nrb = r // tr

    def body(idx_ref, w_ref, o_ref):
        o_ref[...] = w_ref[...].astype(BF16)

    if kind == "col":
        full, out_map = (r, N_CHIPS * c), (lambda i, idx_ref: (i, idx_ref[1]))
    else:
        full, out_map = (N_CHIPS * r, c), (lambda i, idx_ref: (idx_ref[1] * nrb + i, 0))
    return pl.pallas_call(
        body, name=name, out_shape=jax.ShapeDtypeStruct(full, BF16),
        grid_spec=pltpu.PrefetchScalarGridSpec(
            num_scalar_prefetch=1, grid=(nrb,), in_specs=[pl.BlockSpec((tr, c), lambda i, idx_ref: (i, 0))],
            out_specs=pl.BlockSpec((tr, c), out_map)),
        compiler_params=_params(),
    )(idx, w)


def _matmul_gathering(a, placed, order, name):
    t, k = a.shape
    n = placed.shape[1]
    w = n // N_CHIPS
    tm, tn = _pick(t, _M_TILES), _pick(w, _N_TILES)
    ni, nj = t // tm, w // tn
    per_shard, total = ni * nj, N_CHIPS * ni * nj
    gather = _AllGather([placed], ["col"])

    def body(ord_ref, a_ref, w_own_ref, o_ref, w_ref, wbuf, fetch_sem, send, recv):
        s, i, j = pl.program_id(0), pl.program_id(1), pl.program_id(2)
        step = (s * ni + i) * nj + j
        e = _Env((w_own_ref,), (w_ref,), send, recv)

        def fetch(src, st):
            col = pl.multiple_of((ord_ref[st // per_shard] * nj + st % nj) * tn, LANE)
            return pltpu.make_async_copy(src.at[:, pl.ds(col, tn)], wbuf.at[st % 2], fetch_sem.at[st % 2])

        @pl.when(step == 0)
        def _():
            gather.start(e)
            fetch(w_own_ref, step).start()

        nxt = step + 1
        for kk, (cx, cy) in enumerate(e.chips):
            @pl.when(nxt == (kk + 1) * per_shard)
            def _():
                gather._ici(e, 0, kk, 2 * cx + cy).wait_recv()
                gather._d2d(e, 0, kk, e.c).start()
                gather._d2d(e, 0, kk, 1 - e.c).wait_recv()

        @pl.when(nxt < per_shard)
        def _():
            fetch(w_own_ref, nxt).start()

        @pl.when((nxt >= per_shard) & (nxt < total))
        def _():
            fetch(w_ref, nxt).start()

        fetch(w_ref, step).wait()
        o_ref[...] = jnp.dot(a_ref[...], wbuf[step % 2], preferred_element_type=F32)

        @pl.when(step == total - 1)
        def _():
            for kk in range(3):
                gather._ici(e, 0, kk, e.me).wait_send()
                gather._d2d(e, 0, kk, e.c).wait_send()

    z, full = pl.pallas_call(
        body, name=name, out_shape=(jax.ShapeDtypeStruct((t, n), F32), jax.ShapeDtypeStruct(placed.shape, placed.dtype)),
        grid_spec=pltpu.PrefetchScalarGridSpec(
            num_scalar_prefetch=1, grid=(N_CHIPS, ni, nj),
            in_specs=[pl.BlockSpec((tm, k), lambda s, i, j, ord_ref: (i, 0)), ANY],
            out_specs=(pl.BlockSpec((tm, tn), lambda s, i, j, ord_ref: (i, ord_ref[s] * nj + j)), ANY),
            scratch_shapes=[pltpu.VMEM((2, k, tn), placed.dtype), pltpu.SemaphoreType.DMA((2,)),
                            pltpu.SemaphoreType.DMA((gather.n_sems,)), pltpu.SemaphoreType.DMA((gather.n_sems,))]),
        input_output_aliases={2: 1}, compiler_params=_params(),
    )(order, a, placed)
    return z, full


def _add_own_half(g4, recv, idx, out_dtype, name):
    p, _, rh, n = g4.shape
    tr, tc = _tile2d(rh, n, 1024 * 1024)

    def body(idx_ref, g_ref, r_ref, o_ref):
        o_ref[...] = (g_ref[...] + r_ref[...]).astype(out_dtype)

    return pl.pallas_call(
        body, name=name, out_shape=jax.ShapeDtypeStruct((p, rh, n), out_dtype),
        grid_spec=pltpu.PrefetchScalarGridSpec(
            num_scalar_prefetch=1, grid=(p, rh // tr, n // tc),
            in_specs=[pl.BlockSpec((None, None, tr, tc), lambda q, i, j, idx_ref: (q, idx_ref[0], i, j)),
                      pl.BlockSpec((None, tr, tc), lambda q, i, j, idx_ref: (q, i, j))],
            out_specs=pl.BlockSpec((None, tr, tc), lambda q, i, j, idx_ref: (q, i, j))),
        compiler_params=_params(),
    )(idx, g4, recv)


def _sum_chips(own, kind, parts, idx, slots, to_all, name):
    _, rh, w = parts.shape
    tr, tc = _tile2d(rh, w, 512 * 1024)
    nrb, ncb = rh // tr, w // tc

    def body(idx_ref, own_ref, p0, p1, p2, o_ref):
        o_ref[...] = ((own_ref[...].astype(F32) + p0[...].astype(F32)) + p1[...].astype(F32)) + p2[...].astype(F32)

    if kind == "col":
        own_spec = pl.BlockSpec((None, tr, tc), lambda i, j, idx_ref: (0, i, idx_ref[1] * ncb + j))
    else:
        own_spec = pl.BlockSpec((None, tr, tc), lambda i, j, idx_ref: (idx_ref[1], i, j))
    if to_all:
        out_map = lambda i, j, idx_ref: ((2 * idx_ref[1] + idx_ref[0]) * nrb + i, j)
    else:
        out_map = lambda i, j, idx_ref: (idx_ref[0] * nrb + i, j)

    def part(k):
        return pl.BlockSpec((None, tr, tc), lambda i, j, idx_ref: (k, i, j))

    return pl.pallas_call(
        body, name=name, out_shape=jax.ShapeDtypeStruct((slots * rh, w), F32),
        grid_spec=pltpu.PrefetchScalarGridSpec(
            num_scalar_prefetch=1, grid=(nrb, ncb), in_specs=[own_spec, part(0), part(1), part(2)],
            out_specs=pl.BlockSpec((tr, tc), out_map)),
        compiler_params=_params(),
    )(idx, own, parts, parts, parts)


class _Reduce:
    def __init__(self, name, g, kind, idx, wire, to_all):
        r, c = g.shape
        self.name, self.kind, self.idx, self.wire, self.to_all = name, kind, idx, wire, to_all
        self.view = g.reshape(1, 2, r // 2, c) if kind == "col" else g.reshape(N_CHIPS, 2, r // (2 * N_CHIPS), c)

    def sibling(self):
        return _SiblingExchange([self.view])

    def got_sibling(self, outs):
        self.sum = _add_own_half(self.view, outs[0], self.idx, self.wire, "grad_chip_sum_" + self.name)

    def chips(self):
        return _ChipExchange([self.sum], [self.kind])

    def got_chips(self, outs):
        self.total = _sum_chips(self.sum, self.kind, outs[0], self.idx, 2 * N_CHIPS if self.to_all else 2,
                                self.to_all, "grad_total_" + self.name)


def _pack(arrays, rows):
    flat = jnp.concatenate([a.reshape(-1) for a in arrays])
    return jnp.pad(flat, (0, rows * SMALL_PACK_COLS - flat.shape[0])).reshape(rows, SMALL_PACK_COLS)


def _unpack(packed, shapes):
    flat = packed.reshape(-1)
    out, o = [], 0
    for shp in shapes:
        size = math.prod(shp)
        out.append(flat[o:o + size].reshape(shp))
        o += size
    return out


def _pack_rows(shapes):
    total = sum(math.prod(s) for s in shapes)
    unit = SMALL_PACK_COLS * N_CHIPS * 2 * SUBLANE
    return -(-total // unit) * (N_CHIPS * 2 * SUBLANE)


BIG = ("w_in", "w_attn_proj", "w_lru_proj", "w_out", "w_ffn_gate", "w_ffn_up", "w_ffn_down")
BIG_KIND = {"w_in": "col", "w_attn_proj": "row", "w_lru_proj": "row", "w_out": "row", "w_ffn_gate": "col",
            "w_ffn_up": "col", "w_ffn_down": "row"}
SMALL = ("norm1_g", "b_gates", "q_norm_g", "k_norm_g", "sinks", "conv_w", "conv_b", "w_rgate", "b_rgate",
         "w_igate", "b_igate", "lru_lambda", "norm2_g")
PACKED = tuple(n for n in SMALL if n not in ("w_rgate", "w_igate"))
WEIGHTS = ("norm1_g", "w_in", "b_gates", "q_norm_g", "k_norm_g", "sinks", "conv_w", "conv_b", "w_rgate", "b_rgate",
           "w_igate", "b_igate", "lru_lambda", "w_attn_proj", "w_lru_proj", "w_out", "norm2_g", "w_ffn_gate",
           "w_ffn_up", "w_ffn_down")


def kernel(x, positions, norm1_g, w_in, b_gates, q_norm_g, k_norm_g, sinks, conv_w, conv_b, w_rgate, b_rgate, w_igate, b_igate, lru_lambda, w_attn_proj, w_lru_proj, w_out, norm2_g, w_ffn_gate, w_ffn_up, w_ffn_down, loss_target, m_norm1_g, m_w_in, m_b_gates, m_q_norm_g, m_k_norm_g, m_sinks, m_conv_w, m_conv_b, m_w_rgate, m_b_rgate, m_w_igate, m_b_igate, m_lru_lambda, m_w_attn_proj, m_w_lru_proj, m_w_out, m_norm2_g, m_w_ffn_gate, m_w_ffn_up, m_w_ffn_down, v_norm1_g, v_w_in, v_b_gates, v_q_norm_g, v_k_norm_g, v_sinks, v_conv_w, v_conv_b, v_w_rgate, v_b_rgate, v_w_igate, v_b_igate, v_lru_lambda, v_w_attn_proj, v_w_lru_proj, v_w_out, v_norm2_g, v_w_ffn_gate, v_w_ffn_up, v_w_ffn_down):
    args = dict(locals())
    w = {n: args[n] for n in WEIGHTS}
    mom = {n: args["m_" + n] for n in WEIGHTS}
    var = {n: args["v_" + n] for n in WEIGHTS}

    t, d = x.shape[1], x.shape[2]
    hd = q_norm_g.shape[-1]
    nq = sinks.shape[-1]
    q_w = nq * hd
    d_rnn = conv_b.shape[-1]
    taps = conv_w.shape[1]
    n_blocks, bw = w_rgate.shape[1], w_rgate.shape[2]
    in_w = w_in.shape[-1] * N_CHIPS
    kv_w = (in_w - q_w - 2 * d_rnn - 2 * d) // 2
    kv = kv_w // hd
    grp = nq // kv
    u_off = q_w + 2 * kv_w
    gr_off = u_off + d_rnn
    ga_off = gr_off + d_rnn
    gw = bw * LANE // math.gcd(bw, LANE)
    chip = 2 * lax.axis_index("x") + lax.axis_index("y")
    idx = jnp.stack([lax.axis_index("c"), chip]).astype(jnp.int32)

    x2, tgt = x[0], loss_target[0]

    placed = {n: _cast_into_full(w[n][0], BIG_KIND[n], idx, "cast_" + n) for n in BIG}

    def gather(*names):
        return _AllGather([placed[n] for n in names], [BIG_KIND[n] for n in names])

    mx, my = lax.axis_index("x"), lax.axis_index("y")
    order = jnp.stack([chip, 2 * (1 - mx) + my, 2 * mx + (1 - my), 2 * (1 - mx) + (1 - my)]).astype(jnp.int32)
    conv_w_full = _gather_small(conv_w[0], "allgather_conv_w")
    conv_w_full = jnp.transpose(conv_w_full, (1, 0, 2)).reshape(taps, d_rnn)

    inv_freq = ROPE_THETA ** (-jnp.arange(0, hd // 4, 2, dtype=F32) / (hd // 4))
    ang = positions[0].astype(F32)[:, None] * inv_freq
    cos, sin = jnp.cos(ang), jnp.sin(ang)
    rest = hd - 2 * cos.shape[1]
    cos_t = jnp.concatenate([cos, cos, jnp.ones((t, rest), F32)], axis=1)
    sin_t = jnp.concatenate([-sin, sin, jnp.zeros((t, rest), F32)], axis=1)
    sinks1 = sinks[0]

    xn = _rms_fwd(x2, norm1_g, "rms1_fwd")
    z, win_f = _matmul_gathering(xn, placed["w_in"], order, "in_proj")
    attn, (wap_f, wlp_f, wout_f) = _attn_fwd(z, cos_t, sin_t, q_norm_g, k_norm_g, sinks1, kv, grp, hd, "attn_fwd",
                                             job=gather("w_attn_proj", "w_lru_proj", "w_out"))
    uc = _conv_fwd(z, u_off, d_rnn, conv_w_full, conv_b, "conv_fwd")
    rpre, ipre = _gates_fwd(uc, w_rgate[0], w_igate[0], gw, "gates_fwd")
    (hst, rec), (wg_f,) = _lru_fwd(uc, rpre, ipre, z, gr_off, b_rgate, b_igate, lru_lambda, "lru_fwd",
                                   job=gather("w_ffn_gate"))
    pa = _matmul(attn, wap_f, "nn", "attn_proj")
    plru, merged = _matmul(rec, wlp_f, "nn", "lru_proj", fused=(
        [pa, (z, ga_off), (z, ga_off + d), (b_gates, 0), (b_gates, d)], _merge_after_lru_proj, (F32, BF16)))
    h1 = _matmul(merged, wout_f, "nn", "out_proj", add=x2)
    hn = _rms_fwd(h1, norm2_g, "rms2_fwd")
    gate, (wu_f,) = _matmul(hn, wg_f, "nn", "ffn_gate", job=gather("w_ffn_up"))
    (up, act), (wd_f,) = _matmul(hn, wu_f, "nn", "ffn_up", job=gather("w_ffn_down"),
                                 fused=([gate], _swiglu_after_up, (F32, BF16)))
    yout = _matmul(act, wd_f, "nn", "ffn_down", add=h1)
    dy, dy16, loss_part = _loss_head(yout, tgt, "loss_head")
    loss = lax.psum(loss_part[0, 0], ("x", "y", "c"))

    def reduction(n, g):
        return _Reduce(n, g, BIG_KIND[n], idx, BF16, False)

    r_wd = reduction("w_ffn_down", _matmul(act, dy16, "tn", "d_w_ffn_down"))
    (dgate, dup), got = _matmul(dy16, wd_f, "nt", "d_act", job=r_wd.sibling(),
                                fused=([gate, up], _swiglu_bwd_after_dact, (BF16, BF16)))
    r_wd.got_sibling(got)
    r_wg = reduction("w_ffn_gate", _matmul(hn, dgate, "tn", "d_w_ffn_gate"))
    g_wu, got = _matmul(hn, dup, "tn", "d_w_ffn_up", job=r_wg.sibling())
    r_wg.got_sibling(got)
    r_wu = reduction("w_ffn_up", g_wu)
    dhn, got = _matmul(dgate, wg_f, "nt", "d_hn_gate", job=r_wu.sibling())
    r_wu.got_sibling(got)
    dhn = _matmul(dup, wu_f, "nt", "d_hn_up", add=dhn)
    dh1, g_norm2, dh1_16 = _rms_bwd(dhn, h1, norm2_g, dy, "rms2_bwd", mxu_copy=True)
    r_wout = reduction("w_out", _matmul(merged, dh1_16, "tn", "d_w_out"))
    dmerged, got = _matmul(dh1_16, wout_f, "nt", "d_merged", job=r_wout.sibling())
    r_wout.got_sibling(got)
    (dpa, dz, g_ba), got = _merge_bwd(dmerged, z, b_gates, pa, ga_off, 0, None, "merge_bwd_attn", job=r_wout.chips())
    r_wout.got_chips(got)
    dpl, dz, g_bl = _merge_bwd(dmerged, z, b_gates, plru, ga_off + d, d, dz, "merge_bwd_lru")
    r_wap = reduction("w_attn_proj", _matmul(attn, dpa, "tn", "d_w_attn_proj"))
    dattn, got = _matmul(dpa, wap_f, "nt", "d_attn", job=r_wap.sibling())
    r_wap.got_sibling(got)
    g_wlp, got = _matmul(rec, dpl, "tn", "d_w_lru_proj", job=r_wap.chips())
    r_wap.got_chips(got)
    r_wlp = reduction("w_lru_proj", g_wlp)
    drec, got = _matmul(dpl, wlp_f, "nt", "d_rec", job=r_wlp.sibling())
    r_wlp.got_sibling(got)
    both = _Jobs(r_wlp.chips(), r_wg.chips())
    (dz, drp, dip, duc_direct, g_lam, g_br, g_bi), got = _lru_bwd(
        drec, hst, uc, rpre, ipre, z, gr_off, b_rgate, b_igate, lru_lambda, dz, "lru_bwd", job=both)
    got_wlp, got_wg = both.split(got)
    r_wlp.got_chips(got_wlp)
    r_wg.got_chips(got_wg)
    duc = _gates_bwd_x(duc_direct, drp, dip, w_rgate[0], w_igate[0], gw, "gates_bwd_x")
    g_wr = _gates_bwd_w(uc, drp, n_blocks, bw, gw, "gates_bwd_wr")
    g_wi = _gates_bwd_w(uc, dip, n_blocks, bw, gw, "gates_bwd_wi")
    dz, g_convw, g_convb = _conv_bwd(duc, z, u_off, conv_w_full, dz, "conv_bwd")
    (dq, dk, dv, g_qg, g_kg, g_sinks), got = _attn_bwd(dattn, z, cos_t, sin_t, q_norm_g, k_norm_g, sinks1, kv, grp, hd,
                                                        "attn_bwd", job=_Jobs(r_wu.chips(), r_wd.chips()))
    r_wu.got_chips(got[:1])
    r_wd.got_chips(got[1:])
    for part, col in ((dq, 0), (dk, q_w), (dv, q_w + kv_w)):
        dz = lax.dynamic_update_slice(dz, part, (0, col))
    r_wr = _Reduce("w_rgate", g_wr.reshape(n_blocks * bw, bw), "row", idx, F32, True)
    r_wi = _Reduce("w_igate", g_wi.reshape(n_blocks * bw, bw), "row", idx, F32, True)
    early = [r_wap, r_wlp, r_wout, r_wg, r_wu, r_wd]
    three = _Jobs(r_wr.sibling(), r_wi.sibling(), _FinishExchange([r.total for r in early], [False] * len(early)))
    g_top, got = _matmul(xn, dz, "tn", "d_w_in_top", m_window=(0, d // 2), job=three)
    got_wr, got_wi, finished = three.split(got)
    r_wr.got_sibling(got_wr)
    r_wi.got_sibling(got_wi)
    r_top = _Reduce("w_in_top", g_top, "col", idx, BF16, False)
    three = _Jobs(r_top.sibling(), r_wr.chips(), r_wi.chips())
    g_bot, got = _matmul(xn, dz, "tn", "d_w_in_bot", m_window=(d // 2, d // 2), job=three)
    got_top, got_wr, got_wi = three.split(got)
    r_top.got_sibling(got_top)
    r_wr.got_chips(got_wr)
    r_wi.got_chips(got_wi)
    r_bot = _Reduce("w_in_bot", g_bot, "col", idx, BF16, False)
    both = _Jobs(r_top.chips(), r_bot.sibling())
    dxn, got = _matmul(dz, win_f, "nt", "d_xn_a", m_window=(0, t // 2), into=(None, t), job=both)
    got_top, got_bot = both.split(got)
    r_top.got_chips(got_top)
    r_bot.got_sibling(got_bot)
    dxn, got = _matmul(dz, win_f, "nt", "d_xn_b", m_window=(t // 2, t // 2), into=(dxn, t), job=r_bot.chips())
    r_bot.got_chips(got)
    dx, g_norm1 = _rms_bwd(dxn, x2, norm1_g, dh1, "rms1_bwd")

    small_grads = {"norm1_g": g_norm1, "b_gates": jnp.concatenate([g_ba, g_bl], axis=1), "q_norm_g": g_qg,
                   "k_norm_g": g_kg, "sinks": g_sinks[:, :nq], "conv_w": g_convw, "conv_b": g_convb,
                   "b_rgate": g_br, "b_igate": g_bi, "lru_lambda": g_lam, "norm2_g": g_norm2}
    gshapes = [small_grads[n].shape for n in PACKED]
    small_sum, (top, bot, grads_wr, grads_wi) = _allreduce_small(
        _pack([small_grads[n] for n in PACKED], _pack_rows(gshapes)),
        _FinishExchange([r.total for r in (r_top, r_bot, r_wr, r_wi)], [False, False, True, True]), "grad_last_exchange")
    grads = dict(zip(BIG[1:], finished))
    grads["w_in"] = jnp.concatenate([top, bot], axis=0)
    grads["w_rgate"], grads["w_igate"] = grads_wr, grads_wi
    small_full = dict(zip(PACKED, _unpack(small_sum, gshapes)))
    per = d_rnn // N_CHIPS
    small_full["conv_w"] = lax.dynamic_slice(small_full["conv_w"], (0, chip * per), (taps, per))
    grads.update(small_full)

    delta, new_m, new_v = {}, {}, {}
    for n in BIG + ("w_rgate", "w_igate"):
        as2d = (lambda a: a[0]) if n in BIG else (lambda a: a.reshape(n_blocks * bw, bw))
        if n == "w_in":
            delta[n], new_m[n], new_v[n] = _adamw(as2d(w[n]), grads[n], as2d(mom[n]), as2d(var[n]), "adamw_" + n)
        else:
            delta[n], new_m[n], new_v[n], grads[n] = _adamw(as2d(w[n]), grads[n], as2d(mom[n]), as2d(var[n]),
                                                            "adamw_" + n, pass_grad=True)
    pshapes = [w[n].shape for n in PACKED]
    prows = _pack_rows(pshapes)
    pk = [_pack([src[n] for n in PACKED], prows) for src in (w, grads, mom, var)]
    for res, packed in zip((delta, new_m, new_v), _adamw(pk[0], pk[1], pk[2], pk[3], "adamw_small")):
        res.update(dict(zip(PACKED, _unpack(packed, pshapes))))

    outs = [loss, dx.reshape(x.shape)]
    for res in (grads, delta, new_m, new_v):
        outs += [res[n].reshape(w[n].shape) for n in WEIGHTS]
    return tuple(outs)


def _allreduce_small(x, job, name):
    n_dev = 2 * N_CHIPS
    rel = [(fx, fy, fc) for fx in (0, 1) for fy in (0, 1) for fc in (0, 1)][1:]
    j_in, j_out = len(job.inputs), len(job.out_shapes)

    def body(x_ref, *refs):
        jins, all_ref, o_ref, jouts = refs[:j_in], refs[j_in], refs[j_in + 1], refs[j_in + 2:j_in + 2 + j_out]
        send_sems, recv_sems, jsend, jrecv = refs[j_in + 2 + j_out:]
        e = _Env((x_ref,), (all_ref,), send_sems, recv_sems)
        carried = _Env(jins, jouts, jsend, jrecv, 0, (e.x, e.y, e.c))
        job.start(carried)
        mine = 2 * e.me + e.c

        def peer(r):
            fx, fy, fc = rel[r]
            return (1 - e.x if fx else e.x), (1 - e.y if fy else e.y), (1 - e.c if fc else e.c)

        all_ref[mine] = x_ref[...]
        for r in range(len(rel)):
            e.copy(x_ref, all_ref.at[mine], r, peer(r)).start()
        for r in range(len(rel)):
            px, py, pc = peer(r)
            e.copy(x_ref, all_ref.at[2 * (2 * px + py) + pc], r, peer(r)).wait_recv()
        total = all_ref[0]
        for dev in range(1, n_dev):
            total = total + all_ref[dev]
        o_ref[...] = total
        for r in range(len(rel)):
            e.copy(x_ref, all_ref.at[mine], r, peer(r)).wait_send()
        job.finish(carried)

    vm = pl.BlockSpec(memory_space=pltpu.VMEM)
    res = pl.pallas_call(
        body, name=name,
        out_shape=(jax.ShapeDtypeStruct((n_dev,) + x.shape, x.dtype), jax.ShapeDtypeStruct(x.shape, x.dtype))
        + tuple(job.out_shapes),
        in_specs=[vm] + [ANY] * j_in, out_specs=(vm, vm) + (ANY,) * j_out,
        input_output_aliases={1 + i: 2 + o for i, o in job.aliases.items()},
        scratch_shapes=[pltpu.SemaphoreType.DMA((len(rel),)), pltpu.SemaphoreType.DMA((len(rel),)),
                        pltpu.SemaphoreType.DMA((job.n_sems,)), pltpu.SemaphoreType.DMA((job.n_sems,))])(x, *job.inputs)
    return res[1], tuple(res[2:])


def _gather_small(shard, name):
    def body(s_ref, o_ref, send_sems, recv_sems):
        e = _Env((s_ref,), (o_ref,), send_sems, recv_sems)
        o_ref[e.me] = s_ref[...]
        for k, (cx, cy) in enumerate(e.chips):
            e.copy(s_ref, o_ref.at[e.me], k, (cx, cy, e.c)).start()
        for k, (cx, cy) in enumerate(e.chips):
            e.copy(s_ref, o_ref.at[2 * cx + cy], k, (cx, cy, e.c)).wait_recv()
        for k, (cx, cy) in enumerate(e.chips):
            e.copy(s_ref, o_ref.at[e.me], k, (cx, cy, e.c)).wait_send()

    vm = pl.BlockSpec(memory_space=pltpu.VMEM)
    return pl.pallas_call(body, name=name, out_shape=jax.ShapeDtypeStruct((N_CHIPS,) + shard.shape, shard.dtype),
                          in_specs=[vm], out_specs=vm,
                          scratch_shapes=[pltpu.SemaphoreType.DMA((3,)), pltpu.SemaphoreType.DMA((3,))])(shard)
```

```python
import functools
import math

import jax
import jax.numpy as jnp
from jax import lax
from jax.experimental import pallas as pl
from jax.experimental.pallas import tpu as pltpu

F32 = jnp.float32
BF16 = jnp.bfloat16
MESH = pl.DeviceIdType.MESH

WINDOW = 128
BLK = 128
ROPE_THETA = 500000.0
LRU_C = 8.0
EPS = 1e-6
NEG = -1e30
ADAM_LR = 0.001
ADAM_B1 = 0.9
ADAM_B2 = 0.999
ADAM_EPS = 1e-08
ADAM_WD = 0.01
ADAM_STEP = 10

VMEM_LIMIT_BYTES = 52 * 1024 * 1024
LANE = 128
SUBLANE = 8
N_CHIPS = 4
SMALL_PACK_COLS = 512


def _params(**kw):
    return pltpu.CompilerParams(vmem_limit_bytes=VMEM_LIMIT_BYTES, **kw)


def _pick(dim, cands):
    for c in cands:
        if dim % c == 0:
            return c
    return dim


def _sigmoid(x):
    return 0.5 * jnp.tanh(0.5 * x) + 0.5


ANY = pl.BlockSpec(memory_space=pl.ANY)


class _Env:
    def __init__(self, ins, outs, send, recv, sem0=0, place=None):
        self.ins, self.outs, self.send, self.recv, self.sem0 = ins, outs, send, recv, sem0
        self.x, self.y, self.c = place or (lax.axis_index("x"), lax.axis_index("y"), lax.axis_index("c"))
        self.me = 2 * self.x + self.y
        self.chips = [(1 - self.x, self.y), (self.x, 1 - self.y), (1 - self.x, 1 - self.y)]
        self.sibling = (self.x, self.y, 1 - self.c)

    def sub(self, i0, n_in, o0, n_out, sem0):
        return _Env(self.ins[i0:i0 + n_in], self.outs[o0:o0 + n_out], self.send, self.recv, self.sem0 + sem0,
                    (self.x, self.y, self.c))

    def copy(self, src, dst, sem, to):
        return pltpu.make_async_remote_copy(src_ref=src, dst_ref=dst, send_sem=self.send.at[self.sem0 + sem],
                                            recv_sem=self.recv.at[self.sem0 + sem], device_id=to, device_id_type=MESH)


class _Exchange:
    inputs, out_shapes, aliases, n_sems = (), (), {}, 0

    def start(self, e):
        raise NotImplementedError

    def finish(self, e):
        raise NotImplementedError


class _Jobs(_Exchange):
    def __init__(self, *jobs):
        self.jobs, self.inputs, self.out_shapes, self.aliases, self.n_sems, self.at = jobs, [], [], {}, 0, []
        for job in jobs:
            self.at.append((len(self.inputs), len(self.out_shapes), self.n_sems))
            self.aliases.update({len(self.inputs) + i: len(self.out_shapes) + o for i, o in job.aliases.items()})
            self.inputs += list(job.inputs)
            self.out_shapes += list(job.out_shapes)
            self.n_sems += job.n_sems

    def _each(self, e):
        for job, (i0, o0, s0) in zip(self.jobs, self.at):
            yield job, e.sub(i0, len(job.inputs), o0, len(job.out_shapes), s0)

    def split(self, outs):
        return [tuple(outs[o0:o0 + len(job.out_shapes)]) for job, (_, o0, _) in zip(self.jobs, self.at)]

    def start(self, e):
        for job, se in self._each(e):
            job.start(se)

    def finish(self, e):
        for job, se in self._each(e):
            job.finish(se)


def _call(body, name, out_shape, grid, in_specs, out_specs, args, scratch_shapes=(), job=None, aliases=None):
    aliases = dict(aliases or {})
    if job is None:
        return pl.pallas_call(body, name=name, out_shape=out_shape, grid=grid, in_specs=list(in_specs),
                              out_specs=out_specs, scratch_shapes=list(scratch_shapes), input_output_aliases=aliases,
                              compiler_params=_params())(*args), ()
    single = not isinstance(out_shape, (tuple, list))
    shapes = [out_shape] if single else list(out_shape)
    ospecs = [out_specs] if single else list(out_specs)
    n_in, n_out, n_scr = len(args), len(shapes), len(scratch_shapes)
    j_in, j_out = len(job.inputs), len(job.out_shapes)

    def hosted(*refs):
        ins, jins = refs[:n_in], refs[n_in:n_in + j_in]
        outs = refs[n_in + j_in:n_in + j_in + n_out]
        jouts = refs[n_in + j_in + n_out:n_in + j_in + n_out + j_out]
        rest = refs[n_in + j_in + n_out + j_out:]
        e = _Env(jins, jouts, rest[n_scr], rest[n_scr + 1])
        first = functools.reduce(jnp.logical_and, [pl.program_id(d) == 0 for d in range(len(grid))])
        last = functools.reduce(jnp.logical_and, [pl.program_id(d) == g - 1 for d, g in enumerate(grid)])

        @pl.when(first)
        def _():
            job.start(e)

        body(*ins, *outs, *rest[:n_scr])

        @pl.when(last)
        def _():
            job.finish(e)

    res = pl.pallas_call(
        hosted, name=name, out_shape=tuple(shapes + list(job.out_shapes)), grid=grid,
        in_specs=list(in_specs) + [ANY] * j_in, out_specs=tuple(ospecs + [ANY] * j_out),
        scratch_shapes=list(scratch_shapes) + [pltpu.SemaphoreType.DMA((job.n_sems,)),
                                               pltpu.SemaphoreType.DMA((job.n_sems,))],
        input_output_aliases={**aliases, **{n_in + i: n_out + o for i, o in job.aliases.items()}},
        compiler_params=_params())(*args, *job.inputs)
    return (res[0] if single else tuple(res[:n_out])), tuple(res[n_out:])


_M_TILES = (1024, 1408, 1280, 512, 256, 128)
_N_TILES = (1408, 1280, 1024, 640, 512, 256, 128)
MXU_FULL_ROWS = 1024
MATMUL_VMEM_BUDGET = 42 * 1024 * 1024
MXU_FLOPS_PER_HBM_BYTE = 500


def _matmul_tiles(m, n, k, sa, sb, so, has_add, tn_divides=0):
    best = None
    for tm in [c for c in _M_TILES if m % c == 0] or [m]:
        for tn in [c for c in _N_TILES if n % c == 0 and tn_divides % c == 0] or [n]:
            for nk in range(1, 17):
                tk = k // nk
                if k % nk or tk % LANE:
                    continue
                need = 2 * (tm * tk * sa + tk * tn * sb) + 2 * tm * tn * (so + (4 if has_add else 0))
                need += tm * tn * 4 if nk > 1 else 0
                fetched = tk * tn * sb + tm * tk * sa // (1 if nk > 1 else n // tn)
                if need > MATMUL_VMEM_BUDGET:
                    continue
                mxu_bound = fetched * MXU_FLOPS_PER_HBM_BYTE <= 2 * tm * tn * tk
                key = (mxu_bound, min(tm, MXU_FULL_ROWS), -nk, tn, tm)
                if best is None or key > best[0]:
                    best = (key, (tm, tn, tk))
    assert best is not None, (m, n, k)
    return best[1]


def _matmul(a, b, mode, name, add=None, out_dtype=F32, job=None, m_window=None, into=None, fused=None):
    if mode == "nn":
        (m, k), (k2, n) = a.shape, b.shape
    elif mode == "nt":
        (m, k), (n, k2) = a.shape, b.shape
    else:
        (k, m), (k2, n) = a.shape, b.shape
    assert k == k2, (a.shape, b.shape, mode)
    m0, m = m_window or (0, m)
    tiles, fuse_fn, out_dtypes = fused or ((), None, (out_dtype,))
    tiles = [x if isinstance(x, tuple) else (x, 0) for x in tiles]
    tm, tn, tk = _matmul_tiles(math.gcd(m, m0) if m0 else m, n, k, a.dtype.itemsize, b.dtype.itemsize,
                               sum(jnp.dtype(dt).itemsize for dt in out_dtypes)
                               + sum(x.dtype.itemsize for x, _ in tiles if x.shape[0] > 1),
                               add is not None, math.gcd(*[c0 for _, c0 in tiles], 0))
    nk, mb0 = k // tk, m0 // tm
    if mode == "nn":
        a_spec = pl.BlockSpec((tm, tk), lambda i, j, kk: (mb0 + i, kk))
        b_spec = pl.BlockSpec((tk, tn), lambda i, j, kk: (kk, j))
        dims = (((1,), (0,)), ((), ()))
    elif mode == "nt":
        a_spec = pl.BlockSpec((tm, tk), lambda i, j, kk: (mb0 + i, kk))
        b_spec = pl.BlockSpec((tn, tk), lambda i, j, kk: (j, kk))
        dims = (((1,), (1,)), ((), ()))
    else:
        a_spec = pl.BlockSpec((tk, tm), lambda i, j, kk: (kk, mb0 + i))
        b_spec = pl.BlockSpec((tk, tn), lambda i, j, kk: (kk, j))
        dims = (((0,), (0,)), ((), ()))
    out_rows, ob0 = (into[1], mb0) if into is not None else (m, 0)
    o_spec = pl.BlockSpec((tm, tn), lambda i, j, kk: (ob0 + i, j))
    has_add = add is not None
    begun = into is not None and into[0] is not None

    n_side, n_out = has_add + len(tiles), len(out_dtypes)

    def body(*refs):
        a_ref, b_ref = refs[:2]
        side = refs[2:2 + n_side]
        o_refs = refs[len(refs) - n_out - (nk > 1):len(refs) - (nk > 1)]
        part = lax.dot_general(a_ref[...].astype(BF16), b_ref[...].astype(BF16), dims, preferred_element_type=F32)

        def finish(r):
            if has_add:
                r = r + side[0][...]
            vals = fuse_fn(r, *[x[...] for x in side[has_add:]]) if fuse_fn else (r,)
            for o_ref, val, dt in zip(o_refs, vals, out_dtypes):
                o_ref[...] = val.astype(dt)

        if nk == 1:
            finish(part)
            return
        acc = refs[-1]
        kk = pl.program_id(2)

        @pl.when(kk == 0)
        def _():
            acc[...] = part

        @pl.when(kk > 0)
        def _():
            acc[...] += part

        @pl.when(kk == nk - 1)
        def _():
            finish(acc[...])

    def side_spec(x, c0):
        if x.shape[0] == 1:
            return pl.BlockSpec((1, tn), lambda i, j, kk: (0, c0 // tn + j))
        return pl.BlockSpec((tm, tn), lambda i, j, kk: (mb0 + i, c0 // tn + j))

    in_specs = [a_spec, b_spec] + ([side_spec(add, 0)] if has_add else []) + [side_spec(x, c0) for x, c0 in tiles]
    args = (a, b) + ((add,) if has_add else ()) + tuple(x for x, _ in tiles)
    aliases = None
    if begun:
        aliases = {len(args): 0}
        in_specs, args = in_specs + [ANY], args + (into[0],)
    shapes = tuple(jax.ShapeDtypeStruct((out_rows, n), dt) for dt in out_dtypes)
    res, extra = _call(body, name, shapes if fused else shapes[0], (m // tm, n // tn, nk), in_specs,
                       (o_spec,) * n_out if fused else o_spec, args, [pltpu.VMEM((tm, tn), F32)] if nk > 1 else [],
                       job, aliases)
    return res if job is None else (res, extra)


def _row_tile(rows, cols, budget_elems=512 * 1024):
    cands = [c for c in (1024, 704, 512, 352, 256, 128, 64, 32, 16) if c * cols <= budget_elems]
    return _pick(rows, cands or (16,))


_EW_COLS = (1280, 1408, 1024, 640, 512, 256, 128)


def _tile2d(rows, cols, max_elems):
    tc = _pick(cols, _EW_COLS)
    return _row_tile(rows, tc, max_elems), tc


def _rms_fwd(x, g, name):
    t, d = x.shape
    tr = _row_tile(t, d)

    def body(x_ref, g_ref, o_ref):
        xv = x_ref[...]
        rstd = lax.rsqrt(jnp.mean(xv * xv, axis=-1, keepdims=True) + EPS)
        o_ref[...] = (xv * rstd * g_ref[...]).astype(BF16)

    spec = pl.BlockSpec((tr, d), lambda i: (i, 0))
    return pl.pallas_call(body, name=name, out_shape=jax.ShapeDtypeStruct((t, d), BF16), grid=(t // tr,),
                          in_specs=[spec, pl.BlockSpec((1, d), lambda i: (0, 0))], out_specs=spec,
                          compiler_params=_params())(x, g)


def _rms_bwd(dxn, x, g, resid, name, job=None, mxu_copy=False):
    t, d = x.shape
    tr = _row_tile(t, d, 512 * 1024)

    def body(dxn_ref, x_ref, g_ref, r_ref, dx_ref, dg_ref, *dx16_ref):
        @pl.when(pl.program_id(0) == 0)
        def _():
            dg_ref[...] = jnp.zeros_like(dg_ref)

        xv = x_ref[...]
        rstd = lax.rsqrt(jnp.mean(xv * xv, axis=-1, keepdims=True) + EPS)
        xhat = xv * rstd
        dy = dxn_ref[...]
        dg_ref[...] += jnp.sum(dy * xhat, axis=0, keepdims=True)
        dxhat = dy * g_ref[...]
        dx = r_ref[...] + rstd * (dxhat - xhat * jnp.mean(dxhat * xhat, axis=-1, keepdims=True))
        dx_ref[...] = dx
        if mxu_copy:
            dx16_ref[0][...] = dx.astype(BF16)

    spec = pl.BlockSpec((tr, d), lambda i: (i, 0))
    vec = pl.BlockSpec((1, d), lambda i: (0, 0))
    shapes = (jax.ShapeDtypeStruct((t, d), F32), jax.ShapeDtypeStruct((1, d), F32))
    shapes += (jax.ShapeDtypeStruct((t, d), BF16),) if mxu_copy else ()
    res, extra = _call(body, name, shapes, (t // tr,), [spec, spec, vec, spec],
                       (spec, vec) + ((spec,) if mxu_copy else ()), (dxn, x, g, resid), (), job)
    return res if job is None else (res, extra)


def _swiglu_after_up(up, gate):
    return up, gate * _sigmoid(gate) * up


def _swiglu_bwd_after_dact(dact, gate, up):
    sg = _sigmoid(gate)
    return dact * up * (sg * (1.0 + gate * (1.0 - sg))), dact * (gate * sg)


def _merge_after_lru_proj(plru, pa, ga, gl, ba, bl):
    return plru, _sigmoid(ga + ba) * pa + _sigmoid(gl + bl) * plru


def _merge_bwd(dmerged, z, b_gates, p, z0, b0, dz, name, job=None):
    t, d = p.shape
    cw = _pick(math.gcd(z0, d), (512, 256, 128))
    tr = _row_tile(t, cw, 256 * 1024)
    oz, ob, nd = z0 // cw, b0 // cw, d // cw

    def body(dm_ref, g_ref, b_ref, p_ref, *rest):
        dp_ref, dg_ref, sum_ref = rest[-3:]

        @pl.when(pl.program_id(1) == 0)
        def _():
            sum_ref[...] = jnp.zeros_like(sum_ref)

        dm = dm_ref[...]
        sg = _sigmoid(g_ref[...] + b_ref[...])
        dp_ref[...] = (dm * sg).astype(BF16)
        dg = dm * p_ref[...] * (sg * (1.0 - sg))
        dg_ref[...] = dg.astype(BF16)
        sum_ref[...] += jnp.sum(dg, axis=0, keepdims=True)

    blk = pl.BlockSpec((tr, cw), lambda j, i: (i, j))
    at_z = pl.BlockSpec((tr, cw), lambda j, i: (i, oz + j))
    in_specs = [blk, at_z, pl.BlockSpec((1, cw), lambda j, i: (0, ob + j)), blk]
    args, aliases = (dmerged, z, b_gates, p), None
    if dz is not None:
        in_specs, args, aliases = in_specs + [ANY], args + (dz,), {4: 1}
    res, extra = _call(
        body, name, (jax.ShapeDtypeStruct((t, d), BF16), jax.ShapeDtypeStruct(z.shape, BF16),
                     jax.ShapeDtypeStruct((1, d), F32)), (nd, t // tr), in_specs,
        (blk, at_z, pl.BlockSpec((1, cw), lambda j, i: (0, j))), args, (), job, aliases)
    return res if job is None else (res, extra)


def _loss_head(y, target, name):
    t, d = y.shape
    tr = _row_tile(t, d, 512 * 1024)
    nt = t // tr

    def body(y_ref, t_ref, dy_ref, dy16_ref, loss_ref, acc):
        i = pl.program_id(0)

        @pl.when(i == 0)
        def _():
            acc[...] = jnp.zeros_like(acc)

        e = y_ref[...] - t_ref[...]
        dy = e * (1.0 / d)
        dy_ref[...] = dy
        dy16_ref[...] = dy.astype(BF16)
        acc[...] += jnp.sum(e * e, axis=0, keepdims=True)

        @pl.when(i == nt - 1)
        def _():
            loss_ref[...] = (0.5 / d) * jnp.sum(acc[...], axis=-1, keepdims=True)

    spec = pl.BlockSpec((tr, d), lambda i: (i, 0))
    return pl.pallas_call(
        body, name=name, out_shape=(jax.ShapeDtypeStruct((t, d), F32), jax.ShapeDtypeStruct((t, d), BF16),
                                    jax.ShapeDtypeStruct((1, 1), F32)),
        grid=(nt,), in_specs=[spec, spec], out_specs=(spec, spec, pl.BlockSpec((1, 1), lambda i: (0, 0))),
        scratch_shapes=[pltpu.VMEM((1, d), F32)], compiler_params=_params(),
    )(y, target)


def _adamw(w, g, m, v, name, pass_grad=False):
    r, c = w.shape
    tr, tc = _tile2d(r, c, 512 * 1024)
    c1 = 1.0 - ADAM_B1 ** ADAM_STEP
    c2 = 1.0 - ADAM_B2 ** ADAM_STEP

    def body(w_ref, g_ref, m_ref, v_ref, d_ref, nm_ref, nv_ref, *g_out):
        gv = g_ref[...]
        if pass_grad:
            g_out[0][...] = gv
        mn = ADAM_B1 * m_ref[...] + (1.0 - ADAM_B1) * gv
        vn = ADAM_B2 * v_ref[...] + (1.0 - ADAM_B2) * (gv * gv)
        d_ref[...] = -ADAM_LR * ((mn / c1) / (jnp.sqrt(vn / c2) + ADAM_EPS) + ADAM_WD * w_ref[...])
        nm_ref[...] = mn
        nv_ref[...] = vn

    spec = pl.BlockSpec((tr, tc), lambda i, j: (i, j))
    shp = jax.ShapeDtypeStruct((r, c), F32)
    n_out = 4 if pass_grad else 3
    return pl.pallas_call(body, name=name, out_shape=(shp,) * n_out, grid=(r // tr, c // tc), in_specs=[spec] * 4,
                          out_specs=(spec,) * n_out, compiler_params=_params())(w, g, m, v)


def _swap_halves(v, half):
    n = v.shape[-1]
    lane = lax.broadcasted_iota(jnp.int32, v.shape, 1)
    return jnp.where(lane < half, pltpu.roll(v, n - half, 1),
                     jnp.where(lane < 2 * half, pltpu.roll(v, half, 1), 0.0))


def _rope(y, c, s, half):
    return y * c + _swap_halves(y, half) * s


def _rope_bwd(dout, c, s, half):
    return dout * c + _swap_halves(dout * s, half)


def _stack_heads(ref, grp, hd):
    return jnp.concatenate([ref[:, g * hd:(g + 1) * hd] for g in range(grp)], axis=0)


def _softmax_with_sinks(s, sink_ref, first, grp, i):
    rows = s.shape[0]
    qi = lax.broadcasted_iota(jnp.int32, s.shape, 0) & (BLK - 1)
    kj = lax.broadcasted_iota(jnp.int32, s.shape, 1)
    rel = qi + BLK - kj
    s = jnp.where((rel >= 0) & (rel < WINDOW) & ((kj >= BLK) | (i > 0)), s, NEG)
    head = lax.broadcasted_iota(jnp.int32, (rows, 1), 0) // BLK
    sk = jnp.zeros((rows, 1), F32)
    for g in range(grp):
        sk = jnp.where(head == g, sink_ref[first + g], sk)
    mx = jnp.maximum(jnp.max(s, axis=-1, keepdims=True), sk)
    p = jnp.exp(s - mx)
    esk = jnp.exp(sk - mx)
    inv_den = 1.0 / (jnp.sum(p, axis=-1, keepdims=True) + esk)
    return p * inv_den, esk * inv_den, head


def _norm_fwd(xraw, g):
    rstd = lax.rsqrt(jnp.mean(xraw * xraw, axis=-1, keepdims=True) + EPS)
    xhat = xraw * rstd
    return xhat, rstd, xhat * g


def _norm_bwd(dy, xhat, rstd, g):
    dxhat = dy * g
    dx = rstd * (dxhat - xhat * jnp.mean(dxhat * xhat, axis=-1, keepdims=True))
    return dx, jnp.sum(dy * xhat, axis=0, keepdims=True)


def _attn_specs(nb, grp, hd, kv, clamp):
    qo, ko, vo = 0, (kv * grp), (kv * grp + kv)
    cur = (lambda i: jnp.minimum(i, nb - 1)) if clamp else (lambda i: i)
    prev = lambda i: jnp.maximum(cur(i) - 1, 0)
    zq = pl.BlockSpec((BLK, grp * hd), lambda h, i: (cur(i), h))
    kc = pl.BlockSpec((BLK, hd), lambda h, i: (cur(i), ko + h))
    kp = pl.BlockSpec((BLK, hd), lambda h, i: (prev(i), ko + h))
    vc = pl.BlockSpec((BLK, hd), lambda h, i: (cur(i), vo + h))
    vp = pl.BlockSpec((BLK, hd), lambda h, i: (prev(i), vo + h))
    tc = pl.BlockSpec((BLK, hd), lambda h, i: (cur(i), 0))
    tp = pl.BlockSpec((BLK, hd), lambda h, i: (prev(i), 0))
    gs = pl.BlockSpec((1, hd), lambda h, i: (0, 0))
    return zq, kc, kp, vc, vp, tc, tp, gs


def _attn_fwd(z, cos_t, sin_t, qg, kg, sinks, kv, grp, hd, name, job=None):
    t = z.shape[0]
    nb = t // BLK
    half = hd // 8
    scale = 1.0 / math.sqrt(hd)
    zq, kc, kp, vc, vp, tc, tp, gs = _attn_specs(nb, grp, hd, kv, False)

    def body(sink_ref, zq_ref, kc_ref, kp_ref, vc_ref, vp_ref, cc_ref, sc_ref, cp_ref, sp_ref, qg_ref, kg_ref, o_ref):
        h, i = pl.program_id(0), pl.program_id(1)

        def normrope(xraw, g, c, s):
            return _rope(_norm_fwd(xraw, g)[2], c, s, half)

        cc, sc = cc_ref[...], sc_ref[...]
        kcur = normrope(kc_ref[...], kg_ref[...], cc, sc)
        kprev = normrope(kp_ref[...], kg_ref[...], cp_ref[...], sp_ref[...])
        kk = jnp.concatenate([kprev, kcur], axis=0).astype(BF16)
        vv = jnp.concatenate([vp_ref[...], vc_ref[...]], axis=0).astype(BF16)
        for g in range(grp):
            q = normrope(zq_ref[:, g * hd:(g + 1) * hd], qg_ref[...], cc, sc).astype(BF16)
            s = lax.dot_general(q, kk, (((1,), (1,)), ((), ())), preferred_element_type=F32) * scale
            p, _, _ = _softmax_with_sinks(s, sink_ref, h * grp + g, 1, i)
            o_ref[:, g * hd:(g + 1) * hd] = jnp.dot(p.astype(BF16), vv, preferred_element_type=F32).astype(BF16)

    res, extra = _call(
        body, name, jax.ShapeDtypeStruct((t, kv * grp * hd), BF16), (kv, nb),
        [pl.BlockSpec(memory_space=pltpu.SMEM), zq, kc, kp, vc, vp, tc, tc, tp, tp, gs, gs],
        pl.BlockSpec((BLK, grp * hd), lambda h, i: (i, h)),
        (sinks, z, z, z, z, z, cos_t, sin_t, cos_t, sin_t, qg, kg), (), job)
    return res if job is None else (res, extra)


def _attn_bwd(dattn, z, cos_t, sin_t, qg, kg, sinks, kv, grp, hd, name, job=None):
    t = z.shape[0]
    nb = t // BLK
    half = hd // 8
    scale = 1.0 / math.sqrt(hd)
    zq, kc, kp, vc, vp, tc, tp, gs = _attn_specs(nb, grp, hd, kv, True)

    def body(sink_ref, zq_ref, kc_ref, kp_ref, vc_ref, vp_ref, cc_ref, sc_ref, cp_ref, sp_ref, qg_ref, kg_ref, do_ref,
             dq_ref, dk_ref, dv_ref, dqg_ref, dkg_ref, dsk_ref, dk_carry, dv_carry):
        h, i = pl.program_id(0), pl.program_id(1)
        lane1 = lax.broadcasted_iota(jnp.int32, (1, LANE), 1)

        @pl.when((h == 0) & (i == 0))
        def _():
            dqg_ref[...] = jnp.zeros_like(dqg_ref)
            dkg_ref[...] = jnp.zeros_like(dkg_ref)
            dsk_ref[...] = jnp.zeros_like(dsk_ref)

        @pl.when(i == 0)
        def _():
            dk_carry[...] = jnp.zeros_like(dk_carry)
            dv_carry[...] = jnp.zeros_like(dv_carry)

        @pl.when(i < nb)
        def _():
            cc, sc, cp, sp = cc_ref[...], sc_ref[...], cp_ref[...], sp_ref[...]
            qgv, kgv = qg_ref[...], kg_ref[...]
            xh_kc, rs_kc, y_kc = _norm_fwd(kc_ref[...], kgv)
            xh_kp, rs_kp, y_kp = _norm_fwd(kp_ref[...], kgv)
            kk = jnp.concatenate([_rope(y_kp, cp, sp, half), _rope(y_kc, cc, sc, half)], axis=0).astype(BF16)
            vv = jnp.concatenate([vp_ref[...], vc_ref[...]], axis=0).astype(BF16)
            cq, sq = jnp.concatenate([cc] * grp, axis=0), jnp.concatenate([sc] * grp, axis=0)
            xh_q, rs_q, y_q = _norm_fwd(_stack_heads(zq_ref, grp, hd), qgv)
            q = _rope(y_q, cq, sq, half).astype(BF16)
            s = lax.dot_general(q, kk, (((1,), (1,)), ((), ())), preferred_element_type=F32) * scale
            p, psink, head = _softmax_with_sinks(s, sink_ref, h * grp, grp, i)
            dog = _stack_heads(do_ref, grp, hd).astype(BF16)
            dp = lax.dot_general(dog, vv, (((1,), (1,)), ((), ())), preferred_element_type=F32)
            rsum = jnp.sum(p * dp, axis=-1, keepdims=True)
            ds = (p * (dp - rsum) * scale).astype(BF16)
            dsink = -psink * rsum
            dsk = jnp.zeros((1, LANE), F32)
            for g in range(grp):
                dsk = dsk + jnp.where(lane1 == h * grp + g,
                                      jnp.sum(jnp.where(head == g, dsink, 0.0), axis=0, keepdims=True), 0.0)
            dqn = jnp.dot(ds, kk, preferred_element_type=F32)
            dkk = lax.dot_general(ds, q, (((0,), (0,)), ((), ())), preferred_element_type=F32)
            dvv = lax.dot_general(p.astype(BF16), dog, (((0,), (0,)), ((), ())), preferred_element_type=F32)
            dxq, dqg = _norm_bwd(_rope_bwd(dqn, cq, sq, half), xh_q, rs_q, qgv)
            dxq = dxq.astype(BF16)
            for g in range(grp):
                dq_ref[:, g * hd:(g + 1) * hd] = dxq[g * BLK:(g + 1) * BLK]
            dkp_raw, dg_kp = _norm_bwd(_rope_bwd(dkk[:BLK], cp, sp, half), xh_kp, rs_kp, kgv)
            dkc_raw, dg_kc = _norm_bwd(_rope_bwd(dkk[BLK:], cc, sc, half), xh_kc, rs_kc, kgv)
            dk_ref[...] = (dk_carry[...] + dkp_raw).astype(BF16)
            dv_ref[...] = (dv_carry[...] + dvv[:BLK]).astype(BF16)
            dk_carry[...] = dkc_raw
            dv_carry[...] = dvv[BLK:]
            dqg_ref[...] += dqg
            dkg_ref[...] += dg_kp + dg_kc
            dsk_ref[...] += dsk

        @pl.when(i == nb)
        def _():
            dk_ref[...] = dk_carry[...].astype(BF16)
            dv_ref[...] = dv_carry[...].astype(BF16)

    kvw = kv * hd
    vec = pl.BlockSpec((1, hd), lambda h, i: (0, 0))
    shifted = pl.BlockSpec((BLK, hd), lambda h, i: (jnp.maximum(i - 1, 0), h))
    res, extra = _call(
        body, name,
        (jax.ShapeDtypeStruct((t, kv * grp * hd), BF16), jax.ShapeDtypeStruct((t, kvw), BF16),
         jax.ShapeDtypeStruct((t, kvw), BF16), jax.ShapeDtypeStruct((1, hd), F32),
         jax.ShapeDtypeStruct((1, hd), F32), jax.ShapeDtypeStruct((1, LANE), F32)),
        (kv, nb + 1),
        [pl.BlockSpec(memory_space=pltpu.SMEM), zq, kc, kp, vc, vp, tc, tc, tp, tp, gs, gs,
         pl.BlockSpec((BLK, grp * hd), lambda h, i: (jnp.minimum(i, nb - 1), h))],
        (pl.BlockSpec((BLK, grp * hd), lambda h, i: (jnp.minimum(i, nb - 1), h)), shifted, shifted, vec, vec,
         pl.BlockSpec((1, LANE), lambda h, i: (0, 0))),
        (sinks, z, z, z, z, z, cos_t, sin_t, cos_t, sin_t, qg, kg, dattn),
        [pltpu.VMEM((BLK, hd), F32), pltpu.VMEM((BLK, hd), F32)], job)
    return res if job is None else (res, extra)


def _window(rows, cb, c0, row_of, col_of):
    assert rows % SUBLANE == 0 and cb % LANE == 0 and c0 % LANE == 0, (rows, cb, c0)
    return pl.BlockSpec((pl.Element(rows), pl.Element(cb)),
                        lambda *g: (pl.multiple_of(row_of(*g) * rows, SUBLANE), pl.multiple_of(c0 + col_of(*g) * cb, LANE)))


def _conv_fwd(z, c0, c, w, b, name):
    t = z.shape[0]
    taps = w.shape[0]
    cb = _pick(c, (1408, 1024, 512, 256, 128))
    tr = _row_tile(t, cb, 512 * 1024)
    hb = tr // SUBLANE

    def body(u_ref, halo_ref, w_ref, b_ref, o_ref):
        i = pl.program_id(0)
        x = u_ref[...]
        acc = b_ref[...] + w_ref[taps - 1:taps, :] * x
        for k in range(taps - 1):
            acc = acc + w_ref[k:k + 1, :] * pltpu.roll(x, taps - 1 - k, 0)
        o_ref[...] = acc
        row = lax.broadcasted_iota(jnp.int32, (SUBLANE, cb), 0)
        hp = jnp.where(i > 0, halo_ref[...], 0.0)
        x8 = u_ref[0:SUBLANE, :]
        acc8 = b_ref[...] + w_ref[taps - 1:taps, :] * x8
        for k in range(taps - 1):
            s = taps - 1 - k
            acc8 = acc8 + w_ref[k:k + 1, :] * jnp.where(row < s, pltpu.roll(hp, s, 0), pltpu.roll(x8, s, 0))
        o_ref[0:SUBLANE, :] = acc8

    blk = pl.BlockSpec((tr, cb), lambda i, j: (i, j))
    return pl.pallas_call(
        body, name=name, out_shape=jax.ShapeDtypeStruct((t, c), F32), grid=(t // tr, c // cb),
        in_specs=[_window(tr, cb, c0, lambda i, j: i, lambda i, j: j),
                  _window(SUBLANE, cb, c0, lambda i, j: jnp.maximum(i * hb - 1, 0), lambda i, j: j),
                  pl.BlockSpec((taps, cb), lambda i, j: (0, j)), pl.BlockSpec((1, cb), lambda i, j: (0, j))],
        out_specs=blk, compiler_params=_params(),
    )(z, z, w, b)


def _conv_bwd(duc, z, c0, w, dz, name):
    t, c = duc.shape
    taps = w.shape[0]
    cb = _pick(c, (1408, 1024, 512, 256, 128))
    tr = _row_tile(t, cb, 512 * 1024)
    hb, nt = tr // SUBLANE, t // tr

    def body(g_ref, gnext_ref, u_ref, uprev_ref, w_ref, dz_ref, du16_ref, dw_ref, db_ref, du_ref):
        i = pl.program_id(1)

        @pl.when(i == 0)
        def _():
            dw_ref[...] = jnp.zeros_like(dw_ref)
            db_ref[...] = jnp.zeros_like(db_ref)

        row = lax.broadcasted_iota(jnp.int32, (SUBLANE, cb), 0)
        g, x = g_ref[...], u_ref[...]
        du = w_ref[taps - 1:taps, :] * g
        for k in range(taps - 1):
            du = du + w_ref[k:k + 1, :] * pltpu.roll(g, tr - (taps - 1 - k), 0)
        du_ref[...] = du
        hn = jnp.where(i < nt - 1, gnext_ref[...], 0.0)
        g8 = g_ref[tr - SUBLANE:tr, :]
        du8 = w_ref[taps - 1:taps, :] * g8
        for k in range(taps - 1):
            s = taps - 1 - k
            du8 = du8 + w_ref[k:k + 1, :] * jnp.where(row >= SUBLANE - s, pltpu.roll(hn, SUBLANE - s, 0),
                                                     pltpu.roll(g8, SUBLANE - s, 0))
        du_ref[tr - SUBLANE:tr, :] = du8
        du16_ref[...] = du_ref[...].astype(BF16)

        hp = jnp.where(i > 0, uprev_ref[...], 0.0)
        xl8, gf8 = u_ref[tr - SUBLANE:tr, :], g_ref[0:SUBLANE, :]
        db_ref[...] += jnp.sum(g, axis=0, keepdims=True)
        dw_ref[taps - 1:taps, :] += jnp.sum(g * x, axis=0, keepdims=True)
        for k in range(taps - 1):
            s = taps - 1 - k
            fix = jnp.where(row < s, pltpu.roll(hp, s, 0) - pltpu.roll(xl8, s, 0), 0.0)
            dw_ref[k:k + 1, :] += (jnp.sum(g * pltpu.roll(x, s, 0), axis=0, keepdims=True)
                                   + jnp.sum(gf8 * fix, axis=0, keepdims=True))

    blk = pl.BlockSpec((tr, cb), lambda j, i: (i, j))
    nh = t // SUBLANE
    return pl.pallas_call(
        body, name=name,
        out_shape=(jax.ShapeDtypeStruct(dz.shape, BF16), jax.ShapeDtypeStruct((taps, c), F32),
                   jax.ShapeDtypeStruct((1, c), F32)),
        grid=(c // cb, nt),
        in_specs=[blk, pl.BlockSpec((SUBLANE, cb), lambda j, i: (jnp.minimum((i + 1) * hb, nh - 1), j)),
                  _window(tr, cb, c0, lambda j, i: i, lambda j, i: j),
                  _window(SUBLANE, cb, c0, lambda j, i: jnp.maximum(i * hb - 1, 0), lambda j, i: j),
                  pl.BlockSpec((taps, cb), lambda j, i: (0, j)), ANY],
        out_specs=(_window(tr, cb, c0, lambda j, i: i, lambda j, i: j), pl.BlockSpec((taps, cb), lambda j, i: (0, j)),
                   pl.BlockSpec((1, cb), lambda j, i: (0, j))),
        scratch_shapes=[pltpu.VMEM((tr, cb), F32)], input_output_aliases={5: 0}, compiler_params=_params(),
    )(duc, duc, z, z, w, dz)


def _gates_fwd(uc, wr, wi, gw, name):
    t, c = uc.shape
    n, bw, _ = wr.shape
    per, ng = gw // bw, c // gw
    tr = _pick(t, (512, 256, 128))

    def body(u_ref, wr_ref, wi_ref, r_ref, i_ref):
        for b in range(per):
            cols = slice(b * bw, (b + 1) * bw)
            a = u_ref[:, cols].astype(BF16)
            r_ref[:, cols] = jnp.dot(a, wr_ref[b].astype(BF16), preferred_element_type=F32)
            i_ref[:, cols] = jnp.dot(a, wi_ref[b].astype(BF16), preferred_element_type=F32)

    blk = pl.BlockSpec((tr, gw), lambda h, i: (i, h))
    wsp = pl.BlockSpec((per, bw, bw), lambda h, i: (h, 0, 0))
    shp = jax.ShapeDtypeStruct((t, c), F32)
    return pl.pallas_call(body, name=name, out_shape=(shp, shp), grid=(ng, t // tr), in_specs=[blk, wsp, wsp],
                          out_specs=(blk, blk), compiler_params=_params())(uc, wr, wi)


def _gates_bwd_x(duc, drp, dip, wr, wi, gw, name):
    t, c = duc.shape
    n, bw, _ = wr.shape
    per, ng = gw // bw, c // gw
    tr = _pick(t, (512, 256, 128))
    dims = (((1,), (1,)), ((), ()))

    def body(d_ref, r_ref, i_ref, wr_ref, wi_ref, o_ref):
        for b in range(per):
            cols = slice(b * bw, (b + 1) * bw)
            o_ref[:, cols] = (
                d_ref[:, cols]
                + lax.dot_general(r_ref[:, cols].astype(BF16), wr_ref[b].astype(BF16), dims, preferred_element_type=F32)
                + lax.dot_general(i_ref[:, cols].astype(BF16), wi_ref[b].astype(BF16), dims, preferred_element_type=F32))

    blk = pl.BlockSpec((tr, gw), lambda h, i: (i, h))
    wsp = pl.BlockSpec((per, bw, bw), lambda h, i: (h, 0, 0))
    return pl.pallas_call(body, name=name, out_shape=jax.ShapeDtypeStruct((t, c), F32), grid=(ng, t // tr),
                          in_specs=[blk, blk, blk, wsp, wsp], out_specs=blk, compiler_params=_params())(duc, drp, dip, wr, wi)


def _gates_bwd_w(uc, dpre, n, bw, gw, name):
    t, c = uc.shape
    per, ng = gw // bw, c // gw
    tk = _pick(t, (512, 256, 128))
    dims = (((0,), (0,)), ((), ()))

    def body(u_ref, d_ref, o_ref):
        @pl.when(pl.program_id(1) == 0)
        def _():
            o_ref[...] = jnp.zeros_like(o_ref)

        for b in range(per):
            cols = slice(b * bw, (b + 1) * bw)
            o_ref[b] += lax.dot_general(u_ref[:, cols].astype(BF16), d_ref[:, cols].astype(BF16), dims,
                                        preferred_element_type=F32)

    blk = pl.BlockSpec((tk, gw), lambda h, i: (i, h))
    return pl.pallas_call(body, name=name, out_shape=jax.ShapeDtypeStruct((n, bw, bw), F32), grid=(ng, t // tk),
                          in_specs=[blk, blk], out_specs=pl.BlockSpec((per, bw, bw), lambda h, i: (h, 0, 0)),
                          compiler_params=_params())(uc, dpre)


def _softplus(x):
    return jnp.maximum(x, 0.0) + jnp.log(1.0 + jnp.exp(-jnp.abs(x)))


_GELU_C = math.sqrt(2.0 / math.pi)


def _gelu_parts(x):
    inner = _GELU_C * (x + 0.044715 * (x * x * x))
    th = jnp.tanh(inner)
    gelu = 0.5 * x * (1.0 + th)
    dgelu = 0.5 * (1.0 + th) + 0.5 * x * (1.0 - th * th) * (_GELU_C * (1.0 + 3.0 * 0.044715 * (x * x)))
    return gelu, dgelu


def _lru_gate_values(rpre, ipre, br, bi, sp):
    r = _sigmoid(rpre + br)
    ig = _sigmoid(ipre + bi)
    log_a = -LRU_C * r * sp
    a = jnp.exp(log_a)
    e2 = jnp.tanh(-log_a) * (1.0 + a * a)
    inv = lax.rsqrt(jnp.maximum(e2, 1e-30))
    return r, ig, a, e2 * inv, inv


def _lru_fwd(uc, rpre, ipre, z, gr0, br, bi, lam, name, job=None):
    t, c = uc.shape
    cb = _pick(c, (1408, 1024, 512, 256, 128))
    tb = _pick(t, (512, 256, 128))
    ntile = tb // SUBLANE

    def body(uc_ref, r_ref, i_ref, gr_ref, br_ref, bi_ref, lam_ref, h_ref, rec16_ref, carry, rec_ref):
        @pl.when(pl.program_id(1) == 0)
        def _():
            carry[...] = jnp.zeros_like(carry)

        sp = _softplus(-lam_ref[...])
        br, bi = br_ref[...], bi_ref[...]
        row = lax.broadcasted_iota(jnp.int32, (SUBLANE, cb), 0)

        def tile(k, c_in):
            sl = pl.ds(pl.multiple_of(k * SUBLANE, SUBLANE), SUBLANE)
            ucv = uc_ref[sl, :]
            _, ig, a, mult, _ = _lru_gate_values(r_ref[sl, :], i_ref[sl, :], br, bi, sp)
            b = mult * (ig * ucv)
            for d in (1, 2, 4):
                a_s = jnp.where(row >= d, pltpu.roll(a, d, 0), 1.0)
                b_s = jnp.where(row >= d, pltpu.roll(b, d, 0), 0.0)
                b = a * b_s + b
                a = a * a_s
            hv = b + a * c_in
            h_ref[sl, :] = hv
            rec_ref[sl, :] = hv * _gelu_parts(gr_ref[sl, :])[0]
            return hv[SUBLANE - 1:SUBLANE, :]

        c_out = lax.fori_loop(0, ntile, tile, carry[0:1, :])
        carry[...] = jnp.broadcast_to(c_out, (SUBLANE, cb))
        rec16_ref[...] = rec_ref[...].astype(BF16)

    blk = pl.BlockSpec((tb, cb), lambda j, i: (i, j))
    vec = pl.BlockSpec((1, cb), lambda j, i: (0, j))
    res, extra = _call(body, name, (jax.ShapeDtypeStruct((t, c), F32), jax.ShapeDtypeStruct((t, c), BF16)),
                       (c // cb, t // tb),
                       [blk, blk, blk, _window(tb, cb, gr0, lambda j, i: i, lambda j, i: j), vec, vec, vec], (blk, blk),
                       (uc, rpre, ipre, z, br, bi, lam), [pltpu.VMEM((SUBLANE, cb), F32), pltpu.VMEM((tb, cb), F32)], job)
    return res if job is None else (res, extra)


def _lru_bwd(drec, hst, uc, rpre, ipre, z, gr0, br, bi, lam, dz, name, job=None):
    t, c = uc.shape
    cb = _pick(c, (1408, 1024, 512, 256, 128))
    tb = _pick(t, (256, 128))
    ntile, nt, hb = tb // SUBLANE, t // tb, tb // SUBLANE

    def body(drec_ref, h_ref, hprev_ref, uc_ref, r_ref, i_ref, gr_ref, br_ref, bi_ref, lam_ref, dz_ref,
             dgr16_ref, drp_ref, dip_ref, duc_ref, dlam_ref, dbr_ref, dbi_ref, carry, dgr_ref):
        step = pl.program_id(1)
        first_block = step == nt - 1

        @pl.when(step == 0)
        def _():
            carry[...] = jnp.zeros_like(carry)
            dlam_ref[...] = jnp.zeros_like(dlam_ref)
            dbr_ref[...] = jnp.zeros_like(dbr_ref)
            dbi_ref[...] = jnp.zeros_like(dbi_ref)

        lam = lam_ref[...]
        sp = _softplus(-lam)
        br, bi = br_ref[...], bi_ref[...]
        row = lax.broadcasted_iota(jnp.int32, (SUBLANE, cb), 0)
        halo = jnp.where(first_block, 0.0, hprev_ref[...])

        def tile(kk, state):
            c_p, acc_sp, acc_br, acc_bi = state
            k = ntile - 1 - kk
            sl = pl.ds(pl.multiple_of(k * SUBLANE, SUBLANE), SUBLANE)
            slp = pl.ds(pl.multiple_of(jnp.maximum(k - 1, 0) * SUBLANE, SUBLANE), SUBLANE)
            ucv = uc_ref[sl, :]
            r, ig, a, mult, inv_mult = _lru_gate_values(r_ref[sl, :], i_ref[sl, :], br, bi, sp)
            hv = h_ref[sl, :]
            below = jnp.where(k > 0, h_ref[slp, :], halo)
            hprev = jnp.where(row == 0, pltpu.roll(below, 1, 0), pltpu.roll(hv, 1, 0))
            gelu, dgelu = _gelu_parts(gr_ref[sl, :])
            drec = drec_ref[sl, :]
            dh = drec * gelu
            dgr_ref[sl, :] = drec * hv * dgelu
            pa, pb = a, a * dh
            for d in (1, 2, 4):
                a_s = jnp.where(row < SUBLANE - d, pltpu.roll(pa, SUBLANE - d, 0), 1.0)
                b_s = jnp.where(row < SUBLANE - d, pltpu.roll(pb, SUBLANE - d, 0), 0.0)
                pb = pa * b_s + pb
                pa = pa * a_s
            pv = pb + pa * c_p
            gt = dh + jnp.where(row == SUBLANE - 1, c_p, pltpu.roll(pv, SUBLANE - 1, 0))
            da = gt * hprev
            duc_ref[sl, :] = gt * mult * ig
            dmult = gt * ig * ucv
            dig = gt * mult * ucv
            dla = da * a - jnp.where(mult > 0.0, dmult * (a * a) * inv_mult, 0.0)
            drp = dla * (-LRU_C * sp) * (r * (1.0 - r))
            dip = dig * (ig * (1.0 - ig))
            drp_ref[sl, :] = drp
            dip_ref[sl, :] = dip
            return pv[0:1, :], acc_sp + dla * (-LRU_C * r), acc_br + drp, acc_bi + dip

        zero = jnp.zeros((SUBLANE, cb), F32)
        c_out, acc_sp, acc_br, acc_bi = lax.fori_loop(0, ntile, tile, (carry[0:1, :], zero, zero, zero))
        carry[...] = jnp.broadcast_to(c_out, (SUBLANE, cb))
        dlam_ref[...] += jnp.sum(acc_sp, axis=0, keepdims=True) * (-_sigmoid(-lam))
        dbr_ref[...] += jnp.sum(acc_br, axis=0, keepdims=True)
        dbi_ref[...] += jnp.sum(acc_bi, axis=0, keepdims=True)
        dgr16_ref[...] = dgr_ref[...].astype(BF16)

    blk = pl.BlockSpec((tb, cb), lambda j, i: (nt - 1 - i, j))
    vec = pl.BlockSpec((1, cb), lambda j, i: (0, j))
    halo_spec = pl.BlockSpec((SUBLANE, cb), lambda j, i: (jnp.maximum((nt - 1 - i) * hb - 1, 0), j))
    big, small = jax.ShapeDtypeStruct((t, c), F32), jax.ShapeDtypeStruct((1, c), F32)
    at_gr = _window(tb, cb, gr0, lambda j, i: nt - 1 - i, lambda j, i: j)
    res, extra = _call(
        body, name, (jax.ShapeDtypeStruct(dz.shape, BF16), big, big, big, small, small, small), (c // cb, nt),
        [blk, blk, halo_spec, blk, blk, blk, at_gr, vec, vec, vec, ANY], (at_gr, blk, blk, blk, vec, vec, vec),
        (drec, hst, hst, uc, rpre, ipre, z, br, bi, lam, dz),
        [pltpu.VMEM((SUBLANE, cb), F32), pltpu.VMEM((tb, cb), F32)], job, {10: 0})
    return res if job is None else (res, extra)


def _shard_region(ref, kind, chip, half, rh, width, part=(0, 1, 1)):
    p0, p1, n = part
    r0, rows = p0 * (rh // n), (p1 - p0) * (rh // n)
    if kind == "col":
        return ref.at[pl.ds(half * rh + r0, rows), pl.ds(chip * width, width)]
    return ref.at[pl.ds(chip * (2 * rh) + half * rh + r0, rows), :]


class _AllGather(_Exchange):
    def __init__(self, fulls, kinds, part=(0, 1, 1)):
        self.inputs, self.kinds, self.part = list(fulls), kinds, part
        self.out_shapes = [jax.ShapeDtypeStruct(f.shape, f.dtype) for f in fulls]
        self.aliases = {a: a for a in range(len(fulls))}
        self.n_sems = 6 * len(fulls)
        self.geo = [(f.shape[0] // 2, f.shape[1] // N_CHIPS) if k == "col" else (f.shape[0] // (2 * N_CHIPS), f.shape[1])
                    for f, k in zip(fulls, kinds)]
        assert all(rh % part[2] == 0 for rh, _ in self.geo), (self.geo, part)

    def _region(self, ref, a, chip, half):
        return _shard_region(ref, self.kinds[a], chip, half, *self.geo[a], self.part)

    def _ici(self, e, a, k, chip):
        cx, cy = e.chips[k]
        return e.copy(self._region(e.ins[a], a, chip, e.c), self._region(e.outs[a], a, chip, e.c), a * 6 + k,
                      (cx, cy, e.c))

    def _d2d(self, e, a, k, half):
        cx, cy = e.chips[k]
        region = self._region(e.outs[a], a, 2 * cx + cy, half)
        return e.copy(region, region, a * 6 + 3 + k, e.sibling)

    def start(self, e):
        for a in range(len(self.inputs)):
            for k in range(3):
                self._ici(e, a, k, e.me).start()

    def finish(self, e):
        n = len(self.inputs)
        for a in range(n):
            for k, (cx, cy) in enumerate(e.chips):
                self._ici(e, a, k, 2 * cx + cy).wait_recv()
                self._d2d(e, a, k, e.c).start()
        for a in range(n):
            for k in range(3):
                self._d2d(e, a, k, 1 - e.c).wait_recv()
        for a in range(n):
            for k in range(3):
                self._ici(e, a, k, e.me).wait_send()
                self._d2d(e, a, k, e.c).wait_send()


class _SiblingExchange(_Exchange):
    def __init__(self, grads):
        self.inputs = list(grads)
        self.out_shapes = [jax.ShapeDtypeStruct((g.shape[0],) + g.shape[2:], g.dtype) for g in grads]
        self.n_sems = len(grads)

    def _copy(self, e, a):
        return e.copy(e.ins[a].at[:, 1 - e.c], e.outs[a], a, e.sibling)

    def start(self, e):
        for a in range(len(self.inputs)):
            self._copy(e, a).start()

    def finish(self, e):
        for a in range(len(self.inputs)):
            self._copy(e, a).wait()


def _piece(ref, kind, chip, width):
    if kind == "col":
        return ref.at[0, :, pl.ds(chip * width, width)]
    return ref.at[chip]


class _ChipExchange(_Exchange):
    def __init__(self, sums, kinds):
        self.inputs, self.kinds = list(sums), kinds
        self.widths = [s.shape[2] // N_CHIPS if k == "col" else s.shape[2] for s, k in zip(sums, kinds)]
        self.out_shapes = [jax.ShapeDtypeStruct((3, s.shape[1], w), s.dtype) for s, w in zip(sums, self.widths)]
        self.n_sems = 3 * len(sums)

    def _copy(self, e, a, k, chip):
        cx, cy = e.chips[k]
        return e.copy(_piece(e.ins[a], self.kinds[a], chip, self.widths[a]), e.outs[a].at[k], a * 3 + k, (cx, cy, e.c))

    def start(self, e):
        for a in range(len(self.inputs)):
            for k, (cx, cy) in enumerate(e.chips):
                self._copy(e, a, k, 2 * cx + cy).start()

    def finish(self, e):
        for a in range(len(self.inputs)):
            for k, (cx, cy) in enumerate(e.chips):
                self._copy(e, a, k, 2 * cx + cy).wait()


class _FinishExchange(_Exchange):
    def __init__(self, finals, to_all):
        self.inputs, self.to_all = list(finals), list(to_all)
        self.out_shapes = [jax.ShapeDtypeStruct(f.shape, f.dtype) for f in finals]
        self.aliases = {a: a for a in range(len(finals))}
        self.first_sem, self.n_sems = [], 0
        for all8 in self.to_all:
            self.first_sem.append(self.n_sems)
            self.n_sems += 7 if all8 else 1
        self.rel = [(fx, fy, fc) for fx in (0, 1) for fy in (0, 1) for fc in (0, 1)][1:]

    def _copies(self, e, mine):
        for a, all8 in enumerate(self.to_all):
            src = e.ins[a] if mine else e.outs[a]
            if not all8:
                rh = self.inputs[a].shape[0] // 2
                rows = pl.ds((e.c if mine else 1 - e.c) * rh, rh)
                yield e.copy(src.at[rows, :], e.outs[a].at[rows, :], self.first_sem[a], e.sibling)
                continue
            rh = self.inputs[a].shape[0] // (2 * N_CHIPS)
            for r, (fx, fy, fc) in enumerate(self.rel):
                px, py, pc = (1 - e.x if fx else e.x), (1 - e.y if fy else e.y), (1 - e.c if fc else e.c)
                rows = pl.ds(((2 * e.me + e.c) if mine else (2 * (2 * px + py) + pc)) * rh, rh)
                yield e.copy(src.at[rows, :], e.outs[a].at[rows, :], self.first_sem[a] + r, (px, py, pc))

    def start(self, e):
        for cp in self._copies(e, True):
            cp.start()

    def finish(self, e):
        for cp in self._copies(e, False):
            cp.wait_recv()
        for cp in self._copies(e, True):
            cp.wait_send()


def _cast_into_full(w, kind, idx, name):
    r, c = w.shape
    tr = _row_tile(r, c)
    nrb = r // tr

    def body(idx_ref, w_ref, o_ref):
        o_ref[...] = w_ref[...].astype(BF16)

    if kind == "col":
        full, out_map = (r, N_CHIPS * c), (lambda i, idx_ref: (i, idx_ref[1]))
    else:
        full, out_map = (N_CHIPS * r, c), (lambda i, idx_ref: (idx_ref[1] * nrb + i, 0))
    return pl.pallas_call(
        body, name=name, out_shape=jax.ShapeDtypeStruct(full, BF16),
        grid_spec=pltpu.PrefetchScalarGridSpec(
            num_scalar_prefetch=1, grid=(nrb,), in_specs=[pl.BlockSpec((tr, c), lambda i, idx_ref: (i, 0))],
            out_specs=pl.BlockSpec((tr, c), out_map)),
        compiler_params=_params(),
    )(idx, w)


def _matmul_gathering(a, placed, order, name):
    t, k = a.shape
    n = placed.shape[1]
    w = n // N_CHIPS
    tm, tn = _pick(t, _M_TILES), _pick(w, _N_TILES)
    ni, nj = t // tm, w // tn
    per_shard, total = ni * nj, N_CHIPS * ni * nj
    gather = _AllGather([placed], ["col"])

    def body(ord_ref, a_ref, w_own_ref, o_ref, w_ref, wbuf, fetch_sem, send, recv):
        s, i, j = pl.program_id(0), pl.program_id(1), pl.program_id(2)
        step = (s * ni + i) * nj + j
        e = _Env((w_own_ref,), (w_ref,), send, recv)

        def fetch(src, st):
            col = pl.multiple_of((ord_ref[st // per_shard] * nj + st % nj) * tn, LANE)
            return pltpu.make_async_copy(src.at[:, pl.ds(col, tn)], wbuf.at[st % 2], fetch_sem.at[st % 2])

        @pl.when(step == 0)
        def _():
            gather.start(e)
            fetch(w_own_ref, step).start()

        nxt = step + 1
        for kk, (cx, cy) in enumerate(e.chips):
            @pl.when(nxt == (kk + 1) * per_shard)
            def _():
                gather._ici(e, 0, kk, 2 * cx + cy).wait_recv()
                gather._d2d(e, 0, kk, e.c).start()
                gather._d2d(e, 0, kk, 1 - e.c).wait_recv()

        @pl.when(nxt < per_shard)
        def _():
            fetch(w_own_ref, nxt).start()

        @pl.when((nxt >= per_shard) & (nxt < total))
        def _():
            fetch(w_ref, nxt).start()

        fetch(w_ref, step).wait()
        o_ref[...] = jnp.dot(a_ref[...], wbuf[step % 2], preferred_element_type=F32)

        @pl.when(step == total - 1)
        def _():
            for kk in range(3):
                gather._ici(e, 0, kk, e.me).wait_send()
                gather._d2d(e, 0, kk, e.c).wait_send()

    z, full = pl.pallas_call(
        body, name=name, out_shape=(jax.ShapeDtypeStruct((t, n), F32), jax.ShapeDtypeStruct(placed.shape, placed.dtype)),
        grid_spec=pltpu.PrefetchScalarGridSpec(
            num_scalar_prefetch=1, grid=(N_CHIPS, ni, nj),
            in_specs=[pl.BlockSpec((tm, k), lambda s, i, j, ord_ref: (i, 0)), ANY],
            out_specs=(pl.BlockSpec((tm, tn), lambda s, i, j, ord_ref: (i, ord_ref[s] * nj + j)), ANY),
            scratch_shapes=[pltpu.VMEM((2, k, tn), placed.dtype), pltpu.SemaphoreType.DMA((2,)),
                            pltpu.SemaphoreType.DMA((gather.n_sems,)), pltpu.SemaphoreType.DMA((gather.n_sems,))]),
        input_output_aliases={2: 1}, compiler_params=_params(),
    )(order, a, placed)
    return z, full


def _add_own_half(g4, recv, idx, out_dtype, name):
    p, _, rh, n = g4.shape
    tr, tc = _tile2d(rh, n, 1024 * 1024)

    def body(idx_ref, g_ref, r_ref, o_ref):
        o_ref[...] = (g_ref[...] + r_ref[...]).astype(out_dtype)

    return pl.pallas_call(
        body, name=name, out_shape=jax.ShapeDtypeStruct((p, rh, n), out_dtype),
        grid_spec=pltpu.PrefetchScalarGridSpec(
            num_scalar_prefetch=1, grid=(p, rh // tr, n // tc),
            in_specs=[pl.BlockSpec((None, None, tr, tc), lambda q, i, j, idx_ref: (q, idx_ref[0], i, j)),
                      pl.BlockSpec((None, tr, tc), lambda q, i, j, idx_ref: (q, i, j))],
            out_specs=pl.BlockSpec((None, tr, tc), lambda q, i, j, idx_ref: (q, i, j))),
        compiler_params=_params(),
    )(idx, g4, recv)


def _sum_chips(own, kind, parts, idx, slots, to_all, name):
    _, rh, w = parts.shape
    tr, tc = _tile2d(rh, w, 512 * 1024)
    nrb, ncb = rh // tr, w // tc

    def body(idx_ref, own_ref, p0, p1, p2, o_ref):
        o_ref[...] = ((own_ref[...].astype(F32) + p0[...].astype(F32)) + p1[...].astype(F32)) + p2[...].astype(F32)

    if kind == "col":
        own_spec = pl.BlockSpec((None, tr, tc), lambda i, j, idx_ref: (0, i, idx_ref[1] * ncb + j))
    else:
        own_spec = pl.BlockSpec((None, tr, tc), lambda i, j, idx_ref: (idx_ref[1], i, j))
    if to_all:
        out_map = lambda i, j, idx_ref: ((2 * idx_ref[1] + idx_ref[0]) * nrb + i, j)
    else:
        out_map = lambda i, j, idx_ref: (idx_ref[0] * nrb + i, j)

    def part(k):
        return pl.BlockSpec((None, tr, tc), lambda i, j, idx_ref: (k, i, j))

    return pl.pallas_call(
        body, name=name, out_shape=jax.ShapeDtypeStruct((slots * rh, w), F32),
        grid_spec=pltpu.PrefetchScalarGridSpec(
            num_scalar_prefetch=1, grid=(nrb, ncb), in_specs=[own_spec, part(0), part(1), part(2)],
            out_specs=pl.BlockSpec((tr, tc), out_map)),
        compiler_params=_params(),
    )(idx, own, parts, parts, parts)


class _Reduce:
    def __init__(self, name, g, kind, idx, wire, to_all):
        r, c = g.shape
        self.name, self.kind, self.idx, self.wire, self.to_all = name, kind, idx, wire, to_all
        self.view = g.reshape(1, 2, r // 2, c) if kind == "col" else g.reshape(N_CHIPS, 2, r // (2 * N_CHIPS), c)

    def sibling(self):
        return _SiblingExchange([self.view])

    def got_sibling(self, outs):
        self.sum = _add_own_half(self.view, outs[0], self.idx, self.wire, "grad_chip_sum_" + self.name)

    def chips(self):
        return _ChipExchange([self.sum], [self.kind])

    def got_chips(self, outs):
        self.total = _sum_chips(self.sum, self.kind, outs[0], self.idx, 2 * N_CHIPS if self.to_all else 2,
                                self.to_all, "grad_total_" + self.name)


def _pack(arrays, rows):
    flat = jnp.concatenate([a.reshape(-1) for a in arrays])
    return jnp.pad(flat, (0, rows * SMALL_PACK_COLS - flat.shape[0])).reshape(rows, SMALL_PACK_COLS)


def _unpack(packed, shapes):
    flat = packed.reshape(-1)
    out, o = [], 0
    for shp in shapes:
        size = math.prod(shp)
        out.append(flat[o:o + size].reshape(shp))
        o += size
    return out


def _pack_rows(shapes):
    total = sum(math.prod(s) for s in shapes)
    unit = SMALL_PACK_COLS * N_CHIPS * 2 * SUBLANE
    return -(-total // unit) * (N_CHIPS * 2 * SUBLANE)


BIG = ("w_in", "w_attn_proj", "w_lru_proj", "w_out", "w_ffn_gate", "w_ffn_up", "w_ffn_down")
BIG_KIND = {"w_in": "col", "w_attn_proj": "row", "w_lru_proj": "row", "w_out": "row", "w_ffn_gate": "col",
            "w_ffn_up": "col", "w_ffn_down": "row"}
SMALL = ("norm1_g", "b_gates", "q_norm_g", "k_norm_g", "sinks", "conv_w", "conv_b", "w_rgate", "b_rgate",
         "w_igate", "b_igate", "lru_lambda", "norm2_g")
PACKED = tuple(n for n in SMALL if n not in ("w_rgate", "w_igate"))
WEIGHTS = ("norm1_g", "w_in", "b_gates", "q_norm_g", "k_norm_g", "sinks", "conv_w", "conv_b", "w_rgate", "b_rgate",
           "w_igate", "b_igate", "lru_lambda", "w_attn_proj", "w_lru_proj", "w_out", "norm2_g", "w_ffn_gate",
           "w_ffn_up", "w_ffn_down")


def kernel(x, positions, norm1_g, w_in, b_gates, q_norm_g, k_norm_g, sinks, conv_w, conv_b, w_rgate, b_rgate, w_igate, b_igate, lru_lambda, w_attn_proj, w_lru_proj, w_out, norm2_g, w_ffn_gate, w_ffn_up, w_ffn_down, loss_target, m_norm1_g, m_w_in, m_b_gates, m_q_norm_g, m_k_norm_g, m_sinks, m_conv_w, m_conv_b, m_w_rgate, m_b_rgate, m_w_igate, m_b_igate, m_lru_lambda, m_w_attn_proj, m_w_lru_proj, m_w_out, m_norm2_g, m_w_ffn_gate, m_w_ffn_up, m_w_ffn_down, v_norm1_g, v_w_in, v_b_gates, v_q_norm_g, v_k_norm_g, v_sinks, v_conv_w, v_conv_b, v_w_rgate, v_b_rgate, v_w_igate, v_b_igate, v_lru_lambda, v_w_attn_proj, v_w_lru_proj, v_w_out, v_norm2_g, v_w_ffn_gate, v_w_ffn_up, v_w_ffn_down):
    args = dict(locals())
    w = {n: args[n] for n in WEIGHTS}
    mom = {n: args["m_" + n] for n in WEIGHTS}
    var = {n: args["v_" + n] for n in WEIGHTS}

    t, d = x.shape[1], x.shape[2]
    hd = q_norm_g.shape[-1]
    nq = sinks.shape[-1]
    q_w = nq * hd
    d_rnn = conv_b.shape[-1]
    taps = conv_w.shape[1]
    n_blocks, bw = w_rgate.shape[1], w_rgate.shape[2]
    in_w = w_in.shape[-1] * N_CHIPS
    kv_w = (in_w - q_w - 2 * d_rnn - 2 * d) // 2
    kv = kv_w // hd
    grp = nq // kv
    u_off = q_w + 2 * kv_w
    gr_off = u_off + d_rnn
    ga_off = gr_off + d_rnn
    gw = bw * LANE // math.gcd(bw, LANE)
    chip = 2 * lax.axis_index("x") + lax.axis_index("y")
    idx = jnp.stack([lax.axis_index("c"), chip]).astype(jnp.int32)

    x2, tgt = x[0], loss_target[0]

    placed = {n: _cast_into_full(w[n][0], BIG_KIND[n], idx, "cast_" + n) for n in BIG}

    def gather(*names):
        return _AllGather([placed[n] for n in names], [BIG_KIND[n] for n in names])

    mx, my = lax.axis_index("x"), lax.axis_index("y")
    order = jnp.stack([chip, 2 * (1 - mx) + my, 2 * mx + (1 - my), 2 * (1 - mx) + (1 - my)]).astype(jnp.int32)
    conv_w_full = _gather_small(conv_w[0], "allgather_conv_w")
    conv_w_full = jnp.transpose(conv_w_full, (1, 0, 2)).reshape(taps, d_rnn)

    inv_freq = ROPE_THETA ** (-jnp.arange(0, hd // 4, 2, dtype=F32) / (hd // 4))
    ang = positions[0].astype(F32)[:, None] * inv_freq
    cos, sin = jnp.cos(ang), jnp.sin(ang)
    rest = hd - 2 * cos.shape[1]
    cos_t = jnp.concatenate([cos, cos, jnp.ones((t, rest), F32)], axis=1)
    sin_t = jnp.concatenate([-sin, sin, jnp.zeros((t, rest), F32)], axis=1)
    sinks1 = sinks[0]

    xn = _rms_fwd(x2, norm1_g, "rms1_fwd")
    z, win_f = _matmul_gathering(xn, placed["w_in"], order, "in_proj")
    attn, (wap_f, wlp_f, wout_f) = _attn_fwd(z, cos_t, sin_t, q_norm_g, k_norm_g, sinks1, kv, grp, hd, "attn_fwd",
                                             job=gather("w_attn_proj", "w_lru_proj", "w_out"))
    uc = _conv_fwd(z, u_off, d_rnn, conv_w_full, conv_b, "conv_fwd")
    rpre, ipre = _gates_fwd(uc, w_rgate[0], w_igate[0], gw, "gates_fwd")
    (hst, rec), (wg_f,) = _lru_fwd(uc, rpre, ipre, z, gr_off, b_rgate, b_igate, lru_lambda, "lru_fwd",
                                   job=gather("w_ffn_gate"))
    def slices(n, p0, p1, of, so_far):
        return _AllGather([so_far], [BIG_KIND[n]], (p0, p1, of))

    pa, (wu_f,) = _matmul(attn, wap_f, "nn", "attn_proj", job=slices("w_ffn_up", 0, 1, 4, placed["w_ffn_up"]))
    (plru, merged), (wu_f,) = _matmul(rec, wlp_f, "nn", "lru_proj", job=slices("w_ffn_up", 1, 3, 4, wu_f), fused=(
        [pa, (z, ga_off), (z, ga_off + d), (b_gates, 0), (b_gates, d)], _merge_after_lru_proj, (F32, BF16)))
    h1, (wu_f,) = _matmul(merged, wout_f, "nn", "out_proj", add=x2, job=slices("w_ffn_up", 3, 4, 4, wu_f))
    hn = _rms_fwd(h1, norm2_g, "rms2_fwd")
    gate, (wd_f,) = _matmul(hn, wg_f, "nn", "ffn_gate", job=slices("w_ffn_down", 0, 1, 2, placed["w_ffn_down"]))
    (up, act), (wd_f,) = _matmul(hn, wu_f, "nn", "ffn_up", job=slices("w_ffn_down", 1, 2, 2, wd_f),
                                 fused=([gate], _swiglu_after_up, (F32, BF16)))
    yout = _matmul(act, wd_f, "nn", "ffn_down", add=h1)
    dy, dy16, loss_part = _loss_head(yout, tgt, "loss_head")
    loss = lax.psum(loss_part[0, 0], ("x", "y", "c"))

    def reduction(n, g):
        return _Reduce(n, g, BIG_KIND[n], idx, BF16, False)

    r_wd = reduction("w_ffn_down", _matmul(act, dy16, "tn", "d_w_ffn_down"))
    (dgate, dup), got = _matmul(dy16, wd_f, "nt", "d_act", job=r_wd.sibling(),
                                fused=([gate, up], _swiglu_bwd_after_dact, (BF16, BF16)))
    r_wd.got_sibling(got)
    r_wg = reduction("w_ffn_gate", _matmul(hn, dgate, "tn", "d_w_ffn_gate"))
    g_wu, got = _matmul(hn, dup, "tn", "d_w_ffn_up", job=r_wg.sibling())
    r_wg.got_sibling(got)
    r_wu = reduction("w_ffn_up", g_wu)
    dhn, got = _matmul(dgate, wg_f, "nt", "d_hn_gate", job=r_wu.sibling())
    r_wu.got_sibling(got)
    dhn = _matmul(dup, wu_f, "nt", "d_hn_up", add=dhn)
    dh1, g_norm2, dh1_16 = _rms_bwd(dhn, h1, norm2_g, dy, "rms2_bwd", mxu_copy=True)
    r_wout = reduction("w_out", _matmul(merged, dh1_16, "tn", "d_w_out"))
    dmerged, got = _matmul(dh1_16, wout_f, "nt", "d_merged", job=r_wout.sibling())
    r_wout.got_sibling(got)
    (dpa, dz, g_ba), got = _merge_bwd(dmerged, z, b_gates, pa, ga_off, 0, None, "merge_bwd_attn", job=r_wout.chips())
    r_wout.got_chips(got)
    dpl, dz, g_bl = _merge_bwd(dmerged, z, b_gates, plru, ga_off + d, d, dz, "merge_bwd_lru")
    r_wap = reduction("w_attn_proj", _matmul(attn, dpa, "tn", "d_w_attn_proj"))
    dattn, got = _matmul(dpa, wap_f, "nt", "d_attn", job=r_wap.sibling())
    r_wap.got_sibling(got)
    g_wlp, got = _matmul(rec, dpl, "tn", "d_w_lru_proj", job=r_wap.chips())
    r_wap.got_chips(got)
    r_wlp = reduction("w_lru_proj", g_wlp)
    drec, got = _matmul(dpl, wlp_f, "nt", "d_rec", job=r_wlp.sibling())
    r_wlp.got_sibling(got)
    both = _Jobs(r_wlp.chips(), r_wg.chips())
    (dz, drp, dip, duc_direct, g_lam, g_br, g_bi), got = _lru_bwd(
        drec, hst, uc, rpre, ipre, z, gr_off, b_rgate, b_igate, lru_lambda, dz, "lru_bwd", job=both)
    got_wlp, got_wg = both.split(got)
    r_wlp.got_chips(got_wlp)
    r_wg.got_chips(got_wg)
    duc = _gates_bwd_x(duc_direct, drp, dip, w_rgate[0], w_igate[0], gw, "gates_bwd_x")
    g_wr = _gates_bwd_w(uc, drp, n_blocks, bw, gw, "gates_bwd_wr")
    g_wi = _gates_bwd_w(uc, dip, n_blocks, bw, gw, "gates_bwd_wi")
    dz, g_convw, g_convb = _conv_bwd(duc, z, u_off, conv_w_full, dz, "conv_bwd")
    (dq, dk, dv, g_qg, g_kg, g_sinks), got = _attn_bwd(dattn, z, cos_t, sin_t, q_norm_g, k_norm_g, sinks1, kv, grp, hd,
                                                        "attn_bwd", job=_Jobs(r_wu.chips(), r_wd.chips()))
    r_wu.got_chips(got[:1])
    r_wd.got_chips(got[1:])
    for part, col in ((dq, 0), (dk, q_w), (dv, q_w + kv_w)):
        dz = lax.dynamic_update_slice(dz, part, (0, col))
    r_wr = _Reduce("w_rgate", g_wr.reshape(n_blocks * bw, bw), "row", idx, F32, True)
    r_wi = _Reduce("w_igate", g_wi.reshape(n_blocks * bw, bw), "row", idx, F32, True)
    early = [r_wap, r_wlp, r_wout, r_wg, r_wu, r_wd]
    three = _Jobs(r_wr.sibling(), r_wi.sibling(), _FinishExchange([r.total for r in early], [False] * len(early)))
    g_top, got = _matmul(xn, dz, "tn", "d_w_in_top", m_window=(0, d // 2), job=three)
    got_wr, got_wi, finished = three.split(got)
    r_wr.got_sibling(got_wr)
    r_wi.got_sibling(got_wi)
    r_top = _Reduce("w_in_top", g_top, "col", idx, BF16, False)
    three = _Jobs(r_top.sibling(), r_wr.chips(), r_wi.chips())
    g_bot, got = _matmul(xn, dz, "tn", "d_w_in_bot", m_window=(d // 2, d // 2), job=three)
    got_top, got_wr, got_wi = three.split(got)
    r_top.got_sibling(got_top)
    r_wr.got_chips(got_wr)
    r_wi.got_chips(got_wi)
    r_bot = _Reduce("w_in_bot", g_bot, "col", idx, BF16, False)
    both = _Jobs(r_top.chips(), r_bot.sibling())
    dxn, got = _matmul(dz, win_f, "nt", "d_xn_a", m_window=(0, t // 2), into=(None, t), job=both)
    got_top, got_bot = both.split(got)
    r_top.got_chips(got_top)
    r_bot.got_sibling(got_bot)
    dxn, got = _matmul(dz, win_f, "nt", "d_xn_b", m_window=(t // 2, t // 2), into=(dxn, t), job=r_bot.chips())
    r_bot.got_chips(got)
    dx, g_norm1 = _rms_bwd(dxn, x2, norm1_g, dh1, "rms1_bwd")

    small_grads = {"norm1_g": g_norm1, "b_gates": jnp.concatenate([g_ba, g_bl], axis=1), "q_norm_g": g_qg,
                   "k_norm_g": g_kg, "sinks": g_sinks[:, :nq], "conv_w": g_convw, "conv_b": g_convb,
                   "b_rgate": g_br, "b_igate": g_bi, "lru_lambda": g_lam, "norm2_g": g_norm2}
    gshapes = [small_grads[n].shape for n in PACKED]
    small_sum, (top, bot, grads_wr, grads_wi) = _allreduce_small(
        _pack([small_grads[n] for n in PACKED], _pack_rows(gshapes)),
        _FinishExchange([r.total for r in (r_top, r_bot, r_wr, r_wi)], [False, False, True, True]), "grad_last_exchange")
    grads = dict(zip(BIG[1:], finished))
    grads["w_in"] = jnp.concatenate([top, bot], axis=0)
    grads["w_rgate"], grads["w_igate"] = grads_wr, grads_wi
    small_full = dict(zip(PACKED, _unpack(small_sum, gshapes)))
    per = d_rnn // N_CHIPS
    small_full["conv_w"] = lax.dynamic_slice(small_full["conv_w"], (0, chip * per), (taps, per))
    grads.update(small_full)

    delta, new_m, new_v = {}, {}, {}
    for n in BIG + ("w_rgate", "w_igate"):
        as2d = (lambda a: a[0]) if n in BIG else (lambda a: a.reshape(n_blocks * bw, bw))
        if n == "w_in":
            delta[n], new_m[n], new_v[n] = _adamw(as2d(w[n]), grads[n], as2d(mom[n]), as2d(var[n]), "adamw_" + n)
        else:
            delta[n], new_m[n], new_v[n], grads[n] = _adamw(as2d(w[n]), grads[n], as2d(mom[n]), as2d(var[n]),
                                                            "adamw_" + n, pass_grad=True)
    pshapes = [w[n].shape for n in PACKED]
    prows = _pack_rows(pshapes)
    pk = [_pack([src[n] for n in PACKED], prows) for src in (w, grads, mom, var)]
    for res, packed in zip((delta, new_m, new_v), _adamw(pk[0], pk[1], pk[2], pk[3], "adamw_small")):
        res.update(dict(zip(PACKED, _unpack(packed, pshapes))))

    outs = [loss, dx.reshape(x.shape)]
    for res in (grads, delta, new_m, new_v):
        outs += [res[n].reshape(w[n].shape) for n in WEIGHTS]
    return tuple(outs)


def _allreduce_small(x, job, name):
    n_dev = 2 * N_CHIPS
    rel = [(fx, fy, fc) for fx in (0, 1) for fy in (0, 1) for fc in (0, 1)][1:]
    j_in, j_out = len(job.inputs), len(job.out_shapes)

    def body(x_ref, *refs):
        jins, all_ref, o_ref, jouts = refs[:j_in], refs[j_in], refs[j_in + 1], refs[j_in + 2:j_in + 2 + j_out]
        send_sems, recv_sems, jsend, jrecv = refs[j_in + 2 + j_out:]
        e = _Env((x_ref,), (all_ref,), send_sems, recv_sems)
        carried = _Env(jins, jouts, jsend, jrecv, 0, (e.x, e.y, e.c))
        job.start(carried)
        mine = 2 * e.me + e.c

        def peer(r):
            fx, fy, fc = rel[r]
            return (1 - e.x if fx else e.x), (1 - e.y if fy else e.y), (1 - e.c if fc else e.c)

        all_ref[mine] = x_ref[...]
        for r in range(len(rel)):
            e.copy(x_ref, all_ref.at[mine], r, peer(r)).start()
        for r in range(len(rel)):
            px, py, pc = peer(r)
            e.copy(x_ref, all_ref.at[2 * (2 * px + py) + pc], r, peer(r)).wait_recv()
        total = all_ref[0]
        for dev in range(1, n_dev):
            total = total + all_ref[dev]
        o_ref[...] = total
        for r in range(len(rel)):
            e.copy(x_ref, all_ref.at[mine], r, peer(r)).wait_send()
        job.finish(carried)

    vm = pl.BlockSpec(memory_space=pltpu.VMEM)
    res = pl.pallas_call(
        body, name=name,
        out_shape=(jax.ShapeDtypeStruct((n_dev,) + x.shape, x.dtype), jax.ShapeDtypeStruct(x.shape, x.dtype))
        + tuple(job.out_shapes),
        in_specs=[vm] + [ANY] * j_in, out_specs=(vm, vm) + (ANY,) * j_out,
        input_output_aliases={1 + i: 2 + o for i, o in job.aliases.items()},
        scratch_shapes=[pltpu.SemaphoreType.DMA((len(rel),)), pltpu.SemaphoreType.DMA((len(rel),)),
                        pltpu.SemaphoreType.DMA((job.n_sems,)), pltpu.SemaphoreType.DMA((job.n_sems,))])(x, *job.inputs)
    return res[1], tuple(res[2:])


def _gather_small(shard, name):
    def body(s_ref, o_ref, send_sems, recv_sems):
        e = _Env((s_ref,), (o_ref,), send_sems, recv_sems)
        o_ref[e.me] = s_ref[...]
        for k, (cx, cy) in enumerate(e.chips):
            e.copy(s_ref, o_ref.at[e.me], k, (cx, cy, e.c)).start()
        for k, (cx, cy) in enumerate(e.chips):
            e.copy(s_ref, o_ref.at[2 * cx + cy], k, (cx, cy, e.c)).wait_recv()
        for k, (cx, cy) in enumerate(e.chips):
            e.copy(s_ref, o_ref.at[e.me], k, (cx, cy, e.c)).wait_send()

    vm = pl.BlockSpec(memory_space=pltpu.VMEM)
    return pl.pallas_call(body, name=name, out_shape=jax.ShapeDtypeStruct((N_CHIPS,) + shard.shape, shard.dtype),
                          in_specs=[vm], out_specs=vm,
                          scratch_shapes=[pltpu.SemaphoreType.DMA((3,)), pltpu.SemaphoreType.DMA((3,))])(shard)
```

```python
import functools
import math

import jax
import jax.numpy as jnp
from jax import lax
from jax.experimental import pallas as pl
from jax.experimental.pallas import tpu as pltpu

F32 = jnp.float32
BF16 = jnp.bfloat16
MESH = pl.DeviceIdType.MESH

WINDOW = 128
BLK = 128
ROPE_THETA = 500000.0
LRU_C = 8.0
EPS = 1e-6
NEG = -1e30
ADAM_LR = 0.001
ADAM_B1 = 0.9
ADAM_B2 = 0.999
ADAM_EPS = 1e-08
ADAM_WD = 0.01
ADAM_STEP = 10

VMEM_LIMIT_BYTES = 52 * 1024 * 1024
LANE = 128
SUBLANE = 8
N_CHIPS = 4
SMALL_PACK_COLS = 512


def _params(**kw):
    return pltpu.CompilerParams(vmem_limit_bytes=VMEM_LIMIT_BYTES, **kw)


def _pick(dim, cands):
    for c in cands:
        if dim % c == 0:
            return c
    return dim


def _sigmoid(x):
    return 0.5 * jnp.tanh(0.5 * x) + 0.5


ANY = pl.BlockSpec(memory_space=pl.ANY)


class _Env:
    def __init__(self, ins, outs, send, recv, sem0=0, place=None):
        self.ins, self.outs, self.send, self.recv, self.sem0 = ins, outs, send, recv, sem0
        self.x, self.y, self.c = place or (lax.axis_index("x"), lax.axis_index("y"), lax.axis_index("c"))
        self.me = 2 * self.x + self.y
        self.chips = [(1 - self.x, self.y), (self.x, 1 - self.y), (1 - self.x, 1 - self.y)]
        self.sibling = (self.x, self.y, 1 - self.c)

    def sub(self, i0, n_in, o0, n_out, sem0):
        return _Env(self.ins[i0:i0 + n_in], self.outs[o0:o0 + n_out], self.send, self.recv, self.sem0 + sem0,
                    (self.x, self.y, self.c))

    def copy(self, src, dst, sem, to):
        return pltpu.make_async_remote_copy(src_ref=src, dst_ref=dst, send_sem=self.send.at[self.sem0 + sem],
                                            recv_sem=self.recv.at[self.sem0 + sem], device_id=to, device_id_type=MESH)


class _Exchange:
    inputs, out_shapes, aliases, n_sems = (), (), {}, 0

    def start(self, e):
        raise NotImplementedError

    def finish(self, e):
        raise NotImplementedError


class _Jobs(_Exchange):
    def __init__(self, *jobs):
        self.jobs, self.inputs, self.out_shapes, self.aliases, self.n_sems, self.at = jobs, [], [], {}, 0, []
        for job in jobs:
            self.at.append((len(self.inputs), len(self.out_shapes), self.n_sems))
            self.aliases.update({len(self.inputs) + i: len(self.out_shapes) + o for i, o in job.aliases.items()})
            self.inputs += list(job.inputs)
            self.out_shapes += list(job.out_shapes)
            self.n_sems += job.n_sems

    def _each(self, e):
        for job, (i0, o0, s0) in zip(self.jobs, self.at):
            yield job, e.sub(i0, len(job.inputs), o0, len(job.out_shapes), s0)

    def split(self, outs):
        return [tuple(outs[o0:o0 + len(job.out_shapes)]) for job, (_, o0, _) in zip(self.jobs, self.at)]

    def start(self, e):
        for job, se in self._each(e):
            job.start(se)

    def finish(self, e):
        for job, se in self._each(e):
            job.finish(se)


def _call(body, name, out_shape, grid, in_specs, out_specs, args, scratch_shapes=(), job=None, aliases=None):
    aliases = dict(aliases or {})
    if job is None:
        return pl.pallas_call(body, name=name, out_shape=out_shape, grid=grid, in_specs=list(in_specs),
                              out_specs=out_specs, scratch_shapes=list(scratch_shapes), input_output_aliases=aliases,
                              compiler_params=_params())(*args), ()
    single = not isinstance(out_shape, (tuple, list))
    shapes = [out_shape] if single else list(out_shape)
    ospecs = [out_specs] if single else list(out_specs)
    n_in, n_out, n_scr = len(args), len(shapes), len(scratch_shapes)
    j_in, j_out = len(job.inputs), len(job.out_shapes)

    def hosted(*refs):
        ins, jins = refs[:n_in], refs[n_in:n_in + j_in]
        outs = refs[n_in + j_in:n_in + j_in + n_out]
        jouts = refs[n_in + j_in + n_out:n_in + j_in + n_out + j_out]
        rest = refs[n_in + j_in + n_out + j_out:]
        e = _Env(jins, jouts, rest[n_scr], rest[n_scr + 1])
        first = functools.reduce(jnp.logical_and, [pl.program_id(d) == 0 for d in range(len(grid))])
        last = functools.reduce(jnp.logical_and, [pl.program_id(d) == g - 1 for d, g in enumerate(grid)])

        @pl.when(first)
        def _():
            job.start(e)

        body(*ins, *outs, *rest[:n_scr])

        @pl.when(last)
        def _():
            job.finish(e)

    res = pl.pallas_call(
        hosted, name=name, out_shape=tuple(shapes + list(job.out_shapes)), grid=grid,
        in_specs=list(in_specs) + [ANY] * j_in, out_specs=tuple(ospecs + [ANY] * j_out),
        scratch_shapes=list(scratch_shapes) + [pltpu.SemaphoreType.DMA((job.n_sems,)),
                                               pltpu.SemaphoreType.DMA((job.n_sems,))],
        input_output_aliases={**aliases, **{n_in + i: n_out + o for i, o in job.aliases.items()}},
        compiler_params=_params())(*args, *job.inputs)
    return (res[0] if single else tuple(res[:n_out])), tuple(res[n_out:])


_M_TILES = (1024, 1408, 1280, 512, 256, 128)
_N_TILES = (1408, 1280, 1024, 640, 512, 256, 128)
MXU_FULL_ROWS = 1024
MATMUL_VMEM_BUDGET = 42 * 1024 * 1024
MXU_FLOPS_PER_HBM_BYTE = 500


def _matmul_tiles(m, n, k, sa, sb, so, has_add, tn_divides=0):
    best = None
    for tm in [c for c in _M_TILES if m % c == 0] or [m]:
        for tn in [c for c in _N_TILES if n % c == 0 and tn_divides % c == 0] or [n]:
            for nk in range(1, 17):
                tk = k // nk
                if k % nk or tk % LANE:
                    continue
                need = 2 * (tm * tk * sa + tk * tn * sb) + 2 * tm * tn * (so + (4 if has_add else 0))
                need += tm * tn * 4 if nk > 1 else 0
                fetched = tk * tn * sb + tm * tk * sa // (1 if nk > 1 else n // tn)
                if need > MATMUL_VMEM_BUDGET:
                    continue
                mxu_bound = fetched * MXU_FLOPS_PER_HBM_BYTE <= 2 * tm * tn * tk
                key = (mxu_bound, min(tm, MXU_FULL_ROWS), -nk, tn, tm)
                if best is None or key > best[0]:
                    best = (key, (tm, tn, tk))
    assert best is not None, (m, n, k)
    return best[1]


def _matmul(a, b, mode, name, add=None, out_dtype=F32, job=None, m_window=None, into=None, fused=None):
    if mode == "nn":
        (m, k), (k2, n) = a.shape, b.shape
    elif mode == "nt":
        (m, k), (n, k2) = a.shape, b.shape
    else:
        (k, m), (k2, n) = a.shape, b.shape
    assert k == k2, (a.shape, b.shape, mode)
    m0, m = m_window or (0, m)
    tiles, fuse_fn, out_dtypes = fused or ((), None, (out_dtype,))
    tiles = [x if isinstance(x, tuple) else (x, 0) for x in tiles]
    tm, tn, tk = _matmul_tiles(math.gcd(m, m0) if m0 else m, n, k, a.dtype.itemsize, b.dtype.itemsize,
                               sum(jnp.dtype(dt).itemsize for dt in out_dtypes)
                               + sum(x.dtype.itemsize for x, _ in tiles if x.shape[0] > 1),
                               add is not None, math.gcd(*[c0 for _, c0 in tiles], 0))
    nk, mb0 = k // tk, m0 // tm
    if mode == "nn":
        a_spec = pl.BlockSpec((tm, tk), lambda i, j, kk: (mb0 + i, kk))
        b_spec = pl.BlockSpec((tk, tn), lambda i, j, kk: (kk, j))
        dims = (((1,), (0,)), ((), ()))
    elif mode == "nt":
        a_spec = pl.BlockSpec((tm, tk), lambda i, j, kk: (mb0 + i, kk))
        b_spec = pl.BlockSpec((tn, tk), lambda i, j, kk: (j, kk))
        dims = (((1,), (1,)), ((), ()))
    else:
        a_spec = pl.BlockSpec((tk, tm), lambda i, j, kk: (kk, mb0 + i))
        b_spec = pl.BlockSpec((tk, tn), lambda i, j, kk: (kk, j))
        dims = (((0,), (0,)), ((), ()))
    out_rows, ob0 = (into[1], mb0) if into is not None else (m, 0)
    o_spec = pl.BlockSpec((tm, tn), lambda i, j, kk: (ob0 + i, j))
    has_add = add is not None
    begun = into is not None and into[0] is not None

    n_side, n_out = has_add + len(tiles), len(out_dtypes)

    def body(*refs):
        a_ref, b_ref = refs[:2]
        side = refs[2:2 + n_side]
        o_refs = refs[len(refs) - n_out - (nk > 1):len(refs) - (nk > 1)]
        part = lax.dot_general(a_ref[...].astype(BF16), b_ref[...].astype(BF16), dims, preferred_element_type=F32)

        def finish(r):
            if has_add:
                r = r + side[0][...]
            vals = fuse_fn(r, *[x[...] for x in side[has_add:]]) if fuse_fn else (r,)
            for o_ref, val, dt in zip(o_refs, vals, out_dtypes):
                o_ref[...] = val.astype(dt)

        if nk == 1:
            finish(part)
            return
        acc = refs[-1]
        kk = pl.program_id(2)

        @pl.when(kk == 0)
        def _():
            acc[...] = part

        @pl.when(kk > 0)
        def _():
            acc[...] += part

        @pl.when(kk == nk - 1)
        def _():
            finish(acc[...])

    def side_spec(x, c0):
        if x.shape[0] == 1:
            return pl.BlockSpec((1, tn), lambda i, j, kk: (0, c0 // tn + j))
        return pl.BlockSpec((tm, tn), lambda i, j, kk: (mb0 + i, c0 // tn + j))

    in_specs = [a_spec, b_spec] + ([side_spec(add, 0)] if has_add else []) + [side_spec(x, c0) for x, c0 in tiles]
    args = (a, b) + ((add,) if has_add else ()) + tuple(x for x, _ in tiles)
    aliases = None
    if begun:
        aliases = {len(args): 0}
        in_specs, args = in_specs + [ANY], args + (into[0],)
    shapes = tuple(jax.ShapeDtypeStruct((out_rows, n), dt) for dt in out_dtypes)
    res, extra = _call(body, name, shapes if fused else shapes[0], (m // tm, n // tn, nk), in_specs,
                       (o_spec,) * n_out if fused else o_spec, args, [pltpu.VMEM((tm, tn), F32)] if nk > 1 else [],
                       job, aliases)
    return res if job is None else (res, extra)


def _row_tile(rows, cols, budget_elems=512 * 1024):
    cands = [c for c in (1024, 704, 512, 352, 256, 128, 64, 32, 16) if c * cols <= budget_elems]
    return _pick(rows, cands or (16,))


_EW_COLS = (1280, 1408, 1024, 640, 512, 256, 128)


def _tile2d(rows, cols, max_elems):
    tc = _pick(cols, _EW_COLS)
    return _row_tile(rows, tc, max_elems), tc


def _rms_fwd(x, g, name):
    t, d = x.shape
    tr = _row_tile(t, d)

    def body(x_ref, g_ref, o_ref):
        xv = x_ref[...]
        rstd = lax.rsqrt(jnp.mean(xv * xv, axis=-1, keepdims=True) + EPS)
        o_ref[...] = (xv * rstd * g_ref[...]).astype(BF16)

    spec = pl.BlockSpec((tr, d), lambda i: (i, 0))
    return pl.pallas_call(body, name=name, out_shape=jax.ShapeDtypeStruct((t, d), BF16), grid=(t // tr,),
                          in_specs=[spec, pl.BlockSpec((1, d), lambda i: (0, 0))], out_specs=spec,
                          compiler_params=_params())(x, g)


def _rms_bwd(dxn, x, g, resid, name, job=None, mxu_copy=False):
    t, d = x.shape
    tr = _row_tile(t, d, 512 * 1024)

    def body(dxn_ref, x_ref, g_ref, r_ref, dx_ref, dg_ref, *dx16_ref):
        @pl.when(pl.program_id(0) == 0)
        def _():
            dg_ref[...] = jnp.zeros_like(dg_ref)

        xv = x_ref[...]
        rstd = lax.rsqrt(jnp.mean(xv * xv, axis=-1, keepdims=True) + EPS)
        xhat = xv * rstd
        dy = dxn_ref[...]
        dg_ref[...] += jnp.sum(dy * xhat, axis=0, keepdims=True)
        dxhat = dy * g_ref[...]
        dx = r_ref[...] + rstd * (dxhat - xhat * jnp.mean(dxhat * xhat, axis=-1, keepdims=True))
        dx_ref[...] = dx
        if mxu_copy:
            dx16_ref[0][...] = dx.astype(BF16)

    spec = pl.BlockSpec((tr, d), lambda i: (i, 0))
    vec = pl.BlockSpec((1, d), lambda i: (0, 0))
    shapes = (jax.ShapeDtypeStruct((t, d), F32), jax.ShapeDtypeStruct((1, d), F32))
    shapes += (jax.ShapeDtypeStruct((t, d), BF16),) if mxu_copy else ()
    res, extra = _call(body, name, shapes, (t // tr,), [spec, spec, vec, spec],
                       (spec, vec) + ((spec,) if mxu_copy else ()), (dxn, x, g, resid), (), job)
    return res if job is None else (res, extra)


def _swiglu_after_up(up, gate):
    return up, gate * _sigmoid(gate) * up


def _swiglu_bwd_after_dact(dact, gate, up):
    sg = _sigmoid(gate)
    return dact * up * (sg * (1.0 + gate * (1.0 - sg))), dact * (gate * sg)


def _merge_after_lru_proj(plru, pa, ga, gl, ba, bl):
    return plru, _sigmoid(ga + ba) * pa + _sigmoid(gl + bl) * plru


def _merge_bwd(dmerged, z, b_gates, p, z0, b0, dz, name, job=None):
    t, d = p.shape
    cw = _pick(math.gcd(z0, d), (512, 256, 128))
    tr = _row_tile(t, cw, 256 * 1024)
    oz, ob, nd = z0 // cw, b0 // cw, d // cw

    def body(dm_ref, g_ref, b_ref, p_ref, *rest):
        dp_ref, dg_ref, sum_ref = rest[-3:]

        @pl.when(pl.program_id(1) == 0)
        def _():
            sum_ref[...] = jnp.zeros_like(sum_ref)

        dm = dm_ref[...]
        sg = _sigmoid(g_ref[...] + b_ref[...])
        dp_ref[...] = (dm * sg).astype(BF16)
        dg = dm * p_ref[...] * (sg * (1.0 - sg))
        dg_ref[...] = dg.astype(BF16)
        sum_ref[...] += jnp.sum(dg, axis=0, keepdims=True)

    blk = pl.BlockSpec((tr, cw), lambda j, i: (i, j))
    at_z = pl.BlockSpec((tr, cw), lambda j, i: (i, oz + j))
    in_specs = [blk, at_z, pl.BlockSpec((1, cw), lambda j, i: (0, ob + j)), blk]
    args, aliases = (dmerged, z, b_gates, p), None
    if dz is not None:
        in_specs, args, aliases = in_specs + [ANY], args + (dz,), {4: 1}
    res, extra = _call(
        body, name, (jax.ShapeDtypeStruct((t, d), BF16), jax.ShapeDtypeStruct(z.shape, BF16),
                     jax.ShapeDtypeStruct((1, d), F32)), (nd, t // tr), in_specs,
        (blk, at_z, pl.BlockSpec((1, cw), lambda j, i: (0, j))), args, (), job, aliases)
    return res if job is None else (res, extra)


def _loss_head(y, target, name):
    t, d = y.shape
    tr = _row_tile(t, d, 512 * 1024)
    nt = t // tr

    def body(y_ref, t_ref, dy_ref, dy16_ref, loss_ref, acc):
        i = pl.program_id(0)

        @pl.when(i == 0)
        def _():
            acc[...] = jnp.zeros_like(acc)

        e = y_ref[...] - t_ref[...]
        dy = e * (1.0 / d)
        dy_ref[...] = dy
        dy16_ref[...] = dy.astype(BF16)
        acc[...] += jnp.sum(e * e, axis=0, keepdims=True)

        @pl.when(i == nt - 1)
        def _():
            loss_ref[...] = (0.5 / d) * jnp.sum(acc[...], axis=-1, keepdims=True)

    spec = pl.BlockSpec((tr, d), lambda i: (i, 0))
    return pl.pallas_call(
        body, name=name, out_shape=(jax.ShapeDtypeStruct((t, d), F32), jax.ShapeDtypeStruct((t, d), BF16),
                                    jax.ShapeDtypeStruct((1, 1), F32)),
        grid=(nt,), in_specs=[spec, spec], out_specs=(spec, spec, pl.BlockSpec((1, 1), lambda i: (0, 0))),
        scratch_shapes=[pltpu.VMEM((1, d), F32)], compiler_params=_params(),
    )(y, target)


def _adamw(w, g, m, v, name, pass_grad=False):
    r, c = w.shape
    tr, tc = _tile2d(r, c, 512 * 1024)
    c1 = 1.0 - ADAM_B1 ** ADAM_STEP
    c2 = 1.0 - ADAM_B2 ** ADAM_STEP

    def body(w_ref, g_ref, m_ref, v_ref, d_ref, nm_ref, nv_ref, *g_out):
        gv = g_ref[...]
        if pass_grad:
            g_out[0][...] = gv
        mn = ADAM_B1 * m_ref[...] + (1.0 - ADAM_B1) * gv
        vn = ADAM_B2 * v_ref[...] + (1.0 - ADAM_B2) * (gv * gv)
        d_ref[...] = -ADAM_LR * ((mn / c1) / (jnp.sqrt(vn / c2) + ADAM_EPS) + ADAM_WD * w_ref[...])
        nm_ref[...] = mn
        nv_ref[...] = vn

    spec = pl.BlockSpec((tr, tc), lambda i, j: (i, j))
    shp = jax.ShapeDtypeStruct((r, c), F32)
    n_out = 4 if pass_grad else 3
    return pl.pallas_call(body, name=name, out_shape=(shp,) * n_out, grid=(r // tr, c // tc), in_specs=[spec] * 4,
                          out_specs=(spec,) * n_out, compiler_params=_params())(w, g, m, v)


def _swap_halves(v, half):
    n = v.shape[-1]
    lane = lax.broadcasted_iota(jnp.int32, v.shape, 1)
    return jnp.where(lane < half, pltpu.roll(v, n - half, 1),
                     jnp.where(lane < 2 * half, pltpu.roll(v, half, 1), 0.0))


def _rope(y, c, s, half):
    return y * c + _swap_halves(y, half) * s


def _rope_bwd(dout, c, s, half):
    return dout * c + _swap_halves(dout * s, half)


def _stack_heads(ref, grp, hd):
    return jnp.concatenate([ref[:, g * hd:(g + 1) * hd] for g in range(grp)], axis=0)


def _softmax_with_sinks(s, sink_ref, first, grp, i):
    rows = s.shape[0]
    qi = lax.broadcasted_iota(jnp.int32, s.shape, 0) & (BLK - 1)
    kj = lax.broadcasted_iota(jnp.int32, s.shape, 1)
    rel = qi + BLK - kj
    s = jnp.where((rel >= 0) & (rel < WINDOW) & ((kj >= BLK) | (i > 0)), s, NEG)
    head = lax.broadcasted_iota(jnp.int32, (rows, 1), 0) // BLK
    sk = jnp.zeros((rows, 1), F32)
    for g in range(grp):
        sk = jnp.where(head == g, sink_ref[first + g], sk)
    mx = jnp.maximum(jnp.max(s, axis=-1, keepdims=True), sk)
    p = jnp.exp(s - mx)
    esk = jnp.exp(sk - mx)
    inv_den = 1.0 / (jnp.sum(p, axis=-1, keepdims=True) + esk)
    return p * inv_den, esk * inv_den, head


def _norm_fwd(xraw, g):
    rstd = lax.rsqrt(jnp.mean(xraw * xraw, axis=-1, keepdims=True) + EPS)
    xhat = xraw * rstd
    return xhat, rstd, xhat * g


def _norm_bwd(dy, xhat, rstd, g):
    dxhat = dy * g
    dx = rstd * (dxhat - xhat * jnp.mean(dxhat * xhat, axis=-1, keepdims=True))
    return dx, jnp.sum(dy * xhat, axis=0, keepdims=True)


def _attn_specs(nb, grp, hd, kv, clamp):
    qo, ko, vo = 0, (kv * grp), (kv * grp + kv)
    cur = (lambda i: jnp.minimum(i, nb - 1)) if clamp else (lambda i: i)
    prev = lambda i: jnp.maximum(cur(i) - 1, 0)
    zq = pl.BlockSpec((BLK, grp * hd), lambda h, i: (cur(i), h))
    kc = pl.BlockSpec((BLK, hd), lambda h, i: (cur(i), ko + h))
    kp = pl.BlockSpec((BLK, hd), lambda h, i: (prev(i), ko + h))
    vc = pl.BlockSpec((BLK, hd), lambda h, i: (cur(i), vo + h))
    vp = pl.BlockSpec((BLK, hd), lambda h, i: (prev(i), vo + h))
    tc = pl.BlockSpec((BLK, hd), lambda h, i: (cur(i), 0))
    tp = pl.BlockSpec((BLK, hd), lambda h, i: (prev(i), 0))
    gs = pl.BlockSpec((1, hd), lambda h, i: (0, 0))
    return zq, kc, kp, vc, vp, tc, tp, gs


def _attn_fwd(z, cos_t, sin_t, qg, kg, sinks, kv, grp, hd, name, job=None):
    t = z.shape[0]
    nb = t // BLK
    half = hd // 8
    scale = 1.0 / math.sqrt(hd)
    zq, kc, kp, vc, vp, tc, tp, gs = _attn_specs(nb, grp, hd, kv, False)

    def body(sink_ref, zq_ref, kc_ref, kp_ref, vc_ref, vp_ref, cc_ref, sc_ref, cp_ref, sp_ref, qg_ref, kg_ref, o_ref):
        h, i = pl.program_id(0), pl.program_id(1)

        def normrope(xraw, g, c, s):
            return _rope(_norm_fwd(xraw, g)[2], c, s, half)

        cc, sc = cc_ref[...], sc_ref[...]
        kcur = normrope(kc_ref[...], kg_ref[...], cc, sc)
        kprev = normrope(kp_ref[...], kg_ref[...], cp_ref[...], sp_ref[...])
        kk = jnp.concatenate([kprev, kcur], axis=0).astype(BF16)
        vv = jnp.concatenate([vp_ref[...], vc_ref[...]], axis=0).astype(BF16)
        for g in range(grp):
            q = normrope(zq_ref[:, g * hd:(g + 1) * hd], qg_ref[...], cc, sc).astype(BF16)
            s = lax.dot_general(q, kk, (((1,), (1,)), ((), ())), preferred_element_type=F32) * scale
            p, _, _ = _softmax_with_sinks(s, sink_ref, h * grp + g, 1, i)
            o_ref[:, g * hd:(g + 1) * hd] = jnp.dot(p.astype(BF16), vv, preferred_element_type=F32).astype(BF16)

    res, extra = _call(
        body, name, jax.ShapeDtypeStruct((t, kv * grp * hd), BF16), (kv, nb),
        [pl.BlockSpec(memory_space=pltpu.SMEM), zq, kc, kp, vc, vp, tc, tc, tp, tp, gs, gs],
        pl.BlockSpec((BLK, grp * hd), lambda h, i: (i, h)),
        (sinks, z, z, z, z, z, cos_t, sin_t, cos_t, sin_t, qg, kg), (), job)
    return res if job is None else (res, extra)


def _attn_bwd(dattn, z, cos_t, sin_t, qg, kg, sinks, kv, grp, hd, name, job=None):
    t = z.shape[0]
    nb = t // BLK
    half = hd // 8
    scale = 1.0 / math.sqrt(hd)
    zq, kc, kp, vc, vp, tc, tp, gs = _attn_specs(nb, grp, hd, kv, True)

    def body(sink_ref, zq_ref, kc_ref, kp_ref, vc_ref, vp_ref, cc_ref, sc_ref, cp_ref, sp_ref, qg_ref, kg_ref, do_ref,
             dq_ref, dk_ref, dv_ref, dqg_ref, dkg_ref, dsk_ref, dk_carry, dv_carry):
        h, i = pl.program_id(0), pl.program_id(1)
        lane1 = lax.broadcasted_iota(jnp.int32, (1, LANE), 1)

        @pl.when((h == 0) & (i == 0))
        def _():
            dqg_ref[...] = jnp.zeros_like(dqg_ref)
            dkg_ref[...] = jnp.zeros_like(dkg_ref)
            dsk_ref[...] = jnp.zeros_like(dsk_ref)

        @pl.when(i == 0)
        def _():
            dk_carry[...] = jnp.zeros_like(dk_carry)
            dv_carry[...] = jnp.zeros_like(dv_carry)

        @pl.when(i < nb)
        def _():
            cc, sc, cp, sp = cc_ref[...], sc_ref[...], cp_ref[...], sp_ref[...]
            qgv, kgv = qg_ref[...], kg_ref[...]
            xh_kc, rs_kc, y_kc = _norm_fwd(kc_ref[...], kgv)
            xh_kp, rs_kp, y_kp = _norm_fwd(kp_ref[...], kgv)
            kk = jnp.concatenate([_rope(y_kp, cp, sp, half), _rope(y_kc, cc, sc, half)], axis=0).astype(BF16)
            vv = jnp.concatenate([vp_ref[...], vc_ref[...]], axis=0).astype(BF16)
            cq, sq = jnp.concatenate([cc] * grp, axis=0), jnp.concatenate([sc] * grp, axis=0)
            xh_q, rs_q, y_q = _norm_fwd(_stack_heads(zq_ref, grp, hd), qgv)
            q = _rope(y_q, cq, sq, half).astype(BF16)
            s = lax.dot_general(q, kk, (((1,), (1,)), ((), ())), preferred_element_type=F32) * scale
            p, psink, head = _softmax_with_sinks(s, sink_ref, h * grp, grp, i)
            dog = _stack_heads(do_ref, grp, hd).astype(BF16)
            dp = lax.dot_general(dog, vv, (((1,), (1,)), ((), ())), preferred_element_type=F32)
            rsum = jnp.sum(p * dp, axis=-1, keepdims=True)
            ds = (p * (dp - rsum) * scale).astype(BF16)
            dsink = -psink * rsum
            dsk = jnp.zeros((1, LANE), F32)
            for g in range(grp):
                dsk = dsk + jnp.where(lane1 == h * grp + g,
                                      jnp.sum(jnp.where(head == g, dsink, 0.0), axis=0, keepdims=True), 0.0)
            dqn = jnp.dot(ds, kk, preferred_element_type=F32)
            dkk = lax.dot_general(ds, q, (((0,), (0,)), ((), ())), preferred_element_type=F32)
            dvv = lax.dot_general(p.astype(BF16), dog, (((0,), (0,)), ((), ())), preferred_element_type=F32)
            dxq, dqg = _norm_bwd(_rope_bwd(dqn, cq, sq, half), xh_q, rs_q, qgv)
            dxq = dxq.astype(BF16)
            for g in range(grp):
                dq_ref[:, g * hd:(g + 1) * hd] = dxq[g * BLK:(g + 1) * BLK]
            dkp_raw, dg_kp = _norm_bwd(_rope_bwd(dkk[:BLK], cp, sp, half), xh_kp, rs_kp, kgv)
            dkc_raw, dg_kc = _norm_bwd(_rope_bwd(dkk[BLK:], cc, sc, half), xh_kc, rs_kc, kgv)
            dk_ref[...] = (dk_carry[...] + dkp_raw).astype(BF16)
            dv_ref[...] = (dv_carry[...] + dvv[:BLK]).astype(BF16)
            dk_carry[...] = dkc_raw
            dv_carry[...] = dvv[BLK:]
            dqg_ref[...] += dqg
            dkg_ref[...] += dg_kp + dg_kc
            dsk_ref[...] += dsk

        @pl.when(i == nb)
        def _():
            dk_ref[...] = dk_carry[...].astype(BF16)
            dv_ref[...] = dv_carry[...].astype(BF16)

    kvw = kv * hd
    vec = pl.BlockSpec((1, hd), lambda h, i: (0, 0))
    shifted = pl.BlockSpec((BLK, hd), lambda h, i: (jnp.maximum(i - 1, 0), h))
    res, extra = _call(
        body, name,
        (jax.ShapeDtypeStruct((t, kv * grp * hd), BF16), jax.ShapeDtypeStruct((t, kvw), BF16),
         jax.ShapeDtypeStruct((t, kvw), BF16), jax.ShapeDtypeStruct((1, hd), F32),
         jax.ShapeDtypeStruct((1, hd), F32), jax.ShapeDtypeStruct((1, LANE), F32)),
        (kv, nb + 1),
        [pl.BlockSpec(memory_space=pltpu.SMEM), zq, kc, kp, vc, vp, tc, tc, tp, tp, gs, gs,
         pl.BlockSpec((BLK, grp * hd), lambda h, i: (jnp.minimum(i, nb - 1), h))],
        (pl.BlockSpec((BLK, grp * hd), lambda h, i: (jnp.minimum(i, nb - 1), h)), shifted, shifted, vec, vec,
         pl.BlockSpec((1, LANE), lambda h, i: (0, 0))),
        (sinks, z, z, z, z, z, cos_t, sin_t, cos_t, sin_t, qg, kg, dattn),
        [pltpu.VMEM((BLK, hd), F32), pltpu.VMEM((BLK, hd), F32)], job)
    return res if job is None else (res, extra)


def _window(rows, cb, c0, row_of, col_of):
    assert rows % SUBLANE == 0 and cb % LANE == 0 and c0 % LANE == 0, (rows, cb, c0)
    return pl.BlockSpec((pl.Element(rows), pl.Element(cb)),
                        lambda *g: (pl.multiple_of(row_of(*g) * rows, SUBLANE), pl.multiple_of(c0 + col_of(*g) * cb, LANE)))


def _conv_fwd(z, c0, c, w, b, name):
    t = z.shape[0]
    taps = w.shape[0]
    cb = _pick(c, (1408, 1024, 512, 256, 128))
    tr = _row_tile(t, cb, 512 * 1024)
    hb = tr // SUBLANE

    def body(u_ref, halo_ref, w_ref, b_ref, o_ref):
        i = pl.program_id(0)
        x = u_ref[...]
        acc = b_ref[...] + w_ref[taps - 1:taps, :] * x
        for k in range(taps - 1):
            acc = acc + w_ref[k:k + 1, :] * pltpu.roll(x, taps - 1 - k, 0)
        o_ref[...] = acc
        row = lax.broadcasted_iota(jnp.int32, (SUBLANE, cb), 0)
        hp = jnp.where(i > 0, halo_ref[...], 0.0)
        x8 = u_ref[0:SUBLANE, :]
        acc8 = b_ref[...] + w_ref[taps - 1:taps, :] * x8
        for k in range(taps - 1):
            s = taps - 1 - k
            acc8 = acc8 + w_ref[k:k + 1, :] * jnp.where(row < s, pltpu.roll(hp, s, 0), pltpu.roll(x8, s, 0))
        o_ref[0:SUBLANE, :] = acc8

    blk = pl.BlockSpec((tr, cb), lambda i, j: (i, j))
    return pl.pallas_call(
        body, name=name, out_shape=jax.ShapeDtypeStruct((t, c), F32), grid=(t // tr, c // cb),
        in_specs=[_window(tr, cb, c0, lambda i, j: i, lambda i, j: j),
                  _window(SUBLANE, cb, c0, lambda i, j: jnp.maximum(i * hb - 1, 0), lambda i, j: j),
                  pl.BlockSpec((taps, cb), lambda i, j: (0, j)), pl.BlockSpec((1, cb), lambda i, j: (0, j))],
        out_specs=blk, compiler_params=_params(),
    )(z, z, w, b)


def _conv_bwd(duc, z, c0, w, dz, name):
    t, c = duc.shape
    taps = w.shape[0]
    cb = _pick(c, (1408, 1024, 512, 256, 128))
    tr = _row_tile(t, cb, 512 * 1024)
    hb, nt = tr // SUBLANE, t // tr

    def body(g_ref, gnext_ref, u_ref, uprev_ref, w_ref, dz_ref, du16_ref, dw_ref, db_ref, du_ref):
        i = pl.program_id(1)

        @pl.when(i == 0)
        def _():
            dw_ref[...] = jnp.zeros_like(dw_ref)
            db_ref[...] = jnp.zeros_like(db_ref)

        row = lax.broadcasted_iota(jnp.int32, (SUBLANE, cb), 0)
        g, x = g_ref[...], u_ref[...]
        du = w_ref[taps - 1:taps, :] * g
        for k in range(taps - 1):
            du = du + w_ref[k:k + 1, :] * pltpu.roll(g, tr - (taps - 1 - k), 0)
        du_ref[...] = du
        hn = jnp.where(i < nt - 1, gnext_ref[...], 0.0)
        g8 = g_ref[tr - SUBLANE:tr, :]
        du8 = w_ref[taps - 1:taps, :] * g8
        for k in range(taps - 1):
            s = taps - 1 - k
            du8 = du8 + w_ref[k:k + 1, :] * jnp.where(row >= SUBLANE - s, pltpu.roll(hn, SUBLANE - s, 0),
                                                     pltpu.roll(g8, SUBLANE - s, 0))
        du_ref[tr - SUBLANE:tr, :] = du8
        du16_ref[...] = du_ref[...].astype(BF16)

        hp = jnp.where(i > 0, uprev_ref[...], 0.0)
        xl8, gf8 = u_ref[tr - SUBLANE:tr, :], g_ref[0:SUBLANE, :]
        db_ref[...] += jnp.sum(g, axis=0, keepdims=True)
        dw_ref[taps - 1:taps, :] += jnp.sum(g * x, axis=0, keepdims=True)
        for k in range(taps - 1):
            s = taps - 1 - k
            fix = jnp.where(row < s, pltpu.roll(hp, s, 0) - pltpu.roll(xl8, s, 0), 0.0)
            dw_ref[k:k + 1, :] += (jnp.sum(g * pltpu.roll(x, s, 0), axis=0, keepdims=True)
                                   + jnp.sum(gf8 * fix, axis=0, keepdims=True))

    blk = pl.BlockSpec((tr, cb), lambda j, i: (i, j))
    nh = t // SUBLANE
    return pl.pallas_call(
        body, name=name,
        out_shape=(jax.ShapeDtypeStruct(dz.shape, BF16), jax.ShapeDtypeStruct((taps, c), F32),
                   jax.ShapeDtypeStruct((1, c), F32)),
        grid=(c // cb, nt),
        in_specs=[blk, pl.BlockSpec((SUBLANE, cb), lambda j, i: (jnp.minimum((i + 1) * hb, nh - 1), j)),
                  _window(tr, cb, c0, lambda j, i: i, lambda j, i: j),
                  _window(SUBLANE, cb, c0, lambda j, i: jnp.maximum(i * hb - 1, 0), lambda j, i: j),
                  pl.BlockSpec((taps, cb), lambda j, i: (0, j)), ANY],
        out_specs=(_window(tr, cb, c0, lambda j, i: i, lambda j, i: j), pl.BlockSpec((taps, cb), lambda j, i: (0, j)),
                   pl.BlockSpec((1, cb), lambda j, i: (0, j))),
        scratch_shapes=[pltpu.VMEM((tr, cb), F32)], input_output_aliases={5: 0}, compiler_params=_params(),
    )(duc, duc, z, z, w, dz)


def _gates_fwd(uc, wr, wi, gw, name):
    t, c = uc.shape
    n, bw, _ = wr.shape
    per, ng = gw // bw, c // gw
    tr = _pick(t, (512, 256, 128))

    def body(u_ref, wr_ref, wi_ref, r_ref, i_ref):
        for b in range(per):
            cols = slice(b * bw, (b + 1) * bw)
            a = u_ref[:, cols].astype(BF16)
            r_ref[:, cols] = jnp.dot(a, wr_ref[b].astype(BF16), preferred_element_type=F32)
            i_ref[:, cols] = jnp.dot(a, wi_ref[b].astype(BF16), preferred_element_type=F32)

    blk = pl.BlockSpec((tr, gw), lambda h, i: (i, h))
    wsp = pl.BlockSpec((per, bw, bw), lambda h, i: (h, 0, 0))
    shp = jax.ShapeDtypeStruct((t, c), F32)
    return pl.pallas_call(body, name=name, out_shape=(shp, shp), grid=(ng, t // tr), in_specs=[blk, wsp, wsp],
                          out_specs=(blk, blk), compiler_params=_params())(uc, wr, wi)


def _gates_bwd_x(duc, drp, dip, wr, wi, gw, name):
    t, c = duc.shape
    n, bw, _ = wr.shape
    per, ng = gw // bw, c // gw
    tr = _pick(t, (512, 256, 128))
    dims = (((1,), (1,)), ((), ()))

    def body(d_ref, r_ref, i_ref, wr_ref, wi_ref, o_ref):
        for b in range(per):
            cols = slice(b * bw, (b + 1) * bw)
            o_ref[:, cols] = (
                d_ref[:, cols]
                + lax.dot_general(r_ref[:, cols].astype(BF16), wr_ref[b].astype(BF16), dims, preferred_element_type=F32)
                + lax.dot_general(i_ref[:, cols].astype(BF16), wi_ref[b].astype(BF16), dims, preferred_element_type=F32))

    blk = pl.BlockSpec((tr, gw), lambda h, i: (i, h))
    wsp = pl.BlockSpec((per, bw, bw), lambda h, i: (h, 0, 0))
    return pl.pallas_call(body, name=name, out_shape=jax.ShapeDtypeStruct((t, c), F32), grid=(ng, t // tr),
                          in_specs=[blk, blk, blk, wsp, wsp], out_specs=blk, compiler_params=_params())(duc, drp, dip, wr, wi)


def _gates_bwd_w(uc, dpre, n, bw, gw, name):
    t, c = uc.shape
    per, ng = gw // bw, c // gw
    tk = _pick(t, (512, 256, 128))
    dims = (((0,), (0,)), ((), ()))

    def body(u_ref, d_ref, o_ref):
        @pl.when(pl.program_id(1) == 0)
        def _():
            o_ref[...] = jnp.zeros_like(o_ref)

        for b in range(per):
            cols = slice(b * bw, (b + 1) * bw)
            o_ref[b] += lax.dot_general(u_ref[:, cols].astype(BF16), d_ref[:, cols].astype(BF16), dims,
                                        preferred_element_type=F32)

    blk = pl.BlockSpec((tk, gw), lambda h, i: (i, h))
    return pl.pallas_call(body, name=name, out_shape=jax.ShapeDtypeStruct((n, bw, bw), F32), grid=(ng, t // tk),
                          in_specs=[blk, blk], out_specs=pl.BlockSpec((per, bw, bw), lambda h, i: (h, 0, 0)),
                          compiler_params=_params())(uc, dpre)


def _softplus(x):
    return jnp.maximum(x, 0.0) + jnp.log(1.0 + jnp.exp(-jnp.abs(x)))


_GELU_C = math.sqrt(2.0 / math.pi)


def _gelu_parts(x):
    inner = _GELU_C * (x + 0.044715 * (x * x * x))
    th = jnp.tanh(inner)
    gelu = 0.5 * x * (1.0 + th)
    dgelu = 0.5 * (1.0 + th) + 0.5 * x * (1.0 - th * th) * (_GELU_C * (1.0 + 3.0 * 0.044715 * (x * x)))
    return gelu, dgelu


def _lru_gate_values(rpre, ipre, br, bi, sp):
    r = _sigmoid(rpre + br)
    ig = _sigmoid(ipre + bi)
    log_a = -LRU_C * r * sp
    a = jnp.exp(log_a)
    e2 = jnp.tanh(-log_a) * (1.0 + a * a)
    inv = lax.rsqrt(jnp.maximum(e2, 1e-30))
    return r, ig, a, e2 * inv, inv


def _lru_fwd(uc, rpre, ipre, z, gr0, br, bi, lam, name, job=None):
    t, c = uc.shape
    cb = _pick(c, (1408, 1024, 512, 256, 128))
    tb = _pick(t, (512, 256, 128))
    ntile = tb // SUBLANE

    def body(uc_ref, r_ref, i_ref, gr_ref, br_ref, bi_ref, lam_ref, h_ref, rec16_ref, carry, rec_ref):
        @pl.when(pl.program_id(1) == 0)
        def _():
            carry[...] = jnp.zeros_like(carry)

        sp = _softplus(-lam_ref[...])
        br, bi = br_ref[...], bi_ref[...]
        row = lax.broadcasted_iota(jnp.int32, (SUBLANE, cb), 0)

        def tile(k, c_in):
            sl = pl.ds(pl.multiple_of(k * SUBLANE, SUBLANE), SUBLANE)
            ucv = uc_ref[sl, :]
            _, ig, a, mult, _ = _lru_gate_values(r_ref[sl, :], i_ref[sl, :], br, bi, sp)
            b = mult * (ig * ucv)
            for d in (1, 2, 4):
                a_s = jnp.where(row >= d, pltpu.roll(a, d, 0), 1.0)
                b_s = jnp.where(row >= d, pltpu.roll(b, d, 0), 0.0)
                b = a * b_s + b
                a = a * a_s
            hv = b + a * c_in
            h_ref[sl, :] = hv
            rec_ref[sl, :] = hv * _gelu_parts(gr_ref[sl, :])[0]
            return hv[SUBLANE - 1:SUBLANE, :]

        c_out = lax.fori_loop(0, ntile, tile, carry[0:1, :])
        carry[...] = jnp.broadcast_to(c_out, (SUBLANE, cb))
        rec16_ref[...] = rec_ref[...].astype(BF16)

    blk = pl.BlockSpec((tb, cb), lambda j, i: (i, j))
    vec = pl.BlockSpec((1, cb), lambda j, i: (0, j))
    res, extra = _call(body, name, (jax.ShapeDtypeStruct((t, c), F32), jax.ShapeDtypeStruct((t, c), BF16)),
                       (c // cb, t // tb),
                       [blk, blk, blk, _window(tb, cb, gr0, lambda j, i: i, lambda j, i: j), vec, vec, vec], (blk, blk),
                       (uc, rpre, ipre, z, br, bi, lam), [pltpu.VMEM((SUBLANE, cb), F32), pltpu.VMEM((tb, cb), F32)], job)
    return res if job is None else (res, extra)


def _lru_bwd(drec, hst, uc, rpre, ipre, z, gr0, br, bi, lam, dz, name, job=None):
    t, c = uc.shape
    cb = _pick(c, (1408, 1024, 512, 256, 128))
    tb = _pick(t, (256, 128))
    ntile, nt, hb = tb // SUBLANE, t // tb, tb // SUBLANE

    def body(drec_ref, h_ref, hprev_ref, uc_ref, r_ref, i_ref, gr_ref, br_ref, bi_ref, lam_ref, dz_ref,
             dgr16_ref, drp_ref, dip_ref, duc_ref, dlam_ref, dbr_ref, dbi_ref, carry, dgr_ref):
        step = pl.program_id(1)
        first_block = step == nt - 1

        @pl.when(step == 0)
        def _():
            carry[...] = jnp.zeros_like(carry)
            dlam_ref[...] = jnp.zeros_like(dlam_ref)
            dbr_ref[...] = jnp.zeros_like(dbr_ref)
            dbi_ref[...] = jnp.zeros_like(dbi_ref)

        lam = lam_ref[...]
        sp = _softplus(-lam)
        br, bi = br_ref[...], bi_ref[...]
        row = lax.broadcasted_iota(jnp.int32, (SUBLANE, cb), 0)
        halo = jnp.where(first_block, 0.0, hprev_ref[...])

        def tile(kk, state):
            c_p, acc_sp, acc_br, acc_bi = state
            k = ntile - 1 - kk
            sl = pl.ds(pl.multiple_of(k * SUBLANE, SUBLANE), SUBLANE)
            slp = pl.ds(pl.multiple_of(jnp.maximum(k - 1, 0) * SUBLANE, SUBLANE), SUBLANE)
            ucv = uc_ref[sl, :]
            r, ig, a, mult, inv_mult = _lru_gate_values(r_ref[sl, :], i_ref[sl, :], br, bi, sp)
            hv = h_ref[sl, :]
            below = jnp.where(k > 0, h_ref[slp, :], halo)
            hprev = jnp.where(row == 0, pltpu.roll(below, 1, 0), pltpu.roll(hv, 1, 0))
            gelu, dgelu = _gelu_parts(gr_ref[sl, :])
            drec = drec_ref[sl, :]
            dh = drec * gelu
            dgr_ref[sl, :] = drec * hv * dgelu
            pa, pb = a, a * dh
            for d in (1, 2, 4):
                a_s = jnp.where(row < SUBLANE - d, pltpu.roll(pa, SUBLANE - d, 0), 1.0)
                b_s = jnp.where(row < SUBLANE - d, pltpu.roll(pb, SUBLANE - d, 0), 0.0)
                pb = pa * b_s + pb
                pa = pa * a_s
            pv = pb + pa * c_p
            gt = dh + jnp.where(row == SUBLANE - 1, c_p, pltpu.roll(pv, SUBLANE - 1, 0))
            da = gt * hprev
            duc_ref[sl, :] = gt * mult * ig
            dmult = gt * ig * ucv
            dig = gt * mult * ucv
            dla = da * a - jnp.where(mult > 0.0, dmult * (a * a) * inv_mult, 0.0)
            drp = dla * (-LRU_C * sp) * (r * (1.0 - r))
            dip = dig * (ig * (1.0 - ig))
            drp_ref[sl, :] = drp
            dip_ref[sl, :] = dip
            return pv[0:1, :], acc_sp + dla * (-LRU_C * r), acc_br + drp, acc_bi + dip

        zero = jnp.zeros((SUBLANE, cb), F32)
        c_out, acc_sp, acc_br, acc_bi = lax.fori_loop(0, ntile, tile, (carry[0:1, :], zero, zero, zero))
        carry[...] = jnp.broadcast_to(c_out, (SUBLANE, cb))
        dlam_ref[...] += jnp.sum(acc_sp, axis=0, keepdims=True) * (-_sigmoid(-lam))
        dbr_ref[...] += jnp.sum(acc_br, axis=0, keepdims=True)
        dbi_ref[...] += jnp.sum(acc_bi, axis=0, keepdims=True)
        dgr16_ref[...] = dgr_ref[...].astype(BF16)

    blk = pl.BlockSpec((tb, cb), lambda j, i: (nt - 1 - i, j))
    vec = pl.BlockSpec((1, cb), lambda j, i: (0, j))
    halo_spec = pl.BlockSpec((SUBLANE, cb), lambda j, i: (jnp.maximum((nt - 1 - i) * hb - 1, 0), j))
    big, small = jax.ShapeDtypeStruct((t, c), F32), jax.ShapeDtypeStruct((1, c), F32)
    at_gr = _window(tb, cb, gr0, lambda j, i: nt - 1 - i, lambda j, i: j)
    res, extra = _call(
        body, name, (jax.ShapeDtypeStruct(dz.shape, BF16), big, big, big, small, small, small), (c // cb, nt),
        [blk, blk, halo_spec, blk, blk, blk, at_gr, vec, vec, vec, ANY], (at_gr, blk, blk, blk, vec, vec, vec),
        (drec, hst, hst, uc, rpre, ipre, z, br, bi, lam, dz),
        [pltpu.VMEM((SUBLANE, cb), F32), pltpu.VMEM((tb, cb), F32)], job, {10: 0})
    return res if job is None else (res, extra)


def _shard_region(ref, kind, chip, half, rh, width):
    if kind == "col":
        return ref.at[pl.ds(half * rh, rh), pl.ds(chip * width, width)]
    return ref.at[pl.ds(chip * (2 * rh) + half * rh, rh), :]


class _AllGather(_Exchange):
    def __init__(self, fulls, kinds):
        self.inputs, self.kinds = list(fulls), kinds
        self.out_shapes = [jax.ShapeDtypeStruct(f.shape, f.dtype) for f in fulls]
        self.aliases = {a: a for a in range(len(fulls))}
        self.n_sems = 6 * len(fulls)
        self.geo = [(f.shape[0] // 2, f.shape[1] // N_CHIPS) if k == "col" else (f.shape[0] // (2 * N_CHIPS), f.shape[1])
                    for f, k in zip(fulls, kinds)]

    def _region(self, ref, a, chip, half):
        return _shard_region(ref, self.kinds[a], chip, half, *self.geo[a])

    def _ici(self, e, a, k, chip):
        cx, cy = e.chips[k]
        return e.copy(self._region(e.ins[a], a, chip, e.c), self._region(e.outs[a], a, chip, e.c), a * 6 + k,
                      (cx, cy, e.c))

    def _d2d(self, e, a, k, half):
        cx, cy = e.chips[k]
        region = self._region(e.outs[a], a, 2 * cx + cy, half)
        return e.copy(region, region, a * 6 + 3 + k, e.sibling)

    def start(self, e):
        for a in range(len(self.inputs)):
            for k in range(3):
                self._ici(e, a, k, e.me).start()

    def finish(self, e):
        n = len(self.inputs)
        for a in range(n):
            for k, (cx, cy) in enumerate(e.chips):
                self._ici(e, a, k, 2 * cx + cy).wait_recv()
                self._d2d(e, a, k, e.c).start()
        for a in range(n):
            for k in range(3):
                self._d2d(e, a, k, 1 - e.c).wait_recv()
        for a in range(n):
            for k in range(3):
                self._ici(e, a, k, e.me).wait_send()
                self._d2d(e, a, k, e.c).wait_send()


class _SiblingExchange(_Exchange):
    def __init__(self, grads):
        self.inputs = list(grads)
        self.out_shapes = [jax.ShapeDtypeStruct((g.shape[0],) + g.shape[2:], g.dtype) for g in grads]
        self.n_sems = len(grads)

    def _copy(self, e, a):
        return e.copy(e.ins[a].at[:, 1 - e.c], e.outs[a], a, e.sibling)

    def start(self, e):
        for a in range(len(self.inputs)):
            self._copy(e, a).start()

    def finish(self, e):
        for a in range(len(self.inputs)):
            self._copy(e, a).wait()


def _piece(ref, kind, chip, width):
    if kind == "col":
        return ref.at[0, :, pl.ds(chip * width, width)]
    return ref.at[chip]


class _ChipExchange(_Exchange):
    def __init__(self, sums, kinds):
        self.inputs, self.kinds = list(sums), kinds
        self.widths = [s.shape[2] // N_CHIPS if k == "col" else s.shape[2] for s, k in zip(sums, kinds)]
        self.out_shapes = [jax.ShapeDtypeStruct((3, s.shape[1], w), s.dtype) for s, w in zip(sums, self.widths)]
        self.n_sems = 3 * len(sums)

    def _copy(self, e, a, k, chip):
        cx, cy = e.chips[k]
        return e.copy(_piece(e.ins[a], self.kinds[a], chip, self.widths[a]), e.outs[a].at[k], a * 3 + k, (cx, cy, e.c))

    def start(self, e):
        for a in range(len(self.inputs)):
            for k, (cx, cy) in enumerate(e.chips):
                self._copy(e, a, k, 2 * cx + cy).start()

    def finish(self, e):
        for a in range(len(self.inputs)):
            for k, (cx, cy) in enumerate(e.chips):
                self._copy(e, a, k, 2 * cx + cy).wait()


class _FinishExchange(_Exchange):
    def __init__(self, finals, to_all):
        self.inputs, self.to_all = list(finals), list(to_all)
        self.out_shapes = [jax.ShapeDtypeStruct(f.shape, f.dtype) for f in finals]
        self.aliases = {a: a for a in range(len(finals))}
        self.first_sem, self.n_sems = [], 0
        for all8 in self.to_all:
            self.first_sem.append(self.n_sems)
            self.n_sems += 7 if all8 else 1
        self.rel = [(fx, fy, fc) for fx in (0, 1) for fy in (0, 1) for fc in (0, 1)][1:]

    def _copies(self, e, mine):
        for a, all8 in enumerate(self.to_all):
            src = e.ins[a] if mine else e.outs[a]
            if not all8:
                rh = self.inputs[a].shape[0] // 2
                rows = pl.ds((e.c if mine else 1 - e.c) * rh, rh)
                yield e.copy(src.at[rows, :], e.outs[a].at[rows, :], self.first_sem[a], e.sibling)
                continue
            rh = self.inputs[a].shape[0] // (2 * N_CHIPS)
            for r, (fx, fy, fc) in enumerate(self.rel):
                px, py, pc = (1 - e.x if fx else e.x), (1 - e.y if fy else e.y), (1 - e.c if fc else e.c)
                rows = pl.ds(((2 * e.me + e.c) if mine else (2 * (2 * px + py) + pc)) * rh, rh)
                yield e.copy(src.at[rows, :], e.outs[a].at[rows, :], self.first_sem[a] + r, (px, py, pc))

    def start(self, e):
        for cp in self._copies(e, True):
            cp.start()

    def finish(self, e):
        for cp in self._copies(e, False):
            cp.wait_recv()
        for cp in self._copies(e, True):
            cp.wait_send()


def _cast_into_full(w, kind, idx, name):
    r, c = w.shape
    tr = _row_tile(r, c)
    nrb = r // tr

    def body(idx_ref, w_ref, o_ref):
        o_ref[...] = w_ref[...].astype(BF16)

    if kind == "col":
        full, out_map = (r, N_CHIPS * c), (lambda i, idx_ref: (i, idx_ref[1]))
    else:
        full, out_map = (N_CHIPS * r, c), (lambda i, idx_ref: (idx_ref[1] * nrb + i, 0))
    return pl.pallas_call(
        body, name=name, out_shape=jax.ShapeDtypeStruct(full, BF16),
        grid_spec=pltpu.PrefetchScalarGridSpec(
            num_scalar_prefetch=1, grid=(nrb,), in_specs=[pl.BlockSpec((tr, c), lambda i, idx_ref: (i, 0))],
            out_specs=pl.BlockSpec((tr, c), out_map)),
        compiler_params=_params(),
    )(idx, w)


def _matmul_gathering(a, placed, order, name):
    t, k = a.shape
    n = placed.shape[1]
    w = n // N_CHIPS
    tm, tn = _pick(t, _M_TILES), _pick(w, _N_TILES)
    ni, nj = t // tm, w // tn
    per_shard, total = ni * nj, N_CHIPS * ni * nj
    gather = _AllGather([placed], ["col"])

    def body(ord_ref, a_ref, w_own_ref, o_ref, w_ref, wbuf, fetch_sem, send, recv):
        s, i, j = pl.program_id(0), pl.program_id(1), pl.program_id(2)
        step = (s * ni + i) * nj + j
        e = _Env((w_own_ref,), (w_ref,), send, recv)

        def fetch(src, st):
            col = pl.multiple_of((ord_ref[st // per_shard] * nj + st % nj) * tn, LANE)
            return pltpu.make_async_copy(src.at[:, pl.ds(col, tn)], wbuf.at[st % 2], fetch_sem.at[st % 2])

        @pl.when(step == 0)
        def _():
            gather.start(e)
            fetch(w_own_ref, step).start()

        nxt = step + 1
        for kk, (cx, cy) in enumerate(e.chips):
            @pl.when(nxt == (kk + 1) * per_shard)
            def _():
                gather._ici(e, 0, kk, 2 * cx + cy).wait_recv()
                gather._d2d(e, 0, kk, e.c).start()
                gather._d2d(e, 0, kk, 1 - e.c).wait_recv()

        @pl.when(nxt < per_shard)
        def _():
            fetch(w_own_ref, nxt).start()

        @pl.when((nxt >= per_shard) & (nxt < total))
        def _():
            fetch(w_ref, nxt).start()

        fetch(w_ref, step).wait()
        o_ref[...] = jnp.dot(a_ref[...], wbuf[step % 2], preferred_element_type=F32)

        @pl.when(step == total - 1)
        def _():
            for kk in range(3):
                gather._ici(e, 0, kk, e.me).wait_send()
                gather._d2d(e, 0, kk, e.c).wait_send()

    z, full = pl.pallas_call(
        body, name=name, out_shape=(jax.ShapeDtypeStruct((t, n), F32), jax.ShapeDtypeStruct(placed.shape, placed.dtype)),
        grid_spec=pltpu.PrefetchScalarGridSpec(
            num_scalar_prefetch=1, grid=(N_CHIPS, ni, nj),
            in_specs=[pl.BlockSpec((tm, k), lambda s, i, j, ord_ref: (i, 0)), ANY],
            out_specs=(pl.BlockSpec((tm, tn), lambda s, i, j, ord_ref: (i, ord_ref[s] * nj + j)), ANY),
            scratch_shapes=[pltpu.VMEM((2, k, tn), placed.dtype), pltpu.SemaphoreType.DMA((2,)),
                            pltpu.SemaphoreType.DMA((gather.n_sems,)), pltpu.SemaphoreType.DMA((gather.n_sems,))]),
        input_output_aliases={2: 1}, compiler_params=_params(),
    )(order, a, placed)
    return z, full


def _add_own_half(g4, recv, idx, out_dtype, name):
    p, _, rh, n = g4.shape
    tr, tc = _tile2d(rh, n, 1024 * 1024)

    def body(idx_ref, g_ref, r_ref, o_ref):
        o_ref[...] = (g_ref[...] + r_ref[...]).astype(out_dtype)

    return pl.pallas_call(
        body, name=name, out_shape=jax.ShapeDtypeStruct((p, rh, n), out_dtype),
        grid_spec=pltpu.PrefetchScalarGridSpec(
            num_scalar_prefetch=1, grid=(p, rh // tr, n // tc),
            in_specs=[pl.BlockSpec((None, None, tr, tc), lambda q, i, j, idx_ref: (q, idx_ref[0], i, j)),
                      pl.BlockSpec((None, tr, tc), lambda q, i, j, idx_ref: (q, i, j))],
            out_specs=pl.BlockSpec((None, tr, tc), lambda q, i, j, idx_ref: (q, i, j))),
        compiler_params=_params(),
    )(idx, g4, recv)


def _sum_chips(own, kind, parts, idx, slots, to_all, name):
    _, rh, w = parts.shape
    tr, tc = _tile2d(rh, w, 512 * 1024)
    nrb, ncb = rh // tr, w // tc

    def body(idx_ref, own_ref, p0, p1, p2, o_ref):
        o_ref[...] = ((own_ref[...].astype(F32) + p0[...].astype(F32)) + p1[...].astype(F32)) + p2[...].astype(F32)

    if kind == "col":
        own_spec = pl.BlockSpec((None, tr, tc), lambda i, j, idx_ref: (0, i, idx_ref[1] * ncb + j))
    else:
        own_spec = pl.BlockSpec((None, tr, tc), lambda i, j, idx_ref: (idx_ref[1], i, j))
    if to_all:
        out_map = lambda i, j, idx_ref: ((2 * idx_ref[1] + idx_ref[0]) * nrb + i, j)
    else:
        out_map = lambda i, j, idx_ref: (idx_ref[0] * nrb + i, j)

    def part(k):
        return pl.BlockSpec((None, tr, tc), lambda i, j, idx_ref: (k, i, j))

    return pl.pallas_call(
        body, name=name, out_shape=jax.ShapeDtypeStruct((slots * rh, w), F32),
        grid_spec=pltpu.PrefetchScalarGridSpec(
            num_scalar_prefetch=1, grid=(nrb, ncb), in_specs=[own_spec, part(0), part(1), part(2)],
            out_specs=pl.BlockSpec((tr, tc), out_map)),
        compiler_params=_params(),
    )(idx, own, parts, parts, parts)


class _Reduce:
    def __init__(self, name, g, kind, idx, wire, to_all):
        r, c = g.shape
        self.name, self.kind, self.idx, self.wire, self.to_all = name, kind, idx, wire, to_all
        self.view = g.reshape(1, 2, r // 2, c) if kind == "col" else g.reshape(N_CHIPS, 2, r // (2 * N_CHIPS), c)

    def sibling(self):
        return _SiblingExchange([self.view])

    def got_sibling(self, outs):
        self.sum = _add_own_half(self.view, outs[0], self.idx, self.wire, "grad_chip_sum_" + self.name)

    def chips(self):
        return _ChipExchange([self.sum], [self.kind])

    def got_chips(self, outs):
        self.total = _sum_chips(self.sum, self.kind, outs[0], self.idx, 2 * N_CHIPS if self.to_all else 2,
                                self.to_all, "grad_total_" + self.name)


def _pack(arrays, rows):
    flat = jnp.concatenate([a.reshape(-1) for a in arrays])
    return jnp.pad(flat, (0, rows * SMALL_PACK_COLS - flat.shape[0])).reshape(rows, SMALL_PACK_COLS)


def _unpack(packed, shapes):
    flat = packed.reshape(-1)
    out, o = [], 0
    for shp in shapes:
        size = math.prod(shp)
        out.append(flat[o:o + size].reshape(shp))
        o += size
    return out


def _pack_rows(shapes):
    total = sum(math.prod(s) for s in shapes)
    unit = SMALL_PACK_COLS * N_CHIPS * 2 * SUBLANE
    return -(-total // unit) * (N_CHIPS * 2 * SUBLANE)


BIG = ("w_in", "w_attn_proj", "w_lru_proj", "w_out", "w_ffn_gate", "w_ffn_up", "w_ffn_down")
BIG_KIND = {"w_in": "col", "w_attn_proj": "row", "w_lru_proj": "row", "w_out": "row", "w_ffn_gate": "col",
            "w_ffn_up": "col", "w_ffn_down": "row"}
SMALL = ("norm1_g", "b_gates", "q_norm_g", "k_norm_g", "sinks", "conv_w", "conv_b", "w_rgate", "b_rgate",
         "w_igate", "b_igate", "lru_lambda", "norm2_g")
PACKED = tuple(n for n in SMALL if n not in ("w_rgate", "w_igate"))
WEIGHTS = ("norm1_g", "w_in", "b_gates", "q_norm_g", "k_norm_g", "sinks", "conv_w", "conv_b", "w_rgate", "b_rgate",
           "w_igate", "b_igate", "lru_lambda", "w_attn_proj", "w_lru_proj", "w_out", "norm2_g", "w_ffn_gate",
           "w_ffn_up", "w_ffn_down")


def kernel(x, positions, norm1_g, w_in, b_gates, q_norm_g, k_norm_g, sinks, conv_w, conv_b, w_rgate, b_rgate, w_igate, b_igate, lru_lambda, w_attn_proj, w_lru_proj, w_out, norm2_g, w_ffn_gate, w_ffn_up, w_ffn_down, loss_target, m_norm1_g, m_w_in, m_b_gates, m_q_norm_g, m_k_norm_g, m_sinks, m_conv_w, m_conv_b, m_w_rgate, m_b_rgate, m_w_igate, m_b_igate, m_lru_lambda, m_w_attn_proj, m_w_lru_proj, m_w_out, m_norm2_g, m_w_ffn_gate, m_w_ffn_up, m_w_ffn_down, v_norm1_g, v_w_in, v_b_gates, v_q_norm_g, v_k_norm_g, v_sinks, v_conv_w, v_conv_b, v_w_rgate, v_b_rgate, v_w_igate, v_b_igate, v_lru_lambda, v_w_attn_proj, v_w_lru_proj, v_w_out, v_norm2_g, v_w_ffn_gate, v_w_ffn_up, v_w_ffn_down):
    args = dict(locals())
    w = {n: args[n] for n in WEIGHTS}
    mom = {n: args["m_" + n] for n in WEIGHTS}
    var = {n: args["v_" + n] for n in WEIGHTS}

    t, d = x.shape[1], x.shape[2]
    hd = q_norm_g.shape[-1]
    nq = sinks.shape[-1]
    q_w = nq * hd
    d_rnn = conv_b.shape[-1]
    taps = conv_w.shape[1]
    n_blocks, bw = w_rgate.shape[1], w_rgate.shape[2]
    in_w = w_in.shape[-1] * N_CHIPS
    kv_w = (in_w - q_w - 2 * d_rnn - 2 * d) // 2
    kv = kv_w // hd
    grp = nq // kv
    u_off = q_w + 2 * kv_w
    gr_off = u_off + d_rnn
    ga_off = gr_off + d_rnn
    gw = bw * LANE // math.gcd(bw, LANE)
    chip = 2 * lax.axis_index("x") + lax.axis_index("y")
    idx = jnp.stack([lax.axis_index("c"), chip]).astype(jnp.int32)

    x2, tgt = x[0], loss_target[0]

    placed = {n: _cast_into_full(w[n][0], BIG_KIND[n], idx, "cast_" + n) for n in BIG}

    def gather(*names):
        return _AllGather([placed[n] for n in names], [BIG_KIND[n] for n in names])

    mx, my = lax.axis_index("x"), lax.axis_index("y")
    order = jnp.stack([chip, 2 * (1 - mx) + my, 2 * mx + (1 - my), 2 * (1 - mx) + (1 - my)]).astype(jnp.int32)
    conv_w_full = _gather_small(conv_w[0], "allgather_conv_w")
    conv_w_full = jnp.transpose(conv_w_full, (1, 0, 2)).reshape(taps, d_rnn)

    inv_freq = ROPE_THETA ** (-jnp.arange(0, hd // 4, 2, dtype=F32) / (hd // 4))
    ang = positions[0].astype(F32)[:, None] * inv_freq
    cos, sin = jnp.cos(ang), jnp.sin(ang)
    rest = hd - 2 * cos.shape[1]
    cos_t = jnp.concatenate([cos, cos, jnp.ones((t, rest), F32)], axis=1)
    sin_t = jnp.concatenate([-sin, sin, jnp.zeros((t, rest), F32)], axis=1)
    sinks1 = sinks[0]

    xn = _rms_fwd(x2, norm1_g, "rms1_fwd")
    z, win_f = _matmul_gathering(xn, placed["w_in"], order, "in_proj")
    attn, (wap_f, wlp_f, wout_f) = _attn_fwd(z, cos_t, sin_t, q_norm_g, k_norm_g, sinks1, kv, grp, hd, "attn_fwd",
                                             job=gather("w_attn_proj", "w_lru_proj", "w_out"))
    uc = _conv_fwd(z, u_off, d_rnn, conv_w_full, conv_b, "conv_fwd")
    rpre, ipre = _gates_fwd(uc, w_rgate[0], w_igate[0], gw, "gates_fwd")
    (hst, rec), (wg_f,) = _lru_fwd(uc, rpre, ipre, z, gr_off, b_rgate, b_igate, lru_lambda, "lru_fwd",
                                   job=gather("w_ffn_gate"))
    pa = _matmul(attn, wap_f, "nn", "attn_proj")
    plru, merged = _matmul(rec, wlp_f, "nn", "lru_proj", fused=(
        [pa, (z, ga_off), (z, ga_off + d), (b_gates, 0), (b_gates, d)], _merge_after_lru_proj, (F32, BF16)))
    h1 = _matmul(merged, wout_f, "nn", "out_proj", add=x2)
    hn = _rms_fwd(h1, norm2_g, "rms2_fwd")
    gate, (wu_f,) = _matmul(hn, wg_f, "nn", "ffn_gate", job=gather("w_ffn_up"))
    (up, act), (wd_f,) = _matmul(hn, wu_f, "nn", "ffn_up", job=gather("w_ffn_down"),
                                 fused=([gate], _swiglu_after_up, (F32, BF16)))
    yout = _matmul(act, wd_f, "nn", "ffn_down", add=h1)
    dy, dy16, loss_part = _loss_head(yout, tgt, "loss_head")
    loss = lax.psum(loss_part[0, 0], ("x", "y", "c"))

    def reduction(n, g):
        return _Reduce(n, g, BIG_KIND[n], idx, BF16, False)

    r_wd = reduction("w_ffn_down", _matmul(act, dy16, "tn", "d_w_ffn_down"))
    dgate, dup = _matmul(dy16, wd_f, "nt", "d_act", fused=([gate, up], _swiglu_bwd_after_dact, (BF16, BF16)))
    r_wg = reduction("w_ffn_gate", _matmul(hn, dgate, "tn", "d_w_ffn_gate"))
    both = _Jobs(r_wg.sibling(), r_wd.sibling())
    g_wu, got = _matmul(hn, dup, "tn", "d_w_ffn_up", job=both)
    got_wg, got_wd = both.split(got)
    r_wg.got_sibling(got_wg)
    r_wd.got_sibling(got_wd)
    r_wu = reduction("w_ffn_up", g_wu)
    dhn = _matmul(dgate, wg_f, "nt", "d_hn_gate")
    dhn = _matmul(dup, wu_f, "nt", "d_hn_up", add=dhn)
    dh1, g_norm2, dh1_16 = _rms_bwd(dhn, h1, norm2_g, dy, "rms2_bwd", mxu_copy=True)
    r_wout = reduction("w_out", _matmul(merged, dh1_16, "tn", "d_w_out"))
    both = _Jobs(r_wout.sibling(), r_wu.sibling())
    dmerged, got = _matmul(dh1_16, wout_f, "nt", "d_merged", job=both)
    got_wout, got_wu = both.split(got)
    r_wout.got_sibling(got_wout)
    r_wu.got_sibling(got_wu)
    (dpa, dz, g_ba), got = _merge_bwd(dmerged, z, b_gates, pa, ga_off, 0, None, "merge_bwd_attn", job=r_wout.chips())
    r_wout.got_chips(got)
    dpl, dz, g_bl = _merge_bwd(dmerged, z, b_gates, plru, ga_off + d, d, dz, "merge_bwd_lru")
    r_wap = reduction("w_attn_proj", _matmul(attn, dpa, "tn", "d_w_attn_proj"))
    dattn, got = _matmul(dpa, wap_f, "nt", "d_attn", job=r_wap.sibling())
    r_wap.got_sibling(got)
    g_wlp, got = _matmul(rec, dpl, "tn", "d_w_lru_proj", job=r_wap.chips())
    r_wap.got_chips(got)
    r_wlp = reduction("w_lru_proj", g_wlp)
    drec, got = _matmul(dpl, wlp_f, "nt", "d_rec", job=r_wlp.sibling())
    r_wlp.got_sibling(got)
    both = _Jobs(r_wlp.chips(), r_wg.chips())
    (dz, drp, dip, duc_direct, g_lam, g_br, g_bi), got = _lru_bwd(
        drec, hst, uc, rpre, ipre, z, gr_off, b_rgate, b_igate, lru_lambda, dz, "lru_bwd", job=both)
    got_wlp, got_wg = both.split(got)
    r_wlp.got_chips(got_wlp)
    r_wg.got_chips(got_wg)
    duc = _gates_bwd_x(duc_direct, drp, dip, w_rgate[0], w_igate[0], gw, "gates_bwd_x")
    g_wr = _gates_bwd_w(uc, drp, n_blocks, bw, gw, "gates_bwd_wr")
    g_wi = _gates_bwd_w(uc, dip, n_blocks, bw, gw, "gates_bwd_wi")
    dz, g_convw, g_convb = _conv_bwd(duc, z, u_off, conv_w_full, dz, "conv_bwd")
    (dq, dk, dv, g_qg, g_kg, g_sinks), got = _attn_bwd(dattn, z, cos_t, sin_t, q_norm_g, k_norm_g, sinks1, kv, grp, hd,
                                                        "attn_bwd", job=_Jobs(r_wu.chips(), r_wd.chips()))
    r_wu.got_chips(got[:1])
    r_wd.got_chips(got[1:])
    for part, col in ((dq, 0), (dk, q_w), (dv, q_w + kv_w)):
        dz = lax.dynamic_update_slice(dz, part, (0, col))
    r_wr = _Reduce("w_rgate", g_wr.reshape(n_blocks * bw, bw), "row", idx, F32, True)
    r_wi = _Reduce("w_igate", g_wi.reshape(n_blocks * bw, bw), "row", idx, F32, True)
    early = [r_wap, r_wlp, r_wout, r_wg, r_wu, r_wd]
    three = _Jobs(r_wr.sibling(), r_wi.sibling(), _FinishExchange([r.total for r in early], [False] * len(early)))
    g_top, got = _matmul(xn, dz, "tn", "d_w_in_top", m_window=(0, d // 2), job=three)
    got_wr, got_wi, finished = three.split(got)
    r_wr.got_sibling(got_wr)
    r_wi.got_sibling(got_wi)
    r_top = _Reduce("w_in_top", g_top, "col", idx, BF16, False)
    three = _Jobs(r_top.sibling(), r_wr.chips(), r_wi.chips())
    g_bot, got = _matmul(xn, dz, "tn", "d_w_in_bot", m_window=(d // 2, d // 2), job=three)
    got_top, got_wr, got_wi = three.split(got)
    r_top.got_sibling(got_top)
    r_wr.got_chips(got_wr)
    r_wi.got_chips(got_wi)
    r_bot = _Reduce("w_in_bot", g_bot, "col", idx, BF16, False)
    both = _Jobs(r_top.chips(), r_bot.sibling())
    dxn, got = _matmul(dz, win_f, "nt", "d_xn_a", m_window=(0, t // 2), into=(None, t), job=both)
    got_top, got_bot = both.split(got)
    r_top.got_chips(got_top)
    r_bot.got_sibling(got_bot)
    dxn, got = _matmul(dz, win_f, "nt", "d_xn_b", m_window=(t // 2, t // 2), into=(dxn, t), job=r_bot.chips())
    r_bot.got_chips(got)
    dx, g_norm1 = _rms_bwd(dxn, x2, norm1_g, dh1, "rms1_bwd")

    small_grads = {"norm1_g": g_norm1, "b_gates": jnp.concatenate([g_ba, g_bl], axis=1), "q_norm_g": g_qg,
                   "k_norm_g": g_kg, "sinks": g_sinks[:, :nq], "conv_w": g_convw, "conv_b": g_convb,
                   "b_rgate": g_br, "b_igate": g_bi, "lru_lambda": g_lam, "norm2_g": g_norm2}
    gshapes = [small_grads[n].shape for n in PACKED]
    small_sum, (top, bot, grads_wr, grads_wi) = _allreduce_small(
        _pack([small_grads[n] for n in PACKED], _pack_rows(gshapes)),
        _FinishExchange([r.total for r in (r_top, r_bot, r_wr, r_wi)], [False, False, True, True]), "grad_last_exchange")
    grads = dict(zip(BIG[1:], finished))
    grads["w_in"] = jnp.concatenate([top, bot], axis=0)
    grads["w_rgate"], grads["w_igate"] = grads_wr, grads_wi
    small_full = dict(zip(PACKED, _unpack(small_sum, gshapes)))
    per = d_rnn // N_CHIPS
    small_full["conv_w"] = lax.dynamic_slice(small_full["conv_w"], (0, chip * per), (taps, per))
    grads.update(small_full)

    delta, new_m, new_v = {}, {}, {}
    for n in BIG + ("w_rgate", "w_igate"):
        as2d = (lambda a: a[0]) if n in BIG else (lambda a: a.reshape(n_blocks * bw, bw))
        if n == "w_in":
            delta[n], new_m[n], new_v[n] = _adamw(as2d(w[n]), grads[n], as2d(mom[n]), as2d(var[n]), "adamw_" + n)
        else:
            delta[n], new_m[n], new_v[n], grads[n] = _adamw(as2d(w[n]), grads[n], as2d(mom[n]), as2d(var[n]),
                                                            "adamw_" + n, pass_grad=True)
    pshapes = [w[n].shape for n in PACKED]
    prows = _pack_rows(pshapes)
    pk = [_pack([src[n] for n in PACKED], prows) for src in (w, grads, mom, var)]
    for res, packed in zip((delta, new_m, new_v), _adamw(pk[0], pk[1], pk[2], pk[3], "adamw_small")):
        res.update(dict(zip(PACKED, _unpack(packed, pshapes))))

    outs = [loss, dx.reshape(x.shape)]
    for res in (grads, delta, new_m, new_v):
        outs += [res[n].reshape(w[n].shape) for n in WEIGHTS]
    return tuple(outs)


def _allreduce_small(x, job, name):
    n_dev = 2 * N_CHIPS
    rel = [(fx, fy, fc) for fx in (0, 1) for fy in (0, 1) for fc in (0, 1)][1:]
    j_in, j_out = len(job.inputs), len(job.out_shapes)

    def body(x_ref, *refs):
        jins, all_ref, o_ref, jouts = refs[:j_in], refs[j_in], refs[j_in + 1], refs[j_in + 2:j_in + 2 + j_out]
        send_sems, recv_sems, jsend, jrecv = refs[j_in + 2 + j_out:]
        e = _Env((x_ref,), (all_ref,), send_sems, recv_sems)
        carried = _Env(jins, jouts, jsend, jrecv, 0, (e.x, e.y, e.c))
        job.start(carried)
        mine = 2 * e.me + e.c

        def peer(r):
            fx, fy, fc = rel[r]
            return (1 - e.x if fx else e.x), (1 - e.y if fy else e.y), (1 - e.c if fc else e.c)

        all_ref[mine] = x_ref[...]
        for r in range(len(rel)):
            e.copy(x_ref, all_ref.at[mine], r, peer(r)).start()
        for r in range(len(rel)):
            px, py, pc = peer(r)
            e.copy(x_ref, all_ref.at[2 * (2 * px + py) + pc], r, peer(r)).wait_recv()
        total = all_ref[0]
        for dev in range(1, n_dev):
            total = total + all_ref[dev]
        o_ref[...] = total
        for r in range(len(rel)):
            e.copy(x_ref, all_ref.at[mine], r, peer(r)).wait_send()
        job.finish(carried)

    vm = pl.BlockSpec(memory_space=pltpu.VMEM)
    res = pl.pallas_call(
        body, name=name,
        out_shape=(jax.ShapeDtypeStruct((n_dev,) + x.shape, x.dtype), jax.ShapeDtypeStruct(x.shape, x.dtype))
        + tuple(job.out_shapes),
        in_specs=[vm] + [ANY] * j_in, out_specs=(vm, vm) + (ANY,) * j_out,
        input_output_aliases={1 + i: 2 + o for i, o in job.aliases.items()},
        scratch_shapes=[pltpu.SemaphoreType.DMA((len(rel),)), pltpu.SemaphoreType.DMA((len(rel),)),
                        pltpu.SemaphoreType.DMA((job.n_sems,)), pltpu.SemaphoreType.DMA((job.n_sems,))])(x, *job.inputs)
    return res[1], tuple(res[2:])


def _gather_small(shard, name):
    def body(s_ref, o_ref, send_sems, recv_sems):
        e = _Env((s_ref,), (o_ref,), send_sems, recv_sems)
        o_ref[e.me] = s_ref[...]
        for k, (cx, cy) in enumerate(e.chips):
            e.copy(s_ref, o_ref.at[e.me], k, (cx, cy, e.c)).start()
        for k, (cx, cy) in enumerate(e.chips):
            e.copy(s_ref, o_ref.at[2 * cx + cy], k, (cx, cy, e.c)).wait_recv()
        for k, (cx, cy) in enumerate(e.chips):
            e.copy(s_ref, o_ref.at[e.me], k, (cx, cy, e.c)).wait_send()

    vm = pl.BlockSpec(memory_space=pltpu.VMEM)
    return pl.pallas_call(body, name=name, out_shape=jax.ShapeDtypeStruct((N_CHIPS,) + shard.shape, shard.dtype),
                          in_specs=[vm], out_specs=vm,
                          scratch_shapes=[pltpu.SemaphoreType.DMA((3,)), pltpu.SemaphoreType.DMA((3,))])(shard)
```

```python
import functools
import math

import jax
import jax.numpy as jnp
from jax import lax
from jax.experimental import pallas as pl
from jax.experimental.pallas import tpu as pltpu

F32 = jnp.float32
BF16 = jnp.bfloat16
MESH = pl.DeviceIdType.MESH

WINDOW = 128
BLK = 128
ROPE_THETA = 500000.0
LRU_C = 8.0
EPS = 1e-6
NEG = -1e30
ADAM_LR = 0.001
ADAM_B1 = 0.9
ADAM_B2 = 0.999
ADAM_EPS = 1e-08
ADAM_WD = 0.01
ADAM_STEP = 10

VMEM_LIMIT_BYTES = 52 * 1024 * 1024
LANE = 128
SUBLANE = 8
N_CHIPS = 4
SMALL_PACK_COLS = 512


def _params(**kw):
    return pltpu.CompilerParams(vmem_limit_bytes=VMEM_LIMIT_BYTES, **kw)


def _pick(dim, cands):
    for c in cands:
        if dim % c == 0:
            return c
    return dim


def _sigmoid(x):
    return 0.5 * jnp.tanh(0.5 * x) + 0.5


ANY = pl.BlockSpec(memory_space=pl.ANY)
LOCAL_FETCH_DMA_PRIORITY = 1


class _Env:
    def __init__(self, ins, outs, send, recv, sem0=0, place=None):
        self.ins, self.outs, self.send, self.recv, self.sem0 = ins, outs, send, recv, sem0
        self.x, self.y, self.c = place or (lax.axis_index("x"), lax.axis_index("y"), lax.axis_index("c"))
        self.me = 2 * self.x + self.y
        self.chips = [(1 - self.x, self.y), (self.x, 1 - self.y), (1 - self.x, 1 - self.y)]
        self.sibling = (self.x, self.y, 1 - self.c)

    def sub(self, i0, n_in, o0, n_out, sem0):
        return _Env(self.ins[i0:i0 + n_in], self.outs[o0:o0 + n_out], self.send, self.recv, self.sem0 + sem0,
                    (self.x, self.y, self.c))

    def copy(self, src, dst, sem, to):
        return pltpu.make_async_remote_copy(src_ref=src, dst_ref=dst, send_sem=self.send.at[self.sem0 + sem],
                                            recv_sem=self.recv.at[self.sem0 + sem], device_id=to, device_id_type=MESH)


class _Exchange:
    inputs, out_shapes, aliases, n_sems = (), (), {}, 0

    def start(self, e):
        raise NotImplementedError

    def finish(self, e):
        raise NotImplementedError


class _Jobs(_Exchange):
    def __init__(self, *jobs):
        self.jobs, self.inputs, self.out_shapes, self.aliases, self.n_sems, self.at = jobs, [], [], {}, 0, []
        for job in jobs:
            self.at.append((len(self.inputs), len(self.out_shapes), self.n_sems))
            self.aliases.update({len(self.inputs) + i: len(self.out_shapes) + o for i, o in job.aliases.items()})
            self.inputs += list(job.inputs)
            self.out_shapes += list(job.out_shapes)
            self.n_sems += job.n_sems

    def _each(self, e):
        for job, (i0, o0, s0) in zip(self.jobs, self.at):
            yield job, e.sub(i0, len(job.inputs), o0, len(job.out_shapes), s0)

    def split(self, outs):
        return [tuple(outs[o0:o0 + len(job.out_shapes)]) for job, (_, o0, _) in zip(self.jobs, self.at)]

    def start(self, e):
        for job, se in self._each(e):
            job.start(se)

    def finish(self, e):
        for job, se in self._each(e):
            job.finish(se)


def _call(body, name, out_shape, grid, in_specs, out_specs, args, scratch_shapes=(), job=None, aliases=None):
    aliases = dict(aliases or {})
    if job is None:
        return pl.pallas_call(body, name=name, out_shape=out_shape, grid=grid, in_specs=list(in_specs),
                              out_specs=out_specs, scratch_shapes=list(scratch_shapes), input_output_aliases=aliases,
                              compiler_params=_params())(*args), ()
    single = not isinstance(out_shape, (tuple, list))
    shapes = [out_shape] if single else list(out_shape)
    ospecs = [out_specs] if single else list(out_specs)
    n_in, n_out, n_scr = len(args), len(shapes), len(scratch_shapes)
    j_in, j_out = len(job.inputs), len(job.out_shapes)

    def hosted(*refs):
        ins, jins = refs[:n_in], refs[n_in:n_in + j_in]
        outs = refs[n_in + j_in:n_in + j_in + n_out]
        jouts = refs[n_in + j_in + n_out:n_in + j_in + n_out + j_out]
        rest = refs[n_in + j_in + n_out + j_out:]
        e = _Env(jins, jouts, rest[n_scr], rest[n_scr + 1])
        first = functools.reduce(jnp.logical_and, [pl.program_id(d) == 0 for d in range(len(grid))])
        last = functools.reduce(jnp.logical_and, [pl.program_id(d) == g - 1 for d, g in enumerate(grid)])

        @pl.when(first)
        def _():
            job.start(e)

        body(*ins, *outs, *rest[:n_scr])

        @pl.when(last)
        def _():
            job.finish(e)

    res = pl.pallas_call(
        hosted, name=name, out_shape=tuple(shapes + list(job.out_shapes)), grid=grid,
        in_specs=list(in_specs) + [ANY] * j_in, out_specs=tuple(ospecs + [ANY] * j_out),
        scratch_shapes=list(scratch_shapes) + [pltpu.SemaphoreType.DMA((job.n_sems,)),
                                               pltpu.SemaphoreType.DMA((job.n_sems,))],
        input_output_aliases={**aliases, **{n_in + i: n_out + o for i, o in job.aliases.items()}},
        compiler_params=_params())(*args, *job.inputs)
    return (res[0] if single else tuple(res[:n_out])), tuple(res[n_out:])


_M_TILES = (1024, 1408, 1280, 512, 256, 128)
_N_TILES = (1408, 1280, 1024, 640, 512, 256, 128)
MXU_FULL_ROWS = 1024
MATMUL_VMEM_BUDGET = 42 * 1024 * 1024
MXU_FLOPS_PER_HBM_BYTE = 500


def _matmul_tiles(m, n, k, sa, sb, so, has_add, tn_divides=0):
    best = None
    for tm in [c for c in _M_TILES if m % c == 0] or [m]:
        for tn in [c for c in _N_TILES if n % c == 0 and tn_divides % c == 0] or [n]:
            for nk in range(1, 17):
                tk = k // nk
                if k % nk or tk % LANE:
                    continue
                need = 2 * (tm * tk * sa + tk * tn * sb) + 2 * tm * tn * (so + (4 if has_add else 0))
                need += tm * tn * 4 if nk > 1 else 0
                fetched = tk * tn * sb + tm * tk * sa // (1 if nk > 1 else n // tn)
                if need > MATMUL_VMEM_BUDGET:
                    continue
                mxu_bound = fetched * MXU_FLOPS_PER_HBM_BYTE <= 2 * tm * tn * tk
                key = (mxu_bound, min(tm, MXU_FULL_ROWS), -nk, tn, tm)
                if best is None or key > best[0]:
                    best = (key, (tm, tn, tk))
    assert best is not None, (m, n, k)
    return best[1]


def _matmul(a, b, mode, name, add=None, out_dtype=F32, job=None, m_window=None, into=None, fused=None):
    if mode == "nn":
        (m, k), (k2, n) = a.shape, b.shape
    elif mode == "nt":
        (m, k), (n, k2) = a.shape, b.shape
    else:
        (k, m), (k2, n) = a.shape, b.shape
    assert k == k2, (a.shape, b.shape, mode)
    m0, m = m_window or (0, m)
    tiles, fuse_fn, out_dtypes = fused or ((), None, (out_dtype,))
    tiles = [x if isinstance(x, tuple) else (x, 0) for x in tiles]
    tm, tn, tk = _matmul_tiles(math.gcd(m, m0) if m0 else m, n, k, a.dtype.itemsize, b.dtype.itemsize,
                               sum(jnp.dtype(dt).itemsize for dt in out_dtypes)
                               + sum(x.dtype.itemsize for x, _ in tiles if x.shape[0] > 1),
                               add is not None, math.gcd(*[c0 for _, c0 in tiles], 0))
    nk, mb0 = k // tk, m0 // tm
    if mode == "nn":
        a_spec = pl.BlockSpec((tm, tk), lambda i, j, kk: (mb0 + i, kk))
        b_spec = pl.BlockSpec((tk, tn), lambda i, j, kk: (kk, j))
        dims = (((1,), (0,)), ((), ()))
    elif mode == "nt":
        a_spec = pl.BlockSpec((tm, tk), lambda i, j, kk: (mb0 + i, kk))
        b_spec = pl.BlockSpec((tn, tk), lambda i, j, kk: (j, kk))
        dims = (((1,), (1,)), ((), ()))
    else:
        a_spec = pl.BlockSpec((tk, tm), lambda i, j, kk: (kk, mb0 + i))
        b_spec = pl.BlockSpec((tk, tn), lambda i, j, kk: (kk, j))
        dims = (((0,), (0,)), ((), ()))
    out_rows, ob0 = (into[1], mb0) if into is not None else (m, 0)
    o_spec = pl.BlockSpec((tm, tn), lambda i, j, kk: (ob0 + i, j))
    has_add = add is not None
    begun = into is not None and into[0] is not None

    n_side, n_out = has_add + len(tiles), len(out_dtypes)

    def body(*refs):
        a_ref, b_ref = refs[:2]
        side = refs[2:2 + n_side]
        o_refs = refs[len(refs) - n_out - (nk > 1):len(refs) - (nk > 1)]
        part = lax.dot_general(a_ref[...].astype(BF16), b_ref[...].astype(BF16), dims, preferred_element_type=F32)

        def finish(r):
            if has_add:
                r = r + side[0][...]
            vals = fuse_fn(r, *[x[...] for x in side[has_add:]]) if fuse_fn else (r,)
            for o_ref, val, dt in zip(o_refs, vals, out_dtypes):
                o_ref[...] = val.astype(dt)

        if nk == 1:
            finish(part)
            return
        acc = refs[-1]
        kk = pl.program_id(2)

        @pl.when(kk == 0)
        def _():
            acc[...] = part

        @pl.when(kk > 0)
        def _():
            acc[...] += part

        @pl.when(kk == nk - 1)
        def _():
            finish(acc[...])

    def side_spec(x, c0):
        if x.shape[0] == 1:
            return pl.BlockSpec((1, tn), lambda i, j, kk: (0, c0 // tn + j))
        return pl.BlockSpec((tm, tn), lambda i, j, kk: (mb0 + i, c0 // tn + j))

    in_specs = [a_spec, b_spec] + ([side_spec(add, 0)] if has_add else []) + [side_spec(x, c0) for x, c0 in tiles]
    args = (a, b) + ((add,) if has_add else ()) + tuple(x for x, _ in tiles)
    aliases = None
    if begun:
        aliases = {len(args): 0}
        in_specs, args = in_specs + [ANY], args + (into[0],)
    shapes = tuple(jax.ShapeDtypeStruct((out_rows, n), dt) for dt in out_dtypes)
    res, extra = _call(body, name, shapes if fused else shapes[0], (m // tm, n // tn, nk), in_specs,
                       (o_spec,) * n_out if fused else o_spec, args, [pltpu.VMEM((tm, tn), F32)] if nk > 1 else [],
                       job, aliases)
    return res if job is None else (res, extra)


def _row_tile(rows, cols, budget_elems=512 * 1024):
    cands = [c for c in (1024, 704, 512, 352, 256, 128, 64, 32, 16) if c * cols <= budget_elems]
    return _pick(rows, cands or (16,))


_EW_COLS = (1280, 1408, 1024, 640, 512, 256, 128)


def _tile2d(rows, cols, max_elems):
    tc = _pick(cols, _EW_COLS)
    return _row_tile(rows, tc, max_elems), tc


def _rms_fwd(x, g, name):
    t, d = x.shape
    tr = _row_tile(t, d)

    def body(x_ref, g_ref, o_ref):
        xv = x_ref[...]
        rstd = lax.rsqrt(jnp.mean(xv * xv, axis=-1, keepdims=True) + EPS)
        o_ref[...] = (xv * rstd * g_ref[...]).astype(BF16)

    spec = pl.BlockSpec((tr, d), lambda i: (i, 0))
    return pl.pallas_call(body, name=name, out_shape=jax.ShapeDtypeStruct((t, d), BF16), grid=(t // tr,),
                          in_specs=[spec, pl.BlockSpec((1, d), lambda i: (0, 0))], out_specs=spec,
                          compiler_params=_params())(x, g)


def _rms_bwd(dxn, x, g, resid, name, job=None, mxu_copy=False):
    t, d = x.shape
    tr = _row_tile(t, d, 512 * 1024)

    def body(dxn_ref, x_ref, g_ref, r_ref, dx_ref, dg_ref, *dx16_ref):
        @pl.when(pl.program_id(0) == 0)
        def _():
            dg_ref[...] = jnp.zeros_like(dg_ref)

        xv = x_ref[...]
        rstd = lax.rsqrt(jnp.mean(xv * xv, axis=-1, keepdims=True) + EPS)
        xhat = xv * rstd
        dy = dxn_ref[...]
        dg_ref[...] += jnp.sum(dy * xhat, axis=0, keepdims=True)
        dxhat = dy * g_ref[...]
        dx = r_ref[...] + rstd * (dxhat - xhat * jnp.mean(dxhat * xhat, axis=-1, keepdims=True))
        dx_ref[...] = dx
        if mxu_copy:
            dx16_ref[0][...] = dx.astype(BF16)

    spec = pl.BlockSpec((tr, d), lambda i: (i, 0))
    vec = pl.BlockSpec((1, d), lambda i: (0, 0))
    shapes = (jax.ShapeDtypeStruct((t, d), F32), jax.ShapeDtypeStruct((1, d), F32))
    shapes += (jax.ShapeDtypeStruct((t, d), BF16),) if mxu_copy else ()
    res, extra = _call(body, name, shapes, (t // tr,), [spec, spec, vec, spec],
                       (spec, vec) + ((spec,) if mxu_copy else ()), (dxn, x, g, resid), (), job)
    return res if job is None else (res, extra)


def _swiglu_after_up(up, gate):
    return up, gate * _sigmoid(gate) * up


def _swiglu_bwd_after_dact(dact, gate, up):
    sg = _sigmoid(gate)
    return dact * up * (sg * (1.0 + gate * (1.0 - sg))), dact * (gate * sg)


def _merge_after_lru_proj(plru, pa, ga, gl, ba, bl):
    return plru, _sigmoid(ga + ba) * pa + _sigmoid(gl + bl) * plru


def _merge_bwd(dmerged, z, b_gates, p, z0, b0, dz, name, job=None):
    t, d = p.shape
    cw = _pick(math.gcd(z0, d), (512, 256, 128))
    tr = _row_tile(t, cw, 256 * 1024)
    oz, ob, nd = z0 // cw, b0 // cw, d // cw

    def body(dm_ref, g_ref, b_ref, p_ref, *rest):
        dp_ref, dg_ref, sum_ref = rest[-3:]

        @pl.when(pl.program_id(1) == 0)
        def _():
            sum_ref[...] = jnp.zeros_like(sum_ref)

        dm = dm_ref[...]
        sg = _sigmoid(g_ref[...] + b_ref[...])
        dp_ref[...] = (dm * sg).astype(BF16)
        dg = dm * p_ref[...] * (sg * (1.0 - sg))
        dg_ref[...] = dg.astype(BF16)
        sum_ref[...] += jnp.sum(dg, axis=0, keepdims=True)

    blk = pl.BlockSpec((tr, cw), lambda j, i: (i, j))
    at_z = pl.BlockSpec((tr, cw), lambda j, i: (i, oz + j))
    in_specs = [blk, at_z, pl.BlockSpec((1, cw), lambda j, i: (0, ob + j)), blk]
    args, aliases = (dmerged, z, b_gates, p), None
    if dz is not None:
        in_specs, args, aliases = in_specs + [ANY], args + (dz,), {4: 1}
    res, extra = _call(
        body, name, (jax.ShapeDtypeStruct((t, d), BF16), jax.ShapeDtypeStruct(z.shape, BF16),
                     jax.ShapeDtypeStruct((1, d), F32)), (nd, t // tr), in_specs,
        (blk, at_z, pl.BlockSpec((1, cw), lambda j, i: (0, j))), args, (), job, aliases)
    return res if job is None else (res, extra)


def _loss_head(y, target, name):
    t, d = y.shape
    tr = _row_tile(t, d, 512 * 1024)
    nt = t // tr

    def body(y_ref, t_ref, dy_ref, dy16_ref, loss_ref, acc):
        i = pl.program_id(0)

        @pl.when(i == 0)
        def _():
            acc[...] = jnp.zeros_like(acc)

        e = y_ref[...] - t_ref[...]
        dy = e * (1.0 / d)
        dy_ref[...] = dy
        dy16_ref[...] = dy.astype(BF16)
        acc[...] += jnp.sum(e * e, axis=0, keepdims=True)

        @pl.when(i == nt - 1)
        def _():
            loss_ref[...] = (0.5 / d) * jnp.sum(acc[...], axis=-1, keepdims=True)

    spec = pl.BlockSpec((tr, d), lambda i: (i, 0))
    return pl.pallas_call(
        body, name=name, out_shape=(jax.ShapeDtypeStruct((t, d), F32), jax.ShapeDtypeStruct((t, d), BF16),
                                    jax.ShapeDtypeStruct((1, 1), F32)),
        grid=(nt,), in_specs=[spec, spec], out_specs=(spec, spec, pl.BlockSpec((1, 1), lambda i: (0, 0))),
        scratch_shapes=[pltpu.VMEM((1, d), F32)], compiler_params=_params(),
    )(y, target)


def _adamw(w, g, m, v, name, pass_grad=False):
    r, c = w.shape
    tr, tc = _tile2d(r, c, 512 * 1024)
    c1 = 1.0 - ADAM_B1 ** ADAM_STEP
    c2 = 1.0 - ADAM_B2 ** ADAM_STEP

    def body(w_ref, g_ref, m_ref, v_ref, d_ref, nm_ref, nv_ref, *g_out):
        gv = g_ref[...]
        if pass_grad:
            g_out[0][...] = gv
        mn = ADAM_B1 * m_ref[...] + (1.0 - ADAM_B1) * gv
        vn = ADAM_B2 * v_ref[...] + (1.0 - ADAM_B2) * (gv * gv)
        d_ref[...] = -ADAM_LR * ((mn / c1) / (jnp.sqrt(vn / c2) + ADAM_EPS) + ADAM_WD * w_ref[...])
        nm_ref[...] = mn
        nv_ref[...] = vn

    spec = pl.BlockSpec((tr, tc), lambda i, j: (i, j))
    shp = jax.ShapeDtypeStruct((r, c), F32)
    n_out = 4 if pass_grad else 3
    return pl.pallas_call(body, name=name, out_shape=(shp,) * n_out, grid=(r // tr, c // tc), in_specs=[spec] * 4,
                          out_specs=(spec,) * n_out, compiler_params=_params())(w, g, m, v)


def _swap_halves(v, half):
    n = v.shape[-1]
    lane = lax.broadcasted_iota(jnp.int32, v.shape, 1)
    return jnp.where(lane < half, pltpu.roll(v, n - half, 1),
                     jnp.where(lane < 2 * half, pltpu.roll(v, half, 1), 0.0))


def _rope(y, c, s, half):
    return y * c + _swap_halves(y, half) * s


def _rope_bwd(dout, c, s, half):
    return dout * c + _swap_halves(dout * s, half)


def _stack_heads(ref, grp, hd):
    return jnp.concatenate([ref[:, g * hd:(g + 1) * hd] for g in range(grp)], axis=0)


def _softmax_with_sinks(s, sink_ref, first, grp, i):
    rows = s.shape[0]
    qi = lax.broadcasted_iota(jnp.int32, s.shape, 0) & (BLK - 1)
    kj = lax.broadcasted_iota(jnp.int32, s.shape, 1)
    rel = qi + BLK - kj
    s = jnp.where((rel >= 0) & (rel < WINDOW) & ((kj >= BLK) | (i > 0)), s, NEG)
    head = lax.broadcasted_iota(jnp.int32, (rows, 1), 0) // BLK
    sk = jnp.zeros((rows, 1), F32)
    for g in range(grp):
        sk = jnp.where(head == g, sink_ref[first + g], sk)
    mx = jnp.maximum(jnp.max(s, axis=-1, keepdims=True), sk)
    p = jnp.exp(s - mx)
    esk = jnp.exp(sk - mx)
    inv_den = 1.0 / (jnp.sum(p, axis=-1, keepdims=True) + esk)
    return p * inv_den, esk * inv_den, head


def _norm_fwd(xraw, g):
    rstd = lax.rsqrt(jnp.mean(xraw * xraw, axis=-1, keepdims=True) + EPS)
    xhat = xraw * rstd
    return xhat, rstd, xhat * g


def _norm_bwd(dy, xhat, rstd, g):
    dxhat = dy * g
    dx = rstd * (dxhat - xhat * jnp.mean(dxhat * xhat, axis=-1, keepdims=True))
    return dx, jnp.sum(dy * xhat, axis=0, keepdims=True)


def _attn_specs(nb, grp, hd, kv, clamp):
    qo, ko, vo = 0, (kv * grp), (kv * grp + kv)
    cur = (lambda i: jnp.minimum(i, nb - 1)) if clamp else (lambda i: i)
    prev = lambda i: jnp.maximum(cur(i) - 1, 0)
    zq = pl.BlockSpec((BLK, grp * hd), lambda h, i: (cur(i), h))
    kc = pl.BlockSpec((BLK, hd), lambda h, i: (cur(i), ko + h))
    kp = pl.BlockSpec((BLK, hd), lambda h, i: (prev(i), ko + h))
    vc = pl.BlockSpec((BLK, hd), lambda h, i: (cur(i), vo + h))
    vp = pl.BlockSpec((BLK, hd), lambda h, i: (prev(i), vo + h))
    tc = pl.BlockSpec((BLK, hd), lambda h, i: (cur(i), 0))
    tp = pl.BlockSpec((BLK, hd), lambda h, i: (prev(i), 0))
    gs = pl.BlockSpec((1, hd), lambda h, i: (0, 0))
    return zq, kc, kp, vc, vp, tc, tp, gs


def _attn_fwd(z, cos_t, sin_t, qg, kg, sinks, kv, grp, hd, name, job=None):
    t = z.shape[0]
    nb = t // BLK
    half = hd // 8
    scale = 1.0 / math.sqrt(hd)
    zq, kc, kp, vc, vp, tc, tp, gs = _attn_specs(nb, grp, hd, kv, False)

    def body(sink_ref, zq_ref, kc_ref, kp_ref, vc_ref, vp_ref, cc_ref, sc_ref, cp_ref, sp_ref, qg_ref, kg_ref, o_ref):
        h, i = pl.program_id(0), pl.program_id(1)

        def normrope(xraw, g, c, s):
            return _rope(_norm_fwd(xraw, g)[2], c, s, half)

        cc, sc = cc_ref[...], sc_ref[...]
        kcur = normrope(kc_ref[...], kg_ref[...], cc, sc)
        kprev = normrope(kp_ref[...], kg_ref[...], cp_ref[...], sp_ref[...])
        kk = jnp.concatenate([kprev, kcur], axis=0).astype(BF16)
        vv = jnp.concatenate([vp_ref[...], vc_ref[...]], axis=0).astype(BF16)
        for g in range(grp):
            q = normrope(zq_ref[:, g * hd:(g + 1) * hd], qg_ref[...], cc, sc).astype(BF16)
            s = lax.dot_general(q, kk, (((1,), (1,)), ((), ())), preferred_element_type=F32) * scale
            p, _, _ = _softmax_with_sinks(s, sink_ref, h * grp + g, 1, i)
            o_ref[:, g * hd:(g + 1) * hd] = jnp.dot(p.astype(BF16), vv, preferred_element_type=F32).astype(BF16)

    res, extra = _call(
        body, name, jax.ShapeDtypeStruct((t, kv * grp * hd), BF16), (kv, nb),
        [pl.BlockSpec(memory_space=pltpu.SMEM), zq, kc, kp, vc, vp, tc, tc, tp, tp, gs, gs],
        pl.BlockSpec((BLK, grp * hd), lambda h, i: (i, h)),
        (sinks, z, z, z, z, z, cos_t, sin_t, cos_t, sin_t, qg, kg), (), job)
    return res if job is None else (res, extra)


def _attn_bwd(dattn, z, cos_t, sin_t, qg, kg, sinks, kv, grp, hd, name, job=None):
    t = z.shape[0]
    nb = t // BLK
    half = hd // 8
    scale = 1.0 / math.sqrt(hd)
    zq, kc, kp, vc, vp, tc, tp, gs = _attn_specs(nb, grp, hd, kv, True)

    def body(sink_ref, zq_ref, kc_ref, kp_ref, vc_ref, vp_ref, cc_ref, sc_ref, cp_ref, sp_ref, qg_ref, kg_ref, do_ref,
             dq_ref, dk_ref, dv_ref, dqg_ref, dkg_ref, dsk_ref, dk_carry, dv_carry):
        h, i = pl.program_id(0), pl.program_id(1)
        lane1 = lax.broadcasted_iota(jnp.int32, (1, LANE), 1)

        @pl.when((h == 0) & (i == 0))
        def _():
            dqg_ref[...] = jnp.zeros_like(dqg_ref)
            dkg_ref[...] = jnp.zeros_like(dkg_ref)
            dsk_ref[...] = jnp.zeros_like(dsk_ref)

        @pl.when(i == 0)
        def _():
            dk_carry[...] = jnp.zeros_like(dk_carry)
            dv_carry[...] = jnp.zeros_like(dv_carry)

        @pl.when(i < nb)
        def _():
            cc, sc, cp, sp = cc_ref[...], sc_ref[...], cp_ref[...], sp_ref[...]
            qgv, kgv = qg_ref[...], kg_ref[...]
            xh_kc, rs_kc, y_kc = _norm_fwd(kc_ref[...], kgv)
            xh_kp, rs_kp, y_kp = _norm_fwd(kp_ref[...], kgv)
            kk = jnp.concatenate([_rope(y_kp, cp, sp, half), _rope(y_kc, cc, sc, half)], axis=0).astype(BF16)
            vv = jnp.concatenate([vp_ref[...], vc_ref[...]], axis=0).astype(BF16)
            cq, sq = jnp.concatenate([cc] * grp, axis=0), jnp.concatenate([sc] * grp, axis=0)
            xh_q, rs_q, y_q = _norm_fwd(_stack_heads(zq_ref, grp, hd), qgv)
            q = _rope(y_q, cq, sq, half).astype(BF16)
            s = lax.dot_general(q, kk, (((1,), (1,)), ((), ())), preferred_element_type=F32) * scale
            p, psink, head = _softmax_with_sinks(s, sink_ref, h * grp, grp, i)
            dog = _stack_heads(do_ref, grp, hd).astype(BF16)
            dp = lax.dot_general(dog, vv, (((1,), (1,)), ((), ())), preferred_element_type=F32)
            rsum = jnp.sum(p * dp, axis=-1, keepdims=True)
            ds = (p * (dp - rsum) * scale).astype(BF16)
            dsink = -psink * rsum
            dsk = jnp.zeros((1, LANE), F32)
            for g in range(grp):
                dsk = dsk + jnp.where(lane1 == h * grp + g,
                                      jnp.sum(jnp.where(head == g, dsink, 0.0), axis=0, keepdims=True), 0.0)
            dqn = jnp.dot(ds, kk, preferred_element_type=F32)
            dkk = lax.dot_general(ds, q, (((0,), (0,)), ((), ())), preferred_element_type=F32)
            dvv = lax.dot_general(p.astype(BF16), dog, (((0,), (0,)), ((), ())), preferred_element_type=F32)
            dxq, dqg = _norm_bwd(_rope_bwd(dqn, cq, sq, half), xh_q, rs_q, qgv)
            dxq = dxq.astype(BF16)
            for g in range(grp):
                dq_ref[:, g * hd:(g + 1) * hd] = dxq[g * BLK:(g + 1) * BLK]
            dkp_raw, dg_kp = _norm_bwd(_rope_bwd(dkk[:BLK], cp, sp, half), xh_kp, rs_kp, kgv)
            dkc_raw, dg_kc = _norm_bwd(_rope_bwd(dkk[BLK:], cc, sc, half), xh_kc, rs_kc, kgv)
            dk_ref[...] = (dk_carry[...] + dkp_raw).astype(BF16)
            dv_ref[...] = (dv_carry[...] + dvv[:BLK]).astype(BF16)
            dk_carry[...] = dkc_raw
            dv_carry[...] = dvv[BLK:]
            dqg_ref[...] += dqg
            dkg_ref[...] += dg_kp + dg_kc
            dsk_ref[...] += dsk

        @pl.when(i == nb)
        def _():
            dk_ref[...] = dk_carry[...].astype(BF16)
            dv_ref[...] = dv_carry[...].astype(BF16)

    kvw = kv * hd
    vec = pl.BlockSpec((1, hd), lambda h, i: (0, 0))
    shifted = pl.BlockSpec((BLK, hd), lambda h, i: (jnp.maximum(i - 1, 0), h))
    res, extra = _call(
        body, name,
        (jax.ShapeDtypeStruct((t, kv * grp * hd), BF16), jax.ShapeDtypeStruct((t, kvw), BF16),
         jax.ShapeDtypeStruct((t, kvw), BF16), jax.ShapeDtypeStruct((1, hd), F32),
         jax.ShapeDtypeStruct((1, hd), F32), jax.ShapeDtypeStruct((1, LANE), F32)),
        (kv, nb + 1),
        [pl.BlockSpec(memory_space=pltpu.SMEM), zq, kc, kp, vc, vp, tc, tc, tp, tp, gs, gs,
         pl.BlockSpec((BLK, grp * hd), lambda h, i: (jnp.minimum(i, nb - 1), h))],
        (pl.BlockSpec((BLK, grp * hd), lambda h, i: (jnp.minimum(i, nb - 1), h)), shifted, shifted, vec, vec,
         pl.BlockSpec((1, LANE), lambda h, i: (0, 0))),
        (sinks, z, z, z, z, z, cos_t, sin_t, cos_t, sin_t, qg, kg, dattn),
        [pltpu.VMEM((BLK, hd), F32), pltpu.VMEM((BLK, hd), F32)], job)
    return res if job is None else (res, extra)


def _window(rows, cb, c0, row_of, col_of):
    assert rows % SUBLANE == 0 and cb % LANE == 0 and c0 % LANE == 0, (rows, cb, c0)
    return pl.BlockSpec((pl.Element(rows), pl.Element(cb)),
                        lambda *g: (pl.multiple_of(row_of(*g) * rows, SUBLANE), pl.multiple_of(c0 + col_of(*g) * cb, LANE)))


def _conv_fwd(z, c0, c, w, b, name):
    t = z.shape[0]
    taps = w.shape[0]
    cb = _pick(c, (1408, 1024, 512, 256, 128))
    tr = _row_tile(t, cb, 512 * 1024)
    hb = tr // SUBLANE

    def body(u_ref, halo_ref, w_ref, b_ref, o_ref):
        i = pl.program_id(0)
        x = u_ref[...]
        acc = b_ref[...] + w_ref[taps - 1:taps, :] * x
        for k in range(taps - 1):
            acc = acc + w_ref[k:k + 1, :] * pltpu.roll(x, taps - 1 - k, 0)
        o_ref[...] = acc
        row = lax.broadcasted_iota(jnp.int32, (SUBLANE, cb), 0)
        hp = jnp.where(i > 0, halo_ref[...], 0.0)
        x8 = u_ref[0:SUBLANE, :]
        acc8 = b_ref[...] + w_ref[taps - 1:taps, :] * x8
        for k in range(taps - 1):
            s = taps - 1 - k
            acc8 = acc8 + w_ref[k:k + 1, :] * jnp.where(row < s, pltpu.roll(hp, s, 0), pltpu.roll(x8, s, 0))
        o_ref[0:SUBLANE, :] = acc8

    blk = pl.BlockSpec((tr, cb), lambda i, j: (i, j))
    return pl.pallas_call(
        body, name=name, out_shape=jax.ShapeDtypeStruct((t, c), F32), grid=(t // tr, c // cb),
        in_specs=[_window(tr, cb, c0, lambda i, j: i, lambda i, j: j),
                  _window(SUBLANE, cb, c0, lambda i, j: jnp.maximum(i * hb - 1, 0), lambda i, j: j),
                  pl.BlockSpec((taps, cb), lambda i, j: (0, j)), pl.BlockSpec((1, cb), lambda i, j: (0, j))],
        out_specs=blk, compiler_params=_params(),
    )(z, z, w, b)


def _conv_bwd(duc, z, c0, w, dz, name):
    t, c = duc.shape
    taps = w.shape[0]
    cb = _pick(c, (1408, 1024, 512, 256, 128))
    tr = _row_tile(t, cb, 512 * 1024)
    hb, nt = tr // SUBLANE, t // tr

    def body(g_ref, gnext_ref, u_ref, uprev_ref, w_ref, dz_ref, du16_ref, dw_ref, db_ref, du_ref):
        i = pl.program_id(1)

        @pl.when(i == 0)
        def _():
            dw_ref[...] = jnp.zeros_like(dw_ref)
            db_ref[...] = jnp.zeros_like(db_ref)

        row = lax.broadcasted_iota(jnp.int32, (SUBLANE, cb), 0)
        g, x = g_ref[...], u_ref[...]
        du = w_ref[taps - 1:taps, :] * g
        for k in range(taps - 1):
            du = du + w_ref[k:k + 1, :] * pltpu.roll(g, tr - (taps - 1 - k), 0)
        du_ref[...] = du
        hn = jnp.where(i < nt - 1, gnext_ref[...], 0.0)
        g8 = g_ref[tr - SUBLANE:tr, :]
        du8 = w_ref[taps - 1:taps, :] * g8
        for k in range(taps - 1):
            s = taps - 1 - k
            du8 = du8 + w_ref[k:k + 1, :] * jnp.where(row >= SUBLANE - s, pltpu.roll(hn, SUBLANE - s, 0),
                                                     pltpu.roll(g8, SUBLANE - s, 0))
        du_ref[tr - SUBLANE:tr, :] = du8
        du16_ref[...] = du_ref[...].astype(BF16)

        hp = jnp.where(i > 0, uprev_ref[...], 0.0)
        xl8, gf8 = u_ref[tr - SUBLANE:tr, :], g_ref[0:SUBLANE, :]
        db_ref[...] += jnp.sum(g, axis=0, keepdims=True)
        dw_ref[taps - 1:taps, :] += jnp.sum(g * x, axis=0, keepdims=True)
        for k in range(taps - 1):
            s = taps - 1 - k
            fix = jnp.where(row < s, pltpu.roll(hp, s, 0) - pltpu.roll(xl8, s, 0), 0.0)
            dw_ref[k:k + 1, :] += (jnp.sum(g * pltpu.roll(x, s, 0), axis=0, keepdims=True)
                                   + jnp.sum(gf8 * fix, axis=0, keepdims=True))

    blk = pl.BlockSpec((tr, cb), lambda j, i: (i, j))
    nh = t // SUBLANE
    return pl.pallas_call(
        body, name=name,
        out_shape=(jax.ShapeDtypeStruct(dz.shape, BF16), jax.ShapeDtypeStruct((taps, c), F32),
                   jax.ShapeDtypeStruct((1, c), F32)),
        grid=(c // cb, nt),
        in_specs=[blk, pl.BlockSpec((SUBLANE, cb), lambda j, i: (jnp.minimum((i + 1) * hb, nh - 1), j)),
                  _window(tr, cb, c0, lambda j, i: i, lambda j, i: j),
                  _window(SUBLANE, cb, c0, lambda j, i: jnp.maximum(i * hb - 1, 0), lambda j, i: j),
                  pl.BlockSpec((taps, cb), lambda j, i: (0, j)), ANY],
        out_specs=(_window(tr, cb, c0, lambda j, i: i, lambda j, i: j), pl.BlockSpec((taps, cb), lambda j, i: (0, j)),
                   pl.BlockSpec((1, cb), lambda j, i: (0, j))),
        scratch_shapes=[pltpu.VMEM((tr, cb), F32)], input_output_aliases={5: 0}, compiler_params=_params(),
    )(duc, duc, z, z, w, dz)


def _gates_fwd(uc, wr, wi, gw, name):
    t, c = uc.shape
    n, bw, _ = wr.shape
    per, ng = gw // bw, c // gw
    tr = _pick(t, (512, 256, 128))

    def body(u_ref, wr_ref, wi_ref, r_ref, i_ref):
        for b in range(per):
            cols = slice(b * bw, (b + 1) * bw)
            a = u_ref[:, cols].astype(BF16)
            r_ref[:, cols] = jnp.dot(a, wr_ref[b].astype(BF16), preferred_element_type=F32)
            i_ref[:, cols] = jnp.dot(a, wi_ref[b].astype(BF16), preferred_element_type=F32)

    blk = pl.BlockSpec((tr, gw), lambda h, i: (i, h))
    wsp = pl.BlockSpec((per, bw, bw), lambda h, i: (h, 0, 0))
    shp = jax.ShapeDtypeStruct((t, c), F32)
    return pl.pallas_call(body, name=name, out_shape=(shp, shp), grid=(ng, t // tr), in_specs=[blk, wsp, wsp],
                          out_specs=(blk, blk), compiler_params=_params())(uc, wr, wi)


def _gates_bwd_x(duc, drp, dip, wr, wi, gw, name):
    t, c = duc.shape
    n, bw, _ = wr.shape
    per, ng = gw // bw, c // gw
    tr = _pick(t, (512, 256, 128))
    dims = (((1,), (1,)), ((), ()))

    def body(d_ref, r_ref, i_ref, wr_ref, wi_ref, o_ref):
        for b in range(per):
            cols = slice(b * bw, (b + 1) * bw)
            o_ref[:, cols] = (
                d_ref[:, cols]
                + lax.dot_general(r_ref[:, cols].astype(BF16), wr_ref[b].astype(BF16), dims, preferred_element_type=F32)
                + lax.dot_general(i_ref[:, cols].astype(BF16), wi_ref[b].astype(BF16), dims, preferred_element_type=F32))

    blk = pl.BlockSpec((tr, gw), lambda h, i: (i, h))
    wsp = pl.BlockSpec((per, bw, bw), lambda h, i: (h, 0, 0))
    return pl.pallas_call(body, name=name, out_shape=jax.ShapeDtypeStruct((t, c), F32), grid=(ng, t // tr),
                          in_specs=[blk, blk, blk, wsp, wsp], out_specs=blk, compiler_params=_params())(duc, drp, dip, wr, wi)


def _gates_bwd_w(uc, dpre, n, bw, gw, name):
    t, c = uc.shape
    per, ng = gw // bw, c // gw
    tk = _pick(t, (512, 256, 128))
    dims = (((0,), (0,)), ((), ()))

    def body(u_ref, d_ref, o_ref):
        @pl.when(pl.program_id(1) == 0)
        def _():
            o_ref[...] = jnp.zeros_like(o_ref)

        for b in range(per):
            cols = slice(b * bw, (b + 1) * bw)
            o_ref[b] += lax.dot_general(u_ref[:, cols].astype(BF16), d_ref[:, cols].astype(BF16), dims,
                                        preferred_element_type=F32)

    blk = pl.BlockSpec((tk, gw), lambda h, i: (i, h))
    return pl.pallas_call(body, name=name, out_shape=jax.ShapeDtypeStruct((n, bw, bw), F32), grid=(ng, t // tk),
                          in_specs=[blk, blk], out_specs=pl.BlockSpec((per, bw, bw), lambda h, i: (h, 0, 0)),
                          compiler_params=_params())(uc, dpre)


def _softplus(x):
    return jnp.maximum(x, 0.0) + jnp.log(1.0 + jnp.exp(-jnp.abs(x)))


_GELU_C = math.sqrt(2.0 / math.pi)


def _gelu_parts(x):
    inner = _GELU_C * (x + 0.044715 * (x * x * x))
    th = jnp.tanh(inner)
    gelu = 0.5 * x * (1.0 + th)
    dgelu = 0.5 * (1.0 + th) + 0.5 * x * (1.0 - th * th) * (_GELU_C * (1.0 + 3.0 * 0.044715 * (x * x)))
    return gelu, dgelu


def _lru_gate_values(rpre, ipre, br, bi, sp):
    r = _sigmoid(rpre + br)
    ig = _sigmoid(ipre + bi)
    log_a = -LRU_C * r * sp
    a = jnp.exp(log_a)
    e2 = jnp.tanh(-log_a) * (1.0 + a * a)
    inv = lax.rsqrt(jnp.maximum(e2, 1e-30))
    return r, ig, a, e2 * inv, inv


def _lru_fwd(uc, rpre, ipre, z, gr0, br, bi, lam, name, job=None):
    t, c = uc.shape
    cb = _pick(c, (1408, 1024, 512, 256, 128))
    tb = _pick(t, (512, 256, 128))
    ntile = tb // SUBLANE

    def body(uc_ref, r_ref, i_ref, gr_ref, br_ref, bi_ref, lam_ref, h_ref, rec16_ref, carry, rec_ref):
        @pl.when(pl.program_id(1) == 0)
        def _():
            carry[...] = jnp.zeros_like(carry)

        sp = _softplus(-lam_ref[...])
        br, bi = br_ref[...], bi_ref[...]
        row = lax.broadcasted_iota(jnp.int32, (SUBLANE, cb), 0)

        def tile(k, c_in):
            sl = pl.ds(pl.multiple_of(k * SUBLANE, SUBLANE), SUBLANE)
            ucv = uc_ref[sl, :]
            _, ig, a, mult, _ = _lru_gate_values(r_ref[sl, :], i_ref[sl, :], br, bi, sp)
            b = mult * (ig * ucv)
            for d in (1, 2, 4):
                a_s = jnp.where(row >= d, pltpu.roll(a, d, 0), 1.0)
                b_s = jnp.where(row >= d, pltpu.roll(b, d, 0), 0.0)
                b = a * b_s + b
                a = a * a_s
            hv = b + a * c_in
            h_ref[sl, :] = hv
            rec_ref[sl, :] = hv * _gelu_parts(gr_ref[sl, :])[0]
            return hv[SUBLANE - 1:SUBLANE, :]

        c_out = lax.fori_loop(0, ntile, tile, carry[0:1, :])
        carry[...] = jnp.broadcast_to(c_out, (SUBLANE, cb))
        rec16_ref[...] = rec_ref[...].astype(BF16)

    blk = pl.BlockSpec((tb, cb), lambda j, i: (i, j))
    vec = pl.BlockSpec((1, cb), lambda j, i: (0, j))
    res, extra = _call(body, name, (jax.ShapeDtypeStruct((t, c), F32), jax.ShapeDtypeStruct((t, c), BF16)),
                       (c // cb, t // tb),
                       [blk, blk, blk, _window(tb, cb, gr0, lambda j, i: i, lambda j, i: j), vec, vec, vec], (blk, blk),
                       (uc, rpre, ipre, z, br, bi, lam), [pltpu.VMEM((SUBLANE, cb), F32), pltpu.VMEM((tb, cb), F32)], job)
    return res if job is None else (res, extra)


def _lru_bwd(drec, hst, uc, rpre, ipre, z, gr0, br, bi, lam, dz, name, job=None):
    t, c = uc.shape
    cb = _pick(c, (1408, 1024, 512, 256, 128))
    tb = _pick(t, (256, 128))
    ntile, nt, hb = tb // SUBLANE, t // tb, tb // SUBLANE

    def body(drec_ref, h_ref, hprev_ref, uc_ref, r_ref, i_ref, gr_ref, br_ref, bi_ref, lam_ref, dz_ref,
             dgr16_ref, drp_ref, dip_ref, duc_ref, dlam_ref, dbr_ref, dbi_ref, carry, dgr_ref):
        step = pl.program_id(1)
        first_block = step == nt - 1

        @pl.when(step == 0)
        def _():
            carry[...] = jnp.zeros_like(carry)
            dlam_ref[...] = jnp.zeros_like(dlam_ref)
            dbr_ref[...] = jnp.zeros_like(dbr_ref)
            dbi_ref[...] = jnp.zeros_like(dbi_ref)

        lam = lam_ref[...]
        sp = _softplus(-lam)
        br, bi = br_ref[...], bi_ref[...]
        row = lax.broadcasted_iota(jnp.int32, (SUBLANE, cb), 0)
        halo = jnp.where(first_block, 0.0, hprev_ref[...])

        def tile(kk, state):
            c_p, acc_sp, acc_br, acc_bi = state
            k = ntile - 1 - kk
            sl = pl.ds(pl.multiple_of(k * SUBLANE, SUBLANE), SUBLANE)
            slp = pl.ds(pl.multiple_of(jnp.maximum(k - 1, 0) * SUBLANE, SUBLANE), SUBLANE)
            ucv = uc_ref[sl, :]
            r, ig, a, mult, inv_mult = _lru_gate_values(r_ref[sl, :], i_ref[sl, :], br, bi, sp)
            hv = h_ref[sl, :]
            below = jnp.where(k > 0, h_ref[slp, :], halo)
            hprev = jnp.where(row == 0, pltpu.roll(below, 1, 0), pltpu.roll(hv, 1, 0))
            gelu, dgelu = _gelu_parts(gr_ref[sl, :])
            drec = drec_ref[sl, :]
            dh = drec * gelu
            dgr_ref[sl, :] = drec * hv * dgelu
            pa, pb = a, a * dh
            for d in (1, 2, 4):
                a_s = jnp.where(row < SUBLANE - d, pltpu.roll(pa, SUBLANE - d, 0), 1.0)
                b_s = jnp.where(row < SUBLANE - d, pltpu.roll(pb, SUBLANE - d, 0), 0.0)
                pb = pa * b_s + pb
                pa = pa * a_s
            pv = pb + pa * c_p
            gt = dh + jnp.where(row == SUBLANE - 1, c_p, pltpu.roll(pv, SUBLANE - 1, 0))
            da = gt * hprev
            duc_ref[sl, :] = gt * mult * ig
            dmult = gt * ig * ucv
            dig = gt * mult * ucv
            dla = da * a - jnp.where(mult > 0.0, dmult * (a * a) * inv_mult, 0.0)
            drp = dla * (-LRU_C * sp) * (r * (1.0 - r))
            dip = dig * (ig * (1.0 - ig))
            drp_ref[sl, :] = drp
            dip_ref[sl, :] = dip
            return pv[0:1, :], acc_sp + dla * (-LRU_C * r), acc_br + drp, acc_bi + dip

        zero = jnp.zeros((SUBLANE, cb), F32)
        c_out, acc_sp, acc_br, acc_bi = lax.fori_loop(0, ntile, tile, (carry[0:1, :], zero, zero, zero))
        carry[...] = jnp.broadcast_to(c_out, (SUBLANE, cb))
        dlam_ref[...] += jnp.sum(acc_sp, axis=0, keepdims=True) * (-_sigmoid(-lam))
        dbr_ref[...] += jnp.sum(acc_br, axis=0, keepdims=True)
        dbi_ref[...] += jnp.sum(acc_bi, axis=0, keepdims=True)
        dgr16_ref[...] = dgr_ref[...].astype(BF16)

    blk = pl.BlockSpec((tb, cb), lambda j, i: (nt - 1 - i, j))
    vec = pl.BlockSpec((1, cb), lambda j, i: (0, j))
    halo_spec = pl.BlockSpec((SUBLANE, cb), lambda j, i: (jnp.maximum((nt - 1 - i) * hb - 1, 0), j))
    big, small = jax.ShapeDtypeStruct((t, c), F32), jax.ShapeDtypeStruct((1, c), F32)
    at_gr = _window(tb, cb, gr0, lambda j, i: nt - 1 - i, lambda j, i: j)
    res, extra = _call(
        body, name, (jax.ShapeDtypeStruct(dz.shape, BF16), big, big, big, small, small, small), (c // cb, nt),
        [blk, blk, halo_spec, blk, blk, blk, at_gr, vec, vec, vec, ANY], (at_gr, blk, blk, blk, vec, vec, vec),
        (drec, hst, hst, uc, rpre, ipre, z, br, bi, lam, dz),
        [pltpu.VMEM((SUBLANE, cb), F32), pltpu.VMEM((tb, cb), F32)], job, {10: 0})
    return res if job is None else (res, extra)


def _shard_region(ref, kind, chip, half, rh, width):
    if kind == "col":
        return ref.at[pl.ds(half * rh, rh), pl.ds(chip * width, width)]
    return ref.at[pl.ds(chip * (2 * rh) + half * rh, rh), :]


class _AllGather(_Exchange):
    def __init__(self, fulls, kinds):
        self.inputs, self.kinds = list(fulls), kinds
        self.out_shapes = [jax.ShapeDtypeStruct(f.shape, f.dtype) for f in fulls]
        self.aliases = {a: a for a in range(len(fulls))}
        self.n_sems = 6 * len(fulls)
        self.geo = [(f.shape[0] // 2, f.shape[1] // N_CHIPS) if k == "col" else (f.shape[0] // (2 * N_CHIPS), f.shape[1])
                    for f, k in zip(fulls, kinds)]

    def _region(self, ref, a, chip, half):
        return _shard_region(ref, self.kinds[a], chip, half, *self.geo[a])

    def _ici(self, e, a, k, chip):
        cx, cy = e.chips[k]
        return e.copy(self._region(e.ins[a], a, chip, e.c), self._region(e.outs[a], a, chip, e.c), a * 6 + k,
                      (cx, cy, e.c))

    def _d2d(self, e, a, k, half):
        cx, cy = e.chips[k]
        region = self._region(e.outs[a], a, 2 * cx + cy, half)
        return e.copy(region, region, a * 6 + 3 + k, e.sibling)

    def start(self, e):
        for a in range(len(self.inputs)):
            for k in range(3):
                self._ici(e, a, k, e.me).start()

    def finish(self, e):
        n = len(self.inputs)
        for a in range(n):
            for k, (cx, cy) in enumerate(e.chips):
                self._ici(e, a, k, 2 * cx + cy).wait_recv()
                self._d2d(e, a, k, e.c).start()
        for a in range(n):
            for k in range(3):
                self._d2d(e, a, k, 1 - e.c).wait_recv()
        for a in range(n):
            for k in range(3):
                self._ici(e, a, k, e.me).wait_send()
                self._d2d(e, a, k, e.c).wait_send()


class _SiblingExchange(_Exchange):
    def __init__(self, grads):
        self.inputs = list(grads)
        self.out_shapes = [jax.ShapeDtypeStruct((g.shape[0],) + g.shape[2:], g.dtype) for g in grads]
        self.n_sems = len(grads)

    def _copy(self, e, a):
        return e.copy(e.ins[a].at[:, 1 - e.c], e.outs[a], a, e.sibling)

    def start(self, e):
        for a in range(len(self.inputs)):
            self._copy(e, a).start()

    def finish(self, e):
        for a in range(len(self.inputs)):
            self._copy(e, a).wait()


def _piece(ref, kind, chip, width):
    if kind == "col":
        return ref.at[0, :, pl.ds(chip * width, width)]
    return ref.at[chip]


class _ChipExchange(_Exchange):
    def __init__(self, sums, kinds):
        self.inputs, self.kinds = list(sums), kinds
        self.widths = [s.shape[2] // N_CHIPS if k == "col" else s.shape[2] for s, k in zip(sums, kinds)]
        self.out_shapes = [jax.ShapeDtypeStruct((3, s.shape[1], w), s.dtype) for s, w in zip(sums, self.widths)]
        self.n_sems = 3 * len(sums)

    def _copy(self, e, a, k, chip):
        cx, cy = e.chips[k]
        return e.copy(_piece(e.ins[a], self.kinds[a], chip, self.widths[a]), e.outs[a].at[k], a * 3 + k, (cx, cy, e.c))

    def start(self, e):
        for a in range(len(self.inputs)):
            for k, (cx, cy) in enumerate(e.chips):
                self._copy(e, a, k, 2 * cx + cy).start()

    def finish(self, e):
        for a in range(len(self.inputs)):
            for k, (cx, cy) in enumerate(e.chips):
                self._copy(e, a, k, 2 * cx + cy).wait()


class _FinishExchange(_Exchange):
    def __init__(self, finals, to_all):
        self.inputs, self.to_all = list(finals), list(to_all)
        self.out_shapes = [jax.ShapeDtypeStruct(f.shape, f.dtype) for f in finals]
        self.aliases = {a: a for a in range(len(finals))}
        self.first_sem, self.n_sems = [], 0
        for all8 in self.to_all:
            self.first_sem.append(self.n_sems)
            self.n_sems += 7 if all8 else 1
        self.rel = [(fx, fy, fc) for fx in (0, 1) for fy in (0, 1) for fc in (0, 1)][1:]

    def _copies(self, e, mine):
        for a, all8 in enumerate(self.to_all):
            src = e.ins[a] if mine else e.outs[a]
            if not all8:
                rh = self.inputs[a].shape[0] // 2
                rows = pl.ds((e.c if mine else 1 - e.c) * rh, rh)
                yield e.copy(src.at[rows, :], e.outs[a].at[rows, :], self.first_sem[a], e.sibling)
                continue
            rh = self.inputs[a].shape[0] // (2 * N_CHIPS)
            for r, (fx, fy, fc) in enumerate(self.rel):
                px, py, pc = (1 - e.x if fx else e.x), (1 - e.y if fy else e.y), (1 - e.c if fc else e.c)
                rows = pl.ds(((2 * e.me + e.c) if mine else (2 * (2 * px + py) + pc)) * rh, rh)
                yield e.copy(src.at[rows, :], e.outs[a].at[rows, :], self.first_sem[a] + r, (px, py, pc))

    def start(self, e):
        for cp in self._copies(e, True):
            cp.start()

    def finish(self, e):
        for cp in self._copies(e, False):
            cp.wait_recv()
        for cp in self._copies(e, True):
            cp.wait_send()


def _cast_into_full(w, kind, idx, name):
    r, c = w.shape
    tr = _row_tile(r, c)
    nrb = r // tr

    def body(idx_ref, w_ref, o_ref):
        o_ref[...] = w_ref[...].astype(BF16)

    if kind == "col":
        full, out_map = (r, N_CHIPS * c), (lambda i, idx_ref: (i, idx_ref[1]))
    else:
        full, out_map = (N_CHIPS * r, c), (lambda i, idx_ref: (idx_ref[1] * nrb + i, 0))
    return pl.pallas_call(
        body, name=name, out_shape=jax.ShapeDtypeStruct(full, BF16),
        grid_spec=pltpu.PrefetchScalarGridSpec(
            num_scalar_prefetch=1, grid=(nrb,), in_specs=[pl.BlockSpec((tr, c), lambda i, idx_ref: (i, 0))],
            out_specs=pl.BlockSpec((tr, c), out_map)),
        compiler_params=_params(),
    )(idx, w)


def _matmul_gathering(a, placed, order, name):
    t, k = a.shape
    n = placed.shape[1]
    w = n // N_CHIPS
    tm, tn = _pick(t, _M_TILES), _pick(w, _N_TILES)
    ni, nj = t // tm, w // tn
    per_shard, total = ni * nj, N_CHIPS * ni * nj
    gather = _AllGather([placed], ["col"])

    def body(ord_ref, a_ref, w_own_ref, o_ref, w_ref, wbuf, fetch_sem, send, recv):
        s, i, j = pl.program_id(0), pl.program_id(1), pl.program_id(2)
        step = (s * ni + i) * nj + j
        e = _Env((w_own_ref,), (w_ref,), send, recv)

        def fetch(src, st):
            col = pl.multiple_of((ord_ref[st // per_shard] * nj + st % nj) * tn, LANE)
            return pltpu.make_async_copy(src.at[:, pl.ds(col, tn)], wbuf.at[st % 2], fetch_sem.at[st % 2])

        @pl.when(step == 0)
        def _():
            gather.start(e)
            fetch(w_own_ref, step).start(priority=LOCAL_FETCH_DMA_PRIORITY)

        nxt = step + 1
        for kk, (cx, cy) in enumerate(e.chips):
            @pl.when(nxt == (kk + 1) * per_shard)
            def _():
                gather._ici(e, 0, kk, 2 * cx + cy).wait_recv()
                gather._d2d(e, 0, kk, e.c).start()
                gather._d2d(e, 0, kk, 1 - e.c).wait_recv()

        @pl.when(nxt < per_shard)
        def _():
            fetch(w_own_ref, nxt).start(priority=LOCAL_FETCH_DMA_PRIORITY)

        @pl.when((nxt >= per_shard) & (nxt < total))
        def _():
            fetch(w_ref, nxt).start(priority=LOCAL_FETCH_DMA_PRIORITY)

        fetch(w_ref, step).wait()
        o_ref[...] = jnp.dot(a_ref[...], wbuf[step % 2], preferred_element_type=F32)

        @pl.when(step == total - 1)
        def _():
            for kk in range(3):
                gather._ici(e, 0, kk, e.me).wait_send()
                gather._d2d(e, 0, kk, e.c).wait_send()

    z, full = pl.pallas_call(
        body, name=name, out_shape=(jax.ShapeDtypeStruct((t, n), F32), jax.ShapeDtypeStruct(placed.shape, placed.dtype)),
        grid_spec=pltpu.PrefetchScalarGridSpec(
            num_scalar_prefetch=1, grid=(N_CHIPS, ni, nj),
            in_specs=[pl.BlockSpec((tm, k), lambda s, i, j, ord_ref: (i, 0)), ANY],
            out_specs=(pl.BlockSpec((tm, tn), lambda s, i, j, ord_ref: (i, ord_ref[s] * nj + j)), ANY),
            scratch_shapes=[pltpu.VMEM((2, k, tn), placed.dtype), pltpu.SemaphoreType.DMA((2,)),
                            pltpu.SemaphoreType.DMA((gather.n_sems,)), pltpu.SemaphoreType.DMA((gather.n_sems,))]),
        input_output_aliases={2: 1}, compiler_params=_params(),
    )(order, a, placed)
    return z, full


def _add_own_half(g4, recv, idx, out_dtype, name):
    p, _, rh, n = g4.shape
    tr, tc = _tile2d(rh, n, 1024 * 1024)

    def body(idx_ref, g_ref, r_ref, o_ref):
        o_ref[...] = (g_ref[...] + r_ref[...]).astype(out_dtype)

    return pl.pallas_call(
        body, name=name, out_shape=jax.ShapeDtypeStruct((p, rh, n), out_dtype),
        grid_spec=pltpu.PrefetchScalarGridSpec(
            num_scalar_prefetch=1, grid=(p, rh // tr, n // tc),
            in_specs=[pl.BlockSpec((None, None, tr, tc), lambda q, i, j, idx_ref: (q, idx_ref[0], i, j)),
                      pl.BlockSpec((None, tr, tc), lambda q, i, j, idx_ref: (q, i, j))],
            out_specs=pl.BlockSpec((None, tr, tc), lambda q, i, j, idx_ref: (q, i, j))),
        compiler_params=_params(),
    )(idx, g4, recv)


def _sum_chips(own, kind, parts, idx, slots, to_all, name):
    _, rh, w = parts.shape
    tr, tc = _tile2d(rh, w, 512 * 1024)
    nrb, ncb = rh // tr, w // tc

    def body(idx_ref, own_ref, p0, p1, p2, o_ref):
        o_ref[...] = ((own_ref[...].astype(F32) + p0[...].astype(F32)) + p1[...].astype(F32)) + p2[...].astype(F32)

    if kind == "col":
        own_spec = pl.BlockSpec((None, tr, tc), lambda i, j, idx_ref: (0, i, idx_ref[1] * ncb + j))
    else:
        own_spec = pl.BlockSpec((None, tr, tc), lambda i, j, idx_ref: (idx_ref[1], i, j))
    if to_all:
        out_map = lambda i, j, idx_ref: ((2 * idx_ref[1] + idx_ref[0]) * nrb + i, j)
    else:
        out_map = lambda i, j, idx_ref: (idx_ref[0] * nrb + i, j)

    def part(k):
        return pl.BlockSpec((None, tr, tc), lambda i, j, idx_ref: (k, i, j))

    return pl.pallas_call(
        body, name=name, out_shape=jax.ShapeDtypeStruct((slots * rh, w), F32),
        grid_spec=pltpu.PrefetchScalarGridSpec(
            num_scalar_prefetch=1, grid=(nrb, ncb), in_specs=[own_spec, part(0), part(1), part(2)],
            out_specs=pl.BlockSpec((tr, tc), out_map)),
        compiler_params=_params(),
    )(idx, own, parts, parts, parts)


class _Reduce:
    def __init__(self, name, g, kind, idx, wire, to_all):
        r, c = g.shape
        self.name, self.kind, self.idx, self.wire, self.to_all = name, kind, idx, wire, to_all
        self.view = g.reshape(1, 2, r // 2, c) if kind == "col" else g.reshape(N_CHIPS, 2, r // (2 * N_CHIPS), c)

    def sibling(self):
        return _SiblingExchange([self.view])

    def got_sibling(self, outs):
        self.sum = _add_own_half(self.view, outs[0], self.idx, self.wire, "grad_chip_sum_" + self.name)

    def chips(self):
        return _ChipExchange([self.sum], [self.kind])

    def got_chips(self, outs):
        self.total = _sum_chips(self.sum, self.kind, outs[0], self.idx, 2 * N_CHIPS if self.to_all else 2,
                                self.to_all, "grad_total_" + self.name)


def _pack(arrays, rows):
    flat = jnp.concatenate([a.reshape(-1) for a in arrays])
    return jnp.pad(flat, (0, rows * SMALL_PACK_COLS - flat.shape[0])).reshape(rows, SMALL_PACK_COLS)


def _unpack(packed, shapes):
    flat = packed.reshape(-1)
    out, o = [], 0
    for shp in shapes:
        size = math.prod(shp)
        out.append(flat[o:o + size].reshape(shp))
        o += size
    return out


def _pack_rows(shapes):
    total = sum(math.prod(s) for s in shapes)
    unit = SMALL_PACK_COLS * N_CHIPS * 2 * SUBLANE
    return -(-total // unit) * (N_CHIPS * 2 * SUBLANE)


BIG = ("w_in", "w_attn_proj", "w_lru_proj", "w_out", "w_ffn_gate", "w_ffn_up", "w_ffn_down")
BIG_KIND = {"w_in": "col", "w_attn_proj": "row", "w_lru_proj": "row", "w_out": "row", "w_ffn_gate": "col",
            "w_ffn_up": "col", "w_ffn_down": "row"}
SMALL = ("norm1_g", "b_gates", "q_norm_g", "k_norm_g", "sinks", "conv_w", "conv_b", "w_rgate", "b_rgate",
         "w_igate", "b_igate", "lru_lambda", "norm2_g")
PACKED = tuple(n for n in SMALL if n not in ("w_rgate", "w_igate"))
WEIGHTS = ("norm1_g", "w_in", "b_gates", "q_norm_g", "k_norm_g", "sinks", "conv_w", "conv_b", "w_rgate", "b_rgate",
           "w_igate", "b_igate", "lru_lambda", "w_attn_proj", "w_lru_proj", "w_out", "norm2_g", "w_ffn_gate",
           "w_ffn_up", "w_ffn_down")


def kernel(x, positions, norm1_g, w_in, b_gates, q_norm_g, k_norm_g, sinks, conv_w, conv_b, w_rgate, b_rgate, w_igate, b_igate, lru_lambda, w_attn_proj, w_lru_proj, w_out, norm2_g, w_ffn_gate, w_ffn_up, w_ffn_down, loss_target, m_norm1_g, m_w_in, m_b_gates, m_q_norm_g, m_k_norm_g, m_sinks, m_conv_w, m_conv_b, m_w_rgate, m_b_rgate, m_w_igate, m_b_igate, m_lru_lambda, m_w_attn_proj, m_w_lru_proj, m_w_out, m_norm2_g, m_w_ffn_gate, m_w_ffn_up, m_w_ffn_down, v_norm1_g, v_w_in, v_b_gates, v_q_norm_g, v_k_norm_g, v_sinks, v_conv_w, v_conv_b, v_w_rgate, v_b_rgate, v_w_igate, v_b_igate, v_lru_lambda, v_w_attn_proj, v_w_lru_proj, v_w_out, v_norm2_g, v_w_ffn_gate, v_w_ffn_up, v_w_ffn_down):
    args = dict(locals())
    w = {n: args[n] for n in WEIGHTS}
    mom = {n: args["m_" + n] for n in WEIGHTS}
    var = {n: args["v_" + n] for n in WEIGHTS}

    t, d = x.shape[1], x.shape[2]
    hd = q_norm_g.shape[-1]
    nq = sinks.shape[-1]
    q_w = nq * hd
    d_rnn = conv_b.shape[-1]
    taps = conv_w.shape[1]
    n_blocks, bw = w_rgate.shape[1], w_rgate.shape[2]
    in_w = w_in.shape[-1] * N_CHIPS
    kv_w = (in_w - q_w - 2 * d_rnn - 2 * d) // 2
    kv = kv_w // hd
    grp = nq // kv
    u_off = q_w + 2 * kv_w
    gr_off = u_off + d_rnn
    ga_off = gr_off + d_rnn
    gw = bw * LANE // math.gcd(bw, LANE)
    chip = 2 * lax.axis_index("x") + lax.axis_index("y")
    idx = jnp.stack([lax.axis_index("c"), chip]).astype(jnp.int32)

    x2, tgt = x[0], loss_target[0]

    placed = {n: _cast_into_full(w[n][0], BIG_KIND[n], idx, "cast_" + n) for n in BIG}

    def gather(*names):
        return _AllGather([placed[n] for n in names], [BIG_KIND[n] for n in names])

    mx, my = lax.axis_index("x"), lax.axis_index("y")
    order = jnp.stack([chip, 2 * (1 - mx) + my, 2 * mx + (1 - my), 2 * (1 - mx) + (1 - my)]).astype(jnp.int32)
    conv_w_full = _gather_small(conv_w[0], "allgather_conv_w")
    conv_w_full = jnp.transpose(conv_w_full, (1, 0, 2)).reshape(taps, d_rnn)

    inv_freq = ROPE_THETA ** (-jnp.arange(0, hd // 4, 2, dtype=F32) / (hd // 4))
    ang = positions[0].astype(F32)[:, None] * inv_freq
    cos, sin = jnp.cos(ang), jnp.sin(ang)
    rest = hd - 2 * cos.shape[1]
    cos_t = jnp.concatenate([cos, cos, jnp.ones((t, rest), F32)], axis=1)
    sin_t = jnp.concatenate([-sin, sin, jnp.zeros((t, rest), F32)], axis=1)
    sinks1 = sinks[0]

    xn = _rms_fwd(x2, norm1_g, "rms1_fwd")
    z, win_f = _matmul_gathering(xn, placed["w_in"], order, "in_proj")
    attn, (wap_f, wlp_f, wout_f) = _attn_fwd(z, cos_t, sin_t, q_norm_g, k_norm_g, sinks1, kv, grp, hd, "attn_fwd",
                                             job=gather("w_attn_proj", "w_lru_proj", "w_out"))
    uc = _conv_fwd(z, u_off, d_rnn, conv_w_full, conv_b, "conv_fwd")
    rpre, ipre = _gates_fwd(uc, w_rgate[0], w_igate[0], gw, "gates_fwd")
    (hst, rec), (wg_f,) = _lru_fwd(uc, rpre, ipre, z, gr_off, b_rgate, b_igate, lru_lambda, "lru_fwd",
                                   job=gather("w_ffn_gate"))
    pa = _matmul(attn, wap_f, "nn", "attn_proj")
    plru, merged = _matmul(rec, wlp_f, "nn", "lru_proj", fused=(
        [pa, (z, ga_off), (z, ga_off + d), (b_gates, 0), (b_gates, d)], _merge_after_lru_proj, (F32, BF16)))
    h1 = _matmul(merged, wout_f, "nn", "out_proj", add=x2)
    hn = _rms_fwd(h1, norm2_g, "rms2_fwd")
    gate, (wu_f,) = _matmul(hn, wg_f, "nn", "ffn_gate", job=gather("w_ffn_up"))
    (up, act), (wd_f,) = _matmul(hn, wu_f, "nn", "ffn_up", job=gather("w_ffn_down"),
                                 fused=([gate], _swiglu_after_up, (F32, BF16)))
    yout = _matmul(act, wd_f, "nn", "ffn_down", add=h1)
    dy, dy16, loss_part = _loss_head(yout, tgt, "loss_head")
    loss = lax.psum(loss_part[0, 0], ("x", "y", "c"))

    def reduction(n, g):
        return _Reduce(n, g, BIG_KIND[n], idx, BF16, False)

    r_wd = reduction("w_ffn_down", _matmul(act, dy16, "tn", "d_w_ffn_down"))
    dgate, dup = _matmul(dy16, wd_f, "nt", "d_act", fused=([gate, up], _swiglu_bwd_after_dact, (BF16, BF16)))
    r_wg = reduction("w_ffn_gate", _matmul(hn, dgate, "tn", "d_w_ffn_gate"))
    both = _Jobs(r_wg.sibling(), r_wd.sibling())
    g_wu, got = _matmul(hn, dup, "tn", "d_w_ffn_up", job=both)
    got_wg, got_wd = both.split(got)
    r_wg.got_sibling(got_wg)
    r_wd.got_sibling(got_wd)
    r_wu = reduction("w_ffn_up", g_wu)
    dhn = _matmul(dgate, wg_f, "nt", "d_hn_gate")
    dhn = _matmul(dup, wu_f, "nt", "d_hn_up", add=dhn)
    dh1, g_norm2, dh1_16 = _rms_bwd(dhn, h1, norm2_g, dy, "rms2_bwd", mxu_copy=True)
    r_wout = reduction("w_out", _matmul(merged, dh1_16, "tn", "d_w_out"))
    both = _Jobs(r_wout.sibling(), r_wu.sibling())
    dmerged, got = _matmul(dh1_16, wout_f, "nt", "d_merged", job=both)
    got_wout, got_wu = both.split(got)
    r_wout.got_sibling(got_wout)
    r_wu.got_sibling(got_wu)
    (dpa, dz, g_ba), got = _merge_bwd(dmerged, z, b_gates, pa, ga_off, 0, None, "merge_bwd_attn", job=r_wout.chips())
    r_wout.got_chips(got)
    dpl, dz, g_bl = _merge_bwd(dmerged, z, b_gates, plru, ga_off + d, d, dz, "merge_bwd_lru")
    r_wap = reduction("w_attn_proj", _matmul(attn, dpa, "tn", "d_w_attn_proj"))
    dattn, got = _matmul(dpa, wap_f, "nt", "d_attn", job=r_wap.sibling())
    r_wap.got_sibling(got)
    g_wlp, got = _matmul(rec, dpl, "tn", "d_w_lru_proj", job=r_wap.chips())
    r_wap.got_chips(got)
    r_wlp = reduction("w_lru_proj", g_wlp)
    drec, got = _matmul(dpl, wlp_f, "nt", "d_rec", job=r_wlp.sibling())
    r_wlp.got_sibling(got)
    both = _Jobs(r_wlp.chips(), r_wg.chips())
    (dz, drp, dip, duc_direct, g_lam, g_br, g_bi), got = _lru_bwd(
        drec, hst, uc, rpre, ipre, z, gr_off, b_rgate, b_igate, lru_lambda, dz, "lru_bwd", job=both)
    got_wlp, got_wg = both.split(got)
    r_wlp.got_chips(got_wlp)
    r_wg.got_chips(got_wg)
    duc = _gates_bwd_x(duc_direct, drp, dip, w_rgate[0], w_igate[0], gw, "gates_bwd_x")
    g_wr = _gates_bwd_w(uc, drp, n_blocks, bw, gw, "gates_bwd_wr")
    g_wi = _gates_bwd_w(uc, dip, n_blocks, bw, gw, "gates_bwd_wi")
    dz, g_convw, g_convb = _conv_bwd(duc, z, u_off, conv_w_full, dz, "conv_bwd")
    (dq, dk, dv, g_qg, g_kg, g_sinks), got = _attn_bwd(dattn, z, cos_t, sin_t, q_norm_g, k_norm_g, sinks1, kv, grp, hd,
                                                        "attn_bwd", job=_Jobs(r_wu.chips(), r_wd.chips()))
    r_wu.got_chips(got[:1])
    r_wd.got_chips(got[1:])
    for part, col in ((dq, 0), (dk, q_w), (dv, q_w + kv_w)):
        dz = lax.dynamic_update_slice(dz, part, (0, col))
    r_wr = _Reduce("w_rgate", g_wr.reshape(n_blocks * bw, bw), "row", idx, F32, True)
    r_wi = _Reduce("w_igate", g_wi.reshape(n_blocks * bw, bw), "row", idx, F32, True)
    early = [r_wap, r_wlp, r_wout, r_wg, r_wu, r_wd]
    three = _Jobs(r_wr.sibling(), r_wi.sibling(), _FinishExchange([r.total for r in early], [False] * len(early)))
    g_top, got = _matmul(xn, dz, "tn", "d_w_in_top", m_window=(0, d // 2), job=three)
    got_wr, got_wi, finished = three.split(got)
    r_wr.got_sibling(got_wr)
    r_wi.got_sibling(got_wi)
    r_top = _Reduce("w_in_top", g_top, "col", idx, BF16, False)
    three = _Jobs(r_top.sibling(), r_wr.chips(), r_wi.chips())
    g_bot, got = _matmul(xn, dz, "tn", "d_w_in_bot", m_window=(d // 2, d // 2), job=three)
    got_top, got_wr, got_wi = three.split(got)
    r_top.got_sibling(got_top)
    r_wr.got_chips(got_wr)
    r_wi.got_chips(got_wi)
    r_bot = _Reduce("w_in_bot", g_bot, "col", idx, BF16, False)
    both = _Jobs(r_top.chips(), r_bot.sibling())
    dxn, got = _matmul(dz, win_f, "nt", "d_xn_a", m_window=(0, t // 2), into=(None, t), job=both)
    got_top, got_bot = both.split(got)
    r_top.got_chips(got_top)
    r_bot.got_sibling(got_bot)
    dxn, got = _matmul(dz, win_f, "nt", "d_xn_b", m_window=(t // 2, t // 2), into=(dxn, t), job=r_bot.chips())
    r_bot.got_chips(got)
    dx, g_norm1 = _rms_bwd(dxn, x2, norm1_g, dh1, "rms1_bwd")

    small_grads = {"norm1_g": g_norm1, "b_gates": jnp.concatenate([g_ba, g_bl], axis=1), "q_norm_g": g_qg,
                   "k_norm_g": g_kg, "sinks": g_sinks[:, :nq], "conv_w": g_convw, "conv_b": g_convb,
                   "b_rgate": g_br, "b_igate": g_bi, "lru_lambda": g_lam, "norm2_g": g_norm2}
    gshapes = [small_grads[n].shape for n in PACKED]
    small_sum, (top, bot, grads_wr, grads_wi) = _allreduce_small(
        _pack([small_grads[n] for n in PACKED], _pack_rows(gshapes)),
        _FinishExchange([r.total for r in (r_top, r_bot, r_wr, r_wi)], [False, False, True, True]), "grad_last_exchange")
    grads = dict(zip(BIG[1:], finished))
    grads["w_in"] = jnp.concatenate([top, bot], axis=0)
    grads["w_rgate"], grads["w_igate"] = grads_wr, grads_wi
    small_full = dict(zip(PACKED, _unpack(small_sum, gshapes)))
    per = d_rnn // N_CHIPS
    small_full["conv_w"] = lax.dynamic_slice(small_full["conv_w"], (0, chip * per), (taps, per))
    grads.update(small_full)

    delta, new_m, new_v = {}, {}, {}
    for n in BIG + ("w_rgate", "w_igate"):
        as2d = (lambda a: a[0]) if n in BIG else (lambda a: a.reshape(n_blocks * bw, bw))
        if n == "w_in":
            delta[n], new_m[n], new_v[n] = _adamw(as2d(w[n]), grads[n], as2d(mom[n]), as2d(var[n]), "adamw_" + n)
        else:
            delta[n], new_m[n], new_v[n], grads[n] = _adamw(as2d(w[n]), grads[n], as2d(mom[n]), as2d(var[n]),
                                                            "adamw_" + n, pass_grad=True)
    pshapes = [w[n].shape for n in PACKED]
    prows = _pack_rows(pshapes)
    pk = [_pack([src[n] for n in PACKED], prows) for src in (w, grads, mom, var)]
    for res, packed in zip((delta, new_m, new_v), _adamw(pk[0], pk[1], pk[2], pk[3], "adamw_small")):
        res.update(dict(zip(PACKED, _unpack(packed, pshapes))))

    outs = [loss, dx.reshape(x.shape)]
    for res in (grads, delta, new_m, new_v):
        outs += [res[n].reshape(w[n].shape) for n in WEIGHTS]
    return tuple(outs)


def _allreduce_small(x, job, name):
    n_dev = 2 * N_CHIPS
    rel = [(fx, fy, fc) for fx in (0, 1) for fy in (0, 1) for fc in (0, 1)][1:]
    j_in, j_out = len(job.inputs), len(job.out_shapes)

    def body(x_ref, *refs):
        jins, all_ref, o_ref, jouts = refs[:j_in], refs[j_in], refs[j_in + 1], refs[j_in + 2:j_in + 2 + j_out]
        send_sems, recv_sems, jsend, jrecv = refs[j_in + 2 + j_out:]
        e = _Env((x_ref,), (all_ref,), send_sems, recv_sems)
        carried = _Env(jins, jouts, jsend, jrecv, 0, (e.x, e.y, e.c))
        job.start(carried)
        mine = 2 * e.me + e.c

        def peer(r):
            fx, fy, fc = rel[r]
            return (1 - e.x if fx else e.x), (1 - e.y if fy else e.y), (1 - e.c if fc else e.c)

        all_ref[mine] = x_ref[...]
        for r in range(len(rel)):
            e.copy(x_ref, all_ref.at[mine], r, peer(r)).start()
        for r in range(len(rel)):
            px, py, pc = peer(r)
            e.copy(x_ref, all_ref.at[2 * (2 * px + py) + pc], r, peer(r)).wait_recv()
        total = all_ref[0]
        for dev in range(1, n_dev):
            total = total + all_ref[dev]
        o_ref[...] = total
        for r in range(len(rel)):
            e.copy(x_ref, all_ref.at[mine], r, peer(r)).wait_send()
        job.finish(carried)

    vm = pl.BlockSpec(memory_space=pltpu.VMEM)
    res = pl.pallas_call(
        body, name=name,
        out_shape=(jax.ShapeDtypeStruct((n_dev,) + x.shape, x.dtype), jax.ShapeDtypeStruct(x.shape, x.dtype))
        + tuple(job.out_shapes),
        in_specs=[vm] + [ANY] * j_in, out_specs=(vm, vm) + (ANY,) * j_out,
        input_output_aliases={1 + i: 2 + o for i, o in job.aliases.items()},
        scratch_shapes=[pltpu.SemaphoreType.DMA((len(rel),)), pltpu.SemaphoreType.DMA((len(rel),)),
                        pltpu.SemaphoreType.DMA((job.n_sems,)), pltpu.SemaphoreType.DMA((job.n_sems,))])(x, *job.inputs)
    return res[1], tuple(res[2:])


def _gather_small(shard, name):
    def body(s_ref, o_ref, send_sems, recv_sems):
        e = _Env((s_ref,), (o_ref,), send_sems, recv_sems)
        o_ref[e.me] = s_ref[...]
        for k, (cx, cy) in enumerate(e.chips):
            e.copy(s_ref, o_ref.at[e.me], k, (cx, cy, e.c)).start()
        for k, (cx, cy) in enumerate(e.chips):
            e.copy(s_ref, o_ref.at[2 * cx + cy], k, (cx, cy, e.c)).wait_recv()
        for k, (cx, cy) in enumerate(e.chips):
            e.copy(s_ref, o_ref.at[e.me], k, (cx, cy, e.c)).wait_send()

    vm = pl.BlockSpec(memory_space=pltpu.VMEM)
    return pl.pallas_call(body, name=name, out_shape=jax.ShapeDtypeStruct((N_CHIPS,) + shard.shape, shard.dtype),
                          in_specs=[vm], out_specs=vm,
                          scratch_shapes=[pltpu.SemaphoreType.DMA((3,)), pltpu.SemaphoreType.DMA((3,))])(shard)
```
